```python
import jax, jax.numpy as jnp
from jax import lax
import numpy as np

D_MODEL = 1024
BATCH = 2
SEQ = 8192
DEPTH = 1
DEC_BATCH = 32
DEC_SEQ = 16
PAST_LEN = 4096

CHUNK = 64
D_MIX = D_MODEL
C_CONV = 384
CONV_W = 31
C_GMLP = 384
GMLP_HEADS = 4
GMLP_HD = C_GMLP // GMLP_HEADS
GMLP_CHUNK = 128
XA_HEADS = 4
XA_HD = 64
C_XA = XA_HEADS * XA_HD
N_MEM = 256
N_EXPERTS = 32
TOP_K = 4
D_FF = D_MODEL
SWIGLU_LIMIT = 7.0
SWIGLU_ALPHA = 1.702
MOE_BLOCK = 128
EPS = 1e-5
IN_COLS = 2 * C_CONV + 2 * C_GMLP + C_XA
SPLITS = (C_CONV, 2 * C_CONV, 2 * C_CONV + C_GMLP, 2 * C_CONV + 2 * C_GMLP)

kernel_name = 'hymba_conformer_gmlp_moe_stream_step'


def rmsnorm(x, g):
    xf = x.astype(jnp.float32)
    y = xf * lax.rsqrt(jnp.mean(xf * xf, axis=-1, keepdims=True) + EPS)
    return (y * g.astype(jnp.float32)).astype(x.dtype)


def layernorm(x, g, b):
    xf = x.astype(jnp.float32)
    mu = jnp.mean(xf, axis=-1, keepdims=True)
    xc = xf - mu
    var = jnp.mean(xc * xc, axis=-1, keepdims=True)
    return (xc * lax.rsqrt(var + EPS) * g.astype(jnp.float32) + b.astype(jnp.float32)).astype(x.dtype)


def conv_group(z_a, z_g, conv_buf, w_dw, b_dw, ln_g, ln_b):
    glu = z_a * jax.nn.sigmoid(z_g)
    xp = jnp.concatenate([conv_buf, glu], axis=1)
    y = lax.conv_general_dilated(xp, w_dw[:, None, :], window_strides=(1,), padding='VALID',
                                 dimension_numbers=('NWC', 'WIO', 'NWC'),
                                 feature_group_count=C_CONV) + b_dw
    new_buf = xp[:, -(CONV_W - 1):]
    y = layernorm(y, ln_g, ln_b)
    return y * jax.nn.sigmoid(y), new_buf


def gmlp_group(z_u, z_v, ln_g, ln_b, w_sp, b_sp):
    B, L, _ = z_u.shape
    u = jax.nn.gelu(z_u, approximate=False)
    v = layernorm(jax.nn.gelu(z_v, approximate=False), ln_g, ln_b)
    c = min(L, GMLP_CHUNK)
    w = w_sp[:, :c, :c] * jnp.tril(jnp.ones((c, c), w_sp.dtype))
    vh = v.reshape(B, L // c, c, GMLP_HEADS, GMLP_HD)
    s = jnp.einsum('hts,bnshd->bnthd', w, vh) + jnp.swapaxes(b_sp[:, :c], 0, 1)[:, :, None]
    return u * s.reshape(B, L, C_GMLP), v


def memory_kv(mem, g_mem, w_mk, w_mv):
    B, M, _ = mem.shape
    mn = rmsnorm(mem, g_mem)
    k = (mn @ w_mk).reshape(B, M, XA_HEADS, XA_HD)
    v = (mn @ w_mv).reshape(B, M, XA_HEADS, XA_HD)
    return k, v


def cross_attn(q, mem_k, mem_v):
    B, L = q.shape[0], q.shape[1]
    s = jnp.einsum('blhd,bmhd->bhlm', q.astype(jnp.float32), mem_k.astype(jnp.float32)) * (XA_HD ** -0.5)
    p = jax.nn.softmax(s, axis=-1)
    o = jnp.einsum('bhlm,bmhd->blhd', p, mem_v.astype(jnp.float32))
    return o.reshape(B, L, C_XA).astype(q.dtype)


def moe_ffn(h, w_router, b_router, w_gu, b_gu, w_dn, b_dn):
    B, L, D = h.shape
    T = B * L
    A = T * TOP_K
    n_blocks = -(-A // MOE_BLOCK) + N_EXPERTS
    xt = h.reshape(T, D)
    logits = xt.astype(jnp.float32) @ w_router.astype(jnp.float32) + b_router.astype(jnp.float32)
    top_val, top_idx = lax.top_k(logits, TOP_K)
    gates = jax.nn.softmax(top_val, axis=-1)
    flat_e = top_idx.reshape(A).astype(jnp.int32)
    order = jnp.argsort(flat_e).astype(jnp.int32)
    sorted_e = flat_e[order]
    sorted_tok = order // TOP_K
    sorted_gate = gates.reshape(A)[order]
    counts = jnp.bincount(flat_e, length=N_EXPERTS)
    padded = (counts + MOE_BLOCK - 1) // MOE_BLOCK * MOE_BLOCK
    start = jnp.cumsum(counts) - counts
    pad_end = jnp.cumsum(padded)
    pad_start = pad_end - padded
    dest = pad_start[sorted_e] + jnp.arange(A, dtype=jnp.int32) - start[sorted_e]
    slot_tok = jnp.full((n_blocks * MOE_BLOCK,), T, jnp.int32).at[dest].set(sorted_tok)
    block_e = jnp.minimum(jnp.searchsorted(pad_end, jnp.arange(n_blocks) * MOE_BLOCK, side='right'),
                          N_EXPERTS - 1)
    x_pad = jnp.concatenate([xt, jnp.zeros((1, D), xt.dtype)], axis=0)
    xb = x_pad[slot_tok].reshape(n_blocks, MOE_BLOCK, D)

    def expert_block(args):
        xblk, e = args
        gu = xblk @ w_gu[e] + b_gu[e]
        gate = jnp.minimum(gu[:, :D_FF], SWIGLU_LIMIT)
        up = jnp.clip(gu[:, D_FF:], -SWIGLU_LIMIT, SWIGLU_LIMIT)
        act = (up + 1.0) * (gate * jax.nn.sigmoid(SWIGLU_ALPHA * gate))
        return act @ w_dn[e] + b_dn[e]

    yb = lax.map(expert_block, (xb, block_e)).reshape(n_blocks * MOE_BLOCK, D)
    y_assign = yb[dest] * sorted_gate[:, None].astype(yb.dtype)
    out = jax.ops.segment_sum(y_assign, sorted_tok, num_segments=T)
    return out.reshape(B, L, D)


def trunk_layer(x, conv_buf, mem_k, mem_v, g_mix, w_in, w_dw, b_dw, ln_c_g, ln_c_b, ln_v_g, ln_v_b,
                w_sp, b_sp, w_out, g_ffn, w_router, b_router, w_gu, b_gu, w_dn, b_dn):
    B, L, _ = x.shape
    h = rmsnorm(x, g_mix)
    z = h @ w_in
    z_a, z_g, z_u, z_v, z_q = jnp.split(z, SPLITS, axis=-1)
    c_out, new_buf = conv_group(z_a, z_g, conv_buf, w_dw, b_dw, ln_c_g, ln_c_b)
    g_out, v_rows = gmlp_group(z_u, z_v, ln_v_g, ln_v_b, w_sp, b_sp)
    a_out = cross_attn(z_q.reshape(B, L, XA_HEADS, XA_HD), mem_k, mem_v)
    x = x + jnp.concatenate([c_out, g_out, a_out], axis=-1) @ w_out
    x = x + moe_ffn(rmsnorm(x, g_ffn), w_router, b_router, w_gu, b_gu, w_dn, b_dn)
    return x, new_buf, v_rows


def setup_inputs(seed: int = 0) -> dict:
    key = jax.random.key(seed)
    ks = jax.random.split(key, 32)

    def nrm(k, shape, scale=1.0):
        return jax.random.normal(k, shape, jnp.float32) * scale

    return {
        'x_prompt': nrm(ks[0], (BATCH, SEQ, D_MODEL)),
        'x_sample': nrm(ks[1], (DEC_BATCH, DEC_SEQ, D_MODEL)),
        'cache_conv': nrm(ks[2], (DEPTH, DEC_BATCH, CONV_W - 1, C_CONV), 0.5),
        'cache_mem_k': nrm(ks[3], (DEPTH, DEC_BATCH, N_MEM, XA_HEADS, XA_HD)),
        'cache_mem_v': nrm(ks[4], (DEPTH, DEC_BATCH, N_MEM, XA_HEADS, XA_HD)),
        'mem_prompt': nrm(ks[5], (BATCH, N_MEM, D_MODEL)),
        'g_mix': 1.0 + nrm(ks[6], (DEPTH, D_MODEL), 0.05),
        'w_in': nrm(ks[7], (DEPTH, D_MODEL, IN_COLS), D_MODEL ** -0.5),
        'w_dw': nrm(ks[8], (DEPTH, CONV_W, C_CONV), CONV_W ** -0.5),
        'b_dw': nrm(ks[9], (DEPTH, C_CONV), 0.02),
        'ln_conv_g': 1.0 + nrm(ks[10], (DEPTH, C_CONV), 0.05),
        'ln_conv_b': nrm(ks[11], (DEPTH, C_CONV), 0.02),
        'ln_v_g': 1.0 + nrm(ks[12], (DEPTH, C_GMLP), 0.05),
        'ln_v_b': nrm(ks[13], (DEPTH, C_GMLP), 0.02),
        'w_spatial': nrm(ks[14], (DEPTH, GMLP_HEADS, GMLP_CHUNK, GMLP_CHUNK), GMLP_CHUNK ** -0.5),
        'b_spatial': 1.0 + nrm(ks[15], (DEPTH, GMLP_HEADS, GMLP_CHUNK), 0.1),
        'g_mem': 1.0 + nrm(ks[16], (DEPTH, D_MODEL), 0.05),
        'w_mem_k': nrm(ks[17], (DEPTH, D_MODEL, C_XA), D_MODEL ** -0.5),
        'w_mem_v': nrm(ks[18], (DEPTH, D_MODEL, C_XA), D_MODEL ** -0.5),
        'w_out': nrm(ks[19], (DEPTH, D_MIX, D_MODEL), D_MIX ** -0.5),
        'g_ffn': 1.0 + nrm(ks[20], (DEPTH, D_MODEL), 0.05),
        'w_router': nrm(ks[21], (DEPTH, D_MODEL, N_EXPERTS), D_MODEL ** -0.5),
        'b_router': nrm(ks[22], (DEPTH, N_EXPERTS), 0.01),
        'w_gate_up': nrm(ks[23], (DEPTH, N_EXPERTS, D_MODEL, 2 * D_FF), D_MODEL ** -0.5),
        'b_gate_up': nrm(ks[24], (DEPTH, N_EXPERTS, 2 * D_FF), 0.01),
        'w_down': nrm(ks[25], (DEPTH, N_EXPERTS, D_FF, D_MODEL), D_FF ** -0.5),
        'b_down': nrm(ks[26], (DEPTH, N_EXPERTS, D_MODEL), 0.01),
        'g_final': 1.0 + nrm(ks[27], (D_MODEL,), 0.05),
    }


def reference(x_prompt, x_sample, cache_conv, cache_mem_k, cache_mem_v, mem_prompt, g_mix, w_in, w_dw,
              b_dw, ln_conv_g, ln_conv_b, ln_v_g, ln_v_b, w_spatial, b_spatial, g_mem, w_mem_k, w_mem_v,
              w_out, g_ffn, w_router, b_router, w_gate_up, b_gate_up, w_down, b_down, g_final):
    assert x_sample.shape[1] <= CHUNK
    hp, hs = x_prompt, x_sample
    conv_p, mk_p, mv_p, conv_s, vrow_s = [], [], [], [], []
    for l in range(DEPTH):
        lw = (g_mix[l], w_in[l], w_dw[l], b_dw[l], ln_conv_g[l], ln_conv_b[l], ln_v_g[l], ln_v_b[l],
              w_spatial[l], b_spatial[l], w_out[l], g_ffn[l], w_router[l], b_router[l],
              w_gate_up[l], b_gate_up[l], w_down[l], b_down[l])
        mk, mv = memory_kv(mem_prompt, g_mem[l], w_mem_k[l], w_mem_v[l])
        zero_buf = jnp.zeros((hp.shape[0], CONV_W - 1, C_CONV), hp.dtype)
        hp, cb_p, _ = trunk_layer(hp, zero_buf, mk, mv, *lw)
        hs, cb_s, v_s = trunk_layer(hs, cache_conv[l], cache_mem_k[l], cache_mem_v[l], *lw)
        conv_p.append(cb_p)
        mk_p.append(mk)
        mv_p.append(mv)
        conv_s.append(cb_s)
        vrow_s.append(v_s)
    y_prompt = rmsnorm(hp, g_final)
    y_sample = rmsnorm(hs, g_final)
    return (y_prompt, y_sample, jnp.stack(conv_p), jnp.stack(mk_p), jnp.stack(mv_p),
            jnp.stack(conv_s), jnp.stack(vrow_s))
```

```python
import functools

import numpy as np
import jax
import jax.numpy as jnp
from jax import lax
from jax.experimental import pallas as pl
from jax.experimental.pallas import tpu as pltpu

F32 = jnp.float32
BF16 = jnp.bfloat16
I32 = jnp.int32

D_MODEL = 1024
C_CONV = 384
CONV_W = 31
C_GMLP = 384
GMLP_HEADS = 4
GMLP_HD = 96
GMLP_CHUNK = 128
XA_HEADS = 4
XA_HD = 64
C_XA = 256
N_MEM = 256
N_EXPERTS = 32
TOP_K = 4
D_FF = 1024
SWIGLU_LIMIT = 7.0
SWIGLU_ALPHA = 1.702
EPS = 1e-5
IN_COLS = 2 * C_CONV + 2 * C_GMLP + C_XA

SUBLANES = 8
LANES = 128
ROW_TILES = D_MODEL // LANES
HIST = 32
HIST_OFF = HIST - (CONV_W - 1)

PROMPT_TILE = 512
SAMPLE_SEQS_PER_TILE = 8
MOE_BLOCK = 256
DISPATCH_TILE = 512
COMBINE_TILE = 128
VMEM_LIMIT_MIXER = 48 * 1024 * 1024
VMEM_LIMIT_MOE = 52 * 1024 * 1024


def _rmsnorm(x, g):
    return x * lax.rsqrt(jnp.mean(x * x, axis=-1, keepdims=True) + EPS) * g


def _layernorm(x, g, b):
    mu = jnp.mean(x, axis=-1, keepdims=True)
    xc = x - mu
    var = jnp.mean(xc * xc, axis=-1, keepdims=True)
    return xc * lax.rsqrt(var + EPS) * g + b


def _gelu(x):
    return 0.5 * x * (1.0 + lax.erf(x * np.float32(1.0 / np.sqrt(2.0))))


def _sigmoid(x):
    return 1.0 / (1.0 + jnp.exp(-x))


def _rows_to_tiles(dst_ref, val, rows):
    for j in range(ROW_TILES):
        dst_ref[pl.ds(j, rows, stride=ROW_TILES), :] = val[:, j * LANES:(j + 1) * LANES]


def _tiles_to_rows(src_ref, rows):
    return jnp.concatenate(
        [src_ref[pl.ds(j, rows, stride=ROW_TILES), :] for j in range(ROW_TILES)], axis=-1)


def _memkv_kernel(mem_ref, g_ref, w_ref, o_ref):
    mn = _rmsnorm(mem_ref[0], g_ref[...])
    o_ref[0] = jnp.dot(mn.astype(BF16), w_ref[...], preferred_element_type=F32)


def _memkv(mem, g_mem, w_kv):
    b = mem.shape[0]
    return pl.pallas_call(
        _memkv_kernel,
        grid=(b,),
        in_specs=[pl.BlockSpec((1, N_MEM, D_MODEL), lambda i: (i, 0, 0)),
                  pl.BlockSpec((1, D_MODEL), lambda i: (0, 0)),
                  pl.BlockSpec((D_MODEL, 2 * C_XA), lambda i: (0, 0))],
        out_specs=pl.BlockSpec((1, N_MEM, 2 * C_XA), lambda i: (i, 0, 0)),
        out_shape=jax.ShapeDtypeStruct((b, N_MEM, 2 * C_XA), F32),
        name="memkv",
    )(mem, g_mem, w_kv)


def _mixer_kernel(x_ref, hist_ref, mk_ref, mv_ref, gmix_ref, win_ref, wdw_ref, bdw_ref,
                  lcg_ref, lcb_ref, lvg_ref, lvb_ref, wsp_ref, bsp_ref, wout_ref, gffn_ref,
                  wrh_ref, wrl_ref, br_ref,
                  x1_ref, h2_ref, eidx_ref, gate_ref, rank_ref, cnt_ref, histout_ref, v_ref,
                  ext_ref, *, ns, sl, carry, sp_chunk):
    tm = ns * sl
    i = pl.program_id(1)

    x = x_ref[0]
    h = _rmsnorm(x, gmix_ref[...])
    z = jnp.dot(h.astype(BF16), win_ref[...], preferred_element_type=F32)
    z_a = z[:, 0:C_CONV]
    z_g = z[:, C_CONV:2 * C_CONV]
    z_u = z[:, 2 * C_CONV:2 * C_CONV + C_GMLP]
    z_v = z[:, 2 * C_CONV + C_GMLP:2 * C_CONV + 2 * C_GMLP]
    z_q = z[:, 2 * C_CONV + 2 * C_GMLP:IN_COLS]

    glu = z_a * _sigmoid(z_g)
    if carry:
        @pl.when(i == 0)
        def _():
            ext_ref[:, 0:HIST, :] = hist_ref[0]
    else:
        ext_ref[:, 0:HIST, :] = hist_ref[0]
    conv_parts = []
    for s in range(ns):
        ext_s = ext_ref.at[s]
        ext_s[HIST:HIST + sl, :] = glu[s * sl:(s + 1) * sl]
        rc = min(sl, 64)
        for r0 in range(0, sl, rc):
            acc = jnp.broadcast_to(bdw_ref[...], (rc, C_CONV))
            for j in range(CONV_W):
                acc = acc + wdw_ref[j:j + 1, :] * ext_s[pl.ds(r0 + j + HIST_OFF, rc), :]
            conv_parts.append(acc)
        new_hist = ext_s[sl:sl + HIST, :]
        histout_ref[0, s] = new_hist
        if carry:
            ext_s[0:HIST, :] = new_hist
    y = jnp.concatenate(conv_parts, axis=0) if len(conv_parts) > 1 else conv_parts[0]
    y = _layernorm(y, lcg_ref[...], lcb_ref[...])
    c_out = y * _sigmoid(y)

    u = _gelu(z_u)
    v = _layernorm(_gelu(z_v), lvg_ref[...], lvb_ref[...])
    v_ref[...] = v
    vb = v.astype(BF16)
    rr = lax.broadcasted_iota(I32, (GMLP_CHUNK, GMLP_CHUNK), 0)
    cc = lax.broadcasted_iota(I32, (GMLP_CHUNK, GMLP_CHUNK), 1)
    sp_mask = (cc <= rr) & ((rr // sp_chunk) == (cc // sp_chunk))
    col = lax.broadcasted_iota(I32, (GMLP_CHUNK, C_GMLP), 1)
    w_heads = [jnp.where(sp_mask, wsp_ref[hh], 0.0).astype(BF16) for hh in range(GMLP_HEADS)]
    g_parts = []
    for c in range(tm // GMLP_CHUNK):
        vc = vb[c * GMLP_CHUNK:(c + 1) * GMLP_CHUNK]
        sg = bsp_ref[...]
        for hh in range(GMLP_HEADS):
            head_cols = (col >= hh * GMLP_HD) & (col < (hh + 1) * GMLP_HD)
            vh = jnp.where(head_cols, vc, jnp.zeros_like(vc))
            sg = sg + jnp.dot(w_heads[hh], vh, preferred_element_type=F32)
        g_parts.append(u[c * GMLP_CHUNK:(c + 1) * GMLP_CHUNK] * sg)
    g_out = jnp.concatenate(g_parts, axis=0) if len(g_parts) > 1 else g_parts[0]

    qs = z_q * np.float32(XA_HD ** -0.5)
    qcol = lax.broadcasted_iota(I32, (sl, C_XA), 1)
    a_parts = []
    for s in range(ns):
        q_s = qs[s * sl:(s + 1) * sl]
        kb = mk_ref[0, s].astype(BF16)
        vvb = mv_ref[0, s].astype(BF16)
        a_s = jnp.zeros((sl, C_XA), F32)
        for hh in range(XA_HEADS):
            hmask = (qcol >= hh * XA_HD) & (qcol < (hh + 1) * XA_HD)
            qh = jnp.where(hmask, q_s, 0.0).astype(BF16)
            sc = lax.dot_general(qh, kb, (((1,), (1,)), ((), ())), preferred_element_type=F32)
            p = jnp.exp(sc - jnp.max(sc, axis=-1, keepdims=True))
            den = jnp.sum(p, axis=-1, keepdims=True)
            oh = jnp.dot(p.astype(BF16), vvb, preferred_element_type=F32) / den
            a_s = a_s + jnp.where(hmask, oh, 0.0)
        a_parts.append(a_s)
    a_out = jnp.concatenate(a_parts, axis=0) if len(a_parts) > 1 else a_parts[0]

    mix = jnp.concatenate([c_out, g_out, a_out], axis=-1).astype(BF16)
    x1 = x + jnp.dot(mix, wout_ref[...], preferred_element_type=F32)
    x1_ref[...] = x1

    h2 = _rmsnorm(x1, gffn_ref[...])
    _rows_to_tiles(h2_ref, h2, tm)
    h2_hi = h2.astype(BF16)
    h2_lo = (h2 - h2_hi.astype(F32)).astype(BF16)
    nt_dims = (((1,), (1,)), ((), ()))
    lg = (lax.dot_general(wrh_ref[...], h2_hi, nt_dims, preferred_element_type=F32)
          + lax.dot_general(wrl_ref[...], h2_hi, nt_dims, preferred_element_type=F32)
          + lax.dot_general(wrh_ref[...], h2_lo, nt_dims, preferred_element_type=F32)
          + br_ref[...])
    eio = lax.broadcasted_iota(I32, (N_EXPERTS, tm), 0)
    work = lg
    vals, idxs = [], []
    for _ in range(TOP_K):
        m = jnp.max(work, axis=0, keepdims=True)
        idx = jnp.min(jnp.where(work == m, eio, N_EXPERTS), axis=0, keepdims=True)
        vals.append(m)
        idxs.append(idx)
        work = jnp.where(eio == idx, -jnp.inf, work)
    exps = [jnp.exp(vk - vals[0]) for vk in vals]
    den = exps[0] + exps[1] + exps[2] + exps[3]
    eidx_ref[...] = jnp.concatenate(idxs, axis=0)
    gate_ref[...] = jnp.concatenate([ek / den for ek in exps], axis=0)

    sel = jnp.zeros((N_EXPERTS, tm), F32)
    for idx in idxs:
        sel = sel + jnp.where(eio == idx, 1.0, 0.0)
    tr = lax.broadcasted_iota(I32, (tm, tm), 0)
    tc = lax.broadcasted_iota(I32, (tm, tm), 1)
    before = jnp.where(tr < tc, 1.0, 0.0).astype(BF16)
    ranks = jnp.dot(sel.astype(BF16), before, preferred_element_type=F32)
    rank_ref[...] = jnp.concatenate(
        [jnp.sum(jnp.where(eio == idx, ranks, 0.0), axis=0, keepdims=True) for idx in idxs],
        axis=0).astype(I32)
    cnt_ref[0] = jnp.broadcast_to(jnp.sum(sel, axis=1, keepdims=True), (N_EXPERTS, LANES)).astype(I32)


def _mixer(x, hist, mem_k, mem_v, wts, *, ns, sl, carry, sp_chunk):
    g, r, _ = x.shape
    tm = ns * sl
    nt = r // tm
    ntot = g * nt
    const2 = lambda b, i: (0, 0)
    const3 = lambda b, i: (0, 0, 0)
    tile_row = lambda b, i: (b * nt + i, 0)
    tile_lane = lambda b, i: (0, b * nt + i)
    in_specs = [
        pl.BlockSpec((1, tm, D_MODEL), lambda b, i: (b, i, 0)),
        pl.BlockSpec((1, ns, HIST, C_CONV), lambda b, i: (b, 0, 0, 0)),
        pl.BlockSpec((1, ns, N_MEM, C_XA), lambda b, i: (b, 0, 0, 0)),
        pl.BlockSpec((1, ns, N_MEM, C_XA), lambda b, i: (b, 0, 0, 0)),
        pl.BlockSpec((1, D_MODEL), const2),
        pl.BlockSpec((D_MODEL, IN_COLS), const2),
        pl.BlockSpec((HIST, C_CONV), const2),
        pl.BlockSpec((1, C_CONV), const2),
        pl.BlockSpec((1, C_CONV), const2),
        pl.BlockSpec((1, C_CONV), const2),
        pl.BlockSpec((1, C_GMLP), const2),
        pl.BlockSpec((1, C_GMLP), const2),
        pl.BlockSpec((GMLP_HEADS, GMLP_CHUNK, GMLP_CHUNK), const3),
        pl.BlockSpec((GMLP_CHUNK, C_GMLP), const2),
        pl.BlockSpec((D_MODEL, D_MODEL), const2),
        pl.BlockSpec((1, D_MODEL), const2),
        pl.BlockSpec((N_EXPERTS, D_MODEL), const2),
        pl.BlockSpec((N_EXPERTS, D_MODEL), const2),
        pl.BlockSpec((N_EXPERTS, 1), const2),
    ]
    out_specs = [
        pl.BlockSpec((tm, D_MODEL), tile_row),
        pl.BlockSpec((tm * ROW_TILES, LANES), tile_row),
        pl.BlockSpec((TOP_K, tm), tile_lane),
        pl.BlockSpec((TOP_K, tm), tile_lane),
        pl.BlockSpec((TOP_K, tm), tile_lane),
        pl.BlockSpec((1, N_EXPERTS, LANES), lambda b, i: (b * nt + i, 0, 0)),
        pl.BlockSpec((1, ns, HIST, C_CONV), lambda b, i: (b, 0, 0, 0)),
        pl.BlockSpec((tm, C_GMLP), tile_row),
    ]
    rows = g * r
    out_shape = [
        jax.ShapeDtypeStruct((rows, D_MODEL), F32),
        jax.ShapeDtypeStruct((rows * ROW_TILES, LANES), F32),
        jax.ShapeDtypeStruct((TOP_K, rows), I32),
        jax.ShapeDtypeStruct((TOP_K, rows), F32),
        jax.ShapeDtypeStruct((TOP_K, rows), I32),
        jax.ShapeDtypeStruct((ntot, N_EXPERTS, LANES), I32),
        jax.ShapeDtypeStruct((g, ns, HIST, C_CONV), F32),
        jax.ShapeDtypeStruct((rows, C_GMLP), F32),
    ]
    kern = functools.partial(_mixer_kernel, ns=ns, sl=sl, carry=carry, sp_chunk=sp_chunk)
    return pl.pallas_call(
        kern,
        grid=(g, nt),
        in_specs=in_specs,
        out_specs=out_specs,
        out_shape=out_shape,
        scratch_shapes=[pltpu.VMEM((ns, HIST + sl, C_CONV), F32)],
        compiler_params=pltpu.CompilerParams(
            dimension_semantics=("arbitrary", "arbitrary"),
            vmem_limit_bytes=VMEM_LIMIT_MIXER),
        name="mixer_carry" if carry else "mixer_cache",
    )(x, hist, mem_k, mem_v, *wts)


def _dispatch_kernel(dest_ref, h2_hbm, xs_in_hbm, xs_hbm, sem, *, tile):
    del xs_in_hbm
    t0 = pl.program_id(0) * tile

    def row_copy(t, k):
        return pltpu.make_async_copy(h2_hbm.at[t0 + t], xs_hbm.at[dest_ref[0, 0, TOP_K * t + k]], sem)

    def start(t, carry):
        for k in range(TOP_K):
            row_copy(t, k).start()
        return carry

    def wait(t, carry):
        for k in range(TOP_K):
            row_copy(t, k).wait()
        return carry

    lax.fori_loop(0, tile, start, 0)
    lax.fori_loop(0, tile, wait, 0)


def _dispatch(dest, h2_rows, xs):
    t = h2_rows.shape[0]
    tile = min(DISPATCH_TILE, t)
    nt = t // tile
    return pl.pallas_call(
        functools.partial(_dispatch_kernel, tile=tile),
        grid=(nt,),
        in_specs=[pl.BlockSpec((1, 1, TOP_K * tile), lambda i: (i, 0, 0), memory_space=pltpu.SMEM),
                  pl.BlockSpec(memory_space=pl.ANY),
                  pl.BlockSpec(memory_space=pl.ANY)],
        out_specs=pl.BlockSpec(memory_space=pl.ANY),
        out_shape=jax.ShapeDtypeStruct(xs.shape, xs.dtype),
        scratch_shapes=[pltpu.SemaphoreType.DMA(())],
        input_output_aliases={2: 0},
        compiler_params=pltpu.CompilerParams(dimension_semantics=("arbitrary",)),
        name="dispatch",
    )(dest.reshape(nt, 1, TOP_K * tile), h2_rows, xs)


def _moe_kernel(be_ref, nb_ref, xs_ref, wgu_ref, bgu_ref, wdn_ref, bdn_ref, ys_ref, wgu16, wdn16):
    i = pl.program_id(0)

    @pl.when(i < nb_ref[0])
    def _():
        prev = be_ref[jnp.maximum(i - 1, 0)]
        new_expert = (i == 0) | (be_ref[i] != prev)

        @pl.when(new_expert)
        def _():
            def cast_rows(c, carry):
                r = pl.multiple_of(c * LANES, LANES)
                wgu16[pl.ds(r, LANES), :] = wgu_ref[pl.ds(r, LANES), :].astype(BF16)
                wdn16[pl.ds(r, LANES), :] = wdn_ref[pl.ds(r, LANES), :].astype(BF16)
                return carry
            lax.fori_loop(0, D_MODEL // LANES, cast_rows, 0)

        xb = _tiles_to_rows(xs_ref, MOE_BLOCK).astype(BF16)
        gu = jnp.dot(xb, wgu16[...], preferred_element_type=F32) + bgu_ref[0]
        gate = jnp.minimum(gu[:, :D_FF], SWIGLU_LIMIT)
        up = jnp.clip(gu[:, D_FF:], -SWIGLU_LIMIT, SWIGLU_LIMIT)
        act = (up + 1.0) * (gate * _sigmoid(SWIGLU_ALPHA * gate))
        yb = jnp.dot(act.astype(BF16), wdn16[...], preferred_element_type=F32) + bdn_ref[0]
        _rows_to_tiles(ys_ref, yb, MOE_BLOCK)

    @pl.when(i >= nb_ref[0])
    def _():
        ys_ref[...] = jnp.zeros_like(ys_ref)


def _moe(block_expert, n_used, xs, w_gu, b_gu, w_dn, b_dn):
    n_blocks = block_expert.shape[0]
    blk = lambda i, be, nb: (jnp.minimum(i, nb[0] - 1), 0)
    wsel = lambda i, be, nb: (be[i], 0, 0)
    grid_spec = pltpu.PrefetchScalarGridSpec(
        num_scalar_prefetch=2,
        grid=(n_blocks,),
        in_specs=[pl.BlockSpec((MOE_BLOCK * ROW_TILES, LANES), blk),
                  pl.BlockSpec((None, D_MODEL, 2 * D_FF), wsel),
                  pl.BlockSpec((None, 1, 2 * D_FF), wsel),
                  pl.BlockSpec((None, D_FF, D_MODEL), wsel),
                  pl.BlockSpec((None, 1, D_MODEL), wsel)],
        out_specs=pl.BlockSpec((MOE_BLOCK * ROW_TILES, LANES), lambda i, be, nb: (i, 0)),
        scratch_shapes=[pltpu.VMEM((D_MODEL, 2 * D_FF), BF16),
                        pltpu.VMEM((D_FF, D_MODEL), BF16)],
    )
    return pl.pallas_call(
        _moe_kernel,
        grid_spec=grid_spec,
        out_shape=jax.ShapeDtypeStruct(xs.shape, F32),
        compiler_params=pltpu.CompilerParams(
            dimension_semantics=("arbitrary",),
            vmem_limit_bytes=VMEM_LIMIT_MOE),
        name="moe",
    )(block_expert, n_used, xs, w_gu, b_gu, w_dn, b_dn)


def _combine_kernel(dest_ref, x1_ref, gate_ref, gfin_ref, ys_hbm, out_ref, buf, sem, *, tile):
    def row_copy(t, k):
        return pltpu.make_async_copy(
            ys_hbm.at[dest_ref[0, 0, TOP_K * t + k]],
            buf.at[k, pl.ds(pl.multiple_of(t * ROW_TILES, ROW_TILES), ROW_TILES)], sem)

    def start(t, carry):
        for k in range(TOP_K):
            row_copy(t, k).start()
        return carry

    def wait(t, carry):
        for k in range(TOP_K):
            row_copy(t, k).wait()
        return carry

    lax.fori_loop(0, tile, start, 0)
    lax.fori_loop(0, tile, wait, 0)

    acc = x1_ref[...]
    for k in range(TOP_K):
        acc = acc + gate_ref[:, k:k + 1] * _tiles_to_rows(buf.at[k], tile)
    out_ref[...] = _rmsnorm(acc, gfin_ref[...])


def _combine(dest, x1, gates_t, g_final, ys_rows):
    t = x1.shape[0]
    tile = COMBINE_TILE
    nt = t // tile
    return pl.pallas_call(
        functools.partial(_combine_kernel, tile=tile),
        grid=(nt,),
        in_specs=[pl.BlockSpec((1, 1, TOP_K * tile), lambda i: (i, 0, 0), memory_space=pltpu.SMEM),
                  pl.BlockSpec((tile, D_MODEL), lambda i: (i, 0)),
                  pl.BlockSpec((tile, TOP_K), lambda i: (i, 0)),
                  pl.BlockSpec((1, D_MODEL), lambda i: (0, 0)),
                  pl.BlockSpec(memory_space=pl.ANY)],
        out_specs=pl.BlockSpec((tile, D_MODEL), lambda i: (i, 0)),
        out_shape=jax.ShapeDtypeStruct((t, D_MODEL), F32),
        scratch_shapes=[pltpu.VMEM((TOP_K, tile * ROW_TILES, LANES), F32),
                        pltpu.SemaphoreType.DMA(())],
        compiler_params=pltpu.CompilerParams(dimension_semantics=("arbitrary",)),
        name="combine",
    )(dest.reshape(nt, 1, TOP_K * tile), x1, gates_t, g_final, ys_rows)


def _split_bf16(w):
    hi = w.astype(BF16)
    lo = (w - hi.astype(F32)).astype(BF16)
    return hi, lo


def kernel(x_prompt, x_sample, cache_conv, cache_mem_k, cache_mem_v, mem_prompt, g_mix, w_in, w_dw, b_dw, ln_conv_g, ln_conv_b, ln_v_g, ln_v_b, w_spatial, b_spatial, g_mem, w_mem_k, w_mem_v, w_out, g_ffn, w_router, b_router, w_gate_up, b_gate_up, w_down, b_down, g_final):
    depth = g_mix.shape[0]
    assert depth == 1
    l = 0
    bp, seq, _ = x_prompt.shape
    bs, dseq, _ = x_sample.shape
    assert seq % PROMPT_TILE == 0 and bs % SAMPLE_SEQS_PER_TILE == 0
    assert GMLP_CHUNK % dseq == 0 and SAMPLE_SEQS_PER_TILE * dseq == GMLP_CHUNK

    row = lambda a: a.reshape(1, -1)
    wr_hi, wr_lo = _split_bf16(w_router[l].T)
    w_dw_pad = jnp.pad(w_dw[l], ((0, HIST - CONV_W), (0, 0)))
    bias_rows = lambda b: jnp.repeat(b.T, GMLP_HD, axis=1)
    common = dict(
        gmix=row(g_mix[l]), win=w_in[l].astype(BF16), wdw=w_dw_pad, bdw=row(b_dw[l]),
        lcg=row(ln_conv_g[l]), lcb=row(ln_conv_b[l]), lvg=row(ln_v_g[l]), lvb=row(ln_v_b[l]),
        wout=w_out[l].astype(BF16), gffn=row(g_ffn[l]), wrh=wr_hi, wrl=wr_lo,
        br=b_router[l].reshape(N_EXPERTS, 1))

    def weights(wsp, bsp):
        c = common
        return (c["gmix"], c["win"], c["wdw"], c["bdw"], c["lcg"], c["lcb"], c["lvg"], c["lvb"],
                wsp, bsp, c["wout"], c["gffn"], c["wrh"], c["wrl"], c["br"])

    reps = GMLP_CHUNK // dseq
    wts_p = weights(w_spatial[l], bias_rows(b_spatial[l]))
    wts_s = weights(jnp.tile(w_spatial[l][:, :dseq, :dseq], (1, reps, reps)),
                    bias_rows(jnp.tile(b_spatial[l][:, :dseq], (1, reps))))

    w_kv = jnp.concatenate([w_mem_k[l], w_mem_v[l]], axis=1).astype(BF16)
    kv_p = _memkv(mem_prompt, row(g_mem[l]), w_kv)
    mk_p = kv_p[:, :, :C_XA]
    mv_p = kv_p[:, :, C_XA:]
    zero_hist = jnp.zeros((bp, 1, HIST, C_CONV), F32)
    (x1_p, h2_p, eidx_p, gate_p, rank_p, cnt_p, hist_p, _) = _mixer(
        x_prompt, zero_hist, mk_p[:, None], mv_p[:, None], wts_p,
        ns=1, sl=PROMPT_TILE, carry=True, sp_chunk=GMLP_CHUNK)

    gs = bs // SAMPLE_SEQS_PER_TILE
    hist_s_in = jnp.pad(cache_conv[l], ((0, 0), (HIST_OFF, 0), (0, 0))).reshape(
        gs, SAMPLE_SEQS_PER_TILE, HIST, C_CONV)
    mk_s = cache_mem_k[l].reshape(gs, SAMPLE_SEQS_PER_TILE, N_MEM, C_XA)
    mv_s = cache_mem_v[l].reshape(gs, SAMPLE_SEQS_PER_TILE, N_MEM, C_XA)
    (x1_s, h2_s, eidx_s, gate_s, rank_s, cnt_s, hist_s, v_s) = _mixer(
        x_sample.reshape(gs, SAMPLE_SEQS_PER_TILE * dseq, D_MODEL), hist_s_in, mk_s, mv_s, wts_s,
        ns=SAMPLE_SEQS_PER_TILE, sl=dseq, carry=False, sp_chunk=dseq)

    tp, ts = bp * seq, bs * dseq
    ttot = tp + ts
    n_assign = ttot * TOP_K
    n_blocks = -(-n_assign // MOE_BLOCK) + N_EXPERTS
    cnt = jnp.concatenate([cnt_p[:, :, 0], cnt_s[:, :, 0]], axis=0)
    counts = jnp.sum(cnt, axis=0)
    tile_base = jnp.cumsum(cnt, axis=0) - cnt
    padded = (counts + MOE_BLOCK - 1) // MOE_BLOCK * MOE_BLOCK
    pad_end = jnp.cumsum(padded)
    pad_start = pad_end - padded
    base = pad_start[None, :] + tile_base
    tile_of_tok = np.concatenate([np.arange(tp) // PROMPT_TILE,
                                  tp // PROMPT_TILE + np.arange(ts) // (SAMPLE_SEQS_PER_TILE * dseq)])
    eidx = jnp.concatenate([eidx_p, eidx_s], axis=1).T
    rank = jnp.concatenate([rank_p, rank_s], axis=1).T
    dest = (jnp.take_along_axis(base[tile_of_tok], eidx, axis=1) + rank).astype(I32)
    n_used = (pad_end[-1] // MOE_BLOCK).astype(I32)
    blk_ids = jnp.arange(n_blocks, dtype=I32)
    block_expert = jnp.minimum(
        jnp.searchsorted(pad_end, jnp.minimum(blk_ids, n_used - 1) * MOE_BLOCK, side="right"),
        N_EXPERTS - 1).astype(I32)

    slots = n_blocks * MOE_BLOCK
    xs = jnp.zeros((slots, SUBLANES, LANES), F32)
    xs = _dispatch(dest[:tp], h2_p.reshape(tp, SUBLANES, LANES), xs)
    xs = _dispatch(dest[tp:], h2_s.reshape(ts, SUBLANES, LANES), xs)
    ys = _moe(block_expert, n_used.reshape(1), xs.reshape(slots * SUBLANES, LANES),
              w_gate_up[l], b_gate_up[l][:, None, :], w_down[l], b_down[l][:, None, :])
    ys_rows = ys.reshape(slots, SUBLANES, LANES)
    gates_t = jnp.concatenate([gate_p, gate_s], axis=1).T
    gfin = row(g_final)
    y_p = _combine(dest[:tp], x1_p, gates_t[:tp], gfin, ys_rows)
    y_s = _combine(dest[tp:], x1_s, gates_t[tp:], gfin, ys_rows)

    return (y_p.reshape(bp, seq, D_MODEL),
            y_s.reshape(bs, dseq, D_MODEL),
            hist_p[:, 0, HIST_OFF:, :][None],
            mk_p.reshape(bp, N_MEM, XA_HEADS, XA_HD)[None],
            mv_p.reshape(bp, N_MEM, XA_HEADS, XA_HD)[None],
            hist_s.reshape(bs, HIST, C_CONV)[:, HIST_OFF:, :][None],
            v_s.reshape(bs, dseq, C_GMLP)[None])
```

```python
import functools

import numpy as np
import jax
import jax.numpy as jnp
from jax import lax
from jax.experimental import pallas as pl
from jax.experimental.pallas import tpu as pltpu

F32 = jnp.float32
BF16 = jnp.bfloat16
I32 = jnp.int32

D_MODEL = 1024
C_CONV = 384
CONV_W = 31
C_GMLP = 384
GMLP_HEADS = 4
GMLP_HD = 96
GMLP_CHUNK = 128
XA_HEADS = 4
XA_HD = 64
C_XA = 256
N_MEM = 256
N_EXPERTS = 32
TOP_K = 4
D_FF = 1024
SWIGLU_LIMIT = 7.0
SWIGLU_ALPHA = 1.702
EPS = 1e-5
IN_COLS = 2 * C_CONV + 2 * C_GMLP + C_XA

SUBLANES = 8
LANES = 128
ROW_TILES = D_MODEL // LANES
HIST = 32
HIST_OFF = HIST - (CONV_W - 1)

PROMPT_TILE = 512
SAMPLE_SEQS_PER_TILE = 8
MOE_BLOCK = 256
DISPATCH_TILE = 512
COMBINE_TILE = 128
DMA_UNROLL = 8
VMEM_LIMIT_MIXER = 48 * 1024 * 1024
VMEM_LIMIT_MOE = 52 * 1024 * 1024


def _rmsnorm(x, g):
    return x * lax.rsqrt(jnp.mean(x * x, axis=-1, keepdims=True) + EPS) * g


def _layernorm(x, g, b):
    mu = jnp.mean(x, axis=-1, keepdims=True)
    xc = x - mu
    var = jnp.mean(xc * xc, axis=-1, keepdims=True)
    return xc * lax.rsqrt(var + EPS) * g + b


def _gelu(x):
    return 0.5 * x * (1.0 + lax.erf(x * np.float32(1.0 / np.sqrt(2.0))))


def _sigmoid(x):
    return 1.0 / (1.0 + jnp.exp(-x))


def _rows_to_tiles(dst_ref, val, rows):
    for j in range(ROW_TILES):
        dst_ref[pl.ds(j, rows, stride=ROW_TILES), :] = val[:, j * LANES:(j + 1) * LANES]


def _tiles_to_rows(src_ref, rows):
    return jnp.concatenate(
        [src_ref[pl.ds(j, rows, stride=ROW_TILES), :] for j in range(ROW_TILES)], axis=-1)


def _memkv_kernel(mem_ref, g_ref, w_ref, o_ref):
    mn = _rmsnorm(mem_ref[0], g_ref[...])
    o_ref[0] = jnp.dot(mn.astype(BF16), w_ref[...], preferred_element_type=F32)


def _memkv(mem, g_mem, w_kv):
    b = mem.shape[0]
    return pl.pallas_call(
        _memkv_kernel,
        grid=(b,),
        in_specs=[pl.BlockSpec((1, N_MEM, D_MODEL), lambda i: (i, 0, 0)),
                  pl.BlockSpec((1, D_MODEL), lambda i: (0, 0)),
                  pl.BlockSpec((D_MODEL, 2 * C_XA), lambda i: (0, 0))],
        out_specs=pl.BlockSpec((1, N_MEM, 2 * C_XA), lambda i: (i, 0, 0)),
        out_shape=jax.ShapeDtypeStruct((b, N_MEM, 2 * C_XA), F32),
        name="memkv",
    )(mem, g_mem, w_kv)


def _mixer_kernel(x_ref, hist_ref, mk_ref, mv_ref, gmix_ref, win_ref, wdw_ref, bdw_ref,
                  lcg_ref, lcb_ref, lvg_ref, lvb_ref, wsp_ref, bsp_ref, wout_ref, gffn_ref,
                  wrh_ref, wrl_ref, br_ref,
                  x1_ref, h2_ref, eidx_ref, gate_ref, rank_ref, cnt_ref, histout_ref, v_ref,
                  ext_ref, *, ns, sl, carry, sp_chunk):
    tm = ns * sl
    i = pl.program_id(1)

    x = x_ref[0]
    h = _rmsnorm(x, gmix_ref[...])
    z = jnp.dot(h.astype(BF16), win_ref[...], preferred_element_type=F32)
    z_a = z[:, 0:C_CONV]
    z_g = z[:, C_CONV:2 * C_CONV]
    z_u = z[:, 2 * C_CONV:2 * C_CONV + C_GMLP]
    z_v = z[:, 2 * C_CONV + C_GMLP:2 * C_CONV + 2 * C_GMLP]
    z_q = z[:, 2 * C_CONV + 2 * C_GMLP:IN_COLS]

    glu = z_a * _sigmoid(z_g)
    if carry:
        @pl.when(i == 0)
        def _():
            ext_ref[:, 0:HIST, :] = hist_ref[0]
    else:
        ext_ref[:, 0:HIST, :] = hist_ref[0]
    conv_parts = []
    for s in range(ns):
        ext_s = ext_ref.at[s]
        ext_s[HIST:HIST + sl, :] = glu[s * sl:(s + 1) * sl]
        rc = min(sl, 64)
        for r0 in range(0, sl, rc):
            acc = jnp.broadcast_to(bdw_ref[...], (rc, C_CONV))
            for j in range(CONV_W):
                acc = acc + wdw_ref[j:j + 1, :] * ext_s[pl.ds(r0 + j + HIST_OFF, rc), :]
            conv_parts.append(acc)
        new_hist = ext_s[sl:sl + HIST, :]
        histout_ref[0, s] = new_hist
        if carry:
            ext_s[0:HIST, :] = new_hist
    y = jnp.concatenate(conv_parts, axis=0) if len(conv_parts) > 1 else conv_parts[0]
    y = _layernorm(y, lcg_ref[...], lcb_ref[...])
    c_out = y * _sigmoid(y)

    u = _gelu(z_u)
    v = _layernorm(_gelu(z_v), lvg_ref[...], lvb_ref[...])
    v_ref[...] = v
    vb = v.astype(BF16)
    rr = lax.broadcasted_iota(I32, (GMLP_CHUNK, GMLP_CHUNK), 0)
    cc = lax.broadcasted_iota(I32, (GMLP_CHUNK, GMLP_CHUNK), 1)
    sp_mask = (cc <= rr) & ((rr // sp_chunk) == (cc // sp_chunk))
    col = lax.broadcasted_iota(I32, (GMLP_CHUNK, C_GMLP), 1)
    w_heads = [jnp.where(sp_mask, wsp_ref[hh], 0.0).astype(BF16) for hh in range(GMLP_HEADS)]
    g_parts = []
    for c in range(tm // GMLP_CHUNK):
        vc = vb[c * GMLP_CHUNK:(c + 1) * GMLP_CHUNK]
        sg = bsp_ref[...]
        for hh in range(GMLP_HEADS):
            head_cols = (col >= hh * GMLP_HD) & (col < (hh + 1) * GMLP_HD)
            vh = jnp.where(head_cols, vc, jnp.zeros_like(vc))
            sg = sg + jnp.dot(w_heads[hh], vh, preferred_element_type=F32)
        g_parts.append(u[c * GMLP_CHUNK:(c + 1) * GMLP_CHUNK] * sg)
    g_out = jnp.concatenate(g_parts, axis=0) if len(g_parts) > 1 else g_parts[0]

    qs = z_q * np.float32(XA_HD ** -0.5)
    qcol = lax.broadcasted_iota(I32, (sl, C_XA), 1)
    a_parts = []
    for s in range(ns):
        q_s = qs[s * sl:(s + 1) * sl]
        kb = mk_ref[0, s].astype(BF16)
        vvb = mv_ref[0, s].astype(BF16)
        a_s = jnp.zeros((sl, C_XA), F32)
        for hh in range(XA_HEADS):
            hmask = (qcol >= hh * XA_HD) & (qcol < (hh + 1) * XA_HD)
            qh = jnp.where(hmask, q_s, 0.0).astype(BF16)
            sc = lax.dot_general(qh, kb, (((1,), (1,)), ((), ())), preferred_element_type=F32)
            p = jnp.exp(sc - jnp.max(sc, axis=-1, keepdims=True))
            den = jnp.sum(p, axis=-1, keepdims=True)
            oh = jnp.dot(p.astype(BF16), vvb, preferred_element_type=F32) / den
            a_s = a_s + jnp.where(hmask, oh, 0.0)
        a_parts.append(a_s)
    a_out = jnp.concatenate(a_parts, axis=0) if len(a_parts) > 1 else a_parts[0]

    mix = jnp.concatenate([c_out, g_out, a_out], axis=-1).astype(BF16)
    x1 = x + jnp.dot(mix, wout_ref[...], preferred_element_type=F32)
    x1_ref[...] = x1

    h2 = _rmsnorm(x1, gffn_ref[...])
    _rows_to_tiles(h2_ref, h2, tm)
    h2_hi = h2.astype(BF16)
    h2_lo = (h2 - h2_hi.astype(F32)).astype(BF16)
    nt_dims = (((1,), (1,)), ((), ()))
    lg = (lax.dot_general(wrh_ref[...], h2_hi, nt_dims, preferred_element_type=F32)
          + lax.dot_general(wrl_ref[...], h2_hi, nt_dims, preferred_element_type=F32)
          + lax.dot_general(wrh_ref[...], h2_lo, nt_dims, preferred_element_type=F32)
          + br_ref[...])
    eio = lax.broadcasted_iota(I32, (N_EXPERTS, tm), 0)
    work = lg
    vals, idxs = [], []
    for _ in range(TOP_K):
        m = jnp.max(work, axis=0, keepdims=True)
        idx = jnp.min(jnp.where(work == m, eio, N_EXPERTS), axis=0, keepdims=True)
        vals.append(m)
        idxs.append(idx)
        work = jnp.where(eio == idx, -jnp.inf, work)
    exps = [jnp.exp(vk - vals[0]) for vk in vals]
    den = exps[0] + exps[1] + exps[2] + exps[3]
    eidx_ref[...] = jnp.concatenate(idxs, axis=0)
    gate_ref[...] = jnp.concatenate([ek / den for ek in exps], axis=0)

    sel = jnp.zeros((N_EXPERTS, tm), F32)
    for idx in idxs:
        sel = sel + jnp.where(eio == idx, 1.0, 0.0)
    tr = lax.broadcasted_iota(I32, (tm, tm), 0)
    tc = lax.broadcasted_iota(I32, (tm, tm), 1)
    before = jnp.where(tr < tc, 1.0, 0.0).astype(BF16)
    ranks = jnp.dot(sel.astype(BF16), before, preferred_element_type=F32)
    rank_ref[...] = jnp.concatenate(
        [jnp.sum(jnp.where(eio == idx, ranks, 0.0), axis=0, keepdims=True) for idx in idxs],
        axis=0).astype(I32)
    cnt_ref[0] = jnp.broadcast_to(jnp.sum(sel, axis=1, keepdims=True), (N_EXPERTS, LANES)).astype(I32)


def _mixer(x, hist, mem_k, mem_v, wts, *, ns, sl, carry, sp_chunk):
    g, r, _ = x.shape
    tm = ns * sl
    nt = r // tm
    ntot = g * nt
    const2 = lambda b, i: (0, 0)
    const3 = lambda b, i: (0, 0, 0)
    tile_row = lambda b, i: (b * nt + i, 0)
    tile_lane = lambda b, i: (0, b * nt + i)
    in_specs = [
        pl.BlockSpec((1, tm, D_MODEL), lambda b, i: (b, i, 0)),
        pl.BlockSpec((1, ns, HIST, C_CONV), lambda b, i: (b, 0, 0, 0)),
        pl.BlockSpec((1, ns, N_MEM, C_XA), lambda b, i: (b, 0, 0, 0)),
        pl.BlockSpec((1, ns, N_MEM, C_XA), lambda b, i: (b, 0, 0, 0)),
        pl.BlockSpec((1, D_MODEL), const2),
        pl.BlockSpec((D_MODEL, IN_COLS), const2),
        pl.BlockSpec((HIST, C_CONV), const2),
        pl.BlockSpec((1, C_CONV), const2),
        pl.BlockSpec((1, C_CONV), const2),
        pl.BlockSpec((1, C_CONV), const2),
        pl.BlockSpec((1, C_GMLP), const2),
        pl.BlockSpec((1, C_GMLP), const2),
        pl.BlockSpec((GMLP_HEADS, GMLP_CHUNK, GMLP_CHUNK), const3),
        pl.BlockSpec((GMLP_CHUNK, C_GMLP), const2),
        pl.BlockSpec((D_MODEL, D_MODEL), const2),
        pl.BlockSpec((1, D_MODEL), const2),
        pl.BlockSpec((N_EXPERTS, D_MODEL), const2),
        pl.BlockSpec((N_EXPERTS, D_MODEL), const2),
        pl.BlockSpec((N_EXPERTS, 1), const2),
    ]
    out_specs = [
        pl.BlockSpec((tm, D_MODEL), tile_row),
        pl.BlockSpec((tm * ROW_TILES, LANES), tile_row),
        pl.BlockSpec((TOP_K, tm), tile_lane),
        pl.BlockSpec((TOP_K, tm), tile_lane),
        pl.BlockSpec((TOP_K, tm), tile_lane),
        pl.BlockSpec((1, N_EXPERTS, LANES), lambda b, i: (b * nt + i, 0, 0)),
        pl.BlockSpec((1, ns, HIST, C_CONV), lambda b, i: (b, 0, 0, 0)),
        pl.BlockSpec((tm, C_GMLP), tile_row),
    ]
    rows = g * r
    out_shape = [
        jax.ShapeDtypeStruct((rows, D_MODEL), F32),
        jax.ShapeDtypeStruct((rows * ROW_TILES, LANES), F32),
        jax.ShapeDtypeStruct((TOP_K, rows), I32),
        jax.ShapeDtypeStruct((TOP_K, rows), F32),
        jax.ShapeDtypeStruct((TOP_K, rows), I32),
        jax.ShapeDtypeStruct((ntot, N_EXPERTS, LANES), I32),
        jax.ShapeDtypeStruct((g, ns, HIST, C_CONV), F32),
        jax.ShapeDtypeStruct((rows, C_GMLP), F32),
    ]
    kern = functools.partial(_mixer_kernel, ns=ns, sl=sl, carry=carry, sp_chunk=sp_chunk)
    return pl.pallas_call(
        kern,
        grid=(g, nt),
        in_specs=in_specs,
        out_specs=out_specs,
        out_shape=out_shape,
        scratch_shapes=[pltpu.VMEM((ns, HIST + sl, C_CONV), F32)],
        compiler_params=pltpu.CompilerParams(
            dimension_semantics=("arbitrary", "arbitrary"),
            vmem_limit_bytes=VMEM_LIMIT_MIXER),
        name="mixer_carry" if carry else "mixer_cache",
    )(x, hist, mem_k, mem_v, *wts)


def _dispatch_kernel(dest_ref, h2_ref, *rest, tile):
    xs_hbm, sem = rest[-2], rest[-1]

    def row_copy(t, k):
        src = h2_ref.at[pl.ds(pl.multiple_of(t * ROW_TILES, ROW_TILES), ROW_TILES)]
        dst = xs_hbm.at[pl.ds(pl.multiple_of(dest_ref[k, t] * ROW_TILES, ROW_TILES), ROW_TILES)]
        return pltpu.make_async_copy(src, dst, sem)

    def start(c, carry):
        for u in range(DMA_UNROLL):
            for k in range(TOP_K):
                row_copy(c * DMA_UNROLL + u, k).start()
        return carry

    lax.fori_loop(0, tile // DMA_UNROLL, start, 0)
    for _ in range(TOP_K):
        pltpu.make_async_copy(h2_ref, xs_hbm.at[pl.ds(0, tile * ROW_TILES)], sem).wait()


def _dispatch(dest, h2_rows, xs, slots):
    t = dest.shape[1]
    tile = min(DISPATCH_TILE, t)
    nt = t // tile
    in_specs = [pl.BlockSpec((TOP_K, tile), lambda i: (0, i), memory_space=pltpu.SMEM),
                pl.BlockSpec((tile * ROW_TILES, LANES), lambda i: (i, 0))]
    args = [dest, h2_rows]
    aliases = {}
    if xs is not None:
        in_specs.append(pl.BlockSpec(memory_space=pl.ANY))
        args.append(xs)
        aliases = {2: 0}
    return pl.pallas_call(
        functools.partial(_dispatch_kernel, tile=tile),
        grid=(nt,),
        in_specs=in_specs,
        out_specs=pl.BlockSpec(memory_space=pl.ANY),
        out_shape=jax.ShapeDtypeStruct((slots * ROW_TILES, LANES), F32),
        scratch_shapes=[pltpu.SemaphoreType.DMA(())],
        input_output_aliases=aliases,
        compiler_params=pltpu.CompilerParams(dimension_semantics=("arbitrary",)),
        name="dispatch",
    )(*args)


def _moe_kernel(be_ref, bv_ref, nb_ref, xs_ref, wgu_ref, bgu_ref, wdn_ref, bdn_ref, ys_ref, wgu16, wdn16):
    i = pl.program_id(0)

    @pl.when(i < nb_ref[0])
    def _():
        prev = be_ref[jnp.maximum(i - 1, 0)]
        new_expert = (i == 0) | (be_ref[i] != prev)

        @pl.when(new_expert)
        def _():
            def cast_rows(c, carry):
                r = pl.multiple_of(c * LANES, LANES)
                wgu16[pl.ds(r, LANES), :] = wgu_ref[pl.ds(r, LANES), :].astype(BF16)
                wdn16[pl.ds(r, LANES), :] = wdn_ref[pl.ds(r, LANES), :].astype(BF16)
                return carry
            lax.fori_loop(0, D_MODEL // LANES, cast_rows, 0)

        row_ok = lax.broadcasted_iota(I32, (MOE_BLOCK, 1), 0) < bv_ref[i]
        xb = jnp.where(row_ok, _tiles_to_rows(xs_ref, MOE_BLOCK), 0.0).astype(BF16)
        gu = jnp.dot(xb, wgu16[...], preferred_element_type=F32) + bgu_ref[0]
        gate = jnp.minimum(gu[:, :D_FF], SWIGLU_LIMIT)
        up = jnp.clip(gu[:, D_FF:], -SWIGLU_LIMIT, SWIGLU_LIMIT)
        act = (up + 1.0) * (gate * _sigmoid(SWIGLU_ALPHA * gate))
        yb = jnp.dot(act.astype(BF16), wdn16[...], preferred_element_type=F32) + bdn_ref[0]
        _rows_to_tiles(ys_ref, yb, MOE_BLOCK)

    @pl.when(i >= nb_ref[0])
    def _():
        ys_ref[...] = jnp.zeros_like(ys_ref)


def _moe(block_expert, block_valid, n_used, xs, w_gu, b_gu, w_dn, b_dn):
    n_blocks = block_expert.shape[0]
    blk = lambda i, be, bv, nb: (jnp.minimum(i, nb[0] - 1), 0)
    wsel = lambda i, be, bv, nb: (be[i], 0, 0)
    grid_spec = pltpu.PrefetchScalarGridSpec(
        num_scalar_prefetch=3,
        grid=(n_blocks,),
        in_specs=[pl.BlockSpec((MOE_BLOCK * ROW_TILES, LANES), blk),
                  pl.BlockSpec((None, D_MODEL, 2 * D_FF), wsel),
                  pl.BlockSpec((None, 1, 2 * D_FF), wsel),
                  pl.BlockSpec((None, D_FF, D_MODEL), wsel),
                  pl.BlockSpec((None, 1, D_MODEL), wsel)],
        out_specs=pl.BlockSpec((MOE_BLOCK * ROW_TILES, LANES), lambda i, be, bv, nb: (i, 0)),
        scratch_shapes=[pltpu.VMEM((D_MODEL, 2 * D_FF), BF16),
                        pltpu.VMEM((D_FF, D_MODEL), BF16)],
    )
    return pl.pallas_call(
        _moe_kernel,
        grid_spec=grid_spec,
        out_shape=jax.ShapeDtypeStruct(xs.shape, F32),
        compiler_params=pltpu.CompilerParams(
            dimension_semantics=("arbitrary",),
            vmem_limit_bytes=VMEM_LIMIT_MOE),
        name="moe",
    )(block_expert, block_valid, n_used, xs, w_gu, b_gu, w_dn, b_dn)


def _combine_kernel(dest_ref, x1_ref, gate_ref, gfin_ref, ys_hbm, out_ref, buf, sem, *, tile):
    def row_copy(t, k):
        src = ys_hbm.at[pl.ds(pl.multiple_of(dest_ref[k, t] * ROW_TILES, ROW_TILES), ROW_TILES)]
        dst = buf.at[k, pl.ds(pl.multiple_of(t * ROW_TILES, ROW_TILES), ROW_TILES)]
        return pltpu.make_async_copy(src, dst, sem)

    def start(c, carry):
        for u in range(DMA_UNROLL):
            for k in range(TOP_K):
                row_copy(c * DMA_UNROLL + u, k).start()
        return carry

    lax.fori_loop(0, tile // DMA_UNROLL, start, 0)
    for k in range(TOP_K):
        pltpu.make_async_copy(ys_hbm.at[pl.ds(0, tile * ROW_TILES)], buf.at[k], sem).wait()

    rr = lax.broadcasted_iota(I32, (tile, tile), 0)
    cc = lax.broadcasted_iota(I32, (tile, tile), 1)
    acc = x1_ref[...]
    for k in range(TOP_K):
        g_col = jnp.sum(jnp.where(rr == cc, gate_ref[k:k + 1, :], 0.0), axis=1, keepdims=True)
        acc = acc + g_col * _tiles_to_rows(buf.at[k], tile)
    out_ref[...] = _rmsnorm(acc, gfin_ref[...])


def _combine(dest, x1, gates, g_final, ys):
    t = x1.shape[0]
    tile = COMBINE_TILE
    nt = t // tile
    return pl.pallas_call(
        functools.partial(_combine_kernel, tile=tile),
        grid=(nt,),
        in_specs=[pl.BlockSpec((TOP_K, tile), lambda i: (0, i), memory_space=pltpu.SMEM),
                  pl.BlockSpec((tile, D_MODEL), lambda i: (i, 0)),
                  pl.BlockSpec((TOP_K, tile), lambda i: (0, i)),
                  pl.BlockSpec((1, D_MODEL), lambda i: (0, 0)),
                  pl.BlockSpec(memory_space=pl.ANY)],
        out_specs=pl.BlockSpec((tile, D_MODEL), lambda i: (i, 0)),
        out_shape=jax.ShapeDtypeStruct((t, D_MODEL), F32),
        scratch_shapes=[pltpu.VMEM((TOP_K, tile * ROW_TILES, LANES), F32),
                        pltpu.SemaphoreType.DMA(())],
        compiler_params=pltpu.CompilerParams(dimension_semantics=("arbitrary",)),
        name="combine",
    )(dest, x1, gates, g_final, ys)


def _split_bf16(w):
    hi = w.astype(BF16)
    lo = (w - hi.astype(F32)).astype(BF16)
    return hi, lo


def kernel(x_prompt, x_sample, cache_conv, cache_mem_k, cache_mem_v, mem_prompt, g_mix, w_in, w_dw, b_dw, ln_conv_g, ln_conv_b, ln_v_g, ln_v_b, w_spatial, b_spatial, g_mem, w_mem_k, w_mem_v, w_out, g_ffn, w_router, b_router, w_gate_up, b_gate_up, w_down, b_down, g_final):
    depth = g_mix.shape[0]
    assert depth == 1
    l = 0
    bp, seq, _ = x_prompt.shape
    bs, dseq, _ = x_sample.shape
    assert seq % PROMPT_TILE == 0 and bs % SAMPLE_SEQS_PER_TILE == 0
    assert GMLP_CHUNK % dseq == 0 and SAMPLE_SEQS_PER_TILE * dseq == GMLP_CHUNK

    row = lambda a: a.reshape(1, -1)
    wr_hi, wr_lo = _split_bf16(w_router[l].T)
    w_dw_pad = jnp.pad(w_dw[l], ((0, HIST - CONV_W), (0, 0)))
    bias_rows = lambda b: jnp.repeat(b.T, GMLP_HD, axis=1)
    common = dict(
        gmix=row(g_mix[l]), win=w_in[l].astype(BF16), wdw=w_dw_pad, bdw=row(b_dw[l]),
        lcg=row(ln_conv_g[l]), lcb=row(ln_conv_b[l]), lvg=row(ln_v_g[l]), lvb=row(ln_v_b[l]),
        wout=w_out[l].astype(BF16), gffn=row(g_ffn[l]), wrh=wr_hi, wrl=wr_lo,
        br=b_router[l].reshape(N_EXPERTS, 1))

    def weights(wsp, bsp):
        c = common
        return (c["gmix"], c["win"], c["wdw"], c["bdw"], c["lcg"], c["lcb"], c["lvg"], c["lvb"],
                wsp, bsp, c["wout"], c["gffn"], c["wrh"], c["wrl"], c["br"])

    reps = GMLP_CHUNK // dseq
    wts_p = weights(w_spatial[l], bias_rows(b_spatial[l]))
    wts_s = weights(jnp.tile(w_spatial[l][:, :dseq, :dseq], (1, reps, reps)),
                    bias_rows(jnp.tile(b_spatial[l][:, :dseq], (1, reps))))

    w_kv = jnp.concatenate([w_mem_k[l], w_mem_v[l]], axis=1).astype(BF16)
    kv_p = _memkv(mem_prompt, row(g_mem[l]), w_kv)
    mk_p = kv_p[:, :, :C_XA]
    mv_p = kv_p[:, :, C_XA:]
    zero_hist = jnp.zeros((bp, 1, HIST, C_CONV), F32)
    (x1_p, h2_p, eidx_p, gate_p, rank_p, cnt_p, hist_p, _) = _mixer(
        x_prompt, zero_hist, mk_p[:, None], mv_p[:, None], wts_p,
        ns=1, sl=PROMPT_TILE, carry=True, sp_chunk=GMLP_CHUNK)

    gs = bs // SAMPLE_SEQS_PER_TILE
    hist_s_in = jnp.pad(cache_conv[l], ((0, 0), (HIST_OFF, 0), (0, 0))).reshape(
        gs, SAMPLE_SEQS_PER_TILE, HIST, C_CONV)
    mk_s = cache_mem_k[l].reshape(gs, SAMPLE_SEQS_PER_TILE, N_MEM, C_XA)
    mv_s = cache_mem_v[l].reshape(gs, SAMPLE_SEQS_PER_TILE, N_MEM, C_XA)
    (x1_s, h2_s, eidx_s, gate_s, rank_s, cnt_s, hist_s, v_s) = _mixer(
        x_sample.reshape(gs, SAMPLE_SEQS_PER_TILE * dseq, D_MODEL), hist_s_in, mk_s, mv_s, wts_s,
        ns=SAMPLE_SEQS_PER_TILE, sl=dseq, carry=False, sp_chunk=dseq)

    tp, ts = bp * seq, bs * dseq
    ttot = tp + ts
    n_assign = ttot * TOP_K
    n_blocks = -(-n_assign // MOE_BLOCK) + N_EXPERTS
    cnt = jnp.concatenate([cnt_p[:, :, 0], cnt_s[:, :, 0]], axis=0)
    counts = jnp.sum(cnt, axis=0)
    tile_base = jnp.cumsum(cnt, axis=0) - cnt
    padded = (counts + MOE_BLOCK - 1) // MOE_BLOCK * MOE_BLOCK
    pad_end = jnp.cumsum(padded)
    pad_start = pad_end - padded
    base = pad_start[None, :] + tile_base
    ntp = tp // PROMPT_TILE
    experts = jnp.arange(N_EXPERTS, dtype=I32)[None, :, None]

    def slots_of(eidx, rank, tile_bases, tile_rows):
        base_tok = jnp.repeat(tile_bases.T, tile_rows, axis=1)
        pick = jnp.where(eidx[:, None, :] == experts, base_tok[None], 0)
        return (jnp.sum(pick, axis=1) + rank).astype(I32)

    dest_p = slots_of(eidx_p, rank_p, base[:ntp], PROMPT_TILE)
    dest_s = slots_of(eidx_s, rank_s, base[ntp:], SAMPLE_SEQS_PER_TILE * dseq)
    n_used = (pad_end[-1] // MOE_BLOCK).astype(I32)
    blk_row0 = jnp.minimum(jnp.arange(n_blocks, dtype=I32), n_used - 1) * MOE_BLOCK
    block_expert = jnp.minimum(
        jnp.sum((pad_end[None, :] <= blk_row0[:, None]).astype(I32), axis=1), N_EXPERTS - 1)
    of_block = block_expert[:, None] == jnp.arange(N_EXPERTS, dtype=I32)[None, :]
    last_row = jnp.sum(jnp.where(of_block, (pad_start + counts)[None, :], 0), axis=1)
    block_valid = jnp.clip(last_row - blk_row0, 0, MOE_BLOCK).astype(I32)

    slots = n_blocks * MOE_BLOCK
    xs = _dispatch(dest_p, h2_p, None, slots)
    xs = _dispatch(dest_s, h2_s, xs, slots)
    ys = _moe(block_expert.astype(I32), block_valid, n_used.reshape(1), xs,
              w_gate_up[l], b_gate_up[l][:, None, :], w_down[l], b_down[l][:, None, :])
    gfin = row(g_final)
    y_p = _combine(dest_p, x1_p, gate_p, gfin, ys)
    y_s = _combine(dest_s, x1_s, gate_s, gfin, ys)

    return (y_p.reshape(bp, seq, D_MODEL),
            y_s.reshape(bs, dseq, D_MODEL),
            hist_p[:, 0, HIST_OFF:, :][None],
            mk_p.reshape(bp, N_MEM, XA_HEADS, XA_HD)[None],
            mv_p.reshape(bp, N_MEM, XA_HEADS, XA_HD)[None],
            hist_s.reshape(bs, HIST, C_CONV)[:, HIST_OFF:, :][None],
            v_s.reshape(bs, dseq, C_GMLP)[None])
```

```python
import functools

import numpy as np
import jax
import jax.numpy as jnp
from jax import lax
from jax.experimental import pallas as pl
from jax.experimental.pallas import tpu as pltpu

F32 = jnp.float32
BF16 = jnp.bfloat16
I32 = jnp.int32

D_MODEL = 1024
C_CONV = 384
CONV_W = 31
C_GMLP = 384
GMLP_HEADS = 4
GMLP_HD = 96
GMLP_CHUNK = 128
XA_HEADS = 4
XA_HD = 64
C_XA = 256
N_MEM = 256
N_EXPERTS = 32
TOP_K = 4
D_FF = 1024
SWIGLU_LIMIT = 7.0
SWIGLU_ALPHA = 1.702
EPS = 1e-5
IN_COLS = 2 * C_CONV + 2 * C_GMLP + C_XA

SUBLANES = 8
LANES = 128
ROW_TILES = D_MODEL // LANES
HIST = 32
HIST_OFF = HIST - (CONV_W - 1)

PROMPT_TILE = 512
SAMPLE_SEQS_PER_TILE = 8
MOE_BLOCK = 512
DISPATCH_TILE = 512
COMBINE_TILE = 128
DMA_UNROLL = 8
VMEM_LIMIT_MIXER = 48 * 1024 * 1024
VMEM_LIMIT_MOE = 52 * 1024 * 1024


def _rmsnorm(x, g):
    return x * lax.rsqrt(jnp.mean(x * x, axis=-1, keepdims=True) + EPS) * g


def _layernorm(x, g, b):
    mu = jnp.mean(x, axis=-1, keepdims=True)
    xc = x - mu
    var = jnp.mean(xc * xc, axis=-1, keepdims=True)
    return xc * lax.rsqrt(var + EPS) * g + b


def _gelu(x):
    return 0.5 * x * (1.0 + lax.erf(x * np.float32(1.0 / np.sqrt(2.0))))


def _sigmoid(x):
    return 1.0 / (1.0 + jnp.exp(-x))


def _rows_to_tiles(dst_ref, val, rows):
    for j in range(ROW_TILES):
        dst_ref[pl.ds(j, rows, stride=ROW_TILES), :] = val[:, j * LANES:(j + 1) * LANES]


def _tiles_to_rows(src_ref, rows):
    return jnp.concatenate(
        [src_ref[pl.ds(j, rows, stride=ROW_TILES), :] for j in range(ROW_TILES)], axis=-1)


def _memkv_kernel(mem_ref, g_ref, w_ref, o_ref):
    mn = _rmsnorm(mem_ref[0], g_ref[...])
    o_ref[0] = jnp.dot(mn.astype(BF16), w_ref[...], preferred_element_type=F32)


def _memkv(mem, g_mem, w_kv):
    b = mem.shape[0]
    return pl.pallas_call(
        _memkv_kernel,
        grid=(b,),
        in_specs=[pl.BlockSpec((1, N_MEM, D_MODEL), lambda i: (i, 0, 0)),
                  pl.BlockSpec((1, D_MODEL), lambda i: (0, 0)),
                  pl.BlockSpec((D_MODEL, 2 * C_XA), lambda i: (0, 0))],
        out_specs=pl.BlockSpec((1, N_MEM, 2 * C_XA), lambda i: (i, 0, 0)),
        out_shape=jax.ShapeDtypeStruct((b, N_MEM, 2 * C_XA), F32),
        name="memkv",
    )(mem, g_mem, w_kv)


def _mixer_kernel(x_ref, hist_ref, mk_ref, mv_ref, gmix_ref, win_ref, wdw_ref, bdw_ref,
                  lcg_ref, lcb_ref, lvg_ref, lvb_ref, wsp_ref, bsp_ref, wout_ref, gffn_ref,
                  wrh_ref, wrl_ref, br_ref,
                  x1_ref, h2_ref, eidx_ref, gate_ref, rank_ref, cnt_ref, histout_ref, v_ref,
                  ext_ref, *, ns, sl, carry, sp_chunk):
    tm = ns * sl
    i = pl.program_id(1)

    x = x_ref[0]
    h = _rmsnorm(x, gmix_ref[...])
    z = jnp.dot(h.astype(BF16), win_ref[...], preferred_element_type=F32)
    z_a = z[:, 0:C_CONV]
    z_g = z[:, C_CONV:2 * C_CONV]
    z_u = z[:, 2 * C_CONV:2 * C_CONV + C_GMLP]
    z_v = z[:, 2 * C_CONV + C_GMLP:2 * C_CONV + 2 * C_GMLP]
    z_q = z[:, 2 * C_CONV + 2 * C_GMLP:IN_COLS]

    glu = z_a * _sigmoid(z_g)
    if carry:
        @pl.when(i == 0)
        def _():
            ext_ref[:, 0:HIST, :] = hist_ref[0]
    else:
        ext_ref[:, 0:HIST, :] = hist_ref[0]
    conv_parts = []
    for s in range(ns):
        ext_s = ext_ref.at[s]
        ext_s[HIST:HIST + sl, :] = glu[s * sl:(s + 1) * sl]
        rc = min(sl, 64)
        for r0 in range(0, sl, rc):
            acc = jnp.broadcast_to(bdw_ref[...], (rc, C_CONV))
            for j in range(CONV_W):
                acc = acc + wdw_ref[j:j + 1, :] * ext_s[pl.ds(r0 + j + HIST_OFF, rc), :]
            conv_parts.append(acc)
        new_hist = ext_s[sl:sl + HIST, :]
        histout_ref[0, s] = new_hist
        if carry:
            ext_s[0:HIST, :] = new_hist
    y = jnp.concatenate(conv_parts, axis=0) if len(conv_parts) > 1 else conv_parts[0]
    y = _layernorm(y, lcg_ref[...], lcb_ref[...])
    c_out = y * _sigmoid(y)

    u = _gelu(z_u)
    v = _layernorm(_gelu(z_v), lvg_ref[...], lvb_ref[...])
    v_ref[...] = v
    vb = v.astype(BF16)
    rr = lax.broadcasted_iota(I32, (GMLP_CHUNK, GMLP_CHUNK), 0)
    cc = lax.broadcasted_iota(I32, (GMLP_CHUNK, GMLP_CHUNK), 1)
    sp_mask = (cc <= rr) & ((rr // sp_chunk) == (cc // sp_chunk))
    col = lax.broadcasted_iota(I32, (GMLP_CHUNK, C_GMLP), 1)
    w_heads = [jnp.where(sp_mask, wsp_ref[hh], 0.0).astype(BF16) for hh in range(GMLP_HEADS)]
    g_parts = []
    for c in range(tm // GMLP_CHUNK):
        vc = vb[c * GMLP_CHUNK:(c + 1) * GMLP_CHUNK]
        sg = bsp_ref[...]
        for hh in range(GMLP_HEADS):
            head_cols = (col >= hh * GMLP_HD) & (col < (hh + 1) * GMLP_HD)
            vh = jnp.where(head_cols, vc, jnp.zeros_like(vc))
            sg = sg + jnp.dot(w_heads[hh], vh, preferred_element_type=F32)
        g_parts.append(u[c * GMLP_CHUNK:(c + 1) * GMLP_CHUNK] * sg)
    g_out = jnp.concatenate(g_parts, axis=0) if len(g_parts) > 1 else g_parts[0]

    qs = z_q * np.float32(XA_HD ** -0.5)
    qcol = lax.broadcasted_iota(I32, (sl, C_XA), 1)
    a_parts = []
    for s in range(ns):
        q_s = qs[s * sl:(s + 1) * sl]
        kb = mk_ref[0, s].astype(BF16)
        vvb = mv_ref[0, s].astype(BF16)
        a_s = jnp.zeros((sl, C_XA), F32)
        for hh in range(XA_HEADS):
            hmask = (qcol >= hh * XA_HD) & (qcol < (hh + 1) * XA_HD)
            qh = jnp.where(hmask, q_s, 0.0).astype(BF16)
            sc = lax.dot_general(qh, kb, (((1,), (1,)), ((), ())), preferred_element_type=F32)
            p = jnp.exp(sc - jnp.max(sc, axis=-1, keepdims=True))
            den = jnp.sum(p, axis=-1, keepdims=True)
            oh = jnp.dot(p.astype(BF16), vvb, preferred_element_type=F32) / den
            a_s = a_s + jnp.where(hmask, oh, 0.0)
        a_parts.append(a_s)
    a_out = jnp.concatenate(a_parts, axis=0) if len(a_parts) > 1 else a_parts[0]

    mix = jnp.concatenate([c_out, g_out, a_out], axis=-1).astype(BF16)
    x1 = x + jnp.dot(mix, wout_ref[...], preferred_element_type=F32)
    x1_ref[...] = x1

    h2 = _rmsnorm(x1, gffn_ref[...])
    _rows_to_tiles(h2_ref, h2, tm)
    h2_hi = h2.astype(BF16)
    h2_lo = (h2 - h2_hi.astype(F32)).astype(BF16)
    nt_dims = (((1,), (1,)), ((), ()))
    lg = (lax.dot_general(wrh_ref[...], h2_hi, nt_dims, preferred_element_type=F32)
          + lax.dot_general(wrl_ref[...], h2_hi, nt_dims, preferred_element_type=F32)
          + lax.dot_general(wrh_ref[...], h2_lo, nt_dims, preferred_element_type=F32)
          + br_ref[...])
    eio = lax.broadcasted_iota(I32, (N_EXPERTS, tm), 0)
    work = lg
    vals, idxs = [], []
    for _ in range(TOP_K):
        m = jnp.max(work, axis=0, keepdims=True)
        idx = jnp.min(jnp.where(work == m, eio, N_EXPERTS), axis=0, keepdims=True)
        vals.append(m)
        idxs.append(idx)
        work = jnp.where(eio == idx, -jnp.inf, work)
    exps = [jnp.exp(vk - vals[0]) for vk in vals]
    den = exps[0] + exps[1] + exps[2] + exps[3]
    eidx_ref[...] = jnp.concatenate(idxs, axis=0)
    gate_ref[...] = jnp.concatenate([ek / den for ek in exps], axis=0)

    sel = jnp.zeros((N_EXPERTS, tm), F32)
    for idx in idxs:
        sel = sel + jnp.where(eio == idx, 1.0, 0.0)
    tr = lax.broadcasted_iota(I32, (tm, tm), 0)
    tc = lax.broadcasted_iota(I32, (tm, tm), 1)
    before = jnp.where(tr < tc, 1.0, 0.0).astype(BF16)
    ranks = jnp.dot(sel.astype(BF16), before, preferred_element_type=F32)
    rank_ref[...] = jnp.concatenate(
        [jnp.sum(jnp.where(eio == idx, ranks, 0.0), axis=0, keepdims=True) for idx in idxs],
        axis=0).astype(I32)
    cnt_ref[0] = jnp.broadcast_to(jnp.sum(sel, axis=1, keepdims=True), (N_EXPERTS, LANES)).astype(I32)


def _mixer(x, hist, mem_k, mem_v, wts, *, ns, sl, carry, sp_chunk):
    g, r, _ = x.shape
    tm = ns * sl
    nt = r // tm
    ntot = g * nt
    const2 = lambda b, i: (0, 0)
    const3 = lambda b, i: (0, 0, 0)
    tile_row = lambda b, i: (b * nt + i, 0)
    tile_lane = lambda b, i: (0, b * nt + i)
    in_specs = [
        pl.BlockSpec((1, tm, D_MODEL), lambda b, i: (b, i, 0)),
        pl.BlockSpec((1, ns, HIST, C_CONV), lambda b, i: (b, 0, 0, 0)),
        pl.BlockSpec((1, ns, N_MEM, C_XA), lambda b, i: (b, 0, 0, 0)),
        pl.BlockSpec((1, ns, N_MEM, C_XA), lambda b, i: (b, 0, 0, 0)),
        pl.BlockSpec((1, D_MODEL), const2),
        pl.BlockSpec((D_MODEL, IN_COLS), const2),
        pl.BlockSpec((HIST, C_CONV), const2),
        pl.BlockSpec((1, C_CONV), const2),
        pl.BlockSpec((1, C_CONV), const2),
        pl.BlockSpec((1, C_CONV), const2),
        pl.BlockSpec((1, C_GMLP), const2),
        pl.BlockSpec((1, C_GMLP), const2),
        pl.BlockSpec((GMLP_HEADS, GMLP_CHUNK, GMLP_CHUNK), const3),
        pl.BlockSpec((GMLP_CHUNK, C_GMLP), const2),
        pl.BlockSpec((D_MODEL, D_MODEL), const2),
        pl.BlockSpec((1, D_MODEL), const2),
        pl.BlockSpec((N_EXPERTS, D_MODEL), const2),
        pl.BlockSpec((N_EXPERTS, D_MODEL), const2),
        pl.BlockSpec((N_EXPERTS, 1), const2),
    ]
    out_specs = [
        pl.BlockSpec((tm, D_MODEL), tile_row),
        pl.BlockSpec((tm * ROW_TILES, LANES), tile_row),
        pl.BlockSpec((TOP_K, tm), tile_lane),
        pl.BlockSpec((TOP_K, tm), tile_lane),
        pl.BlockSpec((TOP_K, tm), tile_lane),
        pl.BlockSpec((1, N_EXPERTS, LANES), lambda b, i: (b * nt + i, 0, 0)),
        pl.BlockSpec((1, ns, HIST, C_CONV), lambda b, i: (b, 0, 0, 0)),
        pl.BlockSpec((tm, C_GMLP), tile_row),
    ]
    rows = g * r
    out_shape = [
        jax.ShapeDtypeStruct((rows, D_MODEL), F32),
        jax.ShapeDtypeStruct((rows * ROW_TILES, LANES), F32),
        jax.ShapeDtypeStruct((TOP_K, rows), I32),
        jax.ShapeDtypeStruct((TOP_K, rows), F32),
        jax.ShapeDtypeStruct((TOP_K, rows), I32),
        jax.ShapeDtypeStruct((ntot, N_EXPERTS, LANES), I32),
        jax.ShapeDtypeStruct((g, ns, HIST, C_CONV), F32),
        jax.ShapeDtypeStruct((rows, C_GMLP), F32),
    ]
    kern = functools.partial(_mixer_kernel, ns=ns, sl=sl, carry=carry, sp_chunk=sp_chunk)
    return pl.pallas_call(
        kern,
        grid=(g, nt),
        in_specs=in_specs,
        out_specs=out_specs,
        out_shape=out_shape,
        scratch_shapes=[pltpu.VMEM((ns, HIST + sl, C_CONV), F32)],
        compiler_params=pltpu.CompilerParams(
            dimension_semantics=("arbitrary", "arbitrary"),
            vmem_limit_bytes=VMEM_LIMIT_MIXER),
        name="mixer_carry" if carry else "mixer_cache",
    )(x, hist, mem_k, mem_v, *wts)


def _dispatch_kernel(dest_ref, h2_ref, *rest, tile):
    xs_hbm, sem = rest[-2], rest[-1]

    def row_copy(t, k):
        src = h2_ref.at[pl.ds(pl.multiple_of(t * ROW_TILES, ROW_TILES), ROW_TILES)]
        dst = xs_hbm.at[pl.ds(pl.multiple_of(dest_ref[k, t] * ROW_TILES, ROW_TILES), ROW_TILES)]
        return pltpu.make_async_copy(src, dst, sem)

    def start(c, carry):
        for u in range(DMA_UNROLL):
            for k in range(TOP_K):
                row_copy(c * DMA_UNROLL + u, k).start()
        return carry

    lax.fori_loop(0, tile // DMA_UNROLL, start, 0)
    for _ in range(TOP_K):
        pltpu.make_async_copy(h2_ref, xs_hbm.at[pl.ds(0, tile * ROW_TILES)], sem).wait()


def _dispatch(dest, h2_rows, xs, slots):
    t = dest.shape[1]
    tile = min(DISPATCH_TILE, t)
    nt = t // tile
    in_specs = [pl.BlockSpec((TOP_K, tile), lambda i: (0, i), memory_space=pltpu.SMEM),
                pl.BlockSpec((tile * ROW_TILES, LANES), lambda i: (i, 0))]
    args = [dest, h2_rows]
    aliases = {}
    if xs is not None:
        in_specs.append(pl.BlockSpec(memory_space=pl.ANY))
        args.append(xs)
        aliases = {2: 0}
    return pl.pallas_call(
        functools.partial(_dispatch_kernel, tile=tile),
        grid=(nt,),
        in_specs=in_specs,
        out_specs=pl.BlockSpec(memory_space=pl.ANY),
        out_shape=jax.ShapeDtypeStruct((slots * ROW_TILES, LANES), F32),
        scratch_shapes=[pltpu.SemaphoreType.DMA(())],
        input_output_aliases=aliases,
        compiler_params=pltpu.CompilerParams(dimension_semantics=("arbitrary",)),
        name="dispatch",
    )(*args)


def _moe_kernel(be_ref, bv_ref, nb_ref, xs_ref, wgu_ref, bgu_ref, wdn_ref, bdn_ref, ys_ref, wgu16, wdn16):
    i = pl.program_id(0)

    @pl.when(i < nb_ref[0])
    def _():
        prev = be_ref[jnp.maximum(i - 1, 0)]
        new_expert = (i == 0) | (be_ref[i] != prev)

        @pl.when(new_expert)
        def _():
            def cast_rows(c, carry):
                r = pl.multiple_of(c * LANES, LANES)
                wgu16[pl.ds(r, LANES), :] = wgu_ref[pl.ds(r, LANES), :].astype(BF16)
                wdn16[pl.ds(r, LANES), :] = wdn_ref[pl.ds(r, LANES), :].astype(BF16)
                return carry
            lax.fori_loop(0, D_MODEL // LANES, cast_rows, 0)

        row_ok = lax.broadcasted_iota(I32, (MOE_BLOCK, 1), 0) < bv_ref[i]
        xb = jnp.where(row_ok, _tiles_to_rows(xs_ref, MOE_BLOCK), 0.0).astype(BF16)
        gu = jnp.dot(xb, wgu16[...], preferred_element_type=F32) + bgu_ref[0]
        gate = jnp.minimum(gu[:, :D_FF], SWIGLU_LIMIT)
        up = jnp.clip(gu[:, D_FF:], -SWIGLU_LIMIT, SWIGLU_LIMIT)
        act = (up + 1.0) * (gate * _sigmoid(SWIGLU_ALPHA * gate))
        yb = jnp.dot(act.astype(BF16), wdn16[...], preferred_element_type=F32) + bdn_ref[0]
        _rows_to_tiles(ys_ref, yb, MOE_BLOCK)

    @pl.when(i >= nb_ref[0])
    def _():
        ys_ref[...] = jnp.zeros_like(ys_ref)


def _moe(block_expert, block_valid, n_used, xs, w_gu, b_gu, w_dn, b_dn):
    n_blocks = block_expert.shape[0]
    blk = lambda i, be, bv, nb: (jnp.minimum(i, nb[0] - 1), 0)
    wsel = lambda i, be, bv, nb: (be[i], 0, 0)
    grid_spec = pltpu.PrefetchScalarGridSpec(
        num_scalar_prefetch=3,
        grid=(n_blocks,),
        in_specs=[pl.BlockSpec((MOE_BLOCK * ROW_TILES, LANES), blk),
                  pl.BlockSpec((None, D_MODEL, 2 * D_FF), wsel),
                  pl.BlockSpec((None, 1, 2 * D_FF), wsel),
                  pl.BlockSpec((None, D_FF, D_MODEL), wsel),
                  pl.BlockSpec((None, 1, D_MODEL), wsel)],
        out_specs=pl.BlockSpec((MOE_BLOCK * ROW_TILES, LANES), lambda i, be, bv, nb: (i, 0)),
        scratch_shapes=[pltpu.VMEM((D_MODEL, 2 * D_FF), BF16),
                        pltpu.VMEM((D_FF, D_MODEL), BF16)],
    )
    return pl.pallas_call(
        _moe_kernel,
        grid_spec=grid_spec,
        out_shape=jax.ShapeDtypeStruct(xs.shape, F32),
        compiler_params=pltpu.CompilerParams(
            dimension_semantics=("arbitrary",),
            vmem_limit_bytes=VMEM_LIMIT_MOE),
        name="moe",
    )(block_expert, block_valid, n_used, xs, w_gu, b_gu, w_dn, b_dn)


def _combine_kernel(dest_ref, x1_ref, gate_ref, gfin_ref, ys_hbm, out_ref, buf, sem, *, tile):
    def row_copy(t, k):
        src = ys_hbm.at[pl.ds(pl.multiple_of(dest_ref[k, t] * ROW_TILES, ROW_TILES), ROW_TILES)]
        dst = buf.at[k, pl.ds(pl.multiple_of(t * ROW_TILES, ROW_TILES), ROW_TILES)]
        return pltpu.make_async_copy(src, dst, sem)

    def start(c, carry):
        for u in range(DMA_UNROLL):
            for k in range(TOP_K):
                row_copy(c * DMA_UNROLL + u, k).start()
        return carry

    lax.fori_loop(0, tile // DMA_UNROLL, start, 0)
    for k in range(TOP_K):
        pltpu.make_async_copy(ys_hbm.at[pl.ds(0, tile * ROW_TILES)], buf.at[k], sem).wait()

    rr = lax.broadcasted_iota(I32, (tile, tile), 0)
    cc = lax.broadcasted_iota(I32, (tile, tile), 1)
    acc = x1_ref[...]
    for k in range(TOP_K):
        g_col = jnp.sum(jnp.where(rr == cc, gate_ref[k:k + 1, :], 0.0), axis=1, keepdims=True)
        acc = acc + g_col * _tiles_to_rows(buf.at[k], tile)
    out_ref[...] = _rmsnorm(acc, gfin_ref[...])


def _combine(dest, x1, gates, g_final, ys):
    t = x1.shape[0]
    tile = COMBINE_TILE
    nt = t // tile
    return pl.pallas_call(
        functools.partial(_combine_kernel, tile=tile),
        grid=(nt,),
        in_specs=[pl.BlockSpec((TOP_K, tile), lambda i: (0, i), memory_space=pltpu.SMEM),
                  pl.BlockSpec((tile, D_MODEL), lambda i: (i, 0)),
                  pl.BlockSpec((TOP_K, tile), lambda i: (0, i)),
                  pl.BlockSpec((1, D_MODEL), lambda i: (0, 0)),
                  pl.BlockSpec(memory_space=pl.ANY)],
        out_specs=pl.BlockSpec((tile, D_MODEL), lambda i: (i, 0)),
        out_shape=jax.ShapeDtypeStruct((t, D_MODEL), F32),
        scratch_shapes=[pltpu.VMEM((TOP_K, tile * ROW_TILES, LANES), F32),
                        pltpu.SemaphoreType.DMA(())],
        compiler_params=pltpu.CompilerParams(dimension_semantics=("arbitrary",)),
        name="combine",
    )(dest, x1, gates, g_final, ys)


def _split_bf16(w):
    hi = w.astype(BF16)
    lo = (w - hi.astype(F32)).astype(BF16)
    return hi, lo


def kernel(x_prompt, x_sample, cache_conv, cache_mem_k, cache_mem_v, mem_prompt, g_mix, w_in, w_dw, b_dw, ln_conv_g, ln_conv_b, ln_v_g, ln_v_b, w_spatial, b_spatial, g_mem, w_mem_k, w_mem_v, w_out, g_ffn, w_router, b_router, w_gate_up, b_gate_up, w_down, b_down, g_final):
    depth = g_mix.shape[0]
    assert depth == 1
    l = 0
    bp, seq, _ = x_prompt.shape
    bs, dseq, _ = x_sample.shape
    assert seq % PROMPT_TILE == 0 and bs % SAMPLE_SEQS_PER_TILE == 0
    assert GMLP_CHUNK % dseq == 0 and SAMPLE_SEQS_PER_TILE * dseq == GMLP_CHUNK

    row = lambda a: a.reshape(1, -1)
    wr_hi, wr_lo = _split_bf16(w_router[l].T)
    w_dw_pad = jnp.pad(w_dw[l], ((0, HIST - CONV_W), (0, 0)))
    bias_rows = lambda b: jnp.repeat(b.T, GMLP_HD, axis=1)
    common = dict(
        gmix=row(g_mix[l]), win=w_in[l].astype(BF16), wdw=w_dw_pad, bdw=row(b_dw[l]),
        lcg=row(ln_conv_g[l]), lcb=row(ln_conv_b[l]), lvg=row(ln_v_g[l]), lvb=row(ln_v_b[l]),
        wout=w_out[l].astype(BF16), gffn=row(g_ffn[l]), wrh=wr_hi, wrl=wr_lo,
        br=b_router[l].reshape(N_EXPERTS, 1))

    def weights(wsp, bsp):
        c = common
        return (c["gmix"], c["win"], c["wdw"], c["bdw"], c["lcg"], c["lcb"], c["lvg"], c["lvb"],
                wsp, bsp, c["wout"], c["gffn"], c["wrh"], c["wrl"], c["br"])

    reps = GMLP_CHUNK // dseq
    wts_p = weights(w_spatial[l], bias_rows(b_spatial[l]))
    wts_s = weights(jnp.tile(w_spatial[l][:, :dseq, :dseq], (1, reps, reps)),
                    bias_rows(jnp.tile(b_spatial[l][:, :dseq], (1, reps))))

    w_kv = jnp.concatenate([w_mem_k[l], w_mem_v[l]], axis=1).astype(BF16)
    kv_p = _memkv(mem_prompt, row(g_mem[l]), w_kv)
    mk_p = kv_p[:, :, :C_XA]
    mv_p = kv_p[:, :, C_XA:]
    zero_hist = jnp.zeros((bp, 1, HIST, C_CONV), F32)
    (x1_p, h2_p, eidx_p, gate_p, rank_p, cnt_p, hist_p, _) = _mixer(
        x_prompt, zero_hist, mk_p[:, None], mv_p[:, None], wts_p,
        ns=1, sl=PROMPT_TILE, carry=True, sp_chunk=GMLP_CHUNK)

    gs = bs // SAMPLE_SEQS_PER_TILE
    hist_s_in = jnp.pad(cache_conv[l], ((0, 0), (HIST_OFF, 0), (0, 0))).reshape(
        gs, SAMPLE_SEQS_PER_TILE, HIST, C_CONV)
    mk_s = cache_mem_k[l].reshape(gs, SAMPLE_SEQS_PER_TILE, N_MEM, C_XA)
    mv_s = cache_mem_v[l].reshape(gs, SAMPLE_SEQS_PER_TILE, N_MEM, C_XA)
    (x1_s, h2_s, eidx_s, gate_s, rank_s, cnt_s, hist_s, v_s) = _mixer(
        x_sample.reshape(gs, SAMPLE_SEQS_PER_TILE * dseq, D_MODEL), hist_s_in, mk_s, mv_s, wts_s,
        ns=SAMPLE_SEQS_PER_TILE, sl=dseq, carry=False, sp_chunk=dseq)

    tp, ts = bp * seq, bs * dseq
    ttot = tp + ts
    n_assign = ttot * TOP_K
    n_blocks = -(-n_assign // MOE_BLOCK) + N_EXPERTS
    cnt = jnp.concatenate([cnt_p[:, :, 0], cnt_s[:, :, 0]], axis=0)
    counts = jnp.sum(cnt, axis=0)
    tile_base = jnp.cumsum(cnt, axis=0) - cnt
    padded = (counts + MOE_BLOCK - 1) // MOE_BLOCK * MOE_BLOCK
    pad_end = jnp.cumsum(padded)
    pad_start = pad_end - padded
    base = pad_start[None, :] + tile_base
    ntp = tp // PROMPT_TILE
    experts = jnp.arange(N_EXPERTS, dtype=I32)[None, :, None]

    def slots_of(eidx, rank, tile_bases, tile_rows):
        base_tok = jnp.repeat(tile_bases.T, tile_rows, axis=1)
        pick = jnp.where(eidx[:, None, :] == experts, base_tok[None], 0)
        return (jnp.sum(pick, axis=1) + rank).astype(I32)

    dest_p = slots_of(eidx_p, rank_p, base[:ntp], PROMPT_TILE)
    dest_s = slots_of(eidx_s, rank_s, base[ntp:], SAMPLE_SEQS_PER_TILE * dseq)
    n_used = (pad_end[-1] // MOE_BLOCK).astype(I32)
    blk_row0 = jnp.minimum(jnp.arange(n_blocks, dtype=I32), n_used - 1) * MOE_BLOCK
    block_expert = jnp.minimum(
        jnp.sum((pad_end[None, :] <= blk_row0[:, None]).astype(I32), axis=1), N_EXPERTS - 1)
    of_block = block_expert[:, None] == jnp.arange(N_EXPERTS, dtype=I32)[None, :]
    last_row = jnp.sum(jnp.where(of_block, (pad_start + counts)[None, :], 0), axis=1)
    block_valid = jnp.clip(last_row - blk_row0, 0, MOE_BLOCK).astype(I32)

    slots = n_blocks * MOE_BLOCK
    xs = _dispatch(dest_p, h2_p, None, slots)
    xs = _dispatch(dest_s, h2_s, xs, slots)
    ys = _moe(block_expert.astype(I32), block_valid, n_used.reshape(1), xs,
              w_gate_up[l], b_gate_up[l][:, None, :], w_down[l], b_down[l][:, None, :])
    gfin = row(g_final)
    y_p = _combine(dest_p, x1_p, gate_p, gfin, ys)
    y_s = _combine(dest_s, x1_s, gate_s, gfin, ys)

    return (y_p.reshape(bp, seq, D_MODEL),
            y_s.reshape(bs, dseq, D_MODEL),
            hist_p[:, 0, HIST_OFF:, :][None],
            mk_p.reshape(bp, N_MEM, XA_HEADS, XA_HD)[None],
            mv_p.reshape(bp, N_MEM, XA_HEADS, XA_HD)[None],
            hist_s.reshape(bs, HIST, C_CONV)[:, HIST_OFF:, :][None],
            v_s.reshape(bs, dseq, C_GMLP)[None])
```

```python
import functools

import numpy as np
import jax
import jax.numpy as jnp
from jax import lax
from jax.experimental import pallas as pl
from jax.experimental.pallas import tpu as pltpu

F32 = jnp.float32
BF16 = jnp.bfloat16
I32 = jnp.int32

D_MODEL = 1024
C_CONV = 384
CONV_W = 31
C_GMLP = 384
GMLP_HEADS = 4
GMLP_HD = 96
GMLP_CHUNK = 128
XA_HEADS = 4
XA_HD = 64
C_XA = 256
N_MEM = 256
N_EXPERTS = 32
TOP_K = 4
D_FF = 1024
SWIGLU_LIMIT = 7.0
SWIGLU_ALPHA = 1.702
EPS = 1e-5
IN_COLS = 2 * C_CONV + 2 * C_GMLP + C_XA

SUBLANES = 8
LANES = 128
ROW_TILES = D_MODEL // LANES
HIST = 32
HIST_OFF = HIST - (CONV_W - 1)

PROMPT_TILE = 512
SAMPLE_SEQS_PER_TILE = 8
MOE_BLOCK = 512
DISPATCH_TILE = 512
COMBINE_TILE = 128
DMA_UNROLL = 8
VMEM_LIMIT_MIXER = 48 * 1024 * 1024
VMEM_LIMIT_MOE = 52 * 1024 * 1024


def _rmsnorm(x, g):
    return x * lax.rsqrt(jnp.mean(x * x, axis=-1, keepdims=True) + EPS) * g


def _layernorm(x, g, b):
    mu = jnp.mean(x, axis=-1, keepdims=True)
    xc = x - mu
    var = jnp.mean(xc * xc, axis=-1, keepdims=True)
    return xc * lax.rsqrt(var + EPS) * g + b


def _gelu(x):
    return 0.5 * x * (1.0 + lax.erf(x * np.float32(1.0 / np.sqrt(2.0))))


def _sigmoid(x):
    return 1.0 / (1.0 + jnp.exp(-x))


def _rows_to_tiles(dst_ref, val, rows):
    for j in range(ROW_TILES):
        dst_ref[pl.ds(j, rows, stride=ROW_TILES), :] = val[:, j * LANES:(j + 1) * LANES]


def _tiles_to_rows(src_ref, rows):
    return jnp.concatenate(
        [src_ref[pl.ds(j, rows, stride=ROW_TILES), :] for j in range(ROW_TILES)], axis=-1)


def _memkv_kernel(mem_ref, g_ref, w_ref, o_ref):
    mn = _rmsnorm(mem_ref[0], g_ref[...])
    o_ref[0] = jnp.dot(mn.astype(BF16), w_ref[...], preferred_element_type=F32)


def _memkv(mem, g_mem, w_kv):
    b = mem.shape[0]
    return pl.pallas_call(
        _memkv_kernel,
        grid=(b,),
        in_specs=[pl.BlockSpec((1, N_MEM, D_MODEL), lambda i: (i, 0, 0)),
                  pl.BlockSpec((1, D_MODEL), lambda i: (0, 0)),
                  pl.BlockSpec((D_MODEL, 2 * C_XA), lambda i: (0, 0))],
        out_specs=pl.BlockSpec((1, N_MEM, 2 * C_XA), lambda i: (i, 0, 0)),
        out_shape=jax.ShapeDtypeStruct((b, N_MEM, 2 * C_XA), F32),
        name="memkv",
    )(mem, g_mem, w_kv)


def _mixer_kernel(x_ref, hist_ref, mk_ref, mv_ref, gmix_ref, win_ref, wdw_ref, bdw_ref,
                  lcg_ref, lcb_ref, lvg_ref, lvb_ref, wsp_ref, bsp_ref, wout_ref, gffn_ref,
                  wrh_ref, wrl_ref, br_ref, *rest, ns, sl, carry, sp_chunk):
    x1_ref, xst_ref, pos_ref, gate_ref, cnt_ref, off_ref, histout_ref, v_ref, ext_ref = rest[-9:]
    tm = ns * sl
    i = pl.program_id(1)

    x = x_ref[0]
    h = _rmsnorm(x, gmix_ref[...])
    z = jnp.dot(h.astype(BF16), win_ref[...], preferred_element_type=F32)
    z_a = z[:, 0:C_CONV]
    z_g = z[:, C_CONV:2 * C_CONV]
    z_u = z[:, 2 * C_CONV:2 * C_CONV + C_GMLP]
    z_v = z[:, 2 * C_CONV + C_GMLP:2 * C_CONV + 2 * C_GMLP]
    z_q = z[:, 2 * C_CONV + 2 * C_GMLP:IN_COLS]

    glu = z_a * _sigmoid(z_g)
    if carry:
        @pl.when(i == 0)
        def _():
            ext_ref[:, 0:HIST, :] = hist_ref[0]
    else:
        ext_ref[:, 0:HIST, :] = hist_ref[0]
    conv_parts = []
    for s in range(ns):
        ext_s = ext_ref.at[s]
        ext_s[HIST:HIST + sl, :] = glu[s * sl:(s + 1) * sl]
        rc = min(sl, 64)
        for r0 in range(0, sl, rc):
            acc = jnp.broadcast_to(bdw_ref[...], (rc, C_CONV))
            for j in range(CONV_W):
                acc = acc + wdw_ref[j:j + 1, :] * ext_s[pl.ds(r0 + j + HIST_OFF, rc), :]
            conv_parts.append(acc)
        new_hist = ext_s[sl:sl + HIST, :]
        histout_ref[0, s] = new_hist
        if carry:
            ext_s[0:HIST, :] = new_hist
    y = jnp.concatenate(conv_parts, axis=0) if len(conv_parts) > 1 else conv_parts[0]
    y = _layernorm(y, lcg_ref[...], lcb_ref[...])
    c_out = y * _sigmoid(y)

    u = _gelu(z_u)
    v = _layernorm(_gelu(z_v), lvg_ref[...], lvb_ref[...])
    v_ref[...] = v
    vb = v.astype(BF16)
    rr = lax.broadcasted_iota(I32, (GMLP_CHUNK, GMLP_CHUNK), 0)
    cc = lax.broadcasted_iota(I32, (GMLP_CHUNK, GMLP_CHUNK), 1)
    sp_mask = (cc <= rr) & ((rr // sp_chunk) == (cc // sp_chunk))
    col = lax.broadcasted_iota(I32, (GMLP_CHUNK, C_GMLP), 1)
    w_heads = [jnp.where(sp_mask, wsp_ref[hh], 0.0).astype(BF16) for hh in range(GMLP_HEADS)]
    g_parts = []
    for c in range(tm // GMLP_CHUNK):
        vc = vb[c * GMLP_CHUNK:(c + 1) * GMLP_CHUNK]
        sg = bsp_ref[...]
        for hh in range(GMLP_HEADS):
            head_cols = (col >= hh * GMLP_HD) & (col < (hh + 1) * GMLP_HD)
            vh = jnp.where(head_cols, vc, jnp.zeros_like(vc))
            sg = sg + jnp.dot(w_heads[hh], vh, preferred_element_type=F32)
        g_parts.append(u[c * GMLP_CHUNK:(c + 1) * GMLP_CHUNK] * sg)
    g_out = jnp.concatenate(g_parts, axis=0) if len(g_parts) > 1 else g_parts[0]

    qs = z_q * np.float32(XA_HD ** -0.5)
    qcol = lax.broadcasted_iota(I32, (sl, C_XA), 1)
    a_parts = []
    for s in range(ns):
        q_s = qs[s * sl:(s + 1) * sl]
        kb = mk_ref[0, s].astype(BF16)
        vvb = mv_ref[0, s].astype(BF16)
        a_s = jnp.zeros((sl, C_XA), F32)
        for hh in range(XA_HEADS):
            hmask = (qcol >= hh * XA_HD) & (qcol < (hh + 1) * XA_HD)
            qh = jnp.where(hmask, q_s, 0.0).astype(BF16)
            sc = lax.dot_general(qh, kb, (((1,), (1,)), ((), ())), preferred_element_type=F32)
            p = jnp.exp(sc - jnp.max(sc, axis=-1, keepdims=True))
            den = jnp.sum(p, axis=-1, keepdims=True)
            oh = jnp.dot(p.astype(BF16), vvb, preferred_element_type=F32) / den
            a_s = a_s + jnp.where(hmask, oh, 0.0)
        a_parts.append(a_s)
    a_out = jnp.concatenate(a_parts, axis=0) if len(a_parts) > 1 else a_parts[0]

    mix = jnp.concatenate([c_out, g_out, a_out], axis=-1).astype(BF16)
    x1 = x + jnp.dot(mix, wout_ref[...], preferred_element_type=F32)
    x1_ref[...] = x1

    h2 = _rmsnorm(x1, gffn_ref[...])
    h2_hi = h2.astype(BF16)
    h2_lo = (h2 - h2_hi.astype(F32)).astype(BF16)
    nt_dims = (((1,), (1,)), ((), ()))
    lg = (lax.dot_general(wrh_ref[...], h2_hi, nt_dims, preferred_element_type=F32)
          + lax.dot_general(wrl_ref[...], h2_hi, nt_dims, preferred_element_type=F32)
          + lax.dot_general(wrh_ref[...], h2_lo, nt_dims, preferred_element_type=F32)
          + br_ref[...])
    eio = lax.broadcasted_iota(I32, (N_EXPERTS, tm), 0)
    work = lg
    vals, idxs = [], []
    for _ in range(TOP_K):
        m = jnp.max(work, axis=0, keepdims=True)
        idx = jnp.min(jnp.where(work == m, eio, N_EXPERTS), axis=0, keepdims=True)
        vals.append(m)
        idxs.append(idx)
        work = jnp.where(eio == idx, -jnp.inf, work)
    exps = [jnp.exp(vk - vals[0]) for vk in vals]
    den = exps[0] + exps[1] + exps[2] + exps[3]
    gate_ref[...] = jnp.concatenate([ek / den for ek in exps], axis=0)

    sel = jnp.zeros((N_EXPERTS, tm), F32)
    for idx in idxs:
        sel = sel + jnp.where(eio == idx, 1.0, 0.0)
    selb = sel.astype(BF16)
    tr = lax.broadcasted_iota(I32, (tm, tm), 0)
    tc = lax.broadcasted_iota(I32, (tm, tm), 1)
    before = jnp.where(tr < tc, 1.0, 0.0).astype(BF16)
    ranks = jnp.dot(selb, before, preferred_element_type=F32)
    er = lax.broadcasted_iota(I32, (N_EXPERTS, N_EXPERTS), 0)
    ec = lax.broadcasted_iota(I32, (N_EXPERTS, N_EXPERTS), 1)
    lower = jnp.where(ec < er, 1.0, 0.0).astype(BF16)
    cnt = jnp.sum(sel, axis=1, keepdims=True)
    off = jnp.sum(jnp.dot(lower, selb, preferred_element_type=F32), axis=1, keepdims=True)
    cnt_ref[0] = jnp.broadcast_to(cnt, (N_EXPERTS, LANES)).astype(I32)
    off_ref[0] = jnp.broadcast_to(off, (N_EXPERTS, LANES)).astype(I32)
    slot_of = off + ranks
    pos = [jnp.sum(jnp.where(eio == idx, slot_of, 0.0), axis=0, keepdims=True).astype(I32) for idx in idxs]
    pos_ref[...] = jnp.concatenate(pos, axis=0)

    jj = lax.broadcasted_iota(I32, (TOP_K * tm, tm), 0)
    perm = jnp.zeros((TOP_K * tm, tm), F32)
    for pk in pos:
        perm = jnp.where(jj == pk, 1.0, perm)
    sorted_rows = jnp.dot(perm.astype(BF16), h2_hi, preferred_element_type=F32)
    _rows_to_tiles(xst_ref, sorted_rows, TOP_K * tm)


def _mixer(x, hist, mem_k, mem_v, wts, *, ns, sl, carry, sp_chunk, sorted_rows_total, sorted_row0, xs_prev):
    g, r, _ = x.shape
    tm = ns * sl
    nt = r // tm
    ntot = g * nt
    assert sorted_row0 % (TOP_K * tm) == 0
    xs_blk0 = sorted_row0 // (TOP_K * tm)
    const2 = lambda b, i: (0, 0)
    const3 = lambda b, i: (0, 0, 0)
    tile_row = lambda b, i: (b * nt + i, 0)
    tile_lane = lambda b, i: (0, b * nt + i)
    in_specs = [
        pl.BlockSpec((1, tm, D_MODEL), lambda b, i: (b, i, 0)),
        pl.BlockSpec((1, ns, HIST, C_CONV), lambda b, i: (b, 0, 0, 0)),
        pl.BlockSpec((1, ns, N_MEM, C_XA), lambda b, i: (b, 0, 0, 0)),
        pl.BlockSpec((1, ns, N_MEM, C_XA), lambda b, i: (b, 0, 0, 0)),
        pl.BlockSpec((1, D_MODEL), const2),
        pl.BlockSpec((D_MODEL, IN_COLS), const2),
        pl.BlockSpec((HIST, C_CONV), const2),
        pl.BlockSpec((1, C_CONV), const2),
        pl.BlockSpec((1, C_CONV), const2),
        pl.BlockSpec((1, C_CONV), const2),
        pl.BlockSpec((1, C_GMLP), const2),
        pl.BlockSpec((1, C_GMLP), const2),
        pl.BlockSpec((GMLP_HEADS, GMLP_CHUNK, GMLP_CHUNK), const3),
        pl.BlockSpec((GMLP_CHUNK, C_GMLP), const2),
        pl.BlockSpec((D_MODEL, D_MODEL), const2),
        pl.BlockSpec((1, D_MODEL), const2),
        pl.BlockSpec((N_EXPERTS, D_MODEL), const2),
        pl.BlockSpec((N_EXPERTS, D_MODEL), const2),
        pl.BlockSpec((N_EXPERTS, 1), const2),
    ]
    tile_cnt = lambda b, i: (b * nt + i, 0, 0)
    out_specs = [
        pl.BlockSpec((tm, D_MODEL), tile_row),
        pl.BlockSpec((TOP_K * tm * ROW_TILES, LANES),
                     lambda b, i: (xs_blk0 + b * nt + i, 0)),
        pl.BlockSpec((TOP_K, tm), tile_lane),
        pl.BlockSpec((TOP_K, tm), tile_lane),
        pl.BlockSpec((1, N_EXPERTS, LANES), tile_cnt),
        pl.BlockSpec((1, N_EXPERTS, LANES), tile_cnt),
        pl.BlockSpec((1, ns, HIST, C_CONV), lambda b, i: (b, 0, 0, 0)),
        pl.BlockSpec((tm, C_GMLP), tile_row),
    ]
    rows = g * r
    out_shape = [
        jax.ShapeDtypeStruct((rows, D_MODEL), F32),
        jax.ShapeDtypeStruct((sorted_rows_total * ROW_TILES, LANES), F32),
        jax.ShapeDtypeStruct((TOP_K, rows), I32),
        jax.ShapeDtypeStruct((TOP_K, rows), F32),
        jax.ShapeDtypeStruct((ntot, N_EXPERTS, LANES), I32),
        jax.ShapeDtypeStruct((ntot, N_EXPERTS, LANES), I32),
        jax.ShapeDtypeStruct((g, ns, HIST, C_CONV), F32),
        jax.ShapeDtypeStruct((rows, C_GMLP), F32),
    ]
    args = [x, hist, mem_k, mem_v, *wts]
    aliases = {}
    if xs_prev is not None:
        in_specs.append(pl.BlockSpec(memory_space=pl.ANY))
        aliases = {len(args): 1}
        args.append(xs_prev)
    kern = functools.partial(_mixer_kernel, ns=ns, sl=sl, carry=carry, sp_chunk=sp_chunk)
    return pl.pallas_call(
        kern,
        grid=(g, nt),
        in_specs=in_specs,
        out_specs=out_specs,
        out_shape=out_shape,
        scratch_shapes=[pltpu.VMEM((ns, HIST + sl, C_CONV), F32)],
        input_output_aliases=aliases,
        compiler_params=pltpu.CompilerParams(
            dimension_semantics=("arbitrary", "arbitrary"),
            vmem_limit_bytes=VMEM_LIMIT_MIXER),
        name="mixer_carry" if carry else "mixer_cache",
    )(*args)


def _strip_pieces(n, max_rows, fn):
    done = 0
    p = max_rows
    while p >= 1:
        has = (n & p) != 0
        pl.when(has)(functools.partial(fn, done, p))
        done = done + jnp.where(has, p, 0)
        p //= 2


def _rows(ref, first_row, rows):
    return ref.at[pl.ds(pl.multiple_of(first_row * ROW_TILES, ROW_TILES), rows * ROW_TILES)]


def _moe_kernel(be_ref, bv_ref, nb_ref, slo_ref, shi_ref, ssrc_ref, sdst_ref, slen_ref,
                xs_hbm, wgu_ref, bgu_ref, wdn_ref, bdn_ref, ys_ref, xbuf, wgu16, wdn16, sems):
    i = pl.program_id(0)
    n_used = nb_ref[0]

    def gather_start(b, slot):
        row0 = b * MOE_BLOCK

        def strip(s, carry):
            lo = jnp.maximum(sdst_ref[s], row0)
            hi = jnp.minimum(sdst_ref[s] + slen_ref[s], row0 + MOE_BLOCK)
            src0 = ssrc_ref[s] + (lo - sdst_ref[s])
            dst0 = lo - row0

            def piece(first, rows):
                pltpu.make_async_copy(_rows(xs_hbm, src0 + first, rows),
                                      _rows(xbuf.at[slot], dst0 + first, rows), sems.at[slot]).start()
            _strip_pieces(hi - lo, MOE_BLOCK, piece)
            return carry
        lax.fori_loop(slo_ref[b], shi_ref[b], strip, 0)

    def gather_wait(b, slot):
        def piece(first, rows):
            del first
            pltpu.make_async_copy(_rows(xs_hbm, 0, rows), _rows(xbuf.at[slot], 0, rows), sems.at[slot]).wait()
        _strip_pieces(bv_ref[b], MOE_BLOCK, piece)

    @pl.when(i == 0)
    def _():
        gather_start(0, 0)

    @pl.when(i + 1 < n_used)
    def _():
        gather_start(i + 1, (i + 1) % 2)

    @pl.when(i < n_used)
    def _():
        slot = i % 2
        gather_wait(i, slot)
        xs_ref = xbuf.at[slot]
        prev = be_ref[jnp.maximum(i - 1, 0)]
        new_expert = (i == 0) | (be_ref[i] != prev)

        @pl.when(new_expert)
        def _():
            def cast_rows(c, carry):
                r = pl.multiple_of(c * LANES, LANES)
                wgu16[pl.ds(r, LANES), :] = wgu_ref[pl.ds(r, LANES), :].astype(BF16)
                wdn16[pl.ds(r, LANES), :] = wdn_ref[pl.ds(r, LANES), :].astype(BF16)
                return carry
            lax.fori_loop(0, D_MODEL // LANES, cast_rows, 0)

        row_ok = lax.broadcasted_iota(I32, (MOE_BLOCK, 1), 0) < bv_ref[i]
        xb = jnp.where(row_ok, _tiles_to_rows(xs_ref, MOE_BLOCK), 0.0).astype(BF16)
        gu = jnp.dot(xb, wgu16[...], preferred_element_type=F32) + bgu_ref[0]
        gate = jnp.minimum(gu[:, :D_FF], SWIGLU_LIMIT)
        up = jnp.clip(gu[:, D_FF:], -SWIGLU_LIMIT, SWIGLU_LIMIT)
        act = (up + 1.0) * (gate * _sigmoid(SWIGLU_ALPHA * gate))
        yb = jnp.dot(act.astype(BF16), wdn16[...], preferred_element_type=F32) + bdn_ref[0]
        _rows_to_tiles(ys_ref, yb, MOE_BLOCK)

    @pl.when(i >= nb_ref[0])
    def _():
        ys_ref[...] = jnp.zeros_like(ys_ref)


def _moe(tables, xs, slots, w_gu, b_gu, w_dn, b_dn):
    n_blocks = tables[0].shape[0]
    wsel = lambda i, be, *_: (be[i], 0, 0)
    grid_spec = pltpu.PrefetchScalarGridSpec(
        num_scalar_prefetch=len(tables),
        grid=(n_blocks,),
        in_specs=[pl.BlockSpec(memory_space=pl.ANY),
                  pl.BlockSpec((None, D_MODEL, 2 * D_FF), wsel),
                  pl.BlockSpec((None, 1, 2 * D_FF), wsel),
                  pl.BlockSpec((None, D_FF, D_MODEL), wsel),
                  pl.BlockSpec((None, 1, D_MODEL), wsel)],
        out_specs=pl.BlockSpec((MOE_BLOCK * ROW_TILES, LANES), lambda i, *_: (i, 0)),
        scratch_shapes=[pltpu.VMEM((2, MOE_BLOCK * ROW_TILES, LANES), F32),
                        pltpu.VMEM((D_MODEL, 2 * D_FF), BF16),
                        pltpu.VMEM((D_FF, D_MODEL), BF16),
                        pltpu.SemaphoreType.DMA((2,))],
    )
    return pl.pallas_call(
        _moe_kernel,
        grid_spec=grid_spec,
        out_shape=jax.ShapeDtypeStruct((slots * ROW_TILES, LANES), F32),
        compiler_params=pltpu.CompilerParams(
            dimension_semantics=("arbitrary",),
            vmem_limit_bytes=VMEM_LIMIT_MOE),
        name="moe",
    )(*tables, xs, w_gu, b_gu, w_dn, b_dn)


def _combine_kernel(csrc_ref, clen_ref, coff_ref, x1_ref, pos_ref, gate_ref, gfin_ref, ys_hbm,
                    out_ref, ybuf, sems, *, tm):
    i = pl.program_id(0)
    n_tiles = pl.num_programs(0)
    n_sorted = TOP_K * tm

    def gather_start(t, slot):
        def strip(e, carry):
            s = t * N_EXPERTS + e
            src0 = csrc_ref[s]
            dst0 = coff_ref[s]

            def piece(first, rows):
                pltpu.make_async_copy(_rows(ys_hbm, src0 + first, rows),
                                      _rows(ybuf.at[slot], dst0 + first, rows), sems.at[slot]).start()
            _strip_pieces(clen_ref[s], tm, piece)
            return carry
        lax.fori_loop(0, N_EXPERTS, strip, 0)

    @pl.when(i == 0)
    def _():
        gather_start(0, 0)

    @pl.when(i + 1 < n_tiles)
    def _():
        gather_start(i + 1, (i + 1) % 2)

    slot = i % 2
    pltpu.make_async_copy(_rows(ys_hbm, 0, n_sorted), ybuf.at[slot], sems.at[slot]).wait()
    y_sorted = _tiles_to_rows(ybuf.at[slot], n_sorted).astype(BF16)

    rr = lax.broadcasted_iota(I32, (tm, tm), 0)
    cc = lax.broadcasted_iota(I32, (tm, tm), 1)
    eye = rr == cc
    jl = lax.broadcasted_iota(I32, (tm, n_sorted), 1).astype(F32)
    unsort = jnp.zeros((tm, n_sorted), F32)
    for k in range(TOP_K):
        p_col = jnp.sum(jnp.where(eye, pos_ref[k:k + 1, :].astype(F32), 0.0), axis=1, keepdims=True)
        g_col = jnp.sum(jnp.where(eye, gate_ref[k:k + 1, :], 0.0), axis=1, keepdims=True)
        unsort = jnp.where(jl == p_col, g_col, unsort)
    acc = x1_ref[...] + jnp.dot(unsort.astype(BF16), y_sorted, preferred_element_type=F32)
    out_ref[...] = _rmsnorm(acc, gfin_ref[...])


def _combine(tables, x1, pos, gates, g_final, ys, tm):
    t = x1.shape[0]
    nt = t // tm
    grid_spec = pltpu.PrefetchScalarGridSpec(
        num_scalar_prefetch=len(tables),
        grid=(nt,),
        in_specs=[pl.BlockSpec((tm, D_MODEL), lambda i, *_: (i, 0)),
                  pl.BlockSpec((TOP_K, tm), lambda i, *_: (0, i)),
                  pl.BlockSpec((TOP_K, tm), lambda i, *_: (0, i)),
                  pl.BlockSpec((1, D_MODEL), lambda i, *_: (0, 0)),
                  pl.BlockSpec(memory_space=pl.ANY)],
        out_specs=pl.BlockSpec((tm, D_MODEL), lambda i, *_: (i, 0)),
        scratch_shapes=[pltpu.VMEM((2, TOP_K * tm * ROW_TILES, LANES), F32),
                        pltpu.SemaphoreType.DMA((2,))],
    )
    return pl.pallas_call(
        functools.partial(_combine_kernel, tm=tm),
        grid_spec=grid_spec,
        out_shape=jax.ShapeDtypeStruct((t, D_MODEL), F32),
        compiler_params=pltpu.CompilerParams(
            dimension_semantics=("arbitrary",),
            vmem_limit_bytes=VMEM_LIMIT_MIXER),
        name="combine",
    )(*tables, x1, pos, gates, g_final, ys)


def _split_bf16(w):
    hi = w.astype(BF16)
    lo = (w - hi.astype(F32)).astype(BF16)
    return hi, lo


def kernel(x_prompt, x_sample, cache_conv, cache_mem_k, cache_mem_v, mem_prompt, g_mix, w_in, w_dw, b_dw, ln_conv_g, ln_conv_b, ln_v_g, ln_v_b, w_spatial, b_spatial, g_mem, w_mem_k, w_mem_v, w_out, g_ffn, w_router, b_router, w_gate_up, b_gate_up, w_down, b_down, g_final):
    depth = g_mix.shape[0]
    assert depth == 1
    l = 0
    bp, seq, _ = x_prompt.shape
    bs, dseq, _ = x_sample.shape
    assert seq % PROMPT_TILE == 0 and bs % SAMPLE_SEQS_PER_TILE == 0
    assert GMLP_CHUNK % dseq == 0 and SAMPLE_SEQS_PER_TILE * dseq == GMLP_CHUNK

    row = lambda a: a.reshape(1, -1)
    wr_hi, wr_lo = _split_bf16(w_router[l].T)
    w_dw_pad = jnp.pad(w_dw[l], ((0, HIST - CONV_W), (0, 0)))
    bias_rows = lambda b: jnp.repeat(b.T, GMLP_HD, axis=1)
    common = dict(
        gmix=row(g_mix[l]), win=w_in[l].astype(BF16), wdw=w_dw_pad, bdw=row(b_dw[l]),
        lcg=row(ln_conv_g[l]), lcb=row(ln_conv_b[l]), lvg=row(ln_v_g[l]), lvb=row(ln_v_b[l]),
        wout=w_out[l].astype(BF16), gffn=row(g_ffn[l]), wrh=wr_hi, wrl=wr_lo,
        br=b_router[l].reshape(N_EXPERTS, 1))

    def weights(wsp, bsp):
        c = common
        return (c["gmix"], c["win"], c["wdw"], c["bdw"], c["lcg"], c["lcb"], c["lvg"], c["lvb"],
                wsp, bsp, c["wout"], c["gffn"], c["wrh"], c["wrl"], c["br"])

    reps = GMLP_CHUNK // dseq
    wts_p = weights(w_spatial[l], bias_rows(b_spatial[l]))
    wts_s = weights(jnp.tile(w_spatial[l][:, :dseq, :dseq], (1, reps, reps)),
                    bias_rows(jnp.tile(b_spatial[l][:, :dseq], (1, reps))))

    w_kv = jnp.concatenate([w_mem_k[l], w_mem_v[l]], axis=1).astype(BF16)
    kv_p = _memkv(mem_prompt, row(g_mem[l]), w_kv)
    mk_p = kv_p[:, :, :C_XA]
    mv_p = kv_p[:, :, C_XA:]
    zero_hist = jnp.zeros((bp, 1, HIST, C_CONV), F32)
    tp, ts = bp * seq, bs * dseq
    n_assign = (tp + ts) * TOP_K
    (x1_p, xs, pos_p, gate_p, cnt_p, off_p, hist_p, _) = _mixer(
        x_prompt, zero_hist, mk_p[:, None], mv_p[:, None], wts_p,
        ns=1, sl=PROMPT_TILE, carry=True, sp_chunk=GMLP_CHUNK,
        sorted_rows_total=n_assign, sorted_row0=0, xs_prev=None)

    gs = bs // SAMPLE_SEQS_PER_TILE
    hist_s_in = jnp.pad(cache_conv[l], ((0, 0), (HIST_OFF, 0), (0, 0))).reshape(
        gs, SAMPLE_SEQS_PER_TILE, HIST, C_CONV)
    mk_s = cache_mem_k[l].reshape(gs, SAMPLE_SEQS_PER_TILE, N_MEM, C_XA)
    mv_s = cache_mem_v[l].reshape(gs, SAMPLE_SEQS_PER_TILE, N_MEM, C_XA)
    tm_s = SAMPLE_SEQS_PER_TILE * dseq
    (x1_s, xs, pos_s, gate_s, cnt_s, off_s, hist_s, v_s) = _mixer(
        x_sample.reshape(gs, tm_s, D_MODEL), hist_s_in, mk_s, mv_s, wts_s,
        ns=SAMPLE_SEQS_PER_TILE, sl=dseq, carry=False, sp_chunk=dseq,
        sorted_rows_total=n_assign, sorted_row0=tp * TOP_K, xs_prev=xs)

    n_blocks = -(-n_assign // MOE_BLOCK) + N_EXPERTS
    ntp, nts = tp // PROMPT_TILE, ts // tm_s
    cnt = jnp.concatenate([cnt_p[:, :, 0], cnt_s[:, :, 0]], axis=0)
    off = jnp.concatenate([off_p[:, :, 0], off_s[:, :, 0]], axis=0)
    tile_row0 = np.concatenate([np.arange(ntp) * (TOP_K * PROMPT_TILE),
                                tp * TOP_K + np.arange(nts) * (TOP_K * tm_s)]).astype(np.int32)
    counts = jnp.sum(cnt, axis=0)
    tile_base = jnp.cumsum(cnt, axis=0) - cnt
    padded = (counts + MOE_BLOCK - 1) // MOE_BLOCK * MOE_BLOCK
    pad_end = jnp.cumsum(padded)
    pad_start = pad_end - padded
    strip_dst = (pad_start[None, :] + tile_base).astype(I32)
    strip_src = (tile_row0[:, None] + off).astype(I32)
    n_used = (pad_end[-1] // MOE_BLOCK).astype(I32)
    blk_row0 = jnp.minimum(jnp.arange(n_blocks, dtype=I32), n_used - 1) * MOE_BLOCK
    block_expert = jnp.minimum(
        jnp.sum((pad_end[None, :] <= blk_row0[:, None]).astype(I32), axis=1), N_EXPERTS - 1)
    of_block = block_expert[:, None] == jnp.arange(N_EXPERTS, dtype=I32)[None, :]
    last_row = jnp.sum(jnp.where(of_block, (pad_start + counts)[None, :], 0), axis=1)
    block_valid = jnp.clip(last_row - blk_row0, 0, MOE_BLOCK).astype(I32)
    sdst = strip_dst.T.reshape(-1)
    ssrc = strip_src.T.reshape(-1)
    slen = cnt.T.reshape(-1).astype(I32)
    s_lo = jnp.sum(((sdst + slen)[None, :] <= blk_row0[:, None]).astype(I32), axis=1)
    s_hi = jnp.sum((sdst[None, :] < (blk_row0 + MOE_BLOCK)[:, None]).astype(I32), axis=1)
    moe_tables = (block_expert.astype(I32), block_valid, n_used.reshape(1),
                  s_lo.astype(I32), s_hi.astype(I32), ssrc, sdst, slen)

    slots = n_blocks * MOE_BLOCK
    ys = _moe(moe_tables, xs, slots,
              w_gate_up[l], b_gate_up[l][:, None, :], w_down[l], b_down[l][:, None, :])
    gfin = row(g_final)

    def combine_tables(t0, t1):
        return (strip_dst[t0:t1].reshape(-1), cnt[t0:t1].reshape(-1).astype(I32),
                off[t0:t1].reshape(-1).astype(I32))

    y_p = _combine(combine_tables(0, ntp), x1_p, pos_p, gate_p, gfin, ys, PROMPT_TILE)
    y_s = _combine(combine_tables(ntp, ntp + nts), x1_s, pos_s, gate_s, gfin, ys, tm_s)

    return (y_p.reshape(bp, seq, D_MODEL),
            y_s.reshape(bs, dseq, D_MODEL),
            hist_p[:, 0, HIST_OFF:, :][None],
            mk_p.reshape(bp, N_MEM, XA_HEADS, XA_HD)[None],
            mv_p.reshape(bp, N_MEM, XA_HEADS, XA_HD)[None],
            hist_s.reshape(bs, HIST, C_CONV)[:, HIST_OFF:, :][None],
            v_s.reshape(bs, dseq, C_GMLP)[None])
```

```python
import functools

import numpy as np
import jax
import jax.numpy as jnp
from jax import lax
from jax.experimental import pallas as pl
from jax.experimental.pallas import tpu as pltpu

F32 = jnp.float32
BF16 = jnp.bfloat16
I32 = jnp.int32

D_MODEL = 1024
C_CONV = 384
CONV_W = 31
C_GMLP = 384
GMLP_HEADS = 4
GMLP_HD = 96
GMLP_CHUNK = 128
XA_HEADS = 4
XA_HD = 64
C_XA = 256
N_MEM = 256
N_EXPERTS = 32
TOP_K = 4
D_FF = 1024
SWIGLU_LIMIT = 7.0
SWIGLU_ALPHA = 1.702
EPS = 1e-5
IN_COLS = 2 * C_CONV + 2 * C_GMLP + C_XA

SUBLANES = 8
LANES = 128
ROW_TILES = D_MODEL // LANES
HIST = 32
HIST_OFF = HIST - (CONV_W - 1)

PROMPT_TILE = 512
SAMPLE_SEQS_PER_TILE = 8
MOE_BLOCK = 512
DISPATCH_TILE = 512
COMBINE_TILE = 128
DMA_UNROLL = 8
VMEM_LIMIT_MIXER = 48 * 1024 * 1024
VMEM_LIMIT_MOE = 52 * 1024 * 1024


def _rmsnorm(x, g):
    return x * lax.rsqrt(jnp.mean(x * x, axis=-1, keepdims=True) + EPS) * g


def _layernorm(x, g, b):
    mu = jnp.mean(x, axis=-1, keepdims=True)
    xc = x - mu
    var = jnp.mean(xc * xc, axis=-1, keepdims=True)
    return xc * lax.rsqrt(var + EPS) * g + b


def _gelu(x):
    return 0.5 * x * (1.0 + lax.erf(x * np.float32(1.0 / np.sqrt(2.0))))


def _sigmoid(x):
    return 1.0 / (1.0 + jnp.exp(-x))


def _rows_to_tiles(dst_ref, val, rows, row0=0):
    for j in range(ROW_TILES):
        dst_ref[pl.ds(row0 * ROW_TILES + j, rows, stride=ROW_TILES), :] = val[:, j * LANES:(j + 1) * LANES]


def _tiles_to_rows(src_ref, rows, row0=0):
    return jnp.concatenate(
        [src_ref[pl.ds(row0 * ROW_TILES + j, rows, stride=ROW_TILES), :] for j in range(ROW_TILES)],
        axis=-1)


def _memkv_kernel(mem_ref, g_ref, w_ref, o_ref):
    mn = _rmsnorm(mem_ref[0], g_ref[...])
    o_ref[0] = jnp.dot(mn.astype(BF16), w_ref[...], preferred_element_type=F32)


def _memkv(mem, g_mem, w_kv):
    b = mem.shape[0]
    return pl.pallas_call(
        _memkv_kernel,
        grid=(b,),
        in_specs=[pl.BlockSpec((1, N_MEM, D_MODEL), lambda i: (i, 0, 0)),
                  pl.BlockSpec((1, D_MODEL), lambda i: (0, 0)),
                  pl.BlockSpec((D_MODEL, 2 * C_XA), lambda i: (0, 0))],
        out_specs=pl.BlockSpec((1, N_MEM, 2 * C_XA), lambda i: (i, 0, 0)),
        out_shape=jax.ShapeDtypeStruct((b, N_MEM, 2 * C_XA), F32),
        name="memkv",
    )(mem, g_mem, w_kv)


def _mixer_kernel(x_ref, hist_ref, mk_ref, mv_ref, gmix_ref, win_ref, wdw_ref, bdw_ref,
                  lcg_ref, lcb_ref, lvg_ref, lvb_ref, wsp_ref, bsp_ref, wout_ref, gffn_ref,
                  wrh_ref, wrl_ref, br_ref, *rest, ns, sl, carry, sp_chunk):
    (x1_ref, xst_ref, pos_ref, gate_ref, cnt_ref, off_ref, histout_ref, v_ref,
     ext_ref, shift_ref) = rest[-10:]
    tm = ns * sl
    i = pl.program_id(1)

    x = x_ref[0]
    h = _rmsnorm(x, gmix_ref[...])
    z = jnp.dot(h.astype(BF16), win_ref[...], preferred_element_type=F32)
    z_a = z[:, 0:C_CONV]
    z_g = z[:, C_CONV:2 * C_CONV]
    z_u = z[:, 2 * C_CONV:2 * C_CONV + C_GMLP]
    z_v = z[:, 2 * C_CONV + C_GMLP:2 * C_CONV + 2 * C_GMLP]
    z_q = z[:, 2 * C_CONV + 2 * C_GMLP:IN_COLS]

    glu = z_a * _sigmoid(z_g)
    if carry:
        @pl.when(i == 0)
        def _():
            ext_ref[:, 0:HIST, :] = hist_ref[0]
    else:
        ext_ref[:, 0:HIST, :] = hist_ref[0]
    conv_parts = []
    for s in range(ns):
        ext_s = ext_ref.at[s]
        ext_s[HIST:HIST + sl, :] = glu[s * sl:(s + 1) * sl]
        n_shift = HIST + sl - SUBLANES
        for r in range(1, SUBLANES):
            shift_ref[s, r - 1, 0:n_shift, :] = ext_s[pl.ds(r, n_shift), :]
        rc = min(sl, 64)
        for r0 in range(0, sl, rc):
            acc = jnp.broadcast_to(bdw_ref[...], (rc, C_CONV))
            for j in range(CONV_W):
                a, r = divmod(j + HIST_OFF, SUBLANES)
                src = ext_s if r == 0 else shift_ref.at[s, r - 1]
                acc = acc + wdw_ref[j:j + 1, :] * src[pl.ds(r0 + a * SUBLANES, rc), :]
            conv_parts.append(acc)
        new_hist = ext_s[sl:sl + HIST, :]
        histout_ref[0, s] = new_hist
        if carry:
            ext_s[0:HIST, :] = new_hist
    y = jnp.concatenate(conv_parts, axis=0) if len(conv_parts) > 1 else conv_parts[0]
    y = _layernorm(y, lcg_ref[...], lcb_ref[...])
    c_out = y * _sigmoid(y)

    u = _gelu(z_u)
    v = _layernorm(_gelu(z_v), lvg_ref[...], lvb_ref[...])
    v_ref[...] = v
    vb = v.astype(BF16)
    rr = lax.broadcasted_iota(I32, (GMLP_CHUNK, GMLP_CHUNK), 0)
    cc = lax.broadcasted_iota(I32, (GMLP_CHUNK, GMLP_CHUNK), 1)
    sp_mask = (cc <= rr) & ((rr // sp_chunk) == (cc // sp_chunk))
    col = lax.broadcasted_iota(I32, (GMLP_CHUNK, C_GMLP), 1)
    w_heads = [jnp.where(sp_mask, wsp_ref[hh], 0.0).astype(BF16) for hh in range(GMLP_HEADS)]
    g_parts = []
    for c in range(tm // GMLP_CHUNK):
        vc = vb[c * GMLP_CHUNK:(c + 1) * GMLP_CHUNK]
        sg = bsp_ref[...]
        for hh in range(GMLP_HEADS):
            head_cols = (col >= hh * GMLP_HD) & (col < (hh + 1) * GMLP_HD)
            vh = jnp.where(head_cols, vc, jnp.zeros_like(vc))
            sg = sg + jnp.dot(w_heads[hh], vh, preferred_element_type=F32)
        g_parts.append(u[c * GMLP_CHUNK:(c + 1) * GMLP_CHUNK] * sg)
    g_out = jnp.concatenate(g_parts, axis=0) if len(g_parts) > 1 else g_parts[0]

    qs = z_q * np.float32(XA_HD ** -0.5)
    qcol = lax.broadcasted_iota(I32, (sl, C_XA), 1)
    a_parts = []
    for s in range(ns):
        q_s = qs[s * sl:(s + 1) * sl]
        kb = mk_ref[0, s].astype(BF16)
        vvb = mv_ref[0, s].astype(BF16)
        a_s = jnp.zeros((sl, C_XA), F32)
        for hh in range(XA_HEADS):
            hmask = (qcol >= hh * XA_HD) & (qcol < (hh + 1) * XA_HD)
            qh = jnp.where(hmask, q_s, 0.0).astype(BF16)
            sc = lax.dot_general(qh, kb, (((1,), (1,)), ((), ())), preferred_element_type=F32)
            p = jnp.exp(sc - jnp.max(sc, axis=-1, keepdims=True))
            den = jnp.sum(p, axis=-1, keepdims=True)
            oh = jnp.dot(p.astype(BF16), vvb, preferred_element_type=F32) / den
            a_s = a_s + jnp.where(hmask, oh, 0.0)
        a_parts.append(a_s)
    a_out = jnp.concatenate(a_parts, axis=0) if len(a_parts) > 1 else a_parts[0]

    mix = jnp.concatenate([c_out, g_out, a_out], axis=-1).astype(BF16)
    x1 = x + jnp.dot(mix, wout_ref[...], preferred_element_type=F32)
    x1_ref[...] = x1

    h2 = _rmsnorm(x1, gffn_ref[...])
    h2_hi = h2.astype(BF16)
    h2_lo = (h2 - h2_hi.astype(F32)).astype(BF16)
    nt_dims = (((1,), (1,)), ((), ()))
    lg = (lax.dot_general(wrh_ref[...], h2_hi, nt_dims, preferred_element_type=F32)
          + lax.dot_general(wrl_ref[...], h2_hi, nt_dims, preferred_element_type=F32)
          + lax.dot_general(wrh_ref[...], h2_lo, nt_dims, preferred_element_type=F32)
          + br_ref[...])
    eio = lax.broadcasted_iota(I32, (N_EXPERTS, tm), 0)
    work = lg
    vals, idxs = [], []
    for _ in range(TOP_K):
        m = jnp.max(work, axis=0, keepdims=True)
        idx = jnp.min(jnp.where(work == m, eio, N_EXPERTS), axis=0, keepdims=True)
        vals.append(m)
        idxs.append(idx)
        work = jnp.where(eio == idx, -jnp.inf, work)
    exps = [jnp.exp(vk - vals[0]) for vk in vals]
    den = exps[0] + exps[1] + exps[2] + exps[3]
    gate_ref[...] = jnp.concatenate([ek / den for ek in exps], axis=0)

    sel = jnp.zeros((N_EXPERTS, tm), F32)
    for idx in idxs:
        sel = sel + jnp.where(eio == idx, 1.0, 0.0)
    selb = sel.astype(BF16)
    tr = lax.broadcasted_iota(I32, (tm, tm), 0)
    tc = lax.broadcasted_iota(I32, (tm, tm), 1)
    before = jnp.where(tr < tc, 1.0, 0.0).astype(BF16)
    ranks = jnp.dot(selb, before, preferred_element_type=F32)
    er = lax.broadcasted_iota(I32, (N_EXPERTS, N_EXPERTS), 0)
    ec = lax.broadcasted_iota(I32, (N_EXPERTS, N_EXPERTS), 1)
    lower = jnp.where(ec < er, 1.0, 0.0).astype(BF16)
    cnt = jnp.sum(sel, axis=1, keepdims=True)
    off = jnp.sum(jnp.dot(lower, selb, preferred_element_type=F32), axis=1, keepdims=True)
    cnt_ref[0] = jnp.broadcast_to(cnt, (N_EXPERTS, LANES)).astype(I32)
    off_ref[0] = jnp.broadcast_to(off, (N_EXPERTS, LANES)).astype(I32)
    slot_of = off + ranks
    pos = [jnp.sum(jnp.where(eio == idx, slot_of, 0.0), axis=0, keepdims=True).astype(I32) for idx in idxs]
    pos_ref[...] = jnp.concatenate(pos, axis=0)

    jj = lax.broadcasted_iota(I32, (TOP_K * tm, tm), 0)
    perm = jnp.zeros((TOP_K * tm, tm), F32)
    for pk in pos:
        perm = jnp.where(jj == pk, 1.0, perm)
    sorted_rows = jnp.dot(perm.astype(BF16), h2_hi, preferred_element_type=F32)
    _rows_to_tiles(xst_ref, sorted_rows, TOP_K * tm)


def _mixer(x, hist, mem_k, mem_v, wts, *, ns, sl, carry, sp_chunk, sorted_rows_total, sorted_row0, xs_prev):
    g, r, _ = x.shape
    tm = ns * sl
    nt = r // tm
    ntot = g * nt
    assert sorted_row0 % (TOP_K * tm) == 0
    xs_blk0 = sorted_row0 // (TOP_K * tm)
    const2 = lambda b, i: (0, 0)
    const3 = lambda b, i: (0, 0, 0)
    tile_row = lambda b, i: (b * nt + i, 0)
    tile_lane = lambda b, i: (0, b * nt + i)
    in_specs = [
        pl.BlockSpec((1, tm, D_MODEL), lambda b, i: (b, i, 0)),
        pl.BlockSpec((1, ns, HIST, C_CONV), lambda b, i: (b, 0, 0, 0)),
        pl.BlockSpec((1, ns, N_MEM, C_XA), lambda b, i: (b, 0, 0, 0)),
        pl.BlockSpec((1, ns, N_MEM, C_XA), lambda b, i: (b, 0, 0, 0)),
        pl.BlockSpec((1, D_MODEL), const2),
        pl.BlockSpec((D_MODEL, IN_COLS), const2),
        pl.BlockSpec((HIST, C_CONV), const2),
        pl.BlockSpec((1, C_CONV), const2),
        pl.BlockSpec((1, C_CONV), const2),
        pl.BlockSpec((1, C_CONV), const2),
        pl.BlockSpec((1, C_GMLP), const2),
        pl.BlockSpec((1, C_GMLP), const2),
        pl.BlockSpec((GMLP_HEADS, GMLP_CHUNK, GMLP_CHUNK), const3),
        pl.BlockSpec((GMLP_CHUNK, C_GMLP), const2),
        pl.BlockSpec((D_MODEL, D_MODEL), const2),
        pl.BlockSpec((1, D_MODEL), const2),
        pl.BlockSpec((N_EXPERTS, D_MODEL), const2),
        pl.BlockSpec((N_EXPERTS, D_MODEL), const2),
        pl.BlockSpec((N_EXPERTS, 1), const2),
    ]
    tile_cnt = lambda b, i: (b * nt + i, 0, 0)
    out_specs = [
        pl.BlockSpec((tm, D_MODEL), tile_row),
        pl.BlockSpec((TOP_K * tm * ROW_TILES, LANES),
                     lambda b, i: (xs_blk0 + b * nt + i, 0)),
        pl.BlockSpec((TOP_K, tm), tile_lane),
        pl.BlockSpec((TOP_K, tm), tile_lane),
        pl.BlockSpec((1, N_EXPERTS, LANES), tile_cnt),
        pl.BlockSpec((1, N_EXPERTS, LANES), tile_cnt),
        pl.BlockSpec((1, ns, HIST, C_CONV), lambda b, i: (b, 0, 0, 0)),
        pl.BlockSpec((tm, C_GMLP), tile_row),
    ]
    rows = g * r
    out_shape = [
        jax.ShapeDtypeStruct((rows, D_MODEL), F32),
        jax.ShapeDtypeStruct((sorted_rows_total * ROW_TILES, LANES), F32),
        jax.ShapeDtypeStruct((TOP_K, rows), I32),
        jax.ShapeDtypeStruct((TOP_K, rows), F32),
        jax.ShapeDtypeStruct((ntot, N_EXPERTS, LANES), I32),
        jax.ShapeDtypeStruct((ntot, N_EXPERTS, LANES), I32),
        jax.ShapeDtypeStruct((g, ns, HIST, C_CONV), F32),
        jax.ShapeDtypeStruct((rows, C_GMLP), F32),
    ]
    args = [x, hist, mem_k, mem_v, *wts]
    aliases = {}
    if xs_prev is not None:
        in_specs.append(pl.BlockSpec(memory_space=pl.ANY))
        aliases = {len(args): 1}
        args.append(xs_prev)
    kern = functools.partial(_mixer_kernel, ns=ns, sl=sl, carry=carry, sp_chunk=sp_chunk)
    return pl.pallas_call(
        kern,
        grid=(g, nt),
        in_specs=in_specs,
        out_specs=out_specs,
        out_shape=out_shape,
        scratch_shapes=[pltpu.VMEM((ns, HIST + sl, C_CONV), F32),
                        pltpu.VMEM((ns, SUBLANES - 1, HIST + sl, C_CONV), F32)],
        input_output_aliases=aliases,
        compiler_params=pltpu.CompilerParams(
            dimension_semantics=("arbitrary", "arbitrary"),
            vmem_limit_bytes=VMEM_LIMIT_MIXER),
        name="mixer_carry" if carry else "mixer_cache",
    )(*args)


def _strip_pieces(n, max_rows, fn):
    done = 0
    p = max_rows
    while p >= 1:
        has = (n & p) != 0
        pl.when(has)(functools.partial(fn, done, p))
        done = done + jnp.where(has, p, 0)
        p //= 2


def _rows(ref, first_row, rows):
    return ref.at[pl.ds(pl.multiple_of(first_row * ROW_TILES, ROW_TILES), rows * ROW_TILES)]


def _moe_kernel(b0_ref, nbk_ref, bv_ref, nb_ref, slo_ref, shi_ref, ssrc_ref, sdst_ref, slen_ref,
                xs_hbm, wgu_ref, bgu_ref, wdn_ref, bdn_ref, ys_hbm,
                xbuf, ybuf, wgu16, wdn16, sems, ysems):
    e = pl.program_id(0)
    n_used = nb_ref[0]
    half = MOE_BLOCK // 2

    def gather_start(b, slot):
        row0 = b * MOE_BLOCK

        def strip(s, carry):
            lo = jnp.maximum(sdst_ref[s], row0)
            hi = jnp.minimum(sdst_ref[s] + slen_ref[s], row0 + MOE_BLOCK)
            src0 = ssrc_ref[s] + (lo - sdst_ref[s])
            dst0 = lo - row0

            def piece(first, rows):
                pltpu.make_async_copy(_rows(xs_hbm, src0 + first, rows),
                                      _rows(xbuf.at[slot], dst0 + first, rows), sems.at[slot]).start()
            _strip_pieces(hi - lo, MOE_BLOCK, piece)
            return carry
        lax.fori_loop(slo_ref[b], shi_ref[b], strip, 0)

    def gather_wait(b, slot):
        def piece(first, rows):
            del first
            pltpu.make_async_copy(_rows(xs_hbm, 0, rows), _rows(xbuf.at[slot], 0, rows), sems.at[slot]).wait()
        _strip_pieces(bv_ref[b], MOE_BLOCK, piece)

    def out_copy(b, slot):
        return pltpu.make_async_copy(ybuf.at[slot], _rows(ys_hbm, b * MOE_BLOCK, MOE_BLOCK), ysems.at[slot])

    def expert_mlp(slot, h, valid):
        row_ok = lax.broadcasted_iota(I32, (half, 1), 0) + h * half < valid
        xb = jnp.where(row_ok, _tiles_to_rows(xbuf.at[slot], half, h * half), 0.0).astype(BF16)
        gu = jnp.dot(xb, wgu16[...], preferred_element_type=F32) + bgu_ref[0]
        gate = jnp.minimum(gu[:, :D_FF], SWIGLU_LIMIT)
        up = jnp.clip(gu[:, D_FF:], -SWIGLU_LIMIT, SWIGLU_LIMIT)
        act = (up + 1.0) * (gate * _sigmoid(SWIGLU_ALPHA * gate))
        yb = jnp.dot(act.astype(BF16), wdn16[...], preferred_element_type=F32) + bdn_ref[0]
        _rows_to_tiles(ybuf.at[slot], yb, half, h * half)

    @pl.when((e == 0) & (n_used > 0))
    def _():
        gather_start(0, 0)

    @pl.when(nbk_ref[e] > 0)
    def _():
        def cast_rows(c, carry):
            r = pl.multiple_of(c * LANES, LANES)
            wgu16[pl.ds(r, LANES), :] = wgu_ref[pl.ds(r, LANES), :].astype(BF16)
            wdn16[pl.ds(r, LANES), :] = wdn_ref[pl.ds(r, LANES), :].astype(BF16)
            return carry
        lax.fori_loop(0, D_MODEL // LANES, cast_rows, 0)

        def block(j, carry):
            b = b0_ref[e] + j
            slot = b % 2

            @pl.when(b + 1 < n_used)
            def _():
                gather_start(b + 1, 1 - slot)

            gather_wait(b, slot)

            @pl.when(b >= 2)
            def _():
                out_copy(b - 2, slot).wait()

            valid = bv_ref[b]
            expert_mlp(slot, 0, valid)

            @pl.when(valid > half)
            def _():
                expert_mlp(slot, 1, valid)

            @pl.when(valid <= half)
            def _():
                ybuf[slot, pl.ds(half * ROW_TILES, half * ROW_TILES), :] = jnp.zeros(
                    (half * ROW_TILES, LANES), F32)

            out_copy(b, slot).start()
            return carry
        lax.fori_loop(0, nbk_ref[e], block, 0)

    @pl.when(e == pl.num_programs(0) - 1)
    def _():
        for back in (1, 2):
            @pl.when(n_used >= back)
            def _():
                out_copy(n_used - back, (n_used - back) % 2).wait()


def _moe(tables, xs, slots, w_gu, b_gu, w_dn, b_dn):
    wsel = lambda e, *_: (e, 0, 0)
    grid_spec = pltpu.PrefetchScalarGridSpec(
        num_scalar_prefetch=len(tables),
        grid=(N_EXPERTS,),
        in_specs=[pl.BlockSpec(memory_space=pl.ANY),
                  pl.BlockSpec((None, D_MODEL, 2 * D_FF), wsel),
                  pl.BlockSpec((None, 1, 2 * D_FF), wsel),
                  pl.BlockSpec((None, D_FF, D_MODEL), wsel),
                  pl.BlockSpec((None, 1, D_MODEL), wsel)],
        out_specs=pl.BlockSpec(memory_space=pl.ANY),
        scratch_shapes=[pltpu.VMEM((2, MOE_BLOCK * ROW_TILES, LANES), F32),
                        pltpu.VMEM((2, MOE_BLOCK * ROW_TILES, LANES), F32),
                        pltpu.VMEM((D_MODEL, 2 * D_FF), BF16),
                        pltpu.VMEM((D_FF, D_MODEL), BF16),
                        pltpu.SemaphoreType.DMA((2,)),
                        pltpu.SemaphoreType.DMA((2,))],
    )
    return pl.pallas_call(
        _moe_kernel,
        grid_spec=grid_spec,
        out_shape=jax.ShapeDtypeStruct((slots * ROW_TILES, LANES), F32),
        compiler_params=pltpu.CompilerParams(
            dimension_semantics=("arbitrary",),
            vmem_limit_bytes=VMEM_LIMIT_MOE),
        name="moe",
    )(*tables, xs, w_gu, b_gu, w_dn, b_dn)


def _combine_kernel(csrc_ref, clen_ref, coff_ref, x1_ref, pos_ref, gate_ref, gfin_ref, ys_hbm,
                    out_ref, ybuf, sems, *, tm):
    i = pl.program_id(0)
    n_tiles = pl.num_programs(0)
    n_sorted = TOP_K * tm

    def gather_start(t, slot):
        def strip(e, carry):
            s = t * N_EXPERTS + e
            src0 = csrc_ref[s]
            dst0 = coff_ref[s]

            def piece(first, rows):
                pltpu.make_async_copy(_rows(ys_hbm, src0 + first, rows),
                                      _rows(ybuf.at[slot], dst0 + first, rows), sems.at[slot]).start()
            _strip_pieces(clen_ref[s], tm, piece)
            return carry
        lax.fori_loop(0, N_EXPERTS, strip, 0)

    @pl.when(i == 0)
    def _():
        gather_start(0, 0)

    @pl.when(i + 1 < n_tiles)
    def _():
        gather_start(i + 1, (i + 1) % 2)

    slot = i % 2
    pltpu.make_async_copy(_rows(ys_hbm, 0, n_sorted), ybuf.at[slot], sems.at[slot]).wait()
    y_sorted = _tiles_to_rows(ybuf.at[slot], n_sorted).astype(BF16)

    rr = lax.broadcasted_iota(I32, (tm, tm), 0)
    cc = lax.broadcasted_iota(I32, (tm, tm), 1)
    eye = rr == cc
    jl = lax.broadcasted_iota(I32, (tm, n_sorted), 1).astype(F32)
    unsort = jnp.zeros((tm, n_sorted), F32)
    for k in range(TOP_K):
        p_col = jnp.sum(jnp.where(eye, pos_ref[k:k + 1, :].astype(F32), 0.0), axis=1, keepdims=True)
        g_col = jnp.sum(jnp.where(eye, gate_ref[k:k + 1, :], 0.0), axis=1, keepdims=True)
        unsort = jnp.where(jl == p_col, g_col, unsort)
    acc = x1_ref[...] + jnp.dot(unsort.astype(BF16), y_sorted, preferred_element_type=F32)
    out_ref[...] = _rmsnorm(acc, gfin_ref[...])


def _combine(tables, x1, pos, gates, g_final, ys, tm):
    t = x1.shape[0]
    nt = t // tm
    grid_spec = pltpu.PrefetchScalarGridSpec(
        num_scalar_prefetch=len(tables),
        grid=(nt,),
        in_specs=[pl.BlockSpec((tm, D_MODEL), lambda i, *_: (i, 0)),
                  pl.BlockSpec((TOP_K, tm), lambda i, *_: (0, i)),
                  pl.BlockSpec((TOP_K, tm), lambda i, *_: (0, i)),
                  pl.BlockSpec((1, D_MODEL), lambda i, *_: (0, 0)),
                  pl.BlockSpec(memory_space=pl.ANY)],
        out_specs=pl.BlockSpec((tm, D_MODEL), lambda i, *_: (i, 0)),
        scratch_shapes=[pltpu.VMEM((2, TOP_K * tm * ROW_TILES, LANES), F32),
                        pltpu.SemaphoreType.DMA((2,))],
    )
    return pl.pallas_call(
        functools.partial(_combine_kernel, tm=tm),
        grid_spec=grid_spec,
        out_shape=jax.ShapeDtypeStruct((t, D_MODEL), F32),
        compiler_params=pltpu.CompilerParams(
            dimension_semantics=("arbitrary",),
            vmem_limit_bytes=VMEM_LIMIT_MIXER),
        name="combine",
    )(*tables, x1, pos, gates, g_final, ys)


def _split_bf16(w):
    hi = w.astype(BF16)
    lo = (w - hi.astype(F32)).astype(BF16)
    return hi, lo


def kernel(x_prompt, x_sample, cache_conv, cache_mem_k, cache_mem_v, mem_prompt, g_mix, w_in, w_dw, b_dw, ln_conv_g, ln_conv_b, ln_v_g, ln_v_b, w_spatial, b_spatial, g_mem, w_mem_k, w_mem_v, w_out, g_ffn, w_router, b_router, w_gate_up, b_gate_up, w_down, b_down, g_final):
    depth = g_mix.shape[0]
    assert depth == 1
    l = 0
    bp, seq, _ = x_prompt.shape
    bs, dseq, _ = x_sample.shape
    assert seq % PROMPT_TILE == 0 and bs % SAMPLE_SEQS_PER_TILE == 0
    assert GMLP_CHUNK % dseq == 0 and SAMPLE_SEQS_PER_TILE * dseq == GMLP_CHUNK

    row = lambda a: a.reshape(1, -1)
    wr_hi, wr_lo = _split_bf16(w_router[l].T)
    w_dw_pad = jnp.pad(w_dw[l], ((0, HIST - CONV_W), (0, 0)))
    bias_rows = lambda b: jnp.repeat(b.T, GMLP_HD, axis=1)
    common = dict(
        gmix=row(g_mix[l]), win=w_in[l].astype(BF16), wdw=w_dw_pad, bdw=row(b_dw[l]),
        lcg=row(ln_conv_g[l]), lcb=row(ln_conv_b[l]), lvg=row(ln_v_g[l]), lvb=row(ln_v_b[l]),
        wout=w_out[l].astype(BF16), gffn=row(g_ffn[l]), wrh=wr_hi, wrl=wr_lo,
        br=b_router[l].reshape(N_EXPERTS, 1))

    def weights(wsp, bsp):
        c = common
        return (c["gmix"], c["win"], c["wdw"], c["bdw"], c["lcg"], c["lcb"], c["lvg"], c["lvb"],
                wsp, bsp, c["wout"], c["gffn"], c["wrh"], c["wrl"], c["br"])

    reps = GMLP_CHUNK // dseq
    wts_p = weights(w_spatial[l], bias_rows(b_spatial[l]))
    wts_s = weights(jnp.tile(w_spatial[l][:, :dseq, :dseq], (1, reps, reps)),
                    bias_rows(jnp.tile(b_spatial[l][:, :dseq], (1, reps))))

    w_kv = jnp.concatenate([w_mem_k[l], w_mem_v[l]], axis=1).astype(BF16)
    kv_p = _memkv(mem_prompt, row(g_mem[l]), w_kv)
    mk_p = kv_p[:, :, :C_XA]
    mv_p = kv_p[:, :, C_XA:]
    zero_hist = jnp.zeros((bp, 1, HIST, C_CONV), F32)
    tp, ts = bp * seq, bs * dseq
    n_assign = (tp + ts) * TOP_K
    (x1_p, xs, pos_p, gate_p, cnt_p, off_p, hist_p, _) = _mixer(
        x_prompt, zero_hist, mk_p[:, None], mv_p[:, None], wts_p,
        ns=1, sl=PROMPT_TILE, carry=True, sp_chunk=GMLP_CHUNK,
        sorted_rows_total=n_assign, sorted_row0=0, xs_prev=None)

    gs = bs // SAMPLE_SEQS_PER_TILE
    hist_s_in = jnp.pad(cache_conv[l], ((0, 0), (HIST_OFF, 0), (0, 0))).reshape(
        gs, SAMPLE_SEQS_PER_TILE, HIST, C_CONV)
    mk_s = cache_mem_k[l].reshape(gs, SAMPLE_SEQS_PER_TILE, N_MEM, C_XA)
    mv_s = cache_mem_v[l].reshape(gs, SAMPLE_SEQS_PER_TILE, N_MEM, C_XA)
    tm_s = SAMPLE_SEQS_PER_TILE * dseq
    (x1_s, xs, pos_s, gate_s, cnt_s, off_s, hist_s, v_s) = _mixer(
        x_sample.reshape(gs, tm_s, D_MODEL), hist_s_in, mk_s, mv_s, wts_s,
        ns=SAMPLE_SEQS_PER_TILE, sl=dseq, carry=False, sp_chunk=dseq,
        sorted_rows_total=n_assign, sorted_row0=tp * TOP_K, xs_prev=xs)

    n_blocks = -(-n_assign // MOE_BLOCK) + N_EXPERTS
    ntp, nts = tp // PROMPT_TILE, ts // tm_s
    cnt = jnp.concatenate([cnt_p[:, :, 0], cnt_s[:, :, 0]], axis=0)
    off = jnp.concatenate([off_p[:, :, 0], off_s[:, :, 0]], axis=0)
    tile_row0 = np.concatenate([np.arange(ntp) * (TOP_K * PROMPT_TILE),
                                tp * TOP_K + np.arange(nts) * (TOP_K * tm_s)]).astype(np.int32)
    counts = jnp.sum(cnt, axis=0)
    tile_base = jnp.cumsum(cnt, axis=0) - cnt
    padded = (counts + MOE_BLOCK - 1) // MOE_BLOCK * MOE_BLOCK
    pad_end = jnp.cumsum(padded)
    pad_start = pad_end - padded
    strip_dst = (pad_start[None, :] + tile_base).astype(I32)
    strip_src = (tile_row0[:, None] + off).astype(I32)
    n_used = (pad_end[-1] // MOE_BLOCK).astype(I32)
    blk_row0 = jnp.minimum(jnp.arange(n_blocks, dtype=I32), n_used - 1) * MOE_BLOCK
    block_expert = jnp.minimum(
        jnp.sum((pad_end[None, :] <= blk_row0[:, None]).astype(I32), axis=1), N_EXPERTS - 1)
    of_block = block_expert[:, None] == jnp.arange(N_EXPERTS, dtype=I32)[None, :]
    last_row = jnp.sum(jnp.where(of_block, (pad_start + counts)[None, :], 0), axis=1)
    block_valid = jnp.clip(last_row - blk_row0, 0, MOE_BLOCK).astype(I32)
    sdst = strip_dst.T.reshape(-1)
    ssrc = strip_src.T.reshape(-1)
    slen = cnt.T.reshape(-1).astype(I32)
    s_lo = jnp.sum(((sdst + slen)[None, :] <= blk_row0[:, None]).astype(I32), axis=1)
    s_hi = jnp.sum((sdst[None, :] < (blk_row0 + MOE_BLOCK)[:, None]).astype(I32), axis=1)
    moe_tables = ((pad_start // MOE_BLOCK).astype(I32), (padded // MOE_BLOCK).astype(I32),
                  block_valid, n_used.reshape(1),
                  s_lo.astype(I32), s_hi.astype(I32), ssrc, sdst, slen)

    slots = n_blocks * MOE_BLOCK
    ys = _moe(moe_tables, xs, slots,
              w_gate_up[l], b_gate_up[l][:, None, :], w_down[l], b_down[l][:, None, :])
    gfin = row(g_final)

    def combine_tables(t0, t1):
        return (strip_dst[t0:t1].reshape(-1), cnt[t0:t1].reshape(-1).astype(I32),
                off[t0:t1].reshape(-1).astype(I32))

    y_p = _combine(combine_tables(0, ntp), x1_p, pos_p, gate_p, gfin, ys, PROMPT_TILE)
    y_s = _combine(combine_tables(ntp, ntp + nts), x1_s, pos_s, gate_s, gfin, ys, tm_s)

    return (y_p.reshape(bp, seq, D_MODEL),
            y_s.reshape(bs, dseq, D_MODEL),
            hist_p[:, 0, HIST_OFF:, :][None],
            mk_p.reshape(bp, N_MEM, XA_HEADS, XA_HD)[None],
            mv_p.reshape(bp, N_MEM, XA_HEADS, XA_HD)[None],
            hist_s.reshape(bs, HIST, C_CONV)[:, HIST_OFF:, :][None],
            v_s.reshape(bs, dseq, C_GMLP)[None])
```

```python
import functools

import numpy as np
import jax
import jax.numpy as jnp
from jax import lax
from jax.experimental import pallas as pl
from jax.experimental.pallas import tpu as pltpu

F32 = jnp.float32
BF16 = jnp.bfloat16
I32 = jnp.int32

D_MODEL = 1024
C_CONV = 384
CONV_W = 31
C_GMLP = 384
GMLP_HEADS = 4
GMLP_HD = 96
GMLP_CHUNK = 128
XA_HEADS = 4
XA_HD = 64
C_XA = 256
N_MEM = 256
N_EXPERTS = 32
TOP_K = 4
D_FF = 1024
SWIGLU_LIMIT = 7.0
SWIGLU_ALPHA = 1.702
EPS = 1e-5
IN_COLS = 2 * C_CONV + 2 * C_GMLP + C_XA

SUBLANES = 8
LANES = 128
ROW_TILES = D_MODEL // LANES
HIST = 32
HIST_OFF = HIST - (CONV_W - 1)

PROMPT_TILE = 512
SAMPLE_SEQS_PER_TILE = 8
MOE_BLOCK = 512
DISPATCH_TILE = 512
COMBINE_TILE = 128
DMA_UNROLL = 8
VMEM_LIMIT_MIXER = 48 * 1024 * 1024
VMEM_LIMIT_MOE = 52 * 1024 * 1024


def _rmsnorm(x, g):
    return x * lax.rsqrt(jnp.mean(x * x, axis=-1, keepdims=True) + EPS) * g


def _layernorm(x, g, b):
    mu = jnp.mean(x, axis=-1, keepdims=True)
    xc = x - mu
    var = jnp.mean(xc * xc, axis=-1, keepdims=True)
    return xc * lax.rsqrt(var + EPS) * g + b


def _gelu(x):
    return 0.5 * x * (1.0 + lax.erf(x * np.float32(1.0 / np.sqrt(2.0))))


def _sigmoid(x):
    return 1.0 / (1.0 + jnp.exp(-x))


def _rows_to_tiles(dst_ref, val, rows, row0=0):
    for j in range(ROW_TILES):
        dst_ref[pl.ds(row0 * ROW_TILES + j, rows, stride=ROW_TILES), :] = val[:, j * LANES:(j + 1) * LANES]


def _tiles_to_rows(src_ref, rows, row0=0):
    return jnp.concatenate(
        [src_ref[pl.ds(row0 * ROW_TILES + j, rows, stride=ROW_TILES), :] for j in range(ROW_TILES)],
        axis=-1)


def _memkv_kernel(mem_ref, g_ref, w_ref, o_ref):
    mn = _rmsnorm(mem_ref[0], g_ref[...])
    o_ref[0] = jnp.dot(mn.astype(BF16), w_ref[...], preferred_element_type=F32)


def _memkv(mem, g_mem, w_kv):
    b = mem.shape[0]
    return pl.pallas_call(
        _memkv_kernel,
        grid=(b,),
        in_specs=[pl.BlockSpec((1, N_MEM, D_MODEL), lambda i: (i, 0, 0)),
                  pl.BlockSpec((1, D_MODEL), lambda i: (0, 0)),
                  pl.BlockSpec((D_MODEL, 2 * C_XA), lambda i: (0, 0))],
        out_specs=pl.BlockSpec((1, N_MEM, 2 * C_XA), lambda i: (i, 0, 0)),
        out_shape=jax.ShapeDtypeStruct((b, N_MEM, 2 * C_XA), F32),
        name="memkv",
    )(mem, g_mem, w_kv)


def _mixer_kernel(x_ref, hist_ref, mk_ref, mv_ref, gmix_ref, win_ref, wdw_ref, bdw_ref,
                  lcg_ref, lcb_ref, lvg_ref, lvb_ref, wsp_ref, bsp_ref, wout_ref, gffn_ref,
                  wrh_ref, wrl_ref, br_ref, *rest, ns, sl, carry, sp_chunk, tiles_per_seq):
    (x1_ref, xst_ref, pos_ref, gate_ref, cnt_ref, off_ref, histout_ref, v_ref,
     ext_ref, shift_ref, h2_keep, pos_keep) = rest[-12:]
    tm = ns * sl
    step = pl.program_id(0)
    n_tiles = pl.num_programs(0) - 1

    def sort_previous_tile():
        jj = lax.broadcasted_iota(I32, (TOP_K * tm, tm), 0)
        perm = jnp.zeros((TOP_K * tm, tm), F32)
        for k in range(TOP_K):
            perm = jnp.where(jj == pos_keep[k:k + 1, :], 1.0, perm)
        sorted_rows = jnp.dot(perm.astype(BF16), h2_keep[...], preferred_element_type=F32)
        _rows_to_tiles(xst_ref, sorted_rows, TOP_K * tm)

    @pl.when(step == 0)
    def _():
        h2_keep[...] = jnp.zeros_like(h2_keep)
        pos_keep[...] = jnp.full(pos_keep.shape, -1, I32)

    @pl.when(step < n_tiles)
    def _():
        sort_previous_tile()
        _mixer_tile(x_ref, hist_ref, mk_ref, mv_ref, gmix_ref, win_ref, wdw_ref, bdw_ref,
                    lcg_ref, lcb_ref, lvg_ref, lvb_ref, wsp_ref, bsp_ref, wout_ref, gffn_ref,
                    wrh_ref, wrl_ref, br_ref, x1_ref, pos_ref, gate_ref, cnt_ref, off_ref,
                    histout_ref, v_ref, ext_ref, shift_ref, h2_keep, pos_keep,
                    first_of_seq=(step % tiles_per_seq) == 0,
                    ns=ns, sl=sl, carry=carry, sp_chunk=sp_chunk)

    @pl.when(step == n_tiles)
    def _():
        sort_previous_tile()


def _mixer_tile(x_ref, hist_ref, mk_ref, mv_ref, gmix_ref, win_ref, wdw_ref, bdw_ref,
                lcg_ref, lcb_ref, lvg_ref, lvb_ref, wsp_ref, bsp_ref, wout_ref, gffn_ref,
                wrh_ref, wrl_ref, br_ref, x1_ref, pos_ref, gate_ref, cnt_ref, off_ref,
                histout_ref, v_ref, ext_ref, shift_ref, h2_keep, pos_keep,
                *, first_of_seq, ns, sl, carry, sp_chunk):
    tm = ns * sl
    x = x_ref[0]
    h = _rmsnorm(x, gmix_ref[...])
    z = jnp.dot(h.astype(BF16), win_ref[...], preferred_element_type=F32)
    z_a = z[:, 0:C_CONV]
    z_g = z[:, C_CONV:2 * C_CONV]
    z_u = z[:, 2 * C_CONV:2 * C_CONV + C_GMLP]
    z_v = z[:, 2 * C_CONV + C_GMLP:2 * C_CONV + 2 * C_GMLP]
    z_q = z[:, 2 * C_CONV + 2 * C_GMLP:IN_COLS]

    glu = z_a * _sigmoid(z_g)
    if carry:
        ext_ref[:, 0:HIST, :] = jnp.where(first_of_seq, hist_ref[0], ext_ref[:, 0:HIST, :])
    else:
        ext_ref[:, 0:HIST, :] = hist_ref[0]
    conv_parts = []
    for s in range(ns):
        ext_s = ext_ref.at[s]
        ext_s[HIST:HIST + sl, :] = glu[s * sl:(s + 1) * sl]
        n_shift = HIST + sl - SUBLANES
        for r in range(1, SUBLANES):
            shift_ref[s, r - 1, 0:n_shift, :] = ext_s[pl.ds(r, n_shift), :]
        rc = min(sl, 64)
        for r0 in range(0, sl, rc):
            acc = jnp.broadcast_to(bdw_ref[...], (rc, C_CONV))
            for j in range(CONV_W):
                a, r = divmod(j + HIST_OFF, SUBLANES)
                src = ext_s if r == 0 else shift_ref.at[s, r - 1]
                acc = acc + wdw_ref[j:j + 1, :] * src[pl.ds(r0 + a * SUBLANES, rc), :]
            conv_parts.append(acc)
        new_hist = ext_s[sl:sl + HIST, :]
        histout_ref[0, s] = new_hist
        if carry:
            ext_s[0:HIST, :] = new_hist
    y = jnp.concatenate(conv_parts, axis=0) if len(conv_parts) > 1 else conv_parts[0]
    y = _layernorm(y, lcg_ref[...], lcb_ref[...])
    c_out = y * _sigmoid(y)

    u = _gelu(z_u)
    v = _layernorm(_gelu(z_v), lvg_ref[...], lvb_ref[...])
    v_ref[...] = v
    vb = v.astype(BF16)
    rr = lax.broadcasted_iota(I32, (GMLP_CHUNK, GMLP_CHUNK), 0)
    cc = lax.broadcasted_iota(I32, (GMLP_CHUNK, GMLP_CHUNK), 1)
    sp_mask = (cc <= rr) & ((rr // sp_chunk) == (cc // sp_chunk))
    col = lax.broadcasted_iota(I32, (GMLP_CHUNK, C_GMLP), 1)
    w_heads = [jnp.where(sp_mask, wsp_ref[hh], 0.0).astype(BF16) for hh in range(GMLP_HEADS)]
    g_parts = []
    for c in range(tm // GMLP_CHUNK):
        vc = vb[c * GMLP_CHUNK:(c + 1) * GMLP_CHUNK]
        sg = bsp_ref[...]
        for hh in range(GMLP_HEADS):
            head_cols = (col >= hh * GMLP_HD) & (col < (hh + 1) * GMLP_HD)
            vh = jnp.where(head_cols, vc, jnp.zeros_like(vc))
            sg = sg + jnp.dot(w_heads[hh], vh, preferred_element_type=F32)
        g_parts.append(u[c * GMLP_CHUNK:(c + 1) * GMLP_CHUNK] * sg)
    g_out = jnp.concatenate(g_parts, axis=0) if len(g_parts) > 1 else g_parts[0]

    qs = z_q * np.float32(XA_HD ** -0.5)
    qcol = lax.broadcasted_iota(I32, (sl, C_XA), 1)
    a_parts = []
    for s in range(ns):
        q_s = qs[s * sl:(s + 1) * sl]
        kb = mk_ref[0, s].astype(BF16)
        vvb = mv_ref[0, s].astype(BF16)
        a_s = jnp.zeros((sl, C_XA), F32)
        for hh in range(XA_HEADS):
            hmask = (qcol >= hh * XA_HD) & (qcol < (hh + 1) * XA_HD)
            qh = jnp.where(hmask, q_s, 0.0).astype(BF16)
            sc = lax.dot_general(qh, kb, (((1,), (1,)), ((), ())), preferred_element_type=F32)
            p = jnp.exp(sc - jnp.max(sc, axis=-1, keepdims=True))
            den = jnp.sum(p, axis=-1, keepdims=True)
            oh = jnp.dot(p.astype(BF16), vvb, preferred_element_type=F32) / den
            a_s = a_s + jnp.where(hmask, oh, 0.0)
        a_parts.append(a_s)
    a_out = jnp.concatenate(a_parts, axis=0) if len(a_parts) > 1 else a_parts[0]

    mix = jnp.concatenate([c_out, g_out, a_out], axis=-1).astype(BF16)
    x1 = x + jnp.dot(mix, wout_ref[...], preferred_element_type=F32)
    x1_ref[...] = x1

    h2 = _rmsnorm(x1, gffn_ref[...])
    h2_hi = h2.astype(BF16)
    h2_lo = (h2 - h2_hi.astype(F32)).astype(BF16)
    nt_dims = (((1,), (1,)), ((), ()))
    lg = (lax.dot_general(wrh_ref[...], h2_hi, nt_dims, preferred_element_type=F32)
          + lax.dot_general(wrl_ref[...], h2_hi, nt_dims, preferred_element_type=F32)
          + lax.dot_general(wrh_ref[...], h2_lo, nt_dims, preferred_element_type=F32)
          + br_ref[...])
    eio = lax.broadcasted_iota(I32, (N_EXPERTS, tm), 0)
    work = lg
    vals, idxs = [], []
    for _ in range(TOP_K):
        m = jnp.max(work, axis=0, keepdims=True)
        idx = jnp.min(jnp.where(work == m, eio, N_EXPERTS), axis=0, keepdims=True)
        vals.append(m)
        idxs.append(idx)
        work = jnp.where(eio == idx, -jnp.inf, work)
    exps = [jnp.exp(vk - vals[0]) for vk in vals]
    den = exps[0] + exps[1] + exps[2] + exps[3]
    gate_ref[...] = jnp.concatenate([ek / den for ek in exps], axis=0)

    sel = jnp.zeros((N_EXPERTS, tm), F32)
    for idx in idxs:
        sel = sel + jnp.where(eio == idx, 1.0, 0.0)
    selb = sel.astype(BF16)
    tr = lax.broadcasted_iota(I32, (tm, tm), 0)
    tc = lax.broadcasted_iota(I32, (tm, tm), 1)
    before = jnp.where(tr < tc, 1.0, 0.0).astype(BF16)
    ranks = jnp.dot(selb, before, preferred_element_type=F32)
    er = lax.broadcasted_iota(I32, (N_EXPERTS, N_EXPERTS), 0)
    ec = lax.broadcasted_iota(I32, (N_EXPERTS, N_EXPERTS), 1)
    lower = jnp.where(ec < er, 1.0, 0.0).astype(BF16)
    cnt = jnp.sum(sel, axis=1, keepdims=True)
    off = jnp.sum(jnp.dot(lower, selb, preferred_element_type=F32), axis=1, keepdims=True)
    cnt_ref[0] = jnp.broadcast_to(cnt, (N_EXPERTS, LANES)).astype(I32)
    off_ref[0] = jnp.broadcast_to(off, (N_EXPERTS, LANES)).astype(I32)
    slot_of = off + ranks
    pos = [jnp.sum(jnp.where(eio == idx, slot_of, 0.0), axis=0, keepdims=True).astype(I32) for idx in idxs]
    pos_all = jnp.concatenate(pos, axis=0)
    pos_ref[...] = pos_all
    pos_keep[...] = pos_all
    h2_keep[...] = h2_hi


def _mixer(x, hist, mem_k, mem_v, wts, *, ns, sl, carry, sp_chunk, sorted_rows_total, sorted_row0, xs_prev):
    g, r, _ = x.shape
    tm = ns * sl
    nt = r // tm
    ntot = g * nt
    assert sorted_row0 % (TOP_K * tm) == 0
    xs_blk0 = sorted_row0 // (TOP_K * tm)
    tile = lambda s: jnp.minimum(s, ntot - 1)
    const2 = lambda s: (0, 0)
    const3 = lambda s: (0, 0, 0)
    tile_row = lambda s: (tile(s), 0)
    tile_lane = lambda s: (0, tile(s))
    per_seq = lambda s: (tile(s) // nt, 0, 0, 0)
    in_specs = [
        pl.BlockSpec((1, tm, D_MODEL), lambda s: (tile(s) // nt, tile(s) % nt, 0)),
        pl.BlockSpec((1, ns, HIST, C_CONV), per_seq),
        pl.BlockSpec((1, ns, N_MEM, C_XA), per_seq),
        pl.BlockSpec((1, ns, N_MEM, C_XA), per_seq),
        pl.BlockSpec((1, D_MODEL), const2),
        pl.BlockSpec((D_MODEL, IN_COLS), const2),
        pl.BlockSpec((HIST, C_CONV), const2),
        pl.BlockSpec((1, C_CONV), const2),
        pl.BlockSpec((1, C_CONV), const2),
        pl.BlockSpec((1, C_CONV), const2),
        pl.BlockSpec((1, C_GMLP), const2),
        pl.BlockSpec((1, C_GMLP), const2),
        pl.BlockSpec((GMLP_HEADS, GMLP_CHUNK, GMLP_CHUNK), const3),
        pl.BlockSpec((GMLP_CHUNK, C_GMLP), const2),
        pl.BlockSpec((D_MODEL, D_MODEL), const2),
        pl.BlockSpec((1, D_MODEL), const2),
        pl.BlockSpec((N_EXPERTS, D_MODEL), const2),
        pl.BlockSpec((N_EXPERTS, D_MODEL), const2),
        pl.BlockSpec((N_EXPERTS, 1), const2),
    ]
    tile_cnt = lambda s: (tile(s), 0, 0)
    out_specs = [
        pl.BlockSpec((tm, D_MODEL), tile_row),
        pl.BlockSpec((TOP_K * tm * ROW_TILES, LANES),
                     lambda s: (xs_blk0 + jnp.maximum(s - 1, 0), 0)),
        pl.BlockSpec((TOP_K, tm), tile_lane),
        pl.BlockSpec((TOP_K, tm), tile_lane),
        pl.BlockSpec((1, N_EXPERTS, LANES), tile_cnt),
        pl.BlockSpec((1, N_EXPERTS, LANES), tile_cnt),
        pl.BlockSpec((1, ns, HIST, C_CONV), per_seq),
        pl.BlockSpec((tm, C_GMLP), tile_row),
    ]
    rows = g * r
    out_shape = [
        jax.ShapeDtypeStruct((rows, D_MODEL), F32),
        jax.ShapeDtypeStruct((sorted_rows_total * ROW_TILES, LANES), F32),
        jax.ShapeDtypeStruct((TOP_K, rows), I32),
        jax.ShapeDtypeStruct((TOP_K, rows), F32),
        jax.ShapeDtypeStruct((ntot, N_EXPERTS, LANES), I32),
        jax.ShapeDtypeStruct((ntot, N_EXPERTS, LANES), I32),
        jax.ShapeDtypeStruct((g, ns, HIST, C_CONV), F32),
        jax.ShapeDtypeStruct((rows, C_GMLP), F32),
    ]
    args = [x, hist, mem_k, mem_v, *wts]
    aliases = {}
    if xs_prev is not None:
        in_specs.append(pl.BlockSpec(memory_space=pl.ANY))
        aliases = {len(args): 1}
        args.append(xs_prev)
    kern = functools.partial(_mixer_kernel, ns=ns, sl=sl, carry=carry, sp_chunk=sp_chunk,
                             tiles_per_seq=nt)
    return pl.pallas_call(
        kern,
        grid=(ntot + 1,),
        in_specs=in_specs,
        out_specs=out_specs,
        out_shape=out_shape,
        scratch_shapes=[pltpu.VMEM((ns, HIST + sl, C_CONV), F32),
                        pltpu.VMEM((ns, SUBLANES - 1, HIST + sl, C_CONV), F32),
                        pltpu.VMEM((tm, D_MODEL), BF16),
                        pltpu.VMEM((TOP_K, tm), I32)],
        input_output_aliases=aliases,
        compiler_params=pltpu.CompilerParams(
            dimension_semantics=("arbitrary",),
            vmem_limit_bytes=VMEM_LIMIT_MIXER),
        name="mixer_carry" if carry else "mixer_cache",
    )(*args)


def _strip_pieces(n, max_rows, fn):
    done = 0
    p = max_rows
    while p >= 1:
        has = (n & p) != 0
        pl.when(has)(functools.partial(fn, done, p))
        done = done + jnp.where(has, p, 0)
        p //= 2


def _rows(ref, first_row, rows):
    return ref.at[pl.ds(pl.multiple_of(first_row * ROW_TILES, ROW_TILES), rows * ROW_TILES)]


def _moe_kernel(b0_ref, nbk_ref, bv_ref, nb_ref, slo_ref, shi_ref, ssrc_ref, sdst_ref, slen_ref,
                xs_hbm, wgu_ref, bgu_ref, wdn_ref, bdn_ref, ys_hbm,
                xbuf, ybuf, wgu16, wdn16, sems, ysems):
    e = pl.program_id(0)
    n_used = nb_ref[0]
    half = MOE_BLOCK // 2

    def gather_start(b, slot):
        row0 = b * MOE_BLOCK

        def strip(s, carry):
            lo = jnp.maximum(sdst_ref[s], row0)
            hi = jnp.minimum(sdst_ref[s] + slen_ref[s], row0 + MOE_BLOCK)
            src0 = ssrc_ref[s] + (lo - sdst_ref[s])
            dst0 = lo - row0

            def piece(first, rows):
                pltpu.make_async_copy(_rows(xs_hbm, src0 + first, rows),
                                      _rows(xbuf.at[slot], dst0 + first, rows), sems.at[slot]).start()
            _strip_pieces(hi - lo, MOE_BLOCK, piece)
            return carry
        lax.fori_loop(slo_ref[b], shi_ref[b], strip, 0)

    def gather_wait(b, slot):
        def piece(first, rows):
            del first
            pltpu.make_async_copy(_rows(xs_hbm, 0, rows), _rows(xbuf.at[slot], 0, rows), sems.at[slot]).wait()
        _strip_pieces(bv_ref[b], MOE_BLOCK, piece)

    def out_copy(b, slot):
        return pltpu.make_async_copy(ybuf.at[slot], _rows(ys_hbm, b * MOE_BLOCK, MOE_BLOCK), ysems.at[slot])

    def expert_mlp(slot, h, valid):
        row_ok = lax.broadcasted_iota(I32, (half, 1), 0) + h * half < valid
        xb = jnp.where(row_ok, _tiles_to_rows(xbuf.at[slot], half, h * half), 0.0).astype(BF16)
        gu = jnp.dot(xb, wgu16[...], preferred_element_type=F32) + bgu_ref[0]
        gate = jnp.minimum(gu[:, :D_FF], SWIGLU_LIMIT)
        up = jnp.clip(gu[:, D_FF:], -SWIGLU_LIMIT, SWIGLU_LIMIT)
        act = (up + 1.0) * (gate * _sigmoid(SWIGLU_ALPHA * gate))
        yb = jnp.dot(act.astype(BF16), wdn16[...], preferred_element_type=F32) + bdn_ref[0]
        _rows_to_tiles(ybuf.at[slot], yb, half, h * half)

    @pl.when((e == 0) & (n_used > 0))
    def _():
        gather_start(0, 0)

    @pl.when(nbk_ref[e] > 0)
    def _():
        def cast_rows(c, carry):
            r = pl.multiple_of(c * LANES, LANES)
            wgu16[pl.ds(r, LANES), :] = wgu_ref[pl.ds(r, LANES), :].astype(BF16)
            wdn16[pl.ds(r, LANES), :] = wdn_ref[pl.ds(r, LANES), :].astype(BF16)
            return carry
        lax.fori_loop(0, D_MODEL // LANES, cast_rows, 0)

        def block(j, carry):
            b = b0_ref[e] + j
            slot = b % 2

            @pl.when(b + 1 < n_used)
            def _():
                gather_start(b + 1, 1 - slot)

            gather_wait(b, slot)

            @pl.when(b >= 2)
            def _():
                out_copy(b - 2, slot).wait()

            valid = bv_ref[b]
            expert_mlp(slot, 0, valid)

            @pl.when(valid > half)
            def _():
                expert_mlp(slot, 1, valid)

            @pl.when(valid <= half)
            def _():
                ybuf[slot, pl.ds(half * ROW_TILES, half * ROW_TILES), :] = jnp.zeros(
                    (half * ROW_TILES, LANES), F32)

            out_copy(b, slot).start()
            return carry
        lax.fori_loop(0, nbk_ref[e], block, 0)

    @pl.when(e == pl.num_programs(0) - 1)
    def _():
        for back in (1, 2):
            @pl.when(n_used >= back)
            def _():
                out_copy(n_used - back, (n_used - back) % 2).wait()


def _moe(tables, xs, slots, w_gu, b_gu, w_dn, b_dn):
    wsel = lambda e, *_: (e, 0, 0)
    grid_spec = pltpu.PrefetchScalarGridSpec(
        num_scalar_prefetch=len(tables),
        grid=(N_EXPERTS,),
        in_specs=[pl.BlockSpec(memory_space=pl.ANY),
                  pl.BlockSpec((None, D_MODEL, 2 * D_FF), wsel),
                  pl.BlockSpec((None, 1, 2 * D_FF), wsel),
                  pl.BlockSpec((None, D_FF, D_MODEL), wsel),
                  pl.BlockSpec((None, 1, D_MODEL), wsel)],
        out_specs=pl.BlockSpec(memory_space=pl.ANY),
        scratch_shapes=[pltpu.VMEM((2, MOE_BLOCK * ROW_TILES, LANES), F32),
                        pltpu.VMEM((2, MOE_BLOCK * ROW_TILES, LANES), F32),
                        pltpu.VMEM((D_MODEL, 2 * D_FF), BF16),
                        pltpu.VMEM((D_FF, D_MODEL), BF16),
                        pltpu.SemaphoreType.DMA((2,)),
                        pltpu.SemaphoreType.DMA((2,))],
    )
    return pl.pallas_call(
        _moe_kernel,
        grid_spec=grid_spec,
        out_shape=jax.ShapeDtypeStruct((slots * ROW_TILES, LANES), F32),
        compiler_params=pltpu.CompilerParams(
            dimension_semantics=("arbitrary",),
            vmem_limit_bytes=VMEM_LIMIT_MOE),
        name="moe",
    )(*tables, xs, w_gu, b_gu, w_dn, b_dn)


def _combine_kernel(csrc_ref, clen_ref, coff_ref, x1_ref, pos_ref, gate_ref, gfin_ref, ys_hbm,
                    out_ref, ybuf, sems, *, tm):
    i = pl.program_id(0)
    n_tiles = pl.num_programs(0)
    n_sorted = TOP_K * tm

    def gather_start(t, slot, enabled=None):
        for e in range(N_EXPERTS):
            s = t * N_EXPERTS + e
            src0 = csrc_ref[s]
            dst0 = coff_ref[s]
            n = clen_ref[s] if enabled is None else jnp.where(enabled, clen_ref[s], 0)

            def piece(first, rows, src0=src0, dst0=dst0):
                pltpu.make_async_copy(_rows(ys_hbm, src0 + first, rows),
                                      _rows(ybuf.at[slot], dst0 + first, rows), sems.at[slot]).start()
            _strip_pieces(n, tm, piece)

    @pl.when(i == 0)
    def _():
        gather_start(0, 0)

    slot = i % 2
    pltpu.make_async_copy(_rows(ys_hbm, 0, n_sorted), ybuf.at[slot], sems.at[slot]).wait()
    gather_start(jnp.minimum(i + 1, n_tiles - 1), (i + 1) % 2, i + 1 < n_tiles)
    y_sorted = _tiles_to_rows(ybuf.at[slot], n_sorted).astype(BF16)

    rr = lax.broadcasted_iota(I32, (tm, tm), 0)
    cc = lax.broadcasted_iota(I32, (tm, tm), 1)
    eye = rr == cc
    jl = lax.broadcasted_iota(I32, (tm, n_sorted), 1).astype(F32)
    unsort = jnp.zeros((tm, n_sorted), F32)
    for k in range(TOP_K):
        p_col = jnp.sum(jnp.where(eye, pos_ref[k:k + 1, :].astype(F32), 0.0), axis=1, keepdims=True)
        g_col = jnp.sum(jnp.where(eye, gate_ref[k:k + 1, :], 0.0), axis=1, keepdims=True)
        unsort = jnp.where(jl == p_col, g_col, unsort)
    acc = x1_ref[...] + jnp.dot(unsort.astype(BF16), y_sorted, preferred_element_type=F32)
    out_ref[...] = _rmsnorm(acc, gfin_ref[...])


def _combine(tables, x1, pos, gates, g_final, ys, tm):
    t = x1.shape[0]
    nt = t // tm
    grid_spec = pltpu.PrefetchScalarGridSpec(
        num_scalar_prefetch=len(tables),
        grid=(nt,),
        in_specs=[pl.BlockSpec((tm, D_MODEL), lambda i, *_: (i, 0)),
                  pl.BlockSpec((TOP_K, tm), lambda i, *_: (0, i)),
                  pl.BlockSpec((TOP_K, tm), lambda i, *_: (0, i)),
                  pl.BlockSpec((1, D_MODEL), lambda i, *_: (0, 0)),
                  pl.BlockSpec(memory_space=pl.ANY)],
        out_specs=pl.BlockSpec((tm, D_MODEL), lambda i, *_: (i, 0)),
        scratch_shapes=[pltpu.VMEM((2, TOP_K * tm * ROW_TILES, LANES), F32),
                        pltpu.SemaphoreType.DMA((2,))],
    )
    return pl.pallas_call(
        functools.partial(_combine_kernel, tm=tm),
        grid_spec=grid_spec,
        out_shape=jax.ShapeDtypeStruct((t, D_MODEL), F32),
        compiler_params=pltpu.CompilerParams(
            dimension_semantics=("arbitrary",),
            vmem_limit_bytes=VMEM_LIMIT_MIXER),
        name="combine",
    )(*tables, x1, pos, gates, g_final, ys)


def _split_bf16(w):
    hi = w.astype(BF16)
    lo = (w - hi.astype(F32)).astype(BF16)
    return hi, lo


def kernel(x_prompt, x_sample, cache_conv, cache_mem_k, cache_mem_v, mem_prompt, g_mix, w_in, w_dw, b_dw, ln_conv_g, ln_conv_b, ln_v_g, ln_v_b, w_spatial, b_spatial, g_mem, w_mem_k, w_mem_v, w_out, g_ffn, w_router, b_router, w_gate_up, b_gate_up, w_down, b_down, g_final):
    depth = g_mix.shape[0]
    assert depth == 1
    l = 0
    bp, seq, _ = x_prompt.shape
    bs, dseq, _ = x_sample.shape
    assert seq % PROMPT_TILE == 0 and bs % SAMPLE_SEQS_PER_TILE == 0
    assert GMLP_CHUNK % dseq == 0 and SAMPLE_SEQS_PER_TILE * dseq == GMLP_CHUNK

    row = lambda a: a.reshape(1, -1)
    wr_hi, wr_lo = _split_bf16(w_router[l].T)
    w_dw_pad = jnp.pad(w_dw[l], ((0, HIST - CONV_W), (0, 0)))
    bias_rows = lambda b: jnp.repeat(b.T, GMLP_HD, axis=1)
    common = dict(
        gmix=row(g_mix[l]), win=w_in[l].astype(BF16), wdw=w_dw_pad, bdw=row(b_dw[l]),
        lcg=row(ln_conv_g[l]), lcb=row(ln_conv_b[l]), lvg=row(ln_v_g[l]), lvb=row(ln_v_b[l]),
        wout=w_out[l].astype(BF16), gffn=row(g_ffn[l]), wrh=wr_hi, wrl=wr_lo,
        br=b_router[l].reshape(N_EXPERTS, 1))

    def weights(wsp, bsp):
        c = common
        return (c["gmix"], c["win"], c["wdw"], c["bdw"], c["lcg"], c["lcb"], c["lvg"], c["lvb"],
                wsp, bsp, c["wout"], c["gffn"], c["wrh"], c["wrl"], c["br"])

    reps = GMLP_CHUNK // dseq
    wts_p = weights(w_spatial[l], bias_rows(b_spatial[l]))
    wts_s = weights(jnp.tile(w_spatial[l][:, :dseq, :dseq], (1, reps, reps)),
                    bias_rows(jnp.tile(b_spatial[l][:, :dseq], (1, reps))))

    w_kv = jnp.concatenate([w_mem_k[l], w_mem_v[l]], axis=1).astype(BF16)
    kv_p = _memkv(mem_prompt, row(g_mem[l]), w_kv)
    mk_p = kv_p[:, :, :C_XA]
    mv_p = kv_p[:, :, C_XA:]
    zero_hist = jnp.zeros((bp, 1, HIST, C_CONV), F32)
    tp, ts = bp * seq, bs * dseq
    n_assign = (tp + ts) * TOP_K
    (x1_p, xs, pos_p, gate_p, cnt_p, off_p, hist_p, _) = _mixer(
        x_prompt, zero_hist, mk_p[:, None], mv_p[:, None], wts_p,
        ns=1, sl=PROMPT_TILE, carry=True, sp_chunk=GMLP_CHUNK,
        sorted_rows_total=n_assign, sorted_row0=0, xs_prev=None)

    gs = bs // SAMPLE_SEQS_PER_TILE
    hist_s_in = jnp.pad(cache_conv[l], ((0, 0), (HIST_OFF, 0), (0, 0))).reshape(
        gs, SAMPLE_SEQS_PER_TILE, HIST, C_CONV)
    mk_s = cache_mem_k[l].reshape(gs, SAMPLE_SEQS_PER_TILE, N_MEM, C_XA)
    mv_s = cache_mem_v[l].reshape(gs, SAMPLE_SEQS_PER_TILE, N_MEM, C_XA)
    tm_s = SAMPLE_SEQS_PER_TILE * dseq
    (x1_s, xs, pos_s, gate_s, cnt_s, off_s, hist_s, v_s) = _mixer(
        x_sample.reshape(gs, tm_s, D_MODEL), hist_s_in, mk_s, mv_s, wts_s,
        ns=SAMPLE_SEQS_PER_TILE, sl=dseq, carry=False, sp_chunk=dseq,
        sorted_rows_total=n_assign, sorted_row0=tp * TOP_K, xs_prev=xs)

    n_blocks = -(-n_assign // MOE_BLOCK) + N_EXPERTS
    ntp, nts = tp // PROMPT_TILE, ts // tm_s
    cnt = jnp.concatenate([cnt_p[:, :, 0], cnt_s[:, :, 0]], axis=0)
    off = jnp.concatenate([off_p[:, :, 0], off_s[:, :, 0]], axis=0)
    tile_row0 = np.concatenate([np.arange(ntp) * (TOP_K * PROMPT_TILE),
                                tp * TOP_K + np.arange(nts) * (TOP_K * tm_s)]).astype(np.int32)
    counts = jnp.sum(cnt, axis=0)
    tile_base = jnp.cumsum(cnt, axis=0) - cnt
    padded = (counts + MOE_BLOCK - 1) // MOE_BLOCK * MOE_BLOCK
    pad_end = jnp.cumsum(padded)
    pad_start = pad_end - padded
    strip_dst = (pad_start[None, :] + tile_base).astype(I32)
    strip_src = (tile_row0[:, None] + off).astype(I32)
    n_used = (pad_end[-1] // MOE_BLOCK).astype(I32)
    blk_row0 = jnp.minimum(jnp.arange(n_blocks, dtype=I32), n_used - 1) * MOE_BLOCK
    block_expert = jnp.minimum(
        jnp.sum((pad_end[None, :] <= blk_row0[:, None]).astype(I32), axis=1), N_EXPERTS - 1)
    of_block = block_expert[:, None] == jnp.arange(N_EXPERTS, dtype=I32)[None, :]
    last_row = jnp.sum(jnp.where(of_block, (pad_start + counts)[None, :], 0), axis=1)
    block_valid = jnp.clip(last_row - blk_row0, 0, MOE_BLOCK).astype(I32)
    sdst = strip_dst.T.reshape(-1)
    ssrc = strip_src.T.reshape(-1)
    slen = cnt.T.reshape(-1).astype(I32)
    s_lo = jnp.sum(((sdst + slen)[None, :] <= blk_row0[:, None]).astype(I32), axis=1)
    s_hi = jnp.sum((sdst[None, :] < (blk_row0 + MOE_BLOCK)[:, None]).astype(I32), axis=1)
    moe_tables = ((pad_start // MOE_BLOCK).astype(I32), (padded // MOE_BLOCK).astype(I32),
                  block_valid, n_used.reshape(1),
                  s_lo.astype(I32), s_hi.astype(I32), ssrc, sdst, slen)

    slots = n_blocks * MOE_BLOCK
    ys = _moe(moe_tables, xs, slots,
              w_gate_up[l], b_gate_up[l][:, None, :], w_down[l], b_down[l][:, None, :])
    gfin = row(g_final)

    def combine_tables(t0, t1):
        return (strip_dst[t0:t1].reshape(-1), cnt[t0:t1].reshape(-1).astype(I32),
                off[t0:t1].reshape(-1).astype(I32))

    y_p = _combine(combine_tables(0, ntp), x1_p, pos_p, gate_p, gfin, ys, PROMPT_TILE)
    y_s = _combine(combine_tables(ntp, ntp + nts), x1_s, pos_s, gate_s, gfin, ys, tm_s)

    return (y_p.reshape(bp, seq, D_MODEL),
            y_s.reshape(bs, dseq, D_MODEL),
            hist_p[:, 0, HIST_OFF:, :][None],
            mk_p.reshape(bp, N_MEM, XA_HEADS, XA_HD)[None],
            mv_p.reshape(bp, N_MEM, XA_HEADS, XA_HD)[None],
            hist_s.reshape(bs, HIST, C_CONV)[:, HIST_OFF:, :][None],
            v_s.reshape(bs, dseq, C_GMLP)[None])
```

```python
import functools

import numpy as np
import jax
import jax.numpy as jnp
from jax import lax
from jax.experimental import pallas as pl
from jax.experimental.pallas import tpu as pltpu

F32 = jnp.float32
BF16 = jnp.bfloat16
I32 = jnp.int32

D_MODEL = 1024
C_CONV = 384
CONV_W = 31
C_GMLP = 384
GMLP_HEADS = 4
GMLP_HD = 96
GMLP_CHUNK = 128
XA_HEADS = 4
XA_HD = 64
C_XA = 256
N_MEM = 256
N_EXPERTS = 32
TOP_K = 4
D_FF = 1024
SWIGLU_LIMIT = 7.0
SWIGLU_ALPHA = 1.702
EPS = 1e-5
IN_COLS = 2 * C_CONV + 2 * C_GMLP + C_XA

SUBLANES = 8
LANES = 128
ROW_TILES = D_MODEL // LANES
HIST = 32
HIST_OFF = HIST - (CONV_W - 1)

PROMPT_TILE = 512
SAMPLE_SEQS_PER_TILE = 8
MOE_BLOCK = 512
VMEM_LIMIT_MIXER = 48 * 1024 * 1024
VMEM_LIMIT_MOE = 52 * 1024 * 1024


def _rmsnorm(x, g):
    return x * lax.rsqrt(jnp.mean(x * x, axis=-1, keepdims=True) + EPS) * g


def _layernorm(x, g, b):
    mu = jnp.mean(x, axis=-1, keepdims=True)
    xc = x - mu
    var = jnp.mean(xc * xc, axis=-1, keepdims=True)
    return xc * lax.rsqrt(var + EPS) * g + b


def _gelu(x):
    return 0.5 * x * (1.0 + lax.erf(x * np.float32(1.0 / np.sqrt(2.0))))


def _sigmoid(x):
    return 1.0 / (1.0 + jnp.exp(-x))


def _rows_to_tiles(dst_ref, val, rows, row0=0):
    for j in range(ROW_TILES):
        dst_ref[pl.ds(row0 * ROW_TILES + j, rows, stride=ROW_TILES), :] = val[:, j * LANES:(j + 1) * LANES]


def _tiles_to_rows(src_ref, rows, row0=0):
    return jnp.concatenate(
        [src_ref[pl.ds(row0 * ROW_TILES + j, rows, stride=ROW_TILES), :] for j in range(ROW_TILES)],
        axis=-1)


def _memkv_kernel(mem_ref, g_ref, w_ref, o_ref):
    mn = _rmsnorm(mem_ref[0], g_ref[...])
    o_ref[0] = jnp.dot(mn.astype(BF16), w_ref[...], preferred_element_type=F32)


def _memkv(mem, g_mem, w_kv):
    b = mem.shape[0]
    return pl.pallas_call(
        _memkv_kernel,
        grid=(b,),
        in_specs=[pl.BlockSpec((1, N_MEM, D_MODEL), lambda i: (i, 0, 0)),
                  pl.BlockSpec((1, D_MODEL), lambda i: (0, 0)),
                  pl.BlockSpec((D_MODEL, 2 * C_XA), lambda i: (0, 0))],
        out_specs=pl.BlockSpec((1, N_MEM, 2 * C_XA), lambda i: (i, 0, 0)),
        out_shape=jax.ShapeDtypeStruct((b, N_MEM, 2 * C_XA), F32),
        name="memkv",
    )(mem, g_mem, w_kv)


def _mixer_kernel(x_ref, hist_ref, mk_ref, mv_ref, gmix_ref, win_ref, wdw_ref, bdw_ref,
                  lcg_ref, lcb_ref, lvg_ref, lvb_ref, wsp_ref, bsp_ref, wout_ref, gffn_ref,
                  wrh_ref, wrl_ref, br_ref, *rest, ns, sl, carry, sp_chunk, tiles_per_seq):
    (x1_ref, xst_ref, pos_ref, gate_ref, cnt_ref, off_ref, histout_ref, v_ref,
     ext_ref, shift_ref, h2_keep, pos_keep) = rest[-12:]
    tm = ns * sl
    step = pl.program_id(0)
    n_tiles = pl.num_programs(0) - 1

    def sort_previous_tile():
        jj = lax.broadcasted_iota(I32, (TOP_K * tm, tm), 0)
        perm = jnp.zeros((TOP_K * tm, tm), F32)
        for k in range(TOP_K):
            perm = jnp.where(jj == pos_keep[k:k + 1, :], 1.0, perm)
        sorted_rows = jnp.dot(perm.astype(BF16), h2_keep[...], preferred_element_type=F32)
        _rows_to_tiles(xst_ref, sorted_rows, TOP_K * tm)

    @pl.when(step == 0)
    def _():
        h2_keep[...] = jnp.zeros_like(h2_keep)
        pos_keep[...] = jnp.full(pos_keep.shape, -1, I32)

    @pl.when(step < n_tiles)
    def _():
        sort_previous_tile()
        _mixer_tile(x_ref, hist_ref, mk_ref, mv_ref, gmix_ref, win_ref, wdw_ref, bdw_ref,
                    lcg_ref, lcb_ref, lvg_ref, lvb_ref, wsp_ref, bsp_ref, wout_ref, gffn_ref,
                    wrh_ref, wrl_ref, br_ref, x1_ref, pos_ref, gate_ref, cnt_ref, off_ref,
                    histout_ref, v_ref, ext_ref, shift_ref, h2_keep, pos_keep,
                    first_of_seq=(step % tiles_per_seq) == 0,
                    ns=ns, sl=sl, carry=carry, sp_chunk=sp_chunk)

    @pl.when(step == n_tiles)
    def _():
        sort_previous_tile()


def _mixer_tile(x_ref, hist_ref, mk_ref, mv_ref, gmix_ref, win_ref, wdw_ref, bdw_ref,
                lcg_ref, lcb_ref, lvg_ref, lvb_ref, wsp_ref, bsp_ref, wout_ref, gffn_ref,
                wrh_ref, wrl_ref, br_ref, x1_ref, pos_ref, gate_ref, cnt_ref, off_ref,
                histout_ref, v_ref, ext_ref, shift_ref, h2_keep, pos_keep,
                *, first_of_seq, ns, sl, carry, sp_chunk):
    tm = ns * sl
    x = x_ref[0]
    h = _rmsnorm(x, gmix_ref[...])
    z = jnp.dot(h.astype(BF16), win_ref[...], preferred_element_type=F32)
    z_a = z[:, 0:C_CONV]
    z_g = z[:, C_CONV:2 * C_CONV]
    z_u = z[:, 2 * C_CONV:2 * C_CONV + C_GMLP]
    z_v = z[:, 2 * C_CONV + C_GMLP:2 * C_CONV + 2 * C_GMLP]
    z_q = z[:, 2 * C_CONV + 2 * C_GMLP:IN_COLS]

    glu = z_a * _sigmoid(z_g)
    if carry:
        ext_ref[:, 0:HIST, :] = jnp.where(first_of_seq, hist_ref[0], ext_ref[:, 0:HIST, :])
    else:
        ext_ref[:, 0:HIST, :] = hist_ref[0]
    conv_parts = []
    for s in range(ns):
        ext_s = ext_ref.at[s]
        ext_s[HIST:HIST + sl, :] = glu[s * sl:(s + 1) * sl]
        n_shift = HIST + sl - SUBLANES
        for r in range(1, SUBLANES):
            shift_ref[s, r - 1, 0:n_shift, :] = ext_s[pl.ds(r, n_shift), :]
        rc = min(sl, 64)
        for r0 in range(0, sl, rc):
            acc = jnp.broadcast_to(bdw_ref[...], (rc, C_CONV))
            for j in range(CONV_W):
                a, r = divmod(j + HIST_OFF, SUBLANES)
                src = ext_s if r == 0 else shift_ref.at[s, r - 1]
                acc = acc + wdw_ref[j:j + 1, :] * src[pl.ds(r0 + a * SUBLANES, rc), :]
            conv_parts.append(acc)
        new_hist = ext_s[sl:sl + HIST, :]
        histout_ref[0, s] = new_hist
        if carry:
            ext_s[0:HIST, :] = new_hist
    y = jnp.concatenate(conv_parts, axis=0) if len(conv_parts) > 1 else conv_parts[0]
    y = _layernorm(y, lcg_ref[...], lcb_ref[...])
    c_out = y * _sigmoid(y)

    u = _gelu(z_u)
    v = _layernorm(_gelu(z_v), lvg_ref[...], lvb_ref[...])
    v_ref[...] = v
    vb = v.astype(BF16)
    rr = lax.broadcasted_iota(I32, (GMLP_CHUNK, GMLP_CHUNK), 0)
    cc = lax.broadcasted_iota(I32, (GMLP_CHUNK, GMLP_CHUNK), 1)
    sp_mask = (cc <= rr) & ((rr // sp_chunk) == (cc // sp_chunk))
    col = lax.broadcasted_iota(I32, (GMLP_CHUNK, C_GMLP), 1)
    w_heads = [jnp.where(sp_mask, wsp_ref[hh], 0.0).astype(BF16) for hh in range(GMLP_HEADS)]
    g_parts = []
    for c in range(tm // GMLP_CHUNK):
        vc = vb[c * GMLP_CHUNK:(c + 1) * GMLP_CHUNK]
        sg = bsp_ref[...]
        for hh in range(GMLP_HEADS):
            head_cols = (col >= hh * GMLP_HD) & (col < (hh + 1) * GMLP_HD)
            vh = jnp.where(head_cols, vc, jnp.zeros_like(vc))
            sg = sg + jnp.dot(w_heads[hh], vh, preferred_element_type=F32)
        g_parts.append(u[c * GMLP_CHUNK:(c + 1) * GMLP_CHUNK] * sg)
    g_out = jnp.concatenate(g_parts, axis=0) if len(g_parts) > 1 else g_parts[0]

    qs = z_q * np.float32(XA_HD ** -0.5)
    qcol = lax.broadcasted_iota(I32, (sl, C_XA), 1)
    a_parts = []
    for s in range(ns):
        q_s = qs[s * sl:(s + 1) * sl]
        kb = mk_ref[0, s].astype(BF16)
        vvb = mv_ref[0, s].astype(BF16)
        hmasks = [(qcol >= hh * XA_HD) & (qcol < (hh + 1) * XA_HD) for hh in range(XA_HEADS)]
        stack = XA_HEADS if sl * XA_HEADS <= N_MEM else 1
        a_s = jnp.zeros((sl, C_XA), F32)
        for h0 in range(0, XA_HEADS, stack):
            heads = range(h0, h0 + stack)
            qh = jnp.concatenate([jnp.where(hmasks[hh], q_s, 0.0) for hh in heads], axis=0).astype(BF16)
            sc = lax.dot_general(qh, kb, (((1,), (1,)), ((), ())), preferred_element_type=F32)
            p = jnp.exp(sc - jnp.max(sc, axis=-1, keepdims=True))
            den = jnp.sum(p, axis=-1, keepdims=True)
            oh = jnp.dot(p.astype(BF16), vvb, preferred_element_type=F32) / den
            for n, hh in enumerate(heads):
                a_s = a_s + jnp.where(hmasks[hh], oh[n * sl:(n + 1) * sl], 0.0)
        a_parts.append(a_s)
    a_out = jnp.concatenate(a_parts, axis=0) if len(a_parts) > 1 else a_parts[0]

    mix = jnp.concatenate([c_out, g_out, a_out], axis=-1).astype(BF16)
    x1 = x + jnp.dot(mix, wout_ref[...], preferred_element_type=F32)
    x1_ref[...] = x1

    h2 = _rmsnorm(x1, gffn_ref[...])
    h2_hi = h2.astype(BF16)
    h2_lo = (h2 - h2_hi.astype(F32)).astype(BF16)
    nt_dims = (((1,), (1,)), ((), ()))
    lg = (lax.dot_general(wrh_ref[...], h2_hi, nt_dims, preferred_element_type=F32)
          + lax.dot_general(wrl_ref[...], h2_hi, nt_dims, preferred_element_type=F32)
          + lax.dot_general(wrh_ref[...], h2_lo, nt_dims, preferred_element_type=F32)
          + br_ref[...])
    eio = lax.broadcasted_iota(I32, (N_EXPERTS, tm), 0)
    work = lg
    vals, idxs = [], []
    for _ in range(TOP_K):
        m = jnp.max(work, axis=0, keepdims=True)
        idx = jnp.min(jnp.where(work == m, eio, N_EXPERTS), axis=0, keepdims=True)
        vals.append(m)
        idxs.append(idx)
        work = jnp.where(eio == idx, -jnp.inf, work)
    exps = [jnp.exp(vk - vals[0]) for vk in vals]
    den = exps[0] + exps[1] + exps[2] + exps[3]
    gate_ref[...] = jnp.concatenate([ek / den for ek in exps], axis=0)

    sel = jnp.zeros((N_EXPERTS, tm), F32)
    for idx in idxs:
        sel = sel + jnp.where(eio == idx, 1.0, 0.0)
    selb = sel.astype(BF16)
    tr = lax.broadcasted_iota(I32, (tm, tm), 0)
    tc = lax.broadcasted_iota(I32, (tm, tm), 1)
    before = jnp.where(tr < tc, 1.0, 0.0).astype(BF16)
    ranks = jnp.dot(selb, before, preferred_element_type=F32)
    er = lax.broadcasted_iota(I32, (N_EXPERTS, N_EXPERTS), 0)
    ec = lax.broadcasted_iota(I32, (N_EXPERTS, N_EXPERTS), 1)
    lower = jnp.where(ec < er, 1.0, 0.0).astype(BF16)
    cnt = jnp.sum(sel, axis=1, keepdims=True)
    off = jnp.sum(jnp.dot(lower, selb, preferred_element_type=F32), axis=1, keepdims=True)
    cnt_ref[0] = jnp.broadcast_to(cnt, (N_EXPERTS, LANES)).astype(I32)
    off_ref[0] = jnp.broadcast_to(off, (N_EXPERTS, LANES)).astype(I32)
    slot_of = off + ranks
    pos = [jnp.sum(jnp.where(eio == idx, slot_of, 0.0), axis=0, keepdims=True).astype(I32) for idx in idxs]
    pos_all = jnp.concatenate(pos, axis=0)
    pos_ref[...] = pos_all
    pos_keep[...] = pos_all
    h2_keep[...] = h2_hi


def _mixer(x, hist, mem_k, mem_v, wts, *, ns, sl, carry, sp_chunk, sorted_rows_total, sorted_row0, xs_prev):
    g, r, _ = x.shape
    tm = ns * sl
    nt = r // tm
    ntot = g * nt
    assert sorted_row0 % (TOP_K * tm) == 0
    xs_blk0 = sorted_row0 // (TOP_K * tm)
    tile = lambda s: jnp.minimum(s, ntot - 1)
    const2 = lambda s: (0, 0)
    const3 = lambda s: (0, 0, 0)
    tile_row = lambda s: (tile(s), 0)
    tile_lane = lambda s: (0, tile(s))
    per_seq = lambda s: (tile(s) // nt, 0, 0, 0)
    in_specs = [
        pl.BlockSpec((1, tm, D_MODEL), lambda s: (tile(s) // nt, tile(s) % nt, 0)),
        pl.BlockSpec((1, ns, HIST, C_CONV), per_seq),
        pl.BlockSpec((1, ns, N_MEM, C_XA), per_seq),
        pl.BlockSpec((1, ns, N_MEM, C_XA), per_seq),
        pl.BlockSpec((1, D_MODEL), const2),
        pl.BlockSpec((D_MODEL, IN_COLS), const2),
        pl.BlockSpec((HIST, C_CONV), const2),
        pl.BlockSpec((1, C_CONV), const2),
        pl.BlockSpec((1, C_CONV), const2),
        pl.BlockSpec((1, C_CONV), const2),
        pl.BlockSpec((1, C_GMLP), const2),
        pl.BlockSpec((1, C_GMLP), const2),
        pl.BlockSpec((GMLP_HEADS, GMLP_CHUNK, GMLP_CHUNK), const3),
        pl.BlockSpec((GMLP_CHUNK, C_GMLP), const2),
        pl.BlockSpec((D_MODEL, D_MODEL), const2),
        pl.BlockSpec((1, D_MODEL), const2),
        pl.BlockSpec((N_EXPERTS, D_MODEL), const2),
        pl.BlockSpec((N_EXPERTS, D_MODEL), const2),
        pl.BlockSpec((N_EXPERTS, 1), const2),
    ]
    tile_cnt = lambda s: (tile(s), 0, 0)
    out_specs = [
        pl.BlockSpec((tm, D_MODEL), tile_row),
        pl.BlockSpec((TOP_K * tm * ROW_TILES, LANES),
                     lambda s: (xs_blk0 + jnp.maximum(s - 1, 0), 0)),
        pl.BlockSpec((TOP_K, tm), tile_lane),
        pl.BlockSpec((TOP_K, tm), tile_lane),
        pl.BlockSpec((1, N_EXPERTS, LANES), tile_cnt),
        pl.BlockSpec((1, N_EXPERTS, LANES), tile_cnt),
        pl.BlockSpec((1, ns, HIST, C_CONV), per_seq),
        pl.BlockSpec((tm, C_GMLP), tile_row),
    ]
    rows = g * r
    out_shape = [
        jax.ShapeDtypeStruct((rows, D_MODEL), F32),
        jax.ShapeDtypeStruct((sorted_rows_total * ROW_TILES, LANES), F32),
        jax.ShapeDtypeStruct((TOP_K, rows), I32),
        jax.ShapeDtypeStruct((TOP_K, rows), F32),
        jax.ShapeDtypeStruct((ntot, N_EXPERTS, LANES), I32),
        jax.ShapeDtypeStruct((ntot, N_EXPERTS, LANES), I32),
        jax.ShapeDtypeStruct((g, ns, HIST, C_CONV), F32),
        jax.ShapeDtypeStruct((rows, C_GMLP), F32),
    ]
    args = [x, hist, mem_k, mem_v, *wts]
    aliases = {}
    if xs_prev is not None:
        in_specs.append(pl.BlockSpec(memory_space=pl.ANY))
        aliases = {len(args): 1}
        args.append(xs_prev)
    kern = functools.partial(_mixer_kernel, ns=ns, sl=sl, carry=carry, sp_chunk=sp_chunk,
                             tiles_per_seq=nt)
    return pl.pallas_call(
        kern,
        grid=(ntot + 1,),
        in_specs=in_specs,
        out_specs=out_specs,
        out_shape=out_shape,
        scratch_shapes=[pltpu.VMEM((ns, HIST + sl, C_CONV), F32),
                        pltpu.VMEM((ns, SUBLANES - 1, HIST + sl, C_CONV), F32),
                        pltpu.VMEM((tm, D_MODEL), BF16),
                        pltpu.VMEM((TOP_K, tm), I32)],
        input_output_aliases=aliases,
        compiler_params=pltpu.CompilerParams(
            dimension_semantics=("arbitrary",),
            vmem_limit_bytes=VMEM_LIMIT_MIXER),
        name="mixer_carry" if carry else "mixer_cache",
    )(*args)


def _strip_pieces(n, max_rows, fn):
    done = 0
    p = max_rows
    while p >= 1:
        has = (n & p) != 0
        pl.when(has)(functools.partial(fn, done, p))
        done = done + jnp.where(has, p, 0)
        p //= 2


def _rows(ref, first_row, rows):
    return ref.at[pl.ds(pl.multiple_of(first_row * ROW_TILES, ROW_TILES), rows * ROW_TILES)]


def _moe_kernel(b0_ref, nbk_ref, bv_ref, nb_ref, slo_ref, shi_ref, ssrc_ref, sdst_ref, slen_ref,
                xs_hbm, wgu_ref, bgu_ref, wdn_ref, bdn_ref, ys_hbm,
                xbuf, ybuf, wgu16, wdn16, sems, ysems):
    e = pl.program_id(0)
    n_used = nb_ref[0]
    half = MOE_BLOCK // 2

    def gather_start(b, slot):
        row0 = b * MOE_BLOCK

        def strip(s, carry):
            lo = jnp.maximum(sdst_ref[s], row0)
            hi = jnp.minimum(sdst_ref[s] + slen_ref[s], row0 + MOE_BLOCK)
            src0 = ssrc_ref[s] + (lo - sdst_ref[s])
            dst0 = lo - row0

            def piece(first, rows):
                pltpu.make_async_copy(_rows(xs_hbm, src0 + first, rows),
                                      _rows(xbuf.at[slot], dst0 + first, rows), sems.at[slot]).start()
            _strip_pieces(hi - lo, MOE_BLOCK, piece)
            return carry
        lax.fori_loop(slo_ref[b], shi_ref[b], strip, 0)

    def gather_wait(b, slot):
        def piece(first, rows):
            del first
            pltpu.make_async_copy(_rows(xs_hbm, 0, rows), _rows(xbuf.at[slot], 0, rows), sems.at[slot]).wait()
        _strip_pieces(bv_ref[b], MOE_BLOCK, piece)

    def out_copy(b, slot):
        return pltpu.make_async_copy(ybuf.at[slot], _rows(ys_hbm, b * MOE_BLOCK, MOE_BLOCK), ysems.at[slot])

    def expert_mlp(slot, h, valid):
        row_ok = lax.broadcasted_iota(I32, (half, 1), 0) + h * half < valid
        xb = jnp.where(row_ok, _tiles_to_rows(xbuf.at[slot], half, h * half), 0.0).astype(BF16)
        gu = jnp.dot(xb, wgu16[...], preferred_element_type=F32) + bgu_ref[0]
        gate = jnp.minimum(gu[:, :D_FF], SWIGLU_LIMIT)
        up = jnp.clip(gu[:, D_FF:], -SWIGLU_LIMIT, SWIGLU_LIMIT)
        act = (up + 1.0) * (gate * _sigmoid(SWIGLU_ALPHA * gate))
        yb = jnp.dot(act.astype(BF16), wdn16[...], preferred_element_type=F32) + bdn_ref[0]
        _rows_to_tiles(ybuf.at[slot], yb, half, h * half)

    @pl.when((e == 0) & (n_used > 0))
    def _():
        gather_start(0, 0)

    @pl.when(nbk_ref[e] > 0)
    def _():
        def cast_rows(c, carry):
            r = pl.multiple_of(c * LANES, LANES)
            wgu16[pl.ds(r, LANES), :] = wgu_ref[pl.ds(r, LANES), :].astype(BF16)
            wdn16[pl.ds(r, LANES), :] = wdn_ref[pl.ds(r, LANES), :].astype(BF16)
            return carry
        lax.fori_loop(0, D_MODEL // LANES, cast_rows, 0)

        def block(j, carry):
            b = b0_ref[e] + j
            slot = b % 2

            @pl.when(b + 1 < n_used)
            def _():
                gather_start(b + 1, 1 - slot)

            gather_wait(b, slot)

            @pl.when(b >= 2)
            def _():
                out_copy(b - 2, slot).wait()

            valid = bv_ref[b]
            expert_mlp(slot, 0, valid)

            @pl.when(valid > half)
            def _():
                expert_mlp(slot, 1, valid)

            @pl.when(valid <= half)
            def _():
                ybuf[slot, pl.ds(half * ROW_TILES, half * ROW_TILES), :] = jnp.zeros(
                    (half * ROW_TILES, LANES), F32)

            out_copy(b, slot).start()
            return carry
        lax.fori_loop(0, nbk_ref[e], block, 0)

    @pl.when(e == pl.num_programs(0) - 1)
    def _():
        for back in (1, 2):
            @pl.when(n_used >= back)
            def _():
                out_copy(n_used - back, (n_used - back) % 2).wait()


def _moe(tables, xs, slots, w_gu, b_gu, w_dn, b_dn):
    wsel = lambda e, *_: (e, 0, 0)
    grid_spec = pltpu.PrefetchScalarGridSpec(
        num_scalar_prefetch=len(tables),
        grid=(N_EXPERTS,),
        in_specs=[pl.BlockSpec(memory_space=pl.ANY),
                  pl.BlockSpec((None, D_MODEL, 2 * D_FF), wsel),
                  pl.BlockSpec((None, 1, 2 * D_FF), wsel),
                  pl.BlockSpec((None, D_FF, D_MODEL), wsel),
                  pl.BlockSpec((None, 1, D_MODEL), wsel)],
        out_specs=pl.BlockSpec(memory_space=pl.ANY),
        scratch_shapes=[pltpu.VMEM((2, MOE_BLOCK * ROW_TILES, LANES), F32),
                        pltpu.VMEM((2, MOE_BLOCK * ROW_TILES, LANES), F32),
                        pltpu.VMEM((D_MODEL, 2 * D_FF), BF16),
                        pltpu.VMEM((D_FF, D_MODEL), BF16),
                        pltpu.SemaphoreType.DMA((2,)),
                        pltpu.SemaphoreType.DMA((2,))],
    )
    return pl.pallas_call(
        _moe_kernel,
        grid_spec=grid_spec,
        out_shape=jax.ShapeDtypeStruct((slots * ROW_TILES, LANES), F32),
        compiler_params=pltpu.CompilerParams(
            dimension_semantics=("arbitrary",),
            vmem_limit_bytes=VMEM_LIMIT_MOE),
        name="moe",
    )(*tables, xs, w_gu, b_gu, w_dn, b_dn)


def _combine_kernel(csrc_ref, clen_ref, coff_ref, x1_ref, pos_ref, gate_ref, gfin_ref, ys_hbm,
                    out_ref, ybuf, sems, *, tm):
    i = pl.program_id(0)
    n_tiles = pl.num_programs(0)
    n_sorted = TOP_K * tm

    def gather_start(t, slot):
        def strip(e, carry):
            s = t * N_EXPERTS + e
            src0 = csrc_ref[s]
            dst0 = coff_ref[s]

            def piece(first, rows):
                pltpu.make_async_copy(_rows(ys_hbm, src0 + first, rows),
                                      _rows(ybuf.at[slot], dst0 + first, rows), sems.at[slot]).start()
            _strip_pieces(clen_ref[s], tm, piece)
            return carry
        lax.fori_loop(0, N_EXPERTS, strip, 0)

    @pl.when(i == 0)
    def _():
        gather_start(0, 0)

    @pl.when(i + 1 < n_tiles)
    def _():
        gather_start(i + 1, (i + 1) % 2)

    slot = i % 2
    pltpu.make_async_copy(_rows(ys_hbm, 0, n_sorted), ybuf.at[slot], sems.at[slot]).wait()
    rr = lax.broadcasted_iota(I32, (tm, tm), 0)
    cc = lax.broadcasted_iota(I32, (tm, tm), 1)
    eye = rr == cc
    p_cols = [jnp.sum(jnp.where(eye, pos_ref[k:k + 1, :].astype(F32), 0.0), axis=1, keepdims=True)
              for k in range(TOP_K)]
    g_cols = [jnp.sum(jnp.where(eye, gate_ref[k:k + 1, :], 0.0), axis=1, keepdims=True)
              for k in range(TOP_K)]
    acc = x1_ref[...]
    for c in range(TOP_K):
        jl = (lax.broadcasted_iota(I32, (tm, tm), 1) + c * tm).astype(F32)
        unsort = jnp.zeros((tm, tm), F32)
        for k in range(TOP_K):
            unsort = jnp.where(jl == p_cols[k], g_cols[k], unsort)
        y_chunk = _tiles_to_rows(ybuf.at[slot], tm, c * tm).astype(BF16)
        acc = acc + jnp.dot(unsort.astype(BF16), y_chunk, preferred_element_type=F32)
    out_ref[...] = _rmsnorm(acc, gfin_ref[...])


def _combine(tables, x1, pos, gates, g_final, ys, tm):
    t = x1.shape[0]
    nt = t // tm
    grid_spec = pltpu.PrefetchScalarGridSpec(
        num_scalar_prefetch=len(tables),
        grid=(nt,),
        in_specs=[pl.BlockSpec((tm, D_MODEL), lambda i, *_: (i, 0)),
                  pl.BlockSpec((TOP_K, tm), lambda i, *_: (0, i)),
                  pl.BlockSpec((TOP_K, tm), lambda i, *_: (0, i)),
                  pl.BlockSpec((1, D_MODEL), lambda i, *_: (0, 0)),
                  pl.BlockSpec(memory_space=pl.ANY)],
        out_specs=pl.BlockSpec((tm, D_MODEL), lambda i, *_: (i, 0)),
        scratch_shapes=[pltpu.VMEM((2, TOP_K * tm * ROW_TILES, LANES), F32),
                        pltpu.SemaphoreType.DMA((2,))],
    )
    return pl.pallas_call(
        functools.partial(_combine_kernel, tm=tm),
        grid_spec=grid_spec,
        out_shape=jax.ShapeDtypeStruct((t, D_MODEL), F32),
        compiler_params=pltpu.CompilerParams(
            dimension_semantics=("arbitrary",),
            vmem_limit_bytes=VMEM_LIMIT_MIXER),
        name="combine",
    )(*tables, x1, pos, gates, g_final, ys)


def _split_bf16(w):
    hi = w.astype(BF16)
    lo = (w - hi.astype(F32)).astype(BF16)
    return hi, lo


def kernel(x_prompt, x_sample, cache_conv, cache_mem_k, cache_mem_v, mem_prompt, g_mix, w_in, w_dw, b_dw, ln_conv_g, ln_conv_b, ln_v_g, ln_v_b, w_spatial, b_spatial, g_mem, w_mem_k, w_mem_v, w_out, g_ffn, w_router, b_router, w_gate_up, b_gate_up, w_down, b_down, g_final):
    depth = g_mix.shape[0]
    assert depth == 1
    l = 0
    bp, seq, _ = x_prompt.shape
    bs, dseq, _ = x_sample.shape
    assert seq % PROMPT_TILE == 0 and bs % SAMPLE_SEQS_PER_TILE == 0
    assert GMLP_CHUNK % dseq == 0 and (SAMPLE_SEQS_PER_TILE * dseq) % GMLP_CHUNK == 0

    row = lambda a: a.reshape(1, -1)
    wr_hi, wr_lo = _split_bf16(w_router[l].T)
    w_dw_pad = jnp.pad(w_dw[l], ((0, HIST - CONV_W), (0, 0)))
    bias_rows = lambda b: jnp.repeat(b.T, GMLP_HD, axis=1)
    common = dict(
        gmix=row(g_mix[l]), win=w_in[l].astype(BF16), wdw=w_dw_pad, bdw=row(b_dw[l]),
        lcg=row(ln_conv_g[l]), lcb=row(ln_conv_b[l]), lvg=row(ln_v_g[l]), lvb=row(ln_v_b[l]),
        wout=w_out[l].astype(BF16), gffn=row(g_ffn[l]), wrh=wr_hi, wrl=wr_lo,
        br=b_router[l].reshape(N_EXPERTS, 1))

    def weights(wsp, bsp):
        c = common
        return (c["gmix"], c["win"], c["wdw"], c["bdw"], c["lcg"], c["lcb"], c["lvg"], c["lvb"],
                wsp, bsp, c["wout"], c["gffn"], c["wrh"], c["wrl"], c["br"])

    reps = GMLP_CHUNK // dseq
    wts_p = weights(w_spatial[l], bias_rows(b_spatial[l]))
    wts_s = weights(jnp.tile(w_spatial[l][:, :dseq, :dseq], (1, reps, reps)),
                    bias_rows(jnp.tile(b_spatial[l][:, :dseq], (1, reps))))

    w_kv = jnp.concatenate([w_mem_k[l], w_mem_v[l]], axis=1).astype(BF16)
    kv_p = _memkv(mem_prompt, row(g_mem[l]), w_kv)
    mk_p = kv_p[:, :, :C_XA]
    mv_p = kv_p[:, :, C_XA:]
    zero_hist = jnp.zeros((bp, 1, HIST, C_CONV), F32)
    tp, ts = bp * seq, bs * dseq
    n_assign = (tp + ts) * TOP_K
    (x1_p, xs, pos_p, gate_p, cnt_p, off_p, hist_p, _) = _mixer(
        x_prompt, zero_hist, mk_p[:, None], mv_p[:, None], wts_p,
        ns=1, sl=PROMPT_TILE, carry=True, sp_chunk=GMLP_CHUNK,
        sorted_rows_total=n_assign, sorted_row0=0, xs_prev=None)

    gs = bs // SAMPLE_SEQS_PER_TILE
    hist_s_in = jnp.pad(cache_conv[l], ((0, 0), (HIST_OFF, 0), (0, 0))).reshape(
        gs, SAMPLE_SEQS_PER_TILE, HIST, C_CONV)
    mk_s = cache_mem_k[l].reshape(gs, SAMPLE_SEQS_PER_TILE, N_MEM, C_XA)
    mv_s = cache_mem_v[l].reshape(gs, SAMPLE_SEQS_PER_TILE, N_MEM, C_XA)
    tm_s = SAMPLE_SEQS_PER_TILE * dseq
    (x1_s, xs, pos_s, gate_s, cnt_s, off_s, hist_s, v_s) = _mixer(
        x_sample.reshape(gs, tm_s, D_MODEL), hist_s_in, mk_s, mv_s, wts_s,
        ns=SAMPLE_SEQS_PER_TILE, sl=dseq, carry=False, sp_chunk=dseq,
        sorted_rows_total=n_assign, sorted_row0=tp * TOP_K, xs_prev=xs)

    n_blocks = -(-n_assign // MOE_BLOCK) + N_EXPERTS
    ntp, nts = tp // PROMPT_TILE, ts // tm_s
    cnt = jnp.concatenate([cnt_p[:, :, 0], cnt_s[:, :, 0]], axis=0)
    off = jnp.concatenate([off_p[:, :, 0], off_s[:, :, 0]], axis=0)
    tile_row0 = np.concatenate([np.arange(ntp) * (TOP_K * PROMPT_TILE),
                                tp * TOP_K + np.arange(nts) * (TOP_K * tm_s)]).astype(np.int32)
    counts = jnp.sum(cnt, axis=0)
    tile_base = jnp.cumsum(cnt, axis=0) - cnt
    padded = (counts + MOE_BLOCK - 1) // MOE_BLOCK * MOE_BLOCK
    pad_end = jnp.cumsum(padded)
    pad_start = pad_end - padded
    strip_dst = (pad_start[None, :] + tile_base).astype(I32)
    strip_src = (tile_row0[:, None] + off).astype(I32)
    n_used = (pad_end[-1] // MOE_BLOCK).astype(I32)
    blk_row0 = jnp.minimum(jnp.arange(n_blocks, dtype=I32), n_used - 1) * MOE_BLOCK
    block_expert = jnp.minimum(
        jnp.sum((pad_end[None, :] <= blk_row0[:, None]).astype(I32), axis=1), N_EXPERTS - 1)
    of_block = block_expert[:, None] == jnp.arange(N_EXPERTS, dtype=I32)[None, :]
    last_row = jnp.sum(jnp.where(of_block, (pad_start + counts)[None, :], 0), axis=1)
    block_valid = jnp.clip(last_row - blk_row0, 0, MOE_BLOCK).astype(I32)
    sdst = strip_dst.T.reshape(-1)
    ssrc = strip_src.T.reshape(-1)
    slen = cnt.T.reshape(-1).astype(I32)
    s_lo = jnp.sum(((sdst + slen)[None, :] <= blk_row0[:, None]).astype(I32), axis=1)
    s_hi = jnp.sum((sdst[None, :] < (blk_row0 + MOE_BLOCK)[:, None]).astype(I32), axis=1)
    moe_tables = ((pad_start // MOE_BLOCK).astype(I32), (padded // MOE_BLOCK).astype(I32),
                  block_valid, n_used.reshape(1),
                  s_lo.astype(I32), s_hi.astype(I32), ssrc, sdst, slen)

    slots = n_blocks * MOE_BLOCK
    ys = _moe(moe_tables, xs, slots,
              w_gate_up[l], b_gate_up[l][:, None, :], w_down[l], b_down[l][:, None, :])
    gfin = row(g_final)

    def combine_tables(t0, t1):
        return (strip_dst[t0:t1].reshape(-1), cnt[t0:t1].reshape(-1).astype(I32),
                off[t0:t1].reshape(-1).astype(I32))

    y_p = _combine(combine_tables(0, ntp), x1_p, pos_p, gate_p, gfin, ys, PROMPT_TILE)
    y_s = _combine(combine_tables(ntp, ntp + nts), x1_s, pos_s, gate_s, gfin, ys, tm_s)

    return (y_p.reshape(bp, seq, D_MODEL),
            y_s.reshape(bs, dseq, D_MODEL),
            hist_p[:, 0, HIST_OFF:, :][None],
            mk_p.reshape(bp, N_MEM, XA_HEADS, XA_HD)[None],
            mv_p.reshape(bp, N_MEM, XA_HEADS, XA_HD)[None],
            hist_s.reshape(bs, HIST, C_CONV)[:, HIST_OFF:, :][None],
            v_s.reshape(bs, dseq, C_GMLP)[None])
```

```python
import functools

import numpy as np
import jax
import jax.numpy as jnp
from jax import lax
from jax.experimental import pallas as pl
from jax.experimental.pallas import tpu as pltpu

F32 = jnp.float32
BF16 = jnp.bfloat16
I32 = jnp.int32

D_MODEL = 1024
C_CONV = 384
CONV_W = 31
C_GMLP = 384
GMLP_HEADS = 4
GMLP_HD = 96
GMLP_CHUNK = 128
XA_HEADS = 4
XA_HD = 64
C_XA = 256
N_MEM = 256
N_EXPERTS = 32
TOP_K = 4
D_FF = 1024
SWIGLU_LIMIT = 7.0
SWIGLU_ALPHA = 1.702
EPS = 1e-5
IN_COLS = 2 * C_CONV + 2 * C_GMLP + C_XA

SUBLANES = 8
LANES = 128
ROW_TILES = D_MODEL // LANES
HIST = 32
HIST_OFF = HIST - (CONV_W - 1)

PROMPT_TILE = 512
SAMPLE_SEQS_PER_TILE = 8
MOE_BLOCK = 512
GATHER_AHEAD = 2
VMEM_LIMIT_MIXER = 48 * 1024 * 1024
VMEM_LIMIT_MOE = 52 * 1024 * 1024


def _rmsnorm(x, g):
    return x * lax.rsqrt(jnp.mean(x * x, axis=-1, keepdims=True) + EPS) * g


def _layernorm(x, g, b):
    mu = jnp.mean(x, axis=-1, keepdims=True)
    xc = x - mu
    var = jnp.mean(xc * xc, axis=-1, keepdims=True)
    return xc * lax.rsqrt(var + EPS) * g + b


def _gelu(x):
    return 0.5 * x * (1.0 + lax.erf(x * np.float32(1.0 / np.sqrt(2.0))))


def _sigmoid(x):
    return 1.0 / (1.0 + jnp.exp(-x))


def _rows_to_tiles(dst_ref, val, rows, row0=0):
    for j in range(ROW_TILES):
        dst_ref[pl.ds(row0 * ROW_TILES + j, rows, stride=ROW_TILES), :] = val[:, j * LANES:(j + 1) * LANES]


def _tiles_to_rows(src_ref, rows, row0=0):
    return jnp.concatenate(
        [src_ref[pl.ds(row0 * ROW_TILES + j, rows, stride=ROW_TILES), :] for j in range(ROW_TILES)],
        axis=-1)


def _memkv_kernel(mem_ref, g_ref, w_ref, o_ref):
    mn = _rmsnorm(mem_ref[0], g_ref[...])
    o_ref[0] = jnp.dot(mn.astype(BF16), w_ref[...], preferred_element_type=F32)


def _memkv(mem, g_mem, w_kv):
    b = mem.shape[0]
    return pl.pallas_call(
        _memkv_kernel,
        grid=(b,),
        in_specs=[pl.BlockSpec((1, N_MEM, D_MODEL), lambda i: (i, 0, 0)),
                  pl.BlockSpec((1, D_MODEL), lambda i: (0, 0)),
                  pl.BlockSpec((D_MODEL, 2 * C_XA), lambda i: (0, 0))],
        out_specs=pl.BlockSpec((1, N_MEM, 2 * C_XA), lambda i: (i, 0, 0)),
        out_shape=jax.ShapeDtypeStruct((b, N_MEM, 2 * C_XA), F32),
        name="memkv",
    )(mem, g_mem, w_kv)


def _mixer_kernel(x_ref, hist_ref, mk_ref, mv_ref, gmix_ref, win_ref, wdw_ref, bdw_ref,
                  lcg_ref, lcb_ref, lvg_ref, lvb_ref, wsp_ref, bsp_ref, wout_ref, gffn_ref,
                  wrh_ref, wrl_ref, br_ref, *rest, ns, sl, carry, sp_chunk, tiles_per_seq):
    (x1_ref, xst_ref, pos_ref, gate_ref, cnt_ref, off_ref, histout_ref, v_ref,
     ext_ref, shift_ref, h2_keep, pos_keep) = rest[-12:]
    tm = ns * sl
    step = pl.program_id(0)
    n_tiles = pl.num_programs(0) - 1

    def sort_previous_tile():
        jj = lax.broadcasted_iota(I32, (TOP_K * tm, tm), 0)
        perm = jnp.zeros((TOP_K * tm, tm), F32)
        for k in range(TOP_K):
            perm = jnp.where(jj == pos_keep[k:k + 1, :], 1.0, perm)
        sorted_rows = jnp.dot(perm.astype(BF16), h2_keep[...], preferred_element_type=F32)
        _rows_to_tiles(xst_ref, sorted_rows, TOP_K * tm)

    @pl.when(step == 0)
    def _():
        h2_keep[...] = jnp.zeros_like(h2_keep)
        pos_keep[...] = jnp.full(pos_keep.shape, -1, I32)

    @pl.when(step < n_tiles)
    def _():
        sort_previous_tile()
        _mixer_tile(x_ref, hist_ref, mk_ref, mv_ref, gmix_ref, win_ref, wdw_ref, bdw_ref,
                    lcg_ref, lcb_ref, lvg_ref, lvb_ref, wsp_ref, bsp_ref, wout_ref, gffn_ref,
                    wrh_ref, wrl_ref, br_ref, x1_ref, pos_ref, gate_ref, cnt_ref, off_ref,
                    histout_ref, v_ref, ext_ref, shift_ref, h2_keep, pos_keep,
                    first_of_seq=(step % tiles_per_seq) == 0,
                    ns=ns, sl=sl, carry=carry, sp_chunk=sp_chunk)

    @pl.when(step == n_tiles)
    def _():
        sort_previous_tile()


def _mixer_tile(x_ref, hist_ref, mk_ref, mv_ref, gmix_ref, win_ref, wdw_ref, bdw_ref,
                lcg_ref, lcb_ref, lvg_ref, lvb_ref, wsp_ref, bsp_ref, wout_ref, gffn_ref,
                wrh_ref, wrl_ref, br_ref, x1_ref, pos_ref, gate_ref, cnt_ref, off_ref,
                histout_ref, v_ref, ext_ref, shift_ref, h2_keep, pos_keep,
                *, first_of_seq, ns, sl, carry, sp_chunk):
    tm = ns * sl
    x = x_ref[0]
    h = _rmsnorm(x, gmix_ref[...])
    z = jnp.dot(h.astype(BF16), win_ref[...], preferred_element_type=F32)
    z_a = z[:, 0:C_CONV]
    z_g = z[:, C_CONV:2 * C_CONV]
    z_u = z[:, 2 * C_CONV:2 * C_CONV + C_GMLP]
    z_v = z[:, 2 * C_CONV + C_GMLP:2 * C_CONV + 2 * C_GMLP]
    z_q = z[:, 2 * C_CONV + 2 * C_GMLP:IN_COLS]

    glu = z_a * _sigmoid(z_g)
    if carry:
        ext_ref[:, 0:HIST, :] = jnp.where(first_of_seq, hist_ref[0], ext_ref[:, 0:HIST, :])
    else:
        ext_ref[:, 0:HIST, :] = hist_ref[0]
    conv_parts = []
    for s in range(ns):
        ext_s = ext_ref.at[s]
        ext_s[HIST:HIST + sl, :] = glu[s * sl:(s + 1) * sl]
        n_shift = HIST + sl - SUBLANES
        for r in range(1, SUBLANES):
            shift_ref[s, r - 1, 0:n_shift, :] = ext_s[pl.ds(r, n_shift), :]
        rc = min(sl, 64)
        for r0 in range(0, sl, rc):
            acc = jnp.broadcast_to(bdw_ref[...], (rc, C_CONV))
            for j in range(CONV_W):
                a, r = divmod(j + HIST_OFF, SUBLANES)
                src = ext_s if r == 0 else shift_ref.at[s, r - 1]
                acc = acc + wdw_ref[j:j + 1, :] * src[pl.ds(r0 + a * SUBLANES, rc), :]
            conv_parts.append(acc)
        new_hist = ext_s[sl:sl + HIST, :]
        histout_ref[0, s] = new_hist
        if carry:
            ext_s[0:HIST, :] = new_hist
    y = jnp.concatenate(conv_parts, axis=0) if len(conv_parts) > 1 else conv_parts[0]
    y = _layernorm(y, lcg_ref[...], lcb_ref[...])
    c_out = y * _sigmoid(y)

    u = _gelu(z_u)
    v = _layernorm(_gelu(z_v), lvg_ref[...], lvb_ref[...])
    v_ref[...] = v
    vb = v.astype(BF16)
    rr = lax.broadcasted_iota(I32, (GMLP_CHUNK, GMLP_CHUNK), 0)
    cc = lax.broadcasted_iota(I32, (GMLP_CHUNK, GMLP_CHUNK), 1)
    sp_mask = (cc <= rr) & ((rr // sp_chunk) == (cc // sp_chunk))
    col = lax.broadcasted_iota(I32, (GMLP_CHUNK, C_GMLP), 1)
    w_heads = [jnp.where(sp_mask, wsp_ref[hh], 0.0).astype(BF16) for hh in range(GMLP_HEADS)]
    g_parts = []
    for c in range(tm // GMLP_CHUNK):
        vc = vb[c * GMLP_CHUNK:(c + 1) * GMLP_CHUNK]
        sg = bsp_ref[...]
        for hh in range(GMLP_HEADS):
            head_cols = (col >= hh * GMLP_HD) & (col < (hh + 1) * GMLP_HD)
            vh = jnp.where(head_cols, vc, jnp.zeros_like(vc))
            sg = sg + jnp.dot(w_heads[hh], vh, preferred_element_type=F32)
        g_parts.append(u[c * GMLP_CHUNK:(c + 1) * GMLP_CHUNK] * sg)
    g_out = jnp.concatenate(g_parts, axis=0) if len(g_parts) > 1 else g_parts[0]

    qs = z_q * np.float32(XA_HD ** -0.5)
    qcol = lax.broadcasted_iota(I32, (sl, C_XA), 1)
    a_parts = []
    for s in range(ns):
        q_s = qs[s * sl:(s + 1) * sl]
        kb = mk_ref[0, s].astype(BF16)
        vvb = mv_ref[0, s].astype(BF16)
        hmasks = [(qcol >= hh * XA_HD) & (qcol < (hh + 1) * XA_HD) for hh in range(XA_HEADS)]
        stack = XA_HEADS if sl * XA_HEADS <= N_MEM else 1
        a_s = jnp.zeros((sl, C_XA), F32)
        for h0 in range(0, XA_HEADS, stack):
            heads = range(h0, h0 + stack)
            qh = jnp.concatenate([jnp.where(hmasks[hh], q_s, 0.0) for hh in heads], axis=0).astype(BF16)
            sc = lax.dot_general(qh, kb, (((1,), (1,)), ((), ())), preferred_element_type=F32)
            p = jnp.exp(sc - jnp.max(sc, axis=-1, keepdims=True))
            den = jnp.sum(p, axis=-1, keepdims=True)
            oh = jnp.dot(p.astype(BF16), vvb, preferred_element_type=F32) / den
            for n, hh in enumerate(heads):
                a_s = a_s + jnp.where(hmasks[hh], oh[n * sl:(n + 1) * sl], 0.0)
        a_parts.append(a_s)
    a_out = jnp.concatenate(a_parts, axis=0) if len(a_parts) > 1 else a_parts[0]

    mix = jnp.concatenate([c_out, g_out, a_out], axis=-1).astype(BF16)
    x1 = x + jnp.dot(mix, wout_ref[...], preferred_element_type=F32)
    x1_ref[...] = x1

    h2 = _rmsnorm(x1, gffn_ref[...])
    h2_hi = h2.astype(BF16)
    h2_lo = (h2 - h2_hi.astype(F32)).astype(BF16)
    nt_dims = (((1,), (1,)), ((), ()))
    lg = (lax.dot_general(wrh_ref[...], h2_hi, nt_dims, preferred_element_type=F32)
          + lax.dot_general(wrl_ref[...], h2_hi, nt_dims, preferred_element_type=F32)
          + lax.dot_general(wrh_ref[...], h2_lo, nt_dims, preferred_element_type=F32)
          + br_ref[...])
    eio = lax.broadcasted_iota(I32, (N_EXPERTS, tm), 0)
    work = lg
    vals, idxs = [], []
    for _ in range(TOP_K):
        m = jnp.max(work, axis=0, keepdims=True)
        idx = jnp.min(jnp.where(work == m, eio, N_EXPERTS), axis=0, keepdims=True)
        vals.append(m)
        idxs.append(idx)
        work = jnp.where(eio == idx, -jnp.inf, work)
    exps = [jnp.exp(vk - vals[0]) for vk in vals]
    den = exps[0] + exps[1] + exps[2] + exps[3]
    gate_ref[...] = jnp.concatenate([ek / den for ek in exps], axis=0)

    sel = jnp.zeros((N_EXPERTS, tm), F32)
    for idx in idxs:
        sel = sel + jnp.where(eio == idx, 1.0, 0.0)
    selb = sel.astype(BF16)
    tr = lax.broadcasted_iota(I32, (tm, tm), 0)
    tc = lax.broadcasted_iota(I32, (tm, tm), 1)
    before = jnp.where(tr < tc, 1.0, 0.0).astype(BF16)
    ranks = jnp.dot(selb, before, preferred_element_type=F32)
    er = lax.broadcasted_iota(I32, (N_EXPERTS, N_EXPERTS), 0)
    ec = lax.broadcasted_iota(I32, (N_EXPERTS, N_EXPERTS), 1)
    lower = jnp.where(ec < er, 1.0, 0.0).astype(BF16)
    cnt = jnp.sum(sel, axis=1, keepdims=True)
    off = jnp.sum(jnp.dot(lower, selb, preferred_element_type=F32), axis=1, keepdims=True)
    cnt_ref[0] = jnp.broadcast_to(cnt, (N_EXPERTS, LANES)).astype(I32)
    off_ref[0] = jnp.broadcast_to(off, (N_EXPERTS, LANES)).astype(I32)
    slot_of = off + ranks
    pos = [jnp.sum(jnp.where(eio == idx, slot_of, 0.0), axis=0, keepdims=True).astype(I32) for idx in idxs]
    pos_all = jnp.concatenate(pos, axis=0)
    pos_ref[...] = pos_all
    pos_keep[...] = pos_all
    h2_keep[...] = h2_hi


def _mixer(x, hist, mem_k, mem_v, wts, *, ns, sl, carry, sp_chunk, sorted_rows_total, sorted_row0, xs_prev):
    g, r, _ = x.shape
    tm = ns * sl
    nt = r // tm
    ntot = g * nt
    assert sorted_row0 % (TOP_K * tm) == 0
    xs_blk0 = sorted_row0 // (TOP_K * tm)
    tile = lambda s: jnp.minimum(s, ntot - 1)
    const2 = lambda s: (0, 0)
    const3 = lambda s: (0, 0, 0)
    tile_row = lambda s: (tile(s), 0)
    tile_lane = lambda s: (0, tile(s))
    per_seq = lambda s: (tile(s) // nt, 0, 0, 0)
    in_specs = [
        pl.BlockSpec((1, tm, D_MODEL), lambda s: (tile(s) // nt, tile(s) % nt, 0)),
        pl.BlockSpec((1, ns, HIST, C_CONV), per_seq),
        pl.BlockSpec((1, ns, N_MEM, C_XA), per_seq),
        pl.BlockSpec((1, ns, N_MEM, C_XA), per_seq),
        pl.BlockSpec((1, D_MODEL), const2),
        pl.BlockSpec((D_MODEL, IN_COLS), const2),
        pl.BlockSpec((HIST, C_CONV), const2),
        pl.BlockSpec((1, C_CONV), const2),
        pl.BlockSpec((1, C_CONV), const2),
        pl.BlockSpec((1, C_CONV), const2),
        pl.BlockSpec((1, C_GMLP), const2),
        pl.BlockSpec((1, C_GMLP), const2),
        pl.BlockSpec((GMLP_HEADS, GMLP_CHUNK, GMLP_CHUNK), const3),
        pl.BlockSpec((GMLP_CHUNK, C_GMLP), const2),
        pl.BlockSpec((D_MODEL, D_MODEL), const2),
        pl.BlockSpec((1, D_MODEL), const2),
        pl.BlockSpec((N_EXPERTS, D_MODEL), const2),
        pl.BlockSpec((N_EXPERTS, D_MODEL), const2),
        pl.BlockSpec((N_EXPERTS, 1), const2),
    ]
    tile_cnt = lambda s: (tile(s), 0, 0)
    out_specs = [
        pl.BlockSpec((tm, D_MODEL), tile_row),
        pl.BlockSpec((TOP_K * tm * ROW_TILES, LANES),
                     lambda s: (xs_blk0 + jnp.maximum(s - 1, 0), 0)),
        pl.BlockSpec((TOP_K, tm), tile_lane),
        pl.BlockSpec((TOP_K, tm), tile_lane),
        pl.BlockSpec((1, N_EXPERTS, LANES), tile_cnt),
        pl.BlockSpec((1, N_EXPERTS, LANES), tile_cnt),
        pl.BlockSpec((1, ns, HIST, C_CONV), per_seq),
        pl.BlockSpec((tm, C_GMLP), tile_row),
    ]
    rows = g * r
    out_shape = [
        jax.ShapeDtypeStruct((rows, D_MODEL), F32),
        jax.ShapeDtypeStruct((sorted_rows_total * ROW_TILES, LANES), F32),
        jax.ShapeDtypeStruct((TOP_K, rows), I32),
        jax.ShapeDtypeStruct((TOP_K, rows), F32),
        jax.ShapeDtypeStruct((ntot, N_EXPERTS, LANES), I32),
        jax.ShapeDtypeStruct((ntot, N_EXPERTS, LANES), I32),
        jax.ShapeDtypeStruct((g, ns, HIST, C_CONV), F32),
        jax.ShapeDtypeStruct((rows, C_GMLP), F32),
    ]
    args = [x, hist, mem_k, mem_v, *wts]
    aliases = {}
    if xs_prev is not None:
        in_specs.append(pl.BlockSpec(memory_space=pl.ANY))
        aliases = {len(args): 1}
        args.append(xs_prev)
    kern = functools.partial(_mixer_kernel, ns=ns, sl=sl, carry=carry, sp_chunk=sp_chunk,
                             tiles_per_seq=nt)
    return pl.pallas_call(
        kern,
        grid=(ntot + 1,),
        in_specs=in_specs,
        out_specs=out_specs,
        out_shape=out_shape,
        scratch_shapes=[pltpu.VMEM((ns, HIST + sl, C_CONV), F32),
                        pltpu.VMEM((ns, SUBLANES - 1, HIST + sl, C_CONV), F32),
                        pltpu.VMEM((tm, D_MODEL), BF16),
                        pltpu.VMEM((TOP_K, tm), I32)],
        input_output_aliases=aliases,
        compiler_params=pltpu.CompilerParams(
            dimension_semantics=("arbitrary",),
            vmem_limit_bytes=VMEM_LIMIT_MIXER),
        name="mixer_carry" if carry else "mixer_cache",
    )(*args)


def _strip_pieces(n, max_rows, fn):
    done = 0
    p = max_rows
    while p >= 1:
        has = (n & p) != 0
        pl.when(has)(functools.partial(fn, done, p))
        done = done + jnp.where(has, p, 0)
        p //= 2


def _rows(ref, first_row, rows):
    return ref.at[pl.ds(pl.multiple_of(first_row * ROW_TILES, ROW_TILES), rows * ROW_TILES)]


def _moe_kernel(b0_ref, nbk_ref, bv_ref, nb_ref, slo_ref, shi_ref, ssrc_ref, sdst_ref, slen_ref,
                xs_hbm, wgu_ref, bgu_ref, wdn_ref, bdn_ref, ys_hbm,
                xbuf, ybuf, wgu16, wdn16, sems, ysems):
    e = pl.program_id(0)
    n_used = nb_ref[0]
    half = MOE_BLOCK // 2

    def gather_start(b, slot):
        row0 = b * MOE_BLOCK

        def strip(s, carry):
            lo = jnp.maximum(sdst_ref[s], row0)
            hi = jnp.minimum(sdst_ref[s] + slen_ref[s], row0 + MOE_BLOCK)
            src0 = ssrc_ref[s] + (lo - sdst_ref[s])
            dst0 = lo - row0

            def piece(first, rows):
                pltpu.make_async_copy(_rows(xs_hbm, src0 + first, rows),
                                      _rows(xbuf.at[slot], dst0 + first, rows), sems.at[slot]).start()
            _strip_pieces(hi - lo, MOE_BLOCK, piece)
            return carry
        lax.fori_loop(slo_ref[b], shi_ref[b], strip, 0)

    def gather_wait(b, slot):
        def piece(first, rows):
            del first
            pltpu.make_async_copy(_rows(xs_hbm, 0, rows), _rows(xbuf.at[slot], 0, rows), sems.at[slot]).wait()
        _strip_pieces(bv_ref[b], MOE_BLOCK, piece)

    def out_copy(b, slot):
        return pltpu.make_async_copy(ybuf.at[slot], _rows(ys_hbm, b * MOE_BLOCK, MOE_BLOCK), ysems.at[slot])

    def expert_mlp(xslot, slot, h, valid):
        row_ok = lax.broadcasted_iota(I32, (half, 1), 0) + h * half < valid
        xb = jnp.where(row_ok, _tiles_to_rows(xbuf.at[xslot], half, h * half), 0.0).astype(BF16)
        gu = jnp.dot(xb, wgu16[...], preferred_element_type=F32) + bgu_ref[0]
        gate = jnp.minimum(gu[:, :D_FF], SWIGLU_LIMIT)
        up = jnp.clip(gu[:, D_FF:], -SWIGLU_LIMIT, SWIGLU_LIMIT)
        act = (up + 1.0) * (gate * _sigmoid(SWIGLU_ALPHA * gate))
        yb = jnp.dot(act.astype(BF16), wdn16[...], preferred_element_type=F32) + bdn_ref[0]
        _rows_to_tiles(ybuf.at[slot], yb, half, h * half)

    @pl.when(e == 0)
    def _():
        for a in range(GATHER_AHEAD):
            @pl.when(a < n_used)
            def _():
                gather_start(a, a)

    @pl.when(nbk_ref[e] > 0)
    def _():
        def cast_rows(c, carry):
            r = pl.multiple_of(c * LANES, LANES)
            wgu16[pl.ds(r, LANES), :] = wgu_ref[pl.ds(r, LANES), :].astype(BF16)
            wdn16[pl.ds(r, LANES), :] = wdn_ref[pl.ds(r, LANES), :].astype(BF16)
            return carry
        lax.fori_loop(0, D_MODEL // LANES, cast_rows, 0)

        def block(j, carry):
            b = b0_ref[e] + j
            slot = b % 2
            xslot = b % (GATHER_AHEAD + 1)

            @pl.when(b + GATHER_AHEAD < n_used)
            def _():
                gather_start(b + GATHER_AHEAD, (b + GATHER_AHEAD) % (GATHER_AHEAD + 1))

            gather_wait(b, xslot)

            @pl.when(b >= 2)
            def _():
                out_copy(b - 2, slot).wait()

            valid = bv_ref[b]
            expert_mlp(xslot, slot, 0, valid)

            @pl.when(valid > half)
            def _():
                expert_mlp(xslot, slot, 1, valid)

            @pl.when(valid <= half)
            def _():
                ybuf[slot, pl.ds(half * ROW_TILES, half * ROW_TILES), :] = jnp.zeros(
                    (half * ROW_TILES, LANES), F32)

            out_copy(b, slot).start()
            return carry
        lax.fori_loop(0, nbk_ref[e], block, 0)

    @pl.when(e == pl.num_programs(0) - 1)
    def _():
        for back in (1, 2):
            @pl.when(n_used >= back)
            def _():
                out_copy(n_used - back, (n_used - back) % 2).wait()


def _moe(tables, xs, slots, w_gu, b_gu, w_dn, b_dn):
    wsel = lambda e, *_: (e, 0, 0)
    grid_spec = pltpu.PrefetchScalarGridSpec(
        num_scalar_prefetch=len(tables),
        grid=(N_EXPERTS,),
        in_specs=[pl.BlockSpec(memory_space=pl.ANY),
                  pl.BlockSpec((None, D_MODEL, 2 * D_FF), wsel),
                  pl.BlockSpec((None, 1, 2 * D_FF), wsel),
                  pl.BlockSpec((None, D_FF, D_MODEL), wsel),
                  pl.BlockSpec((None, 1, D_MODEL), wsel)],
        out_specs=pl.BlockSpec(memory_space=pl.ANY),
        scratch_shapes=[pltpu.VMEM((GATHER_AHEAD + 1, MOE_BLOCK * ROW_TILES, LANES), F32),
                        pltpu.VMEM((2, MOE_BLOCK * ROW_TILES, LANES), F32),
                        pltpu.VMEM((D_MODEL, 2 * D_FF), BF16),
                        pltpu.VMEM((D_FF, D_MODEL), BF16),
                        pltpu.SemaphoreType.DMA((GATHER_AHEAD + 1,)),
                        pltpu.SemaphoreType.DMA((2,))],
    )
    return pl.pallas_call(
        _moe_kernel,
        grid_spec=grid_spec,
        out_shape=jax.ShapeDtypeStruct((slots * ROW_TILES, LANES), F32),
        compiler_params=pltpu.CompilerParams(
            dimension_semantics=("arbitrary",),
            vmem_limit_bytes=VMEM_LIMIT_MOE),
        name="moe",
    )(*tables, xs, w_gu, b_gu, w_dn, b_dn)


def _combine_kernel(csrc_ref, clen_ref, coff_ref, x1_ref, pos_ref, gate_ref, gfin_ref, ys_hbm,
                    out_ref, ybuf_a, ybuf_b, sems, *, tm):
    i = pl.program_id(0)
    n_tiles = 2 * pl.num_programs(0)
    n_sorted = TOP_K * tm
    bufs = (ybuf_a, ybuf_b)

    def gather_start(t, b, enabled=None):
        for e in range(N_EXPERTS):
            s = t * N_EXPERTS + e
            src0 = csrc_ref[s]
            dst0 = coff_ref[s]
            n = clen_ref[s] if enabled is None else jnp.where(enabled, clen_ref[s], 0)

            def piece(first, rows, src0=src0, dst0=dst0):
                pltpu.make_async_copy(_rows(ys_hbm, src0 + first, rows),
                                      _rows(bufs[b], dst0 + first, rows), sems.at[b]).start()
            _strip_pieces(n, tm, piece)

    def gather_wait(b):
        pltpu.make_async_copy(_rows(ys_hbm, 0, n_sorted), bufs[b], sems.at[b]).wait()

    def finish(b):
        lanes = slice(b * tm, (b + 1) * tm)
        rr = lax.broadcasted_iota(I32, (tm, tm), 0)
        cc = lax.broadcasted_iota(I32, (tm, tm), 1)
        eye = rr == cc
        p_cols = [jnp.sum(jnp.where(eye, pos_ref[k:k + 1, lanes].astype(F32), 0.0), axis=1, keepdims=True)
                  for k in range(TOP_K)]
        g_cols = [jnp.sum(jnp.where(eye, gate_ref[k:k + 1, lanes], 0.0), axis=1, keepdims=True)
                  for k in range(TOP_K)]
        acc = x1_ref[lanes, :]
        for c in range(TOP_K):
            jl = (lax.broadcasted_iota(I32, (tm, tm), 1) + c * tm).astype(F32)
            unsort = jnp.zeros((tm, tm), F32)
            for k in range(TOP_K):
                unsort = jnp.where(jl == p_cols[k], g_cols[k], unsort)
            y_chunk = _tiles_to_rows(bufs[b], tm, c * tm).astype(BF16)
            acc = acc + jnp.dot(unsort.astype(BF16), y_chunk, preferred_element_type=F32)
        out_ref[lanes, :] = _rmsnorm(acc, gfin_ref[...])

    @pl.when(i == 0)
    def _():
        gather_start(0, 0)

    gather_wait(0)
    gather_start(2 * i + 1, 1)
    finish(0)
    gather_wait(1)
    gather_start(jnp.minimum(2 * i + 2, n_tiles - 1), 0, 2 * i + 2 < n_tiles)
    finish(1)


def _combine(tables, x1, pos, gates, g_final, ys, tm):
    t = x1.shape[0]
    nt = t // tm
    assert nt % 2 == 0
    grid_spec = pltpu.PrefetchScalarGridSpec(
        num_scalar_prefetch=len(tables),
        grid=(nt // 2,),
        in_specs=[pl.BlockSpec((2 * tm, D_MODEL), lambda i, *_: (i, 0)),
                  pl.BlockSpec((TOP_K, 2 * tm), lambda i, *_: (0, i)),
                  pl.BlockSpec((TOP_K, 2 * tm), lambda i, *_: (0, i)),
                  pl.BlockSpec((1, D_MODEL), lambda i, *_: (0, 0)),
                  pl.BlockSpec(memory_space=pl.ANY)],
        out_specs=pl.BlockSpec((2 * tm, D_MODEL), lambda i, *_: (i, 0)),
        scratch_shapes=[pltpu.VMEM((TOP_K * tm * ROW_TILES, LANES), F32),
                        pltpu.VMEM((TOP_K * tm * ROW_TILES, LANES), F32),
                        pltpu.SemaphoreType.DMA((2,))],
    )
    return pl.pallas_call(
        functools.partial(_combine_kernel, tm=tm),
        grid_spec=grid_spec,
        out_shape=jax.ShapeDtypeStruct((t, D_MODEL), F32),
        compiler_params=pltpu.CompilerParams(
            dimension_semantics=("arbitrary",),
            vmem_limit_bytes=VMEM_LIMIT_MIXER),
        name="combine",
    )(*tables, x1, pos, gates, g_final, ys)


def _split_bf16(w):
    hi = w.astype(BF16)
    lo = (w - hi.astype(F32)).astype(BF16)
    return hi, lo


def kernel(x_prompt, x_sample, cache_conv, cache_mem_k, cache_mem_v, mem_prompt, g_mix, w_in, w_dw, b_dw, ln_conv_g, ln_conv_b, ln_v_g, ln_v_b, w_spatial, b_spatial, g_mem, w_mem_k, w_mem_v, w_out, g_ffn, w_router, b_router, w_gate_up, b_gate_up, w_down, b_down, g_final):
    depth = g_mix.shape[0]
    assert depth == 1
    l = 0
    bp, seq, _ = x_prompt.shape
    bs, dseq, _ = x_sample.shape
    assert seq % PROMPT_TILE == 0 and bs % SAMPLE_SEQS_PER_TILE == 0
    assert GMLP_CHUNK % dseq == 0 and (SAMPLE_SEQS_PER_TILE * dseq) % GMLP_CHUNK == 0

    row = lambda a: a.reshape(1, -1)
    wr_hi, wr_lo = _split_bf16(w_router[l].T)
    w_dw_pad = jnp.pad(w_dw[l], ((0, HIST - CONV_W), (0, 0)))
    bias_rows = lambda b: jnp.repeat(b.T, GMLP_HD, axis=1)
    common = dict(
        gmix=row(g_mix[l]), win=w_in[l].astype(BF16), wdw=w_dw_pad, bdw=row(b_dw[l]),
        lcg=row(ln_conv_g[l]), lcb=row(ln_conv_b[l]), lvg=row(ln_v_g[l]), lvb=row(ln_v_b[l]),
        wout=w_out[l].astype(BF16), gffn=row(g_ffn[l]), wrh=wr_hi, wrl=wr_lo,
        br=b_router[l].reshape(N_EXPERTS, 1))

    def weights(wsp, bsp):
        c = common
        return (c["gmix"], c["win"], c["wdw"], c["bdw"], c["lcg"], c["lcb"], c["lvg"], c["lvb"],
                wsp, bsp, c["wout"], c["gffn"], c["wrh"], c["wrl"], c["br"])

    reps = GMLP_CHUNK // dseq
    wts_p = weights(w_spatial[l], bias_rows(b_spatial[l]))
    wts_s = weights(jnp.tile(w_spatial[l][:, :dseq, :dseq], (1, reps, reps)),
                    bias_rows(jnp.tile(b_spatial[l][:, :dseq], (1, reps))))

    w_kv = jnp.concatenate([w_mem_k[l], w_mem_v[l]], axis=1).astype(BF16)
    kv_p = _memkv(mem_prompt, row(g_mem[l]), w_kv)
    mk_p = kv_p[:, :, :C_XA]
    mv_p = kv_p[:, :, C_XA:]
    zero_hist = jnp.zeros((bp, 1, HIST, C_CONV), F32)
    tp, ts = bp * seq, bs * dseq
    n_assign = (tp + ts) * TOP_K
    (x1_p, xs, pos_p, gate_p, cnt_p, off_p, hist_p, _) = _mixer(
        x_prompt, zero_hist, mk_p[:, None], mv_p[:, None], wts_p,
        ns=1, sl=PROMPT_TILE, carry=True, sp_chunk=GMLP_CHUNK,
        sorted_rows_total=n_assign, sorted_row0=0, xs_prev=None)

    gs = bs // SAMPLE_SEQS_PER_TILE
    hist_s_in = jnp.pad(cache_conv[l], ((0, 0), (HIST_OFF, 0), (0, 0))).reshape(
        gs, SAMPLE_SEQS_PER_TILE, HIST, C_CONV)
    mk_s = cache_mem_k[l].reshape(gs, SAMPLE_SEQS_PER_TILE, N_MEM, C_XA)
    mv_s = cache_mem_v[l].reshape(gs, SAMPLE_SEQS_PER_TILE, N_MEM, C_XA)
    tm_s = SAMPLE_SEQS_PER_TILE * dseq
    (x1_s, xs, pos_s, gate_s, cnt_s, off_s, hist_s, v_s) = _mixer(
        x_sample.reshape(gs, tm_s, D_MODEL), hist_s_in, mk_s, mv_s, wts_s,
        ns=SAMPLE_SEQS_PER_TILE, sl=dseq, carry=False, sp_chunk=dseq,
        sorted_rows_total=n_assign, sorted_row0=tp * TOP_K, xs_prev=xs)

    n_blocks = -(-n_assign // MOE_BLOCK) + N_EXPERTS
    ntp, nts = tp // PROMPT_TILE, ts // tm_s
    cnt = jnp.concatenate([cnt_p[:, :, 0], cnt_s[:, :, 0]], axis=0)
    off = jnp.concatenate([off_p[:, :, 0], off_s[:, :, 0]], axis=0)
    tile_row0 = np.concatenate([np.arange(ntp) * (TOP_K * PROMPT_TILE),
                                tp * TOP_K + np.arange(nts) * (TOP_K * tm_s)]).astype(np.int32)
    counts = jnp.sum(cnt, axis=0)
    tile_base = jnp.cumsum(cnt, axis=0) - cnt
    padded = (counts + MOE_BLOCK - 1) // MOE_BLOCK * MOE_BLOCK
    pad_end = jnp.cumsum(padded)
    pad_start = pad_end - padded
    strip_dst = (pad_start[None, :] + tile_base).astype(I32)
    strip_src = (tile_row0[:, None] + off).astype(I32)
    n_used = (pad_end[-1] // MOE_BLOCK).astype(I32)
    blk_row0 = jnp.minimum(jnp.arange(n_blocks, dtype=I32), n_used - 1) * MOE_BLOCK
    block_expert = jnp.minimum(
        jnp.sum((pad_end[None, :] <= blk_row0[:, None]).astype(I32), axis=1), N_EXPERTS - 1)
    of_block = block_expert[:, None] == jnp.arange(N_EXPERTS, dtype=I32)[None, :]
    last_row = jnp.sum(jnp.where(of_block, (pad_start + counts)[None, :], 0), axis=1)
    block_valid = jnp.clip(last_row - blk_row0, 0, MOE_BLOCK).astype(I32)
    sdst = strip_dst.T.reshape(-1)
    ssrc = strip_src.T.reshape(-1)
    slen = cnt.T.reshape(-1).astype(I32)
    s_lo = jnp.sum(((sdst + slen)[None, :] <= blk_row0[:, None]).astype(I32), axis=1)
    s_hi = jnp.sum((sdst[None, :] < (blk_row0 + MOE_BLOCK)[:, None]).astype(I32), axis=1)
    moe_tables = ((pad_start // MOE_BLOCK).astype(I32), (padded // MOE_BLOCK).astype(I32),
                  block_valid, n_used.reshape(1),
                  s_lo.astype(I32), s_hi.astype(I32), ssrc, sdst, slen)

    slots = n_blocks * MOE_BLOCK
    ys = _moe(moe_tables, xs, slots,
              w_gate_up[l], b_gate_up[l][:, None, :], w_down[l], b_down[l][:, None, :])
    gfin = row(g_final)

    def combine_tables(t0, t1):
        return (strip_dst[t0:t1].reshape(-1), cnt[t0:t1].reshape(-1).astype(I32),
                off[t0:t1].reshape(-1).astype(I32))

    y_p = _combine(combine_tables(0, ntp), x1_p, pos_p, gate_p, gfin, ys, PROMPT_TILE)
    y_s = _combine(combine_tables(ntp, ntp + nts), x1_s, pos_s, gate_s, gfin, ys, tm_s)

    return (y_p.reshape(bp, seq, D_MODEL),
            y_s.reshape(bs, dseq, D_MODEL),
            hist_p[:, 0, HIST_OFF:, :][None],
            mk_p.reshape(bp, N_MEM, XA_HEADS, XA_HD)[None],
            mv_p.reshape(bp, N_MEM, XA_HEADS, XA_HD)[None],
            hist_s.reshape(bs, HIST, C_CONV)[:, HIST_OFF:, :][None],
            v_s.reshape(bs, dseq, C_GMLP)[None])
```

```python
import functools

import numpy as np
import jax
import jax.numpy as jnp
from jax import lax
from jax.experimental import pallas as pl
from jax.experimental.pallas import tpu as pltpu

F32 = jnp.float32
BF16 = jnp.bfloat16
I32 = jnp.int32

D_MODEL = 1024
C_CONV = 384
CONV_W = 31
C_GMLP = 384
GMLP_HEADS = 4
GMLP_HD = 96
GMLP_CHUNK = 128
XA_HEADS = 4
XA_HD = 64
C_XA = 256
N_MEM = 256
N_EXPERTS = 32
TOP_K = 4
D_FF = 1024
SWIGLU_LIMIT = 7.0
SWIGLU_ALPHA = 1.702
EPS = 1e-5
IN_COLS = 2 * C_CONV + 2 * C_GMLP + C_XA

SUBLANES = 8
LANES = 128
ROW_TILES = D_MODEL // LANES
HIST = 32
HIST_OFF = HIST - (CONV_W - 1)

PROMPT_TILE = 512
SAMPLE_SEQS_PER_TILE = 8
MOE_BLOCK = 512
GATHER_AHEAD = 2
VMEM_LIMIT_MIXER = 48 * 1024 * 1024
VMEM_LIMIT_MOE = 52 * 1024 * 1024


def _rmsnorm(x, g):
    return x * lax.rsqrt(jnp.mean(x * x, axis=-1, keepdims=True) + EPS) * g


def _layernorm(x, g, b):
    mu = jnp.mean(x, axis=-1, keepdims=True)
    xc = x - mu
    var = jnp.mean(xc * xc, axis=-1, keepdims=True)
    return xc * lax.rsqrt(var + EPS) * g + b


def _gelu(x):
    return 0.5 * x * (1.0 + lax.erf(x * np.float32(1.0 / np.sqrt(2.0))))


def _sigmoid(x):
    return 1.0 / (1.0 + jnp.exp(-x))


def _rows_to_tiles(dst_ref, val, rows, row0=0):
    for j in range(ROW_TILES):
        dst_ref[pl.ds(row0 * ROW_TILES + j, rows, stride=ROW_TILES), :] = val[:, j * LANES:(j + 1) * LANES]


def _tiles_to_rows(src_ref, rows, row0=0):
    return jnp.concatenate(
        [src_ref[pl.ds(row0 * ROW_TILES + j, rows, stride=ROW_TILES), :] for j in range(ROW_TILES)],
        axis=-1)


def _memkv_kernel(mem_ref, g_ref, w_ref, o_ref):
    mn = _rmsnorm(mem_ref[0], g_ref[...])
    o_ref[0] = jnp.dot(mn.astype(BF16), w_ref[...], preferred_element_type=F32)


def _memkv(mem, g_mem, w_kv):
    b = mem.shape[0]
    return pl.pallas_call(
        _memkv_kernel,
        grid=(b,),
        in_specs=[pl.BlockSpec((1, N_MEM, D_MODEL), lambda i: (i, 0, 0)),
                  pl.BlockSpec((1, D_MODEL), lambda i: (0, 0)),
                  pl.BlockSpec((D_MODEL, 2 * C_XA), lambda i: (0, 0))],
        out_specs=pl.BlockSpec((1, N_MEM, 2 * C_XA), lambda i: (i, 0, 0)),
        out_shape=jax.ShapeDtypeStruct((b, N_MEM, 2 * C_XA), F32),
        name="memkv",
    )(mem, g_mem, w_kv)


def _mixer_kernel(x_ref, hist_ref, mk_ref, mv_ref, gmix_ref, win_ref, wdw_ref, bdw_ref,
                  lcg_ref, lcb_ref, lvg_ref, lvb_ref, wsp_ref, bsp_ref, wout_ref, gffn_ref,
                  wrh_ref, wrl_ref, br_ref, *rest, ns, sl, carry, sp_chunk, tiles_per_seq):
    (x1_ref, xst_ref, pos_ref, gate_ref, cnt_ref, off_ref, histout_ref, v_ref,
     ext_ref, shift_ref, h2_keep, pos_keep) = rest[-12:]
    tm = ns * sl
    step = pl.program_id(0)
    n_tiles = pl.num_programs(0) - 1

    def sort_previous_tile():
        jj = lax.broadcasted_iota(I32, (TOP_K * tm, tm), 0)
        perm = jnp.zeros((TOP_K * tm, tm), F32)
        for k in range(TOP_K):
            perm = jnp.where(jj == pos_keep[k:k + 1, :], 1.0, perm)
        sorted_rows = jnp.dot(perm.astype(BF16), h2_keep[...], preferred_element_type=F32)
        _rows_to_tiles(xst_ref, sorted_rows, TOP_K * tm)

    @pl.when(step == 0)
    def _():
        h2_keep[...] = jnp.zeros_like(h2_keep)
        pos_keep[...] = jnp.full(pos_keep.shape, -1, I32)

    @pl.when(step < n_tiles)
    def _():
        sort_previous_tile()
        _mixer_tile(x_ref, hist_ref, mk_ref, mv_ref, gmix_ref, win_ref, wdw_ref, bdw_ref,
                    lcg_ref, lcb_ref, lvg_ref, lvb_ref, wsp_ref, bsp_ref, wout_ref, gffn_ref,
                    wrh_ref, wrl_ref, br_ref, x1_ref, pos_ref, gate_ref, cnt_ref, off_ref,
                    histout_ref, v_ref, ext_ref, shift_ref, h2_keep, pos_keep,
                    first_of_seq=(step % tiles_per_seq) == 0,
                    ns=ns, sl=sl, carry=carry, sp_chunk=sp_chunk)

    @pl.when(step == n_tiles)
    def _():
        sort_previous_tile()


def _mixer_tile(x_ref, hist_ref, mk_ref, mv_ref, gmix_ref, win_ref, wdw_ref, bdw_ref,
                lcg_ref, lcb_ref, lvg_ref, lvb_ref, wsp_ref, bsp_ref, wout_ref, gffn_ref,
                wrh_ref, wrl_ref, br_ref, x1_ref, pos_ref, gate_ref, cnt_ref, off_ref,
                histout_ref, v_ref, ext_ref, shift_ref, h2_keep, pos_keep,
                *, first_of_seq, ns, sl, carry, sp_chunk):
    tm = ns * sl
    x = x_ref[0]
    h = _rmsnorm(x, gmix_ref[...])
    z = jnp.dot(h.astype(BF16), win_ref[...], preferred_element_type=F32)
    z_a = z[:, 0:C_CONV]
    z_g = z[:, C_CONV:2 * C_CONV]
    z_u = z[:, 2 * C_CONV:2 * C_CONV + C_GMLP]
    z_v = z[:, 2 * C_CONV + C_GMLP:2 * C_CONV + 2 * C_GMLP]
    z_q = z[:, 2 * C_CONV + 2 * C_GMLP:IN_COLS]

    glu = z_a * _sigmoid(z_g)
    if carry:
        ext_ref[:, 0:HIST, :] = jnp.where(first_of_seq, hist_ref[0], ext_ref[:, 0:HIST, :])
    else:
        ext_ref[:, 0:HIST, :] = hist_ref[0]
    conv_parts = []
    for s in range(ns):
        ext_s = ext_ref.at[s]
        ext_s[HIST:HIST + sl, :] = glu[s * sl:(s + 1) * sl]
        n_shift = HIST + sl - SUBLANES
        for r in range(1, SUBLANES):
            shift_ref[s, r - 1, 0:n_shift, :] = ext_s[pl.ds(r, n_shift), :]
        rc = min(sl, 64)
        for r0 in range(0, sl, rc):
            acc = jnp.broadcast_to(bdw_ref[...], (rc, C_CONV))
            for j in range(CONV_W):
                a, r = divmod(j + HIST_OFF, SUBLANES)
                src = ext_s if r == 0 else shift_ref.at[s, r - 1]
                acc = acc + wdw_ref[j:j + 1, :] * src[pl.ds(r0 + a * SUBLANES, rc), :]
            conv_parts.append(acc)
        new_hist = ext_s[sl:sl + HIST, :]
        histout_ref[0, s] = new_hist
        if carry:
            ext_s[0:HIST, :] = new_hist
    y = jnp.concatenate(conv_parts, axis=0) if len(conv_parts) > 1 else conv_parts[0]
    y = _layernorm(y, lcg_ref[...], lcb_ref[...])
    c_out = y * _sigmoid(y)

    u = _gelu(z_u)
    v = _layernorm(_gelu(z_v), lvg_ref[...], lvb_ref[...])
    v_ref[...] = v
    vb = v.astype(BF16)
    rr = lax.broadcasted_iota(I32, (GMLP_CHUNK, GMLP_CHUNK), 0)
    cc = lax.broadcasted_iota(I32, (GMLP_CHUNK, GMLP_CHUNK), 1)
    sp_mask = (cc <= rr) & ((rr // sp_chunk) == (cc // sp_chunk))
    col = lax.broadcasted_iota(I32, (GMLP_CHUNK, C_GMLP), 1)
    w_heads = [jnp.where(sp_mask, wsp_ref[hh], 0.0).astype(BF16) for hh in range(GMLP_HEADS)]
    g_parts = []
    for c in range(tm // GMLP_CHUNK):
        vc = vb[c * GMLP_CHUNK:(c + 1) * GMLP_CHUNK]
        sg = bsp_ref[...]
        for hh in range(GMLP_HEADS):
            head_cols = (col >= hh * GMLP_HD) & (col < (hh + 1) * GMLP_HD)
            vh = jnp.where(head_cols, vc, jnp.zeros_like(vc))
            sg = sg + jnp.dot(w_heads[hh], vh, preferred_element_type=F32)
        g_parts.append(u[c * GMLP_CHUNK:(c + 1) * GMLP_CHUNK] * sg)
    g_out = jnp.concatenate(g_parts, axis=0) if len(g_parts) > 1 else g_parts[0]

    qs = z_q * np.float32(XA_HD ** -0.5)
    qcol = lax.broadcasted_iota(I32, (sl, C_XA), 1)
    a_parts = []
    for s in range(ns):
        q_s = qs[s * sl:(s + 1) * sl]
        kb = mk_ref[0, s].astype(BF16)
        vvb = mv_ref[0, s].astype(BF16)
        hmasks = [(qcol >= hh * XA_HD) & (qcol < (hh + 1) * XA_HD) for hh in range(XA_HEADS)]
        stack = XA_HEADS if sl * XA_HEADS <= N_MEM else 1
        a_s = jnp.zeros((sl, C_XA), F32)
        for h0 in range(0, XA_HEADS, stack):
            heads = range(h0, h0 + stack)
            qh = jnp.concatenate([jnp.where(hmasks[hh], q_s, 0.0) for hh in heads], axis=0).astype(BF16)
            sc = lax.dot_general(qh, kb, (((1,), (1,)), ((), ())), preferred_element_type=F32)
            p = jnp.exp(sc - jnp.max(sc, axis=-1, keepdims=True))
            den = jnp.sum(p, axis=-1, keepdims=True)
            oh = jnp.dot(p.astype(BF16), vvb, preferred_element_type=F32) / den
            for n, hh in enumerate(heads):
                a_s = a_s + jnp.where(hmasks[hh], oh[n * sl:(n + 1) * sl], 0.0)
        a_parts.append(a_s)
    a_out = jnp.concatenate(a_parts, axis=0) if len(a_parts) > 1 else a_parts[0]

    mix = jnp.concatenate([c_out, g_out, a_out], axis=-1).astype(BF16)
    x1 = x + jnp.dot(mix, wout_ref[...], preferred_element_type=F32)
    x1_ref[...] = x1

    h2 = _rmsnorm(x1, gffn_ref[...])
    h2_hi = h2.astype(BF16)
    h2_lo = (h2 - h2_hi.astype(F32)).astype(BF16)
    nt_dims = (((1,), (1,)), ((), ()))
    lg = (lax.dot_general(wrh_ref[...], h2_hi, nt_dims, preferred_element_type=F32)
          + lax.dot_general(wrl_ref[...], h2_hi, nt_dims, preferred_element_type=F32)
          + lax.dot_general(wrh_ref[...], h2_lo, nt_dims, preferred_element_type=F32)
          + br_ref[...])
    eio = lax.broadcasted_iota(I32, (N_EXPERTS, tm), 0)
    work = lg
    vals, idxs = [], []
    for _ in range(TOP_K):
        m = jnp.max(work, axis=0, keepdims=True)
        idx = jnp.min(jnp.where(work == m, eio, N_EXPERTS), axis=0, keepdims=True)
        vals.append(m)
        idxs.append(idx)
        work = jnp.where(eio == idx, -jnp.inf, work)
    exps = [jnp.exp(vk - vals[0]) for vk in vals]
    den = exps[0] + exps[1] + exps[2] + exps[3]
    gate_ref[...] = jnp.concatenate([ek / den for ek in exps], axis=0)

    sel = jnp.zeros((N_EXPERTS, tm), F32)
    for idx in idxs:
        sel = sel + jnp.where(eio == idx, 1.0, 0.0)
    selb = sel.astype(BF16)
    tr = lax.broadcasted_iota(I32, (tm, tm), 0)
    tc = lax.broadcasted_iota(I32, (tm, tm), 1)
    before = jnp.where(tr < tc, 1.0, 0.0).astype(BF16)
    ranks = jnp.dot(selb, before, preferred_element_type=F32)
    er = lax.broadcasted_iota(I32, (N_EXPERTS, N_EXPERTS), 0)
    ec = lax.broadcasted_iota(I32, (N_EXPERTS, N_EXPERTS), 1)
    lower = jnp.where(ec < er, 1.0, 0.0).astype(BF16)
    cnt = jnp.sum(sel, axis=1, keepdims=True)
    off = jnp.sum(jnp.dot(lower, selb, preferred_element_type=F32), axis=1, keepdims=True)
    cnt_ref[0] = jnp.broadcast_to(cnt, (N_EXPERTS, LANES)).astype(I32)
    off_ref[0] = jnp.broadcast_to(off, (N_EXPERTS, LANES)).astype(I32)
    slot_of = off + ranks
    pos = [jnp.sum(jnp.where(eio == idx, slot_of, 0.0), axis=0, keepdims=True).astype(I32) for idx in idxs]
    pos_all = jnp.concatenate(pos, axis=0)
    pos_ref[...] = pos_all
    pos_keep[...] = pos_all
    h2_keep[...] = h2_hi


def _mixer(x, hist, mem_k, mem_v, wts, *, ns, sl, carry, sp_chunk, sorted_rows_total, sorted_row0, xs_prev):
    g, r, _ = x.shape
    tm = ns * sl
    nt = r // tm
    ntot = g * nt
    assert sorted_row0 % (TOP_K * tm) == 0
    xs_blk0 = sorted_row0 // (TOP_K * tm)
    tile = lambda s: jnp.minimum(s, ntot - 1)
    const2 = lambda s: (0, 0)
    const3 = lambda s: (0, 0, 0)
    tile_row = lambda s: (tile(s), 0)
    tile_lane = lambda s: (0, tile(s))
    per_seq = lambda s: (tile(s) // nt, 0, 0, 0)
    in_specs = [
        pl.BlockSpec((1, tm, D_MODEL), lambda s: (tile(s) // nt, tile(s) % nt, 0)),
        pl.BlockSpec((1, ns, HIST, C_CONV), per_seq),
        pl.BlockSpec((1, ns, N_MEM, C_XA), per_seq),
        pl.BlockSpec((1, ns, N_MEM, C_XA), per_seq),
        pl.BlockSpec((1, D_MODEL), const2),
        pl.BlockSpec((D_MODEL, IN_COLS), const2),
        pl.BlockSpec((HIST, C_CONV), const2),
        pl.BlockSpec((1, C_CONV), const2),
        pl.BlockSpec((1, C_CONV), const2),
        pl.BlockSpec((1, C_CONV), const2),
        pl.BlockSpec((1, C_GMLP), const2),
        pl.BlockSpec((1, C_GMLP), const2),
        pl.BlockSpec((GMLP_HEADS, GMLP_CHUNK, GMLP_CHUNK), const3),
        pl.BlockSpec((GMLP_CHUNK, C_GMLP), const2),
        pl.BlockSpec((D_MODEL, D_MODEL), const2),
        pl.BlockSpec((1, D_MODEL), const2),
        pl.BlockSpec((N_EXPERTS, D_MODEL), const2),
        pl.BlockSpec((N_EXPERTS, D_MODEL), const2),
        pl.BlockSpec((N_EXPERTS, 1), const2),
    ]
    tile_cnt = lambda s: (tile(s), 0, 0)
    out_specs = [
        pl.BlockSpec((tm, D_MODEL), tile_row),
        pl.BlockSpec((TOP_K * tm * ROW_TILES, LANES),
                     lambda s: (xs_blk0 + jnp.maximum(s - 1, 0), 0)),
        pl.BlockSpec((TOP_K, tm), tile_lane),
        pl.BlockSpec((TOP_K, tm), tile_lane),
        pl.BlockSpec((1, N_EXPERTS, LANES), tile_cnt),
        pl.BlockSpec((1, N_EXPERTS, LANES), tile_cnt),
        pl.BlockSpec((1, ns, HIST, C_CONV), per_seq),
        pl.BlockSpec((tm, C_GMLP), tile_row),
    ]
    rows = g * r
    out_shape = [
        jax.ShapeDtypeStruct((rows, D_MODEL), F32),
        jax.ShapeDtypeStruct((sorted_rows_total * ROW_TILES, LANES), F32),
        jax.ShapeDtypeStruct((TOP_K, rows), I32),
        jax.ShapeDtypeStruct((TOP_K, rows), F32),
        jax.ShapeDtypeStruct((ntot, N_EXPERTS, LANES), I32),
        jax.ShapeDtypeStruct((ntot, N_EXPERTS, LANES), I32),
        jax.ShapeDtypeStruct((g, ns, HIST, C_CONV), F32),
        jax.ShapeDtypeStruct((rows, C_GMLP), F32),
    ]
    args = [x, hist, mem_k, mem_v, *wts]
    aliases = {}
    if xs_prev is not None:
        in_specs.append(pl.BlockSpec(memory_space=pl.ANY))
        aliases = {len(args): 1}
        args.append(xs_prev)
    kern = functools.partial(_mixer_kernel, ns=ns, sl=sl, carry=carry, sp_chunk=sp_chunk,
                             tiles_per_seq=nt)
    return pl.pallas_call(
        kern,
        grid=(ntot + 1,),
        in_specs=in_specs,
        out_specs=out_specs,
        out_shape=out_shape,
        scratch_shapes=[pltpu.VMEM((ns, HIST + sl, C_CONV), F32),
                        pltpu.VMEM((ns, SUBLANES - 1, HIST + sl, C_CONV), F32),
                        pltpu.VMEM((tm, D_MODEL), BF16),
                        pltpu.VMEM((TOP_K, tm), I32)],
        input_output_aliases=aliases,
        compiler_params=pltpu.CompilerParams(
            dimension_semantics=("arbitrary",),
            vmem_limit_bytes=VMEM_LIMIT_MIXER),
        name="mixer_carry" if carry else "mixer_cache",
    )(*args)


def _strip_pieces(n, max_rows, fn):
    done = 0
    p = max_rows
    while p >= 1:
        has = (n & p) != 0
        pl.when(has)(functools.partial(fn, done, p))
        done = done + jnp.where(has, p, 0)
        p //= 2


def _rows(ref, first_row, rows):
    return ref.at[pl.ds(pl.multiple_of(first_row * ROW_TILES, ROW_TILES), rows * ROW_TILES)]


def _moe_kernel(b0_ref, nbk_ref, bv_ref, nb_ref, slo_ref, shi_ref, ssrc_ref, sdst_ref, slen_ref,
                xs_hbm, wgu_ref, bgu_ref, wdn_ref, bdn_ref, ys_hbm,
                xbuf, ybuf, wgu16, wdn16, sems, ysems):
    e = pl.program_id(0)
    n_used = nb_ref[0]
    half = MOE_BLOCK // 2

    def gather_start(b, slot):
        row0 = b * MOE_BLOCK

        def strip(s, carry):
            lo = jnp.maximum(sdst_ref[s], row0)
            hi = jnp.minimum(sdst_ref[s] + slen_ref[s], row0 + MOE_BLOCK)
            src0 = ssrc_ref[s] + (lo - sdst_ref[s])
            dst0 = lo - row0

            def piece(first, rows):
                pltpu.make_async_copy(_rows(xs_hbm, src0 + first, rows),
                                      _rows(xbuf.at[slot], dst0 + first, rows), sems.at[slot]).start()
            _strip_pieces(hi - lo, MOE_BLOCK, piece)
            return carry
        lax.fori_loop(slo_ref[b], shi_ref[b], strip, 0)

    def gather_wait(b, slot):
        def piece(first, rows):
            del first
            pltpu.make_async_copy(_rows(xs_hbm, 0, rows), _rows(xbuf.at[slot], 0, rows), sems.at[slot]).wait()
        _strip_pieces(bv_ref[b], MOE_BLOCK, piece)

    def out_copy(b, slot):
        return pltpu.make_async_copy(ybuf.at[slot], _rows(ys_hbm, b * MOE_BLOCK, MOE_BLOCK), ysems.at[slot])

    def expert_mlp(xslot, slot, h, valid):
        row_ok = lax.broadcasted_iota(I32, (half, 1), 0) + h * half < valid
        xb = jnp.where(row_ok, _tiles_to_rows(xbuf.at[xslot], half, h * half), 0.0).astype(BF16)
        gu = jnp.dot(xb, wgu16[...], preferred_element_type=F32) + bgu_ref[0]
        gate = jnp.minimum(gu[:, :D_FF], SWIGLU_LIMIT)
        up = jnp.clip(gu[:, D_FF:], -SWIGLU_LIMIT, SWIGLU_LIMIT)
        act = (up + 1.0) * (gate * _sigmoid(SWIGLU_ALPHA * gate))
        yb = jnp.dot(act.astype(BF16), wdn16[...], preferred_element_type=F32) + bdn_ref[0]
        _rows_to_tiles(ybuf.at[slot], yb, half, h * half)

    @pl.when(e == 0)
    def _():
        for a in range(GATHER_AHEAD):
            @pl.when(a < n_used)
            def _():
                gather_start(a, a)

    @pl.when(nbk_ref[e] > 0)
    def _():
        def cast_rows(c, carry):
            r = pl.multiple_of(c * LANES, LANES)
            wgu16[pl.ds(r, LANES), :] = wgu_ref[pl.ds(r, LANES), :].astype(BF16)
            wdn16[pl.ds(r, LANES), :] = wdn_ref[pl.ds(r, LANES), :].astype(BF16)
            return carry
        lax.fori_loop(0, D_MODEL // LANES, cast_rows, 0)

        def block(j, carry):
            b = b0_ref[e] + j
            slot = b % 2
            xslot = b % (GATHER_AHEAD + 1)

            @pl.when(b + GATHER_AHEAD < n_used)
            def _():
                gather_start(b + GATHER_AHEAD, (b + GATHER_AHEAD) % (GATHER_AHEAD + 1))

            gather_wait(b, xslot)

            @pl.when(b >= 2)
            def _():
                out_copy(b - 2, slot).wait()

            valid = bv_ref[b]
            expert_mlp(xslot, slot, 0, valid)

            @pl.when(valid > half)
            def _():
                expert_mlp(xslot, slot, 1, valid)

            @pl.when(valid <= half)
            def _():
                ybuf[slot, pl.ds(half * ROW_TILES, half * ROW_TILES), :] = jnp.zeros(
                    (half * ROW_TILES, LANES), F32)

            out_copy(b, slot).start()
            return carry
        lax.fori_loop(0, nbk_ref[e], block, 0)

    @pl.when(e == pl.num_programs(0) - 1)
    def _():
        for back in (1, 2):
            @pl.when(n_used >= back)
            def _():
                out_copy(n_used - back, (n_used - back) % 2).wait()


def _moe(tables, xs, slots, w_gu, b_gu, w_dn, b_dn):
    wsel = lambda e, *_: (e, 0, 0)
    grid_spec = pltpu.PrefetchScalarGridSpec(
        num_scalar_prefetch=len(tables),
        grid=(N_EXPERTS,),
        in_specs=[pl.BlockSpec(memory_space=pl.ANY),
                  pl.BlockSpec((None, D_MODEL, 2 * D_FF), wsel),
                  pl.BlockSpec((None, 1, 2 * D_FF), wsel),
                  pl.BlockSpec((None, D_FF, D_MODEL), wsel),
                  pl.BlockSpec((None, 1, D_MODEL), wsel)],
        out_specs=pl.BlockSpec(memory_space=pl.ANY),
        scratch_shapes=[pltpu.VMEM((GATHER_AHEAD + 1, MOE_BLOCK * ROW_TILES, LANES), F32),
                        pltpu.VMEM((2, MOE_BLOCK * ROW_TILES, LANES), F32),
                        pltpu.VMEM((D_MODEL, 2 * D_FF), BF16),
                        pltpu.VMEM((D_FF, D_MODEL), BF16),
                        pltpu.SemaphoreType.DMA((GATHER_AHEAD + 1,)),
                        pltpu.SemaphoreType.DMA((2,))],
    )
    return pl.pallas_call(
        _moe_kernel,
        grid_spec=grid_spec,
        out_shape=jax.ShapeDtypeStruct((slots * ROW_TILES, LANES), F32),
        compiler_params=pltpu.CompilerParams(
            dimension_semantics=("arbitrary",),
            vmem_limit_bytes=VMEM_LIMIT_MOE),
        name="moe",
    )(*tables, xs, w_gu, b_gu, w_dn, b_dn)


def _combine_kernel(csrc_ref, clen_ref, coff_ref, x1_ref, pos_ref, gate_ref, gfin_ref, ys_hbm,
                    out_ref, ybuf, sems, *, tm):
    i = pl.program_id(0)
    n_tiles = pl.num_programs(0)
    n_sorted = TOP_K * tm

    def gather_start(t, slot):
        def strip(e, carry):
            s = t * N_EXPERTS + e
            src0 = csrc_ref[s]
            dst0 = coff_ref[s]

            def piece(first, rows):
                pltpu.make_async_copy(_rows(ys_hbm, src0 + first, rows),
                                      _rows(ybuf.at[slot], dst0 + first, rows), sems.at[slot]).start()
            _strip_pieces(clen_ref[s], tm, piece)
            return carry
        lax.fori_loop(0, N_EXPERTS, strip, 0)

    @pl.when(i == 0)
    def _():
        gather_start(0, 0)

    @pl.when(i + 1 < n_tiles)
    def _():
        gather_start(i + 1, (i + 1) % 2)

    slot = i % 2
    pltpu.make_async_copy(_rows(ys_hbm, 0, n_sorted), ybuf.at[slot], sems.at[slot]).wait()
    rr = lax.broadcasted_iota(I32, (tm, tm), 0)
    cc = lax.broadcasted_iota(I32, (tm, tm), 1)
    eye = rr == cc
    p_cols = [jnp.sum(jnp.where(eye, pos_ref[k:k + 1, :].astype(F32), 0.0), axis=1, keepdims=True)
              for k in range(TOP_K)]
    g_cols = [jnp.sum(jnp.where(eye, gate_ref[k:k + 1, :], 0.0), axis=1, keepdims=True)
              for k in range(TOP_K)]
    acc = x1_ref[...]
    for c in range(TOP_K):
        jl = (lax.broadcasted_iota(I32, (tm, tm), 1) + c * tm).astype(F32)
        unsort = jnp.zeros((tm, tm), F32)
        for k in range(TOP_K):
            unsort = jnp.where(jl == p_cols[k], g_cols[k], unsort)
        y_chunk = _tiles_to_rows(ybuf.at[slot], tm, c * tm).astype(BF16)
        acc = acc + jnp.dot(unsort.astype(BF16), y_chunk, preferred_element_type=F32)
    out_ref[...] = _rmsnorm(acc, gfin_ref[...])


def _combine(tables, x1, pos, gates, g_final, ys, tm):
    t = x1.shape[0]
    nt = t // tm
    grid_spec = pltpu.PrefetchScalarGridSpec(
        num_scalar_prefetch=len(tables),
        grid=(nt,),
        in_specs=[pl.BlockSpec((tm, D_MODEL), lambda i, *_: (i, 0)),
                  pl.BlockSpec((TOP_K, tm), lambda i, *_: (0, i)),
                  pl.BlockSpec((TOP_K, tm), lambda i, *_: (0, i)),
                  pl.BlockSpec((1, D_MODEL), lambda i, *_: (0, 0)),
                  pl.BlockSpec(memory_space=pl.ANY)],
        out_specs=pl.BlockSpec((tm, D_MODEL), lambda i, *_: (i, 0)),
        scratch_shapes=[pltpu.VMEM((2, TOP_K * tm * ROW_TILES, LANES), F32),
                        pltpu.SemaphoreType.DMA((2,))],
    )
    return pl.pallas_call(
        functools.partial(_combine_kernel, tm=tm),
        grid_spec=grid_spec,
        out_shape=jax.ShapeDtypeStruct((t, D_MODEL), F32),
        compiler_params=pltpu.CompilerParams(
            dimension_semantics=("arbitrary",),
            vmem_limit_bytes=VMEM_LIMIT_MIXER),
        name="combine",
    )(*tables, x1, pos, gates, g_final, ys)


def _split_bf16(w):
    hi = w.astype(BF16)
    lo = (w - hi.astype(F32)).astype(BF16)
    return hi, lo


def kernel(x_prompt, x_sample, cache_conv, cache_mem_k, cache_mem_v, mem_prompt, g_mix, w_in, w_dw, b_dw, ln_conv_g, ln_conv_b, ln_v_g, ln_v_b, w_spatial, b_spatial, g_mem, w_mem_k, w_mem_v, w_out, g_ffn, w_router, b_router, w_gate_up, b_gate_up, w_down, b_down, g_final):
    depth = g_mix.shape[0]
    assert depth == 1
    l = 0
    bp, seq, _ = x_prompt.shape
    bs, dseq, _ = x_sample.shape
    assert seq % PROMPT_TILE == 0 and bs % SAMPLE_SEQS_PER_TILE == 0
    assert GMLP_CHUNK % dseq == 0 and (SAMPLE_SEQS_PER_TILE * dseq) % GMLP_CHUNK == 0

    row = lambda a: a.reshape(1, -1)
    wr_hi, wr_lo = _split_bf16(w_router[l].T)
    w_dw_pad = jnp.pad(w_dw[l], ((0, HIST - CONV_W), (0, 0)))
    bias_rows = lambda b: jnp.repeat(b.T, GMLP_HD, axis=1)
    common = dict(
        gmix=row(g_mix[l]), win=w_in[l].astype(BF16), wdw=w_dw_pad, bdw=row(b_dw[l]),
        lcg=row(ln_conv_g[l]), lcb=row(ln_conv_b[l]), lvg=row(ln_v_g[l]), lvb=row(ln_v_b[l]),
        wout=w_out[l].astype(BF16), gffn=row(g_ffn[l]), wrh=wr_hi, wrl=wr_lo,
        br=b_router[l].reshape(N_EXPERTS, 1))

    def weights(wsp, bsp):
        c = common
        return (c["gmix"], c["win"], c["wdw"], c["bdw"], c["lcg"], c["lcb"], c["lvg"], c["lvb"],
                wsp, bsp, c["wout"], c["gffn"], c["wrh"], c["wrl"], c["br"])

    reps = GMLP_CHUNK // dseq
    wts_p = weights(w_spatial[l], bias_rows(b_spatial[l]))
    wts_s = weights(jnp.tile(w_spatial[l][:, :dseq, :dseq], (1, reps, reps)),
                    bias_rows(jnp.tile(b_spatial[l][:, :dseq], (1, reps))))

    w_kv = jnp.concatenate([w_mem_k[l], w_mem_v[l]], axis=1).astype(BF16)
    kv_p = _memkv(mem_prompt, row(g_mem[l]), w_kv)
    mk_p = kv_p[:, :, :C_XA]
    mv_p = kv_p[:, :, C_XA:]
    zero_hist = jnp.zeros((bp, 1, HIST, C_CONV), F32)
    tp, ts = bp * seq, bs * dseq
    n_assign = (tp + ts) * TOP_K
    (x1_p, xs, pos_p, gate_p, cnt_p, off_p, hist_p, _) = _mixer(
        x_prompt, zero_hist, mk_p[:, None], mv_p[:, None], wts_p,
        ns=1, sl=PROMPT_TILE, carry=True, sp_chunk=GMLP_CHUNK,
        sorted_rows_total=n_assign, sorted_row0=0, xs_prev=None)

    gs = bs // SAMPLE_SEQS_PER_TILE
    hist_s_in = jnp.pad(cache_conv[l], ((0, 0), (HIST_OFF, 0), (0, 0))).reshape(
        gs, SAMPLE_SEQS_PER_TILE, HIST, C_CONV)
    mk_s = cache_mem_k[l].reshape(gs, SAMPLE_SEQS_PER_TILE, N_MEM, C_XA)
    mv_s = cache_mem_v[l].reshape(gs, SAMPLE_SEQS_PER_TILE, N_MEM, C_XA)
    tm_s = SAMPLE_SEQS_PER_TILE * dseq
    (x1_s, xs, pos_s, gate_s, cnt_s, off_s, hist_s, v_s) = _mixer(
        x_sample.reshape(gs, tm_s, D_MODEL), hist_s_in, mk_s, mv_s, wts_s,
        ns=SAMPLE_SEQS_PER_TILE, sl=dseq, carry=False, sp_chunk=dseq,
        sorted_rows_total=n_assign, sorted_row0=tp * TOP_K, xs_prev=xs)

    n_blocks = -(-n_assign // MOE_BLOCK) + N_EXPERTS
    ntp, nts = tp // PROMPT_TILE, ts // tm_s
    cnt = jnp.concatenate([cnt_p[:, :, 0], cnt_s[:, :, 0]], axis=0)
    off = jnp.concatenate([off_p[:, :, 0], off_s[:, :, 0]], axis=0)
    tile_row0 = np.concatenate([np.arange(ntp) * (TOP_K * PROMPT_TILE),
                                tp * TOP_K + np.arange(nts) * (TOP_K * tm_s)]).astype(np.int32)
    counts = jnp.sum(cnt, axis=0)
    tile_base = jnp.cumsum(cnt, axis=0) - cnt
    padded = (counts + MOE_BLOCK - 1) // MOE_BLOCK * MOE_BLOCK
    pad_end = jnp.cumsum(padded)
    pad_start = pad_end - padded
    strip_dst = (pad_start[None, :] + tile_base).astype(I32)
    strip_src = (tile_row0[:, None] + off).astype(I32)
    n_used = (pad_end[-1] // MOE_BLOCK).astype(I32)
    blk_row0 = jnp.minimum(jnp.arange(n_blocks, dtype=I32), n_used - 1) * MOE_BLOCK
    block_expert = jnp.minimum(
        jnp.sum((pad_end[None, :] <= blk_row0[:, None]).astype(I32), axis=1), N_EXPERTS - 1)
    of_block = block_expert[:, None] == jnp.arange(N_EXPERTS, dtype=I32)[None, :]
    last_row = jnp.sum(jnp.where(of_block, (pad_start + counts)[None, :], 0), axis=1)
    block_valid = jnp.clip(last_row - blk_row0, 0, MOE_BLOCK).astype(I32)
    sdst = strip_dst.T.reshape(-1)
    ssrc = strip_src.T.reshape(-1)
    slen = cnt.T.reshape(-1).astype(I32)
    s_lo = jnp.sum(((sdst + slen)[None, :] <= blk_row0[:, None]).astype(I32), axis=1)
    s_hi = jnp.sum((sdst[None, :] < (blk_row0 + MOE_BLOCK)[:, None]).astype(I32), axis=1)
    moe_tables = ((pad_start // MOE_BLOCK).astype(I32), (padded // MOE_BLOCK).astype(I32),
                  block_valid, n_used.reshape(1),
                  s_lo.astype(I32), s_hi.astype(I32), ssrc, sdst, slen)

    slots = n_blocks * MOE_BLOCK
    ys = _moe(moe_tables, xs, slots,
              w_gate_up[l], b_gate_up[l][:, None, :], w_down[l], b_down[l][:, None, :])
    gfin = row(g_final)

    def combine_tables(t0, t1):
        return (strip_dst[t0:t1].reshape(-1), cnt[t0:t1].reshape(-1).astype(I32),
                off[t0:t1].reshape(-1).astype(I32))

    y_p = _combine(combine_tables(0, ntp), x1_p, pos_p, gate_p, gfin, ys, PROMPT_TILE)
    y_s = _combine(combine_tables(ntp, ntp + nts), x1_s, pos_s, gate_s, gfin, ys, tm_s)

    return (y_p.reshape(bp, seq, D_MODEL),
            y_s.reshape(bs, dseq, D_MODEL),
            hist_p[:, 0, HIST_OFF:, :][None],
            mk_p.reshape(bp, N_MEM, XA_HEADS, XA_HD)[None],
            mv_p.reshape(bp, N_MEM, XA_HEADS, XA_HD)[None],
            hist_s.reshape(bs, HIST, C_CONV)[:, HIST_OFF:, :][None],
            v_s.reshape(bs, dseq, C_GMLP)[None])
```

```python
import functools

import numpy as np
import jax
import jax.numpy as jnp
from jax import lax
from jax.experimental import pallas as pl
from jax.experimental.pallas import tpu as pltpu

F32 = jnp.float32
BF16 = jnp.bfloat16
I32 = jnp.int32
U32 = jnp.uint32

D_MODEL = 1024
C_CONV = 384
CONV_W = 31
C_GMLP = 384
GMLP_HEADS = 4
GMLP_HD = 96
GMLP_CHUNK = 128
XA_HEADS = 4
XA_HD = 64
C_XA = 256
N_MEM = 256
N_EXPERTS = 32
TOP_K = 4
D_FF = 1024
SWIGLU_LIMIT = 7.0
SWIGLU_ALPHA = 1.702
EPS = 1e-5
IN_COLS = 2 * C_CONV + 2 * C_GMLP + C_XA

SUBLANES = 8
LANES = 128
ROW_TILES = D_MODEL // LANES
HIST = 32
HIST_OFF = HIST - (CONV_W - 1)

PROMPT_TILE = 512
SAMPLE_SEQS_PER_TILE = 8
MOE_BLOCK = 512
MOE_PAIRS = MOE_BLOCK // 2
GATHER_AHEAD = 2
VMEM_LIMIT_MIXER = 48 * 1024 * 1024
VMEM_LIMIT_MOE = 52 * 1024 * 1024


def _rmsnorm(x, g):
    return x * lax.rsqrt(jnp.mean(x * x, axis=-1, keepdims=True) + EPS) * g


def _layernorm(x, g, b):
    mu = jnp.mean(x, axis=-1, keepdims=True)
    xc = x - mu
    var = jnp.mean(xc * xc, axis=-1, keepdims=True)
    return xc * lax.rsqrt(var + EPS) * g + b


def _gelu(x):
    return 0.5 * x * (1.0 + lax.erf(x * np.float32(1.0 / np.sqrt(2.0))))


def _sigmoid(x):
    return 1.0 / (1.0 + jnp.exp(-x))


def _rows_to_tiles(dst_ref, val, rows, row0=0):
    for j in range(ROW_TILES):
        dst_ref[pl.ds(row0 * ROW_TILES + j, rows, stride=ROW_TILES), :] = val[:, j * LANES:(j + 1) * LANES]


def _tiles_to_rows(src_ref, rows, row0=0):
    return jnp.concatenate(
        [src_ref[pl.ds(row0 * ROW_TILES + j, rows, stride=ROW_TILES), :] for j in range(ROW_TILES)],
        axis=-1)


def _memkv_kernel(mem_ref, g_ref, w_ref, o_ref):
    mn = _rmsnorm(mem_ref[0], g_ref[...])
    o_ref[0] = jnp.dot(mn.astype(BF16), w_ref[...], preferred_element_type=F32)


def _memkv(mem, g_mem, w_kv):
    b = mem.shape[0]
    return pl.pallas_call(
        _memkv_kernel,
        grid=(b,),
        in_specs=[pl.BlockSpec((1, N_MEM, D_MODEL), lambda i: (i, 0, 0)),
                  pl.BlockSpec((1, D_MODEL), lambda i: (0, 0)),
                  pl.BlockSpec((D_MODEL, 2 * C_XA), lambda i: (0, 0))],
        out_specs=pl.BlockSpec((1, N_MEM, 2 * C_XA), lambda i: (i, 0, 0)),
        out_shape=jax.ShapeDtypeStruct((b, N_MEM, 2 * C_XA), F32),
        name="memkv",
    )(mem, g_mem, w_kv)


def _mixer_kernel(x_ref, hist_ref, mk_ref, mv_ref, gmix_ref, win_ref, wdw_ref, bdw_ref,
                  lcg_ref, lcb_ref, lvg_ref, lvb_ref, wsp_ref, bsp_ref, wout_ref, gffn_ref,
                  wrh_ref, wrl_ref, br_ref, *rest, ns, sl, carry, sp_chunk, tiles_per_seq, n_sorted):
    (x1_ref, xst_ref, pos_ref, gate_ref, cnt_ref, off_ref, histout_ref, v_ref,
     ext_ref, shift_ref, h2_keep, pos_keep) = rest[-12:]
    tm = ns * sl
    step = pl.program_id(0)
    n_tiles = pl.num_programs(0) - 1

    def sort_previous_tile():
        jj = lax.broadcasted_iota(I32, (n_sorted, tm), 0)
        perm = jnp.zeros((n_sorted, tm), F32)
        for k in range(TOP_K):
            perm = jnp.where(jj == pos_keep[k:k + 1, :], 1.0, perm)
        sorted_rows = jnp.dot(perm.astype(BF16), h2_keep[...], preferred_element_type=F32)
        pairs = pltpu.bitcast(sorted_rows.astype(BF16), U32)
        _rows_to_tiles(xst_ref, pairs, n_sorted // 2)

    @pl.when(step == 0)
    def _():
        h2_keep[...] = jnp.zeros_like(h2_keep)
        pos_keep[...] = jnp.full(pos_keep.shape, -1, I32)

    @pl.when(step < n_tiles)
    def _():
        sort_previous_tile()
        _mixer_tile(x_ref, hist_ref, mk_ref, mv_ref, gmix_ref, win_ref, wdw_ref, bdw_ref,
                    lcg_ref, lcb_ref, lvg_ref, lvb_ref, wsp_ref, bsp_ref, wout_ref, gffn_ref,
                    wrh_ref, wrl_ref, br_ref, x1_ref, pos_ref, gate_ref, cnt_ref, off_ref,
                    histout_ref, v_ref, ext_ref, shift_ref, h2_keep, pos_keep,
                    first_of_seq=(step % tiles_per_seq) == 0,
                    ns=ns, sl=sl, carry=carry, sp_chunk=sp_chunk)

    @pl.when(step == n_tiles)
    def _():
        sort_previous_tile()


def _mixer_tile(x_ref, hist_ref, mk_ref, mv_ref, gmix_ref, win_ref, wdw_ref, bdw_ref,
                lcg_ref, lcb_ref, lvg_ref, lvb_ref, wsp_ref, bsp_ref, wout_ref, gffn_ref,
                wrh_ref, wrl_ref, br_ref, x1_ref, pos_ref, gate_ref, cnt_ref, off_ref,
                histout_ref, v_ref, ext_ref, shift_ref, h2_keep, pos_keep,
                *, first_of_seq, ns, sl, carry, sp_chunk):
    tm = ns * sl
    x = x_ref[0]
    h = _rmsnorm(x, gmix_ref[...])
    z = jnp.dot(h.astype(BF16), win_ref[...], preferred_element_type=F32)
    z_a = z[:, 0:C_CONV]
    z_g = z[:, C_CONV:2 * C_CONV]
    z_u = z[:, 2 * C_CONV:2 * C_CONV + C_GMLP]
    z_v = z[:, 2 * C_CONV + C_GMLP:2 * C_CONV + 2 * C_GMLP]
    z_q = z[:, 2 * C_CONV + 2 * C_GMLP:IN_COLS]

    glu = z_a * _sigmoid(z_g)
    if carry:
        ext_ref[:, 0:HIST, :] = jnp.where(first_of_seq, hist_ref[0], ext_ref[:, 0:HIST, :])
    else:
        ext_ref[:, 0:HIST, :] = hist_ref[0]
    conv_parts = []
    for s in range(ns):
        ext_s = ext_ref.at[s]
        ext_s[HIST:HIST + sl, :] = glu[s * sl:(s + 1) * sl]
        n_shift = HIST + sl - SUBLANES
        for r in range(1, SUBLANES):
            shift_ref[s, r - 1, 0:n_shift, :] = ext_s[pl.ds(r, n_shift), :]
        rc = min(sl, 64)
        for r0 in range(0, sl, rc):
            acc = jnp.broadcast_to(bdw_ref[...], (rc, C_CONV))
            for j in range(CONV_W):
                a, r = divmod(j + HIST_OFF, SUBLANES)
                src = ext_s if r == 0 else shift_ref.at[s, r - 1]
                acc = acc + wdw_ref[j:j + 1, :] * src[pl.ds(r0 + a * SUBLANES, rc), :]
            conv_parts.append(acc)
        new_hist = ext_s[sl:sl + HIST, :]
        histout_ref[0, s] = new_hist
        if carry:
            ext_s[0:HIST, :] = new_hist
    y = jnp.concatenate(conv_parts, axis=0) if len(conv_parts) > 1 else conv_parts[0]
    y = _layernorm(y, lcg_ref[...], lcb_ref[...])
    c_out = y * _sigmoid(y)

    u = _gelu(z_u)
    v = _layernorm(_gelu(z_v), lvg_ref[...], lvb_ref[...])
    v_ref[...] = v
    vb = v.astype(BF16)
    rr = lax.broadcasted_iota(I32, (GMLP_CHUNK, GMLP_CHUNK), 0)
    cc = lax.broadcasted_iota(I32, (GMLP_CHUNK, GMLP_CHUNK), 1)
    sp_mask = (cc <= rr) & ((rr // sp_chunk) == (cc // sp_chunk))
    col = lax.broadcasted_iota(I32, (GMLP_CHUNK, C_GMLP), 1)
    w_heads = [jnp.where(sp_mask, wsp_ref[hh], 0.0).astype(BF16) for hh in range(GMLP_HEADS)]
    g_parts = []
    for c in range(tm // GMLP_CHUNK):
        vc = vb[c * GMLP_CHUNK:(c + 1) * GMLP_CHUNK]
        sg = bsp_ref[...]
        for hh in range(GMLP_HEADS):
            head_cols = (col >= hh * GMLP_HD) & (col < (hh + 1) * GMLP_HD)
            vh = jnp.where(head_cols, vc, jnp.zeros_like(vc))
            sg = sg + jnp.dot(w_heads[hh], vh, preferred_element_type=F32)
        g_parts.append(u[c * GMLP_CHUNK:(c + 1) * GMLP_CHUNK] * sg)
    g_out = jnp.concatenate(g_parts, axis=0) if len(g_parts) > 1 else g_parts[0]

    qs = z_q * np.float32(XA_HD ** -0.5)
    qcol = lax.broadcasted_iota(I32, (sl, C_XA), 1)
    a_parts = []
    for s in range(ns):
        q_s = qs[s * sl:(s + 1) * sl]
        kb = mk_ref[0, s].astype(BF16)
        vvb = mv_ref[0, s].astype(BF16)
        hmasks = [(qcol >= hh * XA_HD) & (qcol < (hh + 1) * XA_HD) for hh in range(XA_HEADS)]
        stack = XA_HEADS if sl * XA_HEADS <= N_MEM else 1
        a_s = jnp.zeros((sl, C_XA), F32)
        for h0 in range(0, XA_HEADS, stack):
            heads = range(h0, h0 + stack)
            qh = jnp.concatenate([jnp.where(hmasks[hh], q_s, 0.0) for hh in heads], axis=0).astype(BF16)
            sc = lax.dot_general(qh, kb, (((1,), (1,)), ((), ())), preferred_element_type=F32)
            p = jnp.exp(sc - jnp.max(sc, axis=-1, keepdims=True))
            den = jnp.sum(p, axis=-1, keepdims=True)
            oh = jnp.dot(p.astype(BF16), vvb, preferred_element_type=F32) / den
            for n, hh in enumerate(heads):
                a_s = a_s + jnp.where(hmasks[hh], oh[n * sl:(n + 1) * sl], 0.0)
        a_parts.append(a_s)
    a_out = jnp.concatenate(a_parts, axis=0) if len(a_parts) > 1 else a_parts[0]

    mix = jnp.concatenate([c_out, g_out, a_out], axis=-1).astype(BF16)
    x1 = x + jnp.dot(mix, wout_ref[...], preferred_element_type=F32)
    x1_ref[...] = x1

    h2 = _rmsnorm(x1, gffn_ref[...])
    h2_hi = h2.astype(BF16)
    h2_lo = (h2 - h2_hi.astype(F32)).astype(BF16)
    nt_dims = (((1,), (1,)), ((), ()))
    lg = (lax.dot_general(wrh_ref[...], h2_hi, nt_dims, preferred_element_type=F32)
          + lax.dot_general(wrl_ref[...], h2_hi, nt_dims, preferred_element_type=F32)
          + lax.dot_general(wrh_ref[...], h2_lo, nt_dims, preferred_element_type=F32)
          + br_ref[...])
    eio = lax.broadcasted_iota(I32, (N_EXPERTS, tm), 0)
    work = lg
    vals, idxs = [], []
    for _ in range(TOP_K):
        m = jnp.max(work, axis=0, keepdims=True)
        idx = jnp.min(jnp.where(work == m, eio, N_EXPERTS), axis=0, keepdims=True)
        vals.append(m)
        idxs.append(idx)
        work = jnp.where(eio == idx, -jnp.inf, work)
    exps = [jnp.exp(vk - vals[0]) for vk in vals]
    den = exps[0] + exps[1] + exps[2] + exps[3]
    gate_ref[...] = jnp.concatenate([ek / den for ek in exps], axis=0)

    sel = jnp.zeros((N_EXPERTS, tm), F32)
    for idx in idxs:
        sel = sel + jnp.where(eio == idx, 1.0, 0.0)
    selb = sel.astype(BF16)
    tr = lax.broadcasted_iota(I32, (tm, tm), 0)
    tc = lax.broadcasted_iota(I32, (tm, tm), 1)
    before = jnp.where(tr < tc, 1.0, 0.0).astype(BF16)
    ranks = jnp.dot(selb, before, preferred_element_type=F32)
    er = lax.broadcasted_iota(I32, (N_EXPERTS, N_EXPERTS), 0)
    ec = lax.broadcasted_iota(I32, (N_EXPERTS, N_EXPERTS), 1)
    lower = jnp.where(ec < er, 1.0, 0.0).astype(BF16)
    cnt = jnp.sum(sel, axis=1, keepdims=True)
    half_len = jnp.floor((cnt + 1.0) * 0.5)
    off_pairs = jnp.dot(lower, jnp.broadcast_to(half_len, (N_EXPERTS, LANES)).astype(BF16),
                        preferred_element_type=F32)[:, 0:1]
    cnt_ref[0] = jnp.broadcast_to(half_len, (N_EXPERTS, LANES)).astype(I32)
    off_ref[0] = jnp.broadcast_to(off_pairs, (N_EXPERTS, LANES)).astype(I32)
    slot_of = 2.0 * off_pairs + ranks
    pos = [jnp.sum(jnp.where(eio == idx, slot_of, 0.0), axis=0, keepdims=True).astype(I32) for idx in idxs]
    pos_all = jnp.concatenate(pos, axis=0)
    pos_ref[...] = pos_all
    pos_keep[...] = pos_all
    h2_keep[...] = h2_hi


def _mixer(x, hist, mem_k, mem_v, wts, *, ns, sl, carry, sp_chunk, pairs_per_tile, pairs_total, pair0,
           xs_prev):
    g, r, _ = x.shape
    tm = ns * sl
    nt = r // tm
    ntot = g * nt
    assert 2 * pairs_per_tile >= TOP_K * tm + N_EXPERTS and pair0 % pairs_per_tile == 0
    xs_blk0 = pair0 // pairs_per_tile
    tile = lambda s: jnp.minimum(s, ntot - 1)
    const2 = lambda s: (0, 0)
    const3 = lambda s: (0, 0, 0)
    tile_row = lambda s: (tile(s), 0)
    tile_lane = lambda s: (0, tile(s))
    per_seq = lambda s: (tile(s) // nt, 0, 0, 0)
    in_specs = [
        pl.BlockSpec((1, tm, D_MODEL), lambda s: (tile(s) // nt, tile(s) % nt, 0)),
        pl.BlockSpec((1, ns, HIST, C_CONV), per_seq),
        pl.BlockSpec((1, ns, N_MEM, C_XA), per_seq),
        pl.BlockSpec((1, ns, N_MEM, C_XA), per_seq),
        pl.BlockSpec((1, D_MODEL), const2),
        pl.BlockSpec((D_MODEL, IN_COLS), const2),
        pl.BlockSpec((HIST, C_CONV), const2),
        pl.BlockSpec((1, C_CONV), const2),
        pl.BlockSpec((1, C_CONV), const2),
        pl.BlockSpec((1, C_CONV), const2),
        pl.BlockSpec((1, C_GMLP), const2),
        pl.BlockSpec((1, C_GMLP), const2),
        pl.BlockSpec((GMLP_HEADS, GMLP_CHUNK, GMLP_CHUNK), const3),
        pl.BlockSpec((GMLP_CHUNK, C_GMLP), const2),
        pl.BlockSpec((D_MODEL, D_MODEL), const2),
        pl.BlockSpec((1, D_MODEL), const2),
        pl.BlockSpec((N_EXPERTS, D_MODEL), const2),
        pl.BlockSpec((N_EXPERTS, D_MODEL), const2),
        pl.BlockSpec((N_EXPERTS, 1), const2),
    ]
    tile_cnt = lambda s: (tile(s), 0, 0)
    out_specs = [
        pl.BlockSpec((tm, D_MODEL), tile_row),
        pl.BlockSpec((pairs_per_tile * ROW_TILES, LANES),
                     lambda s: (xs_blk0 + jnp.maximum(s - 1, 0), 0)),
        pl.BlockSpec((TOP_K, tm), tile_lane),
        pl.BlockSpec((TOP_K, tm), tile_lane),
        pl.BlockSpec((1, N_EXPERTS, LANES), tile_cnt),
        pl.BlockSpec((1, N_EXPERTS, LANES), tile_cnt),
        pl.BlockSpec((1, ns, HIST, C_CONV), per_seq),
        pl.BlockSpec((tm, C_GMLP), tile_row),
    ]
    rows = g * r
    out_shape = [
        jax.ShapeDtypeStruct((rows, D_MODEL), F32),
        jax.ShapeDtypeStruct((pairs_total * ROW_TILES, LANES), U32),
        jax.ShapeDtypeStruct((TOP_K, rows), I32),
        jax.ShapeDtypeStruct((TOP_K, rows), F32),
        jax.ShapeDtypeStruct((ntot, N_EXPERTS, LANES), I32),
        jax.ShapeDtypeStruct((ntot, N_EXPERTS, LANES), I32),
        jax.ShapeDtypeStruct((g, ns, HIST, C_CONV), F32),
        jax.ShapeDtypeStruct((rows, C_GMLP), F32),
    ]
    args = [x, hist, mem_k, mem_v, *wts]
    aliases = {}
    if xs_prev is not None:
        in_specs.append(pl.BlockSpec(memory_space=pl.ANY))
        aliases = {len(args): 1}
        args.append(xs_prev)
    kern = functools.partial(_mixer_kernel, ns=ns, sl=sl, carry=carry, sp_chunk=sp_chunk,
                             tiles_per_seq=nt, n_sorted=2 * pairs_per_tile)
    return pl.pallas_call(
        kern,
        grid=(ntot + 1,),
        in_specs=in_specs,
        out_specs=out_specs,
        out_shape=out_shape,
        scratch_shapes=[pltpu.VMEM((ns, HIST + sl, C_CONV), F32),
                        pltpu.VMEM((ns, SUBLANES - 1, HIST + sl, C_CONV), F32),
                        pltpu.VMEM((tm, D_MODEL), BF16),
                        pltpu.VMEM((TOP_K, tm), I32)],
        input_output_aliases=aliases,
        compiler_params=pltpu.CompilerParams(
            dimension_semantics=("arbitrary",),
            vmem_limit_bytes=VMEM_LIMIT_MIXER),
        name="mixer_carry" if carry else "mixer_cache",
    )(*args)


def _strip_pieces(n, max_rows, fn):
    done = 0
    p = max_rows
    while p >= 1:
        has = (n & p) != 0
        pl.when(has)(functools.partial(fn, done, p))
        done = done + jnp.where(has, p, 0)
        p //= 2


def _pow2_at_most(n):
    return 1 << (n.bit_length() - 1)


def _rows(ref, first_row, rows):
    return ref.at[pl.ds(pl.multiple_of(first_row * ROW_TILES, ROW_TILES), rows * ROW_TILES)]


def _moe_kernel(b0_ref, nbk_ref, bv_ref, nb_ref, slo_ref, shi_ref, ssrc_ref, sdst_ref, slen_ref,
                xs_hbm, wgu_ref, bgu_ref, wdn_ref, bdn_ref, ys_hbm,
                xbuf, ybuf, wgu16, wdn16, sems, ysems):
    e = pl.program_id(0)
    n_used = nb_ref[0]
    half = MOE_PAIRS // 2

    def gather_start(b, slot):
        pair0 = b * MOE_PAIRS

        def strip(s, carry):
            lo = jnp.maximum(sdst_ref[s], pair0)
            hi = jnp.minimum(sdst_ref[s] + slen_ref[s], pair0 + MOE_PAIRS)
            src0 = ssrc_ref[s] + (lo - sdst_ref[s])
            dst0 = lo - pair0

            def piece(first, rows):
                pltpu.make_async_copy(_rows(xs_hbm, src0 + first, rows),
                                      _rows(xbuf.at[slot], dst0 + first, rows), sems.at[slot]).start()
            _strip_pieces(hi - lo, MOE_PAIRS, piece)
            return carry
        lax.fori_loop(slo_ref[b], shi_ref[b], strip, 0)

    def gather_wait(b, slot):
        def piece(first, rows):
            del first
            pltpu.make_async_copy(_rows(xs_hbm, 0, rows), _rows(xbuf.at[slot], 0, rows), sems.at[slot]).wait()
        _strip_pieces(bv_ref[b], MOE_PAIRS, piece)

    def out_copy(b, slot):
        return pltpu.make_async_copy(ybuf.at[slot], _rows(ys_hbm, b * MOE_PAIRS, MOE_PAIRS), ysems.at[slot])

    def expert_mlp(xslot, slot, h):
        xb = pltpu.bitcast(_tiles_to_rows(xbuf.at[xslot], half, h * half), BF16)
        gu = jnp.dot(xb, wgu16[...], preferred_element_type=F32) + bgu_ref[0]
        gate = jnp.minimum(gu[:, :D_FF], SWIGLU_LIMIT)
        up = jnp.clip(gu[:, D_FF:], -SWIGLU_LIMIT, SWIGLU_LIMIT)
        act = (up + 1.0) * (gate * _sigmoid(SWIGLU_ALPHA * gate))
        yb = jnp.dot(act.astype(BF16), wdn16[...], preferred_element_type=F32) + bdn_ref[0]
        _rows_to_tiles(ybuf.at[slot], pltpu.bitcast(yb.astype(BF16), U32), half, h * half)

    @pl.when(e == 0)
    def _():
        xbuf[...] = jnp.zeros_like(xbuf)
        for a in range(GATHER_AHEAD):
            @pl.when(a < n_used)
            def _():
                gather_start(a, a)

    @pl.when(nbk_ref[e] > 0)
    def _():
        def cast_rows(c, carry):
            r = pl.multiple_of(c * LANES, LANES)
            wgu16[pl.ds(r, LANES), :] = wgu_ref[pl.ds(r, LANES), :].astype(BF16)
            wdn16[pl.ds(r, LANES), :] = wdn_ref[pl.ds(r, LANES), :].astype(BF16)
            return carry
        lax.fori_loop(0, D_MODEL // LANES, cast_rows, 0)

        def block(j, carry):
            b = b0_ref[e] + j
            slot = b % 2
            xslot = b % (GATHER_AHEAD + 1)

            @pl.when(b + GATHER_AHEAD < n_used)
            def _():
                gather_start(b + GATHER_AHEAD, (b + GATHER_AHEAD) % (GATHER_AHEAD + 1))

            gather_wait(b, xslot)

            @pl.when(b >= 2)
            def _():
                out_copy(b - 2, slot).wait()

            valid = bv_ref[b]
            expert_mlp(xslot, slot, 0)

            @pl.when(valid > half)
            def _():
                expert_mlp(xslot, slot, 1)

            @pl.when(valid <= half)
            def _():
                ybuf[slot, pl.ds(half * ROW_TILES, half * ROW_TILES), :] = jnp.zeros(
                    (half * ROW_TILES, LANES), U32)

            out_copy(b, slot).start()
            return carry
        lax.fori_loop(0, nbk_ref[e], block, 0)

    @pl.when(e == pl.num_programs(0) - 1)
    def _():
        for back in (1, 2):
            @pl.when(n_used >= back)
            def _():
                out_copy(n_used - back, (n_used - back) % 2).wait()


def _moe(tables, xs, slots, w_gu, b_gu, w_dn, b_dn):
    wsel = lambda e, *_: (e, 0, 0)
    grid_spec = pltpu.PrefetchScalarGridSpec(
        num_scalar_prefetch=len(tables),
        grid=(N_EXPERTS,),
        in_specs=[pl.BlockSpec(memory_space=pl.ANY),
                  pl.BlockSpec((None, D_MODEL, 2 * D_FF), wsel),
                  pl.BlockSpec((None, 1, 2 * D_FF), wsel),
                  pl.BlockSpec((None, D_FF, D_MODEL), wsel),
                  pl.BlockSpec((None, 1, D_MODEL), wsel)],
        out_specs=pl.BlockSpec(memory_space=pl.ANY),
        scratch_shapes=[pltpu.VMEM((GATHER_AHEAD + 1, MOE_PAIRS * ROW_TILES, LANES), U32),
                        pltpu.VMEM((2, MOE_PAIRS * ROW_TILES, LANES), U32),
                        pltpu.VMEM((D_MODEL, 2 * D_FF), BF16),
                        pltpu.VMEM((D_FF, D_MODEL), BF16),
                        pltpu.SemaphoreType.DMA((GATHER_AHEAD + 1,)),
                        pltpu.SemaphoreType.DMA((2,))],
    )
    return pl.pallas_call(
        _moe_kernel,
        grid_spec=grid_spec,
        out_shape=jax.ShapeDtypeStruct((slots * ROW_TILES, LANES), U32),
        compiler_params=pltpu.CompilerParams(
            dimension_semantics=("arbitrary",),
            vmem_limit_bytes=VMEM_LIMIT_MOE),
        name="moe",
    )(*tables, xs, w_gu, b_gu, w_dn, b_dn)


def _combine_kernel(csrc_ref, clen_ref, coff_ref, ctot_ref, x1_ref, pos_ref, gate_ref, gfin_ref, ys_hbm,
                    out_ref, ybuf, sems, *, tm, pairs_per_tile):
    i = pl.program_id(0)
    n_tiles = pl.num_programs(0)
    n_sorted = 2 * pairs_per_tile

    def gather_start(t, slot):
        def strip(e, carry):
            s = t * N_EXPERTS + e
            src0 = csrc_ref[s]
            dst0 = coff_ref[s]

            def piece(first, rows):
                pltpu.make_async_copy(_rows(ys_hbm, src0 + first, rows),
                                      _rows(ybuf.at[slot], dst0 + first, rows), sems.at[slot]).start()
            _strip_pieces(clen_ref[s], tm // 2, piece)
            return carry
        lax.fori_loop(0, N_EXPERTS, strip, 0)

    @pl.when(i == 0)
    def _():
        ybuf[...] = jnp.zeros_like(ybuf)
        gather_start(0, 0)

    @pl.when(i + 1 < n_tiles)
    def _():
        gather_start(i + 1, (i + 1) % 2)

    slot = i % 2

    def wait_piece(first, rows):
        del first
        pltpu.make_async_copy(_rows(ys_hbm, 0, rows), _rows(ybuf.at[slot], 0, rows), sems.at[slot]).wait()
    _strip_pieces(ctot_ref[i], _pow2_at_most(pairs_per_tile), wait_piece)

    y_sorted = pltpu.bitcast(_tiles_to_rows(ybuf.at[slot], pairs_per_tile), BF16)

    rr = lax.broadcasted_iota(I32, (tm, tm), 0)
    cc = lax.broadcasted_iota(I32, (tm, tm), 1)
    eye = rr == cc
    jl = lax.broadcasted_iota(I32, (tm, n_sorted), 1).astype(F32)
    unsort = jnp.zeros((tm, n_sorted), F32)
    for k in range(TOP_K):
        p_col = jnp.sum(jnp.where(eye, pos_ref[k:k + 1, :].astype(F32), 0.0), axis=1, keepdims=True)
        g_col = jnp.sum(jnp.where(eye, gate_ref[k:k + 1, :], 0.0), axis=1, keepdims=True)
        unsort = jnp.where(jl == p_col, g_col, unsort)
    acc = x1_ref[...] + jnp.dot(unsort.astype(BF16), y_sorted, preferred_element_type=F32)
    out_ref[...] = _rmsnorm(acc, gfin_ref[...])


def _combine(tables, x1, pos, gates, g_final, ys, tm, pairs_per_tile):
    t = x1.shape[0]
    nt = t // tm
    grid_spec = pltpu.PrefetchScalarGridSpec(
        num_scalar_prefetch=len(tables),
        grid=(nt,),
        in_specs=[pl.BlockSpec((tm, D_MODEL), lambda i, *_: (i, 0)),
                  pl.BlockSpec((TOP_K, tm), lambda i, *_: (0, i)),
                  pl.BlockSpec((TOP_K, tm), lambda i, *_: (0, i)),
                  pl.BlockSpec((1, D_MODEL), lambda i, *_: (0, 0)),
                  pl.BlockSpec(memory_space=pl.ANY)],
        out_specs=pl.BlockSpec((tm, D_MODEL), lambda i, *_: (i, 0)),
        scratch_shapes=[pltpu.VMEM((2, pairs_per_tile * ROW_TILES, LANES), U32),
                        pltpu.SemaphoreType.DMA((2,))],
    )
    return pl.pallas_call(
        functools.partial(_combine_kernel, tm=tm, pairs_per_tile=pairs_per_tile),
        grid_spec=grid_spec,
        out_shape=jax.ShapeDtypeStruct((t, D_MODEL), F32),
        compiler_params=pltpu.CompilerParams(
            dimension_semantics=("arbitrary",),
            vmem_limit_bytes=VMEM_LIMIT_MIXER),
        name="combine",
    )(*tables, x1, pos, gates, g_final, ys)


def _split_bf16(w):
    hi = w.astype(BF16)
    lo = (w - hi.astype(F32)).astype(BF16)
    return hi, lo


def kernel(x_prompt, x_sample, cache_conv, cache_mem_k, cache_mem_v, mem_prompt, g_mix, w_in, w_dw, b_dw, ln_conv_g, ln_conv_b, ln_v_g, ln_v_b, w_spatial, b_spatial, g_mem, w_mem_k, w_mem_v, w_out, g_ffn, w_router, b_router, w_gate_up, b_gate_up, w_down, b_down, g_final):
    depth = g_mix.shape[0]
    assert depth == 1
    l = 0
    bp, seq, _ = x_prompt.shape
    bs, dseq, _ = x_sample.shape
    assert seq % PROMPT_TILE == 0 and bs % SAMPLE_SEQS_PER_TILE == 0
    assert GMLP_CHUNK % dseq == 0 and (SAMPLE_SEQS_PER_TILE * dseq) % GMLP_CHUNK == 0

    row = lambda a: a.reshape(1, -1)
    wr_hi, wr_lo = _split_bf16(w_router[l].T)
    w_dw_pad = jnp.pad(w_dw[l], ((0, HIST - CONV_W), (0, 0)))
    bias_rows = lambda b: jnp.repeat(b.T, GMLP_HD, axis=1)
    common = dict(
        gmix=row(g_mix[l]), win=w_in[l].astype(BF16), wdw=w_dw_pad, bdw=row(b_dw[l]),
        lcg=row(ln_conv_g[l]), lcb=row(ln_conv_b[l]), lvg=row(ln_v_g[l]), lvb=row(ln_v_b[l]),
        wout=w_out[l].astype(BF16), gffn=row(g_ffn[l]), wrh=wr_hi, wrl=wr_lo,
        br=b_router[l].reshape(N_EXPERTS, 1))

    def weights(wsp, bsp):
        c = common
        return (c["gmix"], c["win"], c["wdw"], c["bdw"], c["lcg"], c["lcb"], c["lvg"], c["lvb"],
                wsp, bsp, c["wout"], c["gffn"], c["wrh"], c["wrl"], c["br"])

    reps = GMLP_CHUNK // dseq
    wts_p = weights(w_spatial[l], bias_rows(b_spatial[l]))
    wts_s = weights(jnp.tile(w_spatial[l][:, :dseq, :dseq], (1, reps, reps)),
                    bias_rows(jnp.tile(b_spatial[l][:, :dseq], (1, reps))))

    w_kv = jnp.concatenate([w_mem_k[l], w_mem_v[l]], axis=1).astype(BF16)
    kv_p = _memkv(mem_prompt, row(g_mem[l]), w_kv)
    mk_p = kv_p[:, :, :C_XA]
    mv_p = kv_p[:, :, C_XA:]
    zero_hist = jnp.zeros((bp, 1, HIST, C_CONV), F32)
    tp, ts = bp * seq, bs * dseq
    tm_s = SAMPLE_SEQS_PER_TILE * dseq
    ntp, nts = tp // PROMPT_TILE, ts // tm_s
    ppt_p = (TOP_K * PROMPT_TILE + N_EXPERTS + 1) // 2
    ppt_s = (TOP_K * tm_s + N_EXPERTS + 1) // 2
    while (ntp * ppt_p) % ppt_s:
        ppt_s += 1
    pairs_total = ntp * ppt_p + nts * ppt_s
    (x1_p, xs, pos_p, gate_p, cnt_p, off_p, hist_p, _) = _mixer(
        x_prompt, zero_hist, mk_p[:, None], mv_p[:, None], wts_p,
        ns=1, sl=PROMPT_TILE, carry=True, sp_chunk=GMLP_CHUNK,
        pairs_per_tile=ppt_p, pairs_total=pairs_total, pair0=0, xs_prev=None)

    gs = bs // SAMPLE_SEQS_PER_TILE
    hist_s_in = jnp.pad(cache_conv[l], ((0, 0), (HIST_OFF, 0), (0, 0))).reshape(
        gs, SAMPLE_SEQS_PER_TILE, HIST, C_CONV)
    mk_s = cache_mem_k[l].reshape(gs, SAMPLE_SEQS_PER_TILE, N_MEM, C_XA)
    mv_s = cache_mem_v[l].reshape(gs, SAMPLE_SEQS_PER_TILE, N_MEM, C_XA)
    (x1_s, xs, pos_s, gate_s, cnt_s, off_s, hist_s, v_s) = _mixer(
        x_sample.reshape(gs, tm_s, D_MODEL), hist_s_in, mk_s, mv_s, wts_s,
        ns=SAMPLE_SEQS_PER_TILE, sl=dseq, carry=False, sp_chunk=dseq,
        pairs_per_tile=ppt_s, pairs_total=pairs_total, pair0=ntp * ppt_p, xs_prev=xs)

    max_pairs = ((tp + ts) * TOP_K + (ntp + nts) * N_EXPERTS) // 2
    n_blocks = -(-max_pairs // MOE_PAIRS) + N_EXPERTS
    cnt = jnp.concatenate([cnt_p[:, :, 0], cnt_s[:, :, 0]], axis=0)
    off = jnp.concatenate([off_p[:, :, 0], off_s[:, :, 0]], axis=0)
    tile_pair0 = np.concatenate([np.arange(ntp) * ppt_p,
                                 ntp * ppt_p + np.arange(nts) * ppt_s]).astype(np.int32)
    counts = jnp.sum(cnt, axis=0)
    tile_base = jnp.cumsum(cnt, axis=0) - cnt
    padded = (counts + MOE_PAIRS - 1) // MOE_PAIRS * MOE_PAIRS
    pad_end = jnp.cumsum(padded)
    pad_start = pad_end - padded
    strip_dst = (pad_start[None, :] + tile_base).astype(I32)
    strip_src = (tile_pair0[:, None] + off).astype(I32)
    n_used = (pad_end[-1] // MOE_PAIRS).astype(I32)
    blk_pair0 = jnp.minimum(jnp.arange(n_blocks, dtype=I32), n_used - 1) * MOE_PAIRS
    block_expert = jnp.minimum(
        jnp.sum((pad_end[None, :] <= blk_pair0[:, None]).astype(I32), axis=1), N_EXPERTS - 1)
    of_block = block_expert[:, None] == jnp.arange(N_EXPERTS, dtype=I32)[None, :]
    last_pair = jnp.sum(jnp.where(of_block, (pad_start + counts)[None, :], 0), axis=1)
    block_valid = jnp.clip(last_pair - blk_pair0, 0, MOE_PAIRS).astype(I32)
    sdst = strip_dst.T.reshape(-1)
    ssrc = strip_src.T.reshape(-1)
    slen = cnt.T.reshape(-1).astype(I32)
    s_lo = jnp.sum(((sdst + slen)[None, :] <= blk_pair0[:, None]).astype(I32), axis=1)
    s_hi = jnp.sum((sdst[None, :] < (blk_pair0 + MOE_PAIRS)[:, None]).astype(I32), axis=1)
    moe_tables = ((pad_start // MOE_PAIRS).astype(I32), (padded // MOE_PAIRS).astype(I32),
                  block_valid, n_used.reshape(1),
                  s_lo.astype(I32), s_hi.astype(I32), ssrc, sdst, slen)

    slots = n_blocks * MOE_PAIRS
    ys = _moe(moe_tables, xs, slots,
              w_gate_up[l], b_gate_up[l][:, None, :], w_down[l], b_down[l][:, None, :])
    gfin = row(g_final)

    def combine_tables(t0, t1):
        return (strip_dst[t0:t1].reshape(-1), cnt[t0:t1].reshape(-1).astype(I32),
                off[t0:t1].reshape(-1).astype(I32), jnp.sum(cnt[t0:t1], axis=1).astype(I32))

    y_p = _combine(combine_tables(0, ntp), x1_p, pos_p, gate_p, gfin, ys, PROMPT_TILE, ppt_p)
    y_s = _combine(combine_tables(ntp, ntp + nts), x1_s, pos_s, gate_s, gfin, ys, tm_s, ppt_s)

    return (y_p.reshape(bp, seq, D_MODEL),
            y_s.reshape(bs, dseq, D_MODEL),
            hist_p[:, 0, HIST_OFF:, :][None],
            mk_p.reshape(bp, N_MEM, XA_HEADS, XA_HD)[None],
            mv_p.reshape(bp, N_MEM, XA_HEADS, XA_HD)[None],
            hist_s.reshape(bs, HIST, C_CONV)[:, HIST_OFF:, :][None],
            v_s.reshape(bs, dseq, C_GMLP)[None])
```

```python
import functools

import numpy as np
import jax
import jax.numpy as jnp
from jax import lax
from jax.experimental import pallas as pl
from jax.experimental.pallas import tpu as pltpu

F32 = jnp.float32
BF16 = jnp.bfloat16
I32 = jnp.int32
U32 = jnp.uint32

D_MODEL = 1024
C_CONV = 384
CONV_W = 31
C_GMLP = 384
GMLP_HEADS = 4
GMLP_HD = 96
GMLP_CHUNK = 128
XA_HEADS = 4
XA_HD = 64
C_XA = 256
N_MEM = 256
N_EXPERTS = 32
TOP_K = 4
D_FF = 1024
SWIGLU_LIMIT = 7.0
SWIGLU_ALPHA = 1.702
EPS = 1e-5
IN_COLS = 2 * C_CONV + 2 * C_GMLP + C_XA

SUBLANES = 8
LANES = 128
ROW_TILES = D_MODEL // LANES
HIST = 32
HIST_OFF = HIST - (CONV_W - 1)

PROMPT_TILE = 512
SAMPLE_SEQS_PER_TILE = 8
MOE_BLOCK = 512
MOE_PAIRS = MOE_BLOCK // 2
GATHER_AHEAD = 2
RARE_PIECE = 64
VMEM_LIMIT_MIXER = 48 * 1024 * 1024
VMEM_LIMIT_MOE = 52 * 1024 * 1024


def _rmsnorm(x, g):
    return x * lax.rsqrt(jnp.mean(x * x, axis=-1, keepdims=True) + EPS) * g


def _layernorm(x, g, b):
    mu = jnp.mean(x, axis=-1, keepdims=True)
    xc = x - mu
    var = jnp.mean(xc * xc, axis=-1, keepdims=True)
    return xc * lax.rsqrt(var + EPS) * g + b


def _gelu(x):
    return 0.5 * x * (1.0 + lax.erf(x * np.float32(1.0 / np.sqrt(2.0))))


def _sigmoid(x):
    return 1.0 / (1.0 + jnp.exp(-x))


def _rows_to_tiles(dst_ref, val, rows, row0=0):
    for j in range(ROW_TILES):
        dst_ref[pl.ds(row0 * ROW_TILES + j, rows, stride=ROW_TILES), :] = val[:, j * LANES:(j + 1) * LANES]


def _tiles_to_rows(src_ref, rows, row0=0):
    return jnp.concatenate(
        [src_ref[pl.ds(row0 * ROW_TILES + j, rows, stride=ROW_TILES), :] for j in range(ROW_TILES)],
        axis=-1)


def _memkv_kernel(mem_ref, g_ref, w_ref, o_ref):
    mn = _rmsnorm(mem_ref[0], g_ref[...])
    o_ref[0] = jnp.dot(mn.astype(BF16), w_ref[...], preferred_element_type=F32)


def _memkv(mem, g_mem, w_kv):
    b = mem.shape[0]
    return pl.pallas_call(
        _memkv_kernel,
        grid=(b,),
        in_specs=[pl.BlockSpec((1, N_MEM, D_MODEL), lambda i: (i, 0, 0)),
                  pl.BlockSpec((1, D_MODEL), lambda i: (0, 0)),
                  pl.BlockSpec((D_MODEL, 2 * C_XA), lambda i: (0, 0))],
        out_specs=pl.BlockSpec((1, N_MEM, 2 * C_XA), lambda i: (i, 0, 0)),
        out_shape=jax.ShapeDtypeStruct((b, N_MEM, 2 * C_XA), F32),
        name="memkv",
    )(mem, g_mem, w_kv)


def _mixer_kernel(x_ref, hist_ref, mk_ref, mv_ref, gmix_ref, win_ref, wdw_ref, bdw_ref,
                  lcg_ref, lcb_ref, lvg_ref, lvb_ref, wsp_ref, bsp_ref, wout_ref, gffn_ref,
                  wrh_ref, wrl_ref, br_ref, *rest, ns, sl, carry, sp_chunk, tiles_per_seq, n_sorted):
    (x1_ref, xst_ref, pos_ref, gate_ref, cnt_ref, off_ref, histout_ref, v_ref,
     ext_ref, shift_ref, h2_keep, pos_keep) = rest[-12:]
    tm = ns * sl
    step = pl.program_id(0)
    n_tiles = pl.num_programs(0) - 1

    def sort_previous_tile():
        jj = lax.broadcasted_iota(I32, (n_sorted, tm), 0)
        perm = jnp.zeros((n_sorted, tm), F32)
        for k in range(TOP_K):
            perm = jnp.where(jj == pos_keep[k:k + 1, :], 1.0, perm)
        sorted_rows = jnp.dot(perm.astype(BF16), h2_keep[...], preferred_element_type=F32)
        pairs = pltpu.bitcast(sorted_rows.astype(BF16), U32)
        _rows_to_tiles(xst_ref, pairs, n_sorted // 2)

    @pl.when(step == 0)
    def _():
        h2_keep[...] = jnp.zeros_like(h2_keep)
        pos_keep[...] = jnp.full(pos_keep.shape, -1, I32)

    @pl.when(step < n_tiles)
    def _():
        sort_previous_tile()
        _mixer_tile(x_ref, hist_ref, mk_ref, mv_ref, gmix_ref, win_ref, wdw_ref, bdw_ref,
                    lcg_ref, lcb_ref, lvg_ref, lvb_ref, wsp_ref, bsp_ref, wout_ref, gffn_ref,
                    wrh_ref, wrl_ref, br_ref, x1_ref, pos_ref, gate_ref, cnt_ref, off_ref,
                    histout_ref, v_ref, ext_ref, shift_ref, h2_keep, pos_keep,
                    first_of_seq=(step % tiles_per_seq) == 0,
                    ns=ns, sl=sl, carry=carry, sp_chunk=sp_chunk)

    @pl.when(step == n_tiles)
    def _():
        sort_previous_tile()


def _mixer_tile(x_ref, hist_ref, mk_ref, mv_ref, gmix_ref, win_ref, wdw_ref, bdw_ref,
                lcg_ref, lcb_ref, lvg_ref, lvb_ref, wsp_ref, bsp_ref, wout_ref, gffn_ref,
                wrh_ref, wrl_ref, br_ref, x1_ref, pos_ref, gate_ref, cnt_ref, off_ref,
                histout_ref, v_ref, ext_ref, shift_ref, h2_keep, pos_keep,
                *, first_of_seq, ns, sl, carry, sp_chunk):
    tm = ns * sl
    x = x_ref[0]
    h = _rmsnorm(x, gmix_ref[...])
    z = jnp.dot(h.astype(BF16), win_ref[...], preferred_element_type=F32)
    z_a = z[:, 0:C_CONV]
    z_g = z[:, C_CONV:2 * C_CONV]
    z_u = z[:, 2 * C_CONV:2 * C_CONV + C_GMLP]
    z_v = z[:, 2 * C_CONV + C_GMLP:2 * C_CONV + 2 * C_GMLP]
    z_q = z[:, 2 * C_CONV + 2 * C_GMLP:IN_COLS]

    glu = z_a * _sigmoid(z_g)
    if carry:
        ext_ref[:, 0:HIST, :] = jnp.where(first_of_seq, hist_ref[0], ext_ref[:, 0:HIST, :])
    else:
        ext_ref[:, 0:HIST, :] = hist_ref[0]
    conv_parts = []
    for s in range(ns):
        ext_s = ext_ref.at[s]
        ext_s[HIST:HIST + sl, :] = glu[s * sl:(s + 1) * sl]
        n_shift = HIST + sl - SUBLANES
        for r in range(1, SUBLANES):
            shift_ref[s, r - 1, 0:n_shift, :] = ext_s[pl.ds(r, n_shift), :]
        rc = min(sl, 64)
        for r0 in range(0, sl, rc):
            acc = jnp.broadcast_to(bdw_ref[...], (rc, C_CONV))
            for j in range(CONV_W):
                a, r = divmod(j + HIST_OFF, SUBLANES)
                src = ext_s if r == 0 else shift_ref.at[s, r - 1]
                acc = acc + wdw_ref[j:j + 1, :] * src[pl.ds(r0 + a * SUBLANES, rc), :]
            conv_parts.append(acc)
        new_hist = ext_s[sl:sl + HIST, :]
        histout_ref[0, s] = new_hist
        if carry:
            ext_s[0:HIST, :] = new_hist
    y = jnp.concatenate(conv_parts, axis=0) if len(conv_parts) > 1 else conv_parts[0]
    y = _layernorm(y, lcg_ref[...], lcb_ref[...])
    c_out = y * _sigmoid(y)

    u = _gelu(z_u)
    v = _layernorm(_gelu(z_v), lvg_ref[...], lvb_ref[...])
    v_ref[...] = v
    vb = v.astype(BF16)
    rr = lax.broadcasted_iota(I32, (GMLP_CHUNK, GMLP_CHUNK), 0)
    cc = lax.broadcasted_iota(I32, (GMLP_CHUNK, GMLP_CHUNK), 1)
    sp_mask = (cc <= rr) & ((rr // sp_chunk) == (cc // sp_chunk))
    col = lax.broadcasted_iota(I32, (GMLP_CHUNK, C_GMLP), 1)
    w_heads = [jnp.where(sp_mask, wsp_ref[hh], 0.0).astype(BF16) for hh in range(GMLP_HEADS)]
    g_parts = []
    for c in range(tm // GMLP_CHUNK):
        vc = vb[c * GMLP_CHUNK:(c + 1) * GMLP_CHUNK]
        sg = bsp_ref[...]
        for hh in range(GMLP_HEADS):
            head_cols = (col >= hh * GMLP_HD) & (col < (hh + 1) * GMLP_HD)
            vh = jnp.where(head_cols, vc, jnp.zeros_like(vc))
            sg = sg + jnp.dot(w_heads[hh], vh, preferred_element_type=F32)
        g_parts.append(u[c * GMLP_CHUNK:(c + 1) * GMLP_CHUNK] * sg)
    g_out = jnp.concatenate(g_parts, axis=0) if len(g_parts) > 1 else g_parts[0]

    qs = z_q * np.float32(XA_HD ** -0.5)
    qcol = lax.broadcasted_iota(I32, (sl, C_XA), 1)
    a_parts = []
    for s in range(ns):
        q_s = qs[s * sl:(s + 1) * sl]
        kb = mk_ref[0, s].astype(BF16)
        vvb = mv_ref[0, s].astype(BF16)
        hmasks = [(qcol >= hh * XA_HD) & (qcol < (hh + 1) * XA_HD) for hh in range(XA_HEADS)]
        stack = XA_HEADS if sl * XA_HEADS <= N_MEM else 1
        a_s = jnp.zeros((sl, C_XA), F32)
        for h0 in range(0, XA_HEADS, stack):
            heads = range(h0, h0 + stack)
            qh = jnp.concatenate([jnp.where(hmasks[hh], q_s, 0.0) for hh in heads], axis=0).astype(BF16)
            sc = lax.dot_general(qh, kb, (((1,), (1,)), ((), ())), preferred_element_type=F32)
            p = jnp.exp(sc - jnp.max(sc, axis=-1, keepdims=True))
            den = jnp.sum(p, axis=-1, keepdims=True)
            oh = jnp.dot(p.astype(BF16), vvb, preferred_element_type=F32) / den
            for n, hh in enumerate(heads):
                a_s = a_s + jnp.where(hmasks[hh], oh[n * sl:(n + 1) * sl], 0.0)
        a_parts.append(a_s)
    a_out = jnp.concatenate(a_parts, axis=0) if len(a_parts) > 1 else a_parts[0]

    mix = jnp.concatenate([c_out, g_out, a_out], axis=-1).astype(BF16)
    x1 = x + jnp.dot(mix, wout_ref[...], preferred_element_type=F32)
    x1_ref[...] = x1

    h2 = _rmsnorm(x1, gffn_ref[...])
    h2_hi = h2.astype(BF16)
    h2_lo = (h2 - h2_hi.astype(F32)).astype(BF16)
    nt_dims = (((1,), (1,)), ((), ()))
    lg = (lax.dot_general(wrh_ref[...], h2_hi, nt_dims, preferred_element_type=F32)
          + lax.dot_general(wrl_ref[...], h2_hi, nt_dims, preferred_element_type=F32)
          + lax.dot_general(wrh_ref[...], h2_lo, nt_dims, preferred_element_type=F32)
          + br_ref[...])
    eio = lax.broadcasted_iota(I32, (N_EXPERTS, tm), 0)
    work = lg
    vals, idxs = [], []
    for _ in range(TOP_K):
        m = jnp.max(work, axis=0, keepdims=True)
        idx = jnp.min(jnp.where(work == m, eio, N_EXPERTS), axis=0, keepdims=True)
        vals.append(m)
        idxs.append(idx)
        work = jnp.where(eio == idx, -jnp.inf, work)
    exps = [jnp.exp(vk - vals[0]) for vk in vals]
    den = exps[0] + exps[1] + exps[2] + exps[3]
    gate_ref[...] = jnp.concatenate([ek / den for ek in exps], axis=0)

    sel = jnp.zeros((N_EXPERTS, tm), F32)
    for idx in idxs:
        sel = sel + jnp.where(eio == idx, 1.0, 0.0)
    selb = sel.astype(BF16)
    tr = lax.broadcasted_iota(I32, (tm, tm), 0)
    tc = lax.broadcasted_iota(I32, (tm, tm), 1)
    before = jnp.where(tr < tc, 1.0, 0.0).astype(BF16)
    ranks = jnp.dot(selb, before, preferred_element_type=F32)
    er = lax.broadcasted_iota(I32, (N_EXPERTS, N_EXPERTS), 0)
    ec = lax.broadcasted_iota(I32, (N_EXPERTS, N_EXPERTS), 1)
    lower = jnp.where(ec < er, 1.0, 0.0).astype(BF16)
    cnt = jnp.sum(sel, axis=1, keepdims=True)
    half_len = jnp.floor((cnt + 1.0) * 0.5)
    off_pairs = jnp.dot(lower, jnp.broadcast_to(half_len, (N_EXPERTS, LANES)).astype(BF16),
                        preferred_element_type=F32)[:, 0:1]
    cnt_ref[0] = jnp.broadcast_to(half_len, (N_EXPERTS, LANES)).astype(I32)
    off_ref[0] = jnp.broadcast_to(off_pairs, (N_EXPERTS, LANES)).astype(I32)
    slot_of = 2.0 * off_pairs + ranks
    pos = [jnp.sum(jnp.where(eio == idx, slot_of, 0.0), axis=0, keepdims=True).astype(I32) for idx in idxs]
    pos_all = jnp.concatenate(pos, axis=0)
    pos_ref[...] = pos_all
    pos_keep[...] = pos_all
    h2_keep[...] = h2_hi


def _mixer(x, hist, mem_k, mem_v, wts, *, ns, sl, carry, sp_chunk, pairs_per_tile, pairs_total, pair0,
           xs_prev):
    g, r, _ = x.shape
    tm = ns * sl
    nt = r // tm
    ntot = g * nt
    assert 2 * pairs_per_tile >= TOP_K * tm + N_EXPERTS and pair0 % pairs_per_tile == 0
    xs_blk0 = pair0 // pairs_per_tile
    tile = lambda s: jnp.minimum(s, ntot - 1)
    const2 = lambda s: (0, 0)
    const3 = lambda s: (0, 0, 0)
    tile_row = lambda s: (tile(s), 0)
    tile_lane = lambda s: (0, tile(s))
    per_seq = lambda s: (tile(s) // nt, 0, 0, 0)
    in_specs = [
        pl.BlockSpec((1, tm, D_MODEL), lambda s: (tile(s) // nt, tile(s) % nt, 0)),
        pl.BlockSpec((1, ns, HIST, C_CONV), per_seq),
        pl.BlockSpec((1, ns, N_MEM, C_XA), per_seq),
        pl.BlockSpec((1, ns, N_MEM, C_XA), per_seq),
        pl.BlockSpec((1, D_MODEL), const2),
        pl.BlockSpec((D_MODEL, IN_COLS), const2),
        pl.BlockSpec((HIST, C_CONV), const2),
        pl.BlockSpec((1, C_CONV), const2),
        pl.BlockSpec((1, C_CONV), const2),
        pl.BlockSpec((1, C_CONV), const2),
        pl.BlockSpec((1, C_GMLP), const2),
        pl.BlockSpec((1, C_GMLP), const2),
        pl.BlockSpec((GMLP_HEADS, GMLP_CHUNK, GMLP_CHUNK), const3),
        pl.BlockSpec((GMLP_CHUNK, C_GMLP), const2),
        pl.BlockSpec((D_MODEL, D_MODEL), const2),
        pl.BlockSpec((1, D_MODEL), const2),
        pl.BlockSpec((N_EXPERTS, D_MODEL), const2),
        pl.BlockSpec((N_EXPERTS, D_MODEL), const2),
        pl.BlockSpec((N_EXPERTS, 1), const2),
    ]
    tile_cnt = lambda s: (tile(s), 0, 0)
    out_specs = [
        pl.BlockSpec((tm, D_MODEL), tile_row),
        pl.BlockSpec((pairs_per_tile * ROW_TILES, LANES),
                     lambda s: (xs_blk0 + jnp.maximum(s - 1, 0), 0)),
        pl.BlockSpec((TOP_K, tm), tile_lane),
        pl.BlockSpec((TOP_K, tm), tile_lane),
        pl.BlockSpec((1, N_EXPERTS, LANES), tile_cnt),
        pl.BlockSpec((1, N_EXPERTS, LANES), tile_cnt),
        pl.BlockSpec((1, ns, HIST, C_CONV), per_seq),
        pl.BlockSpec((tm, C_GMLP), tile_row),
    ]
    rows = g * r
    out_shape = [
        jax.ShapeDtypeStruct((rows, D_MODEL), F32),
        jax.ShapeDtypeStruct((pairs_total * ROW_TILES, LANES), U32),
        jax.ShapeDtypeStruct((TOP_K, rows), I32),
        jax.ShapeDtypeStruct((TOP_K, rows), F32),
        jax.ShapeDtypeStruct((ntot, N_EXPERTS, LANES), I32),
        jax.ShapeDtypeStruct((ntot, N_EXPERTS, LANES), I32),
        jax.ShapeDtypeStruct((g, ns, HIST, C_CONV), F32),
        jax.ShapeDtypeStruct((rows, C_GMLP), F32),
    ]
    args = [x, hist, mem_k, mem_v, *wts]
    aliases = {}
    if xs_prev is not None:
        in_specs.append(pl.BlockSpec(memory_space=pl.ANY))
        aliases = {len(args): 1}
        args.append(xs_prev)
    kern = functools.partial(_mixer_kernel, ns=ns, sl=sl, carry=carry, sp_chunk=sp_chunk,
                             tiles_per_seq=nt, n_sorted=2 * pairs_per_tile)
    return pl.pallas_call(
        kern,
        grid=(ntot + 1,),
        in_specs=in_specs,
        out_specs=out_specs,
        out_shape=out_shape,
        scratch_shapes=[pltpu.VMEM((ns, HIST + sl, C_CONV), F32),
                        pltpu.VMEM((ns, SUBLANES - 1, HIST + sl, C_CONV), F32),
                        pltpu.VMEM((tm, D_MODEL), BF16),
                        pltpu.VMEM((TOP_K, tm), I32)],
        input_output_aliases=aliases,
        compiler_params=pltpu.CompilerParams(
            dimension_semantics=("arbitrary",),
            vmem_limit_bytes=VMEM_LIMIT_MIXER),
        name="mixer_carry" if carry else "mixer_cache",
    )(*args)


def _strip_pieces(n, max_rows, fn):
    def pieces(sizes):
        for p in sizes:
            first = n & (-2 * p)
            pl.when((n & p) != 0)(functools.partial(fn, first, p))

    sizes = [max_rows >> k for k in range(max_rows.bit_length())]
    big = [p for p in sizes if p >= RARE_PIECE]
    if len(big) > 1:
        pl.when(n >= RARE_PIECE)(functools.partial(pieces, big))
    else:
        pieces(big)
    pieces([p for p in sizes if p < RARE_PIECE])


def _pow2_at_most(n):
    return 1 << (n.bit_length() - 1)


def _rows(ref, first_row, rows):
    return ref.at[pl.ds(pl.multiple_of(first_row * ROW_TILES, ROW_TILES), rows * ROW_TILES)]


def _moe_kernel(b0_ref, nbk_ref, bv_ref, nb_ref, slo_ref, shi_ref, ssrc_ref, sdst_ref, slen_ref,
                xs_hbm, wgu_ref, bgu_ref, wdn_ref, bdn_ref, ys_hbm,
                xbuf, ybuf, wgu16, wdn16, sems, ysems):
    e = pl.program_id(0)
    n_used = nb_ref[0]
    half = MOE_PAIRS // 2

    def gather_start(b, slot):
        pair0 = b * MOE_PAIRS

        def strip(s, carry):
            lo = jnp.maximum(sdst_ref[s], pair0)
            hi = jnp.minimum(sdst_ref[s] + slen_ref[s], pair0 + MOE_PAIRS)
            src0 = ssrc_ref[s] + (lo - sdst_ref[s])
            dst0 = lo - pair0

            def piece(first, rows):
                pltpu.make_async_copy(_rows(xs_hbm, src0 + first, rows),
                                      _rows(xbuf.at[slot], dst0 + first, rows), sems.at[slot]).start()
            _strip_pieces(hi - lo, MOE_PAIRS, piece)
            return carry
        lax.fori_loop(slo_ref[b], shi_ref[b], strip, 0)

    def gather_wait(b, slot):
        def piece(first, rows):
            del first
            pltpu.make_async_copy(_rows(xs_hbm, 0, rows), _rows(xbuf.at[slot], 0, rows), sems.at[slot]).wait()
        _strip_pieces(bv_ref[b], MOE_PAIRS, piece)

    def out_copy(b, slot):
        return pltpu.make_async_copy(ybuf.at[slot], _rows(ys_hbm, b * MOE_PAIRS, MOE_PAIRS), ysems.at[slot])

    def expert_mlp(xslot, slot, h):
        xb = pltpu.bitcast(_tiles_to_rows(xbuf.at[xslot], half, h * half), BF16)
        gu = jnp.dot(xb, wgu16[...], preferred_element_type=F32) + bgu_ref[0]
        gate = jnp.minimum(gu[:, :D_FF], SWIGLU_LIMIT)
        up = jnp.clip(gu[:, D_FF:], -SWIGLU_LIMIT, SWIGLU_LIMIT)
        act = (up + 1.0) * (gate * _sigmoid(SWIGLU_ALPHA * gate))
        yb = jnp.dot(act.astype(BF16), wdn16[...], preferred_element_type=F32) + bdn_ref[0]
        _rows_to_tiles(ybuf.at[slot], pltpu.bitcast(yb.astype(BF16), U32), half, h * half)

    @pl.when(e == 0)
    def _():
        xbuf[...] = jnp.zeros_like(xbuf)
        for a in range(GATHER_AHEAD):
            @pl.when(a < n_used)
            def _():
                gather_start(a, a)

    @pl.when(nbk_ref[e] > 0)
    def _():
        def cast_rows(c, carry):
            r = pl.multiple_of(c * LANES, LANES)
            wgu16[pl.ds(r, LANES), :] = wgu_ref[pl.ds(r, LANES), :].astype(BF16)
            wdn16[pl.ds(r, LANES), :] = wdn_ref[pl.ds(r, LANES), :].astype(BF16)
            return carry
        lax.fori_loop(0, D_MODEL // LANES, cast_rows, 0)

        def block(j, carry):
            b = b0_ref[e] + j
            slot = b % 2
            xslot = b % (GATHER_AHEAD + 1)

            @pl.when(b + GATHER_AHEAD < n_used)
            def _():
                gather_start(b + GATHER_AHEAD, (b + GATHER_AHEAD) % (GATHER_AHEAD + 1))

            gather_wait(b, xslot)

            @pl.when(b >= 2)
            def _():
                out_copy(b - 2, slot).wait()

            valid = bv_ref[b]
            expert_mlp(xslot, slot, 0)

            @pl.when(valid > half)
            def _():
                expert_mlp(xslot, slot, 1)

            @pl.when(valid <= half)
            def _():
                ybuf[slot, pl.ds(half * ROW_TILES, half * ROW_TILES), :] = jnp.zeros(
                    (half * ROW_TILES, LANES), U32)

            out_copy(b, slot).start()
            return carry
        lax.fori_loop(0, nbk_ref[e], block, 0)

    @pl.when(e == pl.num_programs(0) - 1)
    def _():
        for back in (1, 2):
            @pl.when(n_used >= back)
            def _():
                out_copy(n_used - back, (n_used - back) % 2).wait()


def _moe(tables, xs, slots, w_gu, b_gu, w_dn, b_dn):
    wsel = lambda e, *_: (e, 0, 0)
    grid_spec = pltpu.PrefetchScalarGridSpec(
        num_scalar_prefetch=len(tables),
        grid=(N_EXPERTS,),
        in_specs=[pl.BlockSpec(memory_space=pl.ANY),
                  pl.BlockSpec((None, D_MODEL, 2 * D_FF), wsel),
                  pl.BlockSpec((None, 1, 2 * D_FF), wsel),
                  pl.BlockSpec((None, D_FF, D_MODEL), wsel),
                  pl.BlockSpec((None, 1, D_MODEL), wsel)],
        out_specs=pl.BlockSpec(memory_space=pl.ANY),
        scratch_shapes=[pltpu.VMEM((GATHER_AHEAD + 1, MOE_PAIRS * ROW_TILES, LANES), U32),
                        pltpu.VMEM((2, MOE_PAIRS * ROW_TILES, LANES), U32),
                        pltpu.VMEM((D_MODEL, 2 * D_FF), BF16),
                        pltpu.VMEM((D_FF, D_MODEL), BF16),
                        pltpu.SemaphoreType.DMA((GATHER_AHEAD + 1,)),
                        pltpu.SemaphoreType.DMA((2,))],
    )
    return pl.pallas_call(
        _moe_kernel,
        grid_spec=grid_spec,
        out_shape=jax.ShapeDtypeStruct((slots * ROW_TILES, LANES), U32),
        compiler_params=pltpu.CompilerParams(
            dimension_semantics=("arbitrary",),
            vmem_limit_bytes=VMEM_LIMIT_MOE),
        name="moe",
    )(*tables, xs, w_gu, b_gu, w_dn, b_dn)


def _combine_kernel(csrc_ref, clen_ref, coff_ref, ctot_ref, x1_ref, pos_ref, gate_ref, gfin_ref, ys_hbm,
                    out_ref, ybuf, sems, *, tm, pairs_per_tile):
    i = pl.program_id(0)
    n_tiles = pl.num_programs(0)
    n_sorted = 2 * pairs_per_tile

    def gather_start(t, slot):
        def strip(e, carry):
            s = t * N_EXPERTS + e
            src0 = csrc_ref[s]
            dst0 = coff_ref[s]

            def piece(first, rows):
                pltpu.make_async_copy(_rows(ys_hbm, src0 + first, rows),
                                      _rows(ybuf.at[slot], dst0 + first, rows), sems.at[slot]).start()
            _strip_pieces(clen_ref[s], tm // 2, piece)
            return carry
        lax.fori_loop(0, N_EXPERTS, strip, 0)

    @pl.when(i == 0)
    def _():
        ybuf[...] = jnp.zeros_like(ybuf)
        gather_start(0, 0)

    @pl.when(i + 1 < n_tiles)
    def _():
        gather_start(i + 1, (i + 1) % 2)

    slot = i % 2

    def wait_piece(first, rows):
        del first
        pltpu.make_async_copy(_rows(ys_hbm, 0, rows), _rows(ybuf.at[slot], 0, rows), sems.at[slot]).wait()
    _strip_pieces(ctot_ref[i], _pow2_at_most(pairs_per_tile), wait_piece)

    y_sorted = pltpu.bitcast(_tiles_to_rows(ybuf.at[slot], pairs_per_tile), BF16)

    rr = lax.broadcasted_iota(I32, (tm, tm), 0)
    cc = lax.broadcasted_iota(I32, (tm, tm), 1)
    eye = rr == cc
    jl = lax.broadcasted_iota(I32, (tm, n_sorted), 1).astype(F32)
    unsort = jnp.zeros((tm, n_sorted), F32)
    for k in range(TOP_K):
        p_col = jnp.sum(jnp.where(eye, pos_ref[k:k + 1, :].astype(F32), 0.0), axis=1, keepdims=True)
        g_col = jnp.sum(jnp.where(eye, gate_ref[k:k + 1, :], 0.0), axis=1, keepdims=True)
        unsort = jnp.where(jl == p_col, g_col, unsort)
    acc = x1_ref[...] + jnp.dot(unsort.astype(BF16), y_sorted, preferred_element_type=F32)
    out_ref[...] = _rmsnorm(acc, gfin_ref[...])


def _combine(tables, x1, pos, gates, g_final, ys, tm, pairs_per_tile):
    t = x1.shape[0]
    nt = t // tm
    grid_spec = pltpu.PrefetchScalarGridSpec(
        num_scalar_prefetch=len(tables),
        grid=(nt,),
        in_specs=[pl.BlockSpec((tm, D_MODEL), lambda i, *_: (i, 0)),
                  pl.BlockSpec((TOP_K, tm), lambda i, *_: (0, i)),
                  pl.BlockSpec((TOP_K, tm), lambda i, *_: (0, i)),
                  pl.BlockSpec((1, D_MODEL), lambda i, *_: (0, 0)),
                  pl.BlockSpec(memory_space=pl.ANY)],
        out_specs=pl.BlockSpec((tm, D_MODEL), lambda i, *_: (i, 0)),
        scratch_shapes=[pltpu.VMEM((2, pairs_per_tile * ROW_TILES, LANES), U32),
                        pltpu.SemaphoreType.DMA((2,))],
    )
    return pl.pallas_call(
        functools.partial(_combine_kernel, tm=tm, pairs_per_tile=pairs_per_tile),
        grid_spec=grid_spec,
        out_shape=jax.ShapeDtypeStruct((t, D_MODEL), F32),
        compiler_params=pltpu.CompilerParams(
            dimension_semantics=("arbitrary",),
            vmem_limit_bytes=VMEM_LIMIT_MIXER),
        name="combine",
    )(*tables, x1, pos, gates, g_final, ys)


def _split_bf16(w):
    hi = w.astype(BF16)
    lo = (w - hi.astype(F32)).astype(BF16)
    return hi, lo


def kernel(x_prompt, x_sample, cache_conv, cache_mem_k, cache_mem_v, mem_prompt, g_mix, w_in, w_dw, b_dw, ln_conv_g, ln_conv_b, ln_v_g, ln_v_b, w_spatial, b_spatial, g_mem, w_mem_k, w_mem_v, w_out, g_ffn, w_router, b_router, w_gate_up, b_gate_up, w_down, b_down, g_final):
    depth = g_mix.shape[0]
    assert depth == 1
    l = 0
    bp, seq, _ = x_prompt.shape
    bs, dseq, _ = x_sample.shape
    assert seq % PROMPT_TILE == 0 and bs % SAMPLE_SEQS_PER_TILE == 0
    assert GMLP_CHUNK % dseq == 0 and (SAMPLE_SEQS_PER_TILE * dseq) % GMLP_CHUNK == 0

    row = lambda a: a.reshape(1, -1)
    wr_hi, wr_lo = _split_bf16(w_router[l].T)
    w_dw_pad = jnp.pad(w_dw[l], ((0, HIST - CONV_W), (0, 0)))
    bias_rows = lambda b: jnp.repeat(b.T, GMLP_HD, axis=1)
    common = dict(
        gmix=row(g_mix[l]), win=w_in[l].astype(BF16), wdw=w_dw_pad, bdw=row(b_dw[l]),
        lcg=row(ln_conv_g[l]), lcb=row(ln_conv_b[l]), lvg=row(ln_v_g[l]), lvb=row(ln_v_b[l]),
        wout=w_out[l].astype(BF16), gffn=row(g_ffn[l]), wrh=wr_hi, wrl=wr_lo,
        br=b_router[l].reshape(N_EXPERTS, 1))

    def weights(wsp, bsp):
        c = common
        return (c["gmix"], c["win"], c["wdw"], c["bdw"], c["lcg"], c["lcb"], c["lvg"], c["lvb"],
                wsp, bsp, c["wout"], c["gffn"], c["wrh"], c["wrl"], c["br"])

    reps = GMLP_CHUNK // dseq
    wts_p = weights(w_spatial[l], bias_rows(b_spatial[l]))
    wts_s = weights(jnp.tile(w_spatial[l][:, :dseq, :dseq], (1, reps, reps)),
                    bias_rows(jnp.tile(b_spatial[l][:, :dseq], (1, reps))))

    w_kv = jnp.concatenate([w_mem_k[l], w_mem_v[l]], axis=1).astype(BF16)
    kv_p = _memkv(mem_prompt, row(g_mem[l]), w_kv)
    mk_p = kv_p[:, :, :C_XA]
    mv_p = kv_p[:, :, C_XA:]
    zero_hist = jnp.zeros((bp, 1, HIST, C_CONV), F32)
    tp, ts = bp * seq, bs * dseq
    tm_s = SAMPLE_SEQS_PER_TILE * dseq
    ntp, nts = tp // PROMPT_TILE, ts // tm_s
    ppt_p = (TOP_K * PROMPT_TILE + N_EXPERTS + 1) // 2
    ppt_s = (TOP_K * tm_s + N_EXPERTS + 1) // 2
    while (ntp * ppt_p) % ppt_s:
        ppt_s += 1
    pairs_total = ntp * ppt_p + nts * ppt_s
    (x1_p, xs, pos_p, gate_p, cnt_p, off_p, hist_p, _) = _mixer(
        x_prompt, zero_hist, mk_p[:, None], mv_p[:, None], wts_p,
        ns=1, sl=PROMPT_TILE, carry=True, sp_chunk=GMLP_CHUNK,
        pairs_per_tile=ppt_p, pairs_total=pairs_total, pair0=0, xs_prev=None)

    gs = bs // SAMPLE_SEQS_PER_TILE
    hist_s_in = jnp.pad(cache_conv[l], ((0, 0), (HIST_OFF, 0), (0, 0))).reshape(
        gs, SAMPLE_SEQS_PER_TILE, HIST, C_CONV)
    mk_s = cache_mem_k[l].reshape(gs, SAMPLE_SEQS_PER_TILE, N_MEM, C_XA)
    mv_s = cache_mem_v[l].reshape(gs, SAMPLE_SEQS_PER_TILE, N_MEM, C_XA)
    (x1_s, xs, pos_s, gate_s, cnt_s, off_s, hist_s, v_s) = _mixer(
        x_sample.reshape(gs, tm_s, D_MODEL), hist_s_in, mk_s, mv_s, wts_s,
        ns=SAMPLE_SEQS_PER_TILE, sl=dseq, carry=False, sp_chunk=dseq,
        pairs_per_tile=ppt_s, pairs_total=pairs_total, pair0=ntp * ppt_p, xs_prev=xs)

    max_pairs = ((tp + ts) * TOP_K + (ntp + nts) * N_EXPERTS) // 2
    n_blocks = -(-max_pairs // MOE_PAIRS) + N_EXPERTS
    cnt = jnp.concatenate([cnt_p[:, :, 0], cnt_s[:, :, 0]], axis=0)
    off = jnp.concatenate([off_p[:, :, 0], off_s[:, :, 0]], axis=0)
    tile_pair0 = np.concatenate([np.arange(ntp) * ppt_p,
                                 ntp * ppt_p + np.arange(nts) * ppt_s]).astype(np.int32)
    counts = jnp.sum(cnt, axis=0)
    tile_base = jnp.cumsum(cnt, axis=0) - cnt
    padded = (counts + MOE_PAIRS - 1) // MOE_PAIRS * MOE_PAIRS
    pad_end = jnp.cumsum(padded)
    pad_start = pad_end - padded
    strip_dst = (pad_start[None, :] + tile_base).astype(I32)
    strip_src = (tile_pair0[:, None] + off).astype(I32)
    n_used = (pad_end[-1] // MOE_PAIRS).astype(I32)
    blk_pair0 = jnp.minimum(jnp.arange(n_blocks, dtype=I32), n_used - 1) * MOE_PAIRS
    block_expert = jnp.minimum(
        jnp.sum((pad_end[None, :] <= blk_pair0[:, None]).astype(I32), axis=1), N_EXPERTS - 1)
    of_block = block_expert[:, None] == jnp.arange(N_EXPERTS, dtype=I32)[None, :]
    last_pair = jnp.sum(jnp.where(of_block, (pad_start + counts)[None, :], 0), axis=1)
    block_valid = jnp.clip(last_pair - blk_pair0, 0, MOE_PAIRS).astype(I32)
    sdst = strip_dst.T.reshape(-1)
    ssrc = strip_src.T.reshape(-1)
    slen = cnt.T.reshape(-1).astype(I32)
    s_lo = jnp.sum(((sdst + slen)[None, :] <= blk_pair0[:, None]).astype(I32), axis=1)
    s_hi = jnp.sum((sdst[None, :] < (blk_pair0 + MOE_PAIRS)[:, None]).astype(I32), axis=1)
    moe_tables = ((pad_start // MOE_PAIRS).astype(I32), (padded // MOE_PAIRS).astype(I32),
                  block_valid, n_used.reshape(1),
                  s_lo.astype(I32), s_hi.astype(I32), ssrc, sdst, slen)

    slots = n_blocks * MOE_PAIRS
    ys = _moe(moe_tables, xs, slots,
              w_gate_up[l], b_gate_up[l][:, None, :], w_down[l], b_down[l][:, None, :])
    gfin = row(g_final)

    def combine_tables(t0, t1):
        return (strip_dst[t0:t1].reshape(-1), cnt[t0:t1].reshape(-1).astype(I32),
                off[t0:t1].reshape(-1).astype(I32), jnp.sum(cnt[t0:t1], axis=1).astype(I32))

    y_p = _combine(combine_tables(0, ntp), x1_p, pos_p, gate_p, gfin, ys, PROMPT_TILE, ppt_p)
    y_s = _combine(combine_tables(ntp, ntp + nts), x1_s, pos_s, gate_s, gfin, ys, tm_s, ppt_s)

    return (y_p.reshape(bp, seq, D_MODEL),
            y_s.reshape(bs, dseq, D_MODEL),
            hist_p[:, 0, HIST_OFF:, :][None],
            mk_p.reshape(bp, N_MEM, XA_HEADS, XA_HD)[None],
            mv_p.reshape(bp, N_MEM, XA_HEADS, XA_HD)[None],
            hist_s.reshape(bs, HIST, C_CONV)[:, HIST_OFF:, :][None],
            v_s.reshape(bs, dseq, C_GMLP)[None])
```

```python
import functools

import numpy as np
import jax
import jax.numpy as jnp
from jax import lax
from jax.experimental import pallas as pl
from jax.experimental.pallas import tpu as pltpu

F32 = jnp.float32
BF16 = jnp.bfloat16
I32 = jnp.int32
U32 = jnp.uint32

D_MODEL = 1024
C_CONV = 384
CONV_W = 31
C_GMLP = 384
GMLP_HEADS = 4
GMLP_HD = 96
GMLP_CHUNK = 128
XA_HEADS = 4
XA_HD = 64
C_XA = 256
N_MEM = 256
N_EXPERTS = 32
TOP_K = 4
D_FF = 1024
SWIGLU_LIMIT = 7.0
SWIGLU_ALPHA = 1.702
EPS = 1e-5
IN_COLS = 2 * C_CONV + 2 * C_GMLP + C_XA

SUBLANES = 8
LANES = 128
ROW_TILES = D_MODEL // LANES
HIST = 32
HIST_OFF = HIST - (CONV_W - 1)

PROMPT_TILE = 512
SAMPLE_SEQS_PER_TILE = 8
MOE_BLOCK = 512
MOE_PAIRS = MOE_BLOCK // 2
GATHER_AHEAD = 2
VMEM_LIMIT_MIXER = 48 * 1024 * 1024
VMEM_LIMIT_MOE = 52 * 1024 * 1024


def _rmsnorm(x, g):
    return x * lax.rsqrt(jnp.mean(x * x, axis=-1, keepdims=True) + EPS) * g


def _layernorm(x, g, b):
    mu = jnp.mean(x, axis=-1, keepdims=True)
    xc = x - mu
    var = jnp.mean(xc * xc, axis=-1, keepdims=True)
    return xc * lax.rsqrt(var + EPS) * g + b


def _gelu(x):
    return 0.5 * x * (1.0 + lax.erf(x * np.float32(1.0 / np.sqrt(2.0))))


def _sigmoid(x):
    return 1.0 / (1.0 + jnp.exp(-x))


def _rows_to_tiles(dst_ref, val, rows, row0=0):
    for j in range(ROW_TILES):
        dst_ref[pl.ds(row0 * ROW_TILES + j, rows, stride=ROW_TILES), :] = val[:, j * LANES:(j + 1) * LANES]


def _tiles_to_rows(src_ref, rows, row0=0):
    return jnp.concatenate(
        [src_ref[pl.ds(row0 * ROW_TILES + j, rows, stride=ROW_TILES), :] for j in range(ROW_TILES)],
        axis=-1)


def _memkv_kernel(mem_ref, g_ref, w_ref, o_ref):
    mn = _rmsnorm(mem_ref[0], g_ref[...])
    o_ref[0] = jnp.dot(mn.astype(BF16), w_ref[...], preferred_element_type=F32)


def _memkv(mem, g_mem, w_kv):
    b = mem.shape[0]
    return pl.pallas_call(
        _memkv_kernel,
        grid=(b,),
        in_specs=[pl.BlockSpec((1, N_MEM, D_MODEL), lambda i: (i, 0, 0)),
                  pl.BlockSpec((1, D_MODEL), lambda i: (0, 0)),
                  pl.BlockSpec((D_MODEL, 2 * C_XA), lambda i: (0, 0))],
        out_specs=pl.BlockSpec((1, N_MEM, 2 * C_XA), lambda i: (i, 0, 0)),
        out_shape=jax.ShapeDtypeStruct((b, N_MEM, 2 * C_XA), F32),
        name="memkv",
    )(mem, g_mem, w_kv)


def _mixer_kernel(x_ref, hist_ref, mk_ref, mv_ref, gmix_ref, win_ref, wdw_ref, bdw_ref,
                  lcg_ref, lcb_ref, lvg_ref, lvb_ref, wsp_ref, bsp_ref, wout_ref, gffn_ref,
                  wrh_ref, wrl_ref, br_ref, *rest, ns, sl, carry, sp_chunk, tiles_per_seq, n_sorted):
    (x1_ref, xst_ref, pos_ref, gate_ref, cnt_ref, off_ref, histout_ref, v_ref,
     ext_ref, shift_ref, h2_keep, pos_keep) = rest[-12:]
    tm = ns * sl
    step = pl.program_id(0)
    n_tiles = pl.num_programs(0) - 1

    def sort_previous_tile():
        jj = lax.broadcasted_iota(I32, (n_sorted, tm), 0)
        perm = jnp.zeros((n_sorted, tm), F32)
        for k in range(TOP_K):
            perm = jnp.where(jj == pos_keep[k:k + 1, :], 1.0, perm)
        sorted_rows = jnp.dot(perm.astype(BF16), h2_keep[...], preferred_element_type=F32)
        pairs = pltpu.bitcast(sorted_rows.astype(BF16), U32)
        _rows_to_tiles(xst_ref, pairs, n_sorted // 2)

    @pl.when(step == 0)
    def _():
        h2_keep[...] = jnp.zeros_like(h2_keep)
        pos_keep[...] = jnp.full(pos_keep.shape, -1, I32)

    @pl.when(step < n_tiles)
    def _():
        sort_previous_tile()
        _mixer_tile(x_ref, hist_ref, mk_ref, mv_ref, gmix_ref, win_ref, wdw_ref, bdw_ref,
                    lcg_ref, lcb_ref, lvg_ref, lvb_ref, wsp_ref, bsp_ref, wout_ref, gffn_ref,
                    wrh_ref, wrl_ref, br_ref, x1_ref, pos_ref, gate_ref, cnt_ref, off_ref,
                    histout_ref, v_ref, ext_ref, shift_ref, h2_keep, pos_keep,
                    first_of_seq=(step % tiles_per_seq) == 0,
                    ns=ns, sl=sl, carry=carry, sp_chunk=sp_chunk)

    @pl.when(step == n_tiles)
    def _():
        sort_previous_tile()


def _mixer_tile(x_ref, hist_ref, mk_ref, mv_ref, gmix_ref, win_ref, wdw_ref, bdw_ref,
                lcg_ref, lcb_ref, lvg_ref, lvb_ref, wsp_ref, bsp_ref, wout_ref, gffn_ref,
                wrh_ref, wrl_ref, br_ref, x1_ref, pos_ref, gate_ref, cnt_ref, off_ref,
                histout_ref, v_ref, ext_ref, shift_ref, h2_keep, pos_keep,
                *, first_of_seq, ns, sl, carry, sp_chunk):
    tm = ns * sl
    x = x_ref[0]
    h = _rmsnorm(x, gmix_ref[...])
    z = jnp.dot(h.astype(BF16), win_ref[...], preferred_element_type=F32)
    z_a = z[:, 0:C_CONV]
    z_g = z[:, C_CONV:2 * C_CONV]
    z_u = z[:, 2 * C_CONV:2 * C_CONV + C_GMLP]
    z_v = z[:, 2 * C_CONV + C_GMLP:2 * C_CONV + 2 * C_GMLP]
    z_q = z[:, 2 * C_CONV + 2 * C_GMLP:IN_COLS]

    glu = z_a * _sigmoid(z_g)
    if carry:
        ext_ref[:, 0:HIST, :] = jnp.where(first_of_seq, hist_ref[0], ext_ref[:, 0:HIST, :])
    else:
        ext_ref[:, 0:HIST, :] = hist_ref[0]
    conv_parts = []
    for s in range(ns):
        ext_s = ext_ref.at[s]
        ext_s[HIST:HIST + sl, :] = glu[s * sl:(s + 1) * sl]
        n_shift = HIST + sl - SUBLANES
        for r in range(1, SUBLANES):
            shift_ref[s, r - 1, 0:n_shift, :] = ext_s[pl.ds(r, n_shift), :]
        rc = min(sl, 64)
        for r0 in range(0, sl, rc):
            acc = jnp.broadcast_to(bdw_ref[...], (rc, C_CONV))
            for j in range(CONV_W):
                a, r = divmod(j + HIST_OFF, SUBLANES)
                src = ext_s if r == 0 else shift_ref.at[s, r - 1]
                acc = acc + wdw_ref[j:j + 1, :] * src[pl.ds(r0 + a * SUBLANES, rc), :]
            conv_parts.append(acc)
        new_hist = ext_s[sl:sl + HIST, :]
        histout_ref[0, s] = new_hist
        if carry:
            ext_s[0:HIST, :] = new_hist
    y = jnp.concatenate(conv_parts, axis=0) if len(conv_parts) > 1 else conv_parts[0]
    y = _layernorm(y, lcg_ref[...], lcb_ref[...])
    c_out = y * _sigmoid(y)

    u = _gelu(z_u)
    v = _layernorm(_gelu(z_v), lvg_ref[...], lvb_ref[...])
    v_ref[...] = v
    vb = v.astype(BF16)
    rr = lax.broadcasted_iota(I32, (GMLP_CHUNK, GMLP_CHUNK), 0)
    cc = lax.broadcasted_iota(I32, (GMLP_CHUNK, GMLP_CHUNK), 1)
    sp_mask = (cc <= rr) & ((rr // sp_chunk) == (cc // sp_chunk))
    col = lax.broadcasted_iota(I32, (GMLP_CHUNK, C_GMLP), 1)
    w_heads = [jnp.where(sp_mask, wsp_ref[hh], 0.0).astype(BF16) for hh in range(GMLP_HEADS)]
    g_parts = []
    for c in range(tm // GMLP_CHUNK):
        vc = vb[c * GMLP_CHUNK:(c + 1) * GMLP_CHUNK]
        sg = bsp_ref[...]
        for hh in range(GMLP_HEADS):
            head_cols = (col >= hh * GMLP_HD) & (col < (hh + 1) * GMLP_HD)
            vh = jnp.where(head_cols, vc, jnp.zeros_like(vc))
            sg = sg + jnp.dot(w_heads[hh], vh, preferred_element_type=F32)
        g_parts.append(u[c * GMLP_CHUNK:(c + 1) * GMLP_CHUNK] * sg)
    g_out = jnp.concatenate(g_parts, axis=0) if len(g_parts) > 1 else g_parts[0]

    qs = z_q * np.float32(XA_HD ** -0.5)
    qcol = lax.broadcasted_iota(I32, (sl, C_XA), 1)
    a_parts = []
    for s in range(ns):
        q_s = qs[s * sl:(s + 1) * sl]
        kb = mk_ref[0, s].astype(BF16)
        vvb = mv_ref[0, s].astype(BF16)
        hmasks = [(qcol >= hh * XA_HD) & (qcol < (hh + 1) * XA_HD) for hh in range(XA_HEADS)]
        stack = XA_HEADS if sl * XA_HEADS <= N_MEM else 1
        a_s = jnp.zeros((sl, C_XA), F32)
        for h0 in range(0, XA_HEADS, stack):
            heads = range(h0, h0 + stack)
            qh = jnp.concatenate([jnp.where(hmasks[hh], q_s, 0.0) for hh in heads], axis=0).astype(BF16)
            sc = lax.dot_general(qh, kb, (((1,), (1,)), ((), ())), preferred_element_type=F32)
            p = jnp.exp(sc - jnp.max(sc, axis=-1, keepdims=True))
            den = jnp.sum(p, axis=-1, keepdims=True)
            oh = jnp.dot(p.astype(BF16), vvb, preferred_element_type=F32) / den
            for n, hh in enumerate(heads):
                a_s = a_s + jnp.where(hmasks[hh], oh[n * sl:(n + 1) * sl], 0.0)
        a_parts.append(a_s)
    a_out = jnp.concatenate(a_parts, axis=0) if len(a_parts) > 1 else a_parts[0]

    mix = jnp.concatenate([c_out, g_out, a_out], axis=-1).astype(BF16)
    x1 = x + jnp.dot(mix, wout_ref[...], preferred_element_type=F32)
    x1_ref[...] = x1

    h2 = _rmsnorm(x1, gffn_ref[...])
    h2_hi = h2.astype(BF16)
    h2_lo = (h2 - h2_hi.astype(F32)).astype(BF16)
    nt_dims = (((1,), (1,)), ((), ()))
    lg = (lax.dot_general(wrh_ref[...], h2_hi, nt_dims, preferred_element_type=F32)
          + lax.dot_general(wrl_ref[...], h2_hi, nt_dims, preferred_element_type=F32)
          + lax.dot_general(wrh_ref[...], h2_lo, nt_dims, preferred_element_type=F32)
          + br_ref[...])
    eio = lax.broadcasted_iota(I32, (N_EXPERTS, tm), 0)
    work = lg
    vals, idxs = [], []
    for _ in range(TOP_K):
        m = jnp.max(work, axis=0, keepdims=True)
        idx = jnp.min(jnp.where(work == m, eio, N_EXPERTS), axis=0, keepdims=True)
        vals.append(m)
        idxs.append(idx)
        work = jnp.where(eio == idx, -jnp.inf, work)
    exps = [jnp.exp(vk - vals[0]) for vk in vals]
    den = exps[0] + exps[1] + exps[2] + exps[3]
    gate_ref[...] = jnp.concatenate([ek / den for ek in exps], axis=0)

    sel = jnp.zeros((N_EXPERTS, tm), F32)
    for idx in idxs:
        sel = sel + jnp.where(eio == idx, 1.0, 0.0)
    selb = sel.astype(BF16)
    tr = lax.broadcasted_iota(I32, (tm, tm), 0)
    tc = lax.broadcasted_iota(I32, (tm, tm), 1)
    before = jnp.where(tr < tc, 1.0, 0.0).astype(BF16)
    ranks = jnp.dot(selb, before, preferred_element_type=F32)
    er = lax.broadcasted_iota(I32, (N_EXPERTS, N_EXPERTS), 0)
    ec = lax.broadcasted_iota(I32, (N_EXPERTS, N_EXPERTS), 1)
    lower = jnp.where(ec < er, 1.0, 0.0).astype(BF16)
    cnt = jnp.sum(sel, axis=1, keepdims=True)
    half_len = jnp.floor((cnt + 1.0) * 0.5)
    off_pairs = jnp.dot(lower, jnp.broadcast_to(half_len, (N_EXPERTS, LANES)).astype(BF16),
                        preferred_element_type=F32)[:, 0:1]
    cnt_ref[0] = jnp.broadcast_to(half_len, (N_EXPERTS, LANES)).astype(I32)
    off_ref[0] = jnp.broadcast_to(off_pairs, (N_EXPERTS, LANES)).astype(I32)
    slot_of = 2.0 * off_pairs + ranks
    pos = [jnp.sum(jnp.where(eio == idx, slot_of, 0.0), axis=0, keepdims=True).astype(I32) for idx in idxs]
    pos_all = jnp.concatenate(pos, axis=0)
    pos_ref[...] = pos_all
    pos_keep[...] = pos_all
    h2_keep[...] = h2_hi


def _mixer(x, hist, mem_k, mem_v, wts, *, ns, sl, carry, sp_chunk, pairs_per_tile, pairs_total, pair0,
           xs_prev):
    g, r, _ = x.shape
    tm = ns * sl
    nt = r // tm
    ntot = g * nt
    assert 2 * pairs_per_tile >= TOP_K * tm + N_EXPERTS and pair0 % pairs_per_tile == 0
    xs_blk0 = pair0 // pairs_per_tile
    tile = lambda s: jnp.minimum(s, ntot - 1)
    const2 = lambda s: (0, 0)
    const3 = lambda s: (0, 0, 0)
    tile_row = lambda s: (tile(s), 0)
    tile_lane = lambda s: (0, tile(s))
    per_seq = lambda s: (tile(s) // nt, 0, 0, 0)
    in_specs = [
        pl.BlockSpec((1, tm, D_MODEL), lambda s: (tile(s) // nt, tile(s) % nt, 0)),
        pl.BlockSpec((1, ns, HIST, C_CONV), per_seq),
        pl.BlockSpec((1, ns, N_MEM, C_XA), per_seq),
        pl.BlockSpec((1, ns, N_MEM, C_XA), per_seq),
        pl.BlockSpec((1, D_MODEL), const2),
        pl.BlockSpec((D_MODEL, IN_COLS), const2),
        pl.BlockSpec((HIST, C_CONV), const2),
        pl.BlockSpec((1, C_CONV), const2),
        pl.BlockSpec((1, C_CONV), const2),
        pl.BlockSpec((1, C_CONV), const2),
        pl.BlockSpec((1, C_GMLP), const2),
        pl.BlockSpec((1, C_GMLP), const2),
        pl.BlockSpec((GMLP_HEADS, GMLP_CHUNK, GMLP_CHUNK), const3),
        pl.BlockSpec((GMLP_CHUNK, C_GMLP), const2),
        pl.BlockSpec((D_MODEL, D_MODEL), const2),
        pl.BlockSpec((1, D_MODEL), const2),
        pl.BlockSpec((N_EXPERTS, D_MODEL), const2),
        pl.BlockSpec((N_EXPERTS, D_MODEL), const2),
        pl.BlockSpec((N_EXPERTS, 1), const2),
    ]
    tile_cnt = lambda s: (tile(s), 0, 0)
    out_specs = [
        pl.BlockSpec((tm, D_MODEL), tile_row),
        pl.BlockSpec((pairs_per_tile * ROW_TILES, LANES),
                     lambda s: (xs_blk0 + jnp.maximum(s - 1, 0), 0)),
        pl.BlockSpec((TOP_K, tm), tile_lane),
        pl.BlockSpec((TOP_K, tm), tile_lane),
        pl.BlockSpec((1, N_EXPERTS, LANES), tile_cnt),
        pl.BlockSpec((1, N_EXPERTS, LANES), tile_cnt),
        pl.BlockSpec((1, ns, HIST, C_CONV), per_seq),
        pl.BlockSpec((tm, C_GMLP), tile_row),
    ]
    rows = g * r
    out_shape = [
        jax.ShapeDtypeStruct((rows, D_MODEL), F32),
        jax.ShapeDtypeStruct((pairs_total * ROW_TILES, LANES), U32),
        jax.ShapeDtypeStruct((TOP_K, rows), I32),
        jax.ShapeDtypeStruct((TOP_K, rows), F32),
        jax.ShapeDtypeStruct((ntot, N_EXPERTS, LANES), I32),
        jax.ShapeDtypeStruct((ntot, N_EXPERTS, LANES), I32),
        jax.ShapeDtypeStruct((g, ns, HIST, C_CONV), F32),
        jax.ShapeDtypeStruct((rows, C_GMLP), F32),
    ]
    args = [x, hist, mem_k, mem_v, *wts]
    aliases = {}
    if xs_prev is not None:
        in_specs.append(pl.BlockSpec(memory_space=pl.ANY))
        aliases = {len(args): 1}
        args.append(xs_prev)
    kern = functools.partial(_mixer_kernel, ns=ns, sl=sl, carry=carry, sp_chunk=sp_chunk,
                             tiles_per_seq=nt, n_sorted=2 * pairs_per_tile)
    return pl.pallas_call(
        kern,
        grid=(ntot + 1,),
        in_specs=in_specs,
        out_specs=out_specs,
        out_shape=out_shape,
        scratch_shapes=[pltpu.VMEM((ns, HIST + sl, C_CONV), F32),
                        pltpu.VMEM((ns, SUBLANES - 1, HIST + sl, C_CONV), F32),
                        pltpu.VMEM((tm, D_MODEL), BF16),
                        pltpu.VMEM((TOP_K, tm), I32)],
        input_output_aliases=aliases,
        compiler_params=pltpu.CompilerParams(
            dimension_semantics=("arbitrary",),
            vmem_limit_bytes=VMEM_LIMIT_MIXER),
        name="mixer_carry" if carry else "mixer_cache",
    )(*args)


def _strip_pieces(n, max_rows, fn):
    done = 0
    p = max_rows
    while p >= 1:
        has = (n & p) != 0
        pl.when(has)(functools.partial(fn, done, p))
        done = done + jnp.where(has, p, 0)
        p //= 2


class _Table:
    def __init__(self, ref, offset):
        self.ref, self.offset = ref, offset

    def __getitem__(self, i):
        return self.ref[self.offset + i]


def _pack_tables(tables):
    offsets = tuple(int(o) for o in np.cumsum([0] + [t.shape[0] for t in tables[:-1]]))
    return jnp.concatenate([t.astype(I32) for t in tables]), offsets


def _pow2_at_most(n):
    return 1 << (n.bit_length() - 1)


def _rows(ref, first_row, rows):
    return ref.at[pl.ds(pl.multiple_of(first_row * ROW_TILES, ROW_TILES), rows * ROW_TILES)]


def _moe_kernel(tab_ref, xs_hbm, wgu_ref, bgu_ref, wdn_ref, bdn_ref, ys_hbm,
                xbuf, ybuf, wgu16, wdn16, sems, ysems, *, offsets):
    (b0_ref, nbk_ref, bv_ref, nb_ref, slo_ref, shi_ref, ssrc_ref, sdst_ref,
     slen_ref) = [_Table(tab_ref, o) for o in offsets]
    e = pl.program_id(0)
    n_used = nb_ref[0]
    half = MOE_PAIRS // 2

    def gather_start(b, slot):
        pair0 = b * MOE_PAIRS

        def strip(s, carry):
            lo = jnp.maximum(sdst_ref[s], pair0)
            hi = jnp.minimum(sdst_ref[s] + slen_ref[s], pair0 + MOE_PAIRS)
            src0 = ssrc_ref[s] + (lo - sdst_ref[s])
            dst0 = lo - pair0

            def piece(first, rows):
                pltpu.make_async_copy(_rows(xs_hbm, src0 + first, rows),
                                      _rows(xbuf.at[slot], dst0 + first, rows), sems.at[slot]).start()
            _strip_pieces(hi - lo, MOE_PAIRS, piece)
            return carry
        lax.fori_loop(slo_ref[b], shi_ref[b], strip, 0)

    def gather_wait(b, slot):
        def piece(first, rows):
            del first
            pltpu.make_async_copy(_rows(xs_hbm, 0, rows), _rows(xbuf.at[slot], 0, rows), sems.at[slot]).wait()
        _strip_pieces(bv_ref[b], MOE_PAIRS, piece)

    def out_copy(b, slot):
        return pltpu.make_async_copy(ybuf.at[slot], _rows(ys_hbm, b * MOE_PAIRS, MOE_PAIRS), ysems.at[slot])

    def expert_mlp(xslot, slot, h):
        xb = pltpu.bitcast(_tiles_to_rows(xbuf.at[xslot], half, h * half), BF16)
        gu = jnp.dot(xb, wgu16[...], preferred_element_type=F32) + bgu_ref[0]
        gate = jnp.minimum(gu[:, :D_FF], SWIGLU_LIMIT)
        up = jnp.clip(gu[:, D_FF:], -SWIGLU_LIMIT, SWIGLU_LIMIT)
        act = (up + 1.0) * (gate * _sigmoid(SWIGLU_ALPHA * gate))
        yb = jnp.dot(act.astype(BF16), wdn16[...], preferred_element_type=F32) + bdn_ref[0]
        _rows_to_tiles(ybuf.at[slot], pltpu.bitcast(yb.astype(BF16), U32), half, h * half)

    @pl.when(e == 0)
    def _():
        xbuf[...] = jnp.zeros_like(xbuf)
        for a in range(GATHER_AHEAD):
            @pl.when(a < n_used)
            def _():
                gather_start(a, a)

    @pl.when(nbk_ref[e] > 0)
    def _():
        def cast_rows(c, carry):
            r = pl.multiple_of(c * LANES, LANES)
            wgu16[pl.ds(r, LANES), :] = wgu_ref[pl.ds(r, LANES), :].astype(BF16)
            wdn16[pl.ds(r, LANES), :] = wdn_ref[pl.ds(r, LANES), :].astype(BF16)
            return carry
        lax.fori_loop(0, D_MODEL // LANES, cast_rows, 0)

        def block(j, carry):
            b = b0_ref[e] + j
            slot = b % 2
            xslot = b % (GATHER_AHEAD + 1)

            @pl.when(b + GATHER_AHEAD < n_used)
            def _():
                gather_start(b + GATHER_AHEAD, (b + GATHER_AHEAD) % (GATHER_AHEAD + 1))

            gather_wait(b, xslot)

            @pl.when(b >= 2)
            def _():
                out_copy(b - 2, slot).wait()

            valid = bv_ref[b]
            expert_mlp(xslot, slot, 0)

            @pl.when(valid > half)
            def _():
                expert_mlp(xslot, slot, 1)

            @pl.when(valid <= half)
            def _():
                ybuf[slot, pl.ds(half * ROW_TILES, half * ROW_TILES), :] = jnp.zeros(
                    (half * ROW_TILES, LANES), U32)

            out_copy(b, slot).start()
            return carry
        lax.fori_loop(0, nbk_ref[e], block, 0)

    @pl.when(e == pl.num_programs(0) - 1)
    def _():
        for back in (1, 2):
            @pl.when(n_used >= back)
            def _():
                out_copy(n_used - back, (n_used - back) % 2).wait()


def _moe(tables, xs, slots, w_gu, b_gu, w_dn, b_dn):
    wsel = lambda e, *_: (e, 0, 0)
    packed, offsets = _pack_tables(tables)
    grid_spec = pltpu.PrefetchScalarGridSpec(
        num_scalar_prefetch=1,
        grid=(N_EXPERTS,),
        in_specs=[pl.BlockSpec(memory_space=pl.ANY),
                  pl.BlockSpec((None, D_MODEL, 2 * D_FF), wsel),
                  pl.BlockSpec((None, 1, 2 * D_FF), wsel),
                  pl.BlockSpec((None, D_FF, D_MODEL), wsel),
                  pl.BlockSpec((None, 1, D_MODEL), wsel)],
        out_specs=pl.BlockSpec(memory_space=pl.ANY),
        scratch_shapes=[pltpu.VMEM((GATHER_AHEAD + 1, MOE_PAIRS * ROW_TILES, LANES), U32),
                        pltpu.VMEM((2, MOE_PAIRS * ROW_TILES, LANES), U32),
                        pltpu.VMEM((D_MODEL, 2 * D_FF), BF16),
                        pltpu.VMEM((D_FF, D_MODEL), BF16),
                        pltpu.SemaphoreType.DMA((GATHER_AHEAD + 1,)),
                        pltpu.SemaphoreType.DMA((2,))],
    )
    return pl.pallas_call(
        functools.partial(_moe_kernel, offsets=offsets),
        grid_spec=grid_spec,
        out_shape=jax.ShapeDtypeStruct((slots * ROW_TILES, LANES), U32),
        compiler_params=pltpu.CompilerParams(
            dimension_semantics=("arbitrary",),
            vmem_limit_bytes=VMEM_LIMIT_MOE),
        name="moe",
    )(packed, xs, w_gu, b_gu, w_dn, b_dn)


def _combine_kernel(tab_ref, x1_ref, pos_ref, gate_ref, gfin_ref, ys_hbm,
                    out_ref, ybuf, sems, *, tm, pairs_per_tile, offsets):
    csrc_ref, clen_ref, coff_ref, ctot_ref = [_Table(tab_ref, o) for o in offsets]
    i = pl.program_id(0)
    n_tiles = pl.num_programs(0)
    n_sorted = 2 * pairs_per_tile

    def gather_start(t, slot):
        def strip(e, carry):
            s = t * N_EXPERTS + e
            src0 = csrc_ref[s]
            dst0 = coff_ref[s]

            def piece(first, rows):
                pltpu.make_async_copy(_rows(ys_hbm, src0 + first, rows),
                                      _rows(ybuf.at[slot], dst0 + first, rows), sems.at[slot]).start()
            _strip_pieces(clen_ref[s], tm // 2, piece)
            return carry
        lax.fori_loop(0, N_EXPERTS, strip, 0)

    @pl.when(i == 0)
    def _():
        ybuf[...] = jnp.zeros_like(ybuf)
        gather_start(0, 0)

    @pl.when(i + 1 < n_tiles)
    def _():
        gather_start(i + 1, (i + 1) % 2)

    slot = i % 2

    def wait_piece(first, rows):
        del first
        pltpu.make_async_copy(_rows(ys_hbm, 0, rows), _rows(ybuf.at[slot], 0, rows), sems.at[slot]).wait()
    _strip_pieces(ctot_ref[i], _pow2_at_most(pairs_per_tile), wait_piece)

    y_sorted = pltpu.bitcast(_tiles_to_rows(ybuf.at[slot], pairs_per_tile), BF16)

    rr = lax.broadcasted_iota(I32, (tm, tm), 0)
    cc = lax.broadcasted_iota(I32, (tm, tm), 1)
    eye = rr == cc
    jl = lax.broadcasted_iota(I32, (tm, n_sorted), 1).astype(F32)
    unsort = jnp.zeros((tm, n_sorted), F32)
    for k in range(TOP_K):
        p_col = jnp.sum(jnp.where(eye, pos_ref[k:k + 1, :].astype(F32), 0.0), axis=1, keepdims=True)
        g_col = jnp.sum(jnp.where(eye, gate_ref[k:k + 1, :], 0.0), axis=1, keepdims=True)
        unsort = jnp.where(jl == p_col, g_col, unsort)
    acc = x1_ref[...] + jnp.dot(unsort.astype(BF16), y_sorted, preferred_element_type=F32)
    out_ref[...] = _rmsnorm(acc, gfin_ref[...])


def _combine(tables, x1, pos, gates, g_final, ys, tm, pairs_per_tile):
    t = x1.shape[0]
    nt = t // tm
    packed, offsets = _pack_tables(tables)
    grid_spec = pltpu.PrefetchScalarGridSpec(
        num_scalar_prefetch=1,
        grid=(nt,),
        in_specs=[pl.BlockSpec((tm, D_MODEL), lambda i, *_: (i, 0)),
                  pl.BlockSpec((TOP_K, tm), lambda i, *_: (0, i)),
                  pl.BlockSpec((TOP_K, tm), lambda i, *_: (0, i)),
                  pl.BlockSpec((1, D_MODEL), lambda i, *_: (0, 0)),
                  pl.BlockSpec(memory_space=pl.ANY)],
        out_specs=pl.BlockSpec((tm, D_MODEL), lambda i, *_: (i, 0)),
        scratch_shapes=[pltpu.VMEM((2, pairs_per_tile * ROW_TILES, LANES), U32),
                        pltpu.SemaphoreType.DMA((2,))],
    )
    return pl.pallas_call(
        functools.partial(_combine_kernel, tm=tm, pairs_per_tile=pairs_per_tile, offsets=offsets),
        grid_spec=grid_spec,
        out_shape=jax.ShapeDtypeStruct((t, D_MODEL), F32),
        compiler_params=pltpu.CompilerParams(
            dimension_semantics=("arbitrary",),
            vmem_limit_bytes=VMEM_LIMIT_MIXER),
        name="combine",
    )(packed, x1, pos, gates, g_final, ys)


def _split_bf16(w):
    hi = w.astype(BF16)
    lo = (w - hi.astype(F32)).astype(BF16)
    return hi, lo


def kernel(x_prompt, x_sample, cache_conv, cache_mem_k, cache_mem_v, mem_prompt, g_mix, w_in, w_dw, b_dw, ln_conv_g, ln_conv_b, ln_v_g, ln_v_b, w_spatial, b_spatial, g_mem, w_mem_k, w_mem_v, w_out, g_ffn, w_router, b_router, w_gate_up, b_gate_up, w_down, b_down, g_final):
    depth = g_mix.shape[0]
    assert depth == 1
    l = 0
    bp, seq, _ = x_prompt.shape
    bs, dseq, _ = x_sample.shape
    assert seq % PROMPT_TILE == 0 and bs % SAMPLE_SEQS_PER_TILE == 0
    assert GMLP_CHUNK % dseq == 0 and (SAMPLE_SEQS_PER_TILE * dseq) % GMLP_CHUNK == 0

    row = lambda a: a.reshape(1, -1)
    wr_hi, wr_lo = _split_bf16(w_router[l].T)
    w_dw_pad = jnp.pad(w_dw[l], ((0, HIST - CONV_W), (0, 0)))
    bias_rows = lambda b: jnp.repeat(b.T, GMLP_HD, axis=1)
    common = dict(
        gmix=row(g_mix[l]), win=w_in[l].astype(BF16), wdw=w_dw_pad, bdw=row(b_dw[l]),
        lcg=row(ln_conv_g[l]), lcb=row(ln_conv_b[l]), lvg=row(ln_v_g[l]), lvb=row(ln_v_b[l]),
        wout=w_out[l].astype(BF16), gffn=row(g_ffn[l]), wrh=wr_hi, wrl=wr_lo,
        br=b_router[l].reshape(N_EXPERTS, 1))

    def weights(wsp, bsp):
        c = common
        return (c["gmix"], c["win"], c["wdw"], c["bdw"], c["lcg"], c["lcb"], c["lvg"], c["lvb"],
                wsp, bsp, c["wout"], c["gffn"], c["wrh"], c["wrl"], c["br"])

    reps = GMLP_CHUNK // dseq
    wts_p = weights(w_spatial[l], bias_rows(b_spatial[l]))
    wts_s = weights(jnp.tile(w_spatial[l][:, :dseq, :dseq], (1, reps, reps)),
                    bias_rows(jnp.tile(b_spatial[l][:, :dseq], (1, reps))))

    w_kv = jnp.concatenate([w_mem_k[l], w_mem_v[l]], axis=1).astype(BF16)
    kv_p = _memkv(mem_prompt, row(g_mem[l]), w_kv)
    mk_p = kv_p[:, :, :C_XA]
    mv_p = kv_p[:, :, C_XA:]
    zero_hist = jnp.zeros((bp, 1, HIST, C_CONV), F32)
    tp, ts = bp * seq, bs * dseq
    tm_s = SAMPLE_SEQS_PER_TILE * dseq
    ntp, nts = tp // PROMPT_TILE, ts // tm_s
    ppt_p = (TOP_K * PROMPT_TILE + N_EXPERTS + 1) // 2
    ppt_s = (TOP_K * tm_s + N_EXPERTS + 1) // 2
    while (ntp * ppt_p) % ppt_s:
        ppt_s += 1
    pairs_total = ntp * ppt_p + nts * ppt_s
    (x1_p, xs, pos_p, gate_p, cnt_p, off_p, hist_p, _) = _mixer(
        x_prompt, zero_hist, mk_p[:, None], mv_p[:, None], wts_p,
        ns=1, sl=PROMPT_TILE, carry=True, sp_chunk=GMLP_CHUNK,
        pairs_per_tile=ppt_p, pairs_total=pairs_total, pair0=0, xs_prev=None)

    gs = bs // SAMPLE_SEQS_PER_TILE
    hist_s_in = jnp.pad(cache_conv[l], ((0, 0), (HIST_OFF, 0), (0, 0))).reshape(
        gs, SAMPLE_SEQS_PER_TILE, HIST, C_CONV)
    mk_s = cache_mem_k[l].reshape(gs, SAMPLE_SEQS_PER_TILE, N_MEM, C_XA)
    mv_s = cache_mem_v[l].reshape(gs, SAMPLE_SEQS_PER_TILE, N_MEM, C_XA)
    (x1_s, xs, pos_s, gate_s, cnt_s, off_s, hist_s, v_s) = _mixer(
        x_sample.reshape(gs, tm_s, D_MODEL), hist_s_in, mk_s, mv_s, wts_s,
        ns=SAMPLE_SEQS_PER_TILE, sl=dseq, carry=False, sp_chunk=dseq,
        pairs_per_tile=ppt_s, pairs_total=pairs_total, pair0=ntp * ppt_p, xs_prev=xs)

    max_pairs = ((tp + ts) * TOP_K + (ntp + nts) * N_EXPERTS) // 2
    n_blocks = -(-max_pairs // MOE_PAIRS) + N_EXPERTS
    cnt = jnp.concatenate([cnt_p[:, :, 0], cnt_s[:, :, 0]], axis=0)
    off = jnp.concatenate([off_p[:, :, 0], off_s[:, :, 0]], axis=0)
    tile_pair0 = np.concatenate([np.arange(ntp) * ppt_p,
                                 ntp * ppt_p + np.arange(nts) * ppt_s]).astype(np.int32)
    counts = jnp.sum(cnt, axis=0)
    tile_base = jnp.cumsum(cnt, axis=0) - cnt
    padded = (counts + MOE_PAIRS - 1) // MOE_PAIRS * MOE_PAIRS
    pad_end = jnp.cumsum(padded)
    pad_start = pad_end - padded
    strip_dst = (pad_start[None, :] + tile_base).astype(I32)
    strip_src = (tile_pair0[:, None] + off).astype(I32)
    n_used = (pad_end[-1] // MOE_PAIRS).astype(I32)
    blk_pair0 = jnp.minimum(jnp.arange(n_blocks, dtype=I32), n_used - 1) * MOE_PAIRS
    block_expert = jnp.minimum(
        jnp.sum((pad_end[None, :] <= blk_pair0[:, None]).astype(I32), axis=1), N_EXPERTS - 1)
    of_block = block_expert[:, None] == jnp.arange(N_EXPERTS, dtype=I32)[None, :]
    last_pair = jnp.sum(jnp.where(of_block, (pad_start + counts)[None, :], 0), axis=1)
    block_valid = jnp.clip(last_pair - blk_pair0, 0, MOE_PAIRS).astype(I32)
    sdst = strip_dst.T.reshape(-1)
    ssrc = strip_src.T.reshape(-1)
    slen = cnt.T.reshape(-1).astype(I32)
    s_lo = jnp.sum(((sdst + slen)[None, :] <= blk_pair0[:, None]).astype(I32), axis=1)
    s_hi = jnp.sum((sdst[None, :] < (blk_pair0 + MOE_PAIRS)[:, None]).astype(I32), axis=1)
    moe_tables = ((pad_start // MOE_PAIRS).astype(I32), (padded // MOE_PAIRS).astype(I32),
                  block_valid, n_used.reshape(1),
                  s_lo.astype(I32), s_hi.astype(I32), ssrc, sdst, slen)

    slots = n_blocks * MOE_PAIRS
    ys = _moe(moe_tables, xs, slots,
              w_gate_up[l], b_gate_up[l][:, None, :], w_down[l], b_down[l][:, None, :])
    gfin = row(g_final)

    def combine_tables(t0, t1):
        return (strip_dst[t0:t1].reshape(-1), cnt[t0:t1].reshape(-1).astype(I32),
                off[t0:t1].reshape(-1).astype(I32), jnp.sum(cnt[t0:t1], axis=1).astype(I32))

    y_p = _combine(combine_tables(0, ntp), x1_p, pos_p, gate_p, gfin, ys, PROMPT_TILE, ppt_p)
    y_s = _combine(combine_tables(ntp, ntp + nts), x1_s, pos_s, gate_s, gfin, ys, tm_s, ppt_s)

    return (y_p.reshape(bp, seq, D_MODEL),
            y_s.reshape(bs, dseq, D_MODEL),
            hist_p[:, 0, HIST_OFF:, :][None],
            mk_p.reshape(bp, N_MEM, XA_HEADS, XA_HD)[None],
            mv_p.reshape(bp, N_MEM, XA_HEADS, XA_HD)[None],
            hist_s.reshape(bs, HIST, C_CONV)[:, HIST_OFF:, :][None],
            v_s.reshape(bs, dseq, C_GMLP)[None])
```

```python
import functools

import numpy as np
import jax
import jax.numpy as jnp
from jax import lax
from jax.experimental import pallas as pl
from jax.experimental.pallas import tpu as pltpu

F32 = jnp.float32
BF16 = jnp.bfloat16
I32 = jnp.int32
U32 = jnp.uint32

D_MODEL = 1024
C_CONV = 384
CONV_W = 31
C_GMLP = 384
GMLP_HEADS = 4
GMLP_HD = 96
GMLP_CHUNK = 128
XA_HEADS = 4
XA_HD = 64
C_XA = 256
N_MEM = 256
N_EXPERTS = 32
TOP_K = 4
D_FF = 1024
SWIGLU_LIMIT = 7.0
SWIGLU_ALPHA = 1.702
EPS = 1e-5
IN_COLS = 2 * C_CONV + 2 * C_GMLP + C_XA

SUBLANES = 8
LANES = 128
ROW_TILES = D_MODEL // LANES
HIST = 32
HIST_OFF = HIST - (CONV_W - 1)

PROMPT_TILE = 512
SAMPLE_SEQS_PER_TILE = 8
MOE_BLOCK = 512
MOE_PAIRS = MOE_BLOCK // 2
GATHER_AHEAD = 2
SECOND_SOURCE = 1 << 30
VMEM_LIMIT_MIXER = 48 * 1024 * 1024
VMEM_LIMIT_MOE = 52 * 1024 * 1024


def _rmsnorm(x, g):
    return x * lax.rsqrt(jnp.mean(x * x, axis=-1, keepdims=True) + EPS) * g


def _layernorm(x, g, b):
    mu = jnp.mean(x, axis=-1, keepdims=True)
    xc = x - mu
    var = jnp.mean(xc * xc, axis=-1, keepdims=True)
    return xc * lax.rsqrt(var + EPS) * g + b


def _gelu(x):
    return 0.5 * x * (1.0 + lax.erf(x * np.float32(1.0 / np.sqrt(2.0))))


def _sigmoid(x):
    return 1.0 / (1.0 + jnp.exp(-x))


def _rows_to_tiles(dst_ref, val, rows, row0=0):
    for j in range(ROW_TILES):
        dst_ref[pl.ds(row0 * ROW_TILES + j, rows, stride=ROW_TILES), :] = val[:, j * LANES:(j + 1) * LANES]


def _tiles_to_rows(src_ref, rows, row0=0):
    return jnp.concatenate(
        [src_ref[pl.ds(row0 * ROW_TILES + j, rows, stride=ROW_TILES), :] for j in range(ROW_TILES)],
        axis=-1)


def _memkv_kernel(mem_ref, g_ref, w_ref, o_ref):
    mn = _rmsnorm(mem_ref[0], g_ref[...])
    o_ref[0] = jnp.dot(mn.astype(BF16), w_ref[...], preferred_element_type=F32)


def _memkv(mem, g_mem, w_kv):
    b = mem.shape[0]
    return pl.pallas_call(
        _memkv_kernel,
        grid=(b,),
        in_specs=[pl.BlockSpec((1, N_MEM, D_MODEL), lambda i: (i, 0, 0)),
                  pl.BlockSpec((1, D_MODEL), lambda i: (0, 0)),
                  pl.BlockSpec((D_MODEL, 2 * C_XA), lambda i: (0, 0))],
        out_specs=pl.BlockSpec((1, N_MEM, 2 * C_XA), lambda i: (i, 0, 0)),
        out_shape=jax.ShapeDtypeStruct((b, N_MEM, 2 * C_XA), F32),
        name="memkv",
    )(mem, g_mem, w_kv)


def _mixer_kernel(x_ref, hist_ref, mk_ref, mv_ref, gmix_ref, win_ref, wdw_ref, bdw_ref,
                  lcg_ref, lcb_ref, lvg_ref, lvb_ref, wsp_ref, bsp_ref, wout_ref, gffn_ref,
                  wrh_ref, wrl_ref, br_ref, *rest, ns, sl, carry, sp_chunk, tiles_per_seq, n_sorted):
    (x1_ref, xst_ref, pos_ref, gate_ref, cnt_ref, off_ref, histout_ref, v_ref,
     ext_ref, shift_ref, h2_keep, pos_keep) = rest[-12:]
    tm = ns * sl
    step = pl.program_id(0)
    n_tiles = pl.num_programs(0) - 1

    def sort_previous_tile():
        jj = lax.broadcasted_iota(I32, (n_sorted, tm), 0)
        perm = jnp.zeros((n_sorted, tm), F32)
        for k in range(TOP_K):
            perm = jnp.where(jj == pos_keep[k:k + 1, :], 1.0, perm)
        sorted_rows = jnp.dot(perm.astype(BF16), h2_keep[...], preferred_element_type=F32)
        pairs = pltpu.bitcast(sorted_rows.astype(BF16), U32)
        _rows_to_tiles(xst_ref, pairs, n_sorted // 2)

    @pl.when(step == 0)
    def _():
        h2_keep[...] = jnp.zeros_like(h2_keep)
        pos_keep[...] = jnp.full(pos_keep.shape, -1, I32)

    @pl.when(step < n_tiles)
    def _():
        sort_previous_tile()
        _mixer_tile(x_ref, hist_ref, mk_ref, mv_ref, gmix_ref, win_ref, wdw_ref, bdw_ref,
                    lcg_ref, lcb_ref, lvg_ref, lvb_ref, wsp_ref, bsp_ref, wout_ref, gffn_ref,
                    wrh_ref, wrl_ref, br_ref, x1_ref, pos_ref, gate_ref, cnt_ref, off_ref,
                    histout_ref, v_ref, ext_ref, shift_ref, h2_keep, pos_keep,
                    first_of_seq=(step % tiles_per_seq) == 0,
                    ns=ns, sl=sl, carry=carry, sp_chunk=sp_chunk)

    @pl.when(step == n_tiles)
    def _():
        sort_previous_tile()


def _mixer_tile(x_ref, hist_ref, mk_ref, mv_ref, gmix_ref, win_ref, wdw_ref, bdw_ref,
                lcg_ref, lcb_ref, lvg_ref, lvb_ref, wsp_ref, bsp_ref, wout_ref, gffn_ref,
                wrh_ref, wrl_ref, br_ref, x1_ref, pos_ref, gate_ref, cnt_ref, off_ref,
                histout_ref, v_ref, ext_ref, shift_ref, h2_keep, pos_keep,
                *, first_of_seq, ns, sl, carry, sp_chunk):
    tm = ns * sl
    x = x_ref[0]
    h = _rmsnorm(x, gmix_ref[...])
    z = jnp.dot(h.astype(BF16), win_ref[...], preferred_element_type=F32)
    z_a = z[:, 0:C_CONV]
    z_g = z[:, C_CONV:2 * C_CONV]
    z_u = z[:, 2 * C_CONV:2 * C_CONV + C_GMLP]
    z_v = z[:, 2 * C_CONV + C_GMLP:2 * C_CONV + 2 * C_GMLP]
    z_q = z[:, 2 * C_CONV + 2 * C_GMLP:IN_COLS]

    glu = z_a * _sigmoid(z_g)
    if carry:
        ext_ref[:, 0:HIST, :] = jnp.where(first_of_seq, hist_ref[0], ext_ref[:, 0:HIST, :])
    else:
        ext_ref[:, 0:HIST, :] = hist_ref[0]
    conv_parts = []
    for s in range(ns):
        ext_s = ext_ref.at[s]
        ext_s[HIST:HIST + sl, :] = glu[s * sl:(s + 1) * sl]
        n_shift = HIST + sl - SUBLANES
        for r in range(1, SUBLANES):
            shift_ref[s, r - 1, 0:n_shift, :] = ext_s[pl.ds(r, n_shift), :]
        rc = min(sl, 64)
        for r0 in range(0, sl, rc):
            acc = jnp.broadcast_to(bdw_ref[...], (rc, C_CONV))
            for j in range(CONV_W):
                a, r = divmod(j + HIST_OFF, SUBLANES)
                src = ext_s if r == 0 else shift_ref.at[s, r - 1]
                acc = acc + wdw_ref[j:j + 1, :] * src[pl.ds(r0 + a * SUBLANES, rc), :]
            conv_parts.append(acc)
        new_hist = ext_s[sl:sl + HIST, :]
        histout_ref[0, s] = new_hist
        if carry:
            ext_s[0:HIST, :] = new_hist
    y = jnp.concatenate(conv_parts, axis=0) if len(conv_parts) > 1 else conv_parts[0]
    y = _layernorm(y, lcg_ref[...], lcb_ref[...])
    c_out = y * _sigmoid(y)

    u = _gelu(z_u)
    v = _layernorm(_gelu(z_v), lvg_ref[...], lvb_ref[...])
    v_ref[...] = v
    vb = v.astype(BF16)
    rr = lax.broadcasted_iota(I32, (GMLP_CHUNK, GMLP_CHUNK), 0)
    cc = lax.broadcasted_iota(I32, (GMLP_CHUNK, GMLP_CHUNK), 1)
    sp_mask = (cc <= rr) & ((rr // sp_chunk) == (cc // sp_chunk))
    col = lax.broadcasted_iota(I32, (GMLP_CHUNK, C_GMLP), 1)
    w_heads = [jnp.where(sp_mask, wsp_ref[hh], 0.0).astype(BF16) for hh in range(GMLP_HEADS)]
    g_parts = []
    for c in range(tm // GMLP_CHUNK):
        vc = vb[c * GMLP_CHUNK:(c + 1) * GMLP_CHUNK]
        sg = bsp_ref[...]
        for hh in range(GMLP_HEADS):
            head_cols = (col >= hh * GMLP_HD) & (col < (hh + 1) * GMLP_HD)
            vh = jnp.where(head_cols, vc, jnp.zeros_like(vc))
            sg = sg + jnp.dot(w_heads[hh], vh, preferred_element_type=F32)
        g_parts.append(u[c * GMLP_CHUNK:(c + 1) * GMLP_CHUNK] * sg)
    g_out = jnp.concatenate(g_parts, axis=0) if len(g_parts) > 1 else g_parts[0]

    qs = z_q * np.float32(XA_HD ** -0.5)
    qcol = lax.broadcasted_iota(I32, (sl, C_XA), 1)
    a_parts = []
    for s in range(ns):
        q_s = qs[s * sl:(s + 1) * sl]
        kb = mk_ref[0, s].astype(BF16)
        vvb = mv_ref[0, s].astype(BF16)
        hmasks = [(qcol >= hh * XA_HD) & (qcol < (hh + 1) * XA_HD) for hh in range(XA_HEADS)]
        stack = XA_HEADS if sl * XA_HEADS <= N_MEM else 1
        a_s = jnp.zeros((sl, C_XA), F32)
        for h0 in range(0, XA_HEADS, stack):
            heads = range(h0, h0 + stack)
            qh = jnp.concatenate([jnp.where(hmasks[hh], q_s, 0.0) for hh in heads], axis=0).astype(BF16)
            sc = lax.dot_general(qh, kb, (((1,), (1,)), ((), ())), preferred_element_type=F32)
            p = jnp.exp(sc - jnp.max(sc, axis=-1, keepdims=True))
            den = jnp.sum(p, axis=-1, keepdims=True)
            oh = jnp.dot(p.astype(BF16), vvb, preferred_element_type=F32) / den
            for n, hh in enumerate(heads):
                a_s = a_s + jnp.where(hmasks[hh], oh[n * sl:(n + 1) * sl], 0.0)
        a_parts.append(a_s)
    a_out = jnp.concatenate(a_parts, axis=0) if len(a_parts) > 1 else a_parts[0]

    mix = jnp.concatenate([c_out, g_out, a_out], axis=-1).astype(BF16)
    x1 = x + jnp.dot(mix, wout_ref[...], preferred_element_type=F32)
    x1_ref[...] = x1

    h2 = _rmsnorm(x1, gffn_ref[...])
    h2_hi = h2.astype(BF16)
    h2_lo = (h2 - h2_hi.astype(F32)).astype(BF16)
    nt_dims = (((1,), (1,)), ((), ()))
    lg = (lax.dot_general(wrh_ref[...], h2_hi, nt_dims, preferred_element_type=F32)
          + lax.dot_general(wrl_ref[...], h2_hi, nt_dims, preferred_element_type=F32)
          + lax.dot_general(wrh_ref[...], h2_lo, nt_dims, preferred_element_type=F32)
          + br_ref[...])
    eio = lax.broadcasted_iota(I32, (N_EXPERTS, tm), 0)
    work = lg
    vals, idxs = [], []
    for _ in range(TOP_K):
        m = jnp.max(work, axis=0, keepdims=True)
        idx = jnp.min(jnp.where(work == m, eio, N_EXPERTS), axis=0, keepdims=True)
        vals.append(m)
        idxs.append(idx)
        work = jnp.where(eio == idx, -jnp.inf, work)
    exps = [jnp.exp(vk - vals[0]) for vk in vals]
    den = exps[0] + exps[1] + exps[2] + exps[3]
    gate_ref[...] = jnp.concatenate([ek / den for ek in exps], axis=0)

    sel = jnp.zeros((N_EXPERTS, tm), F32)
    for idx in idxs:
        sel = sel + jnp.where(eio == idx, 1.0, 0.0)
    selb = sel.astype(BF16)
    tr = lax.broadcasted_iota(I32, (tm, tm), 0)
    tc = lax.broadcasted_iota(I32, (tm, tm), 1)
    before = jnp.where(tr < tc, 1.0, 0.0).astype(BF16)
    ranks = jnp.dot(selb, before, preferred_element_type=F32)
    er = lax.broadcasted_iota(I32, (N_EXPERTS, N_EXPERTS), 0)
    ec = lax.broadcasted_iota(I32, (N_EXPERTS, N_EXPERTS), 1)
    lower = jnp.where(ec < er, 1.0, 0.0).astype(BF16)
    cnt = jnp.sum(sel, axis=1, keepdims=True)
    half_len = jnp.floor((cnt + 1.0) * 0.5)
    off_pairs = jnp.dot(lower, jnp.broadcast_to(half_len, (N_EXPERTS, LANES)).astype(BF16),
                        preferred_element_type=F32)[:, 0:1]
    cnt_ref[0] = jnp.broadcast_to(half_len, (N_EXPERTS, LANES)).astype(I32)
    off_ref[0] = jnp.broadcast_to(off_pairs, (N_EXPERTS, LANES)).astype(I32)
    slot_of = 2.0 * off_pairs + ranks
    pos = [jnp.sum(jnp.where(eio == idx, slot_of, 0.0), axis=0, keepdims=True).astype(I32) for idx in idxs]
    pos_all = jnp.concatenate(pos, axis=0)
    pos_ref[...] = pos_all
    pos_keep[...] = pos_all
    h2_keep[...] = h2_hi


def _mixer(x, hist, mem_k, mem_v, wts, *, ns, sl, carry, sp_chunk, pairs_per_tile):
    g, r, _ = x.shape
    tm = ns * sl
    nt = r // tm
    ntot = g * nt
    assert 2 * pairs_per_tile >= TOP_K * tm + N_EXPERTS
    tile = lambda s: jnp.minimum(s, ntot - 1)
    const2 = lambda s: (0, 0)
    const3 = lambda s: (0, 0, 0)
    tile_row = lambda s: (tile(s), 0)
    tile_lane = lambda s: (0, tile(s))
    per_seq = lambda s: (tile(s) // nt, 0, 0, 0)
    in_specs = [
        pl.BlockSpec((1, tm, D_MODEL), lambda s: (tile(s) // nt, tile(s) % nt, 0)),
        pl.BlockSpec((1, ns, HIST, C_CONV), per_seq),
        pl.BlockSpec((1, ns, N_MEM, C_XA), per_seq),
        pl.BlockSpec((1, ns, N_MEM, C_XA), per_seq),
        pl.BlockSpec((1, D_MODEL), const2),
        pl.BlockSpec((D_MODEL, IN_COLS), const2),
        pl.BlockSpec((HIST, C_CONV), const2),
        pl.BlockSpec((1, C_CONV), const2),
        pl.BlockSpec((1, C_CONV), const2),
        pl.BlockSpec((1, C_CONV), const2),
        pl.BlockSpec((1, C_GMLP), const2),
        pl.BlockSpec((1, C_GMLP), const2),
        pl.BlockSpec((GMLP_HEADS, GMLP_CHUNK, GMLP_CHUNK), const3),
        pl.BlockSpec((GMLP_CHUNK, C_GMLP), const2),
        pl.BlockSpec((D_MODEL, D_MODEL), const2),
        pl.BlockSpec((1, D_MODEL), const2),
        pl.BlockSpec((N_EXPERTS, D_MODEL), const2),
        pl.BlockSpec((N_EXPERTS, D_MODEL), const2),
        pl.BlockSpec((N_EXPERTS, 1), const2),
    ]
    tile_cnt = lambda s: (tile(s), 0, 0)
    out_specs = [
        pl.BlockSpec((tm, D_MODEL), tile_row),
        pl.BlockSpec((pairs_per_tile * ROW_TILES, LANES),
                     lambda s: (jnp.maximum(s - 1, 0), 0)),
        pl.BlockSpec((TOP_K, tm), tile_lane),
        pl.BlockSpec((TOP_K, tm), tile_lane),
        pl.BlockSpec((1, N_EXPERTS, LANES), tile_cnt),
        pl.BlockSpec((1, N_EXPERTS, LANES), tile_cnt),
        pl.BlockSpec((1, ns, HIST, C_CONV), per_seq),
        pl.BlockSpec((tm, C_GMLP), tile_row),
    ]
    rows = g * r
    out_shape = [
        jax.ShapeDtypeStruct((rows, D_MODEL), F32),
        jax.ShapeDtypeStruct((ntot * pairs_per_tile * ROW_TILES, LANES), U32),
        jax.ShapeDtypeStruct((TOP_K, rows), I32),
        jax.ShapeDtypeStruct((TOP_K, rows), F32),
        jax.ShapeDtypeStruct((ntot, N_EXPERTS, LANES), I32),
        jax.ShapeDtypeStruct((ntot, N_EXPERTS, LANES), I32),
        jax.ShapeDtypeStruct((g, ns, HIST, C_CONV), F32),
        jax.ShapeDtypeStruct((rows, C_GMLP), F32),
    ]
    kern = functools.partial(_mixer_kernel, ns=ns, sl=sl, carry=carry, sp_chunk=sp_chunk,
                             tiles_per_seq=nt, n_sorted=2 * pairs_per_tile)
    return pl.pallas_call(
        kern,
        grid=(ntot + 1,),
        in_specs=in_specs,
        out_specs=out_specs,
        out_shape=out_shape,
        scratch_shapes=[pltpu.VMEM((ns, HIST + sl, C_CONV), F32),
                        pltpu.VMEM((ns, SUBLANES - 1, HIST + sl, C_CONV), F32),
                        pltpu.VMEM((tm, D_MODEL), BF16),
                        pltpu.VMEM((TOP_K, tm), I32)],
        compiler_params=pltpu.CompilerParams(
            dimension_semantics=("arbitrary",),
            vmem_limit_bytes=VMEM_LIMIT_MIXER),
        name="mixer_carry" if carry else "mixer_cache",
    )(x, hist, mem_k, mem_v, *wts)


def _strip_pieces(n, max_rows, fn):
    done = 0
    p = max_rows
    while p >= 1:
        has = (n & p) != 0
        pl.when(has)(functools.partial(fn, done, p))
        done = done + jnp.where(has, p, 0)
        p //= 2


class _Table:
    def __init__(self, ref, offset):
        self.ref, self.offset = ref, offset

    def __getitem__(self, i):
        return self.ref[self.offset + i]


def _pack_tables(tables):
    offsets = tuple(int(o) for o in np.cumsum([0] + [t.shape[0] for t in tables[:-1]]))
    return jnp.concatenate([t.astype(I32) for t in tables]), offsets


def _pow2_at_most(n):
    return 1 << (n.bit_length() - 1)


def _rows(ref, first_row, rows):
    return ref.at[pl.ds(pl.multiple_of(first_row * ROW_TILES, ROW_TILES), rows * ROW_TILES)]


def _moe_kernel(tab_ref, xs_hbm, xs2_hbm, wgu_ref, bgu_ref, wdn_ref, bdn_ref, ys_hbm,
                xbuf, ybuf, wgu16, wdn16, sems, ysems, *, offsets, n_blocks):
    (b0_ref, nbk_ref, bv_ref, nb_ref, slo_ref, shi_ref, ssrc_ref, sdst_ref,
     slen_ref) = [_Table(tab_ref, o) for o in offsets]
    e = pl.program_id(0)
    n_used = nb_ref[0]
    half = MOE_PAIRS // 2

    def gather_start(b, slot):
        pair0 = b * MOE_PAIRS

        def strip(s, carry):
            lo = jnp.maximum(sdst_ref[s], pair0)
            hi = jnp.minimum(sdst_ref[s] + slen_ref[s], pair0 + MOE_PAIRS)
            second = ssrc_ref[s] >= SECOND_SOURCE
            src0 = (ssrc_ref[s] & (SECOND_SOURCE - 1)) + (lo - sdst_ref[s])
            dst0 = lo - pair0

            def piece(first, rows):
                for src_hbm, use in ((xs_hbm, jnp.logical_not(second)), (xs2_hbm, second)):
                    @pl.when(use)
                    def _():
                        pltpu.make_async_copy(_rows(src_hbm, src0 + first, rows),
                                              _rows(xbuf.at[slot], dst0 + first, rows), sems.at[slot]).start()
            _strip_pieces(hi - lo, MOE_PAIRS, piece)
            return carry
        lax.fori_loop(slo_ref[b], shi_ref[b], strip, 0)

    def gather_wait(b, slot):
        def piece(first, rows):
            del first
            pltpu.make_async_copy(_rows(xs_hbm, 0, rows), _rows(xbuf.at[slot], 0, rows), sems.at[slot]).wait()
        _strip_pieces(bv_ref[b], MOE_PAIRS, piece)

    def out_copy(b, slot):
        return pltpu.make_async_copy(ybuf.at[slot], _rows(ys_hbm, b * MOE_PAIRS, MOE_PAIRS), ysems.at[slot])

    def expert_mlp(xslot, slot, h):
        xb = pltpu.bitcast(_tiles_to_rows(xbuf.at[xslot], half, h * half), BF16)
        gu = jnp.dot(xb, wgu16[...], preferred_element_type=F32) + bgu_ref[0]
        gate = jnp.minimum(gu[:, :D_FF], SWIGLU_LIMIT)
        up = jnp.clip(gu[:, D_FF:], -SWIGLU_LIMIT, SWIGLU_LIMIT)
        act = (up + 1.0) * (gate * _sigmoid(SWIGLU_ALPHA * gate))
        yb = jnp.dot(act.astype(BF16), wdn16[...], preferred_element_type=F32) + bdn_ref[0]
        _rows_to_tiles(ybuf.at[slot], pltpu.bitcast(yb.astype(BF16), U32), half, h * half)

    @pl.when(e == 0)
    def _():
        xbuf[...] = jnp.zeros_like(xbuf)
        for a in range(GATHER_AHEAD):
            @pl.when(a < n_used)
            def _():
                gather_start(a, a)

    @pl.when(nbk_ref[e] > 0)
    def _():
        def cast_rows(c, carry):
            r = pl.multiple_of(c * LANES, LANES)
            wgu16[pl.ds(r, LANES), :] = wgu_ref[pl.ds(r, LANES), :].astype(BF16)
            wdn16[pl.ds(r, LANES), :] = wdn_ref[pl.ds(r, LANES), :].astype(BF16)
            return carry
        lax.fori_loop(0, D_MODEL // LANES, cast_rows, 0)

        def block(j, carry):
            b = b0_ref[e] + j
            slot = b % 2
            xslot = b % (GATHER_AHEAD + 1)

            @pl.when(b + GATHER_AHEAD < n_used)
            def _():
                gather_start(b + GATHER_AHEAD, (b + GATHER_AHEAD) % (GATHER_AHEAD + 1))

            gather_wait(b, xslot)

            @pl.when(b >= 2)
            def _():
                out_copy(b - 2, slot).wait()

            valid = bv_ref[b]
            expert_mlp(xslot, slot, 0)

            @pl.when(valid > half)
            def _():
                expert_mlp(xslot, slot, 1)

            @pl.when(valid <= half)
            def _():
                ybuf[slot, pl.ds(half * ROW_TILES, half * ROW_TILES), :] = jnp.zeros(
                    (half * ROW_TILES, LANES), U32)

            out_copy(b, slot).start()
            return carry
        lax.fori_loop(0, nbk_ref[e], block, 0)

    @pl.when(e == pl.num_programs(0) - 1)
    def _():
        for back in (1, 2):
            @pl.when(n_used >= back)
            def _():
                out_copy(n_used - back, (n_used - back) % 2).wait()
        ybuf[0] = jnp.zeros(ybuf.shape[1:], U32)

        def start_fill(b, carry):
            out_copy(b, 0).start()
            return carry

        def wait_fill(b, carry):
            out_copy(b, 0).wait()
            return carry
        lax.fori_loop(n_used, n_blocks, start_fill, 0)
        lax.fori_loop(n_used, n_blocks, wait_fill, 0)


def _moe(tables, xs, xs2, n_blocks, w_gu, b_gu, w_dn, b_dn):
    wsel = lambda e, *_: (e, 0, 0)
    packed, offsets = _pack_tables(tables)
    slots = n_blocks * MOE_PAIRS
    grid_spec = pltpu.PrefetchScalarGridSpec(
        num_scalar_prefetch=1,
        grid=(N_EXPERTS,),
        in_specs=[pl.BlockSpec(memory_space=pl.ANY),
                  pl.BlockSpec(memory_space=pl.ANY),
                  pl.BlockSpec((None, D_MODEL, 2 * D_FF), wsel),
                  pl.BlockSpec((None, 1, 2 * D_FF), wsel),
                  pl.BlockSpec((None, D_FF, D_MODEL), wsel),
                  pl.BlockSpec((None, 1, D_MODEL), wsel)],
        out_specs=pl.BlockSpec(memory_space=pl.ANY),
        scratch_shapes=[pltpu.VMEM((GATHER_AHEAD + 1, MOE_PAIRS * ROW_TILES, LANES), U32),
                        pltpu.VMEM((2, MOE_PAIRS * ROW_TILES, LANES), U32),
                        pltpu.VMEM((D_MODEL, 2 * D_FF), BF16),
                        pltpu.VMEM((D_FF, D_MODEL), BF16),
                        pltpu.SemaphoreType.DMA((GATHER_AHEAD + 1,)),
                        pltpu.SemaphoreType.DMA((2,))],
    )
    return pl.pallas_call(
        functools.partial(_moe_kernel, offsets=offsets, n_blocks=n_blocks),
        grid_spec=grid_spec,
        out_shape=jax.ShapeDtypeStruct((slots * ROW_TILES, LANES), U32),
        compiler_params=pltpu.CompilerParams(
            dimension_semantics=("arbitrary",),
            vmem_limit_bytes=VMEM_LIMIT_MOE),
        name="moe",
    )(packed, xs, xs2, w_gu, b_gu, w_dn, b_dn)


def _combine_kernel(tab_ref, x1_ref, pos_ref, gate_ref, gfin_ref, ys_hbm,
                    out_ref, ybuf, sems, *, tm, pairs_per_tile, offsets):
    csrc_ref, clen_ref, coff_ref, ctot_ref = [_Table(tab_ref, o) for o in offsets]
    i = pl.program_id(0)
    n_tiles = pl.num_programs(0)
    n_sorted = 2 * pairs_per_tile

    def gather_start(t, slot):
        def strip(e, carry):
            s = t * N_EXPERTS + e
            src0 = csrc_ref[s]
            dst0 = coff_ref[s]

            def piece(first, rows):
                pltpu.make_async_copy(_rows(ys_hbm, src0 + first, rows),
                                      _rows(ybuf.at[slot], dst0 + first, rows), sems.at[slot]).start()
            _strip_pieces(clen_ref[s], tm // 2, piece)
            return carry
        lax.fori_loop(0, N_EXPERTS, strip, 0)

    @pl.when(i == 0)
    def _():
        ybuf[...] = jnp.zeros_like(ybuf)
        gather_start(0, 0)

    @pl.when(i + 1 < n_tiles)
    def _():
        gather_start(i + 1, (i + 1) % 2)

    slot = i % 2

    def wait_piece(first, rows):
        del first
        pltpu.make_async_copy(_rows(ys_hbm, 0, rows), _rows(ybuf.at[slot], 0, rows), sems.at[slot]).wait()
    _strip_pieces(ctot_ref[i], _pow2_at_most(pairs_per_tile), wait_piece)

    y_sorted = pltpu.bitcast(_tiles_to_rows(ybuf.at[slot], pairs_per_tile), BF16)

    rr = lax.broadcasted_iota(I32, (tm, tm), 0)
    cc = lax.broadcasted_iota(I32, (tm, tm), 1)
    eye = rr == cc
    jl = lax.broadcasted_iota(I32, (tm, n_sorted), 1).astype(F32)
    unsort = jnp.zeros((tm, n_sorted), F32)
    for k in range(TOP_K):
        p_col = jnp.sum(jnp.where(eye, pos_ref[k:k + 1, :].astype(F32), 0.0), axis=1, keepdims=True)
        g_col = jnp.sum(jnp.where(eye, gate_ref[k:k + 1, :], 0.0), axis=1, keepdims=True)
        unsort = jnp.where(jl == p_col, g_col, unsort)
    acc = x1_ref[...] + jnp.dot(unsort.astype(BF16), y_sorted, preferred_element_type=F32)
    out_ref[...] = _rmsnorm(acc, gfin_ref[...])


def _combine(tables, x1, pos, gates, g_final, ys, tm, pairs_per_tile):
    t = x1.shape[0]
    nt = t // tm
    packed, offsets = _pack_tables(tables)
    grid_spec = pltpu.PrefetchScalarGridSpec(
        num_scalar_prefetch=1,
        grid=(nt,),
        in_specs=[pl.BlockSpec((tm, D_MODEL), lambda i, *_: (i, 0)),
                  pl.BlockSpec((TOP_K, tm), lambda i, *_: (0, i)),
                  pl.BlockSpec((TOP_K, tm), lambda i, *_: (0, i)),
                  pl.BlockSpec((1, D_MODEL), lambda i, *_: (0, 0)),
                  pl.BlockSpec(memory_space=pl.ANY)],
        out_specs=pl.BlockSpec((tm, D_MODEL), lambda i, *_: (i, 0)),
        scratch_shapes=[pltpu.VMEM((2, pairs_per_tile * ROW_TILES, LANES), U32),
                        pltpu.SemaphoreType.DMA((2,))],
    )
    return pl.pallas_call(
        functools.partial(_combine_kernel, tm=tm, pairs_per_tile=pairs_per_tile, offsets=offsets),
        grid_spec=grid_spec,
        out_shape=jax.ShapeDtypeStruct((t, D_MODEL), F32),
        compiler_params=pltpu.CompilerParams(
            dimension_semantics=("arbitrary",),
            vmem_limit_bytes=VMEM_LIMIT_MIXER),
        name="combine",
    )(packed, x1, pos, gates, g_final, ys)


def _split_bf16(w):
    hi = w.astype(BF16)
    lo = (w - hi.astype(F32)).astype(BF16)
    return hi, lo


def kernel(x_prompt, x_sample, cache_conv, cache_mem_k, cache_mem_v, mem_prompt, g_mix, w_in, w_dw, b_dw, ln_conv_g, ln_conv_b, ln_v_g, ln_v_b, w_spatial, b_spatial, g_mem, w_mem_k, w_mem_v, w_out, g_ffn, w_router, b_router, w_gate_up, b_gate_up, w_down, b_down, g_final):
    depth = g_mix.shape[0]
    assert depth == 1
    l = 0
    bp, seq, _ = x_prompt.shape
    bs, dseq, _ = x_sample.shape
    assert seq % PROMPT_TILE == 0 and bs % SAMPLE_SEQS_PER_TILE == 0
    assert GMLP_CHUNK % dseq == 0 and (SAMPLE_SEQS_PER_TILE * dseq) % GMLP_CHUNK == 0

    row = lambda a: a.reshape(1, -1)
    wr_hi, wr_lo = _split_bf16(w_router[l].T)
    w_dw_pad = jnp.pad(w_dw[l], ((0, HIST - CONV_W), (0, 0)))
    bias_rows = lambda b: jnp.repeat(b.T, GMLP_HD, axis=1)
    common = dict(
        gmix=row(g_mix[l]), win=w_in[l].astype(BF16), wdw=w_dw_pad, bdw=row(b_dw[l]),
        lcg=row(ln_conv_g[l]), lcb=row(ln_conv_b[l]), lvg=row(ln_v_g[l]), lvb=row(ln_v_b[l]),
        wout=w_out[l].astype(BF16), gffn=row(g_ffn[l]), wrh=wr_hi, wrl=wr_lo,
        br=b_router[l].reshape(N_EXPERTS, 1))

    def weights(wsp, bsp):
        c = common
        return (c["gmix"], c["win"], c["wdw"], c["bdw"], c["lcg"], c["lcb"], c["lvg"], c["lvb"],
                wsp, bsp, c["wout"], c["gffn"], c["wrh"], c["wrl"], c["br"])

    reps = GMLP_CHUNK // dseq
    wts_p = weights(w_spatial[l], bias_rows(b_spatial[l]))
    wts_s = weights(jnp.tile(w_spatial[l][:, :dseq, :dseq], (1, reps, reps)),
                    bias_rows(jnp.tile(b_spatial[l][:, :dseq], (1, reps))))

    w_kv = jnp.concatenate([w_mem_k[l], w_mem_v[l]], axis=1).astype(BF16)
    kv_p = _memkv(mem_prompt, row(g_mem[l]), w_kv)
    mk_p = kv_p[:, :, :C_XA]
    mv_p = kv_p[:, :, C_XA:]
    zero_hist = jnp.zeros((bp, 1, HIST, C_CONV), F32)
    tp, ts = bp * seq, bs * dseq
    tm_s = SAMPLE_SEQS_PER_TILE * dseq
    ntp, nts = tp // PROMPT_TILE, ts // tm_s
    ppt_p = (TOP_K * PROMPT_TILE + N_EXPERTS + 1) // 2
    ppt_s = (TOP_K * tm_s + N_EXPERTS + 1) // 2
    (x1_p, xs_p, pos_p, gate_p, cnt_p, off_p, hist_p, _) = _mixer(
        x_prompt, zero_hist, mk_p[:, None], mv_p[:, None], wts_p,
        ns=1, sl=PROMPT_TILE, carry=True, sp_chunk=GMLP_CHUNK, pairs_per_tile=ppt_p)

    gs = bs // SAMPLE_SEQS_PER_TILE
    hist_s_in = jnp.pad(cache_conv[l], ((0, 0), (HIST_OFF, 0), (0, 0))).reshape(
        gs, SAMPLE_SEQS_PER_TILE, HIST, C_CONV)
    mk_s = cache_mem_k[l].reshape(gs, SAMPLE_SEQS_PER_TILE, N_MEM, C_XA)
    mv_s = cache_mem_v[l].reshape(gs, SAMPLE_SEQS_PER_TILE, N_MEM, C_XA)
    (x1_s, xs_s, pos_s, gate_s, cnt_s, off_s, hist_s, v_s) = _mixer(
        x_sample.reshape(gs, tm_s, D_MODEL), hist_s_in, mk_s, mv_s, wts_s,
        ns=SAMPLE_SEQS_PER_TILE, sl=dseq, carry=False, sp_chunk=dseq, pairs_per_tile=ppt_s)

    max_pairs = ((tp + ts) * TOP_K + (ntp + nts) * N_EXPERTS) // 2
    n_blocks = -(-max_pairs // MOE_PAIRS) + N_EXPERTS
    cnt = jnp.concatenate([cnt_p[:, :, 0], cnt_s[:, :, 0]], axis=0)
    off = jnp.concatenate([off_p[:, :, 0], off_s[:, :, 0]], axis=0)
    tile_pair0 = np.concatenate([np.arange(ntp) * ppt_p,
                                 SECOND_SOURCE + np.arange(nts) * ppt_s]).astype(np.int32)
    counts = jnp.sum(cnt, axis=0)
    tile_base = jnp.cumsum(cnt, axis=0) - cnt
    padded = (counts + MOE_PAIRS - 1) // MOE_PAIRS * MOE_PAIRS
    pad_end = jnp.cumsum(padded)
    pad_start = pad_end - padded
    strip_dst = (pad_start[None, :] + tile_base).astype(I32)
    strip_src = (tile_pair0[:, None] + off).astype(I32)
    n_used = (pad_end[-1] // MOE_PAIRS).astype(I32)
    blk_pair0 = jnp.minimum(jnp.arange(n_blocks, dtype=I32), n_used - 1) * MOE_PAIRS
    block_expert = jnp.minimum(
        jnp.sum((pad_end[None, :] <= blk_pair0[:, None]).astype(I32), axis=1), N_EXPERTS - 1)
    of_block = block_expert[:, None] == jnp.arange(N_EXPERTS, dtype=I32)[None, :]
    last_pair = jnp.sum(jnp.where(of_block, (pad_start + counts)[None, :], 0), axis=1)
    block_valid = jnp.clip(last_pair - blk_pair0, 0, MOE_PAIRS).astype(I32)
    sdst = strip_dst.T.reshape(-1)
    ssrc = strip_src.T.reshape(-1)
    slen = cnt.T.reshape(-1).astype(I32)
    s_lo = jnp.sum(((sdst + slen)[None, :] <= blk_pair0[:, None]).astype(I32), axis=1)
    s_hi = jnp.sum((sdst[None, :] < (blk_pair0 + MOE_PAIRS)[:, None]).astype(I32), axis=1)
    moe_tables = ((pad_start // MOE_PAIRS).astype(I32), (padded // MOE_PAIRS).astype(I32),
                  block_valid, n_used.reshape(1),
                  s_lo.astype(I32), s_hi.astype(I32), ssrc, sdst, slen)

    ys = _moe(moe_tables, xs_p, xs_s, n_blocks,
              w_gate_up[l], b_gate_up[l][:, None, :], w_down[l], b_down[l][:, None, :])
    gfin = row(g_final)

    def combine_tables(t0, t1):
        return (strip_dst[t0:t1].reshape(-1), cnt[t0:t1].reshape(-1).astype(I32),
                off[t0:t1].reshape(-1).astype(I32), jnp.sum(cnt[t0:t1], axis=1).astype(I32))

    y_p = _combine(combine_tables(0, ntp), x1_p, pos_p, gate_p, gfin, ys, PROMPT_TILE, ppt_p)
    y_s = _combine(combine_tables(ntp, ntp + nts), x1_s, pos_s, gate_s, gfin, ys, tm_s, ppt_s)

    return (y_p.reshape(bp, seq, D_MODEL),
            y_s.reshape(bs, dseq, D_MODEL),
            hist_p[:, 0, HIST_OFF:, :][None],
            mk_p.reshape(bp, N_MEM, XA_HEADS, XA_HD)[None],
            mv_p.reshape(bp, N_MEM, XA_HEADS, XA_HD)[None],
            hist_s.reshape(bs, HIST, C_CONV)[:, HIST_OFF:, :][None],
            v_s.reshape(bs, dseq, C_GMLP)[None])
```

```python
import functools

import numpy as np
import jax
import jax.numpy as jnp
from jax import lax
from jax.experimental import pallas as pl
from jax.experimental.pallas import tpu as pltpu

F32 = jnp.float32
BF16 = jnp.bfloat16
I32 = jnp.int32
U32 = jnp.uint32

D_MODEL = 1024
C_CONV = 384
CONV_W = 31
C_GMLP = 384
GMLP_HEADS = 4
GMLP_HD = 96
GMLP_CHUNK = 128
XA_HEADS = 4
XA_HD = 64
C_XA = 256
N_MEM = 256
N_EXPERTS = 32
TOP_K = 4
D_FF = 1024
SWIGLU_LIMIT = 7.0
SWIGLU_ALPHA = 1.702
EPS = 1e-5
IN_COLS = 2 * C_CONV + 2 * C_GMLP + C_XA

SUBLANES = 8
LANES = 128
ROW_TILES = D_MODEL // LANES
HIST = 32
HIST_OFF = HIST - (CONV_W - 1)

PROMPT_TILE = 512
SAMPLE_SEQS_PER_TILE = 8
MOE_BLOCK = 512
MOE_PAIRS = MOE_BLOCK // 2
GATHER_AHEAD = 2
VMEM_LIMIT_MIXER = 48 * 1024 * 1024
VMEM_LIMIT_MOE = 52 * 1024 * 1024


def _rmsnorm(x, g):
    return x * lax.rsqrt(jnp.mean(x * x, axis=-1, keepdims=True) + EPS) * g


def _layernorm(x, g, b):
    mu = jnp.mean(x, axis=-1, keepdims=True)
    xc = x - mu
    var = jnp.mean(xc * xc, axis=-1, keepdims=True)
    return xc * lax.rsqrt(var + EPS) * g + b


def _gelu(x):
    return 0.5 * x * (1.0 + lax.erf(x * np.float32(1.0 / np.sqrt(2.0))))


def _sigmoid(x):
    return 1.0 / (1.0 + jnp.exp(-x))


def _rows_to_tiles(dst_ref, val, rows, row0=0):
    for j in range(ROW_TILES):
        dst_ref[pl.ds(row0 * ROW_TILES + j, rows, stride=ROW_TILES), :] = val[:, j * LANES:(j + 1) * LANES]


def _tiles_to_rows(src_ref, rows, row0=0):
    return jnp.concatenate(
        [src_ref[pl.ds(row0 * ROW_TILES + j, rows, stride=ROW_TILES), :] for j in range(ROW_TILES)],
        axis=-1)


def _memkv_kernel(mem_ref, g_ref, w_ref, o_ref):
    mn = _rmsnorm(mem_ref[0], g_ref[...])
    o_ref[0] = jnp.dot(mn.astype(BF16), w_ref[...], preferred_element_type=F32)


def _memkv(mem, g_mem, w_kv):
    b = mem.shape[0]
    return pl.pallas_call(
        _memkv_kernel,
        grid=(b,),
        in_specs=[pl.BlockSpec((1, N_MEM, D_MODEL), lambda i: (i, 0, 0)),
                  pl.BlockSpec((1, D_MODEL), lambda i: (0, 0)),
                  pl.BlockSpec((D_MODEL, 2 * C_XA), lambda i: (0, 0))],
        out_specs=pl.BlockSpec((1, N_MEM, 2 * C_XA), lambda i: (i, 0, 0)),
        out_shape=jax.ShapeDtypeStruct((b, N_MEM, 2 * C_XA), F32),
        name="memkv",
    )(mem, g_mem, w_kv)


def _mixer_kernel(x_ref, hist_ref, mk_ref, mv_ref, gmix_ref, win_ref, wdw_ref, bdw_ref,
                  lcg_ref, lcb_ref, lvg_ref, lvb_ref, wsp_ref, bsp_ref, wout_ref, gffn_ref,
                  wrh_ref, wrl_ref, br_ref, *rest, ns, sl, carry, sp_chunk, tiles_per_seq, n_sorted):
    (x1_ref, xst_ref, pos_ref, gate_ref, cnt_ref, off_ref, histout_ref, v_ref,
     ext_ref, shift_ref, h2_keep, pos_keep) = rest[-12:]
    tm = ns * sl
    step = pl.program_id(0)
    n_tiles = pl.num_programs(0) - 1

    def sort_previous_tile():
        jj = lax.broadcasted_iota(I32, (n_sorted, tm), 0)
        perm = jnp.zeros((n_sorted, tm), F32)
        for k in range(TOP_K):
            perm = jnp.where(jj == pos_keep[k:k + 1, :], 1.0, perm)
        sorted_rows = jnp.dot(perm.astype(BF16), h2_keep[...], preferred_element_type=F32)
        pairs = pltpu.bitcast(sorted_rows.astype(BF16), U32)
        _rows_to_tiles(xst_ref, pairs, n_sorted // 2)

    @pl.when(step == 0)
    def _():
        h2_keep[...] = jnp.zeros_like(h2_keep)
        pos_keep[...] = jnp.full(pos_keep.shape, -1, I32)

    @pl.when(step < n_tiles)
    def _():
        sort_previous_tile()
        _mixer_tile(x_ref, hist_ref, mk_ref, mv_ref, gmix_ref, win_ref, wdw_ref, bdw_ref,
                    lcg_ref, lcb_ref, lvg_ref, lvb_ref, wsp_ref, bsp_ref, wout_ref, gffn_ref,
                    wrh_ref, wrl_ref, br_ref, x1_ref, pos_ref, gate_ref, cnt_ref, off_ref,
                    histout_ref, v_ref, ext_ref, shift_ref, h2_keep, pos_keep,
                    first_of_seq=(step % tiles_per_seq) == 0,
                    ns=ns, sl=sl, carry=carry, sp_chunk=sp_chunk)

    @pl.when(step == n_tiles)
    def _():
        sort_previous_tile()


def _mixer_tile(x_ref, hist_ref, mk_ref, mv_ref, gmix_ref, win_ref, wdw_ref, bdw_ref,
                lcg_ref, lcb_ref, lvg_ref, lvb_ref, wsp_ref, bsp_ref, wout_ref, gffn_ref,
                wrh_ref, wrl_ref, br_ref, x1_ref, pos_ref, gate_ref, cnt_ref, off_ref,
                histout_ref, v_ref, ext_ref, shift_ref, h2_keep, pos_keep,
                *, first_of_seq, ns, sl, carry, sp_chunk):
    tm = ns * sl
    x = x_ref[0]
    h = _rmsnorm(x, gmix_ref[...])
    z = jnp.dot(h.astype(BF16), win_ref[...], preferred_element_type=F32)
    z_a = z[:, 0:C_CONV]
    z_g = z[:, C_CONV:2 * C_CONV]
    z_u = z[:, 2 * C_CONV:2 * C_CONV + C_GMLP]
    z_v = z[:, 2 * C_CONV + C_GMLP:2 * C_CONV + 2 * C_GMLP]
    z_q = z[:, 2 * C_CONV + 2 * C_GMLP:IN_COLS]

    glu = z_a * _sigmoid(z_g)
    if carry:
        ext_ref[:, 0:HIST, :] = jnp.where(first_of_seq, hist_ref[0], ext_ref[:, 0:HIST, :])
    else:
        ext_ref[:, 0:HIST, :] = hist_ref[0]
    conv_parts = []
    for s in range(ns):
        ext_s = ext_ref.at[s]
        ext_s[HIST:HIST + sl, :] = glu[s * sl:(s + 1) * sl]
        n_shift = HIST + sl - SUBLANES
        for r in range(1, SUBLANES):
            shift_ref[s, r - 1, 0:n_shift, :] = ext_s[pl.ds(r, n_shift), :]
        rc = min(sl, 64)
        for r0 in range(0, sl, rc):
            acc = jnp.broadcast_to(bdw_ref[...], (rc, C_CONV))
            for j in range(CONV_W):
                a, r = divmod(j + HIST_OFF, SUBLANES)
                src = ext_s if r == 0 else shift_ref.at[s, r - 1]
                acc = acc + wdw_ref[j:j + 1, :] * src[pl.ds(r0 + a * SUBLANES, rc), :]
            conv_parts.append(acc)
        new_hist = ext_s[sl:sl + HIST, :]
        histout_ref[0, s] = new_hist
        if carry:
            ext_s[0:HIST, :] = new_hist
    y = jnp.concatenate(conv_parts, axis=0) if len(conv_parts) > 1 else conv_parts[0]
    y = _layernorm(y, lcg_ref[...], lcb_ref[...])
    c_out = y * _sigmoid(y)

    u = _gelu(z_u)
    v = _layernorm(_gelu(z_v), lvg_ref[...], lvb_ref[...])
    v_ref[...] = v
    vb = v.astype(BF16)
    rr = lax.broadcasted_iota(I32, (GMLP_CHUNK, GMLP_CHUNK), 0)
    cc = lax.broadcasted_iota(I32, (GMLP_CHUNK, GMLP_CHUNK), 1)
    sp_mask = (cc <= rr) & ((rr // sp_chunk) == (cc // sp_chunk))
    col = lax.broadcasted_iota(I32, (GMLP_CHUNK, C_GMLP), 1)
    w_heads = [jnp.where(sp_mask, wsp_ref[hh], 0.0).astype(BF16) for hh in range(GMLP_HEADS)]
    g_parts = []
    for c in range(tm // GMLP_CHUNK):
        vc = vb[c * GMLP_CHUNK:(c + 1) * GMLP_CHUNK]
        sg = bsp_ref[...]
        for hh in range(GMLP_HEADS):
            head_cols = (col >= hh * GMLP_HD) & (col < (hh + 1) * GMLP_HD)
            vh = jnp.where(head_cols, vc, jnp.zeros_like(vc))
            sg = sg + jnp.dot(w_heads[hh], vh, preferred_element_type=F32)
        g_parts.append(u[c * GMLP_CHUNK:(c + 1) * GMLP_CHUNK] * sg)
    g_out = jnp.concatenate(g_parts, axis=0) if len(g_parts) > 1 else g_parts[0]

    qs = z_q * np.float32(XA_HD ** -0.5)
    qcol = lax.broadcasted_iota(I32, (sl, C_XA), 1)
    a_parts = []
    for s in range(ns):
        q_s = qs[s * sl:(s + 1) * sl]
        kb = mk_ref[0, s].astype(BF16)
        vvb = mv_ref[0, s].astype(BF16)
        hmasks = [(qcol >= hh * XA_HD) & (qcol < (hh + 1) * XA_HD) for hh in range(XA_HEADS)]
        stack = XA_HEADS if sl * XA_HEADS <= N_MEM else 1
        a_s = jnp.zeros((sl, C_XA), F32)
        for h0 in range(0, XA_HEADS, stack):
            heads = range(h0, h0 + stack)
            qh = jnp.concatenate([jnp.where(hmasks[hh], q_s, 0.0) for hh in heads], axis=0).astype(BF16)
            sc = lax.dot_general(qh, kb, (((1,), (1,)), ((), ())), preferred_element_type=F32)
            p = jnp.exp(sc - jnp.max(sc, axis=-1, keepdims=True))
            den = jnp.sum(p, axis=-1, keepdims=True)
            oh = jnp.dot(p.astype(BF16), vvb, preferred_element_type=F32) / den
            for n, hh in enumerate(heads):
                a_s = a_s + jnp.where(hmasks[hh], oh[n * sl:(n + 1) * sl], 0.0)
        a_parts.append(a_s)
    a_out = jnp.concatenate(a_parts, axis=0) if len(a_parts) > 1 else a_parts[0]

    mix = jnp.concatenate([c_out, g_out, a_out], axis=-1).astype(BF16)
    x1 = x + jnp.dot(mix, wout_ref[...], preferred_element_type=F32)
    x1_ref[...] = x1

    h2 = _rmsnorm(x1, gffn_ref[...])
    h2_hi = h2.astype(BF16)
    h2_lo = (h2 - h2_hi.astype(F32)).astype(BF16)
    nt_dims = (((1,), (1,)), ((), ()))
    lg = (lax.dot_general(wrh_ref[...], h2_hi, nt_dims, preferred_element_type=F32)
          + lax.dot_general(wrl_ref[...], h2_hi, nt_dims, preferred_element_type=F32)
          + lax.dot_general(wrh_ref[...], h2_lo, nt_dims, preferred_element_type=F32)
          + br_ref[...])
    eio = lax.broadcasted_iota(I32, (N_EXPERTS, tm), 0)
    work = lg
    vals, idxs = [], []
    for _ in range(TOP_K):
        m = jnp.max(work, axis=0, keepdims=True)
        idx = jnp.min(jnp.where(work == m, eio, N_EXPERTS), axis=0, keepdims=True)
        vals.append(m)
        idxs.append(idx)
        work = jnp.where(eio == idx, -jnp.inf, work)
    exps = [jnp.exp(vk - vals[0]) for vk in vals]
    den = exps[0] + exps[1] + exps[2] + exps[3]
    gate_ref[...] = jnp.concatenate([ek / den for ek in exps], axis=0)

    sel = jnp.zeros((N_EXPERTS, tm), F32)
    for idx in idxs:
        sel = sel + jnp.where(eio == idx, 1.0, 0.0)
    selb = sel.astype(BF16)
    tr = lax.broadcasted_iota(I32, (tm, tm), 0)
    tc = lax.broadcasted_iota(I32, (tm, tm), 1)
    before = jnp.where(tr < tc, 1.0, 0.0).astype(BF16)
    ranks = jnp.dot(selb, before, preferred_element_type=F32)
    er = lax.broadcasted_iota(I32, (N_EXPERTS, N_EXPERTS), 0)
    ec = lax.broadcasted_iota(I32, (N_EXPERTS, N_EXPERTS), 1)
    lower = jnp.where(ec < er, 1.0, 0.0).astype(BF16)
    cnt = jnp.sum(sel, axis=1, keepdims=True)
    half_len = jnp.floor((cnt + 1.0) * 0.5)
    off_pairs = jnp.dot(lower, jnp.broadcast_to(half_len, (N_EXPERTS, LANES)).astype(BF16),
                        preferred_element_type=F32)[:, 0:1]
    cnt_ref[0] = jnp.broadcast_to(half_len, (N_EXPERTS, LANES)).astype(I32)
    off_ref[0] = jnp.broadcast_to(off_pairs, (N_EXPERTS, LANES)).astype(I32)
    slot_of = 2.0 * off_pairs + ranks
    pos = [jnp.sum(jnp.where(eio == idx, slot_of, 0.0), axis=0, keepdims=True).astype(I32) for idx in idxs]
    pos_all = jnp.concatenate(pos, axis=0)
    pos_ref[...] = pos_all
    pos_keep[...] = pos_all
    h2_keep[...] = h2_hi


def _mixer(x, hist, mem_k, mem_v, wts, *, ns, sl, carry, sp_chunk, pairs_per_tile):
    g, r, _ = x.shape
    tm = ns * sl
    nt = r // tm
    ntot = g * nt
    assert 2 * pairs_per_tile >= TOP_K * tm + N_EXPERTS
    tile = lambda s: jnp.minimum(s, ntot - 1)
    const2 = lambda s: (0, 0)
    const3 = lambda s: (0, 0, 0)
    tile_row = lambda s: (tile(s), 0)
    tile_lane = lambda s: (0, tile(s))
    per_seq = lambda s: (tile(s) // nt, 0, 0, 0)
    in_specs = [
        pl.BlockSpec((1, tm, D_MODEL), lambda s: (tile(s) // nt, tile(s) % nt, 0)),
        pl.BlockSpec((1, ns, HIST, C_CONV), per_seq),
        pl.BlockSpec((1, ns, N_MEM, C_XA), per_seq),
        pl.BlockSpec((1, ns, N_MEM, C_XA), per_seq),
        pl.BlockSpec((1, D_MODEL), const2),
        pl.BlockSpec((D_MODEL, IN_COLS), const2),
        pl.BlockSpec((HIST, C_CONV), const2),
        pl.BlockSpec((1, C_CONV), const2),
        pl.BlockSpec((1, C_CONV), const2),
        pl.BlockSpec((1, C_CONV), const2),
        pl.BlockSpec((1, C_GMLP), const2),
        pl.BlockSpec((1, C_GMLP), const2),
        pl.BlockSpec((GMLP_HEADS, GMLP_CHUNK, GMLP_CHUNK), const3),
        pl.BlockSpec((GMLP_CHUNK, C_GMLP), const2),
        pl.BlockSpec((D_MODEL, D_MODEL), const2),
        pl.BlockSpec((1, D_MODEL), const2),
        pl.BlockSpec((N_EXPERTS, D_MODEL), const2),
        pl.BlockSpec((N_EXPERTS, D_MODEL), const2),
        pl.BlockSpec((N_EXPERTS, 1), const2),
    ]
    tile_cnt = lambda s: (tile(s), 0, 0)
    out_specs = [
        pl.BlockSpec((tm, D_MODEL), tile_row),
        pl.BlockSpec((pairs_per_tile * ROW_TILES, LANES),
                     lambda s: (jnp.maximum(s - 1, 0), 0)),
        pl.BlockSpec((TOP_K, tm), tile_lane),
        pl.BlockSpec((TOP_K, tm), tile_lane),
        pl.BlockSpec((1, N_EXPERTS, LANES), tile_cnt),
        pl.BlockSpec((1, N_EXPERTS, LANES), tile_cnt),
        pl.BlockSpec((1, ns, HIST, C_CONV), per_seq),
        pl.BlockSpec((tm, C_GMLP), tile_row),
    ]
    rows = g * r
    out_shape = [
        jax.ShapeDtypeStruct((rows, D_MODEL), F32),
        jax.ShapeDtypeStruct((ntot * pairs_per_tile * ROW_TILES, LANES), U32),
        jax.ShapeDtypeStruct((TOP_K, rows), I32),
        jax.ShapeDtypeStruct((TOP_K, rows), F32),
        jax.ShapeDtypeStruct((ntot, N_EXPERTS, LANES), I32),
        jax.ShapeDtypeStruct((ntot, N_EXPERTS, LANES), I32),
        jax.ShapeDtypeStruct((g, ns, HIST, C_CONV), F32),
        jax.ShapeDtypeStruct((rows, C_GMLP), F32),
    ]
    kern = functools.partial(_mixer_kernel, ns=ns, sl=sl, carry=carry, sp_chunk=sp_chunk,
                             tiles_per_seq=nt, n_sorted=2 * pairs_per_tile)
    return pl.pallas_call(
        kern,
        grid=(ntot + 1,),
        in_specs=in_specs,
        out_specs=out_specs,
        out_shape=out_shape,
        scratch_shapes=[pltpu.VMEM((ns, HIST + sl, C_CONV), F32),
                        pltpu.VMEM((ns, SUBLANES - 1, HIST + sl, C_CONV), F32),
                        pltpu.VMEM((tm, D_MODEL), BF16),
                        pltpu.VMEM((TOP_K, tm), I32)],
        compiler_params=pltpu.CompilerParams(
            dimension_semantics=("arbitrary",),
            vmem_limit_bytes=VMEM_LIMIT_MIXER),
        name="mixer_carry" if carry else "mixer_cache",
    )(x, hist, mem_k, mem_v, *wts)


def _strip_pieces(n, max_rows, fn):
    done = 0
    p = max_rows
    while p >= 1:
        has = (n & p) != 0
        pl.when(has)(functools.partial(fn, done, p))
        done = done + jnp.where(has, p, 0)
        p //= 2


class _Table:
    def __init__(self, ref, offset):
        self.ref, self.offset = ref, offset

    def __getitem__(self, i):
        return self.ref[self.offset + i]


def _pack_tables(tables):
    offsets = tuple(int(o) for o in np.cumsum([0] + [t.shape[0] for t in tables[:-1]]))
    return jnp.concatenate([t.astype(I32) for t in tables]), offsets


def _pow2_at_most(n):
    return 1 << (n.bit_length() - 1)


def _rows(ref, first_row, rows):
    return ref.at[pl.ds(pl.multiple_of(first_row * ROW_TILES, ROW_TILES), rows * ROW_TILES)]


def _moe_kernel(tab_ref, xs_hbm, xs2_hbm, wgu_ref, bgu_ref, wdn_ref, bdn_ref, ys_hbm,
                xbuf, ybuf, wgu16, wdn16, sems, ysems, *, offsets, n_blocks):
    tabs = [_Table(tab_ref, o) for o in offsets]
    b0_ref, nbk_ref, bv_ref, nb_ref = tabs[:4]
    strips_a, strips_b = tabs[4:9], tabs[9:14]
    e = pl.program_id(0)
    n_used = nb_ref[0]
    half = MOE_PAIRS // 2

    def gather_start(b, slot):
        pair0 = b * MOE_PAIRS

        for src_hbm, (slo_ref, shi_ref, ssrc_ref, sdst_ref, slen_ref) in ((xs_hbm, strips_a), (xs2_hbm, strips_b)):
            def strip(s, carry, src_hbm=src_hbm, ssrc_ref=ssrc_ref, sdst_ref=sdst_ref, slen_ref=slen_ref):
                lo = jnp.maximum(sdst_ref[s], pair0)
                hi = jnp.minimum(sdst_ref[s] + slen_ref[s], pair0 + MOE_PAIRS)
                src0 = ssrc_ref[s] + (lo - sdst_ref[s])
                dst0 = lo - pair0

                def piece(first, rows):
                    pltpu.make_async_copy(_rows(src_hbm, src0 + first, rows),
                                          _rows(xbuf.at[slot], dst0 + first, rows), sems.at[slot]).start()
                _strip_pieces(hi - lo, MOE_PAIRS, piece)
                return carry
            lax.fori_loop(slo_ref[b], shi_ref[b], strip, 0)

    def gather_wait(b, slot):
        def piece(first, rows):
            del first
            pltpu.make_async_copy(_rows(xs_hbm, 0, rows), _rows(xbuf.at[slot], 0, rows), sems.at[slot]).wait()
        _strip_pieces(bv_ref[b], MOE_PAIRS, piece)

    def out_copy(b, slot):
        return pltpu.make_async_copy(ybuf.at[slot], _rows(ys_hbm, b * MOE_PAIRS, MOE_PAIRS), ysems.at[slot])

    def expert_mlp(xslot, slot, h):
        xb = pltpu.bitcast(_tiles_to_rows(xbuf.at[xslot], half, h * half), BF16)
        gu = jnp.dot(xb, wgu16[...], preferred_element_type=F32) + bgu_ref[0]
        gate = jnp.minimum(gu[:, :D_FF], SWIGLU_LIMIT)
        up = jnp.clip(gu[:, D_FF:], -SWIGLU_LIMIT, SWIGLU_LIMIT)
        act = (up + 1.0) * (gate * _sigmoid(SWIGLU_ALPHA * gate))
        yb = jnp.dot(act.astype(BF16), wdn16[...], preferred_element_type=F32) + bdn_ref[0]
        _rows_to_tiles(ybuf.at[slot], pltpu.bitcast(yb.astype(BF16), U32), half, h * half)

    @pl.when(e == 0)
    def _():
        xbuf[...] = jnp.zeros_like(xbuf)
        for a in range(GATHER_AHEAD):
            @pl.when(a < n_used)
            def _():
                gather_start(a, a)

    @pl.when(nbk_ref[e] > 0)
    def _():
        def cast_rows(c, carry):
            r = pl.multiple_of(c * LANES, LANES)
            wgu16[pl.ds(r, LANES), :] = wgu_ref[pl.ds(r, LANES), :].astype(BF16)
            wdn16[pl.ds(r, LANES), :] = wdn_ref[pl.ds(r, LANES), :].astype(BF16)
            return carry
        lax.fori_loop(0, D_MODEL // LANES, cast_rows, 0)

        def block(j, carry):
            b = b0_ref[e] + j
            slot = b % 2
            xslot = b % (GATHER_AHEAD + 1)

            @pl.when(b + GATHER_AHEAD < n_used)
            def _():
                gather_start(b + GATHER_AHEAD, (b + GATHER_AHEAD) % (GATHER_AHEAD + 1))

            gather_wait(b, xslot)

            @pl.when(b >= 2)
            def _():
                out_copy(b - 2, slot).wait()

            valid = bv_ref[b]
            expert_mlp(xslot, slot, 0)

            @pl.when(valid > half)
            def _():
                expert_mlp(xslot, slot, 1)

            @pl.when(valid <= half)
            def _():
                ybuf[slot, pl.ds(half * ROW_TILES, half * ROW_TILES), :] = jnp.zeros(
                    (half * ROW_TILES, LANES), U32)

            out_copy(b, slot).start()
            return carry
        lax.fori_loop(0, nbk_ref[e], block, 0)

    @pl.when(e == pl.num_programs(0) - 1)
    def _():
        for back in (1, 2):
            @pl.when(n_used >= back)
            def _():
                out_copy(n_used - back, (n_used - back) % 2).wait()
        ybuf[0] = jnp.zeros(ybuf.shape[1:], U32)

        def start_fill(b, carry):
            out_copy(b, 0).start()
            return carry

        def wait_fill(b, carry):
            out_copy(b, 0).wait()
            return carry
        lax.fori_loop(n_used, n_blocks, start_fill, 0)
        lax.fori_loop(n_used, n_blocks, wait_fill, 0)


def _moe(tables, xs, xs2, n_blocks, w_gu, b_gu, w_dn, b_dn):
    wsel = lambda e, *_: (e, 0, 0)
    packed, offsets = _pack_tables(tables)
    slots = n_blocks * MOE_PAIRS
    grid_spec = pltpu.PrefetchScalarGridSpec(
        num_scalar_prefetch=1,
        grid=(N_EXPERTS,),
        in_specs=[pl.BlockSpec(memory_space=pl.ANY),
                  pl.BlockSpec(memory_space=pl.ANY),
                  pl.BlockSpec((None, D_MODEL, 2 * D_FF), wsel),
                  pl.BlockSpec((None, 1, 2 * D_FF), wsel),
                  pl.BlockSpec((None, D_FF, D_MODEL), wsel),
                  pl.BlockSpec((None, 1, D_MODEL), wsel)],
        out_specs=pl.BlockSpec(memory_space=pl.ANY),
        scratch_shapes=[pltpu.VMEM((GATHER_AHEAD + 1, MOE_PAIRS * ROW_TILES, LANES), U32),
                        pltpu.VMEM((2, MOE_PAIRS * ROW_TILES, LANES), U32),
                        pltpu.VMEM((D_MODEL, 2 * D_FF), BF16),
                        pltpu.VMEM((D_FF, D_MODEL), BF16),
                        pltpu.SemaphoreType.DMA((GATHER_AHEAD + 1,)),
                        pltpu.SemaphoreType.DMA((2,))],
    )
    return pl.pallas_call(
        functools.partial(_moe_kernel, offsets=offsets, n_blocks=n_blocks),
        grid_spec=grid_spec,
        out_shape=jax.ShapeDtypeStruct((slots * ROW_TILES, LANES), U32),
        compiler_params=pltpu.CompilerParams(
            dimension_semantics=("arbitrary",),
            vmem_limit_bytes=VMEM_LIMIT_MOE),
        name="moe",
    )(packed, xs, xs2, w_gu, b_gu, w_dn, b_dn)


def _combine_kernel(tab_ref, x1_ref, pos_ref, gate_ref, gfin_ref, ys_hbm,
                    out_ref, ybuf, sems, *, tm, pairs_per_tile, offsets):
    csrc_ref, clen_ref, coff_ref, ctot_ref = [_Table(tab_ref, o) for o in offsets]
    i = pl.program_id(0)
    n_tiles = pl.num_programs(0)
    n_sorted = 2 * pairs_per_tile

    def gather_start(t, slot):
        def strip(e, carry):
            s = t * N_EXPERTS + e
            src0 = csrc_ref[s]
            dst0 = coff_ref[s]

            def piece(first, rows):
                pltpu.make_async_copy(_rows(ys_hbm, src0 + first, rows),
                                      _rows(ybuf.at[slot], dst0 + first, rows), sems.at[slot]).start()
            _strip_pieces(clen_ref[s], tm // 2, piece)
            return carry
        lax.fori_loop(0, N_EXPERTS, strip, 0)

    @pl.when(i == 0)
    def _():
        ybuf[...] = jnp.zeros_like(ybuf)
        gather_start(0, 0)

    @pl.when(i + 1 < n_tiles)
    def _():
        gather_start(i + 1, (i + 1) % 2)

    slot = i % 2

    def wait_piece(first, rows):
        del first
        pltpu.make_async_copy(_rows(ys_hbm, 0, rows), _rows(ybuf.at[slot], 0, rows), sems.at[slot]).wait()
    _strip_pieces(ctot_ref[i], _pow2_at_most(pairs_per_tile), wait_piece)

    y_sorted = pltpu.bitcast(_tiles_to_rows(ybuf.at[slot], pairs_per_tile), BF16)

    rr = lax.broadcasted_iota(I32, (tm, tm), 0)
    cc = lax.broadcasted_iota(I32, (tm, tm), 1)
    eye = rr == cc
    jl = lax.broadcasted_iota(I32, (tm, n_sorted), 1).astype(F32)
    unsort = jnp.zeros((tm, n_sorted), F32)
    for k in range(TOP_K):
        p_col = jnp.sum(jnp.where(eye, pos_ref[k:k + 1, :].astype(F32), 0.0), axis=1, keepdims=True)
        g_col = jnp.sum(jnp.where(eye, gate_ref[k:k + 1, :], 0.0), axis=1, keepdims=True)
        unsort = jnp.where(jl == p_col, g_col, unsort)
    acc = x1_ref[...] + jnp.dot(unsort.astype(BF16), y_sorted, preferred_element_type=F32)
    out_ref[...] = _rmsnorm(acc, gfin_ref[...])


def _combine(tables, x1, pos, gates, g_final, ys, tm, pairs_per_tile):
    t = x1.shape[0]
    nt = t // tm
    packed, offsets = _pack_tables(tables)
    grid_spec = pltpu.PrefetchScalarGridSpec(
        num_scalar_prefetch=1,
        grid=(nt,),
        in_specs=[pl.BlockSpec((tm, D_MODEL), lambda i, *_: (i, 0)),
                  pl.BlockSpec((TOP_K, tm), lambda i, *_: (0, i)),
                  pl.BlockSpec((TOP_K, tm), lambda i, *_: (0, i)),
                  pl.BlockSpec((1, D_MODEL), lambda i, *_: (0, 0)),
                  pl.BlockSpec(memory_space=pl.ANY)],
        out_specs=pl.BlockSpec((tm, D_MODEL), lambda i, *_: (i, 0)),
        scratch_shapes=[pltpu.VMEM((2, pairs_per_tile * ROW_TILES, LANES), U32),
                        pltpu.SemaphoreType.DMA((2,))],
    )
    return pl.pallas_call(
        functools.partial(_combine_kernel, tm=tm, pairs_per_tile=pairs_per_tile, offsets=offsets),
        grid_spec=grid_spec,
        out_shape=jax.ShapeDtypeStruct((t, D_MODEL), F32),
        compiler_params=pltpu.CompilerParams(
            dimension_semantics=("arbitrary",),
            vmem_limit_bytes=VMEM_LIMIT_MIXER),
        name="combine",
    )(packed, x1, pos, gates, g_final, ys)


def _split_bf16(w):
    hi = w.astype(BF16)
    lo = (w - hi.astype(F32)).astype(BF16)
    return hi, lo


def kernel(x_prompt, x_sample, cache_conv, cache_mem_k, cache_mem_v, mem_prompt, g_mix, w_in, w_dw, b_dw, ln_conv_g, ln_conv_b, ln_v_g, ln_v_b, w_spatial, b_spatial, g_mem, w_mem_k, w_mem_v, w_out, g_ffn, w_router, b_router, w_gate_up, b_gate_up, w_down, b_down, g_final):
    depth = g_mix.shape[0]
    assert depth == 1
    l = 0
    bp, seq, _ = x_prompt.shape
    bs, dseq, _ = x_sample.shape
    assert seq % PROMPT_TILE == 0 and bs % SAMPLE_SEQS_PER_TILE == 0
    assert GMLP_CHUNK % dseq == 0 and (SAMPLE_SEQS_PER_TILE * dseq) % GMLP_CHUNK == 0

    row = lambda a: a.reshape(1, -1)
    wr_hi, wr_lo = _split_bf16(w_router[l].T)
    w_dw_pad = jnp.pad(w_dw[l], ((0, HIST - CONV_W), (0, 0)))
    bias_rows = lambda b: jnp.repeat(b.T, GMLP_HD, axis=1)
    common = dict(
        gmix=row(g_mix[l]), win=w_in[l].astype(BF16), wdw=w_dw_pad, bdw=row(b_dw[l]),
        lcg=row(ln_conv_g[l]), lcb=row(ln_conv_b[l]), lvg=row(ln_v_g[l]), lvb=row(ln_v_b[l]),
        wout=w_out[l].astype(BF16), gffn=row(g_ffn[l]), wrh=wr_hi, wrl=wr_lo,
        br=b_router[l].reshape(N_EXPERTS, 1))

    def weights(wsp, bsp):
        c = common
        return (c["gmix"], c["win"], c["wdw"], c["bdw"], c["lcg"], c["lcb"], c["lvg"], c["lvb"],
                wsp, bsp, c["wout"], c["gffn"], c["wrh"], c["wrl"], c["br"])

    reps = GMLP_CHUNK // dseq
    wts_p = weights(w_spatial[l], bias_rows(b_spatial[l]))
    wts_s = weights(jnp.tile(w_spatial[l][:, :dseq, :dseq], (1, reps, reps)),
                    bias_rows(jnp.tile(b_spatial[l][:, :dseq], (1, reps))))

    w_kv = jnp.concatenate([w_mem_k[l], w_mem_v[l]], axis=1).astype(BF16)
    kv_p = _memkv(mem_prompt, row(g_mem[l]), w_kv)
    mk_p = kv_p[:, :, :C_XA]
    mv_p = kv_p[:, :, C_XA:]
    zero_hist = jnp.zeros((bp, 1, HIST, C_CONV), F32)
    tp, ts = bp * seq, bs * dseq
    tm_s = SAMPLE_SEQS_PER_TILE * dseq
    ntp, nts = tp // PROMPT_TILE, ts // tm_s
    ppt_p = (TOP_K * PROMPT_TILE + N_EXPERTS + 1) // 2
    ppt_s = (TOP_K * tm_s + N_EXPERTS + 1) // 2
    (x1_p, xs_p, pos_p, gate_p, cnt_p, off_p, hist_p, _) = _mixer(
        x_prompt, zero_hist, mk_p[:, None], mv_p[:, None], wts_p,
        ns=1, sl=PROMPT_TILE, carry=True, sp_chunk=GMLP_CHUNK, pairs_per_tile=ppt_p)

    gs = bs // SAMPLE_SEQS_PER_TILE
    hist_s_in = jnp.pad(cache_conv[l], ((0, 0), (HIST_OFF, 0), (0, 0))).reshape(
        gs, SAMPLE_SEQS_PER_TILE, HIST, C_CONV)
    mk_s = cache_mem_k[l].reshape(gs, SAMPLE_SEQS_PER_TILE, N_MEM, C_XA)
    mv_s = cache_mem_v[l].reshape(gs, SAMPLE_SEQS_PER_TILE, N_MEM, C_XA)
    (x1_s, xs_s, pos_s, gate_s, cnt_s, off_s, hist_s, v_s) = _mixer(
        x_sample.reshape(gs, tm_s, D_MODEL), hist_s_in, mk_s, mv_s, wts_s,
        ns=SAMPLE_SEQS_PER_TILE, sl=dseq, carry=False, sp_chunk=dseq, pairs_per_tile=ppt_s)

    max_pairs = ((tp + ts) * TOP_K + (ntp + nts) * N_EXPERTS) // 2
    n_blocks = -(-max_pairs // MOE_PAIRS) + N_EXPERTS
    cnt = jnp.concatenate([cnt_p[:, :, 0], cnt_s[:, :, 0]], axis=0)
    off = jnp.concatenate([off_p[:, :, 0], off_s[:, :, 0]], axis=0)
    tile_pair0 = np.concatenate([np.arange(ntp) * ppt_p, np.arange(nts) * ppt_s]).astype(np.int32)
    counts = jnp.sum(cnt, axis=0)
    tile_base = jnp.cumsum(cnt, axis=0) - cnt
    padded = (counts + MOE_PAIRS - 1) // MOE_PAIRS * MOE_PAIRS
    pad_end = jnp.cumsum(padded)
    pad_start = pad_end - padded
    strip_dst = (pad_start[None, :] + tile_base).astype(I32)
    strip_src = (tile_pair0[:, None] + off).astype(I32)
    n_used = (pad_end[-1] // MOE_PAIRS).astype(I32)
    blk_pair0 = jnp.minimum(jnp.arange(n_blocks, dtype=I32), n_used - 1) * MOE_PAIRS
    block_expert = jnp.minimum(
        jnp.sum((pad_end[None, :] <= blk_pair0[:, None]).astype(I32), axis=1), N_EXPERTS - 1)
    of_block = block_expert[:, None] == jnp.arange(N_EXPERTS, dtype=I32)[None, :]
    last_pair = jnp.sum(jnp.where(of_block, (pad_start + counts)[None, :], 0), axis=1)
    block_valid = jnp.clip(last_pair - blk_pair0, 0, MOE_PAIRS).astype(I32)
    def strip_group(t0, t1):
        sdst = strip_dst[t0:t1].T.reshape(-1)
        ssrc = strip_src[t0:t1].T.reshape(-1)
        slen = cnt[t0:t1].T.reshape(-1).astype(I32)
        s_lo = jnp.sum(((sdst + slen)[None, :] <= blk_pair0[:, None]).astype(I32), axis=1)
        s_hi = jnp.sum((sdst[None, :] < (blk_pair0 + MOE_PAIRS)[:, None]).astype(I32), axis=1)
        return (s_lo, s_hi, ssrc, sdst, slen)

    moe_tables = ((pad_start // MOE_PAIRS).astype(I32), (padded // MOE_PAIRS).astype(I32),
                  block_valid, n_used.reshape(1),
                  *strip_group(0, ntp), *strip_group(ntp, ntp + nts))

    ys = _moe(moe_tables, xs_p, xs_s, n_blocks,
              w_gate_up[l], b_gate_up[l][:, None, :], w_down[l], b_down[l][:, None, :])
    gfin = row(g_final)

    def combine_tables(t0, t1):
        return (strip_dst[t0:t1].reshape(-1), cnt[t0:t1].reshape(-1).astype(I32),
                off[t0:t1].reshape(-1).astype(I32), jnp.sum(cnt[t0:t1], axis=1).astype(I32))

    y_p = _combine(combine_tables(0, ntp), x1_p, pos_p, gate_p, gfin, ys, PROMPT_TILE, ppt_p)
    y_s = _combine(combine_tables(ntp, ntp + nts), x1_s, pos_s, gate_s, gfin, ys, tm_s, ppt_s)

    return (y_p.reshape(bp, seq, D_MODEL),
            y_s.reshape(bs, dseq, D_MODEL),
            hist_p[:, 0, HIST_OFF:, :][None],
            mk_p.reshape(bp, N_MEM, XA_HEADS, XA_HD)[None],
            mv_p.reshape(bp, N_MEM, XA_HEADS, XA_HD)[None],
            hist_s.reshape(bs, HIST, C_CONV)[:, HIST_OFF:, :][None],
            v_s.reshape(bs, dseq, C_GMLP)[None])
```

```python
import functools

import numpy as np
import jax
import jax.numpy as jnp
from jax import lax
from jax.experimental import pallas as pl
from jax.experimental.pallas import tpu as pltpu

F32 = jnp.float32
BF16 = jnp.bfloat16
I32 = jnp.int32
U32 = jnp.uint32

D_MODEL = 1024
C_CONV = 384
CONV_W = 31
C_GMLP = 384
GMLP_HEADS = 4
GMLP_HD = 96
GMLP_CHUNK = 128
XA_HEADS = 4
XA_HD = 64
C_XA = 256
N_MEM = 256
N_EXPERTS = 32
TOP_K = 4
D_FF = 1024
SWIGLU_LIMIT = 7.0
SWIGLU_ALPHA = 1.702
EPS = 1e-5
IN_COLS = 2 * C_CONV + 2 * C_GMLP + C_XA

SUBLANES = 8
LANES = 128
ROW_TILES = D_MODEL // LANES
HIST = 32
HIST_OFF = HIST - (CONV_W - 1)

PROMPT_TILE = 512
SAMPLE_SEQS_PER_TILE = 8
MOE_BLOCK = 512
MOE_PAIRS = MOE_BLOCK // 2
GATHER_AHEAD = 2
VMEM_LIMIT_MIXER = 48 * 1024 * 1024
VMEM_LIMIT_MOE = 52 * 1024 * 1024


def _rmsnorm(x, g):
    return x * lax.rsqrt(jnp.mean(x * x, axis=-1, keepdims=True) + EPS) * g


def _layernorm(x, g, b):
    mu = jnp.mean(x, axis=-1, keepdims=True)
    xc = x - mu
    var = jnp.mean(xc * xc, axis=-1, keepdims=True)
    return xc * lax.rsqrt(var + EPS) * g + b


def _gelu(x):
    return 0.5 * x * (1.0 + lax.erf(x * np.float32(1.0 / np.sqrt(2.0))))


def _sigmoid(x):
    return 1.0 / (1.0 + jnp.exp(-x))


def _rows_to_tiles(dst_ref, val, rows, row0=0):
    for j in range(ROW_TILES):
        dst_ref[pl.ds(row0 * ROW_TILES + j, rows, stride=ROW_TILES), :] = val[:, j * LANES:(j + 1) * LANES]


def _tiles_to_rows(src_ref, rows, row0=0):
    return jnp.concatenate(
        [src_ref[pl.ds(row0 * ROW_TILES + j, rows, stride=ROW_TILES), :] for j in range(ROW_TILES)],
        axis=-1)


def _memkv_kernel(mem_ref, g_ref, w_ref, o_ref):
    mn = _rmsnorm(mem_ref[0], g_ref[...])
    o_ref[0] = jnp.dot(mn.astype(BF16), w_ref[...], preferred_element_type=F32)


def _memkv(mem, g_mem, w_kv):
    b = mem.shape[0]
    return pl.pallas_call(
        _memkv_kernel,
        grid=(b,),
        in_specs=[pl.BlockSpec((1, N_MEM, D_MODEL), lambda i: (i, 0, 0)),
                  pl.BlockSpec((1, D_MODEL), lambda i: (0, 0)),
                  pl.BlockSpec((D_MODEL, 2 * C_XA), lambda i: (0, 0))],
        out_specs=pl.BlockSpec((1, N_MEM, 2 * C_XA), lambda i: (i, 0, 0)),
        out_shape=jax.ShapeDtypeStruct((b, N_MEM, 2 * C_XA), F32),
        name="memkv",
    )(mem, g_mem, w_kv)


def _mixer_kernel(x_ref, hist_ref, mk_ref, mv_ref, gmix_ref, win_ref, wdw_ref, bdw_ref,
                  lcg_ref, lcb_ref, lvg_ref, lvb_ref, wsp_ref, bsp_ref, wout_ref, gffn_ref,
                  wrh_ref, wrl_ref, br_ref, *rest, ns, sl, carry, sp_chunk, tiles_per_seq, n_sorted):
    (x1_ref, xst_ref, pos_ref, gate_ref, cnt_ref, off_ref, histout_ref, v_ref,
     ext_ref, shift_ref, h2_keep, pos_keep, before_ref, wsp16_ref) = rest
    tm = ns * sl
    step = pl.program_id(0)
    n_tiles = pl.num_programs(0) - 1

    def sort_previous_tile():
        jj = lax.broadcasted_iota(I32, (n_sorted, tm), 0)
        perm = jnp.zeros((n_sorted, tm), F32)
        for k in range(TOP_K):
            perm = jnp.where(jj == pos_keep[k:k + 1, :], 1.0, perm)
        sorted_rows = jnp.dot(perm.astype(BF16), h2_keep[...], preferred_element_type=F32)
        pairs = pltpu.bitcast(sorted_rows.astype(BF16), U32)
        _rows_to_tiles(xst_ref, pairs, n_sorted // 2)

    @pl.when(step == 0)
    def _():
        h2_keep[...] = jnp.zeros_like(h2_keep)
        pos_keep[...] = jnp.full(pos_keep.shape, -1, I32)
        tr = lax.broadcasted_iota(I32, (tm, tm), 0)
        tc = lax.broadcasted_iota(I32, (tm, tm), 1)
        before_ref[...] = jnp.where(tr < tc, 1.0, 0.0).astype(BF16)
        rr = lax.broadcasted_iota(I32, (GMLP_CHUNK, GMLP_CHUNK), 0)
        cc = lax.broadcasted_iota(I32, (GMLP_CHUNK, GMLP_CHUNK), 1)
        sp_mask = (cc <= rr) & ((rr // sp_chunk) == (cc // sp_chunk))
        for hh in range(GMLP_HEADS):
            wsp16_ref[hh] = jnp.where(sp_mask, wsp_ref[hh], 0.0).astype(BF16)

    @pl.when(step < n_tiles)
    def _():
        sort_previous_tile()
        _mixer_tile(x_ref, hist_ref, mk_ref, mv_ref, gmix_ref, win_ref, wdw_ref, bdw_ref,
                    lcg_ref, lcb_ref, lvg_ref, lvb_ref, wsp16_ref, bsp_ref, wout_ref, gffn_ref,
                    wrh_ref, wrl_ref, br_ref, x1_ref, pos_ref, gate_ref, cnt_ref, off_ref,
                    histout_ref, v_ref, ext_ref, shift_ref, h2_keep, pos_keep, before_ref,
                    first_of_seq=(step % tiles_per_seq) == 0,
                    ns=ns, sl=sl, carry=carry)

    @pl.when(step == n_tiles)
    def _():
        sort_previous_tile()


def _mixer_tile(x_ref, hist_ref, mk_ref, mv_ref, gmix_ref, win_ref, wdw_ref, bdw_ref,
                lcg_ref, lcb_ref, lvg_ref, lvb_ref, wsp16_ref, bsp_ref, wout_ref, gffn_ref,
                wrh_ref, wrl_ref, br_ref, x1_ref, pos_ref, gate_ref, cnt_ref, off_ref,
                histout_ref, v_ref, ext_ref, shift_ref, h2_keep, pos_keep, before_ref,
                *, first_of_seq, ns, sl, carry):
    tm = ns * sl
    x = x_ref[0]
    h = _rmsnorm(x, gmix_ref[...])
    z = jnp.dot(h.astype(BF16), win_ref[...], preferred_element_type=F32)
    z_a = z[:, 0:C_CONV]
    z_g = z[:, C_CONV:2 * C_CONV]
    z_u = z[:, 2 * C_CONV:2 * C_CONV + C_GMLP]
    z_v = z[:, 2 * C_CONV + C_GMLP:2 * C_CONV + 2 * C_GMLP]
    z_q = z[:, 2 * C_CONV + 2 * C_GMLP:IN_COLS]

    glu = z_a * _sigmoid(z_g)
    if carry:
        ext_ref[:, 0:HIST, :] = jnp.where(first_of_seq, hist_ref[0], ext_ref[:, 0:HIST, :])
    else:
        ext_ref[:, 0:HIST, :] = hist_ref[0]
    conv_parts = []
    for s in range(ns):
        ext_s = ext_ref.at[s]
        ext_s[HIST:HIST + sl, :] = glu[s * sl:(s + 1) * sl]
        n_shift = HIST + sl - SUBLANES
        for r in range(1, SUBLANES):
            shift_ref[s, r - 1, 0:n_shift, :] = ext_s[pl.ds(r, n_shift), :]
        rc = min(sl, 64)
        for r0 in range(0, sl, rc):
            acc = jnp.broadcast_to(bdw_ref[...], (rc, C_CONV))
            for j in range(CONV_W):
                a, r = divmod(j + HIST_OFF, SUBLANES)
                src = ext_s if r == 0 else shift_ref.at[s, r - 1]
                acc = acc + wdw_ref[j:j + 1, :] * src[pl.ds(r0 + a * SUBLANES, rc), :]
            conv_parts.append(acc)
        new_hist = ext_s[sl:sl + HIST, :]
        histout_ref[0, s] = new_hist
        if carry:
            ext_s[0:HIST, :] = new_hist
    y = jnp.concatenate(conv_parts, axis=0) if len(conv_parts) > 1 else conv_parts[0]
    y = _layernorm(y, lcg_ref[...], lcb_ref[...])
    c_out = y * _sigmoid(y)

    u = _gelu(z_u)
    v = _layernorm(_gelu(z_v), lvg_ref[...], lvb_ref[...])
    v_ref[...] = v
    vb = v.astype(BF16)
    col = lax.broadcasted_iota(I32, (GMLP_CHUNK, C_GMLP), 1)
    w_heads = [wsp16_ref[hh] for hh in range(GMLP_HEADS)]
    g_parts = []
    for c in range(tm // GMLP_CHUNK):
        vc = vb[c * GMLP_CHUNK:(c + 1) * GMLP_CHUNK]
        sg = bsp_ref[...]
        for hh in range(GMLP_HEADS):
            head_cols = (col >= hh * GMLP_HD) & (col < (hh + 1) * GMLP_HD)
            vh = jnp.where(head_cols, vc, jnp.zeros_like(vc))
            sg = sg + jnp.dot(w_heads[hh], vh, preferred_element_type=F32)
        g_parts.append(u[c * GMLP_CHUNK:(c + 1) * GMLP_CHUNK] * sg)
    g_out = jnp.concatenate(g_parts, axis=0) if len(g_parts) > 1 else g_parts[0]

    qs = z_q * np.float32(XA_HD ** -0.5)
    qcol = lax.broadcasted_iota(I32, (sl, C_XA), 1)
    a_parts = []
    for s in range(ns):
        q_s = qs[s * sl:(s + 1) * sl]
        kb = mk_ref[0, s].astype(BF16)
        vvb = mv_ref[0, s].astype(BF16)
        hmasks = [(qcol >= hh * XA_HD) & (qcol < (hh + 1) * XA_HD) for hh in range(XA_HEADS)]
        stack = XA_HEADS if sl * XA_HEADS <= N_MEM else 1
        a_s = jnp.zeros((sl, C_XA), F32)
        for h0 in range(0, XA_HEADS, stack):
            heads = range(h0, h0 + stack)
            qh = jnp.concatenate([jnp.where(hmasks[hh], q_s, 0.0) for hh in heads], axis=0).astype(BF16)
            sc = lax.dot_general(qh, kb, (((1,), (1,)), ((), ())), preferred_element_type=F32)
            p = jnp.exp(sc - jnp.max(sc, axis=-1, keepdims=True))
            den = jnp.sum(p, axis=-1, keepdims=True)
            oh = jnp.dot(p.astype(BF16), vvb, preferred_element_type=F32) / den
            for n, hh in enumerate(heads):
                a_s = a_s + jnp.where(hmasks[hh], oh[n * sl:(n + 1) * sl], 0.0)
        a_parts.append(a_s)
    a_out = jnp.concatenate(a_parts, axis=0) if len(a_parts) > 1 else a_parts[0]

    mix = jnp.concatenate([c_out, g_out, a_out], axis=-1).astype(BF16)
    x1 = x + jnp.dot(mix, wout_ref[...], preferred_element_type=F32)
    x1_ref[...] = x1

    h2 = _rmsnorm(x1, gffn_ref[...])
    h2_hi = h2.astype(BF16)
    h2_lo = (h2 - h2_hi.astype(F32)).astype(BF16)
    nt_dims = (((1,), (1,)), ((), ()))
    lg = (lax.dot_general(wrh_ref[...], h2_hi, nt_dims, preferred_element_type=F32)
          + lax.dot_general(wrl_ref[...], h2_hi, nt_dims, preferred_element_type=F32)
          + lax.dot_general(wrh_ref[...], h2_lo, nt_dims, preferred_element_type=F32)
          + br_ref[...])
    eio = lax.broadcasted_iota(I32, (N_EXPERTS, tm), 0)
    work = lg
    vals, idxs = [], []
    for _ in range(TOP_K):
        m = jnp.max(work, axis=0, keepdims=True)
        idx = jnp.min(jnp.where(work == m, eio, N_EXPERTS), axis=0, keepdims=True)
        vals.append(m)
        idxs.append(idx)
        work = jnp.where(eio == idx, -jnp.inf, work)
    exps = [jnp.exp(vk - vals[0]) for vk in vals]
    den = exps[0] + exps[1] + exps[2] + exps[3]
    gate_ref[...] = jnp.concatenate([ek / den for ek in exps], axis=0)

    sel = jnp.zeros((N_EXPERTS, tm), F32)
    for idx in idxs:
        sel = sel + jnp.where(eio == idx, 1.0, 0.0)
    selb = sel.astype(BF16)
    ranks = jnp.dot(selb, before_ref[...], preferred_element_type=F32)
    er = lax.broadcasted_iota(I32, (N_EXPERTS, N_EXPERTS), 0)
    ec = lax.broadcasted_iota(I32, (N_EXPERTS, N_EXPERTS), 1)
    lower = jnp.where(ec < er, 1.0, 0.0).astype(BF16)
    cnt = jnp.sum(sel, axis=1, keepdims=True)
    half_len = jnp.floor((cnt + 1.0) * 0.5)
    off_pairs = jnp.dot(lower, jnp.broadcast_to(half_len, (N_EXPERTS, LANES)).astype(BF16),
                        preferred_element_type=F32)[:, 0:1]
    cnt_ref[0] = jnp.broadcast_to(half_len, (N_EXPERTS, LANES)).astype(I32)
    off_ref[0] = jnp.broadcast_to(off_pairs, (N_EXPERTS, LANES)).astype(I32)
    slot_of = 2.0 * off_pairs + ranks
    pos = [jnp.sum(jnp.where(eio == idx, slot_of, 0.0), axis=0, keepdims=True).astype(I32) for idx in idxs]
    pos_all = jnp.concatenate(pos, axis=0)
    pos_ref[...] = pos_all
    pos_keep[...] = pos_all
    h2_keep[...] = h2_hi


def _mixer(x, hist, mem_k, mem_v, wts, *, ns, sl, carry, sp_chunk, pairs_per_tile):
    g, r, _ = x.shape
    tm = ns * sl
    nt = r // tm
    ntot = g * nt
    assert 2 * pairs_per_tile >= TOP_K * tm + N_EXPERTS
    tile = lambda s: jnp.minimum(s, ntot - 1)
    const2 = lambda s: (0, 0)
    const3 = lambda s: (0, 0, 0)
    tile_row = lambda s: (tile(s), 0)
    tile_lane = lambda s: (0, tile(s))
    per_seq = lambda s: (tile(s) // nt, 0, 0, 0)
    in_specs = [
        pl.BlockSpec((1, tm, D_MODEL), lambda s: (tile(s) // nt, tile(s) % nt, 0)),
        pl.BlockSpec((1, ns, HIST, C_CONV), per_seq),
        pl.BlockSpec((1, ns, N_MEM, C_XA), per_seq),
        pl.BlockSpec((1, ns, N_MEM, C_XA), per_seq),
        pl.BlockSpec((1, D_MODEL), const2),
        pl.BlockSpec((D_MODEL, IN_COLS), const2),
        pl.BlockSpec((HIST, C_CONV), const2),
        pl.BlockSpec((1, C_CONV), const2),
        pl.BlockSpec((1, C_CONV), const2),
        pl.BlockSpec((1, C_CONV), const2),
        pl.BlockSpec((1, C_GMLP), const2),
        pl.BlockSpec((1, C_GMLP), const2),
        pl.BlockSpec((GMLP_HEADS, GMLP_CHUNK, GMLP_CHUNK), const3),
        pl.BlockSpec((GMLP_CHUNK, C_GMLP), const2),
        pl.BlockSpec((D_MODEL, D_MODEL), const2),
        pl.BlockSpec((1, D_MODEL), const2),
        pl.BlockSpec((N_EXPERTS, D_MODEL), const2),
        pl.BlockSpec((N_EXPERTS, D_MODEL), const2),
        pl.BlockSpec((N_EXPERTS, 1), const2),
    ]
    tile_cnt = lambda s: (tile(s), 0, 0)
    out_specs = [
        pl.BlockSpec((tm, D_MODEL), tile_row),
        pl.BlockSpec((pairs_per_tile * ROW_TILES, LANES),
                     lambda s: (jnp.maximum(s - 1, 0), 0)),
        pl.BlockSpec((TOP_K, tm), tile_lane),
        pl.BlockSpec((TOP_K, tm), tile_lane),
        pl.BlockSpec((1, N_EXPERTS, LANES), tile_cnt),
        pl.BlockSpec((1, N_EXPERTS, LANES), tile_cnt),
        pl.BlockSpec((1, ns, HIST, C_CONV), per_seq),
        pl.BlockSpec((tm, C_GMLP), tile_row),
    ]
    rows = g * r
    out_shape = [
        jax.ShapeDtypeStruct((rows, D_MODEL), F32),
        jax.ShapeDtypeStruct((ntot * pairs_per_tile * ROW_TILES, LANES), U32),
        jax.ShapeDtypeStruct((TOP_K, rows), I32),
        jax.ShapeDtypeStruct((TOP_K, rows), F32),
        jax.ShapeDtypeStruct((ntot, N_EXPERTS, LANES), I32),
        jax.ShapeDtypeStruct((ntot, N_EXPERTS, LANES), I32),
        jax.ShapeDtypeStruct((g, ns, HIST, C_CONV), F32),
        jax.ShapeDtypeStruct((rows, C_GMLP), F32),
    ]
    kern = functools.partial(_mixer_kernel, ns=ns, sl=sl, carry=carry, sp_chunk=sp_chunk,
                             tiles_per_seq=nt, n_sorted=2 * pairs_per_tile)
    return pl.pallas_call(
        kern,
        grid=(ntot + 1,),
        in_specs=in_specs,
        out_specs=out_specs,
        out_shape=out_shape,
        scratch_shapes=[pltpu.VMEM((ns, HIST + sl, C_CONV), F32),
                        pltpu.VMEM((ns, SUBLANES - 1, HIST + sl, C_CONV), F32),
                        pltpu.VMEM((tm, D_MODEL), BF16),
                        pltpu.VMEM((TOP_K, tm), I32),
                        pltpu.VMEM((tm, tm), BF16),
                        pltpu.VMEM((GMLP_HEADS, GMLP_CHUNK, GMLP_CHUNK), BF16)],
        compiler_params=pltpu.CompilerParams(
            dimension_semantics=("arbitrary",),
            vmem_limit_bytes=VMEM_LIMIT_MIXER),
        name="mixer_carry" if carry else "mixer_cache",
    )(x, hist, mem_k, mem_v, *wts)


def _strip_pieces(n, max_rows, fn):
    done = 0
    p = max_rows
    while p >= 1:
        has = (n & p) != 0
        pl.when(has)(functools.partial(fn, done, p))
        done = done + jnp.where(has, p, 0)
        p //= 2


class _Table:
    def __init__(self, ref, offset):
        self.ref, self.offset = ref, offset

    def __getitem__(self, i):
        return self.ref[self.offset + i]


def _pack_tables(tables):
    offsets = tuple(int(o) for o in np.cumsum([0] + [t.shape[0] for t in tables[:-1]]))
    return jnp.concatenate([t.astype(I32) for t in tables]), offsets


def _pow2_at_most(n):
    return 1 << (n.bit_length() - 1)


def _rows(ref, first_row, rows):
    return ref.at[pl.ds(pl.multiple_of(first_row * ROW_TILES, ROW_TILES), rows * ROW_TILES)]


def _moe_kernel(tab_ref, xs_hbm, xs2_hbm, wgu_ref, bgu_ref, wdn_ref, bdn_ref, ys_hbm,
                xbuf, ybuf, wgu16, wdn16, sems, ysems, *, offsets, n_blocks):
    tabs = [_Table(tab_ref, o) for o in offsets]
    b0_ref, nbk_ref, bv_ref, nb_ref = tabs[:4]
    strips_a, strips_b = tabs[4:9], tabs[9:14]
    e = pl.program_id(0)
    n_used = nb_ref[0]
    half = MOE_PAIRS // 2

    def gather_start(b, slot):
        pair0 = b * MOE_PAIRS

        for src_hbm, (slo_ref, shi_ref, ssrc_ref, sdst_ref, slen_ref) in ((xs_hbm, strips_a), (xs2_hbm, strips_b)):
            def strip(s, carry, src_hbm=src_hbm, ssrc_ref=ssrc_ref, sdst_ref=sdst_ref, slen_ref=slen_ref):
                lo = jnp.maximum(sdst_ref[s], pair0)
                hi = jnp.minimum(sdst_ref[s] + slen_ref[s], pair0 + MOE_PAIRS)
                src0 = ssrc_ref[s] + (lo - sdst_ref[s])
                dst0 = lo - pair0

                def piece(first, rows):
                    pltpu.make_async_copy(_rows(src_hbm, src0 + first, rows),
                                          _rows(xbuf.at[slot], dst0 + first, rows), sems.at[slot]).start()
                _strip_pieces(hi - lo, MOE_PAIRS, piece)
                return carry
            lax.fori_loop(slo_ref[b], shi_ref[b], strip, 0)

    def gather_wait(b, slot):
        def piece(first, rows):
            del first
            pltpu.make_async_copy(_rows(xs_hbm, 0, rows), _rows(xbuf.at[slot], 0, rows), sems.at[slot]).wait()
        _strip_pieces(bv_ref[b], MOE_PAIRS, piece)

    def out_copy(b, slot):
        return pltpu.make_async_copy(ybuf.at[slot], _rows(ys_hbm, b * MOE_PAIRS, MOE_PAIRS), ysems.at[slot])

    def expert_mlp(xslot, slot, h):
        xb = pltpu.bitcast(_tiles_to_rows(xbuf.at[xslot], half, h * half), BF16)
        gu = jnp.dot(xb, wgu16[...], preferred_element_type=F32) + bgu_ref[0]
        gate = jnp.minimum(gu[:, :D_FF], SWIGLU_LIMIT)
        up = jnp.clip(gu[:, D_FF:], -SWIGLU_LIMIT, SWIGLU_LIMIT)
        act = (up + 1.0) * (gate * _sigmoid(SWIGLU_ALPHA * gate))
        yb = jnp.dot(act.astype(BF16), wdn16[...], preferred_element_type=F32) + bdn_ref[0]
        _rows_to_tiles(ybuf.at[slot], pltpu.bitcast(yb.astype(BF16), U32), half, h * half)

    @pl.when(e == 0)
    def _():
        xbuf[...] = jnp.zeros_like(xbuf)
        for a in range(GATHER_AHEAD):
            @pl.when(a < n_used)
            def _():
                gather_start(a, a)

    @pl.when(nbk_ref[e] > 0)
    def _():
        def cast_rows(c, carry):
            r = pl.multiple_of(c * LANES, LANES)
            wgu16[pl.ds(r, LANES), :] = wgu_ref[pl.ds(r, LANES), :].astype(BF16)
            wdn16[pl.ds(r, LANES), :] = wdn_ref[pl.ds(r, LANES), :].astype(BF16)
            return carry
        lax.fori_loop(0, D_MODEL // LANES, cast_rows, 0)

        def block(j, carry):
            b = b0_ref[e] + j
            slot = b % 2
            xslot = b % (GATHER_AHEAD + 1)

            @pl.when(b + GATHER_AHEAD < n_used)
            def _():
                gather_start(b + GATHER_AHEAD, (b + GATHER_AHEAD) % (GATHER_AHEAD + 1))

            gather_wait(b, xslot)

            @pl.when(b >= 2)
            def _():
                out_copy(b - 2, slot).wait()

            valid = bv_ref[b]
            expert_mlp(xslot, slot, 0)

            @pl.when(valid > half)
            def _():
                expert_mlp(xslot, slot, 1)

            @pl.when(valid <= half)
            def _():
                ybuf[slot, pl.ds(half * ROW_TILES, half * ROW_TILES), :] = jnp.zeros(
                    (half * ROW_TILES, LANES), U32)

            out_copy(b, slot).start()
            return carry
        lax.fori_loop(0, nbk_ref[e], block, 0)

    @pl.when(e == pl.num_programs(0) - 1)
    def _():
        for back in (1, 2):
            @pl.when(n_used >= back)
            def _():
                out_copy(n_used - back, (n_used - back) % 2).wait()
        ybuf[0] = jnp.zeros(ybuf.shape[1:], U32)

        def start_fill(b, carry):
            out_copy(b, 0).start()
            return carry

        def wait_fill(b, carry):
            out_copy(b, 0).wait()
            return carry
        lax.fori_loop(n_used, n_blocks, start_fill, 0)
        lax.fori_loop(n_used, n_blocks, wait_fill, 0)


def _moe(tables, xs, xs2, n_blocks, w_gu, b_gu, w_dn, b_dn):
    wsel = lambda e, *_: (e, 0, 0)
    packed, offsets = _pack_tables(tables)
    slots = n_blocks * MOE_PAIRS
    grid_spec = pltpu.PrefetchScalarGridSpec(
        num_scalar_prefetch=1,
        grid=(N_EXPERTS,),
        in_specs=[pl.BlockSpec(memory_space=pl.ANY),
                  pl.BlockSpec(memory_space=pl.ANY),
                  pl.BlockSpec((None, D_MODEL, 2 * D_FF), wsel),
                  pl.BlockSpec((None, 1, 2 * D_FF), wsel),
                  pl.BlockSpec((None, D_FF, D_MODEL), wsel),
                  pl.BlockSpec((None, 1, D_MODEL), wsel)],
        out_specs=pl.BlockSpec(memory_space=pl.ANY),
        scratch_shapes=[pltpu.VMEM((GATHER_AHEAD + 1, MOE_PAIRS * ROW_TILES, LANES), U32),
                        pltpu.VMEM((2, MOE_PAIRS * ROW_TILES, LANES), U32),
                        pltpu.VMEM((D_MODEL, 2 * D_FF), BF16),
                        pltpu.VMEM((D_FF, D_MODEL), BF16),
                        pltpu.SemaphoreType.DMA((GATHER_AHEAD + 1,)),
                        pltpu.SemaphoreType.DMA((2,))],
    )
    return pl.pallas_call(
        functools.partial(_moe_kernel, offsets=offsets, n_blocks=n_blocks),
        grid_spec=grid_spec,
        out_shape=jax.ShapeDtypeStruct((slots * ROW_TILES, LANES), U32),
        compiler_params=pltpu.CompilerParams(
            dimension_semantics=("arbitrary",),
            vmem_limit_bytes=VMEM_LIMIT_MOE),
        name="moe",
    )(packed, xs, xs2, w_gu, b_gu, w_dn, b_dn)


def _combine_kernel(tab_ref, x1_ref, pos_ref, gate_ref, gfin_ref, ys_hbm,
                    out_ref, ybuf, sems, *, tm, pairs_per_tile, offsets):
    csrc_ref, clen_ref, coff_ref, ctot_ref = [_Table(tab_ref, o) for o in offsets]
    i = pl.program_id(0)
    n_tiles = pl.num_programs(0)
    n_sorted = 2 * pairs_per_tile

    def gather_start(t, slot):
        def strip(e, carry):
            s = t * N_EXPERTS + e
            src0 = csrc_ref[s]
            dst0 = coff_ref[s]

            def piece(first, rows):
                pltpu.make_async_copy(_rows(ys_hbm, src0 + first, rows),
                                      _rows(ybuf.at[slot], dst0 + first, rows), sems.at[slot]).start()
            _strip_pieces(clen_ref[s], tm // 2, piece)
            return carry
        lax.fori_loop(0, N_EXPERTS, strip, 0)

    @pl.when(i == 0)
    def _():
        ybuf[...] = jnp.zeros_like(ybuf)
        gather_start(0, 0)

    @pl.when(i + 1 < n_tiles)
    def _():
        gather_start(i + 1, (i + 1) % 2)

    slot = i % 2

    def wait_piece(first, rows):
        del first
        pltpu.make_async_copy(_rows(ys_hbm, 0, rows), _rows(ybuf.at[slot], 0, rows), sems.at[slot]).wait()
    _strip_pieces(ctot_ref[i], _pow2_at_most(pairs_per_tile), wait_piece)

    y_sorted = pltpu.bitcast(_tiles_to_rows(ybuf.at[slot], pairs_per_tile), BF16)

    rr = lax.broadcasted_iota(I32, (tm, tm), 0)
    cc = lax.broadcasted_iota(I32, (tm, tm), 1)
    eye = rr == cc
    jl = lax.broadcasted_iota(I32, (tm, n_sorted), 1).astype(F32)
    unsort = jnp.zeros((tm, n_sorted), F32)
    for k in range(TOP_K):
        p_col = jnp.sum(jnp.where(eye, pos_ref[k:k + 1, :].astype(F32), 0.0), axis=1, keepdims=True)
        g_col = jnp.sum(jnp.where(eye, gate_ref[k:k + 1, :], 0.0), axis=1, keepdims=True)
        unsort = jnp.where(jl == p_col, g_col, unsort)
    acc = x1_ref[...] + jnp.dot(unsort.astype(BF16), y_sorted, preferred_element_type=F32)
    out_ref[...] = _rmsnorm(acc, gfin_ref[...])


def _combine(tables, x1, pos, gates, g_final, ys, tm, pairs_per_tile):
    t = x1.shape[0]
    nt = t // tm
    packed, offsets = _pack_tables(tables)
    grid_spec = pltpu.PrefetchScalarGridSpec(
        num_scalar_prefetch=1,
        grid=(nt,),
        in_specs=[pl.BlockSpec((tm, D_MODEL), lambda i, *_: (i, 0)),
                  pl.BlockSpec((TOP_K, tm), lambda i, *_: (0, i)),
                  pl.BlockSpec((TOP_K, tm), lambda i, *_: (0, i)),
                  pl.BlockSpec((1, D_MODEL), lambda i, *_: (0, 0)),
                  pl.BlockSpec(memory_space=pl.ANY)],
        out_specs=pl.BlockSpec((tm, D_MODEL), lambda i, *_: (i, 0)),
        scratch_shapes=[pltpu.VMEM((2, pairs_per_tile * ROW_TILES, LANES), U32),
                        pltpu.SemaphoreType.DMA((2,))],
    )
    return pl.pallas_call(
        functools.partial(_combine_kernel, tm=tm, pairs_per_tile=pairs_per_tile, offsets=offsets),
        grid_spec=grid_spec,
        out_shape=jax.ShapeDtypeStruct((t, D_MODEL), F32),
        compiler_params=pltpu.CompilerParams(
            dimension_semantics=("arbitrary",),
            vmem_limit_bytes=VMEM_LIMIT_MIXER),
        name="combine",
    )(packed, x1, pos, gates, g_final, ys)


def _split_bf16(w):
    hi = w.astype(BF16)
    lo = (w - hi.astype(F32)).astype(BF16)
    return hi, lo


def kernel(x_prompt, x_sample, cache_conv, cache_mem_k, cache_mem_v, mem_prompt, g_mix, w_in, w_dw, b_dw, ln_conv_g, ln_conv_b, ln_v_g, ln_v_b, w_spatial, b_spatial, g_mem, w_mem_k, w_mem_v, w_out, g_ffn, w_router, b_router, w_gate_up, b_gate_up, w_down, b_down, g_final):
    depth = g_mix.shape[0]
    assert depth == 1
    l = 0
    bp, seq, _ = x_prompt.shape
    bs, dseq, _ = x_sample.shape
    assert seq % PROMPT_TILE == 0 and bs % SAMPLE_SEQS_PER_TILE == 0
    assert GMLP_CHUNK % dseq == 0 and (SAMPLE_SEQS_PER_TILE * dseq) % GMLP_CHUNK == 0

    row = lambda a: a.reshape(1, -1)
    wr_hi, wr_lo = _split_bf16(w_router[l].T)
    w_dw_pad = jnp.pad(w_dw[l], ((0, HIST - CONV_W), (0, 0)))
    bias_rows = lambda b: jnp.repeat(b.T, GMLP_HD, axis=1)
    common = dict(
        gmix=row(g_mix[l]), win=w_in[l].astype(BF16), wdw=w_dw_pad, bdw=row(b_dw[l]),
        lcg=row(ln_conv_g[l]), lcb=row(ln_conv_b[l]), lvg=row(ln_v_g[l]), lvb=row(ln_v_b[l]),
        wout=w_out[l].astype(BF16), gffn=row(g_ffn[l]), wrh=wr_hi, wrl=wr_lo,
        br=b_router[l].reshape(N_EXPERTS, 1))

    def weights(wsp, bsp):
        c = common
        return (c["gmix"], c["win"], c["wdw"], c["bdw"], c["lcg"], c["lcb"], c["lvg"], c["lvb"],
                wsp, bsp, c["wout"], c["gffn"], c["wrh"], c["wrl"], c["br"])

    reps = GMLP_CHUNK // dseq
    wts_p = weights(w_spatial[l], bias_rows(b_spatial[l]))
    wts_s = weights(jnp.tile(w_spatial[l][:, :dseq, :dseq], (1, reps, reps)),
                    bias_rows(jnp.tile(b_spatial[l][:, :dseq], (1, reps))))

    w_kv = jnp.concatenate([w_mem_k[l], w_mem_v[l]], axis=1).astype(BF16)
    kv_p = _memkv(mem_prompt, row(g_mem[l]), w_kv)
    mk_p = kv_p[:, :, :C_XA]
    mv_p = kv_p[:, :, C_XA:]
    zero_hist = jnp.zeros((bp, 1, HIST, C_CONV), F32)
    tp, ts = bp * seq, bs * dseq
    tm_s = SAMPLE_SEQS_PER_TILE * dseq
    ntp, nts = tp // PROMPT_TILE, ts // tm_s
    ppt_p = (TOP_K * PROMPT_TILE + N_EXPERTS + 1) // 2
    ppt_s = (TOP_K * tm_s + N_EXPERTS + 1) // 2
    (x1_p, xs_p, pos_p, gate_p, cnt_p, off_p, hist_p, _) = _mixer(
        x_prompt, zero_hist, mk_p[:, None], mv_p[:, None], wts_p,
        ns=1, sl=PROMPT_TILE, carry=True, sp_chunk=GMLP_CHUNK, pairs_per_tile=ppt_p)

    gs = bs // SAMPLE_SEQS_PER_TILE
    hist_s_in = jnp.pad(cache_conv[l], ((0, 0), (HIST_OFF, 0), (0, 0))).reshape(
        gs, SAMPLE_SEQS_PER_TILE, HIST, C_CONV)
    mk_s = cache_mem_k[l].reshape(gs, SAMPLE_SEQS_PER_TILE, N_MEM, C_XA)
    mv_s = cache_mem_v[l].reshape(gs, SAMPLE_SEQS_PER_TILE, N_MEM, C_XA)
    (x1_s, xs_s, pos_s, gate_s, cnt_s, off_s, hist_s, v_s) = _mixer(
        x_sample.reshape(gs, tm_s, D_MODEL), hist_s_in, mk_s, mv_s, wts_s,
        ns=SAMPLE_SEQS_PER_TILE, sl=dseq, carry=False, sp_chunk=dseq, pairs_per_tile=ppt_s)

    max_pairs = ((tp + ts) * TOP_K + (ntp + nts) * N_EXPERTS) // 2
    n_blocks = -(-max_pairs // MOE_PAIRS) + N_EXPERTS
    cnt = jnp.concatenate([cnt_p[:, :, 0], cnt_s[:, :, 0]], axis=0)
    off = jnp.concatenate([off_p[:, :, 0], off_s[:, :, 0]], axis=0)
    tile_pair0 = np.concatenate([np.arange(ntp) * ppt_p, np.arange(nts) * ppt_s]).astype(np.int32)
    counts = jnp.sum(cnt, axis=0)
    tile_base = jnp.cumsum(cnt, axis=0) - cnt
    padded = (counts + MOE_PAIRS - 1) // MOE_PAIRS * MOE_PAIRS
    pad_end = jnp.cumsum(padded)
    pad_start = pad_end - padded
    strip_dst = (pad_start[None, :] + tile_base).astype(I32)
    strip_src = (tile_pair0[:, None] + off).astype(I32)
    n_used = (pad_end[-1] // MOE_PAIRS).astype(I32)
    blk_pair0 = jnp.minimum(jnp.arange(n_blocks, dtype=I32), n_used - 1) * MOE_PAIRS
    block_expert = jnp.minimum(
        jnp.sum((pad_end[None, :] <= blk_pair0[:, None]).astype(I32), axis=1), N_EXPERTS - 1)
    of_block = block_expert[:, None] == jnp.arange(N_EXPERTS, dtype=I32)[None, :]
    last_pair = jnp.sum(jnp.where(of_block, (pad_start + counts)[None, :], 0), axis=1)
    block_valid = jnp.clip(last_pair - blk_pair0, 0, MOE_PAIRS).astype(I32)
    def strip_group(t0, t1):
        sdst = strip_dst[t0:t1].T.reshape(-1)
        ssrc = strip_src[t0:t1].T.reshape(-1)
        slen = cnt[t0:t1].T.reshape(-1).astype(I32)
        s_lo = jnp.sum(((sdst + slen)[None, :] <= blk_pair0[:, None]).astype(I32), axis=1)
        s_hi = jnp.sum((sdst[None, :] < (blk_pair0 + MOE_PAIRS)[:, None]).astype(I32), axis=1)
        return (s_lo, s_hi, ssrc, sdst, slen)

    moe_tables = ((pad_start // MOE_PAIRS).astype(I32), (padded // MOE_PAIRS).astype(I32),
                  block_valid, n_used.reshape(1),
                  *strip_group(0, ntp), *strip_group(ntp, ntp + nts))

    ys = _moe(moe_tables, xs_p, xs_s, n_blocks,
              w_gate_up[l], b_gate_up[l][:, None, :], w_down[l], b_down[l][:, None, :])
    gfin = row(g_final)

    def combine_tables(t0, t1):
        return (strip_dst[t0:t1].reshape(-1), cnt[t0:t1].reshape(-1).astype(I32),
                off[t0:t1].reshape(-1).astype(I32), jnp.sum(cnt[t0:t1], axis=1).astype(I32))

    y_p = _combine(combine_tables(0, ntp), x1_p, pos_p, gate_p, gfin, ys, PROMPT_TILE, ppt_p)
    y_s = _combine(combine_tables(ntp, ntp + nts), x1_s, pos_s, gate_s, gfin, ys, tm_s, ppt_s)

    return (y_p.reshape(bp, seq, D_MODEL),
            y_s.reshape(bs, dseq, D_MODEL),
            hist_p[:, 0, HIST_OFF:, :][None],
            mk_p.reshape(bp, N_MEM, XA_HEADS, XA_HD)[None],
            mv_p.reshape(bp, N_MEM, XA_HEADS, XA_HD)[None],
            hist_s.reshape(bs, HIST, C_CONV)[:, HIST_OFF:, :][None],
            v_s.reshape(bs, dseq, C_GMLP)[None])
```

```python
import functools

import numpy as np
import jax
import jax.numpy as jnp
from jax import lax
from jax.experimental import pallas as pl
from jax.experimental.pallas import tpu as pltpu

F32 = jnp.float32
BF16 = jnp.bfloat16
I32 = jnp.int32
U32 = jnp.uint32

D_MODEL = 1024
C_CONV = 384
CONV_W = 31
C_GMLP = 384
GMLP_HEADS = 4
GMLP_HD = 96
GMLP_CHUNK = 128
XA_HEADS = 4
XA_HD = 64
C_XA = 256
N_MEM = 256
N_EXPERTS = 32
TOP_K = 4
D_FF = 1024
SWIGLU_LIMIT = 7.0
SWIGLU_ALPHA = 1.702
EPS = 1e-5
IN_COLS = 2 * C_CONV + 2 * C_GMLP + C_XA

SUBLANES = 8
LANES = 128
ROW_TILES = D_MODEL // LANES
HIST = 32
HIST_OFF = HIST - (CONV_W - 1)

PROMPT_TILE = 512
SAMPLE_SEQS_PER_TILE = 8
MOE_BLOCK = 512
MOE_PAIRS = MOE_BLOCK // 2
GATHER_AHEAD = 2
VMEM_LIMIT_MIXER = 48 * 1024 * 1024
VMEM_LIMIT_MOE = 52 * 1024 * 1024


def _rmsnorm(x, g):
    return x * lax.rsqrt(jnp.mean(x * x, axis=-1, keepdims=True) + EPS) * g


def _layernorm(x, g, b):
    mu = jnp.mean(x, axis=-1, keepdims=True)
    xc = x - mu
    var = jnp.mean(xc * xc, axis=-1, keepdims=True)
    return xc * lax.rsqrt(var + EPS) * g + b


def _gelu(x):
    return 0.5 * x * (1.0 + lax.erf(x * np.float32(1.0 / np.sqrt(2.0))))


def _sigmoid(x):
    return 1.0 / (1.0 + jnp.exp(-x))


def _rows_to_tiles(dst_ref, val, rows, row0=0):
    for j in range(ROW_TILES):
        dst_ref[pl.ds(row0 * ROW_TILES + j, rows, stride=ROW_TILES), :] = val[:, j * LANES:(j + 1) * LANES]


def _tiles_to_rows(src_ref, rows, row0=0):
    return jnp.concatenate(
        [src_ref[pl.ds(row0 * ROW_TILES + j, rows, stride=ROW_TILES), :] for j in range(ROW_TILES)],
        axis=-1)


def _memkv_kernel(mem_ref, g_ref, w_ref, o_ref):
    mn = _rmsnorm(mem_ref[0], g_ref[...])
    o_ref[0] = jnp.dot(mn.astype(BF16), w_ref[...], preferred_element_type=F32)


def _memkv(mem, g_mem, w_kv):
    b = mem.shape[0]
    return pl.pallas_call(
        _memkv_kernel,
        grid=(b,),
        in_specs=[pl.BlockSpec((1, N_MEM, D_MODEL), lambda i: (i, 0, 0)),
                  pl.BlockSpec((1, D_MODEL), lambda i: (0, 0)),
                  pl.BlockSpec((D_MODEL, 2 * C_XA), lambda i: (0, 0))],
        out_specs=pl.BlockSpec((1, N_MEM, 2 * C_XA), lambda i: (i, 0, 0)),
        out_shape=jax.ShapeDtypeStruct((b, N_MEM, 2 * C_XA), F32),
        name="memkv",
    )(mem, g_mem, w_kv)


def _mixer_kernel(x_ref, hist_ref, mk_ref, mv_ref, gmix_ref, win_ref, wdw_ref, bdw_ref,
                  lcg_ref, lcb_ref, lvg_ref, lvb_ref, wsp_ref, bsp_ref, wout_ref, gffn_ref,
                  wrh_ref, wrl_ref, br_ref, *rest, ns, sl, carry, sp_chunk, tiles_per_seq, n_sorted):
    (x1_ref, xst_ref, pos_ref, gate_ref, cnt_ref, off_ref, histout_ref, v_ref,
     ext_ref, shift_ref, h2_keep, pos_keep, before_ref, wsp16_ref) = rest
    tm = ns * sl
    step = pl.program_id(0)
    n_tiles = pl.num_programs(0) - 1

    def sort_previous_tile():
        jj = lax.broadcasted_iota(I32, (n_sorted, tm), 0)
        perm = jnp.zeros((n_sorted, tm), F32)
        for k in range(TOP_K):
            perm = jnp.where(jj == pos_keep[k:k + 1, :], 1.0, perm)
        sorted_rows = jnp.dot(perm.astype(BF16), h2_keep[...], preferred_element_type=F32)
        pairs = pltpu.bitcast(sorted_rows.astype(BF16), U32)
        _rows_to_tiles(xst_ref, pairs, n_sorted // 2)

    @pl.when(step == 0)
    def _():
        h2_keep[...] = jnp.zeros_like(h2_keep)
        pos_keep[...] = jnp.full(pos_keep.shape, -1, I32)
        tr = lax.broadcasted_iota(I32, (tm, tm), 0)
        tc = lax.broadcasted_iota(I32, (tm, tm), 1)
        before_ref[...] = jnp.where(tr < tc, 1.0, 0.0).astype(BF16)
        rr = lax.broadcasted_iota(I32, (GMLP_CHUNK, GMLP_CHUNK), 0)
        cc = lax.broadcasted_iota(I32, (GMLP_CHUNK, GMLP_CHUNK), 1)
        sp_mask = (cc <= rr) & ((rr // sp_chunk) == (cc // sp_chunk))
        for hh in range(GMLP_HEADS):
            wsp16_ref[hh] = jnp.where(sp_mask, wsp_ref[hh], 0.0).astype(BF16)

    @pl.when(step < n_tiles)
    def _():
        sort_previous_tile()
        _mixer_tile(x_ref, hist_ref, mk_ref, mv_ref, gmix_ref, win_ref, wdw_ref, bdw_ref,
                    lcg_ref, lcb_ref, lvg_ref, lvb_ref, wsp16_ref, bsp_ref, wout_ref, gffn_ref,
                    wrh_ref, wrl_ref, br_ref, x1_ref, pos_ref, gate_ref, cnt_ref, off_ref,
                    histout_ref, v_ref, ext_ref, shift_ref, h2_keep, pos_keep, before_ref,
                    first_of_seq=(step % tiles_per_seq) == 0,
                    ns=ns, sl=sl, carry=carry)

    @pl.when(step == n_tiles)
    def _():
        sort_previous_tile()


def _mixer_tile(x_ref, hist_ref, mk_ref, mv_ref, gmix_ref, win_ref, wdw_ref, bdw_ref,
                lcg_ref, lcb_ref, lvg_ref, lvb_ref, wsp16_ref, bsp_ref, wout_ref, gffn_ref,
                wrh_ref, wrl_ref, br_ref, x1_ref, pos_ref, gate_ref, cnt_ref, off_ref,
                histout_ref, v_ref, ext_ref, shift_ref, h2_keep, pos_keep, before_ref,
                *, first_of_seq, ns, sl, carry):
    tm = ns * sl
    x = x_ref[0]
    h = _rmsnorm(x, gmix_ref[...])
    z = jnp.dot(h.astype(BF16), win_ref[...], preferred_element_type=F32)
    z_a = z[:, 0:C_CONV]
    z_g = z[:, C_CONV:2 * C_CONV]
    z_u = z[:, 2 * C_CONV:2 * C_CONV + C_GMLP]
    z_v = z[:, 2 * C_CONV + C_GMLP:2 * C_CONV + 2 * C_GMLP]
    z_q = z[:, 2 * C_CONV + 2 * C_GMLP:IN_COLS]

    glu = z_a * _sigmoid(z_g)
    if carry:
        ext_ref[:, 0:HIST, :] = jnp.where(first_of_seq, hist_ref[0], ext_ref[:, 0:HIST, :])
    else:
        ext_ref[:, 0:HIST, :] = hist_ref[0]
    conv_parts = []
    for s in range(ns):
        ext_s = ext_ref.at[s]
        ext_s[HIST:HIST + sl, :] = glu[s * sl:(s + 1) * sl]
        n_shift = HIST + sl - SUBLANES
        for r in range(1, SUBLANES):
            shift_ref[s, r - 1, 0:n_shift, :] = ext_s[pl.ds(r, n_shift), :]
        rc = min(sl, 64)
        for r0 in range(0, sl, rc):
            acc = jnp.broadcast_to(bdw_ref[...], (rc, C_CONV))
            for j in range(CONV_W):
                a, r = divmod(j + HIST_OFF, SUBLANES)
                src = ext_s if r == 0 else shift_ref.at[s, r - 1]
                acc = acc + wdw_ref[j:j + 1, :] * src[pl.ds(r0 + a * SUBLANES, rc), :]
            conv_parts.append(acc)
        new_hist = ext_s[sl:sl + HIST, :]
        histout_ref[0, s] = new_hist
        if carry:
            ext_s[0:HIST, :] = new_hist
    y = jnp.concatenate(conv_parts, axis=0) if len(conv_parts) > 1 else conv_parts[0]
    y = _layernorm(y, lcg_ref[...], lcb_ref[...])
    c_out = y * _sigmoid(y)

    u = _gelu(z_u)
    v = _layernorm(_gelu(z_v), lvg_ref[...], lvb_ref[...])
    v_ref[...] = v
    vb = v.astype(BF16)
    col = lax.broadcasted_iota(I32, (GMLP_CHUNK, C_GMLP), 1)
    w_heads = [wsp16_ref[hh] for hh in range(GMLP_HEADS)]
    g_parts = []
    for c in range(tm // GMLP_CHUNK):
        vc = vb[c * GMLP_CHUNK:(c + 1) * GMLP_CHUNK]
        sg = bsp_ref[...]
        for hh in range(GMLP_HEADS):
            head_cols = (col >= hh * GMLP_HD) & (col < (hh + 1) * GMLP_HD)
            vh = jnp.where(head_cols, vc, jnp.zeros_like(vc))
            sg = sg + jnp.dot(w_heads[hh], vh, preferred_element_type=F32)
        g_parts.append(u[c * GMLP_CHUNK:(c + 1) * GMLP_CHUNK] * sg)
    g_out = jnp.concatenate(g_parts, axis=0) if len(g_parts) > 1 else g_parts[0]

    qs = z_q * np.float32(XA_HD ** -0.5)
    qcol = lax.broadcasted_iota(I32, (sl, C_XA), 1)
    a_parts = []
    for s in range(ns):
        q_s = qs[s * sl:(s + 1) * sl]
        kb = mk_ref[0, s].astype(BF16)
        vvb = mv_ref[0, s].astype(BF16)
        hmasks = [(qcol >= hh * XA_HD) & (qcol < (hh + 1) * XA_HD) for hh in range(XA_HEADS)]
        stack = XA_HEADS if sl * XA_HEADS <= N_MEM else 1
        a_s = jnp.zeros((sl, C_XA), F32)
        for h0 in range(0, XA_HEADS, stack):
            heads = range(h0, h0 + stack)
            qh = jnp.concatenate([jnp.where(hmasks[hh], q_s, 0.0) for hh in heads], axis=0).astype(BF16)
            sc = lax.dot_general(qh, kb, (((1,), (1,)), ((), ())), preferred_element_type=F32)
            p = jnp.exp(sc - jnp.max(sc, axis=-1, keepdims=True))
            den = jnp.sum(p, axis=-1, keepdims=True)
            oh = jnp.dot(p.astype(BF16), vvb, preferred_element_type=F32) / den
            for n, hh in enumerate(heads):
                a_s = a_s + jnp.where(hmasks[hh], oh[n * sl:(n + 1) * sl], 0.0)
        a_parts.append(a_s)
    a_out = jnp.concatenate(a_parts, axis=0) if len(a_parts) > 1 else a_parts[0]

    mix = jnp.concatenate([c_out, g_out, a_out], axis=-1).astype(BF16)
    x1 = x + jnp.dot(mix, wout_ref[...], preferred_element_type=F32)
    x1_ref[...] = x1

    h2 = _rmsnorm(x1, gffn_ref[...])
    h2_hi = h2.astype(BF16)
    h2_lo = (h2 - h2_hi.astype(F32)).astype(BF16)
    nt_dims = (((1,), (1,)), ((), ()))
    lg = (lax.dot_general(wrh_ref[...], h2_hi, nt_dims, preferred_element_type=F32)
          + lax.dot_general(wrl_ref[...], h2_hi, nt_dims, preferred_element_type=F32)
          + lax.dot_general(wrh_ref[...], h2_lo, nt_dims, preferred_element_type=F32)
          + br_ref[...])
    eio = lax.broadcasted_iota(I32, (N_EXPERTS, tm), 0)
    work = lg
    vals, idxs = [], []
    for _ in range(TOP_K):
        m = jnp.max(work, axis=0, keepdims=True)
        idx = jnp.min(jnp.where(work == m, eio, N_EXPERTS), axis=0, keepdims=True)
        vals.append(m)
        idxs.append(idx)
        work = jnp.where(eio == idx, -jnp.inf, work)
    exps = [jnp.exp(vk - vals[0]) for vk in vals]
    den = exps[0] + exps[1] + exps[2] + exps[3]
    gate_ref[...] = jnp.concatenate([ek / den for ek in exps], axis=0)

    sel = jnp.zeros((N_EXPERTS, tm), F32)
    for idx in idxs:
        sel = sel + jnp.where(eio == idx, 1.0, 0.0)
    selb = sel.astype(BF16)
    ranks = jnp.dot(selb, before_ref[...], preferred_element_type=F32)
    er = lax.broadcasted_iota(I32, (N_EXPERTS, N_EXPERTS), 0)
    ec = lax.broadcasted_iota(I32, (N_EXPERTS, N_EXPERTS), 1)
    lower = jnp.where(ec < er, 1.0, 0.0).astype(BF16)
    cnt = jnp.sum(sel, axis=1, keepdims=True)
    half_len = jnp.floor((cnt + 1.0) * 0.5)
    off_pairs = jnp.dot(lower, jnp.broadcast_to(half_len, (N_EXPERTS, LANES)).astype(BF16),
                        preferred_element_type=F32)[:, 0:1]
    cnt_ref[0] = jnp.broadcast_to(half_len, (N_EXPERTS, LANES)).astype(I32)
    off_ref[0] = jnp.broadcast_to(off_pairs, (N_EXPERTS, LANES)).astype(I32)
    slot_of = 2.0 * off_pairs + ranks
    pos = [jnp.sum(jnp.where(eio == idx, slot_of, 0.0), axis=0, keepdims=True).astype(I32) for idx in idxs]
    pos_all = jnp.concatenate(pos, axis=0)
    pos_ref[...] = pos_all
    pos_keep[...] = pos_all
    h2_keep[...] = h2_hi


def _mixer(x, hist, mem_k, mem_v, wts, *, ns, sl, carry, sp_chunk, pairs_per_tile):
    g, r, _ = x.shape
    tm = ns * sl
    nt = r // tm
    ntot = g * nt
    assert 2 * pairs_per_tile >= TOP_K * tm + N_EXPERTS
    tile = lambda s: jnp.minimum(s, ntot - 1)
    const2 = lambda s: (0, 0)
    const3 = lambda s: (0, 0, 0)
    tile_row = lambda s: (tile(s), 0)
    tile_lane = lambda s: (0, tile(s))
    per_seq = lambda s: (tile(s) // nt, 0, 0, 0)
    in_specs = [
        pl.BlockSpec((1, tm, D_MODEL), lambda s: (tile(s) // nt, tile(s) % nt, 0)),
        pl.BlockSpec((1, ns, HIST, C_CONV), per_seq),
        pl.BlockSpec((1, ns, N_MEM, C_XA), per_seq),
        pl.BlockSpec((1, ns, N_MEM, C_XA), per_seq),
        pl.BlockSpec((1, D_MODEL), const2),
        pl.BlockSpec((D_MODEL, IN_COLS), const2),
        pl.BlockSpec((HIST, C_CONV), const2),
        pl.BlockSpec((1, C_CONV), const2),
        pl.BlockSpec((1, C_CONV), const2),
        pl.BlockSpec((1, C_CONV), const2),
        pl.BlockSpec((1, C_GMLP), const2),
        pl.BlockSpec((1, C_GMLP), const2),
        pl.BlockSpec((GMLP_HEADS, GMLP_CHUNK, GMLP_CHUNK), const3),
        pl.BlockSpec((GMLP_CHUNK, C_GMLP), const2),
        pl.BlockSpec((D_MODEL, D_MODEL), const2),
        pl.BlockSpec((1, D_MODEL), const2),
        pl.BlockSpec((N_EXPERTS, D_MODEL), const2),
        pl.BlockSpec((N_EXPERTS, D_MODEL), const2),
        pl.BlockSpec((N_EXPERTS, 1), const2),
    ]
    tile_cnt = lambda s: (tile(s), 0, 0)
    out_specs = [
        pl.BlockSpec((tm, D_MODEL), tile_row),
        pl.BlockSpec((pairs_per_tile * ROW_TILES, LANES),
                     lambda s: (jnp.maximum(s - 1, 0), 0)),
        pl.BlockSpec((TOP_K, tm), tile_lane),
        pl.BlockSpec((TOP_K, tm), tile_lane),
        pl.BlockSpec((1, N_EXPERTS, LANES), tile_cnt),
        pl.BlockSpec((1, N_EXPERTS, LANES), tile_cnt),
        pl.BlockSpec((1, ns, HIST, C_CONV), per_seq),
        pl.BlockSpec((tm, C_GMLP), tile_row),
    ]
    rows = g * r
    out_shape = [
        jax.ShapeDtypeStruct((rows, D_MODEL), F32),
        jax.ShapeDtypeStruct((ntot * pairs_per_tile * ROW_TILES, LANES), U32),
        jax.ShapeDtypeStruct((TOP_K, rows), I32),
        jax.ShapeDtypeStruct((TOP_K, rows), F32),
        jax.ShapeDtypeStruct((ntot, N_EXPERTS, LANES), I32),
        jax.ShapeDtypeStruct((ntot, N_EXPERTS, LANES), I32),
        jax.ShapeDtypeStruct((g, ns, HIST, C_CONV), F32),
        jax.ShapeDtypeStruct((rows, C_GMLP), F32),
    ]
    kern = functools.partial(_mixer_kernel, ns=ns, sl=sl, carry=carry, sp_chunk=sp_chunk,
                             tiles_per_seq=nt, n_sorted=2 * pairs_per_tile)
    return pl.pallas_call(
        kern,
        grid=(ntot + 1,),
        in_specs=in_specs,
        out_specs=out_specs,
        out_shape=out_shape,
        scratch_shapes=[pltpu.VMEM((ns, HIST + sl, C_CONV), F32),
                        pltpu.VMEM((ns, SUBLANES - 1, HIST + sl, C_CONV), F32),
                        pltpu.VMEM((tm, D_MODEL), BF16),
                        pltpu.VMEM((TOP_K, tm), I32),
                        pltpu.VMEM((tm, tm), BF16),
                        pltpu.VMEM((GMLP_HEADS, GMLP_CHUNK, GMLP_CHUNK), BF16)],
        compiler_params=pltpu.CompilerParams(
            dimension_semantics=("arbitrary",),
            vmem_limit_bytes=VMEM_LIMIT_MIXER),
        name="mixer_carry" if carry else "mixer_cache",
    )(x, hist, mem_k, mem_v, *wts)


def _strip_pieces(n, max_rows, fn):
    done = 0
    p = max_rows
    while p >= 1:
        has = (n & p) != 0
        pl.when(has)(functools.partial(fn, done, p))
        done = done + jnp.where(has, p, 0)
        p //= 2


class _Table:
    def __init__(self, ref, offset):
        self.ref, self.offset = ref, offset

    def __getitem__(self, i):
        return self.ref[self.offset + i]


def _pack_tables(tables):
    offsets = tuple(int(o) for o in np.cumsum([0] + [t.shape[0] for t in tables[:-1]]))
    return jnp.concatenate([t.astype(I32) for t in tables]), offsets


def _pow2_at_most(n):
    return 1 << (n.bit_length() - 1)


def _rows(ref, first_row, rows):
    return ref.at[pl.ds(pl.multiple_of(first_row * ROW_TILES, ROW_TILES), rows * ROW_TILES)]


def _moe_kernel(tab_ref, xs_hbm, xs2_hbm, wgu_ref, bgu_ref, wdn_ref, bdn_ref, ys_hbm,
                xbuf, ybuf, wgu16, wdn16, sems, ysems, *, offsets, n_blocks):
    tabs = [_Table(tab_ref, o) for o in offsets]
    b0_ref, nbk_ref, bv_ref, nb_ref = tabs[:4]
    strips_a, strips_b = tabs[4:9], tabs[9:14]
    e = pl.program_id(0)
    n_used = nb_ref[0]
    half = MOE_PAIRS // 2

    def gather_start(b, slot):
        pair0 = b * MOE_PAIRS

        for src_hbm, (slo_ref, shi_ref, ssrc_ref, sdst_ref, slen_ref) in ((xs_hbm, strips_a), (xs2_hbm, strips_b)):
            def strip(s, carry, src_hbm=src_hbm, ssrc_ref=ssrc_ref, sdst_ref=sdst_ref, slen_ref=slen_ref):
                lo = jnp.maximum(sdst_ref[s], pair0)
                hi = jnp.minimum(sdst_ref[s] + slen_ref[s], pair0 + MOE_PAIRS)
                src0 = ssrc_ref[s] + (lo - sdst_ref[s])
                dst0 = lo - pair0

                def piece(first, rows):
                    pltpu.make_async_copy(_rows(src_hbm, src0 + first, rows),
                                          _rows(xbuf.at[slot], dst0 + first, rows), sems.at[slot]).start()
                _strip_pieces(hi - lo, MOE_PAIRS, piece)
                return carry
            lax.fori_loop(slo_ref[b], shi_ref[b], strip, 0)

    def gather_wait(b, slot):
        def piece(first, rows):
            del first
            pltpu.make_async_copy(_rows(xs_hbm, 0, rows), _rows(xbuf.at[slot], 0, rows), sems.at[slot]).wait()
        _strip_pieces(bv_ref[b], MOE_PAIRS, piece)

    def out_copy(b, slot):
        return pltpu.make_async_copy(ybuf.at[slot], _rows(ys_hbm, b * MOE_PAIRS, MOE_PAIRS), ysems.at[slot])

    def expert_mlp(xslot, slot, h):
        xb = pltpu.bitcast(_tiles_to_rows(xbuf.at[xslot], half, h * half), BF16)
        gu = jnp.dot(xb, wgu16[...], preferred_element_type=F32) + bgu_ref[0]
        gate = jnp.minimum(gu[:, :D_FF], SWIGLU_LIMIT)
        up = jnp.clip(gu[:, D_FF:], -SWIGLU_LIMIT, SWIGLU_LIMIT)
        act = (up + 1.0) * (gate * _sigmoid(SWIGLU_ALPHA * gate))
        yb = jnp.dot(act.astype(BF16), wdn16[...], preferred_element_type=F32) + bdn_ref[0]
        _rows_to_tiles(ybuf.at[slot], pltpu.bitcast(yb.astype(BF16), U32), half, h * half)

    @pl.when(e == 0)
    def _():
        xbuf[...] = jnp.zeros_like(xbuf)
        for a in range(GATHER_AHEAD):
            @pl.when(a < n_used)
            def _():
                gather_start(a, a)

    @pl.when(nbk_ref[e] > 0)
    def _():
        def cast_rows(c, carry):
            r = pl.multiple_of(c * LANES, LANES)
            wgu16[pl.ds(r, LANES), :] = wgu_ref[pl.ds(r, LANES), :].astype(BF16)
            wdn16[pl.ds(r, LANES), :] = wdn_ref[pl.ds(r, LANES), :].astype(BF16)
            return carry
        lax.fori_loop(0, D_MODEL // LANES, cast_rows, 0)

        def block(j, carry):
            b = b0_ref[e] + j
            slot = b % 2
            xslot = b % (GATHER_AHEAD + 1)

            @pl.when(b + GATHER_AHEAD < n_used)
            def _():
                gather_start(b + GATHER_AHEAD, (b + GATHER_AHEAD) % (GATHER_AHEAD + 1))

            gather_wait(b, xslot)

            @pl.when(b >= 2)
            def _():
                out_copy(b - 2, slot).wait()

            valid = bv_ref[b]

            @pl.when(valid > half)
            def _():
                expert_mlp(xslot, slot, 0)
                expert_mlp(xslot, slot, 1)

            @pl.when(valid <= half)
            def _():
                expert_mlp(xslot, slot, 0)
                ybuf[slot, pl.ds(half * ROW_TILES, half * ROW_TILES), :] = jnp.zeros(
                    (half * ROW_TILES, LANES), U32)

            out_copy(b, slot).start()
            return carry
        lax.fori_loop(0, nbk_ref[e], block, 0)

    @pl.when(e == pl.num_programs(0) - 1)
    def _():
        for back in (1, 2):
            @pl.when(n_used >= back)
            def _():
                out_copy(n_used - back, (n_used - back) % 2).wait()
        ybuf[0] = jnp.zeros(ybuf.shape[1:], U32)

        def start_fill(b, carry):
            out_copy(b, 0).start()
            return carry

        def wait_fill(b, carry):
            out_copy(b, 0).wait()
            return carry
        lax.fori_loop(n_used, n_blocks, start_fill, 0)
        lax.fori_loop(n_used, n_blocks, wait_fill, 0)


def _moe(tables, xs, xs2, n_blocks, w_gu, b_gu, w_dn, b_dn):
    wsel = lambda e, *_: (e, 0, 0)
    packed, offsets = _pack_tables(tables)
    slots = n_blocks * MOE_PAIRS
    grid_spec = pltpu.PrefetchScalarGridSpec(
        num_scalar_prefetch=1,
        grid=(N_EXPERTS,),
        in_specs=[pl.BlockSpec(memory_space=pl.ANY),
                  pl.BlockSpec(memory_space=pl.ANY),
                  pl.BlockSpec((None, D_MODEL, 2 * D_FF), wsel),
                  pl.BlockSpec((None, 1, 2 * D_FF), wsel),
                  pl.BlockSpec((None, D_FF, D_MODEL), wsel),
                  pl.BlockSpec((None, 1, D_MODEL), wsel)],
        out_specs=pl.BlockSpec(memory_space=pl.ANY),
        scratch_shapes=[pltpu.VMEM((GATHER_AHEAD + 1, MOE_PAIRS * ROW_TILES, LANES), U32),
                        pltpu.VMEM((2, MOE_PAIRS * ROW_TILES, LANES), U32),
                        pltpu.VMEM((D_MODEL, 2 * D_FF), BF16),
                        pltpu.VMEM((D_FF, D_MODEL), BF16),
                        pltpu.SemaphoreType.DMA((GATHER_AHEAD + 1,)),
                        pltpu.SemaphoreType.DMA((2,))],
    )
    return pl.pallas_call(
        functools.partial(_moe_kernel, offsets=offsets, n_blocks=n_blocks),
        grid_spec=grid_spec,
        out_shape=jax.ShapeDtypeStruct((slots * ROW_TILES, LANES), U32),
        compiler_params=pltpu.CompilerParams(
            dimension_semantics=("arbitrary",),
            vmem_limit_bytes=VMEM_LIMIT_MOE),
        name="moe",
    )(packed, xs, xs2, w_gu, b_gu, w_dn, b_dn)


def _combine_kernel(tab_ref, x1_ref, pos_ref, gate_ref, gfin_ref, ys_hbm,
                    out_ref, ybuf, sems, *, tm, pairs_per_tile, offsets):
    csrc_ref, clen_ref, coff_ref, ctot_ref = [_Table(tab_ref, o) for o in offsets]
    i = pl.program_id(0)
    n_tiles = pl.num_programs(0)
    n_sorted = 2 * pairs_per_tile

    def gather_start(t, slot):
        def strip(e, carry):
            s = t * N_EXPERTS + e
            src0 = csrc_ref[s]
            dst0 = coff_ref[s]

            def piece(first, rows):
                pltpu.make_async_copy(_rows(ys_hbm, src0 + first, rows),
                                      _rows(ybuf.at[slot], dst0 + first, rows), sems.at[slot]).start()
            _strip_pieces(clen_ref[s], tm // 2, piece)
            return carry
        lax.fori_loop(0, N_EXPERTS, strip, 0)

    @pl.when(i == 0)
    def _():
        ybuf[...] = jnp.zeros_like(ybuf)
        gather_start(0, 0)

    @pl.when(i + 1 < n_tiles)
    def _():
        gather_start(i + 1, (i + 1) % 2)

    slot = i % 2

    def wait_piece(first, rows):
        del first
        pltpu.make_async_copy(_rows(ys_hbm, 0, rows), _rows(ybuf.at[slot], 0, rows), sems.at[slot]).wait()
    _strip_pieces(ctot_ref[i], _pow2_at_most(pairs_per_tile), wait_piece)

    y_sorted = pltpu.bitcast(_tiles_to_rows(ybuf.at[slot], pairs_per_tile), BF16)

    rr = lax.broadcasted_iota(I32, (tm, tm), 0)
    cc = lax.broadcasted_iota(I32, (tm, tm), 1)
    eye = rr == cc
    jl = lax.broadcasted_iota(I32, (tm, n_sorted), 1).astype(F32)
    unsort = jnp.zeros((tm, n_sorted), F32)
    for k in range(TOP_K):
        p_col = jnp.sum(jnp.where(eye, pos_ref[k:k + 1, :].astype(F32), 0.0), axis=1, keepdims=True)
        g_col = jnp.sum(jnp.where(eye, gate_ref[k:k + 1, :], 0.0), axis=1, keepdims=True)
        unsort = jnp.where(jl == p_col, g_col, unsort)
    acc = x1_ref[...] + jnp.dot(unsort.astype(BF16), y_sorted, preferred_element_type=F32)
    out_ref[...] = _rmsnorm(acc, gfin_ref[...])


def _combine(tables, x1, pos, gates, g_final, ys, tm, pairs_per_tile):
    t = x1.shape[0]
    nt = t // tm
    packed, offsets = _pack_tables(tables)
    grid_spec = pltpu.PrefetchScalarGridSpec(
        num_scalar_prefetch=1,
        grid=(nt,),
        in_specs=[pl.BlockSpec((tm, D_MODEL), lambda i, *_: (i, 0)),
                  pl.BlockSpec((TOP_K, tm), lambda i, *_: (0, i)),
                  pl.BlockSpec((TOP_K, tm), lambda i, *_: (0, i)),
                  pl.BlockSpec((1, D_MODEL), lambda i, *_: (0, 0)),
                  pl.BlockSpec(memory_space=pl.ANY)],
        out_specs=pl.BlockSpec((tm, D_MODEL), lambda i, *_: (i, 0)),
        scratch_shapes=[pltpu.VMEM((2, pairs_per_tile * ROW_TILES, LANES), U32),
                        pltpu.SemaphoreType.DMA((2,))],
    )
    return pl.pallas_call(
        functools.partial(_combine_kernel, tm=tm, pairs_per_tile=pairs_per_tile, offsets=offsets),
        grid_spec=grid_spec,
        out_shape=jax.ShapeDtypeStruct((t, D_MODEL), F32),
        compiler_params=pltpu.CompilerParams(
            dimension_semantics=("arbitrary",),
            vmem_limit_bytes=VMEM_LIMIT_MIXER),
        name="combine",
    )(packed, x1, pos, gates, g_final, ys)


def _split_bf16(w):
    hi = w.astype(BF16)
    lo = (w - hi.astype(F32)).astype(BF16)
    return hi, lo


def kernel(x_prompt, x_sample, cache_conv, cache_mem_k, cache_mem_v, mem_prompt, g_mix, w_in, w_dw, b_dw, ln_conv_g, ln_conv_b, ln_v_g, ln_v_b, w_spatial, b_spatial, g_mem, w_mem_k, w_mem_v, w_out, g_ffn, w_router, b_router, w_gate_up, b_gate_up, w_down, b_down, g_final):
    depth = g_mix.shape[0]
    assert depth == 1
    l = 0
    bp, seq, _ = x_prompt.shape
    bs, dseq, _ = x_sample.shape
    assert seq % PROMPT_TILE == 0 and bs % SAMPLE_SEQS_PER_TILE == 0
    assert GMLP_CHUNK % dseq == 0 and (SAMPLE_SEQS_PER_TILE * dseq) % GMLP_CHUNK == 0

    row = lambda a: a.reshape(1, -1)
    wr_hi, wr_lo = _split_bf16(w_router[l].T)
    w_dw_pad = jnp.pad(w_dw[l], ((0, HIST - CONV_W), (0, 0)))
    bias_rows = lambda b: jnp.repeat(b.T, GMLP_HD, axis=1)
    common = dict(
        gmix=row(g_mix[l]), win=w_in[l].astype(BF16), wdw=w_dw_pad, bdw=row(b_dw[l]),
        lcg=row(ln_conv_g[l]), lcb=row(ln_conv_b[l]), lvg=row(ln_v_g[l]), lvb=row(ln_v_b[l]),
        wout=w_out[l].astype(BF16), gffn=row(g_ffn[l]), wrh=wr_hi, wrl=wr_lo,
        br=b_router[l].reshape(N_EXPERTS, 1))

    def weights(wsp, bsp):
        c = common
        return (c["gmix"], c["win"], c["wdw"], c["bdw"], c["lcg"], c["lcb"], c["lvg"], c["lvb"],
                wsp, bsp, c["wout"], c["gffn"], c["wrh"], c["wrl"], c["br"])

    reps = GMLP_CHUNK // dseq
    wts_p = weights(w_spatial[l], bias_rows(b_spatial[l]))
    wts_s = weights(jnp.tile(w_spatial[l][:, :dseq, :dseq], (1, reps, reps)),
                    bias_rows(jnp.tile(b_spatial[l][:, :dseq], (1, reps))))

    w_kv = jnp.concatenate([w_mem_k[l], w_mem_v[l]], axis=1).astype(BF16)
    kv_p = _memkv(mem_prompt, row(g_mem[l]), w_kv)
    mk_p = kv_p[:, :, :C_XA]
    mv_p = kv_p[:, :, C_XA:]
    zero_hist = jnp.zeros((bp, 1, HIST, C_CONV), F32)
    tp, ts = bp * seq, bs * dseq
    tm_s = SAMPLE_SEQS_PER_TILE * dseq
    ntp, nts = tp // PROMPT_TILE, ts // tm_s
    ppt_p = (TOP_K * PROMPT_TILE + N_EXPERTS + 1) // 2
    ppt_s = (TOP_K * tm_s + N_EXPERTS + 1) // 2
    (x1_p, xs_p, pos_p, gate_p, cnt_p, off_p, hist_p, _) = _mixer(
        x_prompt, zero_hist, mk_p[:, None], mv_p[:, None], wts_p,
        ns=1, sl=PROMPT_TILE, carry=True, sp_chunk=GMLP_CHUNK, pairs_per_tile=ppt_p)

    gs = bs // SAMPLE_SEQS_PER_TILE
    hist_s_in = jnp.pad(cache_conv[l], ((0, 0), (HIST_OFF, 0), (0, 0))).reshape(
        gs, SAMPLE_SEQS_PER_TILE, HIST, C_CONV)
    mk_s = cache_mem_k[l].reshape(gs, SAMPLE_SEQS_PER_TILE, N_MEM, C_XA)
    mv_s = cache_mem_v[l].reshape(gs, SAMPLE_SEQS_PER_TILE, N_MEM, C_XA)
    (x1_s, xs_s, pos_s, gate_s, cnt_s, off_s, hist_s, v_s) = _mixer(
        x_sample.reshape(gs, tm_s, D_MODEL), hist_s_in, mk_s, mv_s, wts_s,
        ns=SAMPLE_SEQS_PER_TILE, sl=dseq, carry=False, sp_chunk=dseq, pairs_per_tile=ppt_s)

    max_pairs = ((tp + ts) * TOP_K + (ntp + nts) * N_EXPERTS) // 2
    n_blocks = -(-max_pairs // MOE_PAIRS) + N_EXPERTS
    cnt = jnp.concatenate([cnt_p[:, :, 0], cnt_s[:, :, 0]], axis=0)
    off = jnp.concatenate([off_p[:, :, 0], off_s[:, :, 0]], axis=0)
    tile_pair0 = np.concatenate([np.arange(ntp) * ppt_p, np.arange(nts) * ppt_s]).astype(np.int32)
    counts = jnp.sum(cnt, axis=0)
    tile_base = jnp.cumsum(cnt, axis=0) - cnt
    padded = (counts + MOE_PAIRS - 1) // MOE_PAIRS * MOE_PAIRS
    pad_end = jnp.cumsum(padded)
    pad_start = pad_end - padded
    strip_dst = (pad_start[None, :] + tile_base).astype(I32)
    strip_src = (tile_pair0[:, None] + off).astype(I32)
    n_used = (pad_end[-1] // MOE_PAIRS).astype(I32)
    blk_pair0 = jnp.minimum(jnp.arange(n_blocks, dtype=I32), n_used - 1) * MOE_PAIRS
    block_expert = jnp.minimum(
        jnp.sum((pad_end[None, :] <= blk_pair0[:, None]).astype(I32), axis=1), N_EXPERTS - 1)
    of_block = block_expert[:, None] == jnp.arange(N_EXPERTS, dtype=I32)[None, :]
    last_pair = jnp.sum(jnp.where(of_block, (pad_start + counts)[None, :], 0), axis=1)
    block_valid = jnp.clip(last_pair - blk_pair0, 0, MOE_PAIRS).astype(I32)
    def strip_group(t0, t1):
        sdst = strip_dst[t0:t1].T.reshape(-1)
        ssrc = strip_src[t0:t1].T.reshape(-1)
        slen = cnt[t0:t1].T.reshape(-1).astype(I32)
        s_lo = jnp.sum(((sdst + slen)[None, :] <= blk_pair0[:, None]).astype(I32), axis=1)
        s_hi = jnp.sum((sdst[None, :] < (blk_pair0 + MOE_PAIRS)[:, None]).astype(I32), axis=1)
        return (s_lo, s_hi, ssrc, sdst, slen)

    moe_tables = ((pad_start // MOE_PAIRS).astype(I32), (padded // MOE_PAIRS).astype(I32),
                  block_valid, n_used.reshape(1),
                  *strip_group(0, ntp), *strip_group(ntp, ntp + nts))

    ys = _moe(moe_tables, xs_p, xs_s, n_blocks,
              w_gate_up[l], b_gate_up[l][:, None, :], w_down[l], b_down[l][:, None, :])
    gfin = row(g_final)

    def combine_tables(t0, t1):
        return (strip_dst[t0:t1].reshape(-1), cnt[t0:t1].reshape(-1).astype(I32),
                off[t0:t1].reshape(-1).astype(I32), jnp.sum(cnt[t0:t1], axis=1).astype(I32))

    y_p = _combine(combine_tables(0, ntp), x1_p, pos_p, gate_p, gfin, ys, PROMPT_TILE, ppt_p)
    y_s = _combine(combine_tables(ntp, ntp + nts), x1_s, pos_s, gate_s, gfin, ys, tm_s, ppt_s)

    return (y_p.reshape(bp, seq, D_MODEL),
            y_s.reshape(bs, dseq, D_MODEL),
            hist_p[:, 0, HIST_OFF:, :][None],
            mk_p.reshape(bp, N_MEM, XA_HEADS, XA_HD)[None],
            mv_p.reshape(bp, N_MEM, XA_HEADS, XA_HD)[None],
            hist_s.reshape(bs, HIST, C_CONV)[:, HIST_OFF:, :][None],
            v_s.reshape(bs, dseq, C_GMLP)[None])
```

```python
import functools

import numpy as np
import jax
import jax.numpy as jnp
from jax import lax
from jax.experimental import pallas as pl
from jax.experimental.pallas import tpu as pltpu

F32 = jnp.float32
BF16 = jnp.bfloat16
I32 = jnp.int32
U32 = jnp.uint32

D_MODEL = 1024
C_CONV = 384
CONV_W = 31
C_GMLP = 384
GMLP_HEADS = 4
GMLP_HD = 96
GMLP_CHUNK = 128
XA_HEADS = 4
XA_HD = 64
C_XA = 256
N_MEM = 256
N_EXPERTS = 32
TOP_K = 4
D_FF = 1024
SWIGLU_LIMIT = 7.0
SWIGLU_ALPHA = 1.702
EPS = 1e-5
IN_COLS = 2 * C_CONV + 2 * C_GMLP + C_XA

SUBLANES = 8
LANES = 128
ROW_TILES = D_MODEL // LANES
HIST = 32
HIST_OFF = HIST - (CONV_W - 1)

PROMPT_TILE = 512
SAMPLE_SEQS_PER_TILE = 8
MOE_BLOCK = 512
MOE_PAIRS = MOE_BLOCK // 2
GATHER_AHEAD = 2
TILES_PER_STEP = 2
VMEM_LIMIT_MIXER = 48 * 1024 * 1024
VMEM_LIMIT_MOE = 52 * 1024 * 1024


def _rmsnorm(x, g):
    return x * lax.rsqrt(jnp.mean(x * x, axis=-1, keepdims=True) + EPS) * g


def _layernorm(x, g, b):
    mu = jnp.mean(x, axis=-1, keepdims=True)
    xc = x - mu
    var = jnp.mean(xc * xc, axis=-1, keepdims=True)
    return xc * lax.rsqrt(var + EPS) * g + b


def _gelu(x):
    return 0.5 * x * (1.0 + lax.erf(x * np.float32(1.0 / np.sqrt(2.0))))


def _sigmoid(x):
    return 1.0 / (1.0 + jnp.exp(-x))


def _rows_to_tiles(dst_ref, val, rows, row0=0):
    for j in range(ROW_TILES):
        dst_ref[pl.ds(row0 * ROW_TILES + j, rows, stride=ROW_TILES), :] = val[:, j * LANES:(j + 1) * LANES]


def _tiles_to_rows(src_ref, rows, row0=0):
    return jnp.concatenate(
        [src_ref[pl.ds(row0 * ROW_TILES + j, rows, stride=ROW_TILES), :] for j in range(ROW_TILES)],
        axis=-1)


def _memkv_kernel(mem_ref, g_ref, w_ref, o_ref):
    mn = _rmsnorm(mem_ref[0], g_ref[...])
    o_ref[0] = jnp.dot(mn.astype(BF16), w_ref[...], preferred_element_type=F32)


def _memkv(mem, g_mem, w_kv):
    b = mem.shape[0]
    return pl.pallas_call(
        _memkv_kernel,
        grid=(b,),
        in_specs=[pl.BlockSpec((1, N_MEM, D_MODEL), lambda i: (i, 0, 0)),
                  pl.BlockSpec((1, D_MODEL), lambda i: (0, 0)),
                  pl.BlockSpec((D_MODEL, 2 * C_XA), lambda i: (0, 0))],
        out_specs=pl.BlockSpec((1, N_MEM, 2 * C_XA), lambda i: (i, 0, 0)),
        out_shape=jax.ShapeDtypeStruct((b, N_MEM, 2 * C_XA), F32),
        name="memkv",
    )(mem, g_mem, w_kv)


def _mixer_kernel(x_ref, hist_ref, mk_ref, mv_ref, gmix_ref, win_ref, wdw_ref, bdw_ref,
                  lcg_ref, lcb_ref, lvg_ref, lvb_ref, wsp_ref, bsp_ref, wout_ref, gffn_ref,
                  wrh_ref, wrl_ref, br_ref, *rest, ns, sl, carry, sp_chunk, tiles_per_seq, n_sorted):
    (x1_ref, xst_ref, pos_ref, gate_ref, cnt_ref, off_ref, histout_ref, v_ref,
     ext_ref, shift_ref, h2_keep, pos_keep, before_ref, wsp16_ref) = rest
    tm = ns * sl
    step = pl.program_id(0)
    n_tiles = pl.num_programs(0) - 1

    def sort_previous_tile():
        jj = lax.broadcasted_iota(I32, (n_sorted, tm), 0)
        perm = jnp.zeros((n_sorted, tm), F32)
        for k in range(TOP_K):
            perm = jnp.where(jj == pos_keep[k:k + 1, :], 1.0, perm)
        sorted_rows = jnp.dot(perm.astype(BF16), h2_keep[...], preferred_element_type=F32)
        pairs = pltpu.bitcast(sorted_rows.astype(BF16), U32)
        _rows_to_tiles(xst_ref, pairs, n_sorted // 2)

    @pl.when(step == 0)
    def _():
        h2_keep[...] = jnp.zeros_like(h2_keep)
        pos_keep[...] = jnp.full(pos_keep.shape, -1, I32)
        tr = lax.broadcasted_iota(I32, (tm, tm), 0)
        tc = lax.broadcasted_iota(I32, (tm, tm), 1)
        before_ref[...] = jnp.where(tr < tc, 1.0, 0.0).astype(BF16)
        rr = lax.broadcasted_iota(I32, (GMLP_CHUNK, GMLP_CHUNK), 0)
        cc = lax.broadcasted_iota(I32, (GMLP_CHUNK, GMLP_CHUNK), 1)
        sp_mask = (cc <= rr) & ((rr // sp_chunk) == (cc // sp_chunk))
        for hh in range(GMLP_HEADS):
            wsp16_ref[hh] = jnp.where(sp_mask, wsp_ref[hh], 0.0).astype(BF16)

    @pl.when(step < n_tiles)
    def _():
        sort_previous_tile()
        _mixer_tile(x_ref, hist_ref, mk_ref, mv_ref, gmix_ref, win_ref, wdw_ref, bdw_ref,
                    lcg_ref, lcb_ref, lvg_ref, lvb_ref, wsp16_ref, bsp_ref, wout_ref, gffn_ref,
                    wrh_ref, wrl_ref, br_ref, x1_ref, pos_ref, gate_ref, cnt_ref, off_ref,
                    histout_ref, v_ref, ext_ref, shift_ref, h2_keep, pos_keep, before_ref,
                    first_of_seq=(step % tiles_per_seq) == 0,
                    ns=ns, sl=sl, carry=carry)

    @pl.when(step == n_tiles)
    def _():
        sort_previous_tile()


def _mixer_tile(x_ref, hist_ref, mk_ref, mv_ref, gmix_ref, win_ref, wdw_ref, bdw_ref,
                lcg_ref, lcb_ref, lvg_ref, lvb_ref, wsp16_ref, bsp_ref, wout_ref, gffn_ref,
                wrh_ref, wrl_ref, br_ref, x1_ref, pos_ref, gate_ref, cnt_ref, off_ref,
                histout_ref, v_ref, ext_ref, shift_ref, h2_keep, pos_keep, before_ref,
                *, first_of_seq, ns, sl, carry):
    tm = ns * sl
    x = x_ref[0]
    h = _rmsnorm(x, gmix_ref[...])
    z = jnp.dot(h.astype(BF16), win_ref[...], preferred_element_type=F32)
    z_a = z[:, 0:C_CONV]
    z_g = z[:, C_CONV:2 * C_CONV]
    z_u = z[:, 2 * C_CONV:2 * C_CONV + C_GMLP]
    z_v = z[:, 2 * C_CONV + C_GMLP:2 * C_CONV + 2 * C_GMLP]
    z_q = z[:, 2 * C_CONV + 2 * C_GMLP:IN_COLS]

    glu = z_a * _sigmoid(z_g)
    if carry:
        ext_ref[:, 0:HIST, :] = jnp.where(first_of_seq, hist_ref[0], ext_ref[:, 0:HIST, :])
    else:
        ext_ref[:, 0:HIST, :] = hist_ref[0]
    conv_parts = []
    for s in range(ns):
        ext_s = ext_ref.at[s]
        ext_s[HIST:HIST + sl, :] = glu[s * sl:(s + 1) * sl]
        n_shift = HIST + sl - SUBLANES
        for r in range(1, SUBLANES):
            shift_ref[s, r - 1, 0:n_shift, :] = ext_s[pl.ds(r, n_shift), :]
        rc = min(sl, 64)
        for r0 in range(0, sl, rc):
            acc = jnp.broadcast_to(bdw_ref[...], (rc, C_CONV))
            for j in range(CONV_W):
                a, r = divmod(j + HIST_OFF, SUBLANES)
                src = ext_s if r == 0 else shift_ref.at[s, r - 1]
                acc = acc + wdw_ref[j:j + 1, :] * src[pl.ds(r0 + a * SUBLANES, rc), :]
            conv_parts.append(acc)
        new_hist = ext_s[sl:sl + HIST, :]
        histout_ref[0, s] = new_hist
        if carry:
            ext_s[0:HIST, :] = new_hist
    y = jnp.concatenate(conv_parts, axis=0) if len(conv_parts) > 1 else conv_parts[0]
    y = _layernorm(y, lcg_ref[...], lcb_ref[...])
    c_out = y * _sigmoid(y)

    u = _gelu(z_u)
    v = _layernorm(_gelu(z_v), lvg_ref[...], lvb_ref[...])
    v_ref[...] = v
    vb = v.astype(BF16)
    col = lax.broadcasted_iota(I32, (GMLP_CHUNK, C_GMLP), 1)
    w_heads = [wsp16_ref[hh] for hh in range(GMLP_HEADS)]
    g_parts = []
    for c in range(tm // GMLP_CHUNK):
        vc = vb[c * GMLP_CHUNK:(c + 1) * GMLP_CHUNK]
        sg = bsp_ref[...]
        for hh in range(GMLP_HEADS):
            head_cols = (col >= hh * GMLP_HD) & (col < (hh + 1) * GMLP_HD)
            vh = jnp.where(head_cols, vc, jnp.zeros_like(vc))
            sg = sg + jnp.dot(w_heads[hh], vh, preferred_element_type=F32)
        g_parts.append(u[c * GMLP_CHUNK:(c + 1) * GMLP_CHUNK] * sg)
    g_out = jnp.concatenate(g_parts, axis=0) if len(g_parts) > 1 else g_parts[0]

    qs = z_q * np.float32(XA_HD ** -0.5)
    qcol = lax.broadcasted_iota(I32, (sl, C_XA), 1)
    a_parts = []
    for s in range(ns):
        q_s = qs[s * sl:(s + 1) * sl]
        kb = mk_ref[0, s].astype(BF16)
        vvb = mv_ref[0, s].astype(BF16)
        hmasks = [(qcol >= hh * XA_HD) & (qcol < (hh + 1) * XA_HD) for hh in range(XA_HEADS)]
        stack = XA_HEADS if sl * XA_HEADS <= N_MEM else 1
        a_s = jnp.zeros((sl, C_XA), F32)
        for h0 in range(0, XA_HEADS, stack):
            heads = range(h0, h0 + stack)
            qh = jnp.concatenate([jnp.where(hmasks[hh], q_s, 0.0) for hh in heads], axis=0).astype(BF16)
            sc = lax.dot_general(qh, kb, (((1,), (1,)), ((), ())), preferred_element_type=F32)
            p = jnp.exp(sc - jnp.max(sc, axis=-1, keepdims=True))
            den = jnp.sum(p, axis=-1, keepdims=True)
            oh = jnp.dot(p.astype(BF16), vvb, preferred_element_type=F32) / den
            for n, hh in enumerate(heads):
                a_s = a_s + jnp.where(hmasks[hh], oh[n * sl:(n + 1) * sl], 0.0)
        a_parts.append(a_s)
    a_out = jnp.concatenate(a_parts, axis=0) if len(a_parts) > 1 else a_parts[0]

    mix = jnp.concatenate([c_out, g_out, a_out], axis=-1).astype(BF16)
    x1 = x + jnp.dot(mix, wout_ref[...], preferred_element_type=F32)
    x1_ref[...] = x1

    h2 = _rmsnorm(x1, gffn_ref[...])
    h2_hi = h2.astype(BF16)
    h2_lo = (h2 - h2_hi.astype(F32)).astype(BF16)
    nt_dims = (((1,), (1,)), ((), ()))
    lg = (lax.dot_general(wrh_ref[...], h2_hi, nt_dims, preferred_element_type=F32)
          + lax.dot_general(wrl_ref[...], h2_hi, nt_dims, preferred_element_type=F32)
          + lax.dot_general(wrh_ref[...], h2_lo, nt_dims, preferred_element_type=F32)
          + br_ref[...])
    eio = lax.broadcasted_iota(I32, (N_EXPERTS, tm), 0)
    work = lg
    vals, idxs = [], []
    for _ in range(TOP_K):
        m = jnp.max(work, axis=0, keepdims=True)
        idx = jnp.min(jnp.where(work == m, eio, N_EXPERTS), axis=0, keepdims=True)
        vals.append(m)
        idxs.append(idx)
        work = jnp.where(eio == idx, -jnp.inf, work)
    exps = [jnp.exp(vk - vals[0]) for vk in vals]
    den = exps[0] + exps[1] + exps[2] + exps[3]
    gate_ref[...] = jnp.concatenate([ek / den for ek in exps], axis=0)

    sel = jnp.zeros((N_EXPERTS, tm), F32)
    for idx in idxs:
        sel = sel + jnp.where(eio == idx, 1.0, 0.0)
    selb = sel.astype(BF16)
    ranks = jnp.dot(selb, before_ref[...], preferred_element_type=F32)
    er = lax.broadcasted_iota(I32, (N_EXPERTS, N_EXPERTS), 0)
    ec = lax.broadcasted_iota(I32, (N_EXPERTS, N_EXPERTS), 1)
    lower = jnp.where(ec < er, 1.0, 0.0).astype(BF16)
    cnt = jnp.sum(sel, axis=1, keepdims=True)
    half_len = jnp.floor((cnt + 1.0) * 0.5)
    off_pairs = jnp.dot(lower, jnp.broadcast_to(half_len, (N_EXPERTS, LANES)).astype(BF16),
                        preferred_element_type=F32)[:, 0:1]
    cnt_ref[0] = jnp.broadcast_to(half_len, (N_EXPERTS, LANES)).astype(I32)
    off_ref[0] = jnp.broadcast_to(off_pairs, (N_EXPERTS, LANES)).astype(I32)
    slot_of = 2.0 * off_pairs + ranks
    pos = [jnp.sum(jnp.where(eio == idx, slot_of, 0.0), axis=0, keepdims=True).astype(I32) for idx in idxs]
    pos_all = jnp.concatenate(pos, axis=0)
    pos_ref[...] = pos_all
    pos_keep[...] = pos_all
    h2_keep[...] = h2_hi


def _mixer(x, hist, mem_k, mem_v, wts, *, ns, sl, carry, sp_chunk, pairs_per_tile):
    g, r, _ = x.shape
    tm = ns * sl
    nt = r // tm
    ntot = g * nt
    assert 2 * pairs_per_tile >= TOP_K * tm + N_EXPERTS
    tile = lambda s: jnp.minimum(s, ntot - 1)
    const2 = lambda s: (0, 0)
    const3 = lambda s: (0, 0, 0)
    tile_row = lambda s: (tile(s), 0)
    tile_lane = lambda s: (0, tile(s))
    per_seq = lambda s: (tile(s) // nt, 0, 0, 0)
    in_specs = [
        pl.BlockSpec((1, tm, D_MODEL), lambda s: (tile(s) // nt, tile(s) % nt, 0)),
        pl.BlockSpec((1, ns, HIST, C_CONV), per_seq),
        pl.BlockSpec((1, ns, N_MEM, C_XA), per_seq),
        pl.BlockSpec((1, ns, N_MEM, C_XA), per_seq),
        pl.BlockSpec((1, D_MODEL), const2),
        pl.BlockSpec((D_MODEL, IN_COLS), const2),
        pl.BlockSpec((HIST, C_CONV), const2),
        pl.BlockSpec((1, C_CONV), const2),
        pl.BlockSpec((1, C_CONV), const2),
        pl.BlockSpec((1, C_CONV), const2),
        pl.BlockSpec((1, C_GMLP), const2),
        pl.BlockSpec((1, C_GMLP), const2),
        pl.BlockSpec((GMLP_HEADS, GMLP_CHUNK, GMLP_CHUNK), const3),
        pl.BlockSpec((GMLP_CHUNK, C_GMLP), const2),
        pl.BlockSpec((D_MODEL, D_MODEL), const2),
        pl.BlockSpec((1, D_MODEL), const2),
        pl.BlockSpec((N_EXPERTS, D_MODEL), const2),
        pl.BlockSpec((N_EXPERTS, D_MODEL), const2),
        pl.BlockSpec((N_EXPERTS, 1), const2),
    ]
    tile_cnt = lambda s: (tile(s), 0, 0)
    out_specs = [
        pl.BlockSpec((tm, D_MODEL), tile_row),
        pl.BlockSpec((pairs_per_tile * ROW_TILES, LANES),
                     lambda s: (jnp.maximum(s - 1, 0), 0)),
        pl.BlockSpec((TOP_K, tm), tile_lane),
        pl.BlockSpec((TOP_K, tm), tile_lane),
        pl.BlockSpec((1, N_EXPERTS, LANES), tile_cnt),
        pl.BlockSpec((1, N_EXPERTS, LANES), tile_cnt),
        pl.BlockSpec((1, ns, HIST, C_CONV), per_seq),
        pl.BlockSpec((tm, C_GMLP), tile_row),
    ]
    rows = g * r
    out_shape = [
        jax.ShapeDtypeStruct((rows, D_MODEL), F32),
        jax.ShapeDtypeStruct((ntot * pairs_per_tile * ROW_TILES, LANES), U32),
        jax.ShapeDtypeStruct((TOP_K, rows), I32),
        jax.ShapeDtypeStruct((TOP_K, rows), F32),
        jax.ShapeDtypeStruct((ntot, N_EXPERTS, LANES), I32),
        jax.ShapeDtypeStruct((ntot, N_EXPERTS, LANES), I32),
        jax.ShapeDtypeStruct((g, ns, HIST, C_CONV), F32),
        jax.ShapeDtypeStruct((rows, C_GMLP), F32),
    ]
    kern = functools.partial(_mixer_kernel, ns=ns, sl=sl, carry=carry, sp_chunk=sp_chunk,
                             tiles_per_seq=nt, n_sorted=2 * pairs_per_tile)
    return pl.pallas_call(
        kern,
        grid=(ntot + 1,),
        in_specs=in_specs,
        out_specs=out_specs,
        out_shape=out_shape,
        scratch_shapes=[pltpu.VMEM((ns, HIST + sl, C_CONV), F32),
                        pltpu.VMEM((ns, SUBLANES - 1, HIST + sl, C_CONV), F32),
                        pltpu.VMEM((tm, D_MODEL), BF16),
                        pltpu.VMEM((TOP_K, tm), I32),
                        pltpu.VMEM((tm, tm), BF16),
                        pltpu.VMEM((GMLP_HEADS, GMLP_CHUNK, GMLP_CHUNK), BF16)],
        compiler_params=pltpu.CompilerParams(
            dimension_semantics=("arbitrary",),
            vmem_limit_bytes=VMEM_LIMIT_MIXER),
        name="mixer_carry" if carry else "mixer_cache",
    )(x, hist, mem_k, mem_v, *wts)


def _strip_pieces(n, max_rows, fn):
    done = 0
    p = max_rows
    while p >= 1:
        has = (n & p) != 0
        pl.when(has)(functools.partial(fn, done, p))
        done = done + jnp.where(has, p, 0)
        p //= 2


class _Table:
    def __init__(self, ref, offset):
        self.ref, self.offset = ref, offset

    def __getitem__(self, i):
        return self.ref[self.offset + i]


def _pack_tables(tables):
    offsets = tuple(int(o) for o in np.cumsum([0] + [t.shape[0] for t in tables[:-1]]))
    return jnp.concatenate([t.astype(I32) for t in tables]), offsets


def _pow2_at_most(n):
    return 1 << (n.bit_length() - 1)


def _rows(ref, first_row, rows):
    return ref.at[pl.ds(pl.multiple_of(first_row * ROW_TILES, ROW_TILES), rows * ROW_TILES)]


def _moe_kernel(tab_ref, xs_hbm, xs2_hbm, wgu_ref, bgu_ref, wdn_ref, bdn_ref, ys_hbm,
                xbuf, ybuf, wgu16, wdn16, sems, ysems, *, offsets, n_blocks):
    tabs = [_Table(tab_ref, o) for o in offsets]
    b0_ref, nbk_ref, bv_ref, nb_ref = tabs[:4]
    strips_a, strips_b = tabs[4:9], tabs[9:14]
    e = pl.program_id(0)
    n_used = nb_ref[0]
    half = MOE_PAIRS // 2

    def gather_start(b, slot):
        pair0 = b * MOE_PAIRS

        for src_hbm, (slo_ref, shi_ref, ssrc_ref, sdst_ref, slen_ref) in ((xs_hbm, strips_a), (xs2_hbm, strips_b)):
            def strip(s, carry, src_hbm=src_hbm, ssrc_ref=ssrc_ref, sdst_ref=sdst_ref, slen_ref=slen_ref):
                lo = jnp.maximum(sdst_ref[s], pair0)
                hi = jnp.minimum(sdst_ref[s] + slen_ref[s], pair0 + MOE_PAIRS)
                src0 = ssrc_ref[s] + (lo - sdst_ref[s])
                dst0 = lo - pair0

                def piece(first, rows):
                    pltpu.make_async_copy(_rows(src_hbm, src0 + first, rows),
                                          _rows(xbuf.at[slot], dst0 + first, rows), sems.at[slot]).start()
                _strip_pieces(hi - lo, MOE_PAIRS, piece)
                return carry
            lax.fori_loop(slo_ref[b], shi_ref[b], strip, 0)

    def gather_wait(b, slot):
        def piece(first, rows):
            del first
            pltpu.make_async_copy(_rows(xs_hbm, 0, rows), _rows(xbuf.at[slot], 0, rows), sems.at[slot]).wait()
        _strip_pieces(bv_ref[b], MOE_PAIRS, piece)

    def out_copy(b, slot):
        return pltpu.make_async_copy(ybuf.at[slot], _rows(ys_hbm, b * MOE_PAIRS, MOE_PAIRS), ysems.at[slot])

    def expert_mlp(xslot, slot, h):
        xb = pltpu.bitcast(_tiles_to_rows(xbuf.at[xslot], half, h * half), BF16)
        gu = jnp.dot(xb, wgu16[...], preferred_element_type=F32) + bgu_ref[0]
        gate = jnp.minimum(gu[:, :D_FF], SWIGLU_LIMIT)
        up = jnp.clip(gu[:, D_FF:], -SWIGLU_LIMIT, SWIGLU_LIMIT)
        act = (up + 1.0) * (gate * _sigmoid(SWIGLU_ALPHA * gate))
        yb = jnp.dot(act.astype(BF16), wdn16[...], preferred_element_type=F32) + bdn_ref[0]
        _rows_to_tiles(ybuf.at[slot], pltpu.bitcast(yb.astype(BF16), U32), half, h * half)

    @pl.when(e == 0)
    def _():
        xbuf[...] = jnp.zeros_like(xbuf)
        for a in range(GATHER_AHEAD):
            @pl.when(a < n_used)
            def _():
                gather_start(a, a)

    @pl.when(nbk_ref[e] > 0)
    def _():
        def cast_rows(c, carry):
            r = pl.multiple_of(c * LANES, LANES)
            wgu16[pl.ds(r, LANES), :] = wgu_ref[pl.ds(r, LANES), :].astype(BF16)
            wdn16[pl.ds(r, LANES), :] = wdn_ref[pl.ds(r, LANES), :].astype(BF16)
            return carry
        lax.fori_loop(0, D_MODEL // LANES, cast_rows, 0)

        def block(j, carry):
            b = b0_ref[e] + j
            slot = b % 2
            xslot = b % (GATHER_AHEAD + 1)

            @pl.when(b + GATHER_AHEAD < n_used)
            def _():
                gather_start(b + GATHER_AHEAD, (b + GATHER_AHEAD) % (GATHER_AHEAD + 1))

            gather_wait(b, xslot)

            @pl.when(b >= 2)
            def _():
                out_copy(b - 2, slot).wait()

            valid = bv_ref[b]

            @pl.when(valid > half)
            def _():
                expert_mlp(xslot, slot, 0)
                expert_mlp(xslot, slot, 1)

            @pl.when(valid <= half)
            def _():
                expert_mlp(xslot, slot, 0)
                ybuf[slot, pl.ds(half * ROW_TILES, half * ROW_TILES), :] = jnp.zeros(
                    (half * ROW_TILES, LANES), U32)

            out_copy(b, slot).start()
            return carry
        lax.fori_loop(0, nbk_ref[e], block, 0)

    @pl.when(e == pl.num_programs(0) - 1)
    def _():
        for back in (1, 2):
            @pl.when(n_used >= back)
            def _():
                out_copy(n_used - back, (n_used - back) % 2).wait()
        ybuf[0] = jnp.zeros(ybuf.shape[1:], U32)

        def start_fill(b, carry):
            out_copy(b, 0).start()
            return carry

        def wait_fill(b, carry):
            out_copy(b, 0).wait()
            return carry
        lax.fori_loop(n_used, n_blocks, start_fill, 0)
        lax.fori_loop(n_used, n_blocks, wait_fill, 0)


def _moe(tables, xs, xs2, n_blocks, w_gu, b_gu, w_dn, b_dn):
    wsel = lambda e, *_: (e, 0, 0)
    packed, offsets = _pack_tables(tables)
    slots = n_blocks * MOE_PAIRS
    grid_spec = pltpu.PrefetchScalarGridSpec(
        num_scalar_prefetch=1,
        grid=(N_EXPERTS,),
        in_specs=[pl.BlockSpec(memory_space=pl.ANY),
                  pl.BlockSpec(memory_space=pl.ANY),
                  pl.BlockSpec((None, D_MODEL, 2 * D_FF), wsel),
                  pl.BlockSpec((None, 1, 2 * D_FF), wsel),
                  pl.BlockSpec((None, D_FF, D_MODEL), wsel),
                  pl.BlockSpec((None, 1, D_MODEL), wsel)],
        out_specs=pl.BlockSpec(memory_space=pl.ANY),
        scratch_shapes=[pltpu.VMEM((GATHER_AHEAD + 1, MOE_PAIRS * ROW_TILES, LANES), U32),
                        pltpu.VMEM((2, MOE_PAIRS * ROW_TILES, LANES), U32),
                        pltpu.VMEM((D_MODEL, 2 * D_FF), BF16),
                        pltpu.VMEM((D_FF, D_MODEL), BF16),
                        pltpu.SemaphoreType.DMA((GATHER_AHEAD + 1,)),
                        pltpu.SemaphoreType.DMA((2,))],
    )
    return pl.pallas_call(
        functools.partial(_moe_kernel, offsets=offsets, n_blocks=n_blocks),
        grid_spec=grid_spec,
        out_shape=jax.ShapeDtypeStruct((slots * ROW_TILES, LANES), U32),
        compiler_params=pltpu.CompilerParams(
            dimension_semantics=("arbitrary",),
            vmem_limit_bytes=VMEM_LIMIT_MOE),
        name="moe",
    )(packed, xs, xs2, w_gu, b_gu, w_dn, b_dn)


def _combine_kernel(tab_ref, x1_ref, pos_ref, gate_ref, gfin_ref, ys_hbm,
                    out_ref, ybuf, sems, *, tm, pairs_per_tile, offsets):
    csrc_ref, clen_ref, coff_ref, ctot_ref = [_Table(tab_ref, o) for o in offsets]
    i = pl.program_id(0)
    n_steps = pl.num_programs(0)
    n_sorted = 2 * pairs_per_tile
    n_slots = 2 * TILES_PER_STEP

    def gather_start(t, slot):
        def strip(e, carry):
            s = t * N_EXPERTS + e
            src0 = csrc_ref[s]
            dst0 = coff_ref[s]

            def piece(first, rows):
                pltpu.make_async_copy(_rows(ys_hbm, src0 + first, rows),
                                      _rows(ybuf.at[slot], dst0 + first, rows), sems.at[slot]).start()
            _strip_pieces(clen_ref[s], tm // 2, piece)
            return carry
        lax.fori_loop(0, N_EXPERTS, strip, 0)

    @pl.when(i == 0)
    def _():
        ybuf[...] = jnp.zeros_like(ybuf)
        for h in range(TILES_PER_STEP):
            gather_start(h, h)

    @pl.when(i + 1 < n_steps)
    def _():
        for h in range(TILES_PER_STEP):
            t = (i + 1) * TILES_PER_STEP + h
            gather_start(t, t % n_slots)

    slots = [(i * TILES_PER_STEP + h) % n_slots for h in range(TILES_PER_STEP)]
    for h in range(TILES_PER_STEP):
        def wait_piece(first, rows, slot=slots[h]):
            del first
            pltpu.make_async_copy(_rows(ys_hbm, 0, rows), _rows(ybuf.at[slot], 0, rows), sems.at[slot]).wait()
        _strip_pieces(ctot_ref[i * TILES_PER_STEP + h], _pow2_at_most(pairs_per_tile), wait_piece)

    rr = lax.broadcasted_iota(I32, (tm, tm), 0)
    cc = lax.broadcasted_iota(I32, (tm, tm), 1)
    eye = rr == cc
    jl = lax.broadcasted_iota(I32, (tm, n_sorted), 1).astype(F32)
    for h in range(TILES_PER_STEP):
        tok = slice(h * tm, (h + 1) * tm)
        y_sorted = pltpu.bitcast(_tiles_to_rows(ybuf.at[slots[h]], pairs_per_tile), BF16)
        unsort = jnp.zeros((tm, n_sorted), F32)
        for k in range(TOP_K):
            p_col = jnp.sum(jnp.where(eye, pos_ref[k:k + 1, tok].astype(F32), 0.0), axis=1, keepdims=True)
            g_col = jnp.sum(jnp.where(eye, gate_ref[k:k + 1, tok], 0.0), axis=1, keepdims=True)
            unsort = jnp.where(jl == p_col, g_col, unsort)
        acc = x1_ref[tok, :] + jnp.dot(unsort.astype(BF16), y_sorted, preferred_element_type=F32)
        out_ref[tok, :] = _rmsnorm(acc, gfin_ref[...])


def _combine(tables, x1, pos, gates, g_final, ys, tm, pairs_per_tile):
    t = x1.shape[0]
    nt = t // tm
    assert nt % TILES_PER_STEP == 0
    packed, offsets = _pack_tables(tables)
    grid_spec = pltpu.PrefetchScalarGridSpec(
        num_scalar_prefetch=1,
        grid=(nt // TILES_PER_STEP,),
        in_specs=[pl.BlockSpec((TILES_PER_STEP * tm, D_MODEL), lambda i, *_: (i, 0)),
                  pl.BlockSpec((TOP_K, TILES_PER_STEP * tm), lambda i, *_: (0, i)),
                  pl.BlockSpec((TOP_K, TILES_PER_STEP * tm), lambda i, *_: (0, i)),
                  pl.BlockSpec((1, D_MODEL), lambda i, *_: (0, 0)),
                  pl.BlockSpec(memory_space=pl.ANY)],
        out_specs=pl.BlockSpec((TILES_PER_STEP * tm, D_MODEL), lambda i, *_: (i, 0)),
        scratch_shapes=[pltpu.VMEM((2 * TILES_PER_STEP, pairs_per_tile * ROW_TILES, LANES), U32),
                        pltpu.SemaphoreType.DMA((2 * TILES_PER_STEP,))],
    )
    return pl.pallas_call(
        functools.partial(_combine_kernel, tm=tm, pairs_per_tile=pairs_per_tile, offsets=offsets),
        grid_spec=grid_spec,
        out_shape=jax.ShapeDtypeStruct((t, D_MODEL), F32),
        compiler_params=pltpu.CompilerParams(
            dimension_semantics=("arbitrary",),
            vmem_limit_bytes=VMEM_LIMIT_MIXER),
        name="combine",
    )(packed, x1, pos, gates, g_final, ys)


def _split_bf16(w):
    hi = w.astype(BF16)
    lo = (w - hi.astype(F32)).astype(BF16)
    return hi, lo


def kernel(x_prompt, x_sample, cache_conv, cache_mem_k, cache_mem_v, mem_prompt, g_mix, w_in, w_dw, b_dw, ln_conv_g, ln_conv_b, ln_v_g, ln_v_b, w_spatial, b_spatial, g_mem, w_mem_k, w_mem_v, w_out, g_ffn, w_router, b_router, w_gate_up, b_gate_up, w_down, b_down, g_final):
    depth = g_mix.shape[0]
    assert depth == 1
    l = 0
    bp, seq, _ = x_prompt.shape
    bs, dseq, _ = x_sample.shape
    assert seq % PROMPT_TILE == 0 and bs % SAMPLE_SEQS_PER_TILE == 0
    assert GMLP_CHUNK % dseq == 0 and (SAMPLE_SEQS_PER_TILE * dseq) % GMLP_CHUNK == 0

    row = lambda a: a.reshape(1, -1)
    wr_hi, wr_lo = _split_bf16(w_router[l].T)
    w_dw_pad = jnp.pad(w_dw[l], ((0, HIST - CONV_W), (0, 0)))
    bias_rows = lambda b: jnp.repeat(b.T, GMLP_HD, axis=1)
    common = dict(
        gmix=row(g_mix[l]), win=w_in[l].astype(BF16), wdw=w_dw_pad, bdw=row(b_dw[l]),
        lcg=row(ln_conv_g[l]), lcb=row(ln_conv_b[l]), lvg=row(ln_v_g[l]), lvb=row(ln_v_b[l]),
        wout=w_out[l].astype(BF16), gffn=row(g_ffn[l]), wrh=wr_hi, wrl=wr_lo,
        br=b_router[l].reshape(N_EXPERTS, 1))

    def weights(wsp, bsp):
        c = common
        return (c["gmix"], c["win"], c["wdw"], c["bdw"], c["lcg"], c["lcb"], c["lvg"], c["lvb"],
                wsp, bsp, c["wout"], c["gffn"], c["wrh"], c["wrl"], c["br"])

    reps = GMLP_CHUNK // dseq
    wts_p = weights(w_spatial[l], bias_rows(b_spatial[l]))
    wts_s = weights(jnp.tile(w_spatial[l][:, :dseq, :dseq], (1, reps, reps)),
                    bias_rows(jnp.tile(b_spatial[l][:, :dseq], (1, reps))))

    w_kv = jnp.concatenate([w_mem_k[l], w_mem_v[l]], axis=1).astype(BF16)
    kv_p = _memkv(mem_prompt, row(g_mem[l]), w_kv)
    mk_p = kv_p[:, :, :C_XA]
    mv_p = kv_p[:, :, C_XA:]
    zero_hist = jnp.zeros((bp, 1, HIST, C_CONV), F32)
    tp, ts = bp * seq, bs * dseq
    tm_s = SAMPLE_SEQS_PER_TILE * dseq
    ntp, nts = tp // PROMPT_TILE, ts // tm_s
    ppt_p = (TOP_K * PROMPT_TILE + N_EXPERTS + 1) // 2
    ppt_s = (TOP_K * tm_s + N_EXPERTS + 1) // 2
    (x1_p, xs_p, pos_p, gate_p, cnt_p, off_p, hist_p, _) = _mixer(
        x_prompt, zero_hist, mk_p[:, None], mv_p[:, None], wts_p,
        ns=1, sl=PROMPT_TILE, carry=True, sp_chunk=GMLP_CHUNK, pairs_per_tile=ppt_p)

    gs = bs // SAMPLE_SEQS_PER_TILE
    hist_s_in = jnp.pad(cache_conv[l], ((0, 0), (HIST_OFF, 0), (0, 0))).reshape(
        gs, SAMPLE_SEQS_PER_TILE, HIST, C_CONV)
    mk_s = cache_mem_k[l].reshape(gs, SAMPLE_SEQS_PER_TILE, N_MEM, C_XA)
    mv_s = cache_mem_v[l].reshape(gs, SAMPLE_SEQS_PER_TILE, N_MEM, C_XA)
    (x1_s, xs_s, pos_s, gate_s, cnt_s, off_s, hist_s, v_s) = _mixer(
        x_sample.reshape(gs, tm_s, D_MODEL), hist_s_in, mk_s, mv_s, wts_s,
        ns=SAMPLE_SEQS_PER_TILE, sl=dseq, carry=False, sp_chunk=dseq, pairs_per_tile=ppt_s)

    max_pairs = ((tp + ts) * TOP_K + (ntp + nts) * N_EXPERTS) // 2
    n_blocks = -(-max_pairs // MOE_PAIRS) + N_EXPERTS
    cnt = jnp.concatenate([cnt_p[:, :, 0], cnt_s[:, :, 0]], axis=0)
    off = jnp.concatenate([off_p[:, :, 0], off_s[:, :, 0]], axis=0)
    tile_pair0 = np.concatenate([np.arange(ntp) * ppt_p, np.arange(nts) * ppt_s]).astype(np.int32)
    counts = jnp.sum(cnt, axis=0)
    tile_base = jnp.cumsum(cnt, axis=0) - cnt
    padded = (counts + MOE_PAIRS - 1) // MOE_PAIRS * MOE_PAIRS
    pad_end = jnp.cumsum(padded)
    pad_start = pad_end - padded
    strip_dst = (pad_start[None, :] + tile_base).astype(I32)
    strip_src = (tile_pair0[:, None] + off).astype(I32)
    n_used = (pad_end[-1] // MOE_PAIRS).astype(I32)
    blk_pair0 = jnp.minimum(jnp.arange(n_blocks, dtype=I32), n_used - 1) * MOE_PAIRS
    block_expert = jnp.minimum(
        jnp.sum((pad_end[None, :] <= blk_pair0[:, None]).astype(I32), axis=1), N_EXPERTS - 1)
    of_block = block_expert[:, None] == jnp.arange(N_EXPERTS, dtype=I32)[None, :]
    last_pair = jnp.sum(jnp.where(of_block, (pad_start + counts)[None, :], 0), axis=1)
    block_valid = jnp.clip(last_pair - blk_pair0, 0, MOE_PAIRS).astype(I32)
    def strip_group(t0, t1):
        sdst = strip_dst[t0:t1].T.reshape(-1)
        ssrc = strip_src[t0:t1].T.reshape(-1)
        slen = cnt[t0:t1].T.reshape(-1).astype(I32)
        s_lo = jnp.sum(((sdst + slen)[None, :] <= blk_pair0[:, None]).astype(I32), axis=1)
        s_hi = jnp.sum((sdst[None, :] < (blk_pair0 + MOE_PAIRS)[:, None]).astype(I32), axis=1)
        return (s_lo, s_hi, ssrc, sdst, slen)

    moe_tables = ((pad_start // MOE_PAIRS).astype(I32), (padded // MOE_PAIRS).astype(I32),
                  block_valid, n_used.reshape(1),
                  *strip_group(0, ntp), *strip_group(ntp, ntp + nts))

    ys = _moe(moe_tables, xs_p, xs_s, n_blocks,
              w_gate_up[l], b_gate_up[l][:, None, :], w_down[l], b_down[l][:, None, :])
    gfin = row(g_final)

    def combine_tables(t0, t1):
        return (strip_dst[t0:t1].reshape(-1), cnt[t0:t1].reshape(-1).astype(I32),
                off[t0:t1].reshape(-1).astype(I32), jnp.sum(cnt[t0:t1], axis=1).astype(I32))

    y_p = _combine(combine_tables(0, ntp), x1_p, pos_p, gate_p, gfin, ys, PROMPT_TILE, ppt_p)
    y_s = _combine(combine_tables(ntp, ntp + nts), x1_s, pos_s, gate_s, gfin, ys, tm_s, ppt_s)

    return (y_p.reshape(bp, seq, D_MODEL),
            y_s.reshape(bs, dseq, D_MODEL),
            hist_p[:, 0, HIST_OFF:, :][None],
            mk_p.reshape(bp, N_MEM, XA_HEADS, XA_HD)[None],
            mv_p.reshape(bp, N_MEM, XA_HEADS, XA_HD)[None],
            hist_s.reshape(bs, HIST, C_CONV)[:, HIST_OFF:, :][None],
            v_s.reshape(bs, dseq, C_GMLP)[None])
```

```python
import functools

import numpy as np
import jax
import jax.numpy as jnp
from jax import lax
from jax.experimental import pallas as pl
from jax.experimental.pallas import tpu as pltpu

F32 = jnp.float32
BF16 = jnp.bfloat16
I32 = jnp.int32
U32 = jnp.uint32

D_MODEL = 1024
C_CONV = 384
CONV_W = 31
C_GMLP = 384
GMLP_HEADS = 4
GMLP_HD = 96
GMLP_CHUNK = 128
XA_HEADS = 4
XA_HD = 64
C_XA = 256
N_MEM = 256
N_EXPERTS = 32
TOP_K = 4
D_FF = 1024
SWIGLU_LIMIT = 7.0
SWIGLU_ALPHA = 1.702
EPS = 1e-5
IN_COLS = 2 * C_CONV + 2 * C_GMLP + C_XA

SUBLANES = 8
LANES = 128
ROW_TILES = D_MODEL // LANES
HIST = 32
HIST_OFF = HIST - (CONV_W - 1)

PROMPT_TILE = 512
SAMPLE_SEQS_PER_TILE = 8
MOE_BLOCK = 512
MOE_PAIRS = MOE_BLOCK // 2
GATHER_AHEAD = 2
TILES_PER_STEP = 2
VMEM_LIMIT_MIXER = 48 * 1024 * 1024
VMEM_LIMIT_MOE = 52 * 1024 * 1024


def _rmsnorm(x, g):
    return x * lax.rsqrt(jnp.mean(x * x, axis=-1, keepdims=True) + EPS) * g


def _layernorm(x, g, b):
    mu = jnp.mean(x, axis=-1, keepdims=True)
    xc = x - mu
    var = jnp.mean(xc * xc, axis=-1, keepdims=True)
    return xc * lax.rsqrt(var + EPS) * g + b


def _gelu(x):
    return 0.5 * x * (1.0 + lax.erf(x * np.float32(1.0 / np.sqrt(2.0))))


def _sigmoid(x):
    return 1.0 / (1.0 + jnp.exp(-x))


def _rows_to_tiles(dst_ref, val, rows, row0=0):
    for j in range(ROW_TILES):
        dst_ref[pl.ds(row0 * ROW_TILES + j, rows, stride=ROW_TILES), :] = val[:, j * LANES:(j + 1) * LANES]


def _tiles_to_rows(src_ref, rows, row0=0):
    return jnp.concatenate(
        [src_ref[pl.ds(row0 * ROW_TILES + j, rows, stride=ROW_TILES), :] for j in range(ROW_TILES)],
        axis=-1)


def _memkv_kernel(mem_ref, g_ref, w_ref, o_ref):
    mn = _rmsnorm(mem_ref[0], g_ref[...])
    o_ref[0] = jnp.dot(mn.astype(BF16), w_ref[...], preferred_element_type=F32)


def _memkv(mem, g_mem, w_kv):
    b = mem.shape[0]
    return pl.pallas_call(
        _memkv_kernel,
        grid=(b,),
        in_specs=[pl.BlockSpec((1, N_MEM, D_MODEL), lambda i: (i, 0, 0)),
                  pl.BlockSpec((1, D_MODEL), lambda i: (0, 0)),
                  pl.BlockSpec((D_MODEL, 2 * C_XA), lambda i: (0, 0))],
        out_specs=pl.BlockSpec((1, N_MEM, 2 * C_XA), lambda i: (i, 0, 0)),
        out_shape=jax.ShapeDtypeStruct((b, N_MEM, 2 * C_XA), F32),
        name="memkv",
    )(mem, g_mem, w_kv)


def _mixer_kernel(x_ref, hist_ref, mk_ref, mv_ref, gmix_ref, win_ref, wdw_ref, bdw_ref,
                  lcg_ref, lcb_ref, lvg_ref, lvb_ref, wsp_ref, bsp_ref, wout_ref, gffn_ref,
                  wrh_ref, wrl_ref, br_ref, *rest, ns, sl, carry, sp_chunk, tiles_per_seq, n_sorted):
    (x1_ref, xst_ref, pos_ref, gate_ref, cnt_ref, off_ref, histout_ref, v_ref,
     ext_ref, shift_ref, h2_keep, pos_keep, before_ref, wsp16_ref) = rest
    tm = ns * sl
    step = pl.program_id(0)
    n_tiles = pl.num_programs(0) - 1

    def sort_previous_tile():
        jj = lax.broadcasted_iota(I32, (n_sorted, tm), 0).astype(jnp.int16)
        perm = jnp.zeros((n_sorted, tm), BF16)
        for k in range(TOP_K):
            perm = jnp.where(jj == pos_keep[k:k + 1, :].astype(jnp.int16), jnp.ones((), BF16), perm)
        sorted_rows = jnp.dot(perm, h2_keep[...], preferred_element_type=F32)
        pairs = pltpu.bitcast(sorted_rows.astype(BF16), U32)
        _rows_to_tiles(xst_ref, pairs, n_sorted // 2)

    @pl.when(step == 0)
    def _():
        h2_keep[...] = jnp.zeros_like(h2_keep)
        pos_keep[...] = jnp.full(pos_keep.shape, -1, I32)
        tr = lax.broadcasted_iota(I32, (tm, tm), 0)
        tc = lax.broadcasted_iota(I32, (tm, tm), 1)
        before_ref[...] = jnp.where(tr < tc, 1.0, 0.0).astype(BF16)
        rr = lax.broadcasted_iota(I32, (GMLP_CHUNK, GMLP_CHUNK), 0)
        cc = lax.broadcasted_iota(I32, (GMLP_CHUNK, GMLP_CHUNK), 1)
        sp_mask = (cc <= rr) & ((rr // sp_chunk) == (cc // sp_chunk))
        for hh in range(GMLP_HEADS):
            wsp16_ref[hh] = jnp.where(sp_mask, wsp_ref[hh], 0.0).astype(BF16)

    @pl.when(step < n_tiles)
    def _():
        sort_previous_tile()
        _mixer_tile(x_ref, hist_ref, mk_ref, mv_ref, gmix_ref, win_ref, wdw_ref, bdw_ref,
                    lcg_ref, lcb_ref, lvg_ref, lvb_ref, wsp16_ref, bsp_ref, wout_ref, gffn_ref,
                    wrh_ref, wrl_ref, br_ref, x1_ref, pos_ref, gate_ref, cnt_ref, off_ref,
                    histout_ref, v_ref, ext_ref, shift_ref, h2_keep, pos_keep, before_ref,
                    first_of_seq=(step % tiles_per_seq) == 0,
                    ns=ns, sl=sl, carry=carry)

    @pl.when(step == n_tiles)
    def _():
        sort_previous_tile()


def _mixer_tile(x_ref, hist_ref, mk_ref, mv_ref, gmix_ref, win_ref, wdw_ref, bdw_ref,
                lcg_ref, lcb_ref, lvg_ref, lvb_ref, wsp16_ref, bsp_ref, wout_ref, gffn_ref,
                wrh_ref, wrl_ref, br_ref, x1_ref, pos_ref, gate_ref, cnt_ref, off_ref,
                histout_ref, v_ref, ext_ref, shift_ref, h2_keep, pos_keep, before_ref,
                *, first_of_seq, ns, sl, carry):
    tm = ns * sl
    x = x_ref[0]
    h = _rmsnorm(x, gmix_ref[...])
    z = jnp.dot(h.astype(BF16), win_ref[...], preferred_element_type=F32)
    z_a = z[:, 0:C_CONV]
    z_g = z[:, C_CONV:2 * C_CONV]
    z_u = z[:, 2 * C_CONV:2 * C_CONV + C_GMLP]
    z_v = z[:, 2 * C_CONV + C_GMLP:2 * C_CONV + 2 * C_GMLP]
    z_q = z[:, 2 * C_CONV + 2 * C_GMLP:IN_COLS]

    glu = z_a * _sigmoid(z_g)
    if carry:
        ext_ref[:, 0:HIST, :] = jnp.where(first_of_seq, hist_ref[0], ext_ref[:, 0:HIST, :])
    else:
        ext_ref[:, 0:HIST, :] = hist_ref[0]
    conv_parts = []
    for s in range(ns):
        ext_s = ext_ref.at[s]
        ext_s[HIST:HIST + sl, :] = glu[s * sl:(s + 1) * sl]
        n_shift = HIST + sl - SUBLANES
        for r in range(1, SUBLANES):
            shift_ref[s, r - 1, 0:n_shift, :] = ext_s[pl.ds(r, n_shift), :]
        rc = min(sl, 64)
        for r0 in range(0, sl, rc):
            acc = jnp.broadcast_to(bdw_ref[...], (rc, C_CONV))
            for j in range(CONV_W):
                a, r = divmod(j + HIST_OFF, SUBLANES)
                src = ext_s if r == 0 else shift_ref.at[s, r - 1]
                acc = acc + wdw_ref[j:j + 1, :] * src[pl.ds(r0 + a * SUBLANES, rc), :]
            conv_parts.append(acc)
        new_hist = ext_s[sl:sl + HIST, :]
        histout_ref[0, s] = new_hist
        if carry:
            ext_s[0:HIST, :] = new_hist
    y = jnp.concatenate(conv_parts, axis=0) if len(conv_parts) > 1 else conv_parts[0]
    y = _layernorm(y, lcg_ref[...], lcb_ref[...])
    c_out = y * _sigmoid(y)

    u = _gelu(z_u)
    v = _layernorm(_gelu(z_v), lvg_ref[...], lvb_ref[...])
    v_ref[...] = v
    vb = v.astype(BF16)
    col = lax.broadcasted_iota(I32, (GMLP_CHUNK, C_GMLP), 1)
    w_heads = [wsp16_ref[hh] for hh in range(GMLP_HEADS)]
    g_parts = []
    for c in range(tm // GMLP_CHUNK):
        vc = vb[c * GMLP_CHUNK:(c + 1) * GMLP_CHUNK]
        sg = bsp_ref[...]
        for hh in range(GMLP_HEADS):
            head_cols = (col >= hh * GMLP_HD) & (col < (hh + 1) * GMLP_HD)
            vh = jnp.where(head_cols, vc, jnp.zeros_like(vc))
            sg = sg + jnp.dot(w_heads[hh], vh, preferred_element_type=F32)
        g_parts.append(u[c * GMLP_CHUNK:(c + 1) * GMLP_CHUNK] * sg)
    g_out = jnp.concatenate(g_parts, axis=0) if len(g_parts) > 1 else g_parts[0]

    qs = z_q * np.float32(XA_HD ** -0.5)
    qcol = lax.broadcasted_iota(I32, (sl, C_XA), 1)
    a_parts = []
    for s in range(ns):
        q_s = qs[s * sl:(s + 1) * sl]
        kb = mk_ref[0, s].astype(BF16)
        vvb = mv_ref[0, s].astype(BF16)
        hmasks = [(qcol >= hh * XA_HD) & (qcol < (hh + 1) * XA_HD) for hh in range(XA_HEADS)]
        stack = XA_HEADS if sl * XA_HEADS <= N_MEM else 1
        a_s = jnp.zeros((sl, C_XA), F32)
        for h0 in range(0, XA_HEADS, stack):
            heads = range(h0, h0 + stack)
            qh = jnp.concatenate([jnp.where(hmasks[hh], q_s, 0.0) for hh in heads], axis=0).astype(BF16)
            sc = lax.dot_general(qh, kb, (((1,), (1,)), ((), ())), preferred_element_type=F32)
            p = jnp.exp(sc - jnp.max(sc, axis=-1, keepdims=True))
            den = jnp.sum(p, axis=-1, keepdims=True)
            oh = jnp.dot(p.astype(BF16), vvb, preferred_element_type=F32) / den
            for n, hh in enumerate(heads):
                a_s = a_s + jnp.where(hmasks[hh], oh[n * sl:(n + 1) * sl], 0.0)
        a_parts.append(a_s)
    a_out = jnp.concatenate(a_parts, axis=0) if len(a_parts) > 1 else a_parts[0]

    mix = jnp.concatenate([c_out, g_out, a_out], axis=-1).astype(BF16)
    x1 = x + jnp.dot(mix, wout_ref[...], preferred_element_type=F32)
    x1_ref[...] = x1

    h2 = _rmsnorm(x1, gffn_ref[...])
    h2_hi = h2.astype(BF16)
    h2_lo = (h2 - h2_hi.astype(F32)).astype(BF16)
    nt_dims = (((1,), (1,)), ((), ()))
    lg = (lax.dot_general(wrh_ref[...], h2_hi, nt_dims, preferred_element_type=F32)
          + lax.dot_general(wrl_ref[...], h2_hi, nt_dims, preferred_element_type=F32)
          + lax.dot_general(wrh_ref[...], h2_lo, nt_dims, preferred_element_type=F32)
          + br_ref[...])
    eio = lax.broadcasted_iota(I32, (N_EXPERTS, tm), 0)
    work = lg
    vals, idxs = [], []
    for _ in range(TOP_K):
        m = jnp.max(work, axis=0, keepdims=True)
        idx = jnp.min(jnp.where(work == m, eio, N_EXPERTS), axis=0, keepdims=True)
        vals.append(m)
        idxs.append(idx)
        work = jnp.where(eio == idx, -jnp.inf, work)
    exps = [jnp.exp(vk - vals[0]) for vk in vals]
    den = exps[0] + exps[1] + exps[2] + exps[3]
    gate_ref[...] = jnp.concatenate([ek / den for ek in exps], axis=0)

    sel = jnp.zeros((N_EXPERTS, tm), F32)
    for idx in idxs:
        sel = sel + jnp.where(eio == idx, 1.0, 0.0)
    selb = sel.astype(BF16)
    ranks = jnp.dot(selb, before_ref[...], preferred_element_type=F32)
    er = lax.broadcasted_iota(I32, (N_EXPERTS, N_EXPERTS), 0)
    ec = lax.broadcasted_iota(I32, (N_EXPERTS, N_EXPERTS), 1)
    lower = jnp.where(ec < er, 1.0, 0.0).astype(BF16)
    cnt = jnp.sum(sel, axis=1, keepdims=True)
    half_len = jnp.floor((cnt + 1.0) * 0.5)
    off_pairs = jnp.dot(lower, jnp.broadcast_to(half_len, (N_EXPERTS, LANES)).astype(BF16),
                        preferred_element_type=F32)[:, 0:1]
    cnt_ref[0] = jnp.broadcast_to(half_len, (N_EXPERTS, LANES)).astype(I32)
    off_ref[0] = jnp.broadcast_to(off_pairs, (N_EXPERTS, LANES)).astype(I32)
    slot_of = 2.0 * off_pairs + ranks
    pos = [jnp.sum(jnp.where(eio == idx, slot_of, 0.0), axis=0, keepdims=True).astype(I32) for idx in idxs]
    pos_all = jnp.concatenate(pos, axis=0)
    pos_ref[...] = pos_all
    pos_keep[...] = pos_all
    h2_keep[...] = h2_hi


def _mixer(x, hist, mem_k, mem_v, wts, *, ns, sl, carry, sp_chunk, pairs_per_tile):
    g, r, _ = x.shape
    tm = ns * sl
    nt = r // tm
    ntot = g * nt
    assert 2 * pairs_per_tile >= TOP_K * tm + N_EXPERTS
    tile = lambda s: jnp.minimum(s, ntot - 1)
    const2 = lambda s: (0, 0)
    const3 = lambda s: (0, 0, 0)
    tile_row = lambda s: (tile(s), 0)
    tile_lane = lambda s: (0, tile(s))
    per_seq = lambda s: (tile(s) // nt, 0, 0, 0)
    in_specs = [
        pl.BlockSpec((1, tm, D_MODEL), lambda s: (tile(s) // nt, tile(s) % nt, 0)),
        pl.BlockSpec((1, ns, HIST, C_CONV), per_seq),
        pl.BlockSpec((1, ns, N_MEM, C_XA), per_seq),
        pl.BlockSpec((1, ns, N_MEM, C_XA), per_seq),
        pl.BlockSpec((1, D_MODEL), const2),
        pl.BlockSpec((D_MODEL, IN_COLS), const2),
        pl.BlockSpec((HIST, C_CONV), const2),
        pl.BlockSpec((1, C_CONV), const2),
        pl.BlockSpec((1, C_CONV), const2),
        pl.BlockSpec((1, C_CONV), const2),
        pl.BlockSpec((1, C_GMLP), const2),
        pl.BlockSpec((1, C_GMLP), const2),
        pl.BlockSpec((GMLP_HEADS, GMLP_CHUNK, GMLP_CHUNK), const3),
        pl.BlockSpec((GMLP_CHUNK, C_GMLP), const2),
        pl.BlockSpec((D_MODEL, D_MODEL), const2),
        pl.BlockSpec((1, D_MODEL), const2),
        pl.BlockSpec((N_EXPERTS, D_MODEL), const2),
        pl.BlockSpec((N_EXPERTS, D_MODEL), const2),
        pl.BlockSpec((N_EXPERTS, 1), const2),
    ]
    tile_cnt = lambda s: (tile(s), 0, 0)
    out_specs = [
        pl.BlockSpec((tm, D_MODEL), tile_row),
        pl.BlockSpec((pairs_per_tile * ROW_TILES, LANES),
                     lambda s: (jnp.maximum(s - 1, 0), 0)),
        pl.BlockSpec((TOP_K, tm), tile_lane),
        pl.BlockSpec((TOP_K, tm), tile_lane),
        pl.BlockSpec((1, N_EXPERTS, LANES), tile_cnt),
        pl.BlockSpec((1, N_EXPERTS, LANES), tile_cnt),
        pl.BlockSpec((1, ns, HIST, C_CONV), per_seq),
        pl.BlockSpec((tm, C_GMLP), tile_row),
    ]
    rows = g * r
    out_shape = [
        jax.ShapeDtypeStruct((rows, D_MODEL), F32),
        jax.ShapeDtypeStruct((ntot * pairs_per_tile * ROW_TILES, LANES), U32),
        jax.ShapeDtypeStruct((TOP_K, rows), I32),
        jax.ShapeDtypeStruct((TOP_K, rows), F32),
        jax.ShapeDtypeStruct((ntot, N_EXPERTS, LANES), I32),
        jax.ShapeDtypeStruct((ntot, N_EXPERTS, LANES), I32),
        jax.ShapeDtypeStruct((g, ns, HIST, C_CONV), F32),
        jax.ShapeDtypeStruct((rows, C_GMLP), F32),
    ]
    kern = functools.partial(_mixer_kernel, ns=ns, sl=sl, carry=carry, sp_chunk=sp_chunk,
                             tiles_per_seq=nt, n_sorted=2 * pairs_per_tile)
    return pl.pallas_call(
        kern,
        grid=(ntot + 1,),
        in_specs=in_specs,
        out_specs=out_specs,
        out_shape=out_shape,
        scratch_shapes=[pltpu.VMEM((ns, HIST + sl, C_CONV), F32),
                        pltpu.VMEM((ns, SUBLANES - 1, HIST + sl, C_CONV), F32),
                        pltpu.VMEM((tm, D_MODEL), BF16),
                        pltpu.VMEM((TOP_K, tm), I32),
                        pltpu.VMEM((tm, tm), BF16),
                        pltpu.VMEM((GMLP_HEADS, GMLP_CHUNK, GMLP_CHUNK), BF16)],
        compiler_params=pltpu.CompilerParams(
            dimension_semantics=("arbitrary",),
            vmem_limit_bytes=VMEM_LIMIT_MIXER),
        name="mixer_carry" if carry else "mixer_cache",
    )(x, hist, mem_k, mem_v, *wts)


def _strip_pieces(n, max_rows, fn):
    done = 0
    p = max_rows
    while p >= 1:
        has = (n & p) != 0
        pl.when(has)(functools.partial(fn, done, p))
        done = done + jnp.where(has, p, 0)
        p //= 2


class _Table:
    def __init__(self, ref, offset):
        self.ref, self.offset = ref, offset

    def __getitem__(self, i):
        return self.ref[self.offset + i]


def _pack_tables(tables):
    offsets = tuple(int(o) for o in np.cumsum([0] + [t.shape[0] for t in tables[:-1]]))
    return jnp.concatenate([t.astype(I32) for t in tables]), offsets


def _pow2_at_most(n):
    return 1 << (n.bit_length() - 1)


def _rows(ref, first_row, rows):
    return ref.at[pl.ds(pl.multiple_of(first_row * ROW_TILES, ROW_TILES), rows * ROW_TILES)]


def _moe_kernel(tab_ref, xs_hbm, xs2_hbm, wgu_ref, bgu_ref, wdn_ref, bdn_ref, ys_hbm,
                xbuf, ybuf, wgu16, wdn16, sems, ysems, *, offsets, n_blocks):
    tabs = [_Table(tab_ref, o) for o in offsets]
    b0_ref, nbk_ref, bv_ref, nb_ref = tabs[:4]
    strips_a, strips_b = tabs[4:9], tabs[9:14]
    e = pl.program_id(0)
    n_used = nb_ref[0]
    half = MOE_PAIRS // 2

    def gather_start(b, slot):
        pair0 = b * MOE_PAIRS

        for src_hbm, (slo_ref, shi_ref, ssrc_ref, sdst_ref, slen_ref) in ((xs_hbm, strips_a), (xs2_hbm, strips_b)):
            def strip(s, carry, src_hbm=src_hbm, ssrc_ref=ssrc_ref, sdst_ref=sdst_ref, slen_ref=slen_ref):
                lo = jnp.maximum(sdst_ref[s], pair0)
                hi = jnp.minimum(sdst_ref[s] + slen_ref[s], pair0 + MOE_PAIRS)
                src0 = ssrc_ref[s] + (lo - sdst_ref[s])
                dst0 = lo - pair0

                def piece(first, rows):
                    pltpu.make_async_copy(_rows(src_hbm, src0 + first, rows),
                                          _rows(xbuf.at[slot], dst0 + first, rows), sems.at[slot]).start()
                _strip_pieces(hi - lo, MOE_PAIRS, piece)
                return carry
            lax.fori_loop(slo_ref[b], shi_ref[b], strip, 0)

    def gather_wait(b, slot):
        def piece(first, rows):
            del first
            pltpu.make_async_copy(_rows(xs_hbm, 0, rows), _rows(xbuf.at[slot], 0, rows), sems.at[slot]).wait()
        _strip_pieces(bv_ref[b], MOE_PAIRS, piece)

    def out_copy(b, slot):
        return pltpu.make_async_copy(ybuf.at[slot], _rows(ys_hbm, b * MOE_PAIRS, MOE_PAIRS), ysems.at[slot])

    def expert_mlp(xslot, slot, h):
        xb = pltpu.bitcast(_tiles_to_rows(xbuf.at[xslot], half, h * half), BF16)
        gu = jnp.dot(xb, wgu16[...], preferred_element_type=F32) + bgu_ref[0]
        gate = jnp.minimum(gu[:, :D_FF], SWIGLU_LIMIT)
        up = jnp.clip(gu[:, D_FF:], -SWIGLU_LIMIT, SWIGLU_LIMIT)
        act = (up + 1.0) * (gate * _sigmoid(SWIGLU_ALPHA * gate))
        yb = jnp.dot(act.astype(BF16), wdn16[...], preferred_element_type=F32) + bdn_ref[0]
        _rows_to_tiles(ybuf.at[slot], pltpu.bitcast(yb.astype(BF16), U32), half, h * half)

    @pl.when(e == 0)
    def _():
        xbuf[...] = jnp.zeros_like(xbuf)
        for a in range(GATHER_AHEAD):
            @pl.when(a < n_used)
            def _():
                gather_start(a, a)

    @pl.when(nbk_ref[e] > 0)
    def _():
        def cast_rows(c, carry):
            r = pl.multiple_of(c * LANES, LANES)
            wgu16[pl.ds(r, LANES), :] = wgu_ref[pl.ds(r, LANES), :].astype(BF16)
            wdn16[pl.ds(r, LANES), :] = wdn_ref[pl.ds(r, LANES), :].astype(BF16)
            return carry
        lax.fori_loop(0, D_MODEL // LANES, cast_rows, 0)

        def block(j, carry):
            b = b0_ref[e] + j
            slot = b % 2
            xslot = b % (GATHER_AHEAD + 1)

            @pl.when(b + GATHER_AHEAD < n_used)
            def _():
                gather_start(b + GATHER_AHEAD, (b + GATHER_AHEAD) % (GATHER_AHEAD + 1))

            gather_wait(b, xslot)

            @pl.when(b >= 2)
            def _():
                out_copy(b - 2, slot).wait()

            valid = bv_ref[b]

            @pl.when(valid > half)
            def _():
                expert_mlp(xslot, slot, 0)
                expert_mlp(xslot, slot, 1)

            @pl.when(valid <= half)
            def _():
                expert_mlp(xslot, slot, 0)
                ybuf[slot, pl.ds(half * ROW_TILES, half * ROW_TILES), :] = jnp.zeros(
                    (half * ROW_TILES, LANES), U32)

            out_copy(b, slot).start()
            return carry
        lax.fori_loop(0, nbk_ref[e], block, 0)

    @pl.when(e == pl.num_programs(0) - 1)
    def _():
        for back in (1, 2):
            @pl.when(n_used >= back)
            def _():
                out_copy(n_used - back, (n_used - back) % 2).wait()
        ybuf[0] = jnp.zeros(ybuf.shape[1:], U32)

        def start_fill(b, carry):
            out_copy(b, 0).start()
            return carry

        def wait_fill(b, carry):
            out_copy(b, 0).wait()
            return carry
        lax.fori_loop(n_used, n_blocks, start_fill, 0)
        lax.fori_loop(n_used, n_blocks, wait_fill, 0)


def _moe(tables, xs, xs2, n_blocks, w_gu, b_gu, w_dn, b_dn):
    wsel = lambda e, *_: (e, 0, 0)
    packed, offsets = _pack_tables(tables)
    slots = n_blocks * MOE_PAIRS
    grid_spec = pltpu.PrefetchScalarGridSpec(
        num_scalar_prefetch=1,
        grid=(N_EXPERTS,),
        in_specs=[pl.BlockSpec(memory_space=pl.ANY),
                  pl.BlockSpec(memory_space=pl.ANY),
                  pl.BlockSpec((None, D_MODEL, 2 * D_FF), wsel),
                  pl.BlockSpec((None, 1, 2 * D_FF), wsel),
                  pl.BlockSpec((None, D_FF, D_MODEL), wsel),
                  pl.BlockSpec((None, 1, D_MODEL), wsel)],
        out_specs=pl.BlockSpec(memory_space=pl.ANY),
        scratch_shapes=[pltpu.VMEM((GATHER_AHEAD + 1, MOE_PAIRS * ROW_TILES, LANES), U32),
                        pltpu.VMEM((2, MOE_PAIRS * ROW_TILES, LANES), U32),
                        pltpu.VMEM((D_MODEL, 2 * D_FF), BF16),
                        pltpu.VMEM((D_FF, D_MODEL), BF16),
                        pltpu.SemaphoreType.DMA((GATHER_AHEAD + 1,)),
                        pltpu.SemaphoreType.DMA((2,))],
    )
    return pl.pallas_call(
        functools.partial(_moe_kernel, offsets=offsets, n_blocks=n_blocks),
        grid_spec=grid_spec,
        out_shape=jax.ShapeDtypeStruct((slots * ROW_TILES, LANES), U32),
        compiler_params=pltpu.CompilerParams(
            dimension_semantics=("arbitrary",),
            vmem_limit_bytes=VMEM_LIMIT_MOE),
        name="moe",
    )(packed, xs, xs2, w_gu, b_gu, w_dn, b_dn)


def _combine_kernel(tab_ref, x1_ref, pos_ref, gate_ref, gfin_ref, ys_hbm,
                    out_ref, ybuf, sems, *, tm, pairs_per_tile, offsets):
    csrc_ref, clen_ref, coff_ref, ctot_ref = [_Table(tab_ref, o) for o in offsets]
    i = pl.program_id(0)
    n_steps = pl.num_programs(0)
    n_sorted = 2 * pairs_per_tile
    n_slots = 2 * TILES_PER_STEP

    def gather_start(t, slot):
        def strip(e, carry):
            s = t * N_EXPERTS + e
            src0 = csrc_ref[s]
            dst0 = coff_ref[s]

            def piece(first, rows):
                pltpu.make_async_copy(_rows(ys_hbm, src0 + first, rows),
                                      _rows(ybuf.at[slot], dst0 + first, rows), sems.at[slot]).start()
            _strip_pieces(clen_ref[s], tm // 2, piece)
            return carry
        lax.fori_loop(0, N_EXPERTS, strip, 0)

    @pl.when(i == 0)
    def _():
        ybuf[...] = jnp.zeros_like(ybuf)
        for h in range(TILES_PER_STEP):
            gather_start(h, h)

    @pl.when(i + 1 < n_steps)
    def _():
        for h in range(TILES_PER_STEP):
            t = (i + 1) * TILES_PER_STEP + h
            gather_start(t, t % n_slots)

    slots = [(i * TILES_PER_STEP + h) % n_slots for h in range(TILES_PER_STEP)]
    for h in range(TILES_PER_STEP):
        def wait_piece(first, rows, slot=slots[h]):
            del first
            pltpu.make_async_copy(_rows(ys_hbm, 0, rows), _rows(ybuf.at[slot], 0, rows), sems.at[slot]).wait()
        _strip_pieces(ctot_ref[i * TILES_PER_STEP + h], _pow2_at_most(pairs_per_tile), wait_piece)

    rr = lax.broadcasted_iota(I32, (tm, tm), 0)
    cc = lax.broadcasted_iota(I32, (tm, tm), 1)
    eye = rr == cc
    jl = lax.broadcasted_iota(I32, (tm, n_sorted), 1).astype(jnp.int16)
    for h in range(TILES_PER_STEP):
        tok = slice(h * tm, (h + 1) * tm)
        y_sorted = pltpu.bitcast(_tiles_to_rows(ybuf.at[slots[h]], pairs_per_tile), BF16)
        unsort = jnp.zeros((tm, n_sorted), BF16)
        for k in range(TOP_K):
            p_col = jnp.sum(jnp.where(eye, pos_ref[k:k + 1, tok].astype(F32), 0.0), axis=1, keepdims=True)
            g_col = jnp.sum(jnp.where(eye, gate_ref[k:k + 1, tok], 0.0), axis=1, keepdims=True)
            unsort = jnp.where(jl == p_col.astype(jnp.int16), g_col.astype(BF16), unsort)
        acc = x1_ref[tok, :] + jnp.dot(unsort, y_sorted, preferred_element_type=F32)
        out_ref[tok, :] = _rmsnorm(acc, gfin_ref[...])


def _combine(tables, x1, pos, gates, g_final, ys, tm, pairs_per_tile):
    t = x1.shape[0]
    nt = t // tm
    assert nt % TILES_PER_STEP == 0
    packed, offsets = _pack_tables(tables)
    grid_spec = pltpu.PrefetchScalarGridSpec(
        num_scalar_prefetch=1,
        grid=(nt // TILES_PER_STEP,),
        in_specs=[pl.BlockSpec((TILES_PER_STEP * tm, D_MODEL), lambda i, *_: (i, 0)),
                  pl.BlockSpec((TOP_K, TILES_PER_STEP * tm), lambda i, *_: (0, i)),
                  pl.BlockSpec((TOP_K, TILES_PER_STEP * tm), lambda i, *_: (0, i)),
                  pl.BlockSpec((1, D_MODEL), lambda i, *_: (0, 0)),
                  pl.BlockSpec(memory_space=pl.ANY)],
        out_specs=pl.BlockSpec((TILES_PER_STEP * tm, D_MODEL), lambda i, *_: (i, 0)),
        scratch_shapes=[pltpu.VMEM((2 * TILES_PER_STEP, pairs_per_tile * ROW_TILES, LANES), U32),
                        pltpu.SemaphoreType.DMA((2 * TILES_PER_STEP,))],
    )
    return pl.pallas_call(
        functools.partial(_combine_kernel, tm=tm, pairs_per_tile=pairs_per_tile, offsets=offsets),
        grid_spec=grid_spec,
        out_shape=jax.ShapeDtypeStruct((t, D_MODEL), F32),
        compiler_params=pltpu.CompilerParams(
            dimension_semantics=("arbitrary",),
            vmem_limit_bytes=VMEM_LIMIT_MIXER),
        name="combine",
    )(packed, x1, pos, gates, g_final, ys)


def _split_bf16(w):
    hi = w.astype(BF16)
    lo = (w - hi.astype(F32)).astype(BF16)
    return hi, lo


def kernel(x_prompt, x_sample, cache_conv, cache_mem_k, cache_mem_v, mem_prompt, g_mix, w_in, w_dw, b_dw, ln_conv_g, ln_conv_b, ln_v_g, ln_v_b, w_spatial, b_spatial, g_mem, w_mem_k, w_mem_v, w_out, g_ffn, w_router, b_router, w_gate_up, b_gate_up, w_down, b_down, g_final):
    depth = g_mix.shape[0]
    assert depth == 1
    l = 0
    bp, seq, _ = x_prompt.shape
    bs, dseq, _ = x_sample.shape
    assert seq % PROMPT_TILE == 0 and bs % SAMPLE_SEQS_PER_TILE == 0
    assert GMLP_CHUNK % dseq == 0 and (SAMPLE_SEQS_PER_TILE * dseq) % GMLP_CHUNK == 0

    row = lambda a: a.reshape(1, -1)
    wr_hi, wr_lo = _split_bf16(w_router[l].T)
    w_dw_pad = jnp.pad(w_dw[l], ((0, HIST - CONV_W), (0, 0)))
    bias_rows = lambda b: jnp.repeat(b.T, GMLP_HD, axis=1)
    common = dict(
        gmix=row(g_mix[l]), win=w_in[l].astype(BF16), wdw=w_dw_pad, bdw=row(b_dw[l]),
        lcg=row(ln_conv_g[l]), lcb=row(ln_conv_b[l]), lvg=row(ln_v_g[l]), lvb=row(ln_v_b[l]),
        wout=w_out[l].astype(BF16), gffn=row(g_ffn[l]), wrh=wr_hi, wrl=wr_lo,
        br=b_router[l].reshape(N_EXPERTS, 1))

    def weights(wsp, bsp):
        c = common
        return (c["gmix"], c["win"], c["wdw"], c["bdw"], c["lcg"], c["lcb"], c["lvg"], c["lvb"],
                wsp, bsp, c["wout"], c["gffn"], c["wrh"], c["wrl"], c["br"])

    reps = GMLP_CHUNK // dseq
    wts_p = weights(w_spatial[l], bias_rows(b_spatial[l]))
    wts_s = weights(jnp.tile(w_spatial[l][:, :dseq, :dseq], (1, reps, reps)),
                    bias_rows(jnp.tile(b_spatial[l][:, :dseq], (1, reps))))

    w_kv = jnp.concatenate([w_mem_k[l], w_mem_v[l]], axis=1).astype(BF16)
    kv_p = _memkv(mem_prompt, row(g_mem[l]), w_kv)
    mk_p = kv_p[:, :, :C_XA]
    mv_p = kv_p[:, :, C_XA:]
    zero_hist = jnp.zeros((bp, 1, HIST, C_CONV), F32)
    tp, ts = bp * seq, bs * dseq
    tm_s = SAMPLE_SEQS_PER_TILE * dseq
    ntp, nts = tp // PROMPT_TILE, ts // tm_s
    ppt_p = (TOP_K * PROMPT_TILE + N_EXPERTS + 1) // 2
    ppt_s = (TOP_K * tm_s + N_EXPERTS + 1) // 2
    (x1_p, xs_p, pos_p, gate_p, cnt_p, off_p, hist_p, _) = _mixer(
        x_prompt, zero_hist, mk_p[:, None], mv_p[:, None], wts_p,
        ns=1, sl=PROMPT_TILE, carry=True, sp_chunk=GMLP_CHUNK, pairs_per_tile=ppt_p)

    gs = bs // SAMPLE_SEQS_PER_TILE
    hist_s_in = jnp.pad(cache_conv[l], ((0, 0), (HIST_OFF, 0), (0, 0))).reshape(
        gs, SAMPLE_SEQS_PER_TILE, HIST, C_CONV)
    mk_s = cache_mem_k[l].reshape(gs, SAMPLE_SEQS_PER_TILE, N_MEM, C_XA)
    mv_s = cache_mem_v[l].reshape(gs, SAMPLE_SEQS_PER_TILE, N_MEM, C_XA)
    (x1_s, xs_s, pos_s, gate_s, cnt_s, off_s, hist_s, v_s) = _mixer(
        x_sample.reshape(gs, tm_s, D_MODEL), hist_s_in, mk_s, mv_s, wts_s,
        ns=SAMPLE_SEQS_PER_TILE, sl=dseq, carry=False, sp_chunk=dseq, pairs_per_tile=ppt_s)

    max_pairs = ((tp + ts) * TOP_K + (ntp + nts) * N_EXPERTS) // 2
    n_blocks = -(-max_pairs // MOE_PAIRS) + N_EXPERTS
    cnt = jnp.concatenate([cnt_p[:, :, 0], cnt_s[:, :, 0]], axis=0)
    off = jnp.concatenate([off_p[:, :, 0], off_s[:, :, 0]], axis=0)
    tile_pair0 = np.concatenate([np.arange(ntp) * ppt_p, np.arange(nts) * ppt_s]).astype(np.int32)
    counts = jnp.sum(cnt, axis=0)
    tile_base = jnp.cumsum(cnt, axis=0) - cnt
    padded = (counts + MOE_PAIRS - 1) // MOE_PAIRS * MOE_PAIRS
    pad_end = jnp.cumsum(padded)
    pad_start = pad_end - padded
    strip_dst = (pad_start[None, :] + tile_base).astype(I32)
    strip_src = (tile_pair0[:, None] + off).astype(I32)
    n_used = (pad_end[-1] // MOE_PAIRS).astype(I32)
    blk_pair0 = jnp.minimum(jnp.arange(n_blocks, dtype=I32), n_used - 1) * MOE_PAIRS
    block_expert = jnp.minimum(
        jnp.sum((pad_end[None, :] <= blk_pair0[:, None]).astype(I32), axis=1), N_EXPERTS - 1)
    of_block = block_expert[:, None] == jnp.arange(N_EXPERTS, dtype=I32)[None, :]
    last_pair = jnp.sum(jnp.where(of_block, (pad_start + counts)[None, :], 0), axis=1)
    block_valid = jnp.clip(last_pair - blk_pair0, 0, MOE_PAIRS).astype(I32)
    def strip_group(t0, t1):
        sdst = strip_dst[t0:t1].T.reshape(-1)
        ssrc = strip_src[t0:t1].T.reshape(-1)
        slen = cnt[t0:t1].T.reshape(-1).astype(I32)
        s_lo = jnp.sum(((sdst + slen)[None, :] <= blk_pair0[:, None]).astype(I32), axis=1)
        s_hi = jnp.sum((sdst[None, :] < (blk_pair0 + MOE_PAIRS)[:, None]).astype(I32), axis=1)
        return (s_lo, s_hi, ssrc, sdst, slen)

    moe_tables = ((pad_start // MOE_PAIRS).astype(I32), (padded // MOE_PAIRS).astype(I32),
                  block_valid, n_used.reshape(1),
                  *strip_group(0, ntp), *strip_group(ntp, ntp + nts))

    ys = _moe(moe_tables, xs_p, xs_s, n_blocks,
              w_gate_up[l], b_gate_up[l][:, None, :], w_down[l], b_down[l][:, None, :])
    gfin = row(g_final)

    def combine_tables(t0, t1):
        return (strip_dst[t0:t1].reshape(-1), cnt[t0:t1].reshape(-1).astype(I32),
                off[t0:t1].reshape(-1).astype(I32), jnp.sum(cnt[t0:t1], axis=1).astype(I32))

    y_p = _combine(combine_tables(0, ntp), x1_p, pos_p, gate_p, gfin, ys, PROMPT_TILE, ppt_p)
    y_s = _combine(combine_tables(ntp, ntp + nts), x1_s, pos_s, gate_s, gfin, ys, tm_s, ppt_s)

    return (y_p.reshape(bp, seq, D_MODEL),
            y_s.reshape(bs, dseq, D_MODEL),
            hist_p[:, 0, HIST_OFF:, :][None],
            mk_p.reshape(bp, N_MEM, XA_HEADS, XA_HD)[None],
            mv_p.reshape(bp, N_MEM, XA_HEADS, XA_HD)[None],
            hist_s.reshape(bs, HIST, C_CONV)[:, HIST_OFF:, :][None],
            v_s.reshape(bs, dseq, C_GMLP)[None])
```

```python
import functools

import numpy as np
import jax
import jax.numpy as jnp
from jax import lax
from jax.experimental import pallas as pl
from jax.experimental.pallas import tpu as pltpu

F32 = jnp.float32
BF16 = jnp.bfloat16
I32 = jnp.int32
U32 = jnp.uint32

D_MODEL = 1024
C_CONV = 384
CONV_W = 31
C_GMLP = 384
GMLP_HEADS = 4
GMLP_HD = 96
GMLP_CHUNK = 128
XA_HEADS = 4
XA_HD = 64
C_XA = 256
N_MEM = 256
N_EXPERTS = 32
TOP_K = 4
D_FF = 1024
SWIGLU_LIMIT = 7.0
SWIGLU_ALPHA = 1.702
EPS = 1e-5
IN_COLS = 2 * C_CONV + 2 * C_GMLP + C_XA

SUBLANES = 8
LANES = 128
ROW_TILES = D_MODEL // LANES
HIST = 32
HIST_OFF = HIST - (CONV_W - 1)

PROMPT_TILE = 512
SAMPLE_SEQS_PER_TILE = 8
MOE_BLOCK = 512
MOE_PAIRS = MOE_BLOCK // 2
GATHER_AHEAD = 2
TILES_PER_STEP = 2
VMEM_LIMIT_MIXER = 48 * 1024 * 1024
VMEM_LIMIT_MOE = 52 * 1024 * 1024


def _rmsnorm(x, g):
    return x * lax.rsqrt(jnp.mean(x * x, axis=-1, keepdims=True) + EPS) * g


def _layernorm(x, g, b):
    mu = jnp.mean(x, axis=-1, keepdims=True)
    xc = x - mu
    var = jnp.mean(xc * xc, axis=-1, keepdims=True)
    return xc * lax.rsqrt(var + EPS) * g + b


def _gelu(x):
    return 0.5 * x * (1.0 + lax.erf(x * np.float32(1.0 / np.sqrt(2.0))))


def _sigmoid(x):
    return 1.0 / (1.0 + jnp.exp(-x))


def _rows_to_tiles(dst_ref, val, rows, row0=0):
    for j in range(ROW_TILES):
        dst_ref[pl.ds(row0 * ROW_TILES + j, rows, stride=ROW_TILES), :] = val[:, j * LANES:(j + 1) * LANES]


def _tiles_to_rows(src_ref, rows, row0=0):
    return jnp.concatenate(
        [src_ref[pl.ds(row0 * ROW_TILES + j, rows, stride=ROW_TILES), :] for j in range(ROW_TILES)],
        axis=-1)


def _memkv_kernel(mem_ref, g_ref, w_ref, o_ref):
    mn = _rmsnorm(mem_ref[0], g_ref[...])
    o_ref[0] = jnp.dot(mn.astype(BF16), w_ref[...], preferred_element_type=F32)


def _memkv(mem, g_mem, w_kv):
    b = mem.shape[0]
    return pl.pallas_call(
        _memkv_kernel,
        grid=(b,),
        in_specs=[pl.BlockSpec((1, N_MEM, D_MODEL), lambda i: (i, 0, 0)),
                  pl.BlockSpec((1, D_MODEL), lambda i: (0, 0)),
                  pl.BlockSpec((D_MODEL, 2 * C_XA), lambda i: (0, 0))],
        out_specs=pl.BlockSpec((1, N_MEM, 2 * C_XA), lambda i: (i, 0, 0)),
        out_shape=jax.ShapeDtypeStruct((b, N_MEM, 2 * C_XA), F32),
        name="memkv",
    )(mem, g_mem, w_kv)


def _mixer_kernel(x_ref, hist_ref, mk_ref, mv_ref, gmix_ref, win_ref, wdw_ref, bdw_ref,
                  lcg_ref, lcb_ref, lvg_ref, lvb_ref, wsp_ref, bsp_ref, wout_ref, gffn_ref,
                  wrh_ref, wrl_ref, br_ref, *rest, ns, sl, carry, sp_chunk, tiles_per_seq, n_sorted):
    (x1_ref, xst_ref, pos_ref, gate_ref, cnt_ref, off_ref, histout_ref, v_ref,
     ext_ref, shift_ref, h2_keep, pos_keep, before_ref, wsp16_ref) = rest
    tm = ns * sl
    step = pl.program_id(0)
    n_tiles = pl.num_programs(0) - 1

    def sort_previous_tile():
        jj = lax.broadcasted_iota(I32, (n_sorted, tm), 0).astype(jnp.int16)
        perm = jnp.zeros((n_sorted, tm), BF16)
        for k in range(TOP_K):
            perm = jnp.where(jj == pos_keep[k:k + 1, :].astype(jnp.int16), jnp.ones((), BF16), perm)
        sorted_rows = jnp.dot(perm, h2_keep[...], preferred_element_type=F32)
        pairs = pltpu.bitcast(sorted_rows.astype(BF16), U32)
        _rows_to_tiles(xst_ref, pairs, n_sorted // 2)

    @pl.when(step == 0)
    def _():
        h2_keep[...] = jnp.zeros_like(h2_keep)
        pos_keep[...] = jnp.full(pos_keep.shape, -1, I32)
        tr = lax.broadcasted_iota(I32, (tm, tm), 0)
        tc = lax.broadcasted_iota(I32, (tm, tm), 1)
        before_ref[...] = jnp.where(tr < tc, 1.0, 0.0).astype(BF16)
        rr = lax.broadcasted_iota(I32, (GMLP_CHUNK, GMLP_CHUNK), 0)
        cc = lax.broadcasted_iota(I32, (GMLP_CHUNK, GMLP_CHUNK), 1)
        sp_mask = (cc <= rr) & ((rr // sp_chunk) == (cc // sp_chunk))
        for hh in range(GMLP_HEADS):
            wsp16_ref[hh] = jnp.where(sp_mask, wsp_ref[hh], 0.0).astype(BF16)

    @pl.when(step < n_tiles)
    def _():
        sort_previous_tile()
        _mixer_tile(x_ref, hist_ref, mk_ref, mv_ref, gmix_ref, win_ref, wdw_ref, bdw_ref,
                    lcg_ref, lcb_ref, lvg_ref, lvb_ref, wsp16_ref, bsp_ref, wout_ref, gffn_ref,
                    wrh_ref, wrl_ref, br_ref, x1_ref, pos_ref, gate_ref, cnt_ref, off_ref,
                    histout_ref, v_ref, ext_ref, shift_ref, h2_keep, pos_keep, before_ref,
                    first_of_seq=(step % tiles_per_seq) == 0,
                    ns=ns, sl=sl, carry=carry)

    @pl.when(step == n_tiles)
    def _():
        sort_previous_tile()


def _mixer_tile(x_ref, hist_ref, mk_ref, mv_ref, gmix_ref, win_ref, wdw_ref, bdw_ref,
                lcg_ref, lcb_ref, lvg_ref, lvb_ref, wsp16_ref, bsp_ref, wout_ref, gffn_ref,
                wrh_ref, wrl_ref, br_ref, x1_ref, pos_ref, gate_ref, cnt_ref, off_ref,
                histout_ref, v_ref, ext_ref, shift_ref, h2_keep, pos_keep, before_ref,
                *, first_of_seq, ns, sl, carry):
    tm = ns * sl
    x = x_ref[0]
    h = _rmsnorm(x, gmix_ref[...])
    z = jnp.dot(h.astype(BF16), win_ref[...], preferred_element_type=F32)
    z_a = z[:, 0:C_CONV]
    z_g = z[:, C_CONV:2 * C_CONV]
    z_u = z[:, 2 * C_CONV:2 * C_CONV + C_GMLP]
    z_v = z[:, 2 * C_CONV + C_GMLP:2 * C_CONV + 2 * C_GMLP]
    z_q = z[:, 2 * C_CONV + 2 * C_GMLP:IN_COLS]

    glu = z_a * _sigmoid(z_g)
    if carry:
        ext_ref[:, 0:HIST, :] = jnp.where(first_of_seq, hist_ref[0], ext_ref[:, 0:HIST, :])
    else:
        ext_ref[:, 0:HIST, :] = hist_ref[0]
    conv_parts = []
    for s in range(ns):
        ext_s = ext_ref.at[s]
        ext_s[HIST:HIST + sl, :] = glu[s * sl:(s + 1) * sl]
        n_shift = HIST + sl - SUBLANES
        for r in range(1, SUBLANES):
            shift_ref[s, r - 1, 0:n_shift, :] = ext_s[pl.ds(r, n_shift), :]
        rc = min(sl, 64)
        for r0 in range(0, sl, rc):
            acc = jnp.broadcast_to(bdw_ref[...], (rc, C_CONV))
            for j in range(CONV_W):
                a, r = divmod(j + HIST_OFF, SUBLANES)
                src = ext_s if r == 0 else shift_ref.at[s, r - 1]
                acc = acc + wdw_ref[j:j + 1, :] * src[pl.ds(r0 + a * SUBLANES, rc), :]
            conv_parts.append(acc)
        new_hist = ext_s[sl:sl + HIST, :]
        histout_ref[0, s] = new_hist
        if carry:
            ext_s[0:HIST, :] = new_hist
    y = jnp.concatenate(conv_parts, axis=0) if len(conv_parts) > 1 else conv_parts[0]
    y = _layernorm(y, lcg_ref[...], lcb_ref[...])
    c_out = y * _sigmoid(y)

    u = _gelu(z_u)
    v = _layernorm(_gelu(z_v), lvg_ref[...], lvb_ref[...])
    v_ref[...] = v
    vb = v.astype(BF16)
    col = lax.broadcasted_iota(I32, (GMLP_CHUNK, C_GMLP), 1)
    w_heads = [wsp16_ref[hh] for hh in range(GMLP_HEADS)]
    g_parts = []
    for c in range(tm // GMLP_CHUNK):
        vc = vb[c * GMLP_CHUNK:(c + 1) * GMLP_CHUNK]
        sg = bsp_ref[...]
        for hh in range(GMLP_HEADS):
            head_cols = (col >= hh * GMLP_HD) & (col < (hh + 1) * GMLP_HD)
            vh = jnp.where(head_cols, vc, jnp.zeros_like(vc))
            sg = sg + jnp.dot(w_heads[hh], vh, preferred_element_type=F32)
        g_parts.append(u[c * GMLP_CHUNK:(c + 1) * GMLP_CHUNK] * sg)
    g_out = jnp.concatenate(g_parts, axis=0) if len(g_parts) > 1 else g_parts[0]

    qs = z_q * np.float32(XA_HD ** -0.5)
    qcol = lax.broadcasted_iota(I32, (sl, C_XA), 1)
    a_parts = []
    for s in range(ns):
        q_s = qs[s * sl:(s + 1) * sl]
        kb = mk_ref[0, s].astype(BF16)
        vvb = mv_ref[0, s].astype(BF16)
        hmasks = [(qcol >= hh * XA_HD) & (qcol < (hh + 1) * XA_HD) for hh in range(XA_HEADS)]
        stack = XA_HEADS if sl * XA_HEADS <= N_MEM else 1
        a_s = jnp.zeros((sl, C_XA), F32)
        for h0 in range(0, XA_HEADS, stack):
            heads = range(h0, h0 + stack)
            qh = jnp.concatenate([jnp.where(hmasks[hh], q_s, 0.0) for hh in heads], axis=0).astype(BF16)
            sc = lax.dot_general(qh, kb, (((1,), (1,)), ((), ())), preferred_element_type=F32)
            p = jnp.exp(sc - jnp.max(sc, axis=-1, keepdims=True))
            den = jnp.sum(p, axis=-1, keepdims=True)
            oh = jnp.dot(p.astype(BF16), vvb, preferred_element_type=F32) / den
            for n, hh in enumerate(heads):
                a_s = a_s + jnp.where(hmasks[hh], oh[n * sl:(n + 1) * sl], 0.0)
        a_parts.append(a_s)
    a_out = jnp.concatenate(a_parts, axis=0) if len(a_parts) > 1 else a_parts[0]

    mix = jnp.concatenate([c_out, g_out, a_out], axis=-1).astype(BF16)
    x1 = x + jnp.dot(mix, wout_ref[...], preferred_element_type=F32)
    x1_ref[...] = x1

    h2 = _rmsnorm(x1, gffn_ref[...])
    h2_hi = h2.astype(BF16)
    h2_lo = (h2 - h2_hi.astype(F32)).astype(BF16)
    nt_dims = (((1,), (1,)), ((), ()))
    lg = (lax.dot_general(wrh_ref[...], h2_hi, nt_dims, preferred_element_type=F32)
          + lax.dot_general(wrl_ref[...], h2_hi, nt_dims, preferred_element_type=F32)
          + lax.dot_general(wrh_ref[...], h2_lo, nt_dims, preferred_element_type=F32)
          + br_ref[...])
    eio = lax.broadcasted_iota(I32, (N_EXPERTS, tm), 0)
    work = lg
    vals, idxs = [], []
    for _ in range(TOP_K):
        m = jnp.max(work, axis=0, keepdims=True)
        idx = jnp.min(jnp.where(work == m, eio, N_EXPERTS), axis=0, keepdims=True)
        vals.append(m)
        idxs.append(idx)
        work = jnp.where(eio == idx, -jnp.inf, work)
    exps = [jnp.exp(vk - vals[0]) for vk in vals]
    den = exps[0] + exps[1] + exps[2] + exps[3]
    gate_ref[...] = jnp.concatenate([ek / den for ek in exps], axis=0)

    sel = jnp.zeros((N_EXPERTS, tm), F32)
    for idx in idxs:
        sel = sel + jnp.where(eio == idx, 1.0, 0.0)
    selb = sel.astype(BF16)
    ranks = jnp.dot(selb, before_ref[...], preferred_element_type=F32)
    er = lax.broadcasted_iota(I32, (N_EXPERTS, N_EXPERTS), 0)
    ec = lax.broadcasted_iota(I32, (N_EXPERTS, N_EXPERTS), 1)
    lower = jnp.where(ec < er, 1.0, 0.0).astype(BF16)
    cnt = jnp.sum(sel, axis=1, keepdims=True)
    half_len = jnp.floor((cnt + 1.0) * 0.5)
    off_pairs = jnp.dot(lower, jnp.broadcast_to(half_len, (N_EXPERTS, LANES)).astype(BF16),
                        preferred_element_type=F32)[:, 0:1]
    cnt_ref[0] = jnp.broadcast_to(half_len, (N_EXPERTS, LANES)).astype(I32)
    off_ref[0] = jnp.broadcast_to(off_pairs, (N_EXPERTS, LANES)).astype(I32)
    slot_of = 2.0 * off_pairs + ranks
    pos = [jnp.sum(jnp.where(eio == idx, slot_of, 0.0), axis=0, keepdims=True).astype(I32) for idx in idxs]
    pos_all = jnp.concatenate(pos, axis=0)
    pos_ref[...] = pos_all
    pos_keep[...] = pos_all
    h2_keep[...] = h2_hi


def _mixer(x, hist, mem_k, mem_v, wts, *, ns, sl, carry, sp_chunk, pairs_per_tile):
    g, r, _ = x.shape
    tm = ns * sl
    nt = r // tm
    ntot = g * nt
    assert 2 * pairs_per_tile >= TOP_K * tm + N_EXPERTS
    tile = lambda s: jnp.minimum(s, ntot - 1)
    const2 = lambda s: (0, 0)
    const3 = lambda s: (0, 0, 0)
    tile_row = lambda s: (tile(s), 0)
    tile_lane = lambda s: (0, tile(s))
    per_seq = lambda s: (tile(s) // nt, 0, 0, 0)
    in_specs = [
        pl.BlockSpec((1, tm, D_MODEL), lambda s: (tile(s) // nt, tile(s) % nt, 0)),
        pl.BlockSpec((1, ns, HIST, C_CONV), per_seq),
        pl.BlockSpec((1, ns, N_MEM, C_XA), per_seq),
        pl.BlockSpec((1, ns, N_MEM, C_XA), per_seq),
        pl.BlockSpec((1, D_MODEL), const2),
        pl.BlockSpec((D_MODEL, IN_COLS), const2),
        pl.BlockSpec((HIST, C_CONV), const2),
        pl.BlockSpec((1, C_CONV), const2),
        pl.BlockSpec((1, C_CONV), const2),
        pl.BlockSpec((1, C_CONV), const2),
        pl.BlockSpec((1, C_GMLP), const2),
        pl.BlockSpec((1, C_GMLP), const2),
        pl.BlockSpec((GMLP_HEADS, GMLP_CHUNK, GMLP_CHUNK), const3),
        pl.BlockSpec((GMLP_CHUNK, C_GMLP), const2),
        pl.BlockSpec((D_MODEL, D_MODEL), const2),
        pl.BlockSpec((1, D_MODEL), const2),
        pl.BlockSpec((N_EXPERTS, D_MODEL), const2),
        pl.BlockSpec((N_EXPERTS, D_MODEL), const2),
        pl.BlockSpec((N_EXPERTS, 1), const2),
    ]
    tile_cnt = lambda s: (tile(s), 0, 0)
    out_specs = [
        pl.BlockSpec((tm, D_MODEL), tile_row),
        pl.BlockSpec((pairs_per_tile * ROW_TILES, LANES),
                     lambda s: (jnp.maximum(s - 1, 0), 0)),
        pl.BlockSpec((TOP_K, tm), tile_lane),
        pl.BlockSpec((TOP_K, tm), tile_lane),
        pl.BlockSpec((1, N_EXPERTS, LANES), tile_cnt),
        pl.BlockSpec((1, N_EXPERTS, LANES), tile_cnt),
        pl.BlockSpec((1, ns, HIST, C_CONV), per_seq),
        pl.BlockSpec((tm, C_GMLP), tile_row),
    ]
    rows = g * r
    out_shape = [
        jax.ShapeDtypeStruct((rows, D_MODEL), F32),
        jax.ShapeDtypeStruct((ntot * pairs_per_tile * ROW_TILES, LANES), U32),
        jax.ShapeDtypeStruct((TOP_K, rows), I32),
        jax.ShapeDtypeStruct((TOP_K, rows), F32),
        jax.ShapeDtypeStruct((ntot, N_EXPERTS, LANES), I32),
        jax.ShapeDtypeStruct((ntot, N_EXPERTS, LANES), I32),
        jax.ShapeDtypeStruct((g, ns, HIST, C_CONV), F32),
        jax.ShapeDtypeStruct((rows, C_GMLP), F32),
    ]
    kern = functools.partial(_mixer_kernel, ns=ns, sl=sl, carry=carry, sp_chunk=sp_chunk,
                             tiles_per_seq=nt, n_sorted=2 * pairs_per_tile)
    return pl.pallas_call(
        kern,
        grid=(ntot + 1,),
        in_specs=in_specs,
        out_specs=out_specs,
        out_shape=out_shape,
        scratch_shapes=[pltpu.VMEM((ns, HIST + sl, C_CONV), F32),
                        pltpu.VMEM((ns, SUBLANES - 1, HIST + sl, C_CONV), F32),
                        pltpu.VMEM((tm, D_MODEL), BF16),
                        pltpu.VMEM((TOP_K, tm), I32),
                        pltpu.VMEM((tm, tm), BF16),
                        pltpu.VMEM((GMLP_HEADS, GMLP_CHUNK, GMLP_CHUNK), BF16)],
        compiler_params=pltpu.CompilerParams(
            dimension_semantics=("arbitrary",),
            vmem_limit_bytes=VMEM_LIMIT_MIXER),
        name="mixer_carry" if carry else "mixer_cache",
    )(x, hist, mem_k, mem_v, *wts)


def _strip_pieces(n, max_rows, fn):
    done = 0
    p = max_rows
    while p >= 1:
        has = (n & p) != 0
        pl.when(has)(functools.partial(fn, done, p))
        done = done + jnp.where(has, p, 0)
        p //= 2


class _Table:
    def __init__(self, ref, offset):
        self.ref, self.offset = ref, offset

    def __getitem__(self, i):
        return self.ref[self.offset + i]


def _pack_tables(tables):
    offsets = tuple(int(o) for o in np.cumsum([0] + [t.shape[0] for t in tables[:-1]]))
    return jnp.concatenate([t.astype(I32) for t in tables]), offsets


def _pow2_at_most(n):
    return 1 << (n.bit_length() - 1)


def _rows(ref, first_row, rows):
    return ref.at[pl.ds(pl.multiple_of(first_row * ROW_TILES, ROW_TILES), rows * ROW_TILES)]


def _moe_kernel(tab_ref, xs_hbm, xs2_hbm, wgu_ref, bgu_ref, wdn_ref, bdn_ref, ys_hbm,
                xbuf, ybuf, wgu16, wdn16, sems, ysems, *, offsets, n_blocks):
    tabs = [_Table(tab_ref, o) for o in offsets]
    b0_ref, nbk_ref, bv_ref, nb_ref = tabs[:4]
    strips_a, strips_b = tabs[4:9], tabs[9:14]
    e = pl.program_id(0)
    n_used = nb_ref[0]
    half = MOE_PAIRS // 2

    def gather_start(b, slot):
        pair0 = b * MOE_PAIRS

        for src_hbm, (slo_ref, shi_ref, ssrc_ref, sdst_ref, slen_ref) in ((xs_hbm, strips_a), (xs2_hbm, strips_b)):
            def strip(s, carry, src_hbm=src_hbm, ssrc_ref=ssrc_ref, sdst_ref=sdst_ref, slen_ref=slen_ref):
                lo = jnp.maximum(sdst_ref[s], pair0)
                hi = jnp.minimum(sdst_ref[s] + slen_ref[s], pair0 + MOE_PAIRS)
                src0 = ssrc_ref[s] + (lo - sdst_ref[s])
                dst0 = lo - pair0

                def piece(first, rows):
                    pltpu.make_async_copy(_rows(src_hbm, src0 + first, rows),
                                          _rows(xbuf.at[slot], dst0 + first, rows), sems.at[slot]).start()
                _strip_pieces(hi - lo, MOE_PAIRS, piece)
                return carry
            lax.fori_loop(slo_ref[b], shi_ref[b], strip, 0)

    def gather_wait(b, slot):
        def piece(first, rows):
            del first
            pltpu.make_async_copy(_rows(xs_hbm, 0, rows), _rows(xbuf.at[slot], 0, rows), sems.at[slot]).wait()
        _strip_pieces(bv_ref[b], MOE_PAIRS, piece)

    def out_copy(b, slot):
        return pltpu.make_async_copy(ybuf.at[slot], _rows(ys_hbm, b * MOE_PAIRS, MOE_PAIRS), ysems.at[slot])

    def expert_mlp(xslot, slot, h):
        xb = pltpu.bitcast(_tiles_to_rows(xbuf.at[xslot], half, h * half), BF16)
        gu = jnp.dot(xb, wgu16[...], preferred_element_type=F32) + bgu_ref[0]
        gate = jnp.minimum(gu[:, :D_FF], SWIGLU_LIMIT)
        up = jnp.clip(gu[:, D_FF:], -SWIGLU_LIMIT, SWIGLU_LIMIT)
        act = (up + 1.0) * (gate * _sigmoid(SWIGLU_ALPHA * gate))
        yb = jnp.dot(act.astype(BF16), wdn16[...], preferred_element_type=F32) + bdn_ref[0]
        _rows_to_tiles(ybuf.at[slot], pltpu.bitcast(yb.astype(BF16), U32), half, h * half)

    @pl.when(e == 0)
    def _():
        xbuf[...] = jnp.zeros_like(xbuf)
        for a in range(GATHER_AHEAD):
            @pl.when(a < n_used)
            def _():
                gather_start(a, a)

    @pl.when(nbk_ref[e] > 0)
    def _():
        def cast_rows(c, carry):
            r = pl.multiple_of(c * LANES, LANES)
            wgu16[pl.ds(r, LANES), :] = wgu_ref[pl.ds(r, LANES), :].astype(BF16)
            wdn16[pl.ds(r, LANES), :] = wdn_ref[pl.ds(r, LANES), :].astype(BF16)
            return carry
        lax.fori_loop(0, D_MODEL // LANES, cast_rows, 0)

        def block(j, carry):
            b = b0_ref[e] + j
            slot = b % 2
            xslot = b % (GATHER_AHEAD + 1)

            @pl.when(b + GATHER_AHEAD < n_used)
            def _():
                gather_start(b + GATHER_AHEAD, (b + GATHER_AHEAD) % (GATHER_AHEAD + 1))

            gather_wait(b, xslot)

            @pl.when(b >= 2)
            def _():
                out_copy(b - 2, slot).wait()

            valid = bv_ref[b]

            @pl.when(valid > half)
            def _():
                expert_mlp(xslot, slot, 0)
                expert_mlp(xslot, slot, 1)

            @pl.when(valid <= half)
            def _():
                expert_mlp(xslot, slot, 0)
                ybuf[slot, pl.ds(half * ROW_TILES, half * ROW_TILES), :] = jnp.zeros(
                    (half * ROW_TILES, LANES), U32)

            out_copy(b, slot).start()
            return carry
        lax.fori_loop(0, nbk_ref[e], block, 0)

    @pl.when(e == pl.num_programs(0) - 1)
    def _():
        for back in (1, 2):
            @pl.when(n_used >= back)
            def _():
                out_copy(n_used - back, (n_used - back) % 2).wait()
        ybuf[0] = jnp.zeros(ybuf.shape[1:], U32)

        def start_fill(b, carry):
            out_copy(b, 0).start()
            return carry

        def wait_fill(b, carry):
            out_copy(b, 0).wait()
            return carry
        lax.fori_loop(n_used, n_blocks, start_fill, 0)
        lax.fori_loop(n_used, n_blocks, wait_fill, 0)


def _moe(tables, xs, xs2, n_blocks, w_gu, b_gu, w_dn, b_dn):
    wsel = lambda e, *_: (e, 0, 0)
    packed, offsets = _pack_tables(tables)
    slots = n_blocks * MOE_PAIRS
    grid_spec = pltpu.PrefetchScalarGridSpec(
        num_scalar_prefetch=1,
        grid=(N_EXPERTS,),
        in_specs=[pl.BlockSpec(memory_space=pl.ANY),
                  pl.BlockSpec(memory_space=pl.ANY),
                  pl.BlockSpec((None, D_MODEL, 2 * D_FF), wsel),
                  pl.BlockSpec((None, 1, 2 * D_FF), wsel),
                  pl.BlockSpec((None, D_FF, D_MODEL), wsel),
                  pl.BlockSpec((None, 1, D_MODEL), wsel)],
        out_specs=pl.BlockSpec(memory_space=pl.ANY),
        scratch_shapes=[pltpu.VMEM((GATHER_AHEAD + 1, MOE_PAIRS * ROW_TILES, LANES), U32),
                        pltpu.VMEM((2, MOE_PAIRS * ROW_TILES, LANES), U32),
                        pltpu.VMEM((D_MODEL, 2 * D_FF), BF16),
                        pltpu.VMEM((D_FF, D_MODEL), BF16),
                        pltpu.SemaphoreType.DMA((GATHER_AHEAD + 1,)),
                        pltpu.SemaphoreType.DMA((2,))],
    )
    return pl.pallas_call(
        functools.partial(_moe_kernel, offsets=offsets, n_blocks=n_blocks),
        grid_spec=grid_spec,
        out_shape=jax.ShapeDtypeStruct((slots * ROW_TILES, LANES), U32),
        compiler_params=pltpu.CompilerParams(
            dimension_semantics=("arbitrary",),
            vmem_limit_bytes=VMEM_LIMIT_MOE),
        name="moe",
    )(packed, xs, xs2, w_gu, b_gu, w_dn, b_dn)


def _combine_kernel(tab_ref, x1_ref, pos_ref, gate_ref, gfin_ref, ys_hbm,
                    out_ref, ybuf, sems, *, tm, pairs_per_tile, offsets):
    csrc_ref, clen_ref, coff_ref, ctot_ref = [_Table(tab_ref, o) for o in offsets]
    i = pl.program_id(0)
    n_steps = pl.num_programs(0)
    n_sorted = 2 * pairs_per_tile
    n_slots = 2 * TILES_PER_STEP

    def gather_start(t, slot):
        def strip(e, carry):
            s = t * N_EXPERTS + e
            src0 = csrc_ref[s]
            dst0 = coff_ref[s]

            def piece(first, rows):
                pltpu.make_async_copy(_rows(ys_hbm, src0 + first, rows),
                                      _rows(ybuf.at[slot], dst0 + first, rows), sems.at[slot]).start()
            _strip_pieces(clen_ref[s], tm // 2, piece)
            return carry
        lax.fori_loop(0, N_EXPERTS, strip, 0)

    @pl.when(i == 0)
    def _():
        ybuf[...] = jnp.zeros_like(ybuf)
        for h in range(TILES_PER_STEP):
            gather_start(h, h)

    @pl.when(i + 1 < n_steps)
    def _():
        for h in range(TILES_PER_STEP):
            t = (i + 1) * TILES_PER_STEP + h
            gather_start(t, t % n_slots)

    slots = [(i * TILES_PER_STEP + h) % n_slots for h in range(TILES_PER_STEP)]
    for h in range(TILES_PER_STEP):
        def wait_piece(first, rows, slot=slots[h]):
            del first
            pltpu.make_async_copy(_rows(ys_hbm, 0, rows), _rows(ybuf.at[slot], 0, rows), sems.at[slot]).wait()
        _strip_pieces(ctot_ref[i * TILES_PER_STEP + h], _pow2_at_most(pairs_per_tile), wait_piece)

    jt = lax.broadcasted_iota(I32, (n_sorted, tm), 0).astype(jnp.int16)
    for h in range(TILES_PER_STEP):
        tok = slice(h * tm, (h + 1) * tm)
        y_sorted = pltpu.bitcast(_tiles_to_rows(ybuf.at[slots[h]], pairs_per_tile), BF16)
        unsort_t = jnp.zeros((n_sorted, tm), BF16)
        for k in range(TOP_K):
            unsort_t = jnp.where(jt == pos_ref[k:k + 1, tok].astype(jnp.int16),
                                 gate_ref[k:k + 1, tok].astype(BF16), unsort_t)
        acc = x1_ref[tok, :] + lax.dot_general(unsort_t, y_sorted, (((0,), (0,)), ((), ())),
                                               preferred_element_type=F32)
        out_ref[tok, :] = _rmsnorm(acc, gfin_ref[...])


def _combine(tables, x1, pos, gates, g_final, ys, tm, pairs_per_tile):
    t = x1.shape[0]
    nt = t // tm
    assert nt % TILES_PER_STEP == 0
    packed, offsets = _pack_tables(tables)
    grid_spec = pltpu.PrefetchScalarGridSpec(
        num_scalar_prefetch=1,
        grid=(nt // TILES_PER_STEP,),
        in_specs=[pl.BlockSpec((TILES_PER_STEP * tm, D_MODEL), lambda i, *_: (i, 0)),
                  pl.BlockSpec((TOP_K, TILES_PER_STEP * tm), lambda i, *_: (0, i)),
                  pl.BlockSpec((TOP_K, TILES_PER_STEP * tm), lambda i, *_: (0, i)),
                  pl.BlockSpec((1, D_MODEL), lambda i, *_: (0, 0)),
                  pl.BlockSpec(memory_space=pl.ANY)],
        out_specs=pl.BlockSpec((TILES_PER_STEP * tm, D_MODEL), lambda i, *_: (i, 0)),
        scratch_shapes=[pltpu.VMEM((2 * TILES_PER_STEP, pairs_per_tile * ROW_TILES, LANES), U32),
                        pltpu.SemaphoreType.DMA((2 * TILES_PER_STEP,))],
    )
    return pl.pallas_call(
        functools.partial(_combine_kernel, tm=tm, pairs_per_tile=pairs_per_tile, offsets=offsets),
        grid_spec=grid_spec,
        out_shape=jax.ShapeDtypeStruct((t, D_MODEL), F32),
        compiler_params=pltpu.CompilerParams(
            dimension_semantics=("arbitrary",),
            vmem_limit_bytes=VMEM_LIMIT_MIXER),
        name="combine",
    )(packed, x1, pos, gates, g_final, ys)


def _split_bf16(w):
    hi = w.astype(BF16)
    lo = (w - hi.astype(F32)).astype(BF16)
    return hi, lo


def kernel(x_prompt, x_sample, cache_conv, cache_mem_k, cache_mem_v, mem_prompt, g_mix, w_in, w_dw, b_dw, ln_conv_g, ln_conv_b, ln_v_g, ln_v_b, w_spatial, b_spatial, g_mem, w_mem_k, w_mem_v, w_out, g_ffn, w_router, b_router, w_gate_up, b_gate_up, w_down, b_down, g_final):
    depth = g_mix.shape[0]
    assert depth == 1
    l = 0
    bp, seq, _ = x_prompt.shape
    bs, dseq, _ = x_sample.shape
    assert seq % PROMPT_TILE == 0 and bs % SAMPLE_SEQS_PER_TILE == 0
    assert GMLP_CHUNK % dseq == 0 and (SAMPLE_SEQS_PER_TILE * dseq) % GMLP_CHUNK == 0

    row = lambda a: a.reshape(1, -1)
    wr_hi, wr_lo = _split_bf16(w_router[l].T)
    w_dw_pad = jnp.pad(w_dw[l], ((0, HIST - CONV_W), (0, 0)))
    bias_rows = lambda b: jnp.repeat(b.T, GMLP_HD, axis=1)
    common = dict(
        gmix=row(g_mix[l]), win=w_in[l].astype(BF16), wdw=w_dw_pad, bdw=row(b_dw[l]),
        lcg=row(ln_conv_g[l]), lcb=row(ln_conv_b[l]), lvg=row(ln_v_g[l]), lvb=row(ln_v_b[l]),
        wout=w_out[l].astype(BF16), gffn=row(g_ffn[l]), wrh=wr_hi, wrl=wr_lo,
        br=b_router[l].reshape(N_EXPERTS, 1))

    def weights(wsp, bsp):
        c = common
        return (c["gmix"], c["win"], c["wdw"], c["bdw"], c["lcg"], c["lcb"], c["lvg"], c["lvb"],
                wsp, bsp, c["wout"], c["gffn"], c["wrh"], c["wrl"], c["br"])

    reps = GMLP_CHUNK // dseq
    wts_p = weights(w_spatial[l], bias_rows(b_spatial[l]))
    wts_s = weights(jnp.tile(w_spatial[l][:, :dseq, :dseq], (1, reps, reps)),
                    bias_rows(jnp.tile(b_spatial[l][:, :dseq], (1, reps))))

    w_kv = jnp.concatenate([w_mem_k[l], w_mem_v[l]], axis=1).astype(BF16)
    kv_p = _memkv(mem_prompt, row(g_mem[l]), w_kv)
    mk_p = kv_p[:, :, :C_XA]
    mv_p = kv_p[:, :, C_XA:]
    zero_hist = jnp.zeros((bp, 1, HIST, C_CONV), F32)
    tp, ts = bp * seq, bs * dseq
    tm_s = SAMPLE_SEQS_PER_TILE * dseq
    ntp, nts = tp // PROMPT_TILE, ts // tm_s
    ppt_p = (TOP_K * PROMPT_TILE + N_EXPERTS + 1) // 2
    ppt_s = (TOP_K * tm_s + N_EXPERTS + 1) // 2
    (x1_p, xs_p, pos_p, gate_p, cnt_p, off_p, hist_p, _) = _mixer(
        x_prompt, zero_hist, mk_p[:, None], mv_p[:, None], wts_p,
        ns=1, sl=PROMPT_TILE, carry=True, sp_chunk=GMLP_CHUNK, pairs_per_tile=ppt_p)

    gs = bs // SAMPLE_SEQS_PER_TILE
    hist_s_in = jnp.pad(cache_conv[l], ((0, 0), (HIST_OFF, 0), (0, 0))).reshape(
        gs, SAMPLE_SEQS_PER_TILE, HIST, C_CONV)
    mk_s = cache_mem_k[l].reshape(gs, SAMPLE_SEQS_PER_TILE, N_MEM, C_XA)
    mv_s = cache_mem_v[l].reshape(gs, SAMPLE_SEQS_PER_TILE, N_MEM, C_XA)
    (x1_s, xs_s, pos_s, gate_s, cnt_s, off_s, hist_s, v_s) = _mixer(
        x_sample.reshape(gs, tm_s, D_MODEL), hist_s_in, mk_s, mv_s, wts_s,
        ns=SAMPLE_SEQS_PER_TILE, sl=dseq, carry=False, sp_chunk=dseq, pairs_per_tile=ppt_s)

    max_pairs = ((tp + ts) * TOP_K + (ntp + nts) * N_EXPERTS) // 2
    n_blocks = -(-max_pairs // MOE_PAIRS) + N_EXPERTS
    cnt = jnp.concatenate([cnt_p[:, :, 0], cnt_s[:, :, 0]], axis=0)
    off = jnp.concatenate([off_p[:, :, 0], off_s[:, :, 0]], axis=0)
    tile_pair0 = np.concatenate([np.arange(ntp) * ppt_p, np.arange(nts) * ppt_s]).astype(np.int32)
    counts = jnp.sum(cnt, axis=0)
    tile_base = jnp.cumsum(cnt, axis=0) - cnt
    padded = (counts + MOE_PAIRS - 1) // MOE_PAIRS * MOE_PAIRS
    pad_end = jnp.cumsum(padded)
    pad_start = pad_end - padded
    strip_dst = (pad_start[None, :] + tile_base).astype(I32)
    strip_src = (tile_pair0[:, None] + off).astype(I32)
    n_used = (pad_end[-1] // MOE_PAIRS).astype(I32)
    blk_pair0 = jnp.minimum(jnp.arange(n_blocks, dtype=I32), n_used - 1) * MOE_PAIRS
    block_expert = jnp.minimum(
        jnp.sum((pad_end[None, :] <= blk_pair0[:, None]).astype(I32), axis=1), N_EXPERTS - 1)
    of_block = block_expert[:, None] == jnp.arange(N_EXPERTS, dtype=I32)[None, :]
    last_pair = jnp.sum(jnp.where(of_block, (pad_start + counts)[None, :], 0), axis=1)
    block_valid = jnp.clip(last_pair - blk_pair0, 0, MOE_PAIRS).astype(I32)
    def strip_group(t0, t1):
        sdst = strip_dst[t0:t1].T.reshape(-1)
        ssrc = strip_src[t0:t1].T.reshape(-1)
        slen = cnt[t0:t1].T.reshape(-1).astype(I32)
        s_lo = jnp.sum(((sdst + slen)[None, :] <= blk_pair0[:, None]).astype(I32), axis=1)
        s_hi = jnp.sum((sdst[None, :] < (blk_pair0 + MOE_PAIRS)[:, None]).astype(I32), axis=1)
        return (s_lo, s_hi, ssrc, sdst, slen)

    moe_tables = ((pad_start // MOE_PAIRS).astype(I32), (padded // MOE_PAIRS).astype(I32),
                  block_valid, n_used.reshape(1),
                  *strip_group(0, ntp), *strip_group(ntp, ntp + nts))

    ys = _moe(moe_tables, xs_p, xs_s, n_blocks,
              w_gate_up[l], b_gate_up[l][:, None, :], w_down[l], b_down[l][:, None, :])
    gfin = row(g_final)

    def combine_tables(t0, t1):
        return (strip_dst[t0:t1].reshape(-1), cnt[t0:t1].reshape(-1).astype(I32),
                off[t0:t1].reshape(-1).astype(I32), jnp.sum(cnt[t0:t1], axis=1).astype(I32))

    y_p = _combine(combine_tables(0, ntp), x1_p, pos_p, gate_p, gfin, ys, PROMPT_TILE, ppt_p)
    y_s = _combine(combine_tables(ntp, ntp + nts), x1_s, pos_s, gate_s, gfin, ys, tm_s, ppt_s)

    return (y_p.reshape(bp, seq, D_MODEL),
            y_s.reshape(bs, dseq, D_MODEL),
            hist_p[:, 0, HIST_OFF:, :][None],
            mk_p.reshape(bp, N_MEM, XA_HEADS, XA_HD)[None],
            mv_p.reshape(bp, N_MEM, XA_HEADS, XA_HD)[None],
            hist_s.reshape(bs, HIST, C_CONV)[:, HIST_OFF:, :][None],
            v_s.reshape(bs, dseq, C_GMLP)[None])
```

```python
import functools

import numpy as np
import jax
import jax.numpy as jnp
from jax import lax
from jax.experimental import pallas as pl
from jax.experimental.pallas import tpu as pltpu

F32 = jnp.float32
BF16 = jnp.bfloat16
I32 = jnp.int32
U32 = jnp.uint32

D_MODEL = 1024
C_CONV = 384
CONV_W = 31
C_GMLP = 384
GMLP_HEADS = 4
GMLP_HD = 96
GMLP_CHUNK = 128
XA_HEADS = 4
XA_HD = 64
C_XA = 256
N_MEM = 256
N_EXPERTS = 32
TOP_K = 4
D_FF = 1024
SWIGLU_LIMIT = 7.0
SWIGLU_ALPHA = 1.702
EPS = 1e-5
IN_COLS = 2 * C_CONV + 2 * C_GMLP + C_XA

SUBLANES = 8
LANES = 128
ROW_TILES = D_MODEL // LANES
HIST = 32
HIST_OFF = HIST - (CONV_W - 1)

PROMPT_TILE = 512
SAMPLE_SEQS_PER_TILE = 8
MOE_BLOCK = 512
MOE_PAIRS = MOE_BLOCK // 2
GATHER_AHEAD = 2
TILES_PER_STEP = 2
VMEM_LIMIT_MIXER = 48 * 1024 * 1024
VMEM_LIMIT_MOE = 52 * 1024 * 1024


def _rmsnorm(x, g):
    return x * lax.rsqrt(jnp.mean(x * x, axis=-1, keepdims=True) + EPS) * g


def _layernorm(x, g, b):
    mu = jnp.mean(x, axis=-1, keepdims=True)
    xc = x - mu
    var = jnp.mean(xc * xc, axis=-1, keepdims=True)
    return xc * lax.rsqrt(var + EPS) * g + b


def _gelu(x):
    return 0.5 * x * (1.0 + lax.erf(x * np.float32(1.0 / np.sqrt(2.0))))


def _sigmoid(x):
    return 1.0 / (1.0 + jnp.exp(-x))


def _rows_to_tiles(dst_ref, val, rows, row0=0):
    for j in range(ROW_TILES):
        dst_ref[pl.ds(row0 * ROW_TILES + j, rows, stride=ROW_TILES), :] = val[:, j * LANES:(j + 1) * LANES]


def _tiles_to_rows(src_ref, rows, row0=0):
    return jnp.concatenate(
        [src_ref[pl.ds(row0 * ROW_TILES + j, rows, stride=ROW_TILES), :] for j in range(ROW_TILES)],
        axis=-1)


def _memkv_kernel(mem_ref, g_ref, w_ref, o_ref):
    mn = _rmsnorm(mem_ref[0], g_ref[...])
    o_ref[0] = jnp.dot(mn.astype(BF16), w_ref[...], preferred_element_type=F32)


def _memkv(mem, g_mem, w_kv):
    b = mem.shape[0]
    return pl.pallas_call(
        _memkv_kernel,
        grid=(b,),
        in_specs=[pl.BlockSpec((1, N_MEM, D_MODEL), lambda i: (i, 0, 0)),
                  pl.BlockSpec((1, D_MODEL), lambda i: (0, 0)),
                  pl.BlockSpec((D_MODEL, 2 * C_XA), lambda i: (0, 0))],
        out_specs=pl.BlockSpec((1, N_MEM, 2 * C_XA), lambda i: (i, 0, 0)),
        out_shape=jax.ShapeDtypeStruct((b, N_MEM, 2 * C_XA), F32),
        name="memkv",
    )(mem, g_mem, w_kv)


def _mixer_kernel(x_ref, hist_ref, mk_ref, mv_ref, gmix_ref, win_ref, wdw_ref, bdw_ref,
                  lcg_ref, lcb_ref, lvg_ref, lvb_ref, wsp_ref, bsp_ref, wout_ref, gffn_ref,
                  wrh_ref, wrl_ref, br_ref, *rest, ns, sl, carry, sp_chunk, tiles_per_seq, n_sorted):
    (x1_ref, xst_ref, pos_ref, gate_ref, cnt_ref, off_ref, histout_ref, v_ref,
     ext_ref, shift_ref, h2_keep, pos_keep, before_ref, wsp16_ref) = rest
    tm = ns * sl
    step = pl.program_id(0)
    n_tiles = pl.num_programs(0) - 1

    def sort_previous_tile():
        jj = lax.broadcasted_iota(I32, (n_sorted, tm), 0).astype(jnp.int16)
        perm = jnp.zeros((n_sorted, tm), BF16)
        for k in range(TOP_K):
            perm = jnp.where(jj == pos_keep[k:k + 1, :].astype(jnp.int16), jnp.ones((), BF16), perm)
        sorted_rows = jnp.dot(perm, h2_keep[...], preferred_element_type=F32)
        pairs = pltpu.bitcast(sorted_rows.astype(BF16), U32)
        _rows_to_tiles(xst_ref, pairs, n_sorted // 2)

    @pl.when(step == 0)
    def _():
        h2_keep[...] = jnp.zeros_like(h2_keep)
        pos_keep[...] = jnp.full(pos_keep.shape, -1, I32)
        tr = lax.broadcasted_iota(I32, (tm, tm), 0)
        tc = lax.broadcasted_iota(I32, (tm, tm), 1)
        before_ref[...] = jnp.where(tr < tc, 1.0, 0.0).astype(BF16)
        rr = lax.broadcasted_iota(I32, (GMLP_CHUNK, GMLP_CHUNK), 0)
        cc = lax.broadcasted_iota(I32, (GMLP_CHUNK, GMLP_CHUNK), 1)
        sp_mask = (cc <= rr) & ((rr // sp_chunk) == (cc // sp_chunk))
        for hh in range(GMLP_HEADS):
            wsp16_ref[hh] = jnp.where(sp_mask, wsp_ref[hh], 0.0).astype(BF16)

    @pl.when(step < n_tiles)
    def _():
        sort_previous_tile()
        _mixer_tile(x_ref, hist_ref, mk_ref, mv_ref, gmix_ref, win_ref, wdw_ref, bdw_ref,
                    lcg_ref, lcb_ref, lvg_ref, lvb_ref, wsp16_ref, bsp_ref, wout_ref, gffn_ref,
                    wrh_ref, wrl_ref, br_ref, x1_ref, pos_ref, gate_ref, cnt_ref, off_ref,
                    histout_ref, v_ref, ext_ref, shift_ref, h2_keep, pos_keep, before_ref,
                    first_of_seq=(step % tiles_per_seq) == 0,
                    ns=ns, sl=sl, carry=carry)

    @pl.when(step == n_tiles)
    def _():
        sort_previous_tile()


def _mixer_tile(x_ref, hist_ref, mk_ref, mv_ref, gmix_ref, win_ref, wdw_ref, bdw_ref,
                lcg_ref, lcb_ref, lvg_ref, lvb_ref, wsp16_ref, bsp_ref, wout_ref, gffn_ref,
                wrh_ref, wrl_ref, br_ref, x1_ref, pos_ref, gate_ref, cnt_ref, off_ref,
                histout_ref, v_ref, ext_ref, shift_ref, h2_keep, pos_keep, before_ref,
                *, first_of_seq, ns, sl, carry):
    tm = ns * sl
    x = x_ref[0]
    h = _rmsnorm(x, gmix_ref[...])
    z = jnp.dot(h.astype(BF16), win_ref[...], preferred_element_type=F32)
    z_a = z[:, 0:C_CONV]
    z_g = z[:, C_CONV:2 * C_CONV]
    z_u = z[:, 2 * C_CONV:2 * C_CONV + C_GMLP]
    z_v = z[:, 2 * C_CONV + C_GMLP:2 * C_CONV + 2 * C_GMLP]
    z_q = z[:, 2 * C_CONV + 2 * C_GMLP:IN_COLS]

    glu = z_a * _sigmoid(z_g)
    if carry:
        ext_ref[:, 0:HIST, :] = jnp.where(first_of_seq, hist_ref[0], ext_ref[:, 0:HIST, :])
    else:
        ext_ref[:, 0:HIST, :] = hist_ref[0]
    conv_parts = []
    for s in range(ns):
        ext_s = ext_ref.at[s]
        ext_s[HIST:HIST + sl, :] = glu[s * sl:(s + 1) * sl]
        n_shift = HIST + sl - SUBLANES
        for r in range(1, SUBLANES):
            shift_ref[s, r - 1, 0:n_shift, :] = ext_s[pl.ds(r, n_shift), :]
        rc = min(sl, 64)
        for r0 in range(0, sl, rc):
            acc = jnp.broadcast_to(bdw_ref[...], (rc, C_CONV))
            for j in range(CONV_W):
                a, r = divmod(j + HIST_OFF, SUBLANES)
                src = ext_s if r == 0 else shift_ref.at[s, r - 1]
                acc = acc + wdw_ref[j:j + 1, :] * src[pl.ds(r0 + a * SUBLANES, rc), :]
            conv_parts.append(acc)
        new_hist = ext_s[sl:sl + HIST, :]
        histout_ref[0, s] = new_hist
        if carry:
            ext_s[0:HIST, :] = new_hist
    y = jnp.concatenate(conv_parts, axis=0) if len(conv_parts) > 1 else conv_parts[0]
    y = _layernorm(y, lcg_ref[...], lcb_ref[...])
    c_out = y * _sigmoid(y)

    u = _gelu(z_u)
    v = _layernorm(_gelu(z_v), lvg_ref[...], lvb_ref[...])
    v_ref[...] = v
    vb = v.astype(BF16)
    col = lax.broadcasted_iota(I32, (GMLP_CHUNK, C_GMLP), 1)
    w_heads = [wsp16_ref[hh] for hh in range(GMLP_HEADS)]
    g_parts = []
    for c in range(tm // GMLP_CHUNK):
        vc = vb[c * GMLP_CHUNK:(c + 1) * GMLP_CHUNK]
        sg = bsp_ref[...]
        for hh in range(GMLP_HEADS):
            head_cols = (col >= hh * GMLP_HD) & (col < (hh + 1) * GMLP_HD)
            vh = jnp.where(head_cols, vc, jnp.zeros_like(vc))
            sg = sg + jnp.dot(w_heads[hh], vh, preferred_element_type=F32)
        g_parts.append(u[c * GMLP_CHUNK:(c + 1) * GMLP_CHUNK] * sg)
    g_out = jnp.concatenate(g_parts, axis=0) if len(g_parts) > 1 else g_parts[0]

    qs = z_q * np.float32(XA_HD ** -0.5)
    qcol = lax.broadcasted_iota(I32, (sl, C_XA), 1)
    a_parts = []
    for s in range(ns):
        q_s = qs[s * sl:(s + 1) * sl]
        kb = mk_ref[0, s].astype(BF16)
        vvb = mv_ref[0, s].astype(BF16)
        hmasks = [(qcol >= hh * XA_HD) & (qcol < (hh + 1) * XA_HD) for hh in range(XA_HEADS)]
        stack = XA_HEADS if sl * XA_HEADS <= N_MEM else 1
        a_s = jnp.zeros((sl, C_XA), F32)
        for h0 in range(0, XA_HEADS, stack):
            heads = range(h0, h0 + stack)
            qh = jnp.concatenate([jnp.where(hmasks[hh], q_s, 0.0) for hh in heads], axis=0).astype(BF16)
            sc = lax.dot_general(qh, kb, (((1,), (1,)), ((), ())), preferred_element_type=F32)
            p = jnp.exp(sc - jnp.max(sc, axis=-1, keepdims=True))
            den = jnp.sum(p, axis=-1, keepdims=True)
            oh = jnp.dot(p.astype(BF16), vvb, preferred_element_type=F32) / den
            for n, hh in enumerate(heads):
                a_s = a_s + jnp.where(hmasks[hh], oh[n * sl:(n + 1) * sl], 0.0)
        a_parts.append(a_s)
    a_out = jnp.concatenate(a_parts, axis=0) if len(a_parts) > 1 else a_parts[0]

    mix = jnp.concatenate([c_out, g_out, a_out], axis=-1).astype(BF16)
    x1 = x + jnp.dot(mix, wout_ref[...], preferred_element_type=F32)
    x1_ref[...] = x1

    h2 = _rmsnorm(x1, gffn_ref[...])
    h2_hi = h2.astype(BF16)
    h2_lo = (h2 - h2_hi.astype(F32)).astype(BF16)
    nt_dims = (((1,), (1,)), ((), ()))
    lg = (lax.dot_general(wrh_ref[...], h2_hi, nt_dims, preferred_element_type=F32)
          + lax.dot_general(wrl_ref[...], h2_hi, nt_dims, preferred_element_type=F32)
          + lax.dot_general(wrh_ref[...], h2_lo, nt_dims, preferred_element_type=F32)
          + br_ref[...])
    eio = lax.broadcasted_iota(I32, (N_EXPERTS, tm), 0)
    work = lg
    vals, idxs = [], []
    for _ in range(TOP_K):
        m = jnp.max(work, axis=0, keepdims=True)
        idx = jnp.min(jnp.where(work == m, eio, N_EXPERTS), axis=0, keepdims=True)
        vals.append(m)
        idxs.append(idx)
        work = jnp.where(eio == idx, -jnp.inf, work)
    exps = [jnp.exp(vk - vals[0]) for vk in vals]
    den = exps[0] + exps[1] + exps[2] + exps[3]
    gate_ref[...] = jnp.concatenate([ek / den for ek in exps], axis=0)

    sel = jnp.zeros((N_EXPERTS, tm), F32)
    for idx in idxs:
        sel = sel + jnp.where(eio == idx, 1.0, 0.0)
    selb = sel.astype(BF16)
    ranks = jnp.dot(selb, before_ref[...], preferred_element_type=F32)
    er = lax.broadcasted_iota(I32, (N_EXPERTS, N_EXPERTS), 0)
    ec = lax.broadcasted_iota(I32, (N_EXPERTS, N_EXPERTS), 1)
    lower = jnp.where(ec < er, 1.0, 0.0).astype(BF16)
    cnt = jnp.sum(sel, axis=1, keepdims=True)
    half_len = jnp.floor((cnt + 1.0) * 0.5)
    off_pairs = jnp.dot(lower, jnp.broadcast_to(half_len, (N_EXPERTS, LANES)).astype(BF16),
                        preferred_element_type=F32)[:, 0:1]
    cnt_ref[0] = jnp.broadcast_to(half_len, (N_EXPERTS, LANES)).astype(I32)
    off_ref[0] = jnp.broadcast_to(off_pairs, (N_EXPERTS, LANES)).astype(I32)
    slot_of = 2.0 * off_pairs + ranks
    pos = [jnp.sum(jnp.where(eio == idx, slot_of, 0.0), axis=0, keepdims=True).astype(I32) for idx in idxs]
    pos_all = jnp.concatenate(pos, axis=0)
    pos_ref[...] = pos_all
    pos_keep[...] = pos_all
    h2_keep[...] = h2_hi


def _mixer(x, hist, mem_k, mem_v, wts, *, ns, sl, carry, sp_chunk, pairs_per_tile):
    g, r, _ = x.shape
    tm = ns * sl
    nt = r // tm
    ntot = g * nt
    assert 2 * pairs_per_tile >= TOP_K * tm + N_EXPERTS
    tile = lambda s: jnp.minimum(s, ntot - 1)
    const2 = lambda s: (0, 0)
    const3 = lambda s: (0, 0, 0)
    tile_row = lambda s: (tile(s), 0)
    tile_lane = lambda s: (0, tile(s))
    per_seq = lambda s: (tile(s) // nt, 0, 0, 0)
    in_specs = [
        pl.BlockSpec((1, tm, D_MODEL), lambda s: (tile(s) // nt, tile(s) % nt, 0)),
        pl.BlockSpec((1, ns, HIST, C_CONV), per_seq),
        pl.BlockSpec((1, ns, N_MEM, C_XA), per_seq),
        pl.BlockSpec((1, ns, N_MEM, C_XA), per_seq),
        pl.BlockSpec((1, D_MODEL), const2),
        pl.BlockSpec((D_MODEL, IN_COLS), const2),
        pl.BlockSpec((HIST, C_CONV), const2),
        pl.BlockSpec((1, C_CONV), const2),
        pl.BlockSpec((1, C_CONV), const2),
        pl.BlockSpec((1, C_CONV), const2),
        pl.BlockSpec((1, C_GMLP), const2),
        pl.BlockSpec((1, C_GMLP), const2),
        pl.BlockSpec((GMLP_HEADS, GMLP_CHUNK, GMLP_CHUNK), const3),
        pl.BlockSpec((GMLP_CHUNK, C_GMLP), const2),
        pl.BlockSpec((D_MODEL, D_MODEL), const2),
        pl.BlockSpec((1, D_MODEL), const2),
        pl.BlockSpec((N_EXPERTS, D_MODEL), const2),
        pl.BlockSpec((N_EXPERTS, D_MODEL), const2),
        pl.BlockSpec((N_EXPERTS, 1), const2),
    ]
    tile_cnt = lambda s: (tile(s), 0, 0)
    out_specs = [
        pl.BlockSpec((tm, D_MODEL), tile_row),
        pl.BlockSpec((pairs_per_tile * ROW_TILES, LANES),
                     lambda s: (jnp.maximum(s - 1, 0), 0)),
        pl.BlockSpec((TOP_K, tm), tile_lane),
        pl.BlockSpec((TOP_K, tm), tile_lane),
        pl.BlockSpec((1, N_EXPERTS, LANES), tile_cnt),
        pl.BlockSpec((1, N_EXPERTS, LANES), tile_cnt),
        pl.BlockSpec((1, ns, HIST, C_CONV), per_seq),
        pl.BlockSpec((tm, C_GMLP), tile_row),
    ]
    rows = g * r
    out_shape = [
        jax.ShapeDtypeStruct((rows, D_MODEL), F32),
        jax.ShapeDtypeStruct((ntot * pairs_per_tile * ROW_TILES, LANES), U32),
        jax.ShapeDtypeStruct((TOP_K, rows), I32),
        jax.ShapeDtypeStruct((TOP_K, rows), F32),
        jax.ShapeDtypeStruct((ntot, N_EXPERTS, LANES), I32),
        jax.ShapeDtypeStruct((ntot, N_EXPERTS, LANES), I32),
        jax.ShapeDtypeStruct((g, ns, HIST, C_CONV), F32),
        jax.ShapeDtypeStruct((rows, C_GMLP), F32),
    ]
    kern = functools.partial(_mixer_kernel, ns=ns, sl=sl, carry=carry, sp_chunk=sp_chunk,
                             tiles_per_seq=nt, n_sorted=2 * pairs_per_tile)
    return pl.pallas_call(
        kern,
        grid=(ntot + 1,),
        in_specs=in_specs,
        out_specs=out_specs,
        out_shape=out_shape,
        scratch_shapes=[pltpu.VMEM((ns, HIST + sl, C_CONV), F32),
                        pltpu.VMEM((ns, SUBLANES - 1, HIST + sl, C_CONV), F32),
                        pltpu.VMEM((tm, D_MODEL), BF16),
                        pltpu.VMEM((TOP_K, tm), I32),
                        pltpu.VMEM((tm, tm), BF16),
                        pltpu.VMEM((GMLP_HEADS, GMLP_CHUNK, GMLP_CHUNK), BF16)],
        compiler_params=pltpu.CompilerParams(
            dimension_semantics=("arbitrary",),
            vmem_limit_bytes=VMEM_LIMIT_MIXER),
        name="mixer_carry" if carry else "mixer_cache",
    )(x, hist, mem_k, mem_v, *wts)


def _strip_pieces(n, max_rows, fn):
    done = 0
    p = max_rows
    while p >= 1:
        has = (n & p) != 0
        pl.when(has)(functools.partial(fn, done, p))
        done = done + jnp.where(has, p, 0)
        p //= 2


class _Table:
    def __init__(self, ref, offset):
        self.ref, self.offset = ref, offset

    def __getitem__(self, i):
        return self.ref[self.offset + i]


def _pack_tables(tables):
    offsets = tuple(int(o) for o in np.cumsum([0] + [t.shape[0] for t in tables[:-1]]))
    return jnp.concatenate([t.astype(I32) for t in tables]), offsets


def _pow2_at_most(n):
    return 1 << (n.bit_length() - 1)


def _rows(ref, first_row, rows):
    return ref.at[pl.ds(pl.multiple_of(first_row * ROW_TILES, ROW_TILES), rows * ROW_TILES)]


def _moe_kernel(tab_ref, xs_hbm, xs2_hbm, wgu_ref, bgu_ref, wdn_ref, bdn_ref, ys_hbm,
                xbuf, ybuf, zbuf, wgu16, wdn16, sems, ysems, zsem, *, offsets, n_blocks):
    tabs = [_Table(tab_ref, o) for o in offsets]
    b0_ref, nbk_ref, bv_ref, nb_ref = tabs[:4]
    strips_a, strips_b = tabs[4:9], tabs[9:14]
    e = pl.program_id(0)
    n_used = nb_ref[0]
    half = MOE_PAIRS // 2

    def gather_start(b, slot):
        pair0 = b * MOE_PAIRS

        for src_hbm, (slo_ref, shi_ref, ssrc_ref, sdst_ref, slen_ref) in ((xs_hbm, strips_a), (xs2_hbm, strips_b)):
            def strip(s, carry, src_hbm=src_hbm, ssrc_ref=ssrc_ref, sdst_ref=sdst_ref, slen_ref=slen_ref):
                lo = jnp.maximum(sdst_ref[s], pair0)
                hi = jnp.minimum(sdst_ref[s] + slen_ref[s], pair0 + MOE_PAIRS)
                src0 = ssrc_ref[s] + (lo - sdst_ref[s])
                dst0 = lo - pair0

                def piece(first, rows):
                    pltpu.make_async_copy(_rows(src_hbm, src0 + first, rows),
                                          _rows(xbuf.at[slot], dst0 + first, rows), sems.at[slot]).start()
                _strip_pieces(hi - lo, MOE_PAIRS, piece)
                return carry
            lax.fori_loop(slo_ref[b], shi_ref[b], strip, 0)

    def gather_wait(b, slot):
        def piece(first, rows):
            del first
            pltpu.make_async_copy(_rows(xs_hbm, 0, rows), _rows(xbuf.at[slot], 0, rows), sems.at[slot]).wait()
        _strip_pieces(bv_ref[b], MOE_PAIRS, piece)

    def out_copy(b, slot):
        return pltpu.make_async_copy(ybuf.at[slot], _rows(ys_hbm, b * MOE_PAIRS, MOE_PAIRS), ysems.at[slot])

    def expert_mlp(xslot, slot, h):
        xb = pltpu.bitcast(_tiles_to_rows(xbuf.at[xslot], half, h * half), BF16)
        gu = jnp.dot(xb, wgu16[...], preferred_element_type=F32) + bgu_ref[0]
        gate = jnp.minimum(gu[:, :D_FF], SWIGLU_LIMIT)
        up = jnp.clip(gu[:, D_FF:], -SWIGLU_LIMIT, SWIGLU_LIMIT)
        act = (up + 1.0) * (gate * _sigmoid(SWIGLU_ALPHA * gate))
        yb = jnp.dot(act.astype(BF16), wdn16[...], preferred_element_type=F32) + bdn_ref[0]
        _rows_to_tiles(ybuf.at[slot], pltpu.bitcast(yb.astype(BF16), U32), half, h * half)

    def fill_copy(b):
        return pltpu.make_async_copy(zbuf, _rows(ys_hbm, b * MOE_PAIRS, MOE_PAIRS), zsem)

    @pl.when(e == 0)
    def _():
        zbuf[...] = jnp.zeros_like(zbuf)

        def start_fill(b, carry):
            fill_copy(b).start()
            return carry
        lax.fori_loop(n_used, n_blocks, start_fill, 0)

        xbuf[...] = jnp.zeros_like(xbuf)
        for a in range(GATHER_AHEAD):
            @pl.when(a < n_used)
            def _():
                gather_start(a, a)

    @pl.when(nbk_ref[e] > 0)
    def _():
        def cast_rows(c, carry):
            r = pl.multiple_of(c * LANES, LANES)
            wgu16[pl.ds(r, LANES), :] = wgu_ref[pl.ds(r, LANES), :].astype(BF16)
            wdn16[pl.ds(r, LANES), :] = wdn_ref[pl.ds(r, LANES), :].astype(BF16)
            return carry
        lax.fori_loop(0, D_MODEL // LANES, cast_rows, 0)

        def block(j, carry):
            b = b0_ref[e] + j
            slot = b % 2
            xslot = b % (GATHER_AHEAD + 1)

            @pl.when(b + GATHER_AHEAD < n_used)
            def _():
                gather_start(b + GATHER_AHEAD, (b + GATHER_AHEAD) % (GATHER_AHEAD + 1))

            gather_wait(b, xslot)

            @pl.when(b >= 2)
            def _():
                out_copy(b - 2, slot).wait()

            valid = bv_ref[b]

            @pl.when(valid > half)
            def _():
                expert_mlp(xslot, slot, 0)
                expert_mlp(xslot, slot, 1)

            @pl.when(valid <= half)
            def _():
                expert_mlp(xslot, slot, 0)
                ybuf[slot, pl.ds(half * ROW_TILES, half * ROW_TILES), :] = jnp.zeros(
                    (half * ROW_TILES, LANES), U32)

            out_copy(b, slot).start()
            return carry
        lax.fori_loop(0, nbk_ref[e], block, 0)

    @pl.when(e == pl.num_programs(0) - 1)
    def _():
        for back in (1, 2):
            @pl.when(n_used >= back)
            def _():
                out_copy(n_used - back, (n_used - back) % 2).wait()

        def wait_fill(b, carry):
            fill_copy(b).wait()
            return carry
        lax.fori_loop(n_used, n_blocks, wait_fill, 0)


def _moe(tables, xs, xs2, n_blocks, w_gu, b_gu, w_dn, b_dn):
    wsel = lambda e, *_: (e, 0, 0)
    packed, offsets = _pack_tables(tables)
    slots = n_blocks * MOE_PAIRS
    grid_spec = pltpu.PrefetchScalarGridSpec(
        num_scalar_prefetch=1,
        grid=(N_EXPERTS,),
        in_specs=[pl.BlockSpec(memory_space=pl.ANY),
                  pl.BlockSpec(memory_space=pl.ANY),
                  pl.BlockSpec((None, D_MODEL, 2 * D_FF), wsel),
                  pl.BlockSpec((None, 1, 2 * D_FF), wsel),
                  pl.BlockSpec((None, D_FF, D_MODEL), wsel),
                  pl.BlockSpec((None, 1, D_MODEL), wsel)],
        out_specs=pl.BlockSpec(memory_space=pl.ANY),
        scratch_shapes=[pltpu.VMEM((GATHER_AHEAD + 1, MOE_PAIRS * ROW_TILES, LANES), U32),
                        pltpu.VMEM((2, MOE_PAIRS * ROW_TILES, LANES), U32),
                        pltpu.VMEM((MOE_PAIRS * ROW_TILES, LANES), U32),
                        pltpu.VMEM((D_MODEL, 2 * D_FF), BF16),
                        pltpu.VMEM((D_FF, D_MODEL), BF16),
                        pltpu.SemaphoreType.DMA((GATHER_AHEAD + 1,)),
                        pltpu.SemaphoreType.DMA((2,)),
                        pltpu.SemaphoreType.DMA(())],
    )
    return pl.pallas_call(
        functools.partial(_moe_kernel, offsets=offsets, n_blocks=n_blocks),
        grid_spec=grid_spec,
        out_shape=jax.ShapeDtypeStruct((slots * ROW_TILES, LANES), U32),
        compiler_params=pltpu.CompilerParams(
            dimension_semantics=("arbitrary",),
            vmem_limit_bytes=VMEM_LIMIT_MOE),
        name="moe",
    )(packed, xs, xs2, w_gu, b_gu, w_dn, b_dn)


def _combine_kernel(tab_ref, x1_ref, pos_ref, gate_ref, gfin_ref, ys_hbm,
                    out_ref, ybuf, sems, *, tm, pairs_per_tile, offsets):
    csrc_ref, clen_ref, coff_ref, ctot_ref = [_Table(tab_ref, o) for o in offsets]
    i = pl.program_id(0)
    n_steps = pl.num_programs(0)
    n_sorted = 2 * pairs_per_tile
    n_slots = 2 * TILES_PER_STEP

    def gather_start(t, slot):
        def strip(e, carry):
            s = t * N_EXPERTS + e
            src0 = csrc_ref[s]
            dst0 = coff_ref[s]

            def piece(first, rows):
                pltpu.make_async_copy(_rows(ys_hbm, src0 + first, rows),
                                      _rows(ybuf.at[slot], dst0 + first, rows), sems.at[slot]).start()
            _strip_pieces(clen_ref[s], tm // 2, piece)
            return carry
        lax.fori_loop(0, N_EXPERTS, strip, 0)

    @pl.when(i == 0)
    def _():
        ybuf[...] = jnp.zeros_like(ybuf)
        for h in range(TILES_PER_STEP):
            gather_start(h, h)

    @pl.when(i + 1 < n_steps)
    def _():
        for h in range(TILES_PER_STEP):
            t = (i + 1) * TILES_PER_STEP + h
            gather_start(t, t % n_slots)

    slots = [(i * TILES_PER_STEP + h) % n_slots for h in range(TILES_PER_STEP)]
    for h in range(TILES_PER_STEP):
        def wait_piece(first, rows, slot=slots[h]):
            del first
            pltpu.make_async_copy(_rows(ys_hbm, 0, rows), _rows(ybuf.at[slot], 0, rows), sems.at[slot]).wait()
        _strip_pieces(ctot_ref[i * TILES_PER_STEP + h], _pow2_at_most(pairs_per_tile), wait_piece)

    jt = lax.broadcasted_iota(I32, (n_sorted, tm), 0).astype(jnp.int16)
    for h in range(TILES_PER_STEP):
        tok = slice(h * tm, (h + 1) * tm)
        y_sorted = pltpu.bitcast(_tiles_to_rows(ybuf.at[slots[h]], pairs_per_tile), BF16)
        unsort_t = jnp.zeros((n_sorted, tm), BF16)
        for k in range(TOP_K):
            unsort_t = jnp.where(jt == pos_ref[k:k + 1, tok].astype(jnp.int16),
                                 gate_ref[k:k + 1, tok].astype(BF16), unsort_t)
        acc = x1_ref[tok, :] + lax.dot_general(unsort_t, y_sorted, (((0,), (0,)), ((), ())),
                                               preferred_element_type=F32)
        out_ref[tok, :] = _rmsnorm(acc, gfin_ref[...])


def _combine(tables, x1, pos, gates, g_final, ys, tm, pairs_per_tile):
    t = x1.shape[0]
    nt = t // tm
    assert nt % TILES_PER_STEP == 0
    packed, offsets = _pack_tables(tables)
    grid_spec = pltpu.PrefetchScalarGridSpec(
        num_scalar_prefetch=1,
        grid=(nt // TILES_PER_STEP,),
        in_specs=[pl.BlockSpec((TILES_PER_STEP * tm, D_MODEL), lambda i, *_: (i, 0)),
                  pl.BlockSpec((TOP_K, TILES_PER_STEP * tm), lambda i, *_: (0, i)),
                  pl.BlockSpec((TOP_K, TILES_PER_STEP * tm), lambda i, *_: (0, i)),
                  pl.BlockSpec((1, D_MODEL), lambda i, *_: (0, 0)),
                  pl.BlockSpec(memory_space=pl.ANY)],
        out_specs=pl.BlockSpec((TILES_PER_STEP * tm, D_MODEL), lambda i, *_: (i, 0)),
        scratch_shapes=[pltpu.VMEM((2 * TILES_PER_STEP, pairs_per_tile * ROW_TILES, LANES), U32),
                        pltpu.SemaphoreType.DMA((2 * TILES_PER_STEP,))],
    )
    return pl.pallas_call(
        functools.partial(_combine_kernel, tm=tm, pairs_per_tile=pairs_per_tile, offsets=offsets),
        grid_spec=grid_spec,
        out_shape=jax.ShapeDtypeStruct((t, D_MODEL), F32),
        compiler_params=pltpu.CompilerParams(
            dimension_semantics=("arbitrary",),
            vmem_limit_bytes=VMEM_LIMIT_MIXER),
        name="combine",
    )(packed, x1, pos, gates, g_final, ys)


def _split_bf16(w):
    hi = w.astype(BF16)
    lo = (w - hi.astype(F32)).astype(BF16)
    return hi, lo


def kernel(x_prompt, x_sample, cache_conv, cache_mem_k, cache_mem_v, mem_prompt, g_mix, w_in, w_dw, b_dw, ln_conv_g, ln_conv_b, ln_v_g, ln_v_b, w_spatial, b_spatial, g_mem, w_mem_k, w_mem_v, w_out, g_ffn, w_router, b_router, w_gate_up, b_gate_up, w_down, b_down, g_final):
    depth = g_mix.shape[0]
    assert depth == 1
    l = 0
    bp, seq, _ = x_prompt.shape
    bs, dseq, _ = x_sample.shape
    assert seq % PROMPT_TILE == 0 and bs % SAMPLE_SEQS_PER_TILE == 0
    assert GMLP_CHUNK % dseq == 0 and (SAMPLE_SEQS_PER_TILE * dseq) % GMLP_CHUNK == 0

    row = lambda a: a.reshape(1, -1)
    wr_hi, wr_lo = _split_bf16(w_router[l].T)
    w_dw_pad = jnp.pad(w_dw[l], ((0, HIST - CONV_W), (0, 0)))
    bias_rows = lambda b: jnp.repeat(b.T, GMLP_HD, axis=1)
    common = dict(
        gmix=row(g_mix[l]), win=w_in[l].astype(BF16), wdw=w_dw_pad, bdw=row(b_dw[l]),
        lcg=row(ln_conv_g[l]), lcb=row(ln_conv_b[l]), lvg=row(ln_v_g[l]), lvb=row(ln_v_b[l]),
        wout=w_out[l].astype(BF16), gffn=row(g_ffn[l]), wrh=wr_hi, wrl=wr_lo,
        br=b_router[l].reshape(N_EXPERTS, 1))

    def weights(wsp, bsp):
        c = common
        return (c["gmix"], c["win"], c["wdw"], c["bdw"], c["lcg"], c["lcb"], c["lvg"], c["lvb"],
                wsp, bsp, c["wout"], c["gffn"], c["wrh"], c["wrl"], c["br"])

    reps = GMLP_CHUNK // dseq
    wts_p = weights(w_spatial[l], bias_rows(b_spatial[l]))
    wts_s = weights(jnp.tile(w_spatial[l][:, :dseq, :dseq], (1, reps, reps)),
                    bias_rows(jnp.tile(b_spatial[l][:, :dseq], (1, reps))))

    w_kv = jnp.concatenate([w_mem_k[l], w_mem_v[l]], axis=1).astype(BF16)
    kv_p = _memkv(mem_prompt, row(g_mem[l]), w_kv)
    mk_p = kv_p[:, :, :C_XA]
    mv_p = kv_p[:, :, C_XA:]
    zero_hist = jnp.zeros((bp, 1, HIST, C_CONV), F32)
    tp, ts = bp * seq, bs * dseq
    tm_s = SAMPLE_SEQS_PER_TILE * dseq
    ntp, nts = tp // PROMPT_TILE, ts // tm_s
    ppt_p = (TOP_K * PROMPT_TILE + N_EXPERTS + 1) // 2
    ppt_s = (TOP_K * tm_s + N_EXPERTS + 1) // 2
    (x1_p, xs_p, pos_p, gate_p, cnt_p, off_p, hist_p, _) = _mixer(
        x_prompt, zero_hist, mk_p[:, None], mv_p[:, None], wts_p,
        ns=1, sl=PROMPT_TILE, carry=True, sp_chunk=GMLP_CHUNK, pairs_per_tile=ppt_p)

    gs = bs // SAMPLE_SEQS_PER_TILE
    hist_s_in = jnp.pad(cache_conv[l], ((0, 0), (HIST_OFF, 0), (0, 0))).reshape(
        gs, SAMPLE_SEQS_PER_TILE, HIST, C_CONV)
    mk_s = cache_mem_k[l].reshape(gs, SAMPLE_SEQS_PER_TILE, N_MEM, C_XA)
    mv_s = cache_mem_v[l].reshape(gs, SAMPLE_SEQS_PER_TILE, N_MEM, C_XA)
    (x1_s, xs_s, pos_s, gate_s, cnt_s, off_s, hist_s, v_s) = _mixer(
        x_sample.reshape(gs, tm_s, D_MODEL), hist_s_in, mk_s, mv_s, wts_s,
        ns=SAMPLE_SEQS_PER_TILE, sl=dseq, carry=False, sp_chunk=dseq, pairs_per_tile=ppt_s)

    max_pairs = ((tp + ts) * TOP_K + (ntp + nts) * N_EXPERTS) // 2
    n_blocks = -(-max_pairs // MOE_PAIRS) + N_EXPERTS
    cnt = jnp.concatenate([cnt_p[:, :, 0], cnt_s[:, :, 0]], axis=0)
    off = jnp.concatenate([off_p[:, :, 0], off_s[:, :, 0]], axis=0)
    tile_pair0 = np.concatenate([np.arange(ntp) * ppt_p, np.arange(nts) * ppt_s]).astype(np.int32)
    counts = jnp.sum(cnt, axis=0)
    tile_base = jnp.cumsum(cnt, axis=0) - cnt
    padded = (counts + MOE_PAIRS - 1) // MOE_PAIRS * MOE_PAIRS
    pad_end = jnp.cumsum(padded)
    pad_start = pad_end - padded
    strip_dst = (pad_start[None, :] + tile_base).astype(I32)
    strip_src = (tile_pair0[:, None] + off).astype(I32)
    n_used = (pad_end[-1] // MOE_PAIRS).astype(I32)
    blk_pair0 = jnp.minimum(jnp.arange(n_blocks, dtype=I32), n_used - 1) * MOE_PAIRS
    block_expert = jnp.minimum(
        jnp.sum((pad_end[None, :] <= blk_pair0[:, None]).astype(I32), axis=1), N_EXPERTS - 1)
    of_block = block_expert[:, None] == jnp.arange(N_EXPERTS, dtype=I32)[None, :]
    last_pair = jnp.sum(jnp.where(of_block, (pad_start + counts)[None, :], 0), axis=1)
    block_valid = jnp.clip(last_pair - blk_pair0, 0, MOE_PAIRS).astype(I32)
    def strip_group(t0, t1):
        sdst = strip_dst[t0:t1].T.reshape(-1)
        ssrc = strip_src[t0:t1].T.reshape(-1)
        slen = cnt[t0:t1].T.reshape(-1).astype(I32)
        s_lo = jnp.sum(((sdst + slen)[None, :] <= blk_pair0[:, None]).astype(I32), axis=1)
        s_hi = jnp.sum((sdst[None, :] < (blk_pair0 + MOE_PAIRS)[:, None]).astype(I32), axis=1)
        return (s_lo, s_hi, ssrc, sdst, slen)

    moe_tables = ((pad_start // MOE_PAIRS).astype(I32), (padded // MOE_PAIRS).astype(I32),
                  block_valid, n_used.reshape(1),
                  *strip_group(0, ntp), *strip_group(ntp, ntp + nts))

    ys = _moe(moe_tables, xs_p, xs_s, n_blocks,
              w_gate_up[l], b_gate_up[l][:, None, :], w_down[l], b_down[l][:, None, :])
    gfin = row(g_final)

    def combine_tables(t0, t1):
        return (strip_dst[t0:t1].reshape(-1), cnt[t0:t1].reshape(-1).astype(I32),
                off[t0:t1].reshape(-1).astype(I32), jnp.sum(cnt[t0:t1], axis=1).astype(I32))

    y_p = _combine(combine_tables(0, ntp), x1_p, pos_p, gate_p, gfin, ys, PROMPT_TILE, ppt_p)
    y_s = _combine(combine_tables(ntp, ntp + nts), x1_s, pos_s, gate_s, gfin, ys, tm_s, ppt_s)

    return (y_p.reshape(bp, seq, D_MODEL),
            y_s.reshape(bs, dseq, D_MODEL),
            hist_p[:, 0, HIST_OFF:, :][None],
            mk_p.reshape(bp, N_MEM, XA_HEADS, XA_HD)[None],
            mv_p.reshape(bp, N_MEM, XA_HEADS, XA_HD)[None],
            hist_s.reshape(bs, HIST, C_CONV)[:, HIST_OFF:, :][None],
            v_s.reshape(bs, dseq, C_GMLP)[None])
```

```python
import functools

import numpy as np
import jax
import jax.numpy as jnp
from jax import lax
from jax.experimental import pallas as pl
from jax.experimental.pallas import tpu as pltpu

F32 = jnp.float32
BF16 = jnp.bfloat16
I32 = jnp.int32
U32 = jnp.uint32

D_MODEL = 1024
C_CONV = 384
CONV_W = 31
C_GMLP = 384
GMLP_HEADS = 4
GMLP_HD = 96
GMLP_CHUNK = 128
XA_HEADS = 4
XA_HD = 64
C_XA = 256
N_MEM = 256
N_EXPERTS = 32
TOP_K = 4
D_FF = 1024
SWIGLU_LIMIT = 7.0
SWIGLU_ALPHA = 1.702
EPS = 1e-5
IN_COLS = 2 * C_CONV + 2 * C_GMLP + C_XA

SUBLANES = 8
LANES = 128
ROW_TILES = D_MODEL // LANES
HIST = 32
HIST_OFF = HIST - (CONV_W - 1)

PROMPT_TILE = 512
SAMPLE_SEQS_PER_TILE = 8
MOE_BLOCK = 512
MOE_PAIRS = MOE_BLOCK // 2
GATHER_AHEAD = 2
TILES_PER_STEP = 2
VMEM_LIMIT_MIXER = 48 * 1024 * 1024
VMEM_LIMIT_MOE = 52 * 1024 * 1024


def _rmsnorm(x, g):
    return x * lax.rsqrt(jnp.mean(x * x, axis=-1, keepdims=True) + EPS) * g


def _layernorm(x, g, b):
    mu = jnp.mean(x, axis=-1, keepdims=True)
    xc = x - mu
    var = jnp.mean(xc * xc, axis=-1, keepdims=True)
    return xc * lax.rsqrt(var + EPS) * g + b


def _gelu(x):
    return 0.5 * x * (1.0 + lax.erf(x * np.float32(1.0 / np.sqrt(2.0))))


def _sigmoid(x):
    return 1.0 / (1.0 + jnp.exp(-x))


def _rows_to_tiles(dst_ref, val, rows, row0=0):
    for j in range(ROW_TILES):
        dst_ref[pl.ds(row0 * ROW_TILES + j, rows, stride=ROW_TILES), :] = val[:, j * LANES:(j + 1) * LANES]


def _tiles_to_rows(src_ref, rows, row0=0):
    return jnp.concatenate(
        [src_ref[pl.ds(row0 * ROW_TILES + j, rows, stride=ROW_TILES), :] for j in range(ROW_TILES)],
        axis=-1)


def _memkv_kernel(mem_ref, g_ref, w_ref, o_ref):
    mn = _rmsnorm(mem_ref[0], g_ref[...])
    o_ref[0] = jnp.dot(mn.astype(BF16), w_ref[...], preferred_element_type=F32)


def _memkv(mem, g_mem, w_kv):
    b = mem.shape[0]
    return pl.pallas_call(
        _memkv_kernel,
        grid=(b,),
        in_specs=[pl.BlockSpec((1, N_MEM, D_MODEL), lambda i: (i, 0, 0)),
                  pl.BlockSpec((1, D_MODEL), lambda i: (0, 0)),
                  pl.BlockSpec((D_MODEL, 2 * C_XA), lambda i: (0, 0))],
        out_specs=pl.BlockSpec((1, N_MEM, 2 * C_XA), lambda i: (i, 0, 0)),
        out_shape=jax.ShapeDtypeStruct((b, N_MEM, 2 * C_XA), F32),
        name="memkv",
    )(mem, g_mem, w_kv)


def _mixer_kernel(x_ref, hist_ref, mk_ref, mv_ref, gmix_ref, win_ref, wdw_ref, bdw_ref,
                  lcg_ref, lcb_ref, lvg_ref, lvb_ref, wsp_ref, bsp_ref, wout_ref, gffn_ref,
                  wrh_ref, wrl_ref, br_ref, *rest, ns, sl, carry, sp_chunk, tiles_per_seq, n_sorted):
    (x1_ref, xst_ref, pos_ref, gate_ref, cnt_ref, off_ref, histout_ref, v_ref,
     ext_ref, shift_ref, h2_keep, pos_keep, before_ref, wsp16_ref) = rest
    tm = ns * sl
    step = pl.program_id(0)
    n_tiles = pl.num_programs(0) - 1

    def sort_previous_tile():
        jj = lax.broadcasted_iota(I32, (n_sorted, tm), 0).astype(jnp.int16)
        perm = jnp.zeros((n_sorted, tm), BF16)
        for k in range(TOP_K):
            perm = jnp.where(jj == pos_keep[k:k + 1, :].astype(jnp.int16), jnp.ones((), BF16), perm)
        sorted_rows = jnp.dot(perm, h2_keep[...], preferred_element_type=F32)
        pairs = pltpu.bitcast(sorted_rows.astype(BF16), U32)
        _rows_to_tiles(xst_ref, pairs, n_sorted // 2)

    @pl.when(step == 0)
    def _():
        h2_keep[...] = jnp.zeros_like(h2_keep)
        pos_keep[...] = jnp.full(pos_keep.shape, -1, I32)
        tr = lax.broadcasted_iota(I32, (tm, tm), 0)
        tc = lax.broadcasted_iota(I32, (tm, tm), 1)
        before_ref[...] = jnp.where(tr < tc, 1.0, 0.0).astype(BF16)
        rr = lax.broadcasted_iota(I32, (GMLP_CHUNK, GMLP_CHUNK), 0)
        cc = lax.broadcasted_iota(I32, (GMLP_CHUNK, GMLP_CHUNK), 1)
        sp_mask = (cc <= rr) & ((rr // sp_chunk) == (cc // sp_chunk))
        for hh in range(GMLP_HEADS):
            wsp16_ref[hh] = jnp.where(sp_mask, wsp_ref[hh], 0.0).astype(BF16)

    @pl.when(step < n_tiles)
    def _():
        sort_previous_tile()
        _mixer_tile(x_ref, hist_ref, mk_ref, mv_ref, gmix_ref, win_ref, wdw_ref, bdw_ref,
                    lcg_ref, lcb_ref, lvg_ref, lvb_ref, wsp16_ref, bsp_ref, wout_ref, gffn_ref,
                    wrh_ref, wrl_ref, br_ref, x1_ref, pos_ref, gate_ref, cnt_ref, off_ref,
                    histout_ref, v_ref, ext_ref, shift_ref, h2_keep, pos_keep, before_ref,
                    first_of_seq=(step % tiles_per_seq) == 0,
                    ns=ns, sl=sl, carry=carry)

    @pl.when(step == n_tiles)
    def _():
        sort_previous_tile()


def _mixer_tile(x_ref, hist_ref, mk_ref, mv_ref, gmix_ref, win_ref, wdw_ref, bdw_ref,
                lcg_ref, lcb_ref, lvg_ref, lvb_ref, wsp16_ref, bsp_ref, wout_ref, gffn_ref,
                wrh_ref, wrl_ref, br_ref, x1_ref, pos_ref, gate_ref, cnt_ref, off_ref,
                histout_ref, v_ref, ext_ref, shift_ref, h2_keep, pos_keep, before_ref,
                *, first_of_seq, ns, sl, carry):
    tm = ns * sl
    x = x_ref[0]
    h = _rmsnorm(x, gmix_ref[...])
    z = jnp.dot(h.astype(BF16), win_ref[...], preferred_element_type=F32)
    z_a = z[:, 0:C_CONV]
    z_g = z[:, C_CONV:2 * C_CONV]
    z_u = z[:, 2 * C_CONV:2 * C_CONV + C_GMLP]
    z_v = z[:, 2 * C_CONV + C_GMLP:2 * C_CONV + 2 * C_GMLP]
    z_q = z[:, 2 * C_CONV + 2 * C_GMLP:IN_COLS]

    glu = z_a * _sigmoid(z_g)
    if carry:
        ext_ref[:, 0:HIST, :] = jnp.where(first_of_seq, hist_ref[0], ext_ref[:, 0:HIST, :])
    else:
        ext_ref[:, 0:HIST, :] = hist_ref[0]
    conv_parts = []
    for s in range(ns):
        ext_s = ext_ref.at[s]
        ext_s[HIST:HIST + sl, :] = glu[s * sl:(s + 1) * sl]
        n_shift = HIST + sl - SUBLANES
        for r in range(1, SUBLANES):
            shift_ref[s, r - 1, 0:n_shift, :] = ext_s[pl.ds(r, n_shift), :]
        rc = min(sl, 64)
        for r0 in range(0, sl, rc):
            acc = jnp.broadcast_to(bdw_ref[...], (rc, C_CONV))
            for j in range(CONV_W):
                a, r = divmod(j + HIST_OFF, SUBLANES)
                src = ext_s if r == 0 else shift_ref.at[s, r - 1]
                acc = acc + wdw_ref[j:j + 1, :] * src[pl.ds(r0 + a * SUBLANES, rc), :]
            conv_parts.append(acc)
        new_hist = ext_s[sl:sl + HIST, :]
        histout_ref[0, s] = new_hist
        if carry:
            ext_s[0:HIST, :] = new_hist
    y = jnp.concatenate(conv_parts, axis=0) if len(conv_parts) > 1 else conv_parts[0]
    y = _layernorm(y, lcg_ref[...], lcb_ref[...])
    c_out = y * _sigmoid(y)

    u = _gelu(z_u)
    v = _layernorm(_gelu(z_v), lvg_ref[...], lvb_ref[...])
    v_ref[...] = v
    vb = v.astype(BF16)
    col = lax.broadcasted_iota(I32, (GMLP_CHUNK, C_GMLP), 1)
    w_heads = [wsp16_ref[hh] for hh in range(GMLP_HEADS)]
    g_parts = []
    for c in range(tm // GMLP_CHUNK):
        vc = vb[c * GMLP_CHUNK:(c + 1) * GMLP_CHUNK]
        sg = bsp_ref[...]
        for hh in range(GMLP_HEADS):
            head_cols = (col >= hh * GMLP_HD) & (col < (hh + 1) * GMLP_HD)
            vh = jnp.where(head_cols, vc, jnp.zeros_like(vc))
            sg = sg + jnp.dot(w_heads[hh], vh, preferred_element_type=F32)
        g_parts.append(u[c * GMLP_CHUNK:(c + 1) * GMLP_CHUNK] * sg)
    g_out = jnp.concatenate(g_parts, axis=0) if len(g_parts) > 1 else g_parts[0]

    qs = z_q * np.float32(XA_HD ** -0.5)
    qcol = lax.broadcasted_iota(I32, (sl, C_XA), 1)
    a_parts = []
    for s in range(ns):
        q_s = qs[s * sl:(s + 1) * sl]
        kb = mk_ref[0, s].astype(BF16)
        vvb = mv_ref[0, s].astype(BF16)
        hmasks = [(qcol >= hh * XA_HD) & (qcol < (hh + 1) * XA_HD) for hh in range(XA_HEADS)]
        stack = XA_HEADS if sl * XA_HEADS <= N_MEM else 1
        a_s = jnp.zeros((sl, C_XA), F32)
        for h0 in range(0, XA_HEADS, stack):
            heads = range(h0, h0 + stack)
            qh = jnp.concatenate([jnp.where(hmasks[hh], q_s, 0.0) for hh in heads], axis=0).astype(BF16)
            sc = lax.dot_general(qh, kb, (((1,), (1,)), ((), ())), preferred_element_type=F32)
            p = jnp.exp(sc - jnp.max(sc, axis=-1, keepdims=True))
            den = jnp.sum(p, axis=-1, keepdims=True)
            oh = jnp.dot(p.astype(BF16), vvb, preferred_element_type=F32) / den
            for n, hh in enumerate(heads):
                a_s = a_s + jnp.where(hmasks[hh], oh[n * sl:(n + 1) * sl], 0.0)
        a_parts.append(a_s)
    a_out = jnp.concatenate(a_parts, axis=0) if len(a_parts) > 1 else a_parts[0]

    mix = jnp.concatenate([c_out, g_out, a_out], axis=-1).astype(BF16)
    x1 = x + jnp.dot(mix, wout_ref[...], preferred_element_type=F32)
    x1_ref[...] = x1

    h2 = _rmsnorm(x1, gffn_ref[...])
    h2_hi = h2.astype(BF16)
    h2_lo = (h2 - h2_hi.astype(F32)).astype(BF16)
    nt_dims = (((1,), (1,)), ((), ()))
    lg = (lax.dot_general(wrh_ref[...], h2_hi, nt_dims, preferred_element_type=F32)
          + lax.dot_general(wrl_ref[...], h2_hi, nt_dims, preferred_element_type=F32)
          + lax.dot_general(wrh_ref[...], h2_lo, nt_dims, preferred_element_type=F32)
          + br_ref[...])
    eio = lax.broadcasted_iota(I32, (N_EXPERTS, tm), 0)
    work = lg
    vals, idxs = [], []
    for _ in range(TOP_K):
        m = jnp.max(work, axis=0, keepdims=True)
        idx = jnp.min(jnp.where(work == m, eio, N_EXPERTS), axis=0, keepdims=True)
        vals.append(m)
        idxs.append(idx)
        work = jnp.where(eio == idx, -jnp.inf, work)
    exps = [jnp.exp(vk - vals[0]) for vk in vals]
    den = exps[0] + exps[1] + exps[2] + exps[3]
    gate_ref[...] = jnp.concatenate([ek / den for ek in exps], axis=0)

    sel = jnp.zeros((N_EXPERTS, tm), F32)
    for idx in idxs:
        sel = sel + jnp.where(eio == idx, 1.0, 0.0)
    selb = sel.astype(BF16)
    ranks = jnp.dot(selb, before_ref[...], preferred_element_type=F32)
    er = lax.broadcasted_iota(I32, (N_EXPERTS, N_EXPERTS), 0)
    ec = lax.broadcasted_iota(I32, (N_EXPERTS, N_EXPERTS), 1)
    lower = jnp.where(ec < er, 1.0, 0.0).astype(BF16)
    cnt = jnp.sum(sel, axis=1, keepdims=True)
    half_len = jnp.floor((cnt + 1.0) * 0.5)
    off_pairs = jnp.dot(lower, jnp.broadcast_to(half_len, (N_EXPERTS, LANES)).astype(BF16),
                        preferred_element_type=F32)[:, 0:1]
    cnt_ref[0] = jnp.broadcast_to(half_len, (N_EXPERTS, LANES)).astype(I32)
    off_ref[0] = jnp.broadcast_to(off_pairs, (N_EXPERTS, LANES)).astype(I32)
    slot_of = 2.0 * off_pairs + ranks
    pos = [jnp.sum(jnp.where(eio == idx, slot_of, 0.0), axis=0, keepdims=True).astype(I32) for idx in idxs]
    pos_all = jnp.concatenate(pos, axis=0)
    pos_ref[...] = pos_all
    pos_keep[...] = pos_all
    h2_keep[...] = h2_hi


def _mixer(x, hist, mem_k, mem_v, wts, *, ns, sl, carry, sp_chunk, pairs_per_tile):
    g, r, _ = x.shape
    tm = ns * sl
    nt = r // tm
    ntot = g * nt
    assert 2 * pairs_per_tile >= TOP_K * tm + N_EXPERTS
    tile = lambda s: jnp.minimum(s, ntot - 1)
    const2 = lambda s: (0, 0)
    const3 = lambda s: (0, 0, 0)
    tile_row = lambda s: (tile(s), 0)
    tile_lane = lambda s: (0, tile(s))
    per_seq = lambda s: (tile(s) // nt, 0, 0, 0)
    in_specs = [
        pl.BlockSpec((1, tm, D_MODEL), lambda s: (tile(s) // nt, tile(s) % nt, 0)),
        pl.BlockSpec((1, ns, HIST, C_CONV), per_seq),
        pl.BlockSpec((1, ns, N_MEM, C_XA), per_seq),
        pl.BlockSpec((1, ns, N_MEM, C_XA), per_seq),
        pl.BlockSpec((1, D_MODEL), const2),
        pl.BlockSpec((D_MODEL, IN_COLS), const2),
        pl.BlockSpec((HIST, C_CONV), const2),
        pl.BlockSpec((1, C_CONV), const2),
        pl.BlockSpec((1, C_CONV), const2),
        pl.BlockSpec((1, C_CONV), const2),
        pl.BlockSpec((1, C_GMLP), const2),
        pl.BlockSpec((1, C_GMLP), const2),
        pl.BlockSpec((GMLP_HEADS, GMLP_CHUNK, GMLP_CHUNK), const3),
        pl.BlockSpec((GMLP_CHUNK, C_GMLP), const2),
        pl.BlockSpec((D_MODEL, D_MODEL), const2),
        pl.BlockSpec((1, D_MODEL), const2),
        pl.BlockSpec((N_EXPERTS, D_MODEL), const2),
        pl.BlockSpec((N_EXPERTS, D_MODEL), const2),
        pl.BlockSpec((N_EXPERTS, 1), const2),
    ]
    tile_cnt = lambda s: (tile(s), 0, 0)
    out_specs = [
        pl.BlockSpec((tm, D_MODEL), tile_row),
        pl.BlockSpec((pairs_per_tile * ROW_TILES, LANES),
                     lambda s: (jnp.maximum(s - 1, 0), 0)),
        pl.BlockSpec((TOP_K, tm), tile_lane),
        pl.BlockSpec((TOP_K, tm), tile_lane),
        pl.BlockSpec((1, N_EXPERTS, LANES), tile_cnt),
        pl.BlockSpec((1, N_EXPERTS, LANES), tile_cnt),
        pl.BlockSpec((1, ns, HIST, C_CONV), per_seq),
        pl.BlockSpec((tm, C_GMLP), tile_row),
    ]
    rows = g * r
    out_shape = [
        jax.ShapeDtypeStruct((rows, D_MODEL), F32),
        jax.ShapeDtypeStruct((ntot * pairs_per_tile * ROW_TILES, LANES), U32),
        jax.ShapeDtypeStruct((TOP_K, rows), I32),
        jax.ShapeDtypeStruct((TOP_K, rows), F32),
        jax.ShapeDtypeStruct((ntot, N_EXPERTS, LANES), I32),
        jax.ShapeDtypeStruct((ntot, N_EXPERTS, LANES), I32),
        jax.ShapeDtypeStruct((g, ns, HIST, C_CONV), F32),
        jax.ShapeDtypeStruct((rows, C_GMLP), F32),
    ]
    kern = functools.partial(_mixer_kernel, ns=ns, sl=sl, carry=carry, sp_chunk=sp_chunk,
                             tiles_per_seq=nt, n_sorted=2 * pairs_per_tile)
    return pl.pallas_call(
        kern,
        grid=(ntot + 1,),
        in_specs=in_specs,
        out_specs=out_specs,
        out_shape=out_shape,
        scratch_shapes=[pltpu.VMEM((ns, HIST + sl, C_CONV), F32),
                        pltpu.VMEM((ns, SUBLANES - 1, HIST + sl, C_CONV), F32),
                        pltpu.VMEM((tm, D_MODEL), BF16),
                        pltpu.VMEM((TOP_K, tm), I32),
                        pltpu.VMEM((tm, tm), BF16),
                        pltpu.VMEM((GMLP_HEADS, GMLP_CHUNK, GMLP_CHUNK), BF16)],
        compiler_params=pltpu.CompilerParams(
            dimension_semantics=("arbitrary",),
            vmem_limit_bytes=VMEM_LIMIT_MIXER),
        name="mixer_carry" if carry else "mixer_cache",
    )(x, hist, mem_k, mem_v, *wts)


def _strip_pieces(n, max_rows, fn):
    done = 0
    p = max_rows
    while p >= 1:
        has = (n & p) != 0
        pl.when(has)(functools.partial(fn, done, p))
        done = done + jnp.where(has, p, 0)
        p //= 2


class _Table:
    def __init__(self, ref, offset):
        self.ref, self.offset = ref, offset

    def __getitem__(self, i):
        return self.ref[self.offset + i]


def _pack_tables(tables):
    offsets = tuple(int(o) for o in np.cumsum([0] + [t.shape[0] for t in tables[:-1]]))
    return jnp.concatenate([t.astype(I32) for t in tables]), offsets


def _pow2_at_most(n):
    return 1 << (n.bit_length() - 1)


def _rows(ref, first_row, rows):
    return ref.at[pl.ds(pl.multiple_of(first_row * ROW_TILES, ROW_TILES), rows * ROW_TILES)]


def _moe_kernel(tab_ref, xs_hbm, xs2_hbm, wgu_ref, bgu_ref, wdn_ref, bdn_ref, ys_hbm,
                xbuf, ybuf, zbuf, wgu16, wdn16, sems, ysems, zsem, *, offsets, n_blocks):
    tabs = [_Table(tab_ref, o) for o in offsets]
    b0_ref, nbk_ref, bv_ref, nb_ref = tabs[:4]
    strips_a, strips_b = tabs[4:9], tabs[9:14]
    e = pl.program_id(0)
    n_used = nb_ref[0]
    half = MOE_PAIRS // 2

    def gather_start(b, slot):
        pair0 = b * MOE_PAIRS

        for src_hbm, (slo_ref, shi_ref, ssrc_ref, sdst_ref, slen_ref) in ((xs_hbm, strips_a), (xs2_hbm, strips_b)):
            def strip(s, carry, src_hbm=src_hbm, ssrc_ref=ssrc_ref, sdst_ref=sdst_ref, slen_ref=slen_ref):
                lo = jnp.maximum(sdst_ref[s], pair0)
                hi = jnp.minimum(sdst_ref[s] + slen_ref[s], pair0 + MOE_PAIRS)
                src0 = ssrc_ref[s] + (lo - sdst_ref[s])
                dst0 = lo - pair0

                def piece(first, rows):
                    pltpu.make_async_copy(_rows(src_hbm, src0 + first, rows),
                                          _rows(xbuf.at[slot], dst0 + first, rows),
                                          sems.at[slot]).start(priority=rows.bit_length() % 2)
                _strip_pieces(hi - lo, MOE_PAIRS, piece)
                return carry
            lax.fori_loop(slo_ref[b], shi_ref[b], strip, 0)

    def gather_wait(b, slot):
        def piece(first, rows):
            del first
            pltpu.make_async_copy(_rows(xs_hbm, 0, rows), _rows(xbuf.at[slot], 0, rows), sems.at[slot]).wait()
        _strip_pieces(bv_ref[b], MOE_PAIRS, piece)

    def out_copy(b, slot):
        return pltpu.make_async_copy(ybuf.at[slot], _rows(ys_hbm, b * MOE_PAIRS, MOE_PAIRS), ysems.at[slot])

    def expert_mlp(xslot, slot, h):
        xb = pltpu.bitcast(_tiles_to_rows(xbuf.at[xslot], half, h * half), BF16)
        gu = jnp.dot(xb, wgu16[...], preferred_element_type=F32) + bgu_ref[0]
        gate = jnp.minimum(gu[:, :D_FF], SWIGLU_LIMIT)
        up = jnp.clip(gu[:, D_FF:], -SWIGLU_LIMIT, SWIGLU_LIMIT)
        act = (up + 1.0) * (gate * _sigmoid(SWIGLU_ALPHA * gate))
        yb = jnp.dot(act.astype(BF16), wdn16[...], preferred_element_type=F32) + bdn_ref[0]
        _rows_to_tiles(ybuf.at[slot], pltpu.bitcast(yb.astype(BF16), U32), half, h * half)

    def fill_copy(b):
        return pltpu.make_async_copy(zbuf, _rows(ys_hbm, b * MOE_PAIRS, MOE_PAIRS), zsem)

    @pl.when(e == 0)
    def _():
        zbuf[...] = jnp.zeros_like(zbuf)

        def start_fill(b, carry):
            fill_copy(b).start()
            return carry
        lax.fori_loop(n_used, n_blocks, start_fill, 0)

        xbuf[...] = jnp.zeros_like(xbuf)
        for a in range(GATHER_AHEAD):
            @pl.when(a < n_used)
            def _():
                gather_start(a, a)

    @pl.when(nbk_ref[e] > 0)
    def _():
        def cast_rows(c, carry):
            r = pl.multiple_of(c * LANES, LANES)
            wgu16[pl.ds(r, LANES), :] = wgu_ref[pl.ds(r, LANES), :].astype(BF16)
            wdn16[pl.ds(r, LANES), :] = wdn_ref[pl.ds(r, LANES), :].astype(BF16)
            return carry
        lax.fori_loop(0, D_MODEL // LANES, cast_rows, 0)

        def block(j, carry):
            b = b0_ref[e] + j
            slot = b % 2
            xslot = b % (GATHER_AHEAD + 1)

            @pl.when(b + GATHER_AHEAD < n_used)
            def _():
                gather_start(b + GATHER_AHEAD, (b + GATHER_AHEAD) % (GATHER_AHEAD + 1))

            gather_wait(b, xslot)

            @pl.when(b >= 2)
            def _():
                out_copy(b - 2, slot).wait()

            valid = bv_ref[b]

            @pl.when(valid > half)
            def _():
                expert_mlp(xslot, slot, 0)
                expert_mlp(xslot, slot, 1)

            @pl.when(valid <= half)
            def _():
                expert_mlp(xslot, slot, 0)
                ybuf[slot, pl.ds(half * ROW_TILES, half * ROW_TILES), :] = jnp.zeros(
                    (half * ROW_TILES, LANES), U32)

            out_copy(b, slot).start()
            return carry
        lax.fori_loop(0, nbk_ref[e], block, 0)

    @pl.when(e == pl.num_programs(0) - 1)
    def _():
        for back in (1, 2):
            @pl.when(n_used >= back)
            def _():
                out_copy(n_used - back, (n_used - back) % 2).wait()

        def wait_fill(b, carry):
            fill_copy(b).wait()
            return carry
        lax.fori_loop(n_used, n_blocks, wait_fill, 0)


def _moe(tables, xs, xs2, n_blocks, w_gu, b_gu, w_dn, b_dn):
    wsel = lambda e, *_: (e, 0, 0)
    packed, offsets = _pack_tables(tables)
    slots = n_blocks * MOE_PAIRS
    grid_spec = pltpu.PrefetchScalarGridSpec(
        num_scalar_prefetch=1,
        grid=(N_EXPERTS,),
        in_specs=[pl.BlockSpec(memory_space=pl.ANY),
                  pl.BlockSpec(memory_space=pl.ANY),
                  pl.BlockSpec((None, D_MODEL, 2 * D_FF), wsel),
                  pl.BlockSpec((None, 1, 2 * D_FF), wsel),
                  pl.BlockSpec((None, D_FF, D_MODEL), wsel),
                  pl.BlockSpec((None, 1, D_MODEL), wsel)],
        out_specs=pl.BlockSpec(memory_space=pl.ANY),
        scratch_shapes=[pltpu.VMEM((GATHER_AHEAD + 1, MOE_PAIRS * ROW_TILES, LANES), U32),
                        pltpu.VMEM((2, MOE_PAIRS * ROW_TILES, LANES), U32),
                        pltpu.VMEM((MOE_PAIRS * ROW_TILES, LANES), U32),
                        pltpu.VMEM((D_MODEL, 2 * D_FF), BF16),
                        pltpu.VMEM((D_FF, D_MODEL), BF16),
                        pltpu.SemaphoreType.DMA((GATHER_AHEAD + 1,)),
                        pltpu.SemaphoreType.DMA((2,)),
                        pltpu.SemaphoreType.DMA(())],
    )
    return pl.pallas_call(
        functools.partial(_moe_kernel, offsets=offsets, n_blocks=n_blocks),
        grid_spec=grid_spec,
        out_shape=jax.ShapeDtypeStruct((slots * ROW_TILES, LANES), U32),
        compiler_params=pltpu.CompilerParams(
            dimension_semantics=("arbitrary",),
            vmem_limit_bytes=VMEM_LIMIT_MOE),
        name="moe",
    )(packed, xs, xs2, w_gu, b_gu, w_dn, b_dn)


def _combine_kernel(tab_ref, x1_ref, pos_ref, gate_ref, gfin_ref, ys_hbm,
                    out_ref, ybuf, sems, *, tm, pairs_per_tile, offsets):
    csrc_ref, clen_ref, coff_ref, ctot_ref = [_Table(tab_ref, o) for o in offsets]
    i = pl.program_id(0)
    n_steps = pl.num_programs(0)
    n_sorted = 2 * pairs_per_tile
    n_slots = 2 * TILES_PER_STEP

    def gather_start(t, slot):
        def strip(e, carry):
            s = t * N_EXPERTS + e
            src0 = csrc_ref[s]
            dst0 = coff_ref[s]

            def piece(first, rows):
                pltpu.make_async_copy(_rows(ys_hbm, src0 + first, rows),
                                      _rows(ybuf.at[slot], dst0 + first, rows),
                                      sems.at[slot]).start(priority=rows.bit_length() % 2)
            _strip_pieces(clen_ref[s], tm // 2, piece)
            return carry
        lax.fori_loop(0, N_EXPERTS, strip, 0)

    @pl.when(i == 0)
    def _():
        ybuf[...] = jnp.zeros_like(ybuf)
        for h in range(TILES_PER_STEP):
            gather_start(h, h)

    @pl.when(i + 1 < n_steps)
    def _():
        for h in range(TILES_PER_STEP):
            t = (i + 1) * TILES_PER_STEP + h
            gather_start(t, t % n_slots)

    slots = [(i * TILES_PER_STEP + h) % n_slots for h in range(TILES_PER_STEP)]
    for h in range(TILES_PER_STEP):
        def wait_piece(first, rows, slot=slots[h]):
            del first
            pltpu.make_async_copy(_rows(ys_hbm, 0, rows), _rows(ybuf.at[slot], 0, rows), sems.at[slot]).wait()
        _strip_pieces(ctot_ref[i * TILES_PER_STEP + h], _pow2_at_most(pairs_per_tile), wait_piece)

    jt = lax.broadcasted_iota(I32, (n_sorted, tm), 0).astype(jnp.int16)
    for h in range(TILES_PER_STEP):
        tok = slice(h * tm, (h + 1) * tm)
        y_sorted = pltpu.bitcast(_tiles_to_rows(ybuf.at[slots[h]], pairs_per_tile), BF16)
        unsort_t = jnp.zeros((n_sorted, tm), BF16)
        for k in range(TOP_K):
            unsort_t = jnp.where(jt == pos_ref[k:k + 1, tok].astype(jnp.int16),
                                 gate_ref[k:k + 1, tok].astype(BF16), unsort_t)
        acc = x1_ref[tok, :] + lax.dot_general(unsort_t, y_sorted, (((0,), (0,)), ((), ())),
                                               preferred_element_type=F32)
        out_ref[tok, :] = _rmsnorm(acc, gfin_ref[...])


def _combine(tables, x1, pos, gates, g_final, ys, tm, pairs_per_tile):
    t = x1.shape[0]
    nt = t // tm
    assert nt % TILES_PER_STEP == 0
    packed, offsets = _pack_tables(tables)
    grid_spec = pltpu.PrefetchScalarGridSpec(
        num_scalar_prefetch=1,
        grid=(nt // TILES_PER_STEP,),
        in_specs=[pl.BlockSpec((TILES_PER_STEP * tm, D_MODEL), lambda i, *_: (i, 0)),
                  pl.BlockSpec((TOP_K, TILES_PER_STEP * tm), lambda i, *_: (0, i)),
                  pl.BlockSpec((TOP_K, TILES_PER_STEP * tm), lambda i, *_: (0, i)),
                  pl.BlockSpec((1, D_MODEL), lambda i, *_: (0, 0)),
                  pl.BlockSpec(memory_space=pl.ANY)],
        out_specs=pl.BlockSpec((TILES_PER_STEP * tm, D_MODEL), lambda i, *_: (i, 0)),
        scratch_shapes=[pltpu.VMEM((2 * TILES_PER_STEP, pairs_per_tile * ROW_TILES, LANES), U32),
                        pltpu.SemaphoreType.DMA((2 * TILES_PER_STEP,))],
    )
    return pl.pallas_call(
        functools.partial(_combine_kernel, tm=tm, pairs_per_tile=pairs_per_tile, offsets=offsets),
        grid_spec=grid_spec,
        out_shape=jax.ShapeDtypeStruct((t, D_MODEL), F32),
        compiler_params=pltpu.CompilerParams(
            dimension_semantics=("arbitrary",),
            vmem_limit_bytes=VMEM_LIMIT_MIXER),
        name="combine",
    )(packed, x1, pos, gates, g_final, ys)


def _split_bf16(w):
    hi = w.astype(BF16)
    lo = (w - hi.astype(F32)).astype(BF16)
    return hi, lo


def kernel(x_prompt, x_sample, cache_conv, cache_mem_k, cache_mem_v, mem_prompt, g_mix, w_in, w_dw, b_dw, ln_conv_g, ln_conv_b, ln_v_g, ln_v_b, w_spatial, b_spatial, g_mem, w_mem_k, w_mem_v, w_out, g_ffn, w_router, b_router, w_gate_up, b_gate_up, w_down, b_down, g_final):
    depth = g_mix.shape[0]
    assert depth == 1
    l = 0
    bp, seq, _ = x_prompt.shape
    bs, dseq, _ = x_sample.shape
    assert seq % PROMPT_TILE == 0 and bs % SAMPLE_SEQS_PER_TILE == 0
    assert GMLP_CHUNK % dseq == 0 and (SAMPLE_SEQS_PER_TILE * dseq) % GMLP_CHUNK == 0

    row = lambda a: a.reshape(1, -1)
    wr_hi, wr_lo = _split_bf16(w_router[l].T)
    w_dw_pad = jnp.pad(w_dw[l], ((0, HIST - CONV_W), (0, 0)))
    bias_rows = lambda b: jnp.repeat(b.T, GMLP_HD, axis=1)
    common = dict(
        gmix=row(g_mix[l]), win=w_in[l].astype(BF16), wdw=w_dw_pad, bdw=row(b_dw[l]),
        lcg=row(ln_conv_g[l]), lcb=row(ln_conv_b[l]), lvg=row(ln_v_g[l]), lvb=row(ln_v_b[l]),
        wout=w_out[l].astype(BF16), gffn=row(g_ffn[l]), wrh=wr_hi, wrl=wr_lo,
        br=b_router[l].reshape(N_EXPERTS, 1))

    def weights(wsp, bsp):
        c = common
        return (c["gmix"], c["win"], c["wdw"], c["bdw"], c["lcg"], c["lcb"], c["lvg"], c["lvb"],
                wsp, bsp, c["wout"], c["gffn"], c["wrh"], c["wrl"], c["br"])

    reps = GMLP_CHUNK // dseq
    wts_p = weights(w_spatial[l], bias_rows(b_spatial[l]))
    wts_s = weights(jnp.tile(w_spatial[l][:, :dseq, :dseq], (1, reps, reps)),
                    bias_rows(jnp.tile(b_spatial[l][:, :dseq], (1, reps))))

    w_kv = jnp.concatenate([w_mem_k[l], w_mem_v[l]], axis=1).astype(BF16)
    kv_p = _memkv(mem_prompt, row(g_mem[l]), w_kv)
    mk_p = kv_p[:, :, :C_XA]
    mv_p = kv_p[:, :, C_XA:]
    zero_hist = jnp.zeros((bp, 1, HIST, C_CONV), F32)
    tp, ts = bp * seq, bs * dseq
    tm_s = SAMPLE_SEQS_PER_TILE * dseq
    ntp, nts = tp // PROMPT_TILE, ts // tm_s
    ppt_p = (TOP_K * PROMPT_TILE + N_EXPERTS + 1) // 2
    ppt_s = (TOP_K * tm_s + N_EXPERTS + 1) // 2
    (x1_p, xs_p, pos_p, gate_p, cnt_p, off_p, hist_p, _) = _mixer(
        x_prompt, zero_hist, mk_p[:, None], mv_p[:, None], wts_p,
        ns=1, sl=PROMPT_TILE, carry=True, sp_chunk=GMLP_CHUNK, pairs_per_tile=ppt_p)

    gs = bs // SAMPLE_SEQS_PER_TILE
    hist_s_in = jnp.pad(cache_conv[l], ((0, 0), (HIST_OFF, 0), (0, 0))).reshape(
        gs, SAMPLE_SEQS_PER_TILE, HIST, C_CONV)
    mk_s = cache_mem_k[l].reshape(gs, SAMPLE_SEQS_PER_TILE, N_MEM, C_XA)
    mv_s = cache_mem_v[l].reshape(gs, SAMPLE_SEQS_PER_TILE, N_MEM, C_XA)
    (x1_s, xs_s, pos_s, gate_s, cnt_s, off_s, hist_s, v_s) = _mixer(
        x_sample.reshape(gs, tm_s, D_MODEL), hist_s_in, mk_s, mv_s, wts_s,
        ns=SAMPLE_SEQS_PER_TILE, sl=dseq, carry=False, sp_chunk=dseq, pairs_per_tile=ppt_s)

    max_pairs = ((tp + ts) * TOP_K + (ntp + nts) * N_EXPERTS) // 2
    n_blocks = -(-max_pairs // MOE_PAIRS) + N_EXPERTS
    cnt = jnp.concatenate([cnt_p[:, :, 0], cnt_s[:, :, 0]], axis=0)
    off = jnp.concatenate([off_p[:, :, 0], off_s[:, :, 0]], axis=0)
    tile_pair0 = np.concatenate([np.arange(ntp) * ppt_p, np.arange(nts) * ppt_s]).astype(np.int32)
    counts = jnp.sum(cnt, axis=0)
    tile_base = jnp.cumsum(cnt, axis=0) - cnt
    padded = (counts + MOE_PAIRS - 1) // MOE_PAIRS * MOE_PAIRS
    pad_end = jnp.cumsum(padded)
    pad_start = pad_end - padded
    strip_dst = (pad_start[None, :] + tile_base).astype(I32)
    strip_src = (tile_pair0[:, None] + off).astype(I32)
    n_used = (pad_end[-1] // MOE_PAIRS).astype(I32)
    blk_pair0 = jnp.minimum(jnp.arange(n_blocks, dtype=I32), n_used - 1) * MOE_PAIRS
    block_expert = jnp.minimum(
        jnp.sum((pad_end[None, :] <= blk_pair0[:, None]).astype(I32), axis=1), N_EXPERTS - 1)
    of_block = block_expert[:, None] == jnp.arange(N_EXPERTS, dtype=I32)[None, :]
    last_pair = jnp.sum(jnp.where(of_block, (pad_start + counts)[None, :], 0), axis=1)
    block_valid = jnp.clip(last_pair - blk_pair0, 0, MOE_PAIRS).astype(I32)
    def strip_group(t0, t1):
        sdst = strip_dst[t0:t1].T.reshape(-1)
        ssrc = strip_src[t0:t1].T.reshape(-1)
        slen = cnt[t0:t1].T.reshape(-1).astype(I32)
        s_lo = jnp.sum(((sdst + slen)[None, :] <= blk_pair0[:, None]).astype(I32), axis=1)
        s_hi = jnp.sum((sdst[None, :] < (blk_pair0 + MOE_PAIRS)[:, None]).astype(I32), axis=1)
        return (s_lo, s_hi, ssrc, sdst, slen)

    moe_tables = ((pad_start // MOE_PAIRS).astype(I32), (padded // MOE_PAIRS).astype(I32),
                  block_valid, n_used.reshape(1),
                  *strip_group(0, ntp), *strip_group(ntp, ntp + nts))

    ys = _moe(moe_tables, xs_p, xs_s, n_blocks,
              w_gate_up[l], b_gate_up[l][:, None, :], w_down[l], b_down[l][:, None, :])
    gfin = row(g_final)

    def combine_tables(t0, t1):
        return (strip_dst[t0:t1].reshape(-1), cnt[t0:t1].reshape(-1).astype(I32),
                off[t0:t1].reshape(-1).astype(I32), jnp.sum(cnt[t0:t1], axis=1).astype(I32))

    y_p = _combine(combine_tables(0, ntp), x1_p, pos_p, gate_p, gfin, ys, PROMPT_TILE, ppt_p)
    y_s = _combine(combine_tables(ntp, ntp + nts), x1_s, pos_s, gate_s, gfin, ys, tm_s, ppt_s)

    return (y_p.reshape(bp, seq, D_MODEL),
            y_s.reshape(bs, dseq, D_MODEL),
            hist_p[:, 0, HIST_OFF:, :][None],
            mk_p.reshape(bp, N_MEM, XA_HEADS, XA_HD)[None],
            mv_p.reshape(bp, N_MEM, XA_HEADS, XA_HD)[None],
            hist_s.reshape(bs, HIST, C_CONV)[:, HIST_OFF:, :][None],
            v_s.reshape(bs, dseq, C_GMLP)[None])
```

```python
import functools

import numpy as np
import jax
import jax.numpy as jnp
from jax import lax
from jax.experimental import pallas as pl
from jax.experimental.pallas import tpu as pltpu

F32 = jnp.float32
BF16 = jnp.bfloat16
I32 = jnp.int32
U32 = jnp.uint32

D_MODEL = 1024
C_CONV = 384
CONV_W = 31
C_GMLP = 384
GMLP_HEADS = 4
GMLP_HD = 96
GMLP_CHUNK = 128
XA_HEADS = 4
XA_HD = 64
C_XA = 256
N_MEM = 256
N_EXPERTS = 32
TOP_K = 4
D_FF = 1024
SWIGLU_LIMIT = 7.0
SWIGLU_ALPHA = 1.702
EPS = 1e-5
IN_COLS = 2 * C_CONV + 2 * C_GMLP + C_XA

SUBLANES = 8
LANES = 128
ROW_TILES = D_MODEL // LANES
HIST = 32
HIST_OFF = HIST - (CONV_W - 1)

PROMPT_TILE = 512
SAMPLE_SEQS_PER_TILE = 8
MOE_BLOCK = 512
MOE_PAIRS = MOE_BLOCK // 2
GATHER_AHEAD = 2
TILES_PER_STEP = 2
FF_CHUNK = 256
VMEM_LIMIT_MIXER = 48 * 1024 * 1024
VMEM_LIMIT_MOE = 52 * 1024 * 1024


def _rmsnorm(x, g):
    return x * lax.rsqrt(jnp.mean(x * x, axis=-1, keepdims=True) + EPS) * g


def _layernorm(x, g, b):
    mu = jnp.mean(x, axis=-1, keepdims=True)
    xc = x - mu
    var = jnp.mean(xc * xc, axis=-1, keepdims=True)
    return xc * lax.rsqrt(var + EPS) * g + b


def _gelu(x):
    return 0.5 * x * (1.0 + lax.erf(x * np.float32(1.0 / np.sqrt(2.0))))


def _sigmoid(x):
    return 1.0 / (1.0 + jnp.exp(-x))


def _rows_to_tiles(dst_ref, val, rows, row0=0):
    for j in range(ROW_TILES):
        dst_ref[pl.ds(row0 * ROW_TILES + j, rows, stride=ROW_TILES), :] = val[:, j * LANES:(j + 1) * LANES]


def _tiles_to_rows(src_ref, rows, row0=0):
    return jnp.concatenate(
        [src_ref[pl.ds(row0 * ROW_TILES + j, rows, stride=ROW_TILES), :] for j in range(ROW_TILES)],
        axis=-1)


def _memkv_kernel(mem_ref, g_ref, w_ref, o_ref):
    mn = _rmsnorm(mem_ref[0], g_ref[...])
    o_ref[0] = jnp.dot(mn.astype(BF16), w_ref[...], preferred_element_type=F32)


def _memkv(mem, g_mem, w_kv):
    b = mem.shape[0]
    return pl.pallas_call(
        _memkv_kernel,
        grid=(b,),
        in_specs=[pl.BlockSpec((1, N_MEM, D_MODEL), lambda i: (i, 0, 0)),
                  pl.BlockSpec((1, D_MODEL), lambda i: (0, 0)),
                  pl.BlockSpec((D_MODEL, 2 * C_XA), lambda i: (0, 0))],
        out_specs=pl.BlockSpec((1, N_MEM, 2 * C_XA), lambda i: (i, 0, 0)),
        out_shape=jax.ShapeDtypeStruct((b, N_MEM, 2 * C_XA), F32),
        name="memkv",
    )(mem, g_mem, w_kv)


def _mixer_kernel(x_ref, hist_ref, mk_ref, mv_ref, gmix_ref, win_ref, wdw_ref, bdw_ref,
                  lcg_ref, lcb_ref, lvg_ref, lvb_ref, wsp_ref, bsp_ref, wout_ref, gffn_ref,
                  wrh_ref, wrl_ref, br_ref, *rest, ns, sl, carry, sp_chunk, tiles_per_seq, n_sorted):
    (x1_ref, xst_ref, pos_ref, gate_ref, cnt_ref, off_ref, histout_ref, v_ref,
     ext_ref, shift_ref, h2_keep, pos_keep, before_ref, wsp16_ref) = rest
    tm = ns * sl
    step = pl.program_id(0)
    n_tiles = pl.num_programs(0) - 1

    def sort_previous_tile():
        jj = lax.broadcasted_iota(I32, (n_sorted, tm), 0).astype(jnp.int16)
        perm = jnp.zeros((n_sorted, tm), BF16)
        for k in range(TOP_K):
            perm = jnp.where(jj == pos_keep[k:k + 1, :].astype(jnp.int16), jnp.ones((), BF16), perm)
        sorted_rows = jnp.dot(perm, h2_keep[...], preferred_element_type=F32)
        pairs = pltpu.bitcast(sorted_rows.astype(BF16), U32)
        _rows_to_tiles(xst_ref, pairs, n_sorted // 2)

    @pl.when(step == 0)
    def _():
        h2_keep[...] = jnp.zeros_like(h2_keep)
        pos_keep[...] = jnp.full(pos_keep.shape, -1, I32)
        tr = lax.broadcasted_iota(I32, (tm, tm), 0)
        tc = lax.broadcasted_iota(I32, (tm, tm), 1)
        before_ref[...] = jnp.where(tr < tc, 1.0, 0.0).astype(BF16)
        rr = lax.broadcasted_iota(I32, (GMLP_CHUNK, GMLP_CHUNK), 0)
        cc = lax.broadcasted_iota(I32, (GMLP_CHUNK, GMLP_CHUNK), 1)
        sp_mask = (cc <= rr) & ((rr // sp_chunk) == (cc // sp_chunk))
        for hh in range(GMLP_HEADS):
            wsp16_ref[hh] = jnp.where(sp_mask, wsp_ref[hh], 0.0).astype(BF16)

    @pl.when(step < n_tiles)
    def _():
        sort_previous_tile()
        _mixer_tile(x_ref, hist_ref, mk_ref, mv_ref, gmix_ref, win_ref, wdw_ref, bdw_ref,
                    lcg_ref, lcb_ref, lvg_ref, lvb_ref, wsp16_ref, bsp_ref, wout_ref, gffn_ref,
                    wrh_ref, wrl_ref, br_ref, x1_ref, pos_ref, gate_ref, cnt_ref, off_ref,
                    histout_ref, v_ref, ext_ref, shift_ref, h2_keep, pos_keep, before_ref,
                    first_of_seq=(step % tiles_per_seq) == 0,
                    ns=ns, sl=sl, carry=carry)

    @pl.when(step == n_tiles)
    def _():
        sort_previous_tile()


def _mixer_tile(x_ref, hist_ref, mk_ref, mv_ref, gmix_ref, win_ref, wdw_ref, bdw_ref,
                lcg_ref, lcb_ref, lvg_ref, lvb_ref, wsp16_ref, bsp_ref, wout_ref, gffn_ref,
                wrh_ref, wrl_ref, br_ref, x1_ref, pos_ref, gate_ref, cnt_ref, off_ref,
                histout_ref, v_ref, ext_ref, shift_ref, h2_keep, pos_keep, before_ref,
                *, first_of_seq, ns, sl, carry):
    tm = ns * sl
    x = x_ref[0]
    h = _rmsnorm(x, gmix_ref[...])
    z = jnp.dot(h.astype(BF16), win_ref[...], preferred_element_type=F32)
    z_a = z[:, 0:C_CONV]
    z_g = z[:, C_CONV:2 * C_CONV]
    z_u = z[:, 2 * C_CONV:2 * C_CONV + C_GMLP]
    z_v = z[:, 2 * C_CONV + C_GMLP:2 * C_CONV + 2 * C_GMLP]
    z_q = z[:, 2 * C_CONV + 2 * C_GMLP:IN_COLS]

    glu = z_a * _sigmoid(z_g)
    if carry:
        ext_ref[:, 0:HIST, :] = jnp.where(first_of_seq, hist_ref[0], ext_ref[:, 0:HIST, :])
    else:
        ext_ref[:, 0:HIST, :] = hist_ref[0]
    conv_parts = []
    for s in range(ns):
        ext_s = ext_ref.at[s]
        ext_s[HIST:HIST + sl, :] = glu[s * sl:(s + 1) * sl]
        n_shift = HIST + sl - SUBLANES
        for r in range(1, SUBLANES):
            shift_ref[s, r - 1, 0:n_shift, :] = ext_s[pl.ds(r, n_shift), :]
        rc = min(sl, 64)
        for r0 in range(0, sl, rc):
            acc = jnp.broadcast_to(bdw_ref[...], (rc, C_CONV))
            for j in range(CONV_W):
                a, r = divmod(j + HIST_OFF, SUBLANES)
                src = ext_s if r == 0 else shift_ref.at[s, r - 1]
                acc = acc + wdw_ref[j:j + 1, :] * src[pl.ds(r0 + a * SUBLANES, rc), :]
            conv_parts.append(acc)
        new_hist = ext_s[sl:sl + HIST, :]
        histout_ref[0, s] = new_hist
        if carry:
            ext_s[0:HIST, :] = new_hist
    y = jnp.concatenate(conv_parts, axis=0) if len(conv_parts) > 1 else conv_parts[0]
    y = _layernorm(y, lcg_ref[...], lcb_ref[...])
    c_out = y * _sigmoid(y)

    u = _gelu(z_u)
    v = _layernorm(_gelu(z_v), lvg_ref[...], lvb_ref[...])
    v_ref[...] = v
    vb = v.astype(BF16)
    col = lax.broadcasted_iota(I32, (GMLP_CHUNK, C_GMLP), 1)
    w_heads = [wsp16_ref[hh] for hh in range(GMLP_HEADS)]
    g_parts = []
    for c in range(tm // GMLP_CHUNK):
        vc = vb[c * GMLP_CHUNK:(c + 1) * GMLP_CHUNK]
        sg = bsp_ref[...]
        for hh in range(GMLP_HEADS):
            head_cols = (col >= hh * GMLP_HD) & (col < (hh + 1) * GMLP_HD)
            vh = jnp.where(head_cols, vc, jnp.zeros_like(vc))
            sg = sg + jnp.dot(w_heads[hh], vh, preferred_element_type=F32)
        g_parts.append(u[c * GMLP_CHUNK:(c + 1) * GMLP_CHUNK] * sg)
    g_out = jnp.concatenate(g_parts, axis=0) if len(g_parts) > 1 else g_parts[0]

    qs = z_q * np.float32(XA_HD ** -0.5)
    qcol = lax.broadcasted_iota(I32, (sl, C_XA), 1)
    a_parts = []
    for s in range(ns):
        q_s = qs[s * sl:(s + 1) * sl]
        kb = mk_ref[0, s].astype(BF16)
        vvb = mv_ref[0, s].astype(BF16)
        hmasks = [(qcol >= hh * XA_HD) & (qcol < (hh + 1) * XA_HD) for hh in range(XA_HEADS)]
        stack = XA_HEADS if sl * XA_HEADS <= N_MEM else 1
        a_s = jnp.zeros((sl, C_XA), F32)
        for h0 in range(0, XA_HEADS, stack):
            heads = range(h0, h0 + stack)
            qh = jnp.concatenate([jnp.where(hmasks[hh], q_s, 0.0) for hh in heads], axis=0).astype(BF16)
            sc = lax.dot_general(qh, kb, (((1,), (1,)), ((), ())), preferred_element_type=F32)
            p = jnp.exp(sc - jnp.max(sc, axis=-1, keepdims=True))
            den = jnp.sum(p, axis=-1, keepdims=True)
            oh = jnp.dot(p.astype(BF16), vvb, preferred_element_type=F32) / den
            for n, hh in enumerate(heads):
                a_s = a_s + jnp.where(hmasks[hh], oh[n * sl:(n + 1) * sl], 0.0)
        a_parts.append(a_s)
    a_out = jnp.concatenate(a_parts, axis=0) if len(a_parts) > 1 else a_parts[0]

    mix = jnp.concatenate([c_out, g_out, a_out], axis=-1).astype(BF16)
    x1 = x + jnp.dot(mix, wout_ref[...], preferred_element_type=F32)
    x1_ref[...] = x1

    h2 = _rmsnorm(x1, gffn_ref[...])
    h2_hi = h2.astype(BF16)
    h2_lo = (h2 - h2_hi.astype(F32)).astype(BF16)
    nt_dims = (((1,), (1,)), ((), ()))
    lg = (lax.dot_general(wrh_ref[...], h2_hi, nt_dims, preferred_element_type=F32)
          + lax.dot_general(wrl_ref[...], h2_hi, nt_dims, preferred_element_type=F32)
          + lax.dot_general(wrh_ref[...], h2_lo, nt_dims, preferred_element_type=F32)
          + br_ref[...])
    eio = lax.broadcasted_iota(I32, (N_EXPERTS, tm), 0)
    work = lg
    vals, idxs = [], []
    for _ in range(TOP_K):
        m = jnp.max(work, axis=0, keepdims=True)
        idx = jnp.min(jnp.where(work == m, eio, N_EXPERTS), axis=0, keepdims=True)
        vals.append(m)
        idxs.append(idx)
        work = jnp.where(eio == idx, -jnp.inf, work)
    exps = [jnp.exp(vk - vals[0]) for vk in vals]
    den = exps[0] + exps[1] + exps[2] + exps[3]
    gate_ref[...] = jnp.concatenate([ek / den for ek in exps], axis=0)

    sel = jnp.zeros((N_EXPERTS, tm), F32)
    for idx in idxs:
        sel = sel + jnp.where(eio == idx, 1.0, 0.0)
    selb = sel.astype(BF16)
    ranks = jnp.dot(selb, before_ref[...], preferred_element_type=F32)
    er = lax.broadcasted_iota(I32, (N_EXPERTS, N_EXPERTS), 0)
    ec = lax.broadcasted_iota(I32, (N_EXPERTS, N_EXPERTS), 1)
    lower = jnp.where(ec < er, 1.0, 0.0).astype(BF16)
    cnt = jnp.sum(sel, axis=1, keepdims=True)
    half_len = jnp.floor((cnt + 1.0) * 0.5)
    off_pairs = jnp.dot(lower, jnp.broadcast_to(half_len, (N_EXPERTS, LANES)).astype(BF16),
                        preferred_element_type=F32)[:, 0:1]
    cnt_ref[0] = jnp.broadcast_to(half_len, (N_EXPERTS, LANES)).astype(I32)
    off_ref[0] = jnp.broadcast_to(off_pairs, (N_EXPERTS, LANES)).astype(I32)
    slot_of = 2.0 * off_pairs + ranks
    pos = [jnp.sum(jnp.where(eio == idx, slot_of, 0.0), axis=0, keepdims=True).astype(I32) for idx in idxs]
    pos_all = jnp.concatenate(pos, axis=0)
    pos_ref[...] = pos_all
    pos_keep[...] = pos_all
    h2_keep[...] = h2_hi


def _mixer(x, hist, mem_k, mem_v, wts, *, ns, sl, carry, sp_chunk, pairs_per_tile):
    g, r, _ = x.shape
    tm = ns * sl
    nt = r // tm
    ntot = g * nt
    assert 2 * pairs_per_tile >= TOP_K * tm + N_EXPERTS
    tile = lambda s: jnp.minimum(s, ntot - 1)
    const2 = lambda s: (0, 0)
    const3 = lambda s: (0, 0, 0)
    tile_row = lambda s: (tile(s), 0)
    tile_lane = lambda s: (0, tile(s))
    per_seq = lambda s: (tile(s) // nt, 0, 0, 0)
    in_specs = [
        pl.BlockSpec((1, tm, D_MODEL), lambda s: (tile(s) // nt, tile(s) % nt, 0)),
        pl.BlockSpec((1, ns, HIST, C_CONV), per_seq),
        pl.BlockSpec((1, ns, N_MEM, C_XA), per_seq),
        pl.BlockSpec((1, ns, N_MEM, C_XA), per_seq),
        pl.BlockSpec((1, D_MODEL), const2),
        pl.BlockSpec((D_MODEL, IN_COLS), const2),
        pl.BlockSpec((HIST, C_CONV), const2),
        pl.BlockSpec((1, C_CONV), const2),
        pl.BlockSpec((1, C_CONV), const2),
        pl.BlockSpec((1, C_CONV), const2),
        pl.BlockSpec((1, C_GMLP), const2),
        pl.BlockSpec((1, C_GMLP), const2),
        pl.BlockSpec((GMLP_HEADS, GMLP_CHUNK, GMLP_CHUNK), const3),
        pl.BlockSpec((GMLP_CHUNK, C_GMLP), const2),
        pl.BlockSpec((D_MODEL, D_MODEL), const2),
        pl.BlockSpec((1, D_MODEL), const2),
        pl.BlockSpec((N_EXPERTS, D_MODEL), const2),
        pl.BlockSpec((N_EXPERTS, D_MODEL), const2),
        pl.BlockSpec((N_EXPERTS, 1), const2),
    ]
    tile_cnt = lambda s: (tile(s), 0, 0)
    out_specs = [
        pl.BlockSpec((tm, D_MODEL), tile_row),
        pl.BlockSpec((pairs_per_tile * ROW_TILES, LANES),
                     lambda s: (jnp.maximum(s - 1, 0), 0)),
        pl.BlockSpec((TOP_K, tm), tile_lane),
        pl.BlockSpec((TOP_K, tm), tile_lane),
        pl.BlockSpec((1, N_EXPERTS, LANES), tile_cnt),
        pl.BlockSpec((1, N_EXPERTS, LANES), tile_cnt),
        pl.BlockSpec((1, ns, HIST, C_CONV), per_seq),
        pl.BlockSpec((tm, C_GMLP), tile_row),
    ]
    rows = g * r
    out_shape = [
        jax.ShapeDtypeStruct((rows, D_MODEL), F32),
        jax.ShapeDtypeStruct((ntot * pairs_per_tile * ROW_TILES, LANES), U32),
        jax.ShapeDtypeStruct((TOP_K, rows), I32),
        jax.ShapeDtypeStruct((TOP_K, rows), F32),
        jax.ShapeDtypeStruct((ntot, N_EXPERTS, LANES), I32),
        jax.ShapeDtypeStruct((ntot, N_EXPERTS, LANES), I32),
        jax.ShapeDtypeStruct((g, ns, HIST, C_CONV), F32),
        jax.ShapeDtypeStruct((rows, C_GMLP), F32),
    ]
    kern = functools.partial(_mixer_kernel, ns=ns, sl=sl, carry=carry, sp_chunk=sp_chunk,
                             tiles_per_seq=nt, n_sorted=2 * pairs_per_tile)
    return pl.pallas_call(
        kern,
        grid=(ntot + 1,),
        in_specs=in_specs,
        out_specs=out_specs,
        out_shape=out_shape,
        scratch_shapes=[pltpu.VMEM((ns, HIST + sl, C_CONV), F32),
                        pltpu.VMEM((ns, SUBLANES - 1, HIST + sl, C_CONV), F32),
                        pltpu.VMEM((tm, D_MODEL), BF16),
                        pltpu.VMEM((TOP_K, tm), I32),
                        pltpu.VMEM((tm, tm), BF16),
                        pltpu.VMEM((GMLP_HEADS, GMLP_CHUNK, GMLP_CHUNK), BF16)],
        compiler_params=pltpu.CompilerParams(
            dimension_semantics=("arbitrary",),
            vmem_limit_bytes=VMEM_LIMIT_MIXER),
        name="mixer_carry" if carry else "mixer_cache",
    )(x, hist, mem_k, mem_v, *wts)


def _strip_pieces(n, max_rows, fn):
    done = 0
    p = max_rows
    while p >= 1:
        has = (n & p) != 0
        pl.when(has)(functools.partial(fn, done, p))
        done = done + jnp.where(has, p, 0)
        p //= 2


class _Table:
    def __init__(self, ref, offset):
        self.ref, self.offset = ref, offset

    def __getitem__(self, i):
        return self.ref[self.offset + i]


def _pack_tables(tables):
    offsets = tuple(int(o) for o in np.cumsum([0] + [t.shape[0] for t in tables[:-1]]))
    return jnp.concatenate([t.astype(I32) for t in tables]), offsets


def _pow2_at_most(n):
    return 1 << (n.bit_length() - 1)


def _rows(ref, first_row, rows):
    return ref.at[pl.ds(pl.multiple_of(first_row * ROW_TILES, ROW_TILES), rows * ROW_TILES)]


def _moe_kernel(tab_ref, xs_hbm, xs2_hbm, wgu_ref, bgu_ref, wdn_ref, bdn_ref, ys_hbm,
                xbuf, ybuf, zbuf, wgu16, wdn16, sems, ysems, zsem, *, offsets, n_blocks):
    tabs = [_Table(tab_ref, o) for o in offsets]
    b0_ref, nbk_ref, bv_ref, nb_ref = tabs[:4]
    strips_a, strips_b = tabs[4:9], tabs[9:14]
    e = pl.program_id(0)
    n_used = nb_ref[0]
    half = MOE_PAIRS // 2

    def gather_start(b, slot):
        pair0 = b * MOE_PAIRS

        for src_hbm, (slo_ref, shi_ref, ssrc_ref, sdst_ref, slen_ref) in ((xs_hbm, strips_a), (xs2_hbm, strips_b)):
            def strip(s, carry, src_hbm=src_hbm, ssrc_ref=ssrc_ref, sdst_ref=sdst_ref, slen_ref=slen_ref):
                lo = jnp.maximum(sdst_ref[s], pair0)
                hi = jnp.minimum(sdst_ref[s] + slen_ref[s], pair0 + MOE_PAIRS)
                src0 = ssrc_ref[s] + (lo - sdst_ref[s])
                dst0 = lo - pair0

                def piece(first, rows):
                    pltpu.make_async_copy(_rows(src_hbm, src0 + first, rows),
                                          _rows(xbuf.at[slot], dst0 + first, rows),
                                          sems.at[slot]).start(priority=rows.bit_length() % 2)
                _strip_pieces(hi - lo, MOE_PAIRS, piece)
                return carry
            lax.fori_loop(slo_ref[b], shi_ref[b], strip, 0)

    def gather_wait(b, slot):
        def piece(first, rows):
            del first
            pltpu.make_async_copy(_rows(xs_hbm, 0, rows), _rows(xbuf.at[slot], 0, rows), sems.at[slot]).wait()
        _strip_pieces(bv_ref[b], MOE_PAIRS, piece)

    def out_copy(b, slot):
        return pltpu.make_async_copy(ybuf.at[slot], _rows(ys_hbm, b * MOE_PAIRS, MOE_PAIRS), ysems.at[slot])

    def expert_mlp(xslot, slot, h):
        xb = pltpu.bitcast(_tiles_to_rows(xbuf.at[xslot], half, h * half), BF16)
        acts = []
        for c0 in range(0, D_FF, FF_CHUNK):
            g_cols = pl.ds(c0, FF_CHUNK)
            u_cols = pl.ds(D_FF + c0, FF_CHUNK)
            gate = jnp.dot(xb, wgu16[:, g_cols], preferred_element_type=F32) + bgu_ref[:, g_cols]
            up = jnp.dot(xb, wgu16[:, u_cols], preferred_element_type=F32) + bgu_ref[:, u_cols]
            gate = jnp.minimum(gate, SWIGLU_LIMIT)
            up = jnp.clip(up, -SWIGLU_LIMIT, SWIGLU_LIMIT)
            acts.append(((up + 1.0) * (gate * _sigmoid(SWIGLU_ALPHA * gate))).astype(BF16))
        act = jnp.concatenate(acts, axis=-1)
        yb = jnp.dot(act, wdn16[...], preferred_element_type=F32) + bdn_ref[0]
        _rows_to_tiles(ybuf.at[slot], pltpu.bitcast(yb.astype(BF16), U32), half, h * half)

    def fill_copy(b):
        return pltpu.make_async_copy(zbuf, _rows(ys_hbm, b * MOE_PAIRS, MOE_PAIRS), zsem)

    @pl.when(e == 0)
    def _():
        zbuf[...] = jnp.zeros_like(zbuf)

        def start_fill(b, carry):
            fill_copy(b).start()
            return carry
        lax.fori_loop(n_used, n_blocks, start_fill, 0)

        xbuf[...] = jnp.zeros_like(xbuf)
        for a in range(GATHER_AHEAD):
            @pl.when(a < n_used)
            def _():
                gather_start(a, a)

    @pl.when(nbk_ref[e] > 0)
    def _():
        def cast_rows(c, carry):
            r = pl.multiple_of(c * LANES, LANES)
            wgu16[pl.ds(r, LANES), :] = wgu_ref[pl.ds(r, LANES), :].astype(BF16)
            wdn16[pl.ds(r, LANES), :] = wdn_ref[pl.ds(r, LANES), :].astype(BF16)
            return carry
        lax.fori_loop(0, D_MODEL // LANES, cast_rows, 0)

        def block(j, carry):
            b = b0_ref[e] + j
            slot = b % 2
            xslot = b % (GATHER_AHEAD + 1)

            @pl.when(b + GATHER_AHEAD < n_used)
            def _():
                gather_start(b + GATHER_AHEAD, (b + GATHER_AHEAD) % (GATHER_AHEAD + 1))

            gather_wait(b, xslot)

            @pl.when(b >= 2)
            def _():
                out_copy(b - 2, slot).wait()

            valid = bv_ref[b]

            @pl.when(valid > half)
            def _():
                expert_mlp(xslot, slot, 0)
                expert_mlp(xslot, slot, 1)

            @pl.when(valid <= half)
            def _():
                expert_mlp(xslot, slot, 0)
                ybuf[slot, pl.ds(half * ROW_TILES, half * ROW_TILES), :] = jnp.zeros(
                    (half * ROW_TILES, LANES), U32)

            out_copy(b, slot).start()
            return carry
        lax.fori_loop(0, nbk_ref[e], block, 0)

    @pl.when(e == pl.num_programs(0) - 1)
    def _():
        for back in (1, 2):
            @pl.when(n_used >= back)
            def _():
                out_copy(n_used - back, (n_used - back) % 2).wait()

        def wait_fill(b, carry):
            fill_copy(b).wait()
            return carry
        lax.fori_loop(n_used, n_blocks, wait_fill, 0)


def _moe(tables, xs, xs2, n_blocks, w_gu, b_gu, w_dn, b_dn):
    wsel = lambda e, *_: (e, 0, 0)
    packed, offsets = _pack_tables(tables)
    slots = n_blocks * MOE_PAIRS
    grid_spec = pltpu.PrefetchScalarGridSpec(
        num_scalar_prefetch=1,
        grid=(N_EXPERTS,),
        in_specs=[pl.BlockSpec(memory_space=pl.ANY),
                  pl.BlockSpec(memory_space=pl.ANY),
                  pl.BlockSpec((None, D_MODEL, 2 * D_FF), wsel),
                  pl.BlockSpec((None, 1, 2 * D_FF), wsel),
                  pl.BlockSpec((None, D_FF, D_MODEL), wsel),
                  pl.BlockSpec((None, 1, D_MODEL), wsel)],
        out_specs=pl.BlockSpec(memory_space=pl.ANY),
        scratch_shapes=[pltpu.VMEM((GATHER_AHEAD + 1, MOE_PAIRS * ROW_TILES, LANES), U32),
                        pltpu.VMEM((2, MOE_PAIRS * ROW_TILES, LANES), U32),
                        pltpu.VMEM((MOE_PAIRS * ROW_TILES, LANES), U32),
                        pltpu.VMEM((D_MODEL, 2 * D_FF), BF16),
                        pltpu.VMEM((D_FF, D_MODEL), BF16),
                        pltpu.SemaphoreType.DMA((GATHER_AHEAD + 1,)),
                        pltpu.SemaphoreType.DMA((2,)),
                        pltpu.SemaphoreType.DMA(())],
    )
    return pl.pallas_call(
        functools.partial(_moe_kernel, offsets=offsets, n_blocks=n_blocks),
        grid_spec=grid_spec,
        out_shape=jax.ShapeDtypeStruct((slots * ROW_TILES, LANES), U32),
        compiler_params=pltpu.CompilerParams(
            dimension_semantics=("arbitrary",),
            vmem_limit_bytes=VMEM_LIMIT_MOE),
        name="moe",
    )(packed, xs, xs2, w_gu, b_gu, w_dn, b_dn)


def _combine_kernel(tab_ref, x1_ref, pos_ref, gate_ref, gfin_ref, ys_hbm,
                    out_ref, ybuf, sems, *, tm, pairs_per_tile, offsets):
    csrc_ref, clen_ref, coff_ref, ctot_ref = [_Table(tab_ref, o) for o in offsets]
    i = pl.program_id(0)
    n_steps = pl.num_programs(0)
    n_sorted = 2 * pairs_per_tile
    n_slots = 2 * TILES_PER_STEP

    def gather_start(t, slot):
        def strip(e, carry):
            s = t * N_EXPERTS + e
            src0 = csrc_ref[s]
            dst0 = coff_ref[s]

            def piece(first, rows):
                pltpu.make_async_copy(_rows(ys_hbm, src0 + first, rows),
                                      _rows(ybuf.at[slot], dst0 + first, rows),
                                      sems.at[slot]).start(priority=rows.bit_length() % 2)
            _strip_pieces(clen_ref[s], tm // 2, piece)
            return carry
        lax.fori_loop(0, N_EXPERTS, strip, 0)

    @pl.when(i == 0)
    def _():
        ybuf[...] = jnp.zeros_like(ybuf)
        for h in range(TILES_PER_STEP):
            gather_start(h, h)

    @pl.when(i + 1 < n_steps)
    def _():
        for h in range(TILES_PER_STEP):
            t = (i + 1) * TILES_PER_STEP + h
            gather_start(t, t % n_slots)

    slots = [(i * TILES_PER_STEP + h) % n_slots for h in range(TILES_PER_STEP)]
    for h in range(TILES_PER_STEP):
        def wait_piece(first, rows, slot=slots[h]):
            del first
            pltpu.make_async_copy(_rows(ys_hbm, 0, rows), _rows(ybuf.at[slot], 0, rows), sems.at[slot]).wait()
        _strip_pieces(ctot_ref[i * TILES_PER_STEP + h], _pow2_at_most(pairs_per_tile), wait_piece)

    jt = lax.broadcasted_iota(I32, (n_sorted, tm), 0).astype(jnp.int16)
    for h in range(TILES_PER_STEP):
        tok = slice(h * tm, (h + 1) * tm)
        y_sorted = pltpu.bitcast(_tiles_to_rows(ybuf.at[slots[h]], pairs_per_tile), BF16)
        unsort_t = jnp.zeros((n_sorted, tm), BF16)
        for k in range(TOP_K):
            unsort_t = jnp.where(jt == pos_ref[k:k + 1, tok].astype(jnp.int16),
                                 gate_ref[k:k + 1, tok].astype(BF16), unsort_t)
        acc = x1_ref[tok, :] + lax.dot_general(unsort_t, y_sorted, (((0,), (0,)), ((), ())),
                                               preferred_element_type=F32)
        out_ref[tok, :] = _rmsnorm(acc, gfin_ref[...])


def _combine(tables, x1, pos, gates, g_final, ys, tm, pairs_per_tile):
    t = x1.shape[0]
    nt = t // tm
    assert nt % TILES_PER_STEP == 0
    packed, offsets = _pack_tables(tables)
    grid_spec = pltpu.PrefetchScalarGridSpec(
        num_scalar_prefetch=1,
        grid=(nt // TILES_PER_STEP,),
        in_specs=[pl.BlockSpec((TILES_PER_STEP * tm, D_MODEL), lambda i, *_: (i, 0)),
                  pl.BlockSpec((TOP_K, TILES_PER_STEP * tm), lambda i, *_: (0, i)),
                  pl.BlockSpec((TOP_K, TILES_PER_STEP * tm), lambda i, *_: (0, i)),
                  pl.BlockSpec((1, D_MODEL), lambda i, *_: (0, 0)),
                  pl.BlockSpec(memory_space=pl.ANY)],
        out_specs=pl.BlockSpec((TILES_PER_STEP * tm, D_MODEL), lambda i, *_: (i, 0)),
        scratch_shapes=[pltpu.VMEM((2 * TILES_PER_STEP, pairs_per_tile * ROW_TILES, LANES), U32),
                        pltpu.SemaphoreType.DMA((2 * TILES_PER_STEP,))],
    )
    return pl.pallas_call(
        functools.partial(_combine_kernel, tm=tm, pairs_per_tile=pairs_per_tile, offsets=offsets),
        grid_spec=grid_spec,
        out_shape=jax.ShapeDtypeStruct((t, D_MODEL), F32),
        compiler_params=pltpu.CompilerParams(
            dimension_semantics=("arbitrary",),
            vmem_limit_bytes=VMEM_LIMIT_MIXER),
        name="combine",
    )(packed, x1, pos, gates, g_final, ys)


def _split_bf16(w):
    hi = w.astype(BF16)
    lo = (w - hi.astype(F32)).astype(BF16)
    return hi, lo


def kernel(x_prompt, x_sample, cache_conv, cache_mem_k, cache_mem_v, mem_prompt, g_mix, w_in, w_dw, b_dw, ln_conv_g, ln_conv_b, ln_v_g, ln_v_b, w_spatial, b_spatial, g_mem, w_mem_k, w_mem_v, w_out, g_ffn, w_router, b_router, w_gate_up, b_gate_up, w_down, b_down, g_final):
    depth = g_mix.shape[0]
    assert depth == 1
    l = 0
    bp, seq, _ = x_prompt.shape
    bs, dseq, _ = x_sample.shape
    assert seq % PROMPT_TILE == 0 and bs % SAMPLE_SEQS_PER_TILE == 0
    assert GMLP_CHUNK % dseq == 0 and (SAMPLE_SEQS_PER_TILE * dseq) % GMLP_CHUNK == 0

    row = lambda a: a.reshape(1, -1)
    wr_hi, wr_lo = _split_bf16(w_router[l].T)
    w_dw_pad = jnp.pad(w_dw[l], ((0, HIST - CONV_W), (0, 0)))
    bias_rows = lambda b: jnp.repeat(b.T, GMLP_HD, axis=1)
    common = dict(
        gmix=row(g_mix[l]), win=w_in[l].astype(BF16), wdw=w_dw_pad, bdw=row(b_dw[l]),
        lcg=row(ln_conv_g[l]), lcb=row(ln_conv_b[l]), lvg=row(ln_v_g[l]), lvb=row(ln_v_b[l]),
        wout=w_out[l].astype(BF16), gffn=row(g_ffn[l]), wrh=wr_hi, wrl=wr_lo,
        br=b_router[l].reshape(N_EXPERTS, 1))

    def weights(wsp, bsp):
        c = common
        return (c["gmix"], c["win"], c["wdw"], c["bdw"], c["lcg"], c["lcb"], c["lvg"], c["lvb"],
                wsp, bsp, c["wout"], c["gffn"], c["wrh"], c["wrl"], c["br"])

    reps = GMLP_CHUNK // dseq
    wts_p = weights(w_spatial[l], bias_rows(b_spatial[l]))
    wts_s = weights(jnp.tile(w_spatial[l][:, :dseq, :dseq], (1, reps, reps)),
                    bias_rows(jnp.tile(b_spatial[l][:, :dseq], (1, reps))))

    w_kv = jnp.concatenate([w_mem_k[l], w_mem_v[l]], axis=1).astype(BF16)
    kv_p = _memkv(mem_prompt, row(g_mem[l]), w_kv)
    mk_p = kv_p[:, :, :C_XA]
    mv_p = kv_p[:, :, C_XA:]
    zero_hist = jnp.zeros((bp, 1, HIST, C_CONV), F32)
    tp, ts = bp * seq, bs * dseq
    tm_s = SAMPLE_SEQS_PER_TILE * dseq
    ntp, nts = tp // PROMPT_TILE, ts // tm_s
    ppt_p = (TOP_K * PROMPT_TILE + N_EXPERTS + 1) // 2
    ppt_s = (TOP_K * tm_s + N_EXPERTS + 1) // 2
    (x1_p, xs_p, pos_p, gate_p, cnt_p, off_p, hist_p, _) = _mixer(
        x_prompt, zero_hist, mk_p[:, None], mv_p[:, None], wts_p,
        ns=1, sl=PROMPT_TILE, carry=True, sp_chunk=GMLP_CHUNK, pairs_per_tile=ppt_p)

    gs = bs // SAMPLE_SEQS_PER_TILE
    hist_s_in = jnp.pad(cache_conv[l], ((0, 0), (HIST_OFF, 0), (0, 0))).reshape(
        gs, SAMPLE_SEQS_PER_TILE, HIST, C_CONV)
    mk_s = cache_mem_k[l].reshape(gs, SAMPLE_SEQS_PER_TILE, N_MEM, C_XA)
    mv_s = cache_mem_v[l].reshape(gs, SAMPLE_SEQS_PER_TILE, N_MEM, C_XA)
    (x1_s, xs_s, pos_s, gate_s, cnt_s, off_s, hist_s, v_s) = _mixer(
        x_sample.reshape(gs, tm_s, D_MODEL), hist_s_in, mk_s, mv_s, wts_s,
        ns=SAMPLE_SEQS_PER_TILE, sl=dseq, carry=False, sp_chunk=dseq, pairs_per_tile=ppt_s)

    max_pairs = ((tp + ts) * TOP_K + (ntp + nts) * N_EXPERTS) // 2
    n_blocks = -(-max_pairs // MOE_PAIRS) + N_EXPERTS
    cnt = jnp.concatenate([cnt_p[:, :, 0], cnt_s[:, :, 0]], axis=0)
    off = jnp.concatenate([off_p[:, :, 0], off_s[:, :, 0]], axis=0)
    tile_pair0 = np.concatenate([np.arange(ntp) * ppt_p, np.arange(nts) * ppt_s]).astype(np.int32)
    counts = jnp.sum(cnt, axis=0)
    tile_base = jnp.cumsum(cnt, axis=0) - cnt
    padded = (counts + MOE_PAIRS - 1) // MOE_PAIRS * MOE_PAIRS
    pad_end = jnp.cumsum(padded)
    pad_start = pad_end - padded
    strip_dst = (pad_start[None, :] + tile_base).astype(I32)
    strip_src = (tile_pair0[:, None] + off).astype(I32)
    n_used = (pad_end[-1] // MOE_PAIRS).astype(I32)
    blk_pair0 = jnp.minimum(jnp.arange(n_blocks, dtype=I32), n_used - 1) * MOE_PAIRS
    block_expert = jnp.minimum(
        jnp.sum((pad_end[None, :] <= blk_pair0[:, None]).astype(I32), axis=1), N_EXPERTS - 1)
    of_block = block_expert[:, None] == jnp.arange(N_EXPERTS, dtype=I32)[None, :]
    last_pair = jnp.sum(jnp.where(of_block, (pad_start + counts)[None, :], 0), axis=1)
    block_valid = jnp.clip(last_pair - blk_pair0, 0, MOE_PAIRS).astype(I32)
    def strip_group(t0, t1):
        sdst = strip_dst[t0:t1].T.reshape(-1)
        ssrc = strip_src[t0:t1].T.reshape(-1)
        slen = cnt[t0:t1].T.reshape(-1).astype(I32)
        s_lo = jnp.sum(((sdst + slen)[None, :] <= blk_pair0[:, None]).astype(I32), axis=1)
        s_hi = jnp.sum((sdst[None, :] < (blk_pair0 + MOE_PAIRS)[:, None]).astype(I32), axis=1)
        return (s_lo, s_hi, ssrc, sdst, slen)

    moe_tables = ((pad_start // MOE_PAIRS).astype(I32), (padded // MOE_PAIRS).astype(I32),
                  block_valid, n_used.reshape(1),
                  *strip_group(0, ntp), *strip_group(ntp, ntp + nts))

    ys = _moe(moe_tables, xs_p, xs_s, n_blocks,
              w_gate_up[l], b_gate_up[l][:, None, :], w_down[l], b_down[l][:, None, :])
    gfin = row(g_final)

    def combine_tables(t0, t1):
        return (strip_dst[t0:t1].reshape(-1), cnt[t0:t1].reshape(-1).astype(I32),
                off[t0:t1].reshape(-1).astype(I32), jnp.sum(cnt[t0:t1], axis=1).astype(I32))

    y_p = _combine(combine_tables(0, ntp), x1_p, pos_p, gate_p, gfin, ys, PROMPT_TILE, ppt_p)
    y_s = _combine(combine_tables(ntp, ntp + nts), x1_s, pos_s, gate_s, gfin, ys, tm_s, ppt_s)

    return (y_p.reshape(bp, seq, D_MODEL),
            y_s.reshape(bs, dseq, D_MODEL),
            hist_p[:, 0, HIST_OFF:, :][None],
            mk_p.reshape(bp, N_MEM, XA_HEADS, XA_HD)[None],
            mv_p.reshape(bp, N_MEM, XA_HEADS, XA_HD)[None],
            hist_s.reshape(bs, HIST, C_CONV)[:, HIST_OFF:, :][None],
            v_s.reshape(bs, dseq, C_GMLP)[None])
```

```python
import functools

import numpy as np
import jax
import jax.numpy as jnp
from jax import lax
from jax.experimental import pallas as pl
from jax.experimental.pallas import tpu as pltpu

F32 = jnp.float32
BF16 = jnp.bfloat16
I32 = jnp.int32
U32 = jnp.uint32

D_MODEL = 1024
C_CONV = 384
CONV_W = 31
C_GMLP = 384
GMLP_HEADS = 4
GMLP_HD = 96
GMLP_CHUNK = 128
XA_HEADS = 4
XA_HD = 64
C_XA = 256
N_MEM = 256
N_EXPERTS = 32
TOP_K = 4
D_FF = 1024
SWIGLU_LIMIT = 7.0
SWIGLU_ALPHA = 1.702
EPS = 1e-5
IN_COLS = 2 * C_CONV + 2 * C_GMLP + C_XA

SUBLANES = 8
LANES = 128
ROW_TILES = D_MODEL // LANES
HIST = 32
HIST_OFF = HIST - (CONV_W - 1)

PROMPT_TILE = 512
SAMPLE_SEQS_PER_TILE = 16
MOE_BLOCK = 512
MOE_PAIRS = MOE_BLOCK // 2
GATHER_AHEAD = 2
TILES_PER_STEP = 2
VMEM_LIMIT_MIXER = 48 * 1024 * 1024
VMEM_LIMIT_MOE = 52 * 1024 * 1024


def _rmsnorm(x, g):
    return x * lax.rsqrt(jnp.mean(x * x, axis=-1, keepdims=True) + EPS) * g


def _layernorm(x, g, b):
    mu = jnp.mean(x, axis=-1, keepdims=True)
    xc = x - mu
    var = jnp.mean(xc * xc, axis=-1, keepdims=True)
    return xc * lax.rsqrt(var + EPS) * g + b


def _gelu(x):
    return 0.5 * x * (1.0 + lax.erf(x * np.float32(1.0 / np.sqrt(2.0))))


def _sigmoid(x):
    return 1.0 / (1.0 + jnp.exp(-x))


def _rows_to_tiles(dst_ref, val, rows, row0=0):
    for j in range(ROW_TILES):
        dst_ref[pl.ds(row0 * ROW_TILES + j, rows, stride=ROW_TILES), :] = val[:, j * LANES:(j + 1) * LANES]


def _tiles_to_rows(src_ref, rows, row0=0):
    return jnp.concatenate(
        [src_ref[pl.ds(row0 * ROW_TILES + j, rows, stride=ROW_TILES), :] for j in range(ROW_TILES)],
        axis=-1)


def _memkv_kernel(mem_ref, g_ref, w_ref, o_ref):
    mn = _rmsnorm(mem_ref[0], g_ref[...])
    o_ref[0] = jnp.dot(mn.astype(BF16), w_ref[...], preferred_element_type=F32)


def _memkv(mem, g_mem, w_kv):
    b = mem.shape[0]
    return pl.pallas_call(
        _memkv_kernel,
        grid=(b,),
        in_specs=[pl.BlockSpec((1, N_MEM, D_MODEL), lambda i: (i, 0, 0)),
                  pl.BlockSpec((1, D_MODEL), lambda i: (0, 0)),
                  pl.BlockSpec((D_MODEL, 2 * C_XA), lambda i: (0, 0))],
        out_specs=pl.BlockSpec((1, N_MEM, 2 * C_XA), lambda i: (i, 0, 0)),
        out_shape=jax.ShapeDtypeStruct((b, N_MEM, 2 * C_XA), F32),
        name="memkv",
    )(mem, g_mem, w_kv)


def _mixer_kernel(x_ref, hist_ref, mk_ref, mv_ref, gmix_ref, win_ref, wdw_ref, bdw_ref,
                  lcg_ref, lcb_ref, lvg_ref, lvb_ref, wsp_ref, bsp_ref, wout_ref, gffn_ref,
                  wrh_ref, wrl_ref, br_ref, *rest, ns, sl, carry, sp_chunk, tiles_per_seq, n_sorted):
    (x1_ref, xst_ref, pos_ref, gate_ref, cnt_ref, off_ref, histout_ref, v_ref,
     ext_ref, shift_ref, h2_keep, pos_keep, before_ref, wsp16_ref) = rest
    tm = ns * sl
    step = pl.program_id(0)
    n_tiles = pl.num_programs(0) - 1

    def sort_previous_tile():
        jj = lax.broadcasted_iota(I32, (n_sorted, tm), 0).astype(jnp.int16)
        perm = jnp.zeros((n_sorted, tm), BF16)
        for k in range(TOP_K):
            perm = jnp.where(jj == pos_keep[k:k + 1, :].astype(jnp.int16), jnp.ones((), BF16), perm)
        sorted_rows = jnp.dot(perm, h2_keep[...], preferred_element_type=F32)
        pairs = pltpu.bitcast(sorted_rows.astype(BF16), U32)
        _rows_to_tiles(xst_ref, pairs, n_sorted // 2)

    @pl.when(step == 0)
    def _():
        h2_keep[...] = jnp.zeros_like(h2_keep)
        pos_keep[...] = jnp.full(pos_keep.shape, -1, I32)
        tr = lax.broadcasted_iota(I32, (tm, tm), 0)
        tc = lax.broadcasted_iota(I32, (tm, tm), 1)
        before_ref[...] = jnp.where(tr < tc, 1.0, 0.0).astype(BF16)
        rr = lax.broadcasted_iota(I32, (GMLP_CHUNK, GMLP_CHUNK), 0)
        cc = lax.broadcasted_iota(I32, (GMLP_CHUNK, GMLP_CHUNK), 1)
        sp_mask = (cc <= rr) & ((rr // sp_chunk) == (cc // sp_chunk))
        for hh in range(GMLP_HEADS):
            wsp16_ref[hh] = jnp.where(sp_mask, wsp_ref[hh], 0.0).astype(BF16)

    @pl.when(step < n_tiles)
    def _():
        sort_previous_tile()
        _mixer_tile(x_ref, hist_ref, mk_ref, mv_ref, gmix_ref, win_ref, wdw_ref, bdw_ref,
                    lcg_ref, lcb_ref, lvg_ref, lvb_ref, wsp16_ref, bsp_ref, wout_ref, gffn_ref,
                    wrh_ref, wrl_ref, br_ref, x1_ref, pos_ref, gate_ref, cnt_ref, off_ref,
                    histout_ref, v_ref, ext_ref, shift_ref, h2_keep, pos_keep, before_ref,
                    first_of_seq=(step % tiles_per_seq) == 0,
                    ns=ns, sl=sl, carry=carry)

    @pl.when(step == n_tiles)
    def _():
        sort_previous_tile()


def _mixer_tile(x_ref, hist_ref, mk_ref, mv_ref, gmix_ref, win_ref, wdw_ref, bdw_ref,
                lcg_ref, lcb_ref, lvg_ref, lvb_ref, wsp16_ref, bsp_ref, wout_ref, gffn_ref,
                wrh_ref, wrl_ref, br_ref, x1_ref, pos_ref, gate_ref, cnt_ref, off_ref,
                histout_ref, v_ref, ext_ref, shift_ref, h2_keep, pos_keep, before_ref,
                *, first_of_seq, ns, sl, carry):
    tm = ns * sl
    x = x_ref[0]
    h = _rmsnorm(x, gmix_ref[...])
    z = jnp.dot(h.astype(BF16), win_ref[...], preferred_element_type=F32)
    z_a = z[:, 0:C_CONV]
    z_g = z[:, C_CONV:2 * C_CONV]
    z_u = z[:, 2 * C_CONV:2 * C_CONV + C_GMLP]
    z_v = z[:, 2 * C_CONV + C_GMLP:2 * C_CONV + 2 * C_GMLP]
    z_q = z[:, 2 * C_CONV + 2 * C_GMLP:IN_COLS]

    glu = z_a * _sigmoid(z_g)
    if carry:
        ext_ref[:, 0:HIST, :] = jnp.where(first_of_seq, hist_ref[0], ext_ref[:, 0:HIST, :])
    else:
        ext_ref[:, 0:HIST, :] = hist_ref[0]
    conv_parts = []
    for s in range(ns):
        ext_s = ext_ref.at[s]
        ext_s[HIST:HIST + sl, :] = glu[s * sl:(s + 1) * sl]
        n_shift = HIST + sl - SUBLANES
        for r in range(1, SUBLANES):
            shift_ref[s, r - 1, 0:n_shift, :] = ext_s[pl.ds(r, n_shift), :]
        rc = min(sl, 64)
        for r0 in range(0, sl, rc):
            acc = jnp.broadcast_to(bdw_ref[...], (rc, C_CONV))
            for j in range(CONV_W):
                a, r = divmod(j + HIST_OFF, SUBLANES)
                src = ext_s if r == 0 else shift_ref.at[s, r - 1]
                acc = acc + wdw_ref[j:j + 1, :] * src[pl.ds(r0 + a * SUBLANES, rc), :]
            conv_parts.append(acc)
        new_hist = ext_s[sl:sl + HIST, :]
        histout_ref[0, s] = new_hist
        if carry:
            ext_s[0:HIST, :] = new_hist
    y = jnp.concatenate(conv_parts, axis=0) if len(conv_parts) > 1 else conv_parts[0]
    y = _layernorm(y, lcg_ref[...], lcb_ref[...])
    c_out = y * _sigmoid(y)

    u = _gelu(z_u)
    v = _layernorm(_gelu(z_v), lvg_ref[...], lvb_ref[...])
    v_ref[...] = v
    vb = v.astype(BF16)
    col = lax.broadcasted_iota(I32, (GMLP_CHUNK, C_GMLP), 1)
    w_heads = [wsp16_ref[hh] for hh in range(GMLP_HEADS)]
    g_parts = []
    for c in range(tm // GMLP_CHUNK):
        vc = vb[c * GMLP_CHUNK:(c + 1) * GMLP_CHUNK]
        sg = bsp_ref[...]
        for hh in range(GMLP_HEADS):
            head_cols = (col >= hh * GMLP_HD) & (col < (hh + 1) * GMLP_HD)
            vh = jnp.where(head_cols, vc, jnp.zeros_like(vc))
            sg = sg + jnp.dot(w_heads[hh], vh, preferred_element_type=F32)
        g_parts.append(u[c * GMLP_CHUNK:(c + 1) * GMLP_CHUNK] * sg)
    g_out = jnp.concatenate(g_parts, axis=0) if len(g_parts) > 1 else g_parts[0]

    qs = z_q * np.float32(XA_HD ** -0.5)
    qcol = lax.broadcasted_iota(I32, (sl, C_XA), 1)
    a_parts = []
    for s in range(ns):
        q_s = qs[s * sl:(s + 1) * sl]
        kb = mk_ref[0, s].astype(BF16)
        vvb = mv_ref[0, s].astype(BF16)
        hmasks = [(qcol >= hh * XA_HD) & (qcol < (hh + 1) * XA_HD) for hh in range(XA_HEADS)]
        stack = XA_HEADS if sl * XA_HEADS <= N_MEM else 1
        a_s = jnp.zeros((sl, C_XA), F32)
        for h0 in range(0, XA_HEADS, stack):
            heads = range(h0, h0 + stack)
            qh = jnp.concatenate([jnp.where(hmasks[hh], q_s, 0.0) for hh in heads], axis=0).astype(BF16)
            sc = lax.dot_general(qh, kb, (((1,), (1,)), ((), ())), preferred_element_type=F32)
            p = jnp.exp(sc - jnp.max(sc, axis=-1, keepdims=True))
            den = jnp.sum(p, axis=-1, keepdims=True)
            oh = jnp.dot(p.astype(BF16), vvb, preferred_element_type=F32) / den
            for n, hh in enumerate(heads):
                a_s = a_s + jnp.where(hmasks[hh], oh[n * sl:(n + 1) * sl], 0.0)
        a_parts.append(a_s)
    a_out = jnp.concatenate(a_parts, axis=0) if len(a_parts) > 1 else a_parts[0]

    mix = jnp.concatenate([c_out, g_out, a_out], axis=-1).astype(BF16)
    x1 = x + jnp.dot(mix, wout_ref[...], preferred_element_type=F32)
    x1_ref[...] = x1

    h2 = _rmsnorm(x1, gffn_ref[...])
    h2_hi = h2.astype(BF16)
    h2_lo = (h2 - h2_hi.astype(F32)).astype(BF16)
    nt_dims = (((1,), (1,)), ((), ()))
    lg = (lax.dot_general(wrh_ref[...], h2_hi, nt_dims, preferred_element_type=F32)
          + lax.dot_general(wrl_ref[...], h2_hi, nt_dims, preferred_element_type=F32)
          + lax.dot_general(wrh_ref[...], h2_lo, nt_dims, preferred_element_type=F32)
          + br_ref[...])
    eio = lax.broadcasted_iota(I32, (N_EXPERTS, tm), 0)
    work = lg
    vals, idxs = [], []
    for _ in range(TOP_K):
        m = jnp.max(work, axis=0, keepdims=True)
        idx = jnp.min(jnp.where(work == m, eio, N_EXPERTS), axis=0, keepdims=True)
        vals.append(m)
        idxs.append(idx)
        work = jnp.where(eio == idx, -jnp.inf, work)
    exps = [jnp.exp(vk - vals[0]) for vk in vals]
    den = exps[0] + exps[1] + exps[2] + exps[3]
    gate_ref[...] = jnp.concatenate([ek / den for ek in exps], axis=0)

    sel = jnp.zeros((N_EXPERTS, tm), F32)
    for idx in idxs:
        sel = sel + jnp.where(eio == idx, 1.0, 0.0)
    selb = sel.astype(BF16)
    ranks = jnp.dot(selb, before_ref[...], preferred_element_type=F32)
    er = lax.broadcasted_iota(I32, (N_EXPERTS, N_EXPERTS), 0)
    ec = lax.broadcasted_iota(I32, (N_EXPERTS, N_EXPERTS), 1)
    lower = jnp.where(ec < er, 1.0, 0.0).astype(BF16)
    cnt = jnp.sum(sel, axis=1, keepdims=True)
    half_len = jnp.floor((cnt + 1.0) * 0.5)
    off_pairs = jnp.dot(lower, jnp.broadcast_to(half_len, (N_EXPERTS, LANES)).astype(BF16),
                        preferred_element_type=F32)[:, 0:1]
    cnt_ref[0] = jnp.broadcast_to(half_len, (N_EXPERTS, LANES)).astype(I32)
    off_ref[0] = jnp.broadcast_to(off_pairs, (N_EXPERTS, LANES)).astype(I32)
    slot_of = 2.0 * off_pairs + ranks
    pos = [jnp.sum(jnp.where(eio == idx, slot_of, 0.0), axis=0, keepdims=True).astype(I32) for idx in idxs]
    pos_all = jnp.concatenate(pos, axis=0)
    pos_ref[...] = pos_all
    pos_keep[...] = pos_all
    h2_keep[...] = h2_hi


def _mixer(x, hist, mem_k, mem_v, wts, *, ns, sl, carry, sp_chunk, pairs_per_tile):
    g, r, _ = x.shape
    tm = ns * sl
    nt = r // tm
    ntot = g * nt
    assert 2 * pairs_per_tile >= TOP_K * tm + N_EXPERTS
    tile = lambda s: jnp.minimum(s, ntot - 1)
    const2 = lambda s: (0, 0)
    const3 = lambda s: (0, 0, 0)
    tile_row = lambda s: (tile(s), 0)
    tile_lane = lambda s: (0, tile(s))
    per_seq = lambda s: (tile(s) // nt, 0, 0, 0)
    in_specs = [
        pl.BlockSpec((1, tm, D_MODEL), lambda s: (tile(s) // nt, tile(s) % nt, 0)),
        pl.BlockSpec((1, ns, HIST, C_CONV), per_seq),
        pl.BlockSpec((1, ns, N_MEM, C_XA), per_seq),
        pl.BlockSpec((1, ns, N_MEM, C_XA), per_seq),
        pl.BlockSpec((1, D_MODEL), const2),
        pl.BlockSpec((D_MODEL, IN_COLS), const2),
        pl.BlockSpec((HIST, C_CONV), const2),
        pl.BlockSpec((1, C_CONV), const2),
        pl.BlockSpec((1, C_CONV), const2),
        pl.BlockSpec((1, C_CONV), const2),
        pl.BlockSpec((1, C_GMLP), const2),
        pl.BlockSpec((1, C_GMLP), const2),
        pl.BlockSpec((GMLP_HEADS, GMLP_CHUNK, GMLP_CHUNK), const3),
        pl.BlockSpec((GMLP_CHUNK, C_GMLP), const2),
        pl.BlockSpec((D_MODEL, D_MODEL), const2),
        pl.BlockSpec((1, D_MODEL), const2),
        pl.BlockSpec((N_EXPERTS, D_MODEL), const2),
        pl.BlockSpec((N_EXPERTS, D_MODEL), const2),
        pl.BlockSpec((N_EXPERTS, 1), const2),
    ]
    tile_cnt = lambda s: (tile(s), 0, 0)
    out_specs = [
        pl.BlockSpec((tm, D_MODEL), tile_row),
        pl.BlockSpec((pairs_per_tile * ROW_TILES, LANES),
                     lambda s: (jnp.maximum(s - 1, 0), 0)),
        pl.BlockSpec((TOP_K, tm), tile_lane),
        pl.BlockSpec((TOP_K, tm), tile_lane),
        pl.BlockSpec((1, N_EXPERTS, LANES), tile_cnt),
        pl.BlockSpec((1, N_EXPERTS, LANES), tile_cnt),
        pl.BlockSpec((1, ns, HIST, C_CONV), per_seq),
        pl.BlockSpec((tm, C_GMLP), tile_row),
    ]
    rows = g * r
    out_shape = [
        jax.ShapeDtypeStruct((rows, D_MODEL), F32),
        jax.ShapeDtypeStruct((ntot * pairs_per_tile * ROW_TILES, LANES), U32),
        jax.ShapeDtypeStruct((TOP_K, rows), I32),
        jax.ShapeDtypeStruct((TOP_K, rows), F32),
        jax.ShapeDtypeStruct((ntot, N_EXPERTS, LANES), I32),
        jax.ShapeDtypeStruct((ntot, N_EXPERTS, LANES), I32),
        jax.ShapeDtypeStruct((g, ns, HIST, C_CONV), F32),
        jax.ShapeDtypeStruct((rows, C_GMLP), F32),
    ]
    kern = functools.partial(_mixer_kernel, ns=ns, sl=sl, carry=carry, sp_chunk=sp_chunk,
                             tiles_per_seq=nt, n_sorted=2 * pairs_per_tile)
    return pl.pallas_call(
        kern,
        grid=(ntot + 1,),
        in_specs=in_specs,
        out_specs=out_specs,
        out_shape=out_shape,
        scratch_shapes=[pltpu.VMEM((ns, HIST + sl, C_CONV), F32),
                        pltpu.VMEM((ns, SUBLANES - 1, HIST + sl, C_CONV), F32),
                        pltpu.VMEM((tm, D_MODEL), BF16),
                        pltpu.VMEM((TOP_K, tm), I32),
                        pltpu.VMEM((tm, tm), BF16),
                        pltpu.VMEM((GMLP_HEADS, GMLP_CHUNK, GMLP_CHUNK), BF16)],
        compiler_params=pltpu.CompilerParams(
            dimension_semantics=("arbitrary",),
            vmem_limit_bytes=VMEM_LIMIT_MIXER),
        name="mixer_carry" if carry else "mixer_cache",
    )(x, hist, mem_k, mem_v, *wts)


def _strip_pieces(n, max_rows, fn):
    done = 0
    p = max_rows
    while p >= 1:
        has = (n & p) != 0
        pl.when(has)(functools.partial(fn, done, p))
        done = done + jnp.where(has, p, 0)
        p //= 2


class _Table:
    def __init__(self, ref, offset):
        self.ref, self.offset = ref, offset

    def __getitem__(self, i):
        return self.ref[self.offset + i]


def _pack_tables(tables):
    offsets = tuple(int(o) for o in np.cumsum([0] + [t.shape[0] for t in tables[:-1]]))
    return jnp.concatenate([t.astype(I32) for t in tables]), offsets


def _pow2_at_most(n):
    return 1 << (n.bit_length() - 1)


def _rows(ref, first_row, rows):
    return ref.at[pl.ds(pl.multiple_of(first_row * ROW_TILES, ROW_TILES), rows * ROW_TILES)]


def _moe_kernel(tab_ref, xs_hbm, xs2_hbm, wgu_ref, bgu_ref, wdn_ref, bdn_ref, ys_hbm,
                xbuf, ybuf, zbuf, wgu16, wdn16, sems, ysems, zsem, *, offsets, n_blocks):
    tabs = [_Table(tab_ref, o) for o in offsets]
    b0_ref, nbk_ref, bv_ref, nb_ref = tabs[:4]
    strips_a, strips_b = tabs[4:9], tabs[9:14]
    e = pl.program_id(0)
    n_used = nb_ref[0]
    half = MOE_PAIRS // 2

    def gather_start(b, slot):
        pair0 = b * MOE_PAIRS

        for src_hbm, (slo_ref, shi_ref, ssrc_ref, sdst_ref, slen_ref) in ((xs_hbm, strips_a), (xs2_hbm, strips_b)):
            def strip(s, carry, src_hbm=src_hbm, ssrc_ref=ssrc_ref, sdst_ref=sdst_ref, slen_ref=slen_ref):
                lo = jnp.maximum(sdst_ref[s], pair0)
                hi = jnp.minimum(sdst_ref[s] + slen_ref[s], pair0 + MOE_PAIRS)
                src0 = ssrc_ref[s] + (lo - sdst_ref[s])
                dst0 = lo - pair0

                def piece(first, rows):
                    pltpu.make_async_copy(_rows(src_hbm, src0 + first, rows),
                                          _rows(xbuf.at[slot], dst0 + first, rows), sems.at[slot]).start()
                _strip_pieces(hi - lo, MOE_PAIRS, piece)
                return carry
            lax.fori_loop(slo_ref[b], shi_ref[b], strip, 0)

    def gather_wait(b, slot):
        def piece(first, rows):
            del first
            pltpu.make_async_copy(_rows(xs_hbm, 0, rows), _rows(xbuf.at[slot], 0, rows), sems.at[slot]).wait()
        _strip_pieces(bv_ref[b], MOE_PAIRS, piece)

    def out_copy(b, slot):
        return pltpu.make_async_copy(ybuf.at[slot], _rows(ys_hbm, b * MOE_PAIRS, MOE_PAIRS), ysems.at[slot])

    def expert_mlp(xslot, slot, h):
        xb = pltpu.bitcast(_tiles_to_rows(xbuf.at[xslot], half, h * half), BF16)
        gu = jnp.dot(xb, wgu16[...], preferred_element_type=F32) + bgu_ref[0]
        gate = jnp.minimum(gu[:, :D_FF], SWIGLU_LIMIT)
        up = jnp.clip(gu[:, D_FF:], -SWIGLU_LIMIT, SWIGLU_LIMIT)
        act = (up + 1.0) * (gate * _sigmoid(SWIGLU_ALPHA * gate))
        yb = jnp.dot(act.astype(BF16), wdn16[...], preferred_element_type=F32) + bdn_ref[0]
        _rows_to_tiles(ybuf.at[slot], pltpu.bitcast(yb.astype(BF16), U32), half, h * half)

    def fill_copy(b):
        return pltpu.make_async_copy(zbuf, _rows(ys_hbm, b * MOE_PAIRS, MOE_PAIRS), zsem)

    @pl.when(e == 0)
    def _():
        zbuf[...] = jnp.zeros_like(zbuf)

        def start_fill(b, carry):
            fill_copy(b).start()
            return carry
        lax.fori_loop(n_used, n_blocks, start_fill, 0)

        xbuf[...] = jnp.zeros_like(xbuf)
        for a in range(GATHER_AHEAD):
            @pl.when(a < n_used)
            def _():
                gather_start(a, a)

    @pl.when(nbk_ref[e] > 0)
    def _():
        def cast_rows(c, carry):
            r = pl.multiple_of(c * LANES, LANES)
            wgu16[pl.ds(r, LANES), :] = wgu_ref[pl.ds(r, LANES), :].astype(BF16)
            wdn16[pl.ds(r, LANES), :] = wdn_ref[pl.ds(r, LANES), :].astype(BF16)
            return carry
        lax.fori_loop(0, D_MODEL // LANES, cast_rows, 0)

        def block(j, carry):
            b = b0_ref[e] + j
            slot = b % 2
            xslot = b % (GATHER_AHEAD + 1)

            @pl.when(b + GATHER_AHEAD < n_used)
            def _():
                gather_start(b + GATHER_AHEAD, (b + GATHER_AHEAD) % (GATHER_AHEAD + 1))

            gather_wait(b, xslot)

            @pl.when(b >= 2)
            def _():
                out_copy(b - 2, slot).wait()

            valid = bv_ref[b]

            @pl.when(valid > half)
            def _():
                expert_mlp(xslot, slot, 0)
                expert_mlp(xslot, slot, 1)

            @pl.when(valid <= half)
            def _():
                expert_mlp(xslot, slot, 0)
                ybuf[slot, pl.ds(half * ROW_TILES, half * ROW_TILES), :] = jnp.zeros(
                    (half * ROW_TILES, LANES), U32)

            out_copy(b, slot).start()
            return carry
        lax.fori_loop(0, nbk_ref[e], block, 0)

    @pl.when(e == pl.num_programs(0) - 1)
    def _():
        for back in (1, 2):
            @pl.when(n_used >= back)
            def _():
                out_copy(n_used - back, (n_used - back) % 2).wait()

        def wait_fill(b, carry):
            fill_copy(b).wait()
            return carry
        lax.fori_loop(n_used, n_blocks, wait_fill, 0)


def _moe(tables, xs, xs2, n_blocks, w_gu, b_gu, w_dn, b_dn):
    wsel = lambda e, *_: (e, 0, 0)
    packed, offsets = _pack_tables(tables)
    slots = n_blocks * MOE_PAIRS
    grid_spec = pltpu.PrefetchScalarGridSpec(
        num_scalar_prefetch=1,
        grid=(N_EXPERTS,),
        in_specs=[pl.BlockSpec(memory_space=pl.ANY),
                  pl.BlockSpec(memory_space=pl.ANY),
                  pl.BlockSpec((None, D_MODEL, 2 * D_FF), wsel),
                  pl.BlockSpec((None, 1, 2 * D_FF), wsel),
                  pl.BlockSpec((None, D_FF, D_MODEL), wsel),
                  pl.BlockSpec((None, 1, D_MODEL), wsel)],
        out_specs=pl.BlockSpec(memory_space=pl.ANY),
        scratch_shapes=[pltpu.VMEM((GATHER_AHEAD + 1, MOE_PAIRS * ROW_TILES, LANES), U32),
                        pltpu.VMEM((2, MOE_PAIRS * ROW_TILES, LANES), U32),
                        pltpu.VMEM((MOE_PAIRS * ROW_TILES, LANES), U32),
                        pltpu.VMEM((D_MODEL, 2 * D_FF), BF16),
                        pltpu.VMEM((D_FF, D_MODEL), BF16),
                        pltpu.SemaphoreType.DMA((GATHER_AHEAD + 1,)),
                        pltpu.SemaphoreType.DMA((2,)),
                        pltpu.SemaphoreType.DMA(())],
    )
    return pl.pallas_call(
        functools.partial(_moe_kernel, offsets=offsets, n_blocks=n_blocks),
        grid_spec=grid_spec,
        out_shape=jax.ShapeDtypeStruct((slots * ROW_TILES, LANES), U32),
        compiler_params=pltpu.CompilerParams(
            dimension_semantics=("arbitrary",),
            vmem_limit_bytes=VMEM_LIMIT_MOE),
        name="moe",
    )(packed, xs, xs2, w_gu, b_gu, w_dn, b_dn)


def _combine_kernel(tab_ref, x1_ref, pos_ref, gate_ref, gfin_ref, ys_hbm,
                    out_ref, ybuf, sems, *, tm, pairs_per_tile, offsets):
    csrc_ref, clen_ref, coff_ref, ctot_ref = [_Table(tab_ref, o) for o in offsets]
    i = pl.program_id(0)
    n_steps = pl.num_programs(0)
    n_sorted = 2 * pairs_per_tile
    n_slots = 2 * TILES_PER_STEP

    def gather_start(t, slot):
        def strip(e, carry):
            s = t * N_EXPERTS + e
            src0 = csrc_ref[s]
            dst0 = coff_ref[s]

            def piece(first, rows):
                pltpu.make_async_copy(_rows(ys_hbm, src0 + first, rows),
                                      _rows(ybuf.at[slot], dst0 + first, rows), sems.at[slot]).start()
            _strip_pieces(clen_ref[s], tm // 2, piece)
            return carry
        lax.fori_loop(0, N_EXPERTS, strip, 0)

    @pl.when(i == 0)
    def _():
        ybuf[...] = jnp.zeros_like(ybuf)
        for h in range(TILES_PER_STEP):
            gather_start(h, h)

    @pl.when(i + 1 < n_steps)
    def _():
        for h in range(TILES_PER_STEP):
            t = (i + 1) * TILES_PER_STEP + h
            gather_start(t, t % n_slots)

    slots = [(i * TILES_PER_STEP + h) % n_slots for h in range(TILES_PER_STEP)]
    for h in range(TILES_PER_STEP):
        def wait_piece(first, rows, slot=slots[h]):
            del first
            pltpu.make_async_copy(_rows(ys_hbm, 0, rows), _rows(ybuf.at[slot], 0, rows), sems.at[slot]).wait()
        _strip_pieces(ctot_ref[i * TILES_PER_STEP + h], _pow2_at_most(pairs_per_tile), wait_piece)

    jt = lax.broadcasted_iota(I32, (n_sorted, tm), 0).astype(jnp.int16)
    for h in range(TILES_PER_STEP):
        tok = slice(h * tm, (h + 1) * tm)
        y_sorted = pltpu.bitcast(_tiles_to_rows(ybuf.at[slots[h]], pairs_per_tile), BF16)
        unsort_t = jnp.zeros((n_sorted, tm), BF16)
        for k in range(TOP_K):
            unsort_t = jnp.where(jt == pos_ref[k:k + 1, tok].astype(jnp.int16),
                                 gate_ref[k:k + 1, tok].astype(BF16), unsort_t)
        acc = x1_ref[tok, :] + lax.dot_general(unsort_t, y_sorted, (((0,), (0,)), ((), ())),
                                               preferred_element_type=F32)
        out_ref[tok, :] = _rmsnorm(acc, gfin_ref[...])


def _combine(tables, x1, pos, gates, g_final, ys, tm, pairs_per_tile):
    t = x1.shape[0]
    nt = t // tm
    assert nt % TILES_PER_STEP == 0
    packed, offsets = _pack_tables(tables)
    grid_spec = pltpu.PrefetchScalarGridSpec(
        num_scalar_prefetch=1,
        grid=(nt // TILES_PER_STEP,),
        in_specs=[pl.BlockSpec((TILES_PER_STEP * tm, D_MODEL), lambda i, *_: (i, 0)),
                  pl.BlockSpec((TOP_K, TILES_PER_STEP * tm), lambda i, *_: (0, i)),
                  pl.BlockSpec((TOP_K, TILES_PER_STEP * tm), lambda i, *_: (0, i)),
                  pl.BlockSpec((1, D_MODEL), lambda i, *_: (0, 0)),
                  pl.BlockSpec(memory_space=pl.ANY)],
        out_specs=pl.BlockSpec((TILES_PER_STEP * tm, D_MODEL), lambda i, *_: (i, 0)),
        scratch_shapes=[pltpu.VMEM((2 * TILES_PER_STEP, pairs_per_tile * ROW_TILES, LANES), U32),
                        pltpu.SemaphoreType.DMA((2 * TILES_PER_STEP,))],
    )
    return pl.pallas_call(
        functools.partial(_combine_kernel, tm=tm, pairs_per_tile=pairs_per_tile, offsets=offsets),
        grid_spec=grid_spec,
        out_shape=jax.ShapeDtypeStruct((t, D_MODEL), F32),
        compiler_params=pltpu.CompilerParams(
            dimension_semantics=("arbitrary",),
            vmem_limit_bytes=VMEM_LIMIT_MIXER),
        name="combine",
    )(packed, x1, pos, gates, g_final, ys)


def _split_bf16(w):
    hi = w.astype(BF16)
    lo = (w - hi.astype(F32)).astype(BF16)
    return hi, lo


def kernel(x_prompt, x_sample, cache_conv, cache_mem_k, cache_mem_v, mem_prompt, g_mix, w_in, w_dw, b_dw, ln_conv_g, ln_conv_b, ln_v_g, ln_v_b, w_spatial, b_spatial, g_mem, w_mem_k, w_mem_v, w_out, g_ffn, w_router, b_router, w_gate_up, b_gate_up, w_down, b_down, g_final):
    depth = g_mix.shape[0]
    assert depth == 1
    l = 0
    bp, seq, _ = x_prompt.shape
    bs, dseq, _ = x_sample.shape
    assert seq % PROMPT_TILE == 0 and bs % SAMPLE_SEQS_PER_TILE == 0
    assert GMLP_CHUNK % dseq == 0 and (SAMPLE_SEQS_PER_TILE * dseq) % GMLP_CHUNK == 0

    row = lambda a: a.reshape(1, -1)
    wr_hi, wr_lo = _split_bf16(w_router[l].T)
    w_dw_pad = jnp.pad(w_dw[l], ((0, HIST - CONV_W), (0, 0)))
    bias_rows = lambda b: jnp.repeat(b.T, GMLP_HD, axis=1)
    common = dict(
        gmix=row(g_mix[l]), win=w_in[l].astype(BF16), wdw=w_dw_pad, bdw=row(b_dw[l]),
        lcg=row(ln_conv_g[l]), lcb=row(ln_conv_b[l]), lvg=row(ln_v_g[l]), lvb=row(ln_v_b[l]),
        wout=w_out[l].astype(BF16), gffn=row(g_ffn[l]), wrh=wr_hi, wrl=wr_lo,
        br=b_router[l].reshape(N_EXPERTS, 1))

    def weights(wsp, bsp):
        c = common
        return (c["gmix"], c["win"], c["wdw"], c["bdw"], c["lcg"], c["lcb"], c["lvg"], c["lvb"],
                wsp, bsp, c["wout"], c["gffn"], c["wrh"], c["wrl"], c["br"])

    reps = GMLP_CHUNK // dseq
    wts_p = weights(w_spatial[l], bias_rows(b_spatial[l]))
    wts_s = weights(jnp.tile(w_spatial[l][:, :dseq, :dseq], (1, reps, reps)),
                    bias_rows(jnp.tile(b_spatial[l][:, :dseq], (1, reps))))

    w_kv = jnp.concatenate([w_mem_k[l], w_mem_v[l]], axis=1).astype(BF16)
    kv_p = _memkv(mem_prompt, row(g_mem[l]), w_kv)
    mk_p = kv_p[:, :, :C_XA]
    mv_p = kv_p[:, :, C_XA:]
    zero_hist = jnp.zeros((bp, 1, HIST, C_CONV), F32)
    tp, ts = bp * seq, bs * dseq
    tm_s = SAMPLE_SEQS_PER_TILE * dseq
    ntp, nts = tp // PROMPT_TILE, ts // tm_s
    ppt_p = (TOP_K * PROMPT_TILE + N_EXPERTS + 1) // 2
    ppt_s = (TOP_K * tm_s + N_EXPERTS + 1) // 2
    (x1_p, xs_p, pos_p, gate_p, cnt_p, off_p, hist_p, _) = _mixer(
        x_prompt, zero_hist, mk_p[:, None], mv_p[:, None], wts_p,
        ns=1, sl=PROMPT_TILE, carry=True, sp_chunk=GMLP_CHUNK, pairs_per_tile=ppt_p)

    gs = bs // SAMPLE_SEQS_PER_TILE
    hist_s_in = jnp.pad(cache_conv[l], ((0, 0), (HIST_OFF, 0), (0, 0))).reshape(
        gs, SAMPLE_SEQS_PER_TILE, HIST, C_CONV)
    mk_s = cache_mem_k[l].reshape(gs, SAMPLE_SEQS_PER_TILE, N_MEM, C_XA)
    mv_s = cache_mem_v[l].reshape(gs, SAMPLE_SEQS_PER_TILE, N_MEM, C_XA)
    (x1_s, xs_s, pos_s, gate_s, cnt_s, off_s, hist_s, v_s) = _mixer(
        x_sample.reshape(gs, tm_s, D_MODEL), hist_s_in, mk_s, mv_s, wts_s,
        ns=SAMPLE_SEQS_PER_TILE, sl=dseq, carry=False, sp_chunk=dseq, pairs_per_tile=ppt_s)

    max_pairs = ((tp + ts) * TOP_K + (ntp + nts) * N_EXPERTS) // 2
    n_blocks = -(-max_pairs // MOE_PAIRS) + N_EXPERTS
    cnt = jnp.concatenate([cnt_p[:, :, 0], cnt_s[:, :, 0]], axis=0)
    off = jnp.concatenate([off_p[:, :, 0], off_s[:, :, 0]], axis=0)
    tile_pair0 = np.concatenate([np.arange(ntp) * ppt_p, np.arange(nts) * ppt_s]).astype(np.int32)
    counts = jnp.sum(cnt, axis=0)
    tile_base = jnp.cumsum(cnt, axis=0) - cnt
    padded = (counts + MOE_PAIRS - 1) // MOE_PAIRS * MOE_PAIRS
    pad_end = jnp.cumsum(padded)
    pad_start = pad_end - padded
    strip_dst = (pad_start[None, :] + tile_base).astype(I32)
    strip_src = (tile_pair0[:, None] + off).astype(I32)
    n_used = (pad_end[-1] // MOE_PAIRS).astype(I32)
    blk_pair0 = jnp.minimum(jnp.arange(n_blocks, dtype=I32), n_used - 1) * MOE_PAIRS
    block_expert = jnp.minimum(
        jnp.sum((pad_end[None, :] <= blk_pair0[:, None]).astype(I32), axis=1), N_EXPERTS - 1)
    of_block = block_expert[:, None] == jnp.arange(N_EXPERTS, dtype=I32)[None, :]
    last_pair = jnp.sum(jnp.where(of_block, (pad_start + counts)[None, :], 0), axis=1)
    block_valid = jnp.clip(last_pair - blk_pair0, 0, MOE_PAIRS).astype(I32)
    def strip_group(t0, t1):
        sdst = strip_dst[t0:t1].T.reshape(-1)
        ssrc = strip_src[t0:t1].T.reshape(-1)
        slen = cnt[t0:t1].T.reshape(-1).astype(I32)
        s_lo = jnp.sum(((sdst + slen)[None, :] <= blk_pair0[:, None]).astype(I32), axis=1)
        s_hi = jnp.sum((sdst[None, :] < (blk_pair0 + MOE_PAIRS)[:, None]).astype(I32), axis=1)
        return (s_lo, s_hi, ssrc, sdst, slen)

    moe_tables = ((pad_start // MOE_PAIRS).astype(I32), (padded // MOE_PAIRS).astype(I32),
                  block_valid, n_used.reshape(1),
                  *strip_group(0, ntp), *strip_group(ntp, ntp + nts))

    ys = _moe(moe_tables, xs_p, xs_s, n_blocks,
              w_gate_up[l], b_gate_up[l][:, None, :], w_down[l], b_down[l][:, None, :])
    gfin = row(g_final)

    def combine_tables(t0, t1):
        return (strip_dst[t0:t1].reshape(-1), cnt[t0:t1].reshape(-1).astype(I32),
                off[t0:t1].reshape(-1).astype(I32), jnp.sum(cnt[t0:t1], axis=1).astype(I32))

    y_p = _combine(combine_tables(0, ntp), x1_p, pos_p, gate_p, gfin, ys, PROMPT_TILE, ppt_p)
    y_s = _combine(combine_tables(ntp, ntp + nts), x1_s, pos_s, gate_s, gfin, ys, tm_s, ppt_s)

    return (y_p.reshape(bp, seq, D_MODEL),
            y_s.reshape(bs, dseq, D_MODEL),
            hist_p[:, 0, HIST_OFF:, :][None],
            mk_p.reshape(bp, N_MEM, XA_HEADS, XA_HD)[None],
            mv_p.reshape(bp, N_MEM, XA_HEADS, XA_HD)[None],
            hist_s.reshape(bs, HIST, C_CONV)[:, HIST_OFF:, :][None],
            v_s.reshape(bs, dseq, C_GMLP)[None])
```

```python
import functools

import numpy as np
import jax
import jax.numpy as jnp
from jax import lax
from jax.experimental import pallas as pl
from jax.experimental.pallas import tpu as pltpu

F32 = jnp.float32
BF16 = jnp.bfloat16
I32 = jnp.int32
U32 = jnp.uint32

D_MODEL = 1024
C_CONV = 384
CONV_W = 31
C_GMLP = 384
GMLP_HEADS = 4
GMLP_HD = 96
GMLP_CHUNK = 128
XA_HEADS = 4
XA_HD = 64
C_XA = 256
N_MEM = 256
N_EXPERTS = 32
TOP_K = 4
D_FF = 1024
SWIGLU_LIMIT = 7.0
SWIGLU_ALPHA = 1.702
EPS = 1e-5
IN_COLS = 2 * C_CONV + 2 * C_GMLP + C_XA

SUBLANES = 8
LANES = 128
ROW_TILES = D_MODEL // LANES
HIST = 32
HIST_OFF = HIST - (CONV_W - 1)

PROMPT_TILE = 512
SAMPLE_SEQS_PER_TILE = 16
MOE_BLOCK = 512
MOE_PAIRS = MOE_BLOCK // 2
GATHER_AHEAD = 3
TILES_PER_STEP = 2
VMEM_LIMIT_MIXER = 48 * 1024 * 1024
VMEM_LIMIT_MOE = 52 * 1024 * 1024


def _rmsnorm(x, g):
    return x * lax.rsqrt(jnp.mean(x * x, axis=-1, keepdims=True) + EPS) * g


def _layernorm(x, g, b):
    mu = jnp.mean(x, axis=-1, keepdims=True)
    xc = x - mu
    var = jnp.mean(xc * xc, axis=-1, keepdims=True)
    return xc * lax.rsqrt(var + EPS) * g + b


def _gelu(x):
    return 0.5 * x * (1.0 + lax.erf(x * np.float32(1.0 / np.sqrt(2.0))))


def _sigmoid(x):
    return 1.0 / (1.0 + jnp.exp(-x))


def _rows_to_tiles(dst_ref, val, rows, row0=0):
    for j in range(ROW_TILES):
        dst_ref[pl.ds(row0 * ROW_TILES + j, rows, stride=ROW_TILES), :] = val[:, j * LANES:(j + 1) * LANES]


def _tiles_to_rows(src_ref, rows, row0=0):
    return jnp.concatenate(
        [src_ref[pl.ds(row0 * ROW_TILES + j, rows, stride=ROW_TILES), :] for j in range(ROW_TILES)],
        axis=-1)


def _memkv_kernel(mem_ref, g_ref, w_ref, o_ref):
    mn = _rmsnorm(mem_ref[0], g_ref[...])
    o_ref[0] = jnp.dot(mn.astype(BF16), w_ref[...], preferred_element_type=F32)


def _memkv(mem, g_mem, w_kv):
    b = mem.shape[0]
    return pl.pallas_call(
        _memkv_kernel,
        grid=(b,),
        in_specs=[pl.BlockSpec((1, N_MEM, D_MODEL), lambda i: (i, 0, 0)),
                  pl.BlockSpec((1, D_MODEL), lambda i: (0, 0)),
                  pl.BlockSpec((D_MODEL, 2 * C_XA), lambda i: (0, 0))],
        out_specs=pl.BlockSpec((1, N_MEM, 2 * C_XA), lambda i: (i, 0, 0)),
        out_shape=jax.ShapeDtypeStruct((b, N_MEM, 2 * C_XA), F32),
        name="memkv",
    )(mem, g_mem, w_kv)


def _mixer_kernel(x_ref, hist_ref, mk_ref, mv_ref, gmix_ref, win_ref, wdw_ref, bdw_ref,
                  lcg_ref, lcb_ref, lvg_ref, lvb_ref, wsp_ref, bsp_ref, wout_ref, gffn_ref,
                  wrh_ref, wrl_ref, br_ref, *rest, ns, sl, carry, sp_chunk, tiles_per_seq, n_sorted):
    (x1_ref, xst_ref, pos_ref, gate_ref, cnt_ref, off_ref, histout_ref, v_ref,
     ext_ref, shift_ref, h2_keep, pos_keep, before_ref, wsp16_ref) = rest
    tm = ns * sl
    step = pl.program_id(0)
    n_tiles = pl.num_programs(0) - 1

    def sort_previous_tile():
        jj = lax.broadcasted_iota(I32, (n_sorted, tm), 0).astype(jnp.int16)
        perm = jnp.zeros((n_sorted, tm), BF16)
        for k in range(TOP_K):
            perm = jnp.where(jj == pos_keep[k:k + 1, :].astype(jnp.int16), jnp.ones((), BF16), perm)
        sorted_rows = jnp.dot(perm, h2_keep[...], preferred_element_type=F32)
        pairs = pltpu.bitcast(sorted_rows.astype(BF16), U32)
        _rows_to_tiles(xst_ref, pairs, n_sorted // 2)

    @pl.when(step == 0)
    def _():
        h2_keep[...] = jnp.zeros_like(h2_keep)
        pos_keep[...] = jnp.full(pos_keep.shape, -1, I32)
        tr = lax.broadcasted_iota(I32, (tm, tm), 0)
        tc = lax.broadcasted_iota(I32, (tm, tm), 1)
        before_ref[...] = jnp.where(tr < tc, 1.0, 0.0).astype(BF16)
        rr = lax.broadcasted_iota(I32, (GMLP_CHUNK, GMLP_CHUNK), 0)
        cc = lax.broadcasted_iota(I32, (GMLP_CHUNK, GMLP_CHUNK), 1)
        sp_mask = (cc <= rr) & ((rr // sp_chunk) == (cc // sp_chunk))
        for hh in range(GMLP_HEADS):
            wsp16_ref[hh] = jnp.where(sp_mask, wsp_ref[hh], 0.0).astype(BF16)

    @pl.when(step < n_tiles)
    def _():
        sort_previous_tile()
        _mixer_tile(x_ref, hist_ref, mk_ref, mv_ref, gmix_ref, win_ref, wdw_ref, bdw_ref,
                    lcg_ref, lcb_ref, lvg_ref, lvb_ref, wsp16_ref, bsp_ref, wout_ref, gffn_ref,
                    wrh_ref, wrl_ref, br_ref, x1_ref, pos_ref, gate_ref, cnt_ref, off_ref,
                    histout_ref, v_ref, ext_ref, shift_ref, h2_keep, pos_keep, before_ref,
                    first_of_seq=(step % tiles_per_seq) == 0,
                    ns=ns, sl=sl, carry=carry)

    @pl.when(step == n_tiles)
    def _():
        sort_previous_tile()


def _mixer_tile(x_ref, hist_ref, mk_ref, mv_ref, gmix_ref, win_ref, wdw_ref, bdw_ref,
                lcg_ref, lcb_ref, lvg_ref, lvb_ref, wsp16_ref, bsp_ref, wout_ref, gffn_ref,
                wrh_ref, wrl_ref, br_ref, x1_ref, pos_ref, gate_ref, cnt_ref, off_ref,
                histout_ref, v_ref, ext_ref, shift_ref, h2_keep, pos_keep, before_ref,
                *, first_of_seq, ns, sl, carry):
    tm = ns * sl
    x = x_ref[0]
    h = _rmsnorm(x, gmix_ref[...])
    z = jnp.dot(h.astype(BF16), win_ref[...], preferred_element_type=F32)
    z_a = z[:, 0:C_CONV]
    z_g = z[:, C_CONV:2 * C_CONV]
    z_u = z[:, 2 * C_CONV:2 * C_CONV + C_GMLP]
    z_v = z[:, 2 * C_CONV + C_GMLP:2 * C_CONV + 2 * C_GMLP]
    z_q = z[:, 2 * C_CONV + 2 * C_GMLP:IN_COLS]

    glu = z_a * _sigmoid(z_g)
    if carry:
        ext_ref[:, 0:HIST, :] = jnp.where(first_of_seq, hist_ref[0], ext_ref[:, 0:HIST, :])
    else:
        ext_ref[:, 0:HIST, :] = hist_ref[0]
    conv_parts = []
    for s in range(ns):
        ext_s = ext_ref.at[s]
        ext_s[HIST:HIST + sl, :] = glu[s * sl:(s + 1) * sl]
        n_shift = HIST + sl - SUBLANES
        for r in range(1, SUBLANES):
            shift_ref[s, r - 1, 0:n_shift, :] = ext_s[pl.ds(r, n_shift), :]
        rc = min(sl, 64)
        for r0 in range(0, sl, rc):
            acc = jnp.broadcast_to(bdw_ref[...], (rc, C_CONV))
            for j in range(CONV_W):
                a, r = divmod(j + HIST_OFF, SUBLANES)
                src = ext_s if r == 0 else shift_ref.at[s, r - 1]
                acc = acc + wdw_ref[j:j + 1, :] * src[pl.ds(r0 + a * SUBLANES, rc), :]
            conv_parts.append(acc)
        new_hist = ext_s[sl:sl + HIST, :]
        histout_ref[0, s] = new_hist
        if carry:
            ext_s[0:HIST, :] = new_hist
    y = jnp.concatenate(conv_parts, axis=0) if len(conv_parts) > 1 else conv_parts[0]
    y = _layernorm(y, lcg_ref[...], lcb_ref[...])
    c_out = y * _sigmoid(y)

    u = _gelu(z_u)
    v = _layernorm(_gelu(z_v), lvg_ref[...], lvb_ref[...])
    v_ref[...] = v
    vb = v.astype(BF16)
    col = lax.broadcasted_iota(I32, (GMLP_CHUNK, C_GMLP), 1)
    w_heads = [wsp16_ref[hh] for hh in range(GMLP_HEADS)]
    g_parts = []
    for c in range(tm // GMLP_CHUNK):
        vc = vb[c * GMLP_CHUNK:(c + 1) * GMLP_CHUNK]
        sg = bsp_ref[...]
        for hh in range(GMLP_HEADS):
            head_cols = (col >= hh * GMLP_HD) & (col < (hh + 1) * GMLP_HD)
            vh = jnp.where(head_cols, vc, jnp.zeros_like(vc))
            sg = sg + jnp.dot(w_heads[hh], vh, preferred_element_type=F32)
        g_parts.append(u[c * GMLP_CHUNK:(c + 1) * GMLP_CHUNK] * sg)
    g_out = jnp.concatenate(g_parts, axis=0) if len(g_parts) > 1 else g_parts[0]

    qs = z_q * np.float32(XA_HD ** -0.5)
    qcol = lax.broadcasted_iota(I32, (sl, C_XA), 1)
    a_parts = []
    for s in range(ns):
        q_s = qs[s * sl:(s + 1) * sl]
        kb = mk_ref[0, s].astype(BF16)
        vvb = mv_ref[0, s].astype(BF16)
        hmasks = [(qcol >= hh * XA_HD) & (qcol < (hh + 1) * XA_HD) for hh in range(XA_HEADS)]
        stack = XA_HEADS if sl * XA_HEADS <= N_MEM else 1
        a_s = jnp.zeros((sl, C_XA), F32)
        for h0 in range(0, XA_HEADS, stack):
            heads = range(h0, h0 + stack)
            qh = jnp.concatenate([jnp.where(hmasks[hh], q_s, 0.0) for hh in heads], axis=0).astype(BF16)
            sc = lax.dot_general(qh, kb, (((1,), (1,)), ((), ())), preferred_element_type=F32)
            p = jnp.exp(sc - jnp.max(sc, axis=-1, keepdims=True))
            den = jnp.sum(p, axis=-1, keepdims=True)
            oh = jnp.dot(p.astype(BF16), vvb, preferred_element_type=F32) / den
            for n, hh in enumerate(heads):
                a_s = a_s + jnp.where(hmasks[hh], oh[n * sl:(n + 1) * sl], 0.0)
        a_parts.append(a_s)
    a_out = jnp.concatenate(a_parts, axis=0) if len(a_parts) > 1 else a_parts[0]

    mix = jnp.concatenate([c_out, g_out, a_out], axis=-1).astype(BF16)
    x1 = x + jnp.dot(mix, wout_ref[...], preferred_element_type=F32)
    x1_ref[...] = x1

    h2 = _rmsnorm(x1, gffn_ref[...])
    h2_hi = h2.astype(BF16)
    h2_lo = (h2 - h2_hi.astype(F32)).astype(BF16)
    nt_dims = (((1,), (1,)), ((), ()))
    lg = (lax.dot_general(wrh_ref[...], h2_hi, nt_dims, preferred_element_type=F32)
          + lax.dot_general(wrl_ref[...], h2_hi, nt_dims, preferred_element_type=F32)
          + lax.dot_general(wrh_ref[...], h2_lo, nt_dims, preferred_element_type=F32)
          + br_ref[...])
    eio = lax.broadcasted_iota(I32, (N_EXPERTS, tm), 0)
    work = lg
    vals, idxs = [], []
    for _ in range(TOP_K):
        m = jnp.max(work, axis=0, keepdims=True)
        idx = jnp.min(jnp.where(work == m, eio, N_EXPERTS), axis=0, keepdims=True)
        vals.append(m)
        idxs.append(idx)
        work = jnp.where(eio == idx, -jnp.inf, work)
    exps = [jnp.exp(vk - vals[0]) for vk in vals]
    den = exps[0] + exps[1] + exps[2] + exps[3]
    gate_ref[...] = jnp.concatenate([ek / den for ek in exps], axis=0)

    sel = jnp.zeros((N_EXPERTS, tm), F32)
    for idx in idxs:
        sel = sel + jnp.where(eio == idx, 1.0, 0.0)
    selb = sel.astype(BF16)
    ranks = jnp.dot(selb, before_ref[...], preferred_element_type=F32)
    er = lax.broadcasted_iota(I32, (N_EXPERTS, N_EXPERTS), 0)
    ec = lax.broadcasted_iota(I32, (N_EXPERTS, N_EXPERTS), 1)
    lower = jnp.where(ec < er, 1.0, 0.0).astype(BF16)
    cnt = jnp.sum(sel, axis=1, keepdims=True)
    half_len = jnp.floor((cnt + 1.0) * 0.5)
    off_pairs = jnp.dot(lower, jnp.broadcast_to(half_len, (N_EXPERTS, LANES)).astype(BF16),
                        preferred_element_type=F32)[:, 0:1]
    cnt_ref[0] = jnp.broadcast_to(half_len, (N_EXPERTS, LANES)).astype(I32)
    off_ref[0] = jnp.broadcast_to(off_pairs, (N_EXPERTS, LANES)).astype(I32)
    slot_of = 2.0 * off_pairs + ranks
    pos = [jnp.sum(jnp.where(eio == idx, slot_of, 0.0), axis=0, keepdims=True).astype(I32) for idx in idxs]
    pos_all = jnp.concatenate(pos, axis=0)
    pos_ref[...] = pos_all
    pos_keep[...] = pos_all
    h2_keep[...] = h2_hi


def _mixer(x, hist, mem_k, mem_v, wts, *, ns, sl, carry, sp_chunk, pairs_per_tile):
    g, r, _ = x.shape
    tm = ns * sl
    nt = r // tm
    ntot = g * nt
    assert 2 * pairs_per_tile >= TOP_K * tm + N_EXPERTS
    tile = lambda s: jnp.minimum(s, ntot - 1)
    const2 = lambda s: (0, 0)
    const3 = lambda s: (0, 0, 0)
    tile_row = lambda s: (tile(s), 0)
    tile_lane = lambda s: (0, tile(s))
    per_seq = lambda s: (tile(s) // nt, 0, 0, 0)
    in_specs = [
        pl.BlockSpec((1, tm, D_MODEL), lambda s: (tile(s) // nt, tile(s) % nt, 0)),
        pl.BlockSpec((1, ns, HIST, C_CONV), per_seq),
        pl.BlockSpec((1, ns, N_MEM, C_XA), per_seq),
        pl.BlockSpec((1, ns, N_MEM, C_XA), per_seq),
        pl.BlockSpec((1, D_MODEL), const2),
        pl.BlockSpec((D_MODEL, IN_COLS), const2),
        pl.BlockSpec((HIST, C_CONV), const2),
        pl.BlockSpec((1, C_CONV), const2),
        pl.BlockSpec((1, C_CONV), const2),
        pl.BlockSpec((1, C_CONV), const2),
        pl.BlockSpec((1, C_GMLP), const2),
        pl.BlockSpec((1, C_GMLP), const2),
        pl.BlockSpec((GMLP_HEADS, GMLP_CHUNK, GMLP_CHUNK), const3),
        pl.BlockSpec((GMLP_CHUNK, C_GMLP), const2),
        pl.BlockSpec((D_MODEL, D_MODEL), const2),
        pl.BlockSpec((1, D_MODEL), const2),
        pl.BlockSpec((N_EXPERTS, D_MODEL), const2),
        pl.BlockSpec((N_EXPERTS, D_MODEL), const2),
        pl.BlockSpec((N_EXPERTS, 1), const2),
    ]
    tile_cnt = lambda s: (tile(s), 0, 0)
    out_specs = [
        pl.BlockSpec((tm, D_MODEL), tile_row),
        pl.BlockSpec((pairs_per_tile * ROW_TILES, LANES),
                     lambda s: (jnp.maximum(s - 1, 0), 0)),
        pl.BlockSpec((TOP_K, tm), tile_lane),
        pl.BlockSpec((TOP_K, tm), tile_lane),
        pl.BlockSpec((1, N_EXPERTS, LANES), tile_cnt),
        pl.BlockSpec((1, N_EXPERTS, LANES), tile_cnt),
        pl.BlockSpec((1, ns, HIST, C_CONV), per_seq),
        pl.BlockSpec((tm, C_GMLP), tile_row),
    ]
    rows = g * r
    out_shape = [
        jax.ShapeDtypeStruct((rows, D_MODEL), F32),
        jax.ShapeDtypeStruct((ntot * pairs_per_tile * ROW_TILES, LANES), U32),
        jax.ShapeDtypeStruct((TOP_K, rows), I32),
        jax.ShapeDtypeStruct((TOP_K, rows), F32),
        jax.ShapeDtypeStruct((ntot, N_EXPERTS, LANES), I32),
        jax.ShapeDtypeStruct((ntot, N_EXPERTS, LANES), I32),
        jax.ShapeDtypeStruct((g, ns, HIST, C_CONV), F32),
        jax.ShapeDtypeStruct((rows, C_GMLP), F32),
    ]
    kern = functools.partial(_mixer_kernel, ns=ns, sl=sl, carry=carry, sp_chunk=sp_chunk,
                             tiles_per_seq=nt, n_sorted=2 * pairs_per_tile)
    return pl.pallas_call(
        kern,
        grid=(ntot + 1,),
        in_specs=in_specs,
        out_specs=out_specs,
        out_shape=out_shape,
        scratch_shapes=[pltpu.VMEM((ns, HIST + sl, C_CONV), F32),
                        pltpu.VMEM((ns, SUBLANES - 1, HIST + sl, C_CONV), F32),
                        pltpu.VMEM((tm, D_MODEL), BF16),
                        pltpu.VMEM((TOP_K, tm), I32),
                        pltpu.VMEM((tm, tm), BF16),
                        pltpu.VMEM((GMLP_HEADS, GMLP_CHUNK, GMLP_CHUNK), BF16)],
        compiler_params=pltpu.CompilerParams(
            dimension_semantics=("arbitrary",),
            vmem_limit_bytes=VMEM_LIMIT_MIXER),
        name="mixer_carry" if carry else "mixer_cache",
    )(x, hist, mem_k, mem_v, *wts)


def _strip_pieces(n, max_rows, fn):
    done = 0
    p = max_rows
    while p >= 1:
        has = (n & p) != 0
        pl.when(has)(functools.partial(fn, done, p))
        done = done + jnp.where(has, p, 0)
        p //= 2


class _Table:
    def __init__(self, ref, offset):
        self.ref, self.offset = ref, offset

    def __getitem__(self, i):
        return self.ref[self.offset + i]


def _pack_tables(tables):
    offsets = tuple(int(o) for o in np.cumsum([0] + [t.shape[0] for t in tables[:-1]]))
    return jnp.concatenate([t.astype(I32) for t in tables]), offsets


def _pow2_at_most(n):
    return 1 << (n.bit_length() - 1)


def _rows(ref, first_row, rows):
    return ref.at[pl.ds(pl.multiple_of(first_row * ROW_TILES, ROW_TILES), rows * ROW_TILES)]


def _moe_kernel(tab_ref, xs_hbm, xs2_hbm, wgu_ref, bgu_ref, wdn_ref, bdn_ref, ys_hbm,
                xbuf, ybuf, zbuf, wgu16, wdn16, sems, ysems, zsem, *, offsets, n_blocks):
    tabs = [_Table(tab_ref, o) for o in offsets]
    b0_ref, nbk_ref, bv_ref, nb_ref = tabs[:4]
    strips_a, strips_b = tabs[4:9], tabs[9:14]
    e = pl.program_id(0)
    n_used = nb_ref[0]
    half = MOE_PAIRS // 2

    def gather_start(b, slot):
        pair0 = b * MOE_PAIRS

        for src_hbm, (slo_ref, shi_ref, ssrc_ref, sdst_ref, slen_ref) in ((xs_hbm, strips_a), (xs2_hbm, strips_b)):
            def strip(s, carry, src_hbm=src_hbm, ssrc_ref=ssrc_ref, sdst_ref=sdst_ref, slen_ref=slen_ref):
                lo = jnp.maximum(sdst_ref[s], pair0)
                hi = jnp.minimum(sdst_ref[s] + slen_ref[s], pair0 + MOE_PAIRS)
                src0 = ssrc_ref[s] + (lo - sdst_ref[s])
                dst0 = lo - pair0

                def piece(first, rows):
                    pltpu.make_async_copy(_rows(src_hbm, src0 + first, rows),
                                          _rows(xbuf.at[slot], dst0 + first, rows), sems.at[slot]).start()
                _strip_pieces(hi - lo, MOE_PAIRS, piece)
                return carry
            lax.fori_loop(slo_ref[b], shi_ref[b], strip, 0)

    def gather_wait(b, slot):
        def piece(first, rows):
            del first
            pltpu.make_async_copy(_rows(xs_hbm, 0, rows), _rows(xbuf.at[slot], 0, rows), sems.at[slot]).wait()
        _strip_pieces(bv_ref[b], MOE_PAIRS, piece)

    def out_copy(b, slot):
        return pltpu.make_async_copy(ybuf.at[slot], _rows(ys_hbm, b * MOE_PAIRS, MOE_PAIRS), ysems.at[slot])

    def expert_mlp(xslot, slot, h):
        xb = pltpu.bitcast(_tiles_to_rows(xbuf.at[xslot], half, h * half), BF16)
        gu = jnp.dot(xb, wgu16[...], preferred_element_type=F32) + bgu_ref[0]
        gate = jnp.minimum(gu[:, :D_FF], SWIGLU_LIMIT)
        up = jnp.clip(gu[:, D_FF:], -SWIGLU_LIMIT, SWIGLU_LIMIT)
        act = (up + 1.0) * (gate * _sigmoid(SWIGLU_ALPHA * gate))
        yb = jnp.dot(act.astype(BF16), wdn16[...], preferred_element_type=F32) + bdn_ref[0]
        _rows_to_tiles(ybuf.at[slot], pltpu.bitcast(yb.astype(BF16), U32), half, h * half)

    def fill_copy(b):
        return pltpu.make_async_copy(zbuf, _rows(ys_hbm, b * MOE_PAIRS, MOE_PAIRS), zsem)

    @pl.when(e == 0)
    def _():
        zbuf[...] = jnp.zeros_like(zbuf)

        def start_fill(b, carry):
            fill_copy(b).start()
            return carry
        lax.fori_loop(n_used, n_blocks, start_fill, 0)

        xbuf[...] = jnp.zeros_like(xbuf)
        for a in range(GATHER_AHEAD):
            @pl.when(a < n_used)
            def _():
                gather_start(a, a)

    @pl.when(nbk_ref[e] > 0)
    def _():
        def cast_rows(c, carry):
            r = pl.multiple_of(c * LANES, LANES)
            wgu16[pl.ds(r, LANES), :] = wgu_ref[pl.ds(r, LANES), :].astype(BF16)
            wdn16[pl.ds(r, LANES), :] = wdn_ref[pl.ds(r, LANES), :].astype(BF16)
            return carry
        lax.fori_loop(0, D_MODEL // LANES, cast_rows, 0)

        def block(j, carry):
            b = b0_ref[e] + j
            slot = b % 2
            xslot = b % (GATHER_AHEAD + 1)

            @pl.when(b + GATHER_AHEAD < n_used)
            def _():
                gather_start(b + GATHER_AHEAD, (b + GATHER_AHEAD) % (GATHER_AHEAD + 1))

            gather_wait(b, xslot)

            @pl.when(b >= 2)
            def _():
                out_copy(b - 2, slot).wait()

            valid = bv_ref[b]

            @pl.when(valid > half)
            def _():
                expert_mlp(xslot, slot, 0)
                expert_mlp(xslot, slot, 1)

            @pl.when(valid <= half)
            def _():
                expert_mlp(xslot, slot, 0)
                ybuf[slot, pl.ds(half * ROW_TILES, half * ROW_TILES), :] = jnp.zeros(
                    (half * ROW_TILES, LANES), U32)

            out_copy(b, slot).start()
            return carry
        lax.fori_loop(0, nbk_ref[e], block, 0)

    @pl.when(e == pl.num_programs(0) - 1)
    def _():
        for back in (1, 2):
            @pl.when(n_used >= back)
            def _():
                out_copy(n_used - back, (n_used - back) % 2).wait()

        def wait_fill(b, carry):
            fill_copy(b).wait()
            return carry
        lax.fori_loop(n_used, n_blocks, wait_fill, 0)


def _moe(tables, xs, xs2, n_blocks, w_gu, b_gu, w_dn, b_dn):
    wsel = lambda e, *_: (e, 0, 0)
    packed, offsets = _pack_tables(tables)
    slots = n_blocks * MOE_PAIRS
    grid_spec = pltpu.PrefetchScalarGridSpec(
        num_scalar_prefetch=1,
        grid=(N_EXPERTS,),
        in_specs=[pl.BlockSpec(memory_space=pl.ANY),
                  pl.BlockSpec(memory_space=pl.ANY),
                  pl.BlockSpec((None, D_MODEL, 2 * D_FF), wsel),
                  pl.BlockSpec((None, 1, 2 * D_FF), wsel),
                  pl.BlockSpec((None, D_FF, D_MODEL), wsel),
                  pl.BlockSpec((None, 1, D_MODEL), wsel)],
        out_specs=pl.BlockSpec(memory_space=pl.ANY),
        scratch_shapes=[pltpu.VMEM((GATHER_AHEAD + 1, MOE_PAIRS * ROW_TILES, LANES), U32),
                        pltpu.VMEM((2, MOE_PAIRS * ROW_TILES, LANES), U32),
                        pltpu.VMEM((MOE_PAIRS * ROW_TILES, LANES), U32),
                        pltpu.VMEM((D_MODEL, 2 * D_FF), BF16),
                        pltpu.VMEM((D_FF, D_MODEL), BF16),
                        pltpu.SemaphoreType.DMA((GATHER_AHEAD + 1,)),
                        pltpu.SemaphoreType.DMA((2,)),
                        pltpu.SemaphoreType.DMA(())],
    )
    return pl.pallas_call(
        functools.partial(_moe_kernel, offsets=offsets, n_blocks=n_blocks),
        grid_spec=grid_spec,
        out_shape=jax.ShapeDtypeStruct((slots * ROW_TILES, LANES), U32),
        compiler_params=pltpu.CompilerParams(
            dimension_semantics=("arbitrary",),
            vmem_limit_bytes=VMEM_LIMIT_MOE),
        name="moe",
    )(packed, xs, xs2, w_gu, b_gu, w_dn, b_dn)


def _combine_kernel(tab_ref, x1_ref, pos_ref, gate_ref, gfin_ref, ys_hbm,
                    out_ref, ybuf, sems, *, tm, pairs_per_tile, offsets):
    csrc_ref, clen_ref, coff_ref, ctot_ref = [_Table(tab_ref, o) for o in offsets]
    i = pl.program_id(0)
    n_steps = pl.num_programs(0)
    n_sorted = 2 * pairs_per_tile
    n_slots = 2 * TILES_PER_STEP

    def gather_start(t, slot):
        def strip(e, carry):
            s = t * N_EXPERTS + e
            src0 = csrc_ref[s]
            dst0 = coff_ref[s]

            def piece(first, rows):
                pltpu.make_async_copy(_rows(ys_hbm, src0 + first, rows),
                                      _rows(ybuf.at[slot], dst0 + first, rows), sems.at[slot]).start()
            _strip_pieces(clen_ref[s], tm // 2, piece)
            return carry
        lax.fori_loop(0, N_EXPERTS, strip, 0)

    @pl.when(i == 0)
    def _():
        ybuf[...] = jnp.zeros_like(ybuf)
        for h in range(TILES_PER_STEP):
            gather_start(h, h)

    @pl.when(i + 1 < n_steps)
    def _():
        for h in range(TILES_PER_STEP):
            t = (i + 1) * TILES_PER_STEP + h
            gather_start(t, t % n_slots)

    slots = [(i * TILES_PER_STEP + h) % n_slots for h in range(TILES_PER_STEP)]
    for h in range(TILES_PER_STEP):
        def wait_piece(first, rows, slot=slots[h]):
            del first
            pltpu.make_async_copy(_rows(ys_hbm, 0, rows), _rows(ybuf.at[slot], 0, rows), sems.at[slot]).wait()
        _strip_pieces(ctot_ref[i * TILES_PER_STEP + h], _pow2_at_most(pairs_per_tile), wait_piece)

    jt = lax.broadcasted_iota(I32, (n_sorted, tm), 0).astype(jnp.int16)
    for h in range(TILES_PER_STEP):
        tok = slice(h * tm, (h + 1) * tm)
        y_sorted = pltpu.bitcast(_tiles_to_rows(ybuf.at[slots[h]], pairs_per_tile), BF16)
        unsort_t = jnp.zeros((n_sorted, tm), BF16)
        for k in range(TOP_K):
            unsort_t = jnp.where(jt == pos_ref[k:k + 1, tok].astype(jnp.int16),
                                 gate_ref[k:k + 1, tok].astype(BF16), unsort_t)
        acc = x1_ref[tok, :] + lax.dot_general(unsort_t, y_sorted, (((0,), (0,)), ((), ())),
                                               preferred_element_type=F32)
        out_ref[tok, :] = _rmsnorm(acc, gfin_ref[...])


def _combine(tables, x1, pos, gates, g_final, ys, tm, pairs_per_tile):
    t = x1.shape[0]
    nt = t // tm
    assert nt % TILES_PER_STEP == 0
    packed, offsets = _pack_tables(tables)
    grid_spec = pltpu.PrefetchScalarGridSpec(
        num_scalar_prefetch=1,
        grid=(nt // TILES_PER_STEP,),
        in_specs=[pl.BlockSpec((TILES_PER_STEP * tm, D_MODEL), lambda i, *_: (i, 0)),
                  pl.BlockSpec((TOP_K, TILES_PER_STEP * tm), lambda i, *_: (0, i)),
                  pl.BlockSpec((TOP_K, TILES_PER_STEP * tm), lambda i, *_: (0, i)),
                  pl.BlockSpec((1, D_MODEL), lambda i, *_: (0, 0)),
                  pl.BlockSpec(memory_space=pl.ANY)],
        out_specs=pl.BlockSpec((TILES_PER_STEP * tm, D_MODEL), lambda i, *_: (i, 0)),
        scratch_shapes=[pltpu.VMEM((2 * TILES_PER_STEP, pairs_per_tile * ROW_TILES, LANES), U32),
                        pltpu.SemaphoreType.DMA((2 * TILES_PER_STEP,))],
    )
    return pl.pallas_call(
        functools.partial(_combine_kernel, tm=tm, pairs_per_tile=pairs_per_tile, offsets=offsets),
        grid_spec=grid_spec,
        out_shape=jax.ShapeDtypeStruct((t, D_MODEL), F32),
        compiler_params=pltpu.CompilerParams(
            dimension_semantics=("arbitrary",),
            vmem_limit_bytes=VMEM_LIMIT_MIXER),
        name="combine",
    )(packed, x1, pos, gates, g_final, ys)


def _split_bf16(w):
    hi = w.astype(BF16)
    lo = (w - hi.astype(F32)).astype(BF16)
    return hi, lo


def kernel(x_prompt, x_sample, cache_conv, cache_mem_k, cache_mem_v, mem_prompt, g_mix, w_in, w_dw, b_dw, ln_conv_g, ln_conv_b, ln_v_g, ln_v_b, w_spatial, b_spatial, g_mem, w_mem_k, w_mem_v, w_out, g_ffn, w_router, b_router, w_gate_up, b_gate_up, w_down, b_down, g_final):
    depth = g_mix.shape[0]
    assert depth == 1
    l = 0
    bp, seq, _ = x_prompt.shape
    bs, dseq, _ = x_sample.shape
    assert seq % PROMPT_TILE == 0 and bs % SAMPLE_SEQS_PER_TILE == 0
    assert GMLP_CHUNK % dseq == 0 and (SAMPLE_SEQS_PER_TILE * dseq) % GMLP_CHUNK == 0

    row = lambda a: a.reshape(1, -1)
    wr_hi, wr_lo = _split_bf16(w_router[l].T)
    w_dw_pad = jnp.pad(w_dw[l], ((0, HIST - CONV_W), (0, 0)))
    bias_rows = lambda b: jnp.repeat(b.T, GMLP_HD, axis=1)
    common = dict(
        gmix=row(g_mix[l]), win=w_in[l].astype(BF16), wdw=w_dw_pad, bdw=row(b_dw[l]),
        lcg=row(ln_conv_g[l]), lcb=row(ln_conv_b[l]), lvg=row(ln_v_g[l]), lvb=row(ln_v_b[l]),
        wout=w_out[l].astype(BF16), gffn=row(g_ffn[l]), wrh=wr_hi, wrl=wr_lo,
        br=b_router[l].reshape(N_EXPERTS, 1))

    def weights(wsp, bsp):
        c = common
        return (c["gmix"], c["win"], c["wdw"], c["bdw"], c["lcg"], c["lcb"], c["lvg"], c["lvb"],
                wsp, bsp, c["wout"], c["gffn"], c["wrh"], c["wrl"], c["br"])

    reps = GMLP_CHUNK // dseq
    wts_p = weights(w_spatial[l], bias_rows(b_spatial[l]))
    wts_s = weights(jnp.tile(w_spatial[l][:, :dseq, :dseq], (1, reps, reps)),
                    bias_rows(jnp.tile(b_spatial[l][:, :dseq], (1, reps))))

    w_kv = jnp.concatenate([w_mem_k[l], w_mem_v[l]], axis=1).astype(BF16)
    kv_p = _memkv(mem_prompt, row(g_mem[l]), w_kv)
    mk_p = kv_p[:, :, :C_XA]
    mv_p = kv_p[:, :, C_XA:]
    zero_hist = jnp.zeros((bp, 1, HIST, C_CONV), F32)
    tp, ts = bp * seq, bs * dseq
    tm_s = SAMPLE_SEQS_PER_TILE * dseq
    ntp, nts = tp // PROMPT_TILE, ts // tm_s
    ppt_p = (TOP_K * PROMPT_TILE + N_EXPERTS + 1) // 2
    ppt_s = (TOP_K * tm_s + N_EXPERTS + 1) // 2
    (x1_p, xs_p, pos_p, gate_p, cnt_p, off_p, hist_p, _) = _mixer(
        x_prompt, zero_hist, mk_p[:, None], mv_p[:, None], wts_p,
        ns=1, sl=PROMPT_TILE, carry=True, sp_chunk=GMLP_CHUNK, pairs_per_tile=ppt_p)

    gs = bs // SAMPLE_SEQS_PER_TILE
    hist_s_in = jnp.pad(cache_conv[l], ((0, 0), (HIST_OFF, 0), (0, 0))).reshape(
        gs, SAMPLE_SEQS_PER_TILE, HIST, C_CONV)
    mk_s = cache_mem_k[l].reshape(gs, SAMPLE_SEQS_PER_TILE, N_MEM, C_XA)
    mv_s = cache_mem_v[l].reshape(gs, SAMPLE_SEQS_PER_TILE, N_MEM, C_XA)
    (x1_s, xs_s, pos_s, gate_s, cnt_s, off_s, hist_s, v_s) = _mixer(
        x_sample.reshape(gs, tm_s, D_MODEL), hist_s_in, mk_s, mv_s, wts_s,
        ns=SAMPLE_SEQS_PER_TILE, sl=dseq, carry=False, sp_chunk=dseq, pairs_per_tile=ppt_s)

    max_pairs = ((tp + ts) * TOP_K + (ntp + nts) * N_EXPERTS) // 2
    n_blocks = -(-max_pairs // MOE_PAIRS) + N_EXPERTS
    cnt = jnp.concatenate([cnt_p[:, :, 0], cnt_s[:, :, 0]], axis=0)
    off = jnp.concatenate([off_p[:, :, 0], off_s[:, :, 0]], axis=0)
    tile_pair0 = np.concatenate([np.arange(ntp) * ppt_p, np.arange(nts) * ppt_s]).astype(np.int32)
    counts = jnp.sum(cnt, axis=0)
    tile_base = jnp.cumsum(cnt, axis=0) - cnt
    padded = (counts + MOE_PAIRS - 1) // MOE_PAIRS * MOE_PAIRS
    pad_end = jnp.cumsum(padded)
    pad_start = pad_end - padded
    strip_dst = (pad_start[None, :] + tile_base).astype(I32)
    strip_src = (tile_pair0[:, None] + off).astype(I32)
    n_used = (pad_end[-1] // MOE_PAIRS).astype(I32)
    blk_pair0 = jnp.minimum(jnp.arange(n_blocks, dtype=I32), n_used - 1) * MOE_PAIRS
    block_expert = jnp.minimum(
        jnp.sum((pad_end[None, :] <= blk_pair0[:, None]).astype(I32), axis=1), N_EXPERTS - 1)
    of_block = block_expert[:, None] == jnp.arange(N_EXPERTS, dtype=I32)[None, :]
    last_pair = jnp.sum(jnp.where(of_block, (pad_start + counts)[None, :], 0), axis=1)
    block_valid = jnp.clip(last_pair - blk_pair0, 0, MOE_PAIRS).astype(I32)
    def strip_group(t0, t1):
        sdst = strip_dst[t0:t1].T.reshape(-1)
        ssrc = strip_src[t0:t1].T.reshape(-1)
        slen = cnt[t0:t1].T.reshape(-1).astype(I32)
        s_lo = jnp.sum(((sdst + slen)[None, :] <= blk_pair0[:, None]).astype(I32), axis=1)
        s_hi = jnp.sum((sdst[None, :] < (blk_pair0 + MOE_PAIRS)[:, None]).astype(I32), axis=1)
        return (s_lo, s_hi, ssrc, sdst, slen)

    moe_tables = ((pad_start // MOE_PAIRS).astype(I32), (padded // MOE_PAIRS).astype(I32),
                  block_valid, n_used.reshape(1),
                  *strip_group(0, ntp), *strip_group(ntp, ntp + nts))

    ys = _moe(moe_tables, xs_p, xs_s, n_blocks,
              w_gate_up[l], b_gate_up[l][:, None, :], w_down[l], b_down[l][:, None, :])
    gfin = row(g_final)

    def combine_tables(t0, t1):
        return (strip_dst[t0:t1].reshape(-1), cnt[t0:t1].reshape(-1).astype(I32),
                off[t0:t1].reshape(-1).astype(I32), jnp.sum(cnt[t0:t1], axis=1).astype(I32))

    y_p = _combine(combine_tables(0, ntp), x1_p, pos_p, gate_p, gfin, ys, PROMPT_TILE, ppt_p)
    y_s = _combine(combine_tables(ntp, ntp + nts), x1_s, pos_s, gate_s, gfin, ys, tm_s, ppt_s)

    return (y_p.reshape(bp, seq, D_MODEL),
            y_s.reshape(bs, dseq, D_MODEL),
            hist_p[:, 0, HIST_OFF:, :][None],
            mk_p.reshape(bp, N_MEM, XA_HEADS, XA_HD)[None],
            mv_p.reshape(bp, N_MEM, XA_HEADS, XA_HD)[None],
            hist_s.reshape(bs, HIST, C_CONV)[:, HIST_OFF:, :][None],
            v_s.reshape(bs, dseq, C_GMLP)[None])
```
